```python
import jax, jax.numpy as jnp
from jax import lax
import numpy as np

D_MODEL = 1024
BATCH = 8
SEQ = 4096
DEPTH = 4

HEAD_DIM = 64
ATTN_WIDTH = D_MODEL // 2
CONV_WIDTH = D_MODEL - ATTN_WIDTH
N_ATTN_HEADS = ATTN_WIDTH // HEAD_DIM
N_CONV_GROUPS = CONV_WIDTH // HEAD_DIM
IN_PROJ_WIDTH = 3 * ATTN_WIDTH + 3 * CONV_WIDTH
CONV_K = 3
ROPE_DIM = HEAD_DIM // 4
ROPE_THETA = 500000.0
DILATED_BRANCHES = ((128, 1), (512, 4), (2048, 16))
FFN_HIDDEN = ((8 * D_MODEL // 3 + 255) // 256) * 256
RMS_EPS = 1e-6
NEG_INF = -1e30

kernel_name = "hybrid_dilated_attn_shortconv_encoder"


def rms_norm(x, g):
    xf = x.astype(jnp.float32)
    y = xf * lax.rsqrt(jnp.mean(xf * xf, axis=-1, keepdims=True) + RMS_EPS)
    return (y * g.astype(jnp.float32)).astype(x.dtype)


def rotary_tables(positions):
    inv_freq = ROPE_THETA ** (-jnp.arange(0, ROPE_DIM, 2, dtype=jnp.float32) / ROPE_DIM)
    ang = positions.astype(jnp.float32)[..., None] * inv_freq
    return jnp.cos(ang)[:, None], jnp.sin(ang)[:, None]


def apply_partial_rotary(t, cos, sin):
    tf = t.astype(jnp.float32)
    half = ROPE_DIM // 2
    t1 = tf[..., :half]
    t2 = tf[..., half:ROPE_DIM]
    out = jnp.concatenate([t1 * cos - t2 * sin, t2 * cos + t1 * sin, tf[..., ROPE_DIM:]], axis=-1)
    return out.astype(t.dtype)


def banded_attention(q, k, v, half):
    L, dh = q.shape[-2], q.shape[-1]
    lead = q.shape[:-2]
    nb = -(-L // half)
    lp = nb * half
    pad_q = [(0, 0)] * len(lead) + [(0, lp - L), (0, 0)]
    pad_kv = [(0, 0)] * len(lead) + [(half, lp - L + half), (0, 0)]
    qb = jnp.pad(q, pad_q).reshape(*lead, nb, half, dh).astype(jnp.float32)

    def windows(t):
        tb = jnp.pad(t, pad_kv).reshape(*lead, nb + 2, half, dh).astype(jnp.float32)
        return jnp.concatenate([tb[..., :-2, :, :], tb[..., 1:-1, :, :], tb[..., 2:, :, :]], axis=-2)

    kw = windows(k)
    vw = windows(v)
    s = jnp.einsum('...nqd,...nkd->...nqk', qb, kw) * (dh ** -0.5)
    qi = jnp.arange(nb)[:, None] * half + jnp.arange(half)[None, :]
    ki = jnp.arange(nb)[:, None] * half + jnp.arange(3 * half)[None, :] - half
    valid = (jnp.abs(qi[:, :, None] - ki[:, None, :]) <= half) & ((ki >= 0) & (ki < L))[:, None, :]
    s = jnp.where(valid, s, NEG_INF)
    lse = jax.nn.logsumexp(s, axis=-1)
    p = jnp.exp(s - lse[..., None])
    o = jnp.einsum('...nqk,...nkd->...nqd', p, vw)
    o = o.reshape(*lead, lp, dh)[..., :L, :]
    lse = lse.reshape(*lead, lp)[..., :L]
    return o, lse


def dilated_mixture_attention(q, k, v):
    b, h, s, dh = q.shape
    outs, lses = [], []
    for window, dil in DILATED_BRANCHES:
        L = s // dil
        half = window // (2 * dil)

        def by_residue(t):
            return t.reshape(b, h, L, dil, dh).swapaxes(2, 3)

        o, lse = banded_attention(by_residue(q), by_residue(k), by_residue(v), half)
        outs.append(o.swapaxes(2, 3).reshape(b, h, s, dh))
        lses.append(lse.swapaxes(2, 3).reshape(b, h, s))
    w = jax.nn.softmax(jnp.stack(lses, axis=0), axis=0)
    o = jnp.sum(w[..., None] * jnp.stack(outs, axis=0), axis=0)
    return o.astype(q.dtype)


def short_conv(u, w):
    c = u.shape[-1]
    return lax.conv_general_dilated(
        u, w.reshape(CONV_K, 1, c).astype(u.dtype), window_strides=(1,),
        padding=[(CONV_K // 2, CONV_K // 2)], dimension_numbers=('NWC', 'WIO', 'NWC'),
        feature_group_count=c)


def _fwd_setup_inputs(seed: int = 0) -> dict:
    key = jax.random.key(seed)
    ks = jax.random.split(key, 16)
    f32 = jnp.float32

    def gain(k, shape):
        return 1.0 + 0.02 * jax.random.normal(k, shape, f32)

    x = jax.random.normal(ks[0], (BATCH, SEQ, D_MODEL), f32)
    offsets = jax.random.randint(ks[1], (BATCH, 1), 0, 1024, dtype=jnp.int32)
    positions = (jnp.arange(SEQ, dtype=jnp.int32)[None, :] + offsets).astype(jnp.int32)
    return {
        "x": x,
        "positions": positions,
        "pre_mix_norm": gain(ks[2], (DEPTH, D_MODEL)),
        "w_in": jax.random.normal(ks[3], (DEPTH, D_MODEL, IN_PROJ_WIDTH), f32) * D_MODEL ** -0.5,
        "conv_w": jax.random.normal(ks[4], (DEPTH, CONV_K, CONV_WIDTH), f32) * CONV_K ** -0.5,
        "attn_out_norm": gain(ks[5], (DEPTH, ATTN_WIDTH)),
        "conv_out_norm": gain(ks[6], (DEPTH, CONV_WIDTH)),
        "w_out": jax.random.normal(ks[7], (DEPTH, D_MODEL, D_MODEL), f32) * D_MODEL ** -0.5,
        "post_mix_norm": gain(ks[8], (DEPTH, D_MODEL)),
        "pre_ffn_norm": gain(ks[9], (DEPTH, D_MODEL)),
        "w_gate_up": jax.random.normal(ks[10], (DEPTH, D_MODEL, 2 * FFN_HIDDEN), f32) * D_MODEL ** -0.5,
        "w_down": jax.random.normal(ks[11], (DEPTH, FFN_HIDDEN, D_MODEL), f32) * FFN_HIDDEN ** -0.5,
        "post_ffn_norm": gain(ks[12], (DEPTH, D_MODEL)),
    }


def _fwd_reference(x, positions, pre_mix_norm, w_in, conv_w, attn_out_norm, conv_out_norm,
              w_out, post_mix_norm, pre_ffn_norm, w_gate_up, w_down, post_ffn_norm):
    b, s, _ = x.shape
    cos, sin = rotary_tables(positions)
    split_points = np.cumsum([ATTN_WIDTH] * 3 + [CONV_WIDTH] * 2).tolist()

    def heads(t):
        return t.reshape(b, s, N_ATTN_HEADS, HEAD_DIM).transpose(0, 2, 1, 3)

    for l in range(DEPTH):
        h = rms_norm(x, pre_mix_norm[l])
        proj = jnp.einsum('bsd,de->bse', h, w_in[l])
        q, k, v, conv_u, gate_b, gate_c = jnp.split(proj, split_points, axis=-1)
        q = apply_partial_rotary(heads(q), cos, sin)
        k = apply_partial_rotary(heads(k), cos, sin)
        attn = dilated_mixture_attention(q, k, heads(v))
        attn = attn.transpose(0, 2, 1, 3).reshape(b, s, ATTN_WIDTH)
        conv_y = gate_b * short_conv(gate_c * conv_u, conv_w[l])
        merged = jnp.concatenate([rms_norm(attn, attn_out_norm[l]),
                                  rms_norm(conv_y, conv_out_norm[l])], axis=-1)
        mix = jnp.einsum('bse,ed->bsd', merged, w_out[l])
        x = x + rms_norm(mix, post_mix_norm[l])
        h = rms_norm(x, pre_ffn_norm[l])
        g, u = jnp.split(jnp.einsum('bsd,df->bsf', h, w_gate_up[l]), 2, axis=-1)
        f = jnp.einsum('bsf,fd->bsd', jax.nn.silu(g) * u, w_down[l])
        x = x + rms_norm(f, post_ffn_norm[l])
    return x


import jax as _jax
import jax.numpy as _jnp

TWIN_FORMAT = 'train_step'
FWD_PARAMS = ['x', 'positions', 'pre_mix_norm', 'w_in', 'conv_w', 'attn_out_norm', 'conv_out_norm', 'w_out', 'post_mix_norm', 'pre_ffn_norm', 'w_gate_up', 'w_down', 'post_ffn_norm']
TWIN_WEIGHTS = ['pre_mix_norm', 'w_in', 'conv_w', 'attn_out_norm', 'conv_out_norm', 'w_out', 'post_mix_norm', 'pre_ffn_norm', 'w_gate_up', 'w_down', 'post_ffn_norm']
TWIN_DIFF_INPUT = 'x'
TWIN_INPUTS = ['x', 'positions', 'pre_mix_norm', 'w_in', 'conv_w', 'attn_out_norm', 'conv_out_norm', 'w_out', 'post_mix_norm', 'pre_ffn_norm', 'w_gate_up', 'w_down', 'post_ffn_norm', 'loss_target', 'm_pre_mix_norm', 'm_w_in', 'm_conv_w', 'm_attn_out_norm', 'm_conv_out_norm', 'm_w_out', 'm_post_mix_norm', 'm_pre_ffn_norm', 'm_w_gate_up', 'm_w_down', 'm_post_ffn_norm', 'v_pre_mix_norm', 'v_w_in', 'v_conv_w', 'v_attn_out_norm', 'v_conv_out_norm', 'v_w_out', 'v_post_mix_norm', 'v_pre_ffn_norm', 'v_w_gate_up', 'v_w_down', 'v_post_ffn_norm']
TWIN_OUTPUTS = ['loss', 'grad_x', 'grad_pre_mix_norm', 'grad_w_in', 'grad_conv_w', 'grad_attn_out_norm', 'grad_conv_out_norm', 'grad_w_out', 'grad_post_mix_norm', 'grad_pre_ffn_norm', 'grad_w_gate_up', 'grad_w_down', 'grad_post_ffn_norm', 'delta_pre_mix_norm', 'delta_w_in', 'delta_conv_w', 'delta_attn_out_norm', 'delta_conv_out_norm', 'delta_w_out', 'delta_post_mix_norm', 'delta_pre_ffn_norm', 'delta_w_gate_up', 'delta_w_down', 'delta_post_ffn_norm', 'new_m_pre_mix_norm', 'new_m_w_in', 'new_m_conv_w', 'new_m_attn_out_norm', 'new_m_conv_out_norm', 'new_m_w_out', 'new_m_post_mix_norm', 'new_m_pre_ffn_norm', 'new_m_w_gate_up', 'new_m_w_down', 'new_m_post_ffn_norm', 'new_v_pre_mix_norm', 'new_v_w_in', 'new_v_conv_w', 'new_v_attn_out_norm', 'new_v_conv_out_norm', 'new_v_w_out', 'new_v_post_mix_norm', 'new_v_pre_ffn_norm', 'new_v_w_gate_up', 'new_v_w_down', 'new_v_post_ffn_norm']
TWIN_LEAF_KINDS = {'loss': 'loss', 'grad_x': 'grad_x', 'grad_pre_mix_norm': 'grad_w', 'grad_w_in': 'grad_w', 'grad_conv_w': 'grad_w', 'grad_attn_out_norm': 'grad_w', 'grad_conv_out_norm': 'grad_w', 'grad_w_out': 'grad_w', 'grad_post_mix_norm': 'grad_w', 'grad_pre_ffn_norm': 'grad_w', 'grad_w_gate_up': 'grad_w', 'grad_w_down': 'grad_w', 'grad_post_ffn_norm': 'grad_w', 'delta_pre_mix_norm': 'delta_w', 'delta_w_in': 'delta_w', 'delta_conv_w': 'delta_w', 'delta_attn_out_norm': 'delta_w', 'delta_conv_out_norm': 'delta_w', 'delta_w_out': 'delta_w', 'delta_post_mix_norm': 'delta_w', 'delta_pre_ffn_norm': 'delta_w', 'delta_w_gate_up': 'delta_w', 'delta_w_down': 'delta_w', 'delta_post_ffn_norm': 'delta_w', 'new_m_pre_mix_norm': 'new_m', 'new_m_w_in': 'new_m', 'new_m_conv_w': 'new_m', 'new_m_attn_out_norm': 'new_m', 'new_m_conv_out_norm': 'new_m', 'new_m_w_out': 'new_m', 'new_m_post_mix_norm': 'new_m', 'new_m_pre_ffn_norm': 'new_m', 'new_m_w_gate_up': 'new_m', 'new_m_w_down': 'new_m', 'new_m_post_ffn_norm': 'new_m', 'new_v_pre_mix_norm': 'new_v', 'new_v_w_in': 'new_v', 'new_v_conv_w': 'new_v', 'new_v_attn_out_norm': 'new_v', 'new_v_conv_out_norm': 'new_v', 'new_v_w_out': 'new_v', 'new_v_post_mix_norm': 'new_v', 'new_v_pre_ffn_norm': 'new_v', 'new_v_w_gate_up': 'new_v', 'new_v_w_down': 'new_v', 'new_v_post_ffn_norm': 'new_v'}


def _forward(args):
    return _fwd_reference(*[args[k] for k in FWD_PARAMS])


def _output_shape():
    def fwd():
        inp = _fwd_setup_inputs(0)
        return _fwd_reference(*[inp[k] for k in FWD_PARAMS])
    out = _jax.eval_shape(fwd)
    return out.shape, out.dtype

N_MICROBATCH = 1
ADAM_LR = 0.001
ADAM_B1 = 0.9
ADAM_B2 = 0.999
ADAM_EPS = 1e-08
ADAM_WD = 0.01
ADAM_STEP = 10
PER_EXAMPLE_BATCH_AXIS = {'x': 0, 'positions': 0, 'loss_target': 0}
SHARED_INPUTS = []
_WEIGHT_DTYPES = {'pre_mix_norm': _jnp.float32, 'w_in': _jnp.float32, 'conv_w': _jnp.float32, 'attn_out_norm': _jnp.float32, 'conv_out_norm': _jnp.float32, 'w_out': _jnp.float32, 'post_mix_norm': _jnp.float32, 'pre_ffn_norm': _jnp.float32, 'w_gate_up': _jnp.float32, 'w_down': _jnp.float32, 'post_ffn_norm': _jnp.float32}
MOMENT_SCALE = {'pre_mix_norm': 5.657726e+00, 'w_in': 3.262094e+00, 'conv_w': 1.482445e+00, 'attn_out_norm': 7.787353e+00, 'conv_out_norm': 1.575025e+00, 'w_out': 5.483551e+00, 'post_mix_norm': 3.167957e+01, 'pre_ffn_norm': 2.515969e+00, 'w_gate_up': 1.003646e+00, 'w_down': 1.898885e+00, 'post_ffn_norm': 3.192492e+01}


def _to_microbatches(a, axis):
    t = _jnp.moveaxis(a, axis, 0)
    t = t.reshape((N_MICROBATCH, t.shape[0] // N_MICROBATCH) + t.shape[1:])
    return _jnp.moveaxis(t, 1, axis + 1)


def setup_inputs(seed: int = 0) -> dict:
    inp = _fwd_setup_inputs(seed)
    key = _jax.random.fold_in(_jax.random.key(seed), 7919)
    shape, _ = _output_shape()
    out = dict(inp)
    out["loss_target"] = _jax.random.normal(_jax.random.fold_in(key, 0), shape, _jnp.float32)
    for i, name in enumerate(TWIN_WEIGHTS):
        w = inp[name].astype(_jnp.float32)
        if MOMENT_SCALE is None:
            s = _jnp.sqrt(_jnp.mean(_jnp.square(w)) + 1e-30)
        else:
            s = MOMENT_SCALE[name]
        km, kv = _jax.random.split(_jax.random.fold_in(key, i + 1))
        out[name] = w
        out["m_" + name] = s * _jax.random.normal(km, w.shape, _jnp.float32)
        out["v_" + name] = (s * s) * _jax.random.uniform(kv, w.shape, _jnp.float32, 0.5, 1.5)
    if N_MICROBATCH > 1:
        for name, axis in PER_EXAMPLE_BATCH_AXIS.items():
            out[name] = _to_microbatches(out[name], axis)
    return {'x': out['x'], 'positions': out['positions'], 'pre_mix_norm': out['pre_mix_norm'], 'w_in': out['w_in'], 'conv_w': out['conv_w'], 'attn_out_norm': out['attn_out_norm'], 'conv_out_norm': out['conv_out_norm'], 'w_out': out['w_out'], 'post_mix_norm': out['post_mix_norm'], 'pre_ffn_norm': out['pre_ffn_norm'], 'w_gate_up': out['w_gate_up'], 'w_down': out['w_down'], 'post_ffn_norm': out['post_ffn_norm'], 'loss_target': out['loss_target'], 'm_pre_mix_norm': out['m_pre_mix_norm'], 'm_w_in': out['m_w_in'], 'm_conv_w': out['m_conv_w'], 'm_attn_out_norm': out['m_attn_out_norm'], 'm_conv_out_norm': out['m_conv_out_norm'], 'm_w_out': out['m_w_out'], 'm_post_mix_norm': out['m_post_mix_norm'], 'm_pre_ffn_norm': out['m_pre_ffn_norm'], 'm_w_gate_up': out['m_w_gate_up'], 'm_w_down': out['m_w_down'], 'm_post_ffn_norm': out['m_post_ffn_norm'], 'v_pre_mix_norm': out['v_pre_mix_norm'], 'v_w_in': out['v_w_in'], 'v_conv_w': out['v_conv_w'], 'v_attn_out_norm': out['v_attn_out_norm'], 'v_conv_out_norm': out['v_conv_out_norm'], 'v_w_out': out['v_w_out'], 'v_post_mix_norm': out['v_post_mix_norm'], 'v_pre_ffn_norm': out['v_pre_ffn_norm'], 'v_w_gate_up': out['v_w_gate_up'], 'v_w_down': out['v_w_down'], 'v_post_ffn_norm': out['v_post_ffn_norm']}


def _loss(weights, diff, rest, loss_target):
    with _jax.named_scope("forward"):
        args = {**rest, TWIN_DIFF_INPUT: diff, **{k: w.astype(_WEIGHT_DTYPES[k]) for k, w in weights.items()}}
        y = _forward(args)
    with _jax.named_scope("loss_head"):
        err = _jnp.square(y.astype(_jnp.float32) - loss_target)
        return 0.5 * _jnp.sum(_jnp.mean(err, axis=-1)) if err.ndim else 0.5 * err


def _adamw(w, g, m, v):
    m = ADAM_B1 * m + (1.0 - ADAM_B1) * g
    v = ADAM_B2 * v + (1.0 - ADAM_B2) * _jnp.square(g)
    m_hat = m / (1.0 - ADAM_B1 ** ADAM_STEP)
    v_hat = v / (1.0 - ADAM_B2 ** ADAM_STEP)
    delta = -ADAM_LR * (m_hat / (_jnp.sqrt(v_hat) + ADAM_EPS) + ADAM_WD * w)
    return delta, m, v


def reference(x, positions, pre_mix_norm, w_in, conv_w, attn_out_norm, conv_out_norm, w_out, post_mix_norm, pre_ffn_norm, w_gate_up, w_down, post_ffn_norm, loss_target, m_pre_mix_norm, m_w_in, m_conv_w, m_attn_out_norm, m_conv_out_norm, m_w_out, m_post_mix_norm, m_pre_ffn_norm, m_w_gate_up, m_w_down, m_post_ffn_norm, v_pre_mix_norm, v_w_in, v_conv_w, v_attn_out_norm, v_conv_out_norm, v_w_out, v_post_mix_norm, v_pre_ffn_norm, v_w_gate_up, v_w_down, v_post_ffn_norm):
    given = dict(x=x, positions=positions, pre_mix_norm=pre_mix_norm, w_in=w_in, conv_w=conv_w, attn_out_norm=attn_out_norm, conv_out_norm=conv_out_norm, w_out=w_out, post_mix_norm=post_mix_norm, pre_ffn_norm=pre_ffn_norm, w_gate_up=w_gate_up, w_down=w_down, post_ffn_norm=post_ffn_norm, loss_target=loss_target, m_pre_mix_norm=m_pre_mix_norm, m_w_in=m_w_in, m_conv_w=m_conv_w, m_attn_out_norm=m_attn_out_norm, m_conv_out_norm=m_conv_out_norm, m_w_out=m_w_out, m_post_mix_norm=m_post_mix_norm, m_pre_ffn_norm=m_pre_ffn_norm, m_w_gate_up=m_w_gate_up, m_w_down=m_w_down, m_post_ffn_norm=m_post_ffn_norm, v_pre_mix_norm=v_pre_mix_norm, v_w_in=v_w_in, v_conv_w=v_conv_w, v_attn_out_norm=v_attn_out_norm, v_conv_out_norm=v_conv_out_norm, v_w_out=v_w_out, v_post_mix_norm=v_post_mix_norm, v_pre_ffn_norm=v_pre_ffn_norm, v_w_gate_up=v_w_gate_up, v_w_down=v_w_down, v_post_ffn_norm=v_post_ffn_norm)
    weights = {n: given[n] for n in TWIN_WEIGHTS}
    shared = {n: given[n] for n in SHARED_INPUTS}
    per_example = {n: given[n] for n in ['x', 'positions']}
    grad_fn = _jax.value_and_grad(_loss, argnums=(0, 1))

    def one_microbatch(ex, loss_target):
        ex = dict(ex)
        diff = ex.pop(TWIN_DIFF_INPUT)
        return grad_fn(weights, diff, {**shared, **ex}, loss_target)

    if N_MICROBATCH == 1:
        loss, (grad_w, grad_x) = one_microbatch(per_example, given["loss_target"])
    else:
        def body(carry, xs):
            loss_sum, grad_sum = carry
            l_k, (gw_k, gx_k) = one_microbatch(xs[0], xs[1])
            with _jax.named_scope("update"):
                return (loss_sum + l_k, _jax.tree.map(_jnp.add, grad_sum, gw_k)), gx_k

        init = (_jnp.zeros((), _jnp.float32), _jax.tree.map(_jnp.zeros_like, weights))
        (loss, grad_w), grad_x = _jax.lax.scan(body, init, (per_example, given["loss_target"]))
    with _jax.named_scope("update"):
        delta_w, new_m, new_v = {}, {}, {}
        for n in TWIN_WEIGHTS:
            delta_w[n], new_m[n], new_v[n] = _adamw(weights[n], grad_w[n], given["m_" + n], given["v_" + n])
    return (loss, grad_x, *[grad_w[n] for n in TWIN_WEIGHTS], *[delta_w[n] for n in TWIN_WEIGHTS],
            *[new_m[n] for n in TWIN_WEIGHTS], *[new_v[n] for n in TWIN_WEIGHTS])
```

```python
import functools

import numpy as np
import jax
import jax.numpy as jnp
from jax import lax
from jax.experimental import pallas as pl
from jax.experimental.pallas import tpu as pltpu

F32 = jnp.float32
BF16 = jnp.bfloat16
MESH = pl.DeviceIdType.MESH

D_MODEL = 1024
ATTN_W = 512
CONV_W = 512
HEAD_DIM = 64
ROPE_DIM = 16
ROPE_THETA = 500000.0
FFN = 2816
DEPTH = 4
RMS_EPS = 1e-6
NEG_INF = -1e30
N_CHIPS = 4
N_DEV = 8
LANES = 128
DILATIONS = (1, 4, 16)
BAND = 64
TQ = 128
WIN = TQ + 2 * BAND
SCALE = HEAD_DIM ** -0.5

ADAM_LR = 0.001
ADAM_B1 = 0.9
ADAM_B2 = 0.999
ADAM_EPS = 1e-08
ADAM_WD = 0.01
ADAM_STEP = 10

ANY = pl.BlockSpec(memory_space=pl.ANY)
WHOLE_VMEM = pl.BlockSpec(memory_space=pltpu.VMEM)


def _const_spec(block, index):
    return pl.BlockSpec(block, lambda *_: index)


def _gain_spec(g3, l):
    return _const_spec((None, 1, g3.shape[-1]), (l, 0, 0))


def _rms_fwd(x, g):
    r = lax.rsqrt(jnp.mean(x * x, axis=-1, keepdims=True) + RMS_EPS)
    return (x * r) * g


def _rms_bwd(x, g, dy):
    r = lax.rsqrt(jnp.mean(x * x, axis=-1, keepdims=True) + RMS_EPS)
    xh = x * r
    u = dy * g
    dx = r * (u - xh * jnp.mean(xh * u, axis=-1, keepdims=True))
    return dx, jnp.sum(dy * xh, axis=0, keepdims=True)


def _accumulate(ref, value, first):
    @pl.when(first)
    def _():
        ref[...] = value

    @pl.when(jnp.logical_not(first))
    def _():
        ref[...] += value


def _rope_coeffs(cos, sin):
    m = lax.broadcasted_iota(jnp.int32, cos.shape, 1) % HEAD_DIM
    a = jnp.where(m < ROPE_DIM, cos, 1.0)
    b = jnp.where(m < ROPE_DIM // 2, -sin, 0.0)
    c = jnp.where((m >= ROPE_DIM // 2) & (m < ROPE_DIM), sin, 0.0)
    return a, b, c


def _rope_apply(t, cos, sin):
    a, b, c = _rope_coeffs(cos, sin)
    n = t.shape[1]
    return a * t + b * pltpu.roll(t, n - ROPE_DIM // 2, 1) + c * pltpu.roll(t, ROPE_DIM // 2, 1)


def _rope_transpose(dt, cos, sin):
    a, b, c = _rope_coeffs(cos, sin)
    n = dt.shape[1]
    return a * dt + pltpu.roll(b * dt, ROPE_DIM // 2, 1) + pltpu.roll(c * dt, n - ROPE_DIM // 2, 1)


def _mm_nn(a, w, l, *, tm, name):
    t, k = a.shape
    _, s_n, k2, n = w.shape
    assert k == k2 and t % tm == 0

    def body(a_ref, w_ref, o_ref):
        o_ref[...] = jnp.dot(a_ref[...], w_ref[...], preferred_element_type=F32)

    return pl.pallas_call(
        body, grid=(t // tm, s_n),
        in_specs=[pl.BlockSpec((tm, k), lambda i, s: (i, 0)),
                  pl.BlockSpec((None, None, k, n), lambda i, s: (l, s, 0, 0))],
        out_specs=pl.BlockSpec((tm, n), lambda i, s: (i, s)),
        out_shape=jax.ShapeDtypeStruct((t, s_n * n), F32), name=name)(a, w)


def _mm_nt(a, w, l, *, tm, tko, name):
    t, sn = a.shape
    _, s_n, ko, n = w.shape
    assert sn == s_n * n and t % tm == 0 and ko % tko == 0

    def body(a_ref, w_ref, o_ref):
        acc = lax.dot_general(a_ref[...], w_ref[...], (((1,), (1,)), ((), ())), preferred_element_type=F32)
        if s_n == 1:
            o_ref[...] = acc
        else:
            _accumulate(o_ref, acc, pl.program_id(2) == 0)

    return pl.pallas_call(
        body, grid=(t // tm, ko // tko, s_n),
        in_specs=[pl.BlockSpec((tm, n), lambda i, j, s: (i, s)),
                  pl.BlockSpec((None, None, tko, n), lambda i, j, s: (l, s, j, 0))],
        out_specs=pl.BlockSpec((tm, tko), lambda i, j, s: (i, j)),
        out_shape=jax.ShapeDtypeStruct((t, ko), F32), name=name)(a, w)


def _mm_tn(a, b, acc, l, *, tka, name):
    t, ka = a.shape
    _, s_n, ka2, n = acc.shape
    assert ka == ka2 and b.shape == (t, s_n * n) and ka % tka == 0

    def body(a_ref, b_ref, acc_ref, o_ref):
        del acc_ref
        o_ref[...] = lax.dot_general(a_ref[...], b_ref[...], (((0,), (0,)), ((), ())), preferred_element_type=F32)

    return pl.pallas_call(
        body, grid=(ka // tka, s_n),
        in_specs=[pl.BlockSpec((t, tka), lambda i, s: (0, i)),
                  pl.BlockSpec((t, n), lambda i, s: (0, s)),
                  ANY],
        out_specs=pl.BlockSpec((None, None, tka, n), lambda i, s: (l, s, i, 0)),
        out_shape=jax.ShapeDtypeStruct(acc.shape, F32), input_output_aliases={2: 0}, name=name)(a, b, acc)


def _rope_tables(positions_col, inv_freq_row):
    t = positions_col.shape[0]

    def body(pos_ref, f_ref, cos_ref, sin_ref):
        ang = pos_ref[...].astype(F32) * f_ref[...]
        cos_ref[...] = jnp.cos(ang)
        sin_ref[...] = jnp.sin(ang)

    return pl.pallas_call(
        body, out_shape=[jax.ShapeDtypeStruct((t, LANES), F32)] * 2, name="rope_tables")(positions_col, inv_freq_row)


def _norm_fwd(x, g3, l, *, tm):
    t, w = x.shape

    def body(x_ref, g_ref, h_ref):
        h_ref[...] = _rms_fwd(x_ref[...], g_ref[...]).astype(BF16)

    return pl.pallas_call(
        body, grid=(t // tm,),
        in_specs=[pl.BlockSpec((tm, w), lambda i: (i, 0)), _gain_spec(g3, l)],
        out_specs=pl.BlockSpec((tm, w), lambda i: (i, 0)),
        out_shape=jax.ShapeDtypeStruct((t, w), BF16), name="norm_fwd")(x, g3)


def _resnorm_fwd(x, y, g_post3, l_post, g_next3, l_next, *, tm):
    t, w = x.shape
    with_next = g_next3 is not None
    row = pl.BlockSpec((tm, w), lambda i: (i, 0))

    def body(x_ref, y_ref, gp_ref, *rest):
        x_new = x_ref[...] + _rms_fwd(y_ref[...], gp_ref[...])
        if with_next:
            gn_ref, xo_ref, h_ref = rest
            h_ref[...] = _rms_fwd(x_new, gn_ref[...]).astype(BF16)
        else:
            (xo_ref,) = rest
        xo_ref[...] = x_new

    ins = [x, y, g_post3] + ([g_next3] if with_next else [])
    in_specs = [row, row, _gain_spec(g_post3, l_post)] + ([_gain_spec(g_next3, l_next)] if with_next else [])
    out_shape = [jax.ShapeDtypeStruct((t, w), F32)] + ([jax.ShapeDtypeStruct((t, w), BF16)] if with_next else [])
    out = pl.pallas_call(body, grid=(t // tm,), in_specs=in_specs, out_specs=[row] * len(out_shape),
                         out_shape=out_shape, name="resnorm_fwd")(*ins)
    return (out[0], out[1]) if with_next else (out[0], None)


def _conv_fwd(proj, conv_w, l):
    t = proj.shape[0]
    col0 = 3 * ATTN_W // LANES

    def body(u_ref, gb_ref, gc_ref, w_ref, y_ref):
        c = gc_ref[...] * u_ref[...]
        row = lax.broadcasted_iota(jnp.int32, c.shape, 0)
        c_prev = jnp.where(row == 0, 0.0, pltpu.roll(c, 1, 0))
        c_next = jnp.where(row == t - 1, 0.0, pltpu.roll(c, t - 1, 0))
        w = w_ref[...]
        y_ref[...] = gb_ref[...] * (w[0:1] * c_prev + w[1:2] * c + w[2:3] * c_next)

    nj = CONV_W // LANES
    cols = lambda base: pl.BlockSpec((t, LANES), lambda j: (0, base + j))
    return pl.pallas_call(
        body, grid=(nj,),
        in_specs=[cols(col0), cols(col0 + nj), cols(col0 + 2 * nj),
                  pl.BlockSpec((None, None, 3, LANES), lambda j: (l, j, 0, 0))],
        out_specs=pl.BlockSpec((t, LANES), lambda j: (0, j)),
        out_shape=jax.ShapeDtypeStruct((t, CONV_W), F32), name="conv_fwd")(proj, proj, proj, conv_w)


def _merge_fwd(attn, conv_y, ga3, gc3, l, *, tm):
    t = attn.shape[0]
    row = pl.BlockSpec((tm, ATTN_W), lambda i: (i, 0))

    def body(a_ref, c_ref, ga_ref, gc_ref, m_ref):
        m_ref[:, :ATTN_W] = _rms_fwd(a_ref[...], ga_ref[...]).astype(BF16)
        m_ref[:, ATTN_W:] = _rms_fwd(c_ref[...], gc_ref[...]).astype(BF16)

    return pl.pallas_call(
        body, grid=(t // tm,),
        in_specs=[row, row, _gain_spec(ga3, l), _gain_spec(gc3, l)],
        out_specs=pl.BlockSpec((tm, D_MODEL), lambda i: (i, 0)),
        out_shape=jax.ShapeDtypeStruct((t, D_MODEL), BF16), name="merge_fwd")(attn, conv_y, ga3, gc3)


def _swiglu_fwd(gu, *, tm):
    t = gu.shape[0]

    def body(g_ref, u_ref, a_ref):
        g = g_ref[...]
        a_ref[...] = (g * jax.nn.sigmoid(g) * u_ref[...]).astype(BF16)

    return pl.pallas_call(
        body, grid=(t // tm,),
        in_specs=[pl.BlockSpec((tm, FFN), lambda i: (i, 0)), pl.BlockSpec((tm, FFN), lambda i: (i, 1))],
        out_specs=pl.BlockSpec((tm, FFN), lambda i: (i, 0)),
        out_shape=jax.ShapeDtypeStruct((t, FFN), BF16), name="swiglu_fwd")(gu, gu)


def _loss_fwd_bwd(y, target, *, tm):
    t, w = y.shape
    row = pl.BlockSpec((tm, w), lambda i: (i, 0))

    def body(y_ref, t_ref, dy_ref, loss_ref):
        e = y_ref[...] - t_ref[...]
        dy_ref[...] = e * (1.0 / w)
        sq = jnp.sum(e * e, axis=0, keepdims=True) * (0.5 / w)
        part = sq[:, :LANES]
        for j in range(1, w // LANES):
            part = part + sq[:, j * LANES:(j + 1) * LANES]
        _accumulate(loss_ref, part, pl.program_id(0) == 0)

    return pl.pallas_call(
        body, grid=(t // tm,), in_specs=[row, row],
        out_specs=[row, _const_spec((1, LANES), (0, 0))],
        out_shape=[jax.ShapeDtypeStruct((t, w), F32), jax.ShapeDtypeStruct((1, LANES), F32)], name="loss")(y, target)


def _tile_rows(t, nt, lb, d):
    r = t // nt
    q0 = (t % nt) * TQ
    m0 = jnp.clip(q0 - BAND, 0, lb - WIN)
    if d == 1:
        return pl.ds(pl.multiple_of(q0, TQ), TQ), pl.ds(pl.multiple_of(m0, BAND), WIN), m0 - q0
    return pl.ds(r + d * q0, TQ, stride=d), pl.ds(r + d * m0, WIN, stride=d), m0 - q0


def _for_row_chunks(t, fn, chunk=512):
    def step(i, carry):
        fn(pl.ds(pl.multiple_of(i * chunk, chunk), chunk))
        return carry

    lax.fori_loop(0, t // chunk, step, 0)


def _rope_into(dst_ref, src_ref, cos_ref, sin_ref, t):
    def chunk(rows):
        dst_ref[rows, :] = _rope_apply(src_ref[rows, :], cos_ref[rows, :], sin_ref[rows, :])

    _for_row_chunks(t, chunk)


def _attn_fwd(proj, cos, sin):
    t = proj.shape[0]
    npair = ATTN_W // LANES

    def body(q_ref, k_ref, v_ref, cos_ref, sin_ref, o_ref, m_ref, qs, ks, l_acc):
        _rope_into(qs, q_ref, cos_ref, sin_ref, t)
        _rope_into(ks, k_ref, cos_ref, sin_ref, t)
        m_ref[...] = jnp.full(m_ref.shape, NEG_INF, F32)
        l_acc[...] = jnp.zeros(l_acc.shape, F32)
        o_ref[...] = jnp.zeros(o_ref.shape, F32)
        first_head = lax.broadcasted_iota(jnp.int32, (TQ, LANES), 1) < HEAD_DIM
        dcol = lax.broadcasted_iota(jnp.int32, (TQ, WIN), 1) - lax.broadcasted_iota(jnp.int32, (TQ, WIN), 0)
        for d in DILATIONS:
            lb = t // d
            nt = lb // TQ

            def tile(ti, carry, d=d, lb=lb, nt=nt):
                qrows, krows, off = _tile_rows(ti, nt, lb, d)
                q = qs[qrows, :]
                kw = ks[krows, :].astype(BF16)
                vw = v_ref[krows, :].astype(BF16)
                rel = dcol + off
                valid = (rel >= -BAND) & (rel <= BAND)
                m_old, l_old, o_old = m_ref[qrows, :], l_acc[qrows, :], o_ref[qrows, :]
                new = []
                for e in (0, 1):
                    mine = first_head if e == 0 else jnp.logical_not(first_head)
                    qe = jnp.where(mine, q, 0.0).astype(BF16)
                    s = lax.dot_general(qe, kw, (((1,), (1,)), ((), ())), preferred_element_type=F32) * SCALE
                    s = jnp.where(valid, s, NEG_INF)
                    mo = m_old[:, e * HEAD_DIM:e * HEAD_DIM + 1]
                    mt = jnp.maximum(mo, jnp.max(s, axis=-1, keepdims=True))
                    p = jnp.exp(s - mt)
                    alpha = jnp.exp(mo - mt)
                    lt = alpha * l_old[:, e * HEAD_DIM:e * HEAD_DIM + 1] + jnp.sum(p, axis=-1, keepdims=True)
                    pv = jnp.dot(p.astype(BF16), vw, preferred_element_type=F32)
                    new.append((mt, lt, alpha * o_old + pv))
                m_ref[qrows, :] = jnp.where(first_head, new[0][0], new[1][0])
                l_acc[qrows, :] = jnp.where(first_head, new[0][1], new[1][1])
                o_ref[qrows, :] = jnp.where(first_head, new[0][2], new[1][2])
                return carry

            lax.fori_loop(0, d * nt, tile, 0)

        def finish(rows):
            l_sum = l_acc[rows, :]
            o_ref[rows, :] = o_ref[rows, :] / l_sum
            m_ref[rows, :] = m_ref[rows, :] + jnp.log(l_sum)

        _for_row_chunks(t, finish)

    cols = lambda base: pl.BlockSpec((t, LANES), lambda g: (0, base + g))
    return pl.pallas_call(
        body, grid=(npair,),
        in_specs=[cols(0), cols(npair), cols(2 * npair), WHOLE_VMEM, WHOLE_VMEM],
        out_specs=[cols(0), cols(0)],
        out_shape=[jax.ShapeDtypeStruct((t, ATTN_W), F32)] * 2,
        scratch_shapes=[pltpu.VMEM((t, LANES), F32)] * 3, name="attn_fwd")(proj, proj, proj, cos, sin)


def _attn_bwd(proj, cos, sin, d_attn, lse, delta):
    t = proj.shape[0]
    npair = ATTN_W // LANES

    def body(q_ref, k_ref, v_ref, cos_ref, sin_ref, do_ref, l_ref, dl_ref, dq_ref, dk_ref, dv_ref,
             qs, ks, dq_acc, dk_acc, dv_acc):
        _rope_into(qs, q_ref, cos_ref, sin_ref, t)
        _rope_into(ks, k_ref, cos_ref, sin_ref, t)
        dq_acc[...] = jnp.zeros(dq_acc.shape, F32)
        dk_acc[...] = jnp.zeros(dk_acc.shape, F32)
        dv_acc[...] = jnp.zeros(dv_acc.shape, F32)
        first_head = lax.broadcasted_iota(jnp.int32, (TQ, LANES), 1) < HEAD_DIM
        dcol = lax.broadcasted_iota(jnp.int32, (TQ, WIN), 1) - lax.broadcasted_iota(jnp.int32, (TQ, WIN), 0)
        lanes_t = (((1,), (1,)), ((), ()))
        rows_t = (((0,), (0,)), ((), ()))
        for d in DILATIONS:
            lb = t // d
            nt = lb // TQ

            def tile(ti, carry, d=d, lb=lb, nt=nt):
                qrows, krows, off = _tile_rows(ti, nt, lb, d)
                q, do = qs[qrows, :], do_ref[qrows, :]
                lse_t, delta_t = l_ref[qrows, :], dl_ref[qrows, :]
                kw = ks[krows, :].astype(BF16)
                vw = v_ref[krows, :].astype(BF16)
                rel = dcol + off
                valid = (rel >= -BAND) & (rel <= BAND)
                dq_t = jnp.zeros((TQ, LANES), F32)
                dk_w = jnp.zeros((WIN, LANES), F32)
                dv_w = jnp.zeros((WIN, LANES), F32)
                for e in (0, 1):
                    mine = first_head if e == 0 else jnp.logical_not(first_head)
                    col = slice(e * HEAD_DIM, e * HEAD_DIM + 1)
                    qe = jnp.where(mine, q, 0.0).astype(BF16)
                    doe = jnp.where(mine, do, 0.0).astype(BF16)
                    s = lax.dot_general(qe, kw, lanes_t, preferred_element_type=F32) * SCALE
                    p = jnp.exp(jnp.where(valid, s, NEG_INF) - lse_t[:, col])
                    dp = lax.dot_general(doe, vw, lanes_t, preferred_element_type=F32)
                    ds = (p * (dp - delta_t[:, col]) * SCALE).astype(BF16)
                    dq_t = dq_t + jnp.where(mine, jnp.dot(ds, kw, preferred_element_type=F32), 0.0)
                    dk_w = dk_w + lax.dot_general(ds, qe, rows_t, preferred_element_type=F32)
                    dv_w = dv_w + lax.dot_general(p.astype(BF16), doe, rows_t, preferred_element_type=F32)
                dq_acc[qrows, :] += dq_t
                dk_acc[krows, :] += dk_w
                dv_acc[krows, :] += dv_w
                return carry

            lax.fori_loop(0, d * nt, tile, 0)

        def finish(rows):
            dq_ref[rows, :] = _rope_transpose(dq_acc[rows, :], cos_ref[rows, :], sin_ref[rows, :]).astype(BF16)
            dk_ref[rows, :] = _rope_transpose(dk_acc[rows, :], cos_ref[rows, :], sin_ref[rows, :]).astype(BF16)
            dv_ref[rows, :] = dv_acc[rows, :].astype(BF16)

        _for_row_chunks(t, finish)

    cols = lambda base: pl.BlockSpec((t, LANES), lambda g: (0, base + g))
    return pl.pallas_call(
        body, grid=(npair,),
        in_specs=[cols(0), cols(npair), cols(2 * npair), WHOLE_VMEM, WHOLE_VMEM, cols(0), cols(0), cols(0)],
        out_specs=[cols(0)] * 3,
        out_shape=[jax.ShapeDtypeStruct((t, ATTN_W), BF16)] * 3,
        scratch_shapes=[pltpu.VMEM((t, LANES), F32)] * 5, name="attn_bwd")(
            proj, proj, proj, cos, sin, d_attn, lse, delta)


def _norm_bwd(dres, pre, post, *, tm):
    t, w = dres.shape
    row = pl.BlockSpec((tm, w), lambda i: (i, 0))
    gsum = _const_spec((1, w), (0, 0))
    ins, in_specs, out_shape, out_specs = [dres], [row], [], []
    if pre is not None:
        dh, x, g3, l = pre
        ins += [dh, x, g3]
        in_specs += [row, row, _gain_spec(g3, l)]
        out_shape += [jax.ShapeDtypeStruct((t, w), F32), jax.ShapeDtypeStruct((1, w), F32)]
        out_specs += [row, gsum]
    if post is not None:
        y, g3, l = post
        ins += [y, g3]
        in_specs += [row, _gain_spec(g3, l)]
        out_shape += [jax.ShapeDtypeStruct((t, w), BF16), jax.ShapeDtypeStruct((1, w), F32)]
        out_specs += [row, gsum]
    n_in = len(ins)

    def body(*refs):
        first = pl.program_id(0) == 0
        ins_r, outs_r = list(refs[:n_in]), list(refs[n_in:])
        d = ins_r.pop(0)[...]
        if pre is not None:
            dh_ref, x_ref, g_ref = ins_r[:3]
            ins_r = ins_r[3:]
            dx, dg = _rms_bwd(x_ref[...], g_ref[...], dh_ref[...])
            d = d + dx
            outs_r.pop(0)[...] = d
            _accumulate(outs_r.pop(0), dg, first)
        if post is not None:
            y_ref, g_ref = ins_r
            dy, dg = _rms_bwd(y_ref[...], g_ref[...], d)
            outs_r.pop(0)[...] = dy.astype(BF16)
            _accumulate(outs_r.pop(0), dg, first)

    out = list(pl.pallas_call(body, grid=(t // tm,), in_specs=in_specs, out_specs=out_specs,
                              out_shape=out_shape, name="norm_bwd")(*ins))
    d_new, dg_pre = (out.pop(0), out.pop(0)) if pre is not None else (None, None)
    dy, dg_post = (out.pop(0), out.pop(0)) if post is not None else (None, None)
    return d_new, dy, dg_pre, dg_post


def _swiglu_bwd(gu, da, *, tm):
    t = gu.shape[0]

    def body(g_ref, u_ref, da_ref, o_ref):
        g, u, d = g_ref[...], u_ref[...], da_ref[...]
        sig = jax.nn.sigmoid(g)
        o_ref[:, :FFN] = (d * u * (sig * (1.0 + g * (1.0 - sig)))).astype(BF16)
        o_ref[:, FFN:] = (d * (g * sig)).astype(BF16)

    half = lambda j: pl.BlockSpec((tm, FFN), lambda i: (i, j))
    return pl.pallas_call(
        body, grid=(t // tm,), in_specs=[half(0), half(1), half(0)],
        out_specs=pl.BlockSpec((tm, 2 * FFN), lambda i: (i, 0)),
        out_shape=jax.ShapeDtypeStruct((t, 2 * FFN), BF16), name="swiglu_bwd")(gu, gu, da)


def _merge_bwd(d_merged, attn, conv_y, ga3, gc3, l, *, tm):
    t = attn.shape[0]
    row = pl.BlockSpec((tm, ATTN_W), lambda i: (i, 0))
    gsum = _const_spec((1, ATTN_W), (0, 0))

    def body(dma_ref, dmc_ref, a_ref, c_ref, ga_ref, gc_ref, da_ref, dl_ref, dc_ref, dga_ref, dgc_ref):
        first = pl.program_id(0) == 0
        attn_t = a_ref[...]
        da, dga = _rms_bwd(attn_t, ga_ref[...], dma_ref[...])
        dc, dgc = _rms_bwd(c_ref[...], gc_ref[...], dmc_ref[...])
        da_ref[...] = da
        dc_ref[...] = dc
        same_head = (lax.broadcasted_iota(jnp.int32, (ATTN_W, ATTN_W), 0) // HEAD_DIM
                     == lax.broadcasted_iota(jnp.int32, (ATTN_W, ATTN_W), 1) // HEAD_DIM).astype(BF16)
        rest = da * attn_t
        total = jnp.zeros(rest.shape, F32)
        for _ in range(3):
            term = rest.astype(BF16)
            total = total + jnp.dot(term, same_head, preferred_element_type=F32)
            rest = rest - term.astype(F32)
        dl_ref[...] = total
        _accumulate(dga_ref, dga, first)
        _accumulate(dgc_ref, dgc, first)

    return pl.pallas_call(
        body, grid=(t // tm,),
        in_specs=[pl.BlockSpec((tm, ATTN_W), lambda i: (i, 0)), pl.BlockSpec((tm, CONV_W), lambda i: (i, 1)),
                  row, row, _gain_spec(ga3, l), _gain_spec(gc3, l)],
        out_specs=[row, row, row, gsum, gsum],
        out_shape=[jax.ShapeDtypeStruct((t, ATTN_W), F32)] * 3 + [jax.ShapeDtypeStruct((1, ATTN_W), F32)] * 2,
        name="merge_bwd")(d_merged, d_merged, attn, conv_y, ga3, gc3)


def _conv_bwd(proj, conv_w, l, d_conv_y):
    t = proj.shape[0]
    col0 = 3 * ATTN_W // LANES
    nj = CONV_W // LANES

    def body(u_ref, gb_ref, gc_ref, w_ref, dy_ref, du_ref, dgb_ref, dgc_ref, dw_ref):
        u, gc, dy = u_ref[...], gc_ref[...], dy_ref[...]
        row = lax.broadcasted_iota(jnp.int32, u.shape, 0)
        down = lambda a: jnp.where(row == 0, 0.0, pltpu.roll(a, 1, 0))
        up = lambda a: jnp.where(row == t - 1, 0.0, pltpu.roll(a, t - 1, 0))
        w = w_ref[...]
        c = gc * u
        c_prev, c_next = down(c), up(c)
        dgb_ref[...] = (dy * (w[0:1] * c_prev + w[1:2] * c + w[2:3] * c_next)).astype(BF16)
        dz = dy * gb_ref[...]
        dc = w[0:1] * up(dz) + w[1:2] * dz + w[2:3] * down(dz)
        du_ref[...] = (dc * gc).astype(BF16)
        dgc_ref[...] = (dc * u).astype(BF16)
        dw_ref[0:1, :] = jnp.sum(dz * c_prev, axis=0, keepdims=True)
        dw_ref[1:2, :] = jnp.sum(dz * c, axis=0, keepdims=True)
        dw_ref[2:3, :] = jnp.sum(dz * c_next, axis=0, keepdims=True)

    cols = lambda base: pl.BlockSpec((t, LANES), lambda j: (0, base + j))
    return pl.pallas_call(
        body, grid=(nj,),
        in_specs=[cols(col0), cols(col0 + nj), cols(col0 + 2 * nj),
                  pl.BlockSpec((None, None, 3, LANES), lambda j: (l, j, 0, 0)), cols(0)],
        out_specs=[cols(0)] * 3 + [pl.BlockSpec((None, 3, LANES), lambda j: (j, 0, 0))],
        out_shape=[jax.ShapeDtypeStruct((t, CONV_W), BF16)] * 3 + [jax.ShapeDtypeStruct((nj, 3, LANES), F32)],
        name="conv_bwd")(proj, proj, proj, conv_w, d_conv_y)


def _place():
    x, y, c = lax.axis_index("x"), lax.axis_index("y"), lax.axis_index("c")
    other_chips = [(1 - x, y), (x, 1 - y), (1 - x, 1 - y)]
    return x, y, c, other_chips


def _remote(src, dst, send_sem, recv_sem, to):
    return pltpu.make_async_remote_copy(src_ref=src, dst_ref=dst, send_sem=send_sem, recv_sem=recv_sem,
                                        device_id=to, device_id_type=MESH)


def _allgather_weights(shards):
    n = len(shards)
    depth = shards[0].shape[0]
    half = depth // 2

    def body(*refs):
        srcs, dsts = refs[:n], refs[n:2 * n]
        send_sems, recv_sems, local_sems = refs[2 * n:]
        x, y, c, chips = _place()
        s_own = 2 * x + y
        sibling = (x, y, 1 - c)
        mine = pl.ds(c * half, half)
        theirs = pl.ds((1 - c) * half, half)
        local = [pltpu.make_async_copy(srcs[a], dsts[a].at[:, s_own], local_sems.at[a]) for a in range(n)]
        for cp in local:
            cp.start()
        direct, passed = [], []
        for a in range(n):
            for j, chip in enumerate(chips):
                direct.append(_remote(srcs[a].at[mine], dsts[a].at[mine, s_own],
                                      send_sems.at[a * 3 + j], recv_sems.at[a * 3 + j], (*chip, c)))
        for cp in direct:
            cp.start()
        for a in range(n):
            for j, chip in enumerate(chips):
                landed = dsts[a].at[mine, 2 * chip[0] + chip[1]]
                _remote(landed, landed, send_sems.at[a * 3 + j], recv_sems.at[a * 3 + j], (*chip, c)).wait_recv()
                fwd = _remote(landed, landed, send_sems.at[3 * n + a * 3 + j], recv_sems.at[3 * n + a * 3 + j], sibling)
                fwd.start()
                passed.append(fwd)
        for a in range(n):
            for j, chip in enumerate(chips):
                landed = dsts[a].at[theirs, 2 * chip[0] + chip[1]]
                _remote(landed, landed, send_sems.at[3 * n + a * 3 + j], recv_sems.at[3 * n + a * 3 + j],
                        sibling).wait_recv()
        for cp in direct + passed:
            cp.wait_send()
        for cp in local:
            cp.wait()

    out_shape = [jax.ShapeDtypeStruct((depth, N_CHIPS) + s.shape[1:], s.dtype) for s in shards]
    return pl.pallas_call(
        body, in_specs=[ANY] * n, out_specs=[ANY] * n, out_shape=out_shape,
        scratch_shapes=[pltpu.SemaphoreType.DMA((6 * n,)), pltpu.SemaphoreType.DMA((6 * n,)),
                        pltpu.SemaphoreType.DMA((n,))],
        name="allgather_weights")(*shards)


def _exchange_halves(grads):
    n = len(grads)

    def body(*refs):
        srcs, dsts = refs[:n], refs[n:2 * n]
        send_sems, recv_sems = refs[2 * n:]
        x, y, c, _ = _place()
        copies = []
        for a in range(n):
            hr = grads[a].shape[2] // 2
            theirs = srcs[a].at[:, :, pl.ds(pl.multiple_of((1 - c) * hr, 8), hr), :]
            copies.append(_remote(theirs, dsts[a], send_sems.at[a], recv_sems.at[a], (x, y, 1 - c)))
        for cp in copies:
            cp.start()
        for cp in copies:
            cp.wait()

    out_shape = [jax.ShapeDtypeStruct(g.shape[:2] + (g.shape[2] // 2, g.shape[3]), F32) for g in grads]
    return pl.pallas_call(
        body, in_specs=[ANY] * n, out_specs=[ANY] * n, out_shape=out_shape,
        scratch_shapes=[pltpu.SemaphoreType.DMA((n,)), pltpu.SemaphoreType.DMA((n,))],
        name="exchange_halves")(*grads)


def _add_halves(grad, got, core):
    depth, s_n, rows, cols = grad.shape
    hr = rows // 2

    def body(c_ref, g_ref, r_ref, o_ref):
        del c_ref
        o_ref[...] = (g_ref[...] + r_ref[...]).astype(BF16)

    grid_spec = pltpu.PrefetchScalarGridSpec(
        num_scalar_prefetch=1, grid=(depth, s_n),
        in_specs=[pl.BlockSpec((None, None, hr, cols), lambda l, s, c_ref: (l, s, c_ref[0], 0)),
                  pl.BlockSpec((None, None, hr, cols), lambda l, s, c_ref: (l, s, 0, 0))],
        out_specs=pl.BlockSpec((None, None, hr, cols), lambda l, s, c_ref: (l, s, 0, 0)))
    return pl.pallas_call(body, grid_spec=grid_spec, out_shape=jax.ShapeDtypeStruct((depth, s_n, hr, cols), BF16),
                          name="add_halves")(core, grad, got)


def _exchange_partials(partials):
    n = len(partials)

    def body(*refs):
        srcs, dsts = refs[:n], refs[n:2 * n]
        send_sems, recv_sems = refs[2 * n:]
        x, y, c, chips = _place()
        copies = []
        for a in range(n):
            for k, chip in enumerate(chips):
                copies.append(_remote(srcs[a].at[:, 2 * chip[0] + chip[1]], dsts[a].at[k],
                                      send_sems.at[a * 3 + k], recv_sems.at[a * 3 + k], (*chip, c)))
        for cp in copies:
            cp.start()
        for cp in copies:
            cp.wait()

    out_shape = [jax.ShapeDtypeStruct((3, p.shape[0]) + p.shape[2:], BF16) for p in partials]
    return pl.pallas_call(
        body, in_specs=[ANY] * n, out_specs=[ANY] * n, out_shape=out_shape,
        scratch_shapes=[pltpu.SemaphoreType.DMA((3 * n,)), pltpu.SemaphoreType.DMA((3 * n,))],
        name="exchange_partials")(*partials)


def _sum_partials(partial, got, shard):
    depth, _, hr, cols = partial.shape

    def body(s_ref, p_ref, q_ref, o_ref):
        del s_ref
        acc = p_ref[...].astype(F32)
        for k in range(3):
            acc = acc + q_ref[k].astype(F32)
        o_ref[...] = acc

    grid_spec = pltpu.PrefetchScalarGridSpec(
        num_scalar_prefetch=1, grid=(depth,),
        in_specs=[pl.BlockSpec((None, None, hr, cols), lambda l, s_ref: (l, s_ref[0], 0, 0)),
                  pl.BlockSpec((3, None, hr, cols), lambda l, s_ref: (0, l, 0, 0))],
        out_specs=pl.BlockSpec((None, hr, cols), lambda l, s_ref: (l, 0, 0)))
    return pl.pallas_call(body, grid_spec=grid_spec, out_shape=jax.ShapeDtypeStruct((depth, hr, cols), F32),
                          name="sum_partials")(shard, partial, got)


def _share_halves(halves):
    n = len(halves)

    def body(*refs):
        srcs, dsts = refs[:n], refs[n:2 * n]
        send_sems, recv_sems, local_sems = refs[2 * n:]
        x, y, c, _ = _place()
        local, remote = [], []
        for a in range(n):
            hr = halves[a].shape[1]
            mine = dsts[a].at[:, pl.ds(pl.multiple_of(c * hr, 8), hr), :]
            local.append(pltpu.make_async_copy(srcs[a], mine, local_sems.at[a]))
            remote.append(_remote(srcs[a], mine, send_sems.at[a], recv_sems.at[a], (x, y, 1 - c)))
        for cp in local + remote:
            cp.start()
        for a in range(n):
            hr = halves[a].shape[1]
            theirs = dsts[a].at[:, pl.ds(pl.multiple_of((1 - c) * hr, 8), hr), :]
            _remote(theirs, theirs, send_sems.at[a], recv_sems.at[a], (x, y, 1 - c)).wait_recv()
        for cp in remote:
            cp.wait_send()
        for cp in local:
            cp.wait()

    out_shape = [jax.ShapeDtypeStruct((h.shape[0], 2 * h.shape[1], h.shape[2]), F32) for h in halves]
    return pl.pallas_call(
        body, in_specs=[ANY] * n, out_specs=[ANY] * n, out_shape=out_shape,
        scratch_shapes=[pltpu.SemaphoreType.DMA((n,)), pltpu.SemaphoreType.DMA((n,)), pltpu.SemaphoreType.DMA((n,))],
        name="share_halves")(*halves)


def _allreduce_small(vec, loss_row):
    rows = vec.shape[0]

    def body(v_ref, o_ref, slots, send_sems, recv_sems):
        x, y, c, _ = _place()
        me = 4 * x + 2 * y + c
        slots[me] = v_ref[...]
        copies = []
        for k in range(1, N_DEV):
            flip = lambda v, bit: 1 - v if bit else v
            peer = (flip(x, k & 4), flip(y, k & 2), flip(c, k & 1))
            copies.append(_remote(v_ref, slots.at[me], send_sems.at[k - 1], recv_sems.at[k - 1], peer))
        for cp in copies:
            cp.start()
        for k in range(1, N_DEV):
            flip = lambda v, bit: 1 - v if bit else v
            peer_id = 4 * flip(x, k & 4) + 2 * flip(y, k & 2) + flip(c, k & 1)
            _remote(v_ref, slots.at[peer_id], send_sems.at[k - 1], recv_sems.at[k - 1], (x, y, c)).wait_recv()
        for cp in copies:
            cp.wait_send()
        total = slots[0]
        for dev in range(1, N_DEV):
            total = total + slots[dev]
        o_ref[...] = total
        o_ref[loss_row:loss_row + 1, :] = jnp.broadcast_to(
            jnp.sum(total[loss_row:loss_row + 1, :], axis=-1, keepdims=True), (1, LANES))

    return pl.pallas_call(
        body, in_specs=[WHOLE_VMEM], out_specs=WHOLE_VMEM, out_shape=jax.ShapeDtypeStruct((rows, LANES), F32),
        scratch_shapes=[pltpu.VMEM((N_DEV, rows, LANES), F32), pltpu.SemaphoreType.DMA((N_DEV - 1,)),
                        pltpu.SemaphoreType.DMA((N_DEV - 1,))],
        name="allreduce_small")(vec)


def _adamw(w, g, m, v, *, tr):
    depth, rows, cols = w.shape
    assert rows % tr == 0
    c1 = float(np.float32(1.0 - ADAM_B1 ** ADAM_STEP))
    c2 = float(np.float32(1.0 - ADAM_B2 ** ADAM_STEP))

    def body(w_ref, g_ref, m_ref, v_ref, d_ref, mo_ref, vo_ref):
        g_t = g_ref[...]
        m_new = ADAM_B1 * m_ref[...] + (1.0 - ADAM_B1) * g_t
        v_new = ADAM_B2 * v_ref[...] + (1.0 - ADAM_B2) * (g_t * g_t)
        mo_ref[...] = m_new
        vo_ref[...] = v_new
        d_ref[...] = -ADAM_LR * ((m_new / c1) / (jnp.sqrt(v_new / c2) + ADAM_EPS) + ADAM_WD * w_ref[...])

    blk = pl.BlockSpec((None, tr, cols), lambda l, i: (l, i, 0))
    return pl.pallas_call(
        body, grid=(depth, rows // tr), in_specs=[blk] * 4, out_specs=[blk] * 3,
        out_shape=[jax.ShapeDtypeStruct(w.shape, F32)] * 3, name="adamw")(w, g, m, v)


def _local_step(x, positions, target, gains, weights):
    t = x.shape[0]
    tm = 512
    inv_freq = ROPE_THETA ** (-jnp.arange(0, ROPE_DIM, 2, dtype=F32) / ROPE_DIM)
    lane = np.arange(LANES) % HEAD_DIM
    freq_row = jnp.where(lane < ROPE_DIM, inv_freq[lane % (ROPE_DIM // 2)], 0.0).astype(F32)[None, :]
    cos, sin = _rope_tables(positions.reshape(t, 1), freq_row)

    saved = []
    h1 = _norm_fwd(x, gains["pre_mix_norm"], 0, tm=tm)
    for l in range(DEPTH):
        proj = _mm_nn(h1, weights["w_in"], l, tm=1024, name="in_proj")
        attn, lse = _attn_fwd(proj, cos, sin)
        conv_y = _conv_fwd(proj, weights["conv_w"], l)
        merged = _merge_fwd(attn, conv_y, gains["attn_out_norm"], gains["conv_out_norm"], l, tm=tm)
        mix = _mm_nn(merged, weights["w_out"], l, tm=1024, name="out_proj")
        x1, h2 = _resnorm_fwd(x, mix, gains["post_mix_norm"], l, gains["pre_ffn_norm"], l, tm=tm)
        gu = _mm_nn(h2, weights["w_gate_up"], l, tm=1024, name="gate_up")
        act = _swiglu_fwd(gu, tm=tm)
        f = _mm_nn(act, weights["w_down"], l, tm=1024, name="down")
        nxt = (gains["pre_mix_norm"], l + 1) if l + 1 < DEPTH else (None, None)
        x2, h1_next = _resnorm_fwd(x1, f, gains["post_ffn_norm"], l, *nxt, tm=tm)
        saved.append(dict(x=x, h1=h1, proj=proj, attn=attn, lse=lse, conv_y=conv_y, merged=merged, mix=mix,
                          x1=x1, h2=h2, gu=gu, act=act, f=f))
        x, h1 = x2, h1_next

    dres, loss_lanes = _loss_fwd_bwd(x, target, tm=tm)

    gw = {k: lax.empty(weights[k].shape, F32) for k in ("w_in", "w_out", "w_gate_up", "w_down")}
    g_gain = {k: [None] * DEPTH for k in gains}
    g_conv = [None] * DEPTH
    _, df, _, g_gain["post_ffn_norm"][DEPTH - 1] = _norm_bwd(
        dres, None, (saved[-1]["f"], gains["post_ffn_norm"], DEPTH - 1), tm=tm)
    for l in reversed(range(DEPTH)):
        sv = saved[l]
        d_act = _mm_nt(df, weights["w_down"], l, tm=1024, tko=FFN // 2, name="down_dx")
        gw["w_down"] = _mm_tn(sv["act"], df, gw["w_down"], l, tka=256, name="down_dw")
        dgu = _swiglu_bwd(sv["gu"], d_act, tm=tm)
        dh2 = _mm_nt(dgu, weights["w_gate_up"], l, tm=1024, tko=D_MODEL, name="gate_up_dx")
        gw["w_gate_up"] = _mm_tn(sv["h2"], dgu, gw["w_gate_up"], l, tka=512, name="gate_up_dw")
        dx1, dmix, g_gain["pre_ffn_norm"][l], g_gain["post_mix_norm"][l] = _norm_bwd(
            dres, (dh2, sv["x1"], gains["pre_ffn_norm"], l), (sv["mix"], gains["post_mix_norm"], l), tm=tm)
        d_merged = _mm_nt(dmix, weights["w_out"], l, tm=1024, tko=D_MODEL, name="out_proj_dx")
        gw["w_out"] = _mm_tn(sv["merged"], dmix, gw["w_out"], l, tka=512, name="out_proj_dw")
        d_attn, delta, d_conv_y, g_gain["attn_out_norm"][l], g_gain["conv_out_norm"][l] = _merge_bwd(
            d_merged, sv["attn"], sv["conv_y"], gains["attn_out_norm"], gains["conv_out_norm"], l, tm=tm)
        dq, dk, dv = _attn_bwd(sv["proj"], cos, sin, d_attn, sv["lse"], delta)
        du, dgb, dgc, g_conv[l] = _conv_bwd(sv["proj"], weights["conv_w"], l, d_conv_y)
        d_proj = jnp.concatenate([dq, dk, dv, du, dgb, dgc], axis=1)
        dh1 = _mm_nt(d_proj, weights["w_in"], l, tm=1024, tko=D_MODEL, name="in_proj_dx")
        gw["w_in"] = _mm_tn(sv["h1"], d_proj, gw["w_in"], l, tka=512, name="in_proj_dw")
        below = (saved[l - 1]["f"], gains["post_ffn_norm"], l - 1) if l > 0 else None
        dres, df, g_gain["pre_mix_norm"][l], g_below = _norm_bwd(
            dx1, (dh1, sv["x"], gains["pre_mix_norm"], l), below, tm=tm)
        if l > 0:
            g_gain["post_ffn_norm"][l - 1] = g_below

    g_gain = {k: jnp.concatenate(v, axis=0) for k, v in g_gain.items()}
    return loss_lanes, dres, gw, g_gain, jnp.stack(g_conv, axis=0)


GAIN_NAMES = ("pre_mix_norm", "attn_out_norm", "conv_out_norm", "post_mix_norm", "pre_ffn_norm", "post_ffn_norm")
MATRIX_NAMES = ("w_in", "w_out", "w_gate_up", "w_down")
WEIGHT_ORDER = ("pre_mix_norm", "w_in", "conv_w", "attn_out_norm", "conv_out_norm", "w_out", "post_mix_norm",
                "pre_ffn_norm", "w_gate_up", "w_down", "post_ffn_norm")


def kernel(x, positions, pre_mix_norm, w_in, conv_w, attn_out_norm, conv_out_norm, w_out, post_mix_norm, pre_ffn_norm, w_gate_up, w_down, post_ffn_norm, loss_target, m_pre_mix_norm, m_w_in, m_conv_w, m_attn_out_norm, m_conv_out_norm, m_w_out, m_post_mix_norm, m_pre_ffn_norm, m_w_gate_up, m_w_down, m_post_ffn_norm, v_pre_mix_norm, v_w_in, v_conv_w, v_attn_out_norm, v_conv_out_norm, v_w_out, v_post_mix_norm, v_pre_ffn_norm, v_w_gate_up, v_w_down, v_post_ffn_norm):
    params = dict(pre_mix_norm=pre_mix_norm, w_in=w_in, conv_w=conv_w, attn_out_norm=attn_out_norm,
                  conv_out_norm=conv_out_norm, w_out=w_out, post_mix_norm=post_mix_norm, pre_ffn_norm=pre_ffn_norm,
                  w_gate_up=w_gate_up, w_down=w_down, post_ffn_norm=post_ffn_norm)
    mom1 = dict(pre_mix_norm=m_pre_mix_norm, w_in=m_w_in, conv_w=m_conv_w, attn_out_norm=m_attn_out_norm,
                conv_out_norm=m_conv_out_norm, w_out=m_w_out, post_mix_norm=m_post_mix_norm,
                pre_ffn_norm=m_pre_ffn_norm, w_gate_up=m_w_gate_up, w_down=m_w_down, post_ffn_norm=m_post_ffn_norm)
    mom2 = dict(pre_mix_norm=v_pre_mix_norm, w_in=v_w_in, conv_w=v_conv_w, attn_out_norm=v_attn_out_norm,
                conv_out_norm=v_conv_out_norm, w_out=v_w_out, post_mix_norm=v_post_mix_norm,
                pre_ffn_norm=v_pre_ffn_norm, w_gate_up=v_w_gate_up, w_down=v_w_down, post_ffn_norm=v_post_ffn_norm)
    xi, yi, ci = lax.axis_index("x"), lax.axis_index("y"), lax.axis_index("c")
    core = jnp.reshape(ci, (1,)).astype(jnp.int32)
    shard = jnp.reshape(2 * xi + yi, (1,)).astype(jnp.int32)

    gathered = _allgather_weights([params[k].astype(BF16) for k in MATRIX_NAMES] + [conv_w])
    weights = dict(zip(MATRIX_NAMES + ("conv_w",), gathered))
    weights["w_out"] = weights["w_out"].reshape(DEPTH, 1, D_MODEL, D_MODEL)
    weights["w_down"] = weights["w_down"].reshape(DEPTH, 1, FFN, D_MODEL)
    gains = {k: params[k][:, None, :] for k in GAIN_NAMES}

    loss_lanes, grad_x, gw, g_gain, g_conv = _local_step(x[0], positions[0], loss_target[0], gains, weights)
    gw["w_out"] = gw["w_out"].reshape(DEPTH, N_CHIPS, D_MODEL // N_CHIPS, D_MODEL)
    gw["w_down"] = gw["w_down"].reshape(DEPTH, N_CHIPS, FFN // N_CHIPS, D_MODEL)

    grads = [gw[k] for k in MATRIX_NAMES]
    got = _exchange_halves(grads)
    partials = [_add_halves(g, r, core) for g, r in zip(grads, got)]
    others = _exchange_partials(partials)
    halves = [_sum_partials(p, q, shard) for p, q in zip(partials, others)]
    grad = dict(zip(MATRIX_NAMES, _share_halves(halves)))

    small = [g_gain[k].reshape(-1) for k in GAIN_NAMES] + [g_conv.reshape(-1), loss_lanes.reshape(-1)]
    sizes = [int(s.shape[0]) for s in small]
    flat = jnp.concatenate(small)
    loss_row = (sum(sizes) - LANES) // LANES
    rows = -(-flat.shape[0] // (8 * LANES)) * 8
    flat = jnp.pad(flat, (0, rows * LANES - flat.shape[0])).reshape(rows, LANES)
    total = _allreduce_small(flat, loss_row).reshape(-1)
    offsets = np.cumsum([0] + sizes)
    for i, k in enumerate(GAIN_NAMES):
        grad[k] = total[offsets[i]:offsets[i + 1]].reshape(params[k].shape)
    conv_all = total[offsets[6]:offsets[7]].reshape(DEPTH, N_CHIPS, 3, LANES)
    grad["conv_w"] = lax.dynamic_index_in_dim(conv_all, 2 * xi + yi, axis=1, keepdims=False)
    loss = total[offsets[7]]

    delta, new_m, new_v = {}, {}, {}
    for k in WEIGHT_ORDER:
        shape = params[k].shape
        as3 = (lambda a: a) if len(shape) == 3 else (lambda a: a[:, None, :])
        rows_k = shape[1] if len(shape) == 3 else 1
        tr = {1024: 512, 704: 352, 256: 256}.get(rows_k, rows_k)
        d, m, v = _adamw(as3(params[k]), as3(grad[k]), as3(mom1[k]), as3(mom2[k]), tr=tr)
        delta[k], new_m[k], new_v[k] = d.reshape(shape), m.reshape(shape), v.reshape(shape)

    return (loss, grad_x[None], *[grad[k] for k in WEIGHT_ORDER], *[delta[k] for k in WEIGHT_ORDER],
            *[new_m[k] for k in WEIGHT_ORDER], *[new_v[k] for k in WEIGHT_ORDER])
```

```python
import functools

import numpy as np
import jax
import jax.numpy as jnp
from jax import lax
from jax.experimental import pallas as pl
from jax.experimental.pallas import tpu as pltpu

F32 = jnp.float32
BF16 = jnp.bfloat16
MESH = pl.DeviceIdType.MESH

D_MODEL = 1024
ATTN_W = 512
CONV_W = 512
HEAD_DIM = 64
ROPE_DIM = 16
ROPE_THETA = 500000.0
FFN = 2816
DEPTH = 4
RMS_EPS = 1e-6
NEG_INF = -1e30
N_CHIPS = 4
N_DEV = 8
LANES = 128
DILATIONS = (1, 4, 16)
BAND = 64
TQ = 128
WIN = TQ + 2 * BAND
SCALE = HEAD_DIM ** -0.5

ADAM_LR = 0.001
ADAM_B1 = 0.9
ADAM_B2 = 0.999
ADAM_EPS = 1e-08
ADAM_WD = 0.01
ADAM_STEP = 10

ANY = pl.BlockSpec(memory_space=pl.ANY)
WHOLE_VMEM = pl.BlockSpec(memory_space=pltpu.VMEM)


def _const_spec(block, index):
    return pl.BlockSpec(block, lambda *_: index)


def _gain_spec(g3, l):
    return _const_spec((None, 1, g3.shape[-1]), (l, 0, 0))


def _rms_fwd(x, g):
    r = lax.rsqrt(jnp.mean(x * x, axis=-1, keepdims=True) + RMS_EPS)
    return (x * r) * g


def _rms_bwd(x, g, dy):
    r = lax.rsqrt(jnp.mean(x * x, axis=-1, keepdims=True) + RMS_EPS)
    xh = x * r
    u = dy * g
    dx = r * (u - xh * jnp.mean(xh * u, axis=-1, keepdims=True))
    return dx, jnp.sum(dy * xh, axis=0, keepdims=True)


def _accumulate(ref, value, first):
    @pl.when(first)
    def _():
        ref[...] = value

    @pl.when(jnp.logical_not(first))
    def _():
        ref[...] += value


def _rope_coeffs(cos, sin):
    m = lax.broadcasted_iota(jnp.int32, cos.shape, 1) % HEAD_DIM
    a = jnp.where(m < ROPE_DIM, cos, 1.0)
    b = jnp.where(m < ROPE_DIM // 2, -sin, 0.0)
    c = jnp.where((m >= ROPE_DIM // 2) & (m < ROPE_DIM), sin, 0.0)
    return a, b, c


def _rope_apply(t, cos, sin):
    a, b, c = _rope_coeffs(cos, sin)
    n = t.shape[1]
    return a * t + b * pltpu.roll(t, n - ROPE_DIM // 2, 1) + c * pltpu.roll(t, ROPE_DIM // 2, 1)


def _rope_transpose(dt, cos, sin):
    a, b, c = _rope_coeffs(cos, sin)
    n = dt.shape[1]
    return a * dt + pltpu.roll(b * dt, ROPE_DIM // 2, 1) + pltpu.roll(c * dt, n - ROPE_DIM // 2, 1)


def _mm_nn(a, w, l, *, tm, name):
    t, k = a.shape
    _, s_n, k2, n = w.shape
    assert k == k2 and t % tm == 0

    def body(a_ref, w_ref, o_ref):
        o_ref[...] = jnp.dot(a_ref[...], w_ref[...], preferred_element_type=F32)

    return pl.pallas_call(
        body, grid=(t // tm, s_n),
        in_specs=[pl.BlockSpec((tm, k), lambda i, s: (i, 0)),
                  pl.BlockSpec((None, None, k, n), lambda i, s: (l, s, 0, 0))],
        out_specs=pl.BlockSpec((tm, n), lambda i, s: (i, s)),
        out_shape=jax.ShapeDtypeStruct((t, s_n * n), F32), name=name)(a, w)


def _mm_nt(a, w, l, *, tm, tko, name):
    t, sn = a.shape
    _, s_n, ko, n = w.shape
    assert sn == s_n * n and t % tm == 0 and ko % tko == 0

    def body(a_ref, w_ref, o_ref):
        acc = lax.dot_general(a_ref[...], w_ref[...], (((1,), (1,)), ((), ())), preferred_element_type=F32)
        if s_n == 1:
            o_ref[...] = acc
        else:
            _accumulate(o_ref, acc, pl.program_id(2) == 0)

    return pl.pallas_call(
        body, grid=(t // tm, ko // tko, s_n),
        in_specs=[pl.BlockSpec((tm, n), lambda i, j, s: (i, s)),
                  pl.BlockSpec((None, None, tko, n), lambda i, j, s: (l, s, j, 0))],
        out_specs=pl.BlockSpec((tm, tko), lambda i, j, s: (i, j)),
        out_shape=jax.ShapeDtypeStruct((t, ko), F32), name=name)(a, w)


def _mm_tn(a, b, acc, l, *, tka, name):
    t, ka = a.shape
    _, s_n, ka2, n = acc.shape
    assert ka == ka2 and b.shape == (t, s_n * n) and ka % tka == 0

    def body(a_ref, b_ref, acc_ref, o_ref):
        del acc_ref
        o_ref[...] = lax.dot_general(a_ref[...], b_ref[...], (((0,), (0,)), ((), ())), preferred_element_type=F32)

    return pl.pallas_call(
        body, grid=(ka // tka, s_n),
        in_specs=[pl.BlockSpec((t, tka), lambda i, s: (0, i)),
                  pl.BlockSpec((t, n), lambda i, s: (0, s)),
                  ANY],
        out_specs=pl.BlockSpec((None, None, tka, n), lambda i, s: (l, s, i, 0)),
        out_shape=jax.ShapeDtypeStruct(acc.shape, F32), input_output_aliases={2: 0}, name=name)(a, b, acc)


def _rope_tables(positions_col, inv_freq_row):
    t = positions_col.shape[0]

    def body(pos_ref, f_ref, cos_ref, sin_ref):
        ang = pos_ref[...].astype(F32) * f_ref[...]
        cos_ref[...] = jnp.cos(ang)
        sin_ref[...] = jnp.sin(ang)

    return pl.pallas_call(
        body, out_shape=[jax.ShapeDtypeStruct((t, LANES), F32)] * 2, name="rope_tables")(positions_col, inv_freq_row)


def _norm_fwd(x, g3, l, *, tm):
    t, w = x.shape

    def body(x_ref, g_ref, h_ref):
        h_ref[...] = _rms_fwd(x_ref[...], g_ref[...]).astype(BF16)

    return pl.pallas_call(
        body, grid=(t // tm,),
        in_specs=[pl.BlockSpec((tm, w), lambda i: (i, 0)), _gain_spec(g3, l)],
        out_specs=pl.BlockSpec((tm, w), lambda i: (i, 0)),
        out_shape=jax.ShapeDtypeStruct((t, w), BF16), name="norm_fwd")(x, g3)


def _resnorm_fwd(x, y, g_post3, l_post, g_next3, l_next, *, tm):
    t, w = x.shape
    with_next = g_next3 is not None
    row = pl.BlockSpec((tm, w), lambda i: (i, 0))

    def body(x_ref, y_ref, gp_ref, *rest):
        x_new = x_ref[...] + _rms_fwd(y_ref[...], gp_ref[...])
        if with_next:
            gn_ref, xo_ref, h_ref = rest
            h_ref[...] = _rms_fwd(x_new, gn_ref[...]).astype(BF16)
        else:
            (xo_ref,) = rest
        xo_ref[...] = x_new

    ins = [x, y, g_post3] + ([g_next3] if with_next else [])
    in_specs = [row, row, _gain_spec(g_post3, l_post)] + ([_gain_spec(g_next3, l_next)] if with_next else [])
    out_shape = [jax.ShapeDtypeStruct((t, w), F32)] + ([jax.ShapeDtypeStruct((t, w), BF16)] if with_next else [])
    out = pl.pallas_call(body, grid=(t // tm,), in_specs=in_specs, out_specs=[row] * len(out_shape),
                         out_shape=out_shape, name="resnorm_fwd")(*ins)
    return (out[0], out[1]) if with_next else (out[0], None)


def _conv_fwd(proj, conv_w, l):
    t = proj.shape[0]
    col0 = 3 * ATTN_W // LANES

    def body(u_ref, gb_ref, gc_ref, w_ref, y_ref):
        c = gc_ref[...] * u_ref[...]
        row = lax.broadcasted_iota(jnp.int32, c.shape, 0)
        c_prev = jnp.where(row == 0, 0.0, pltpu.roll(c, 1, 0))
        c_next = jnp.where(row == t - 1, 0.0, pltpu.roll(c, t - 1, 0))
        w = w_ref[...]
        y_ref[...] = gb_ref[...] * (w[0:1] * c_prev + w[1:2] * c + w[2:3] * c_next)

    nj = CONV_W // LANES
    cols = lambda base: pl.BlockSpec((t, LANES), lambda j: (0, base + j))
    return pl.pallas_call(
        body, grid=(nj,),
        in_specs=[cols(col0), cols(col0 + nj), cols(col0 + 2 * nj),
                  pl.BlockSpec((None, None, 3, LANES), lambda j: (l, j, 0, 0))],
        out_specs=pl.BlockSpec((t, LANES), lambda j: (0, j)),
        out_shape=jax.ShapeDtypeStruct((t, CONV_W), F32), name="conv_fwd")(proj, proj, proj, conv_w)


def _merge_fwd(attn, conv_y, ga3, gc3, l, *, tm):
    t = attn.shape[0]
    row = pl.BlockSpec((tm, ATTN_W), lambda i: (i, 0))

    def body(a_ref, c_ref, ga_ref, gc_ref, m_ref):
        m_ref[:, :ATTN_W] = _rms_fwd(a_ref[...], ga_ref[...]).astype(BF16)
        m_ref[:, ATTN_W:] = _rms_fwd(c_ref[...], gc_ref[...]).astype(BF16)

    return pl.pallas_call(
        body, grid=(t // tm,),
        in_specs=[row, row, _gain_spec(ga3, l), _gain_spec(gc3, l)],
        out_specs=pl.BlockSpec((tm, D_MODEL), lambda i: (i, 0)),
        out_shape=jax.ShapeDtypeStruct((t, D_MODEL), BF16), name="merge_fwd")(attn, conv_y, ga3, gc3)


def _swiglu_fwd(gu, *, tm):
    t = gu.shape[0]

    def body(g_ref, u_ref, a_ref):
        g = g_ref[...]
        a_ref[...] = (g * jax.nn.sigmoid(g) * u_ref[...]).astype(BF16)

    return pl.pallas_call(
        body, grid=(t // tm,),
        in_specs=[pl.BlockSpec((tm, FFN), lambda i: (i, 0)), pl.BlockSpec((tm, FFN), lambda i: (i, 1))],
        out_specs=pl.BlockSpec((tm, FFN), lambda i: (i, 0)),
        out_shape=jax.ShapeDtypeStruct((t, FFN), BF16), name="swiglu_fwd")(gu, gu)


def _loss_fwd_bwd(y, target, *, tm):
    t, w = y.shape
    row = pl.BlockSpec((tm, w), lambda i: (i, 0))

    def body(y_ref, t_ref, dy_ref, loss_ref):
        e = y_ref[...] - t_ref[...]
        dy_ref[...] = e * (1.0 / w)
        sq = jnp.sum(e * e, axis=0, keepdims=True) * (0.5 / w)
        part = sq[:, :LANES]
        for j in range(1, w // LANES):
            part = part + sq[:, j * LANES:(j + 1) * LANES]
        _accumulate(loss_ref, part, pl.program_id(0) == 0)

    return pl.pallas_call(
        body, grid=(t // tm,), in_specs=[row, row],
        out_specs=[row, _const_spec((1, LANES), (0, 0))],
        out_shape=[jax.ShapeDtypeStruct((t, w), F32), jax.ShapeDtypeStruct((1, LANES), F32)], name="loss")(y, target)


def _tile_rows(t, nt, lb, d):
    r = t // nt
    q0 = (t % nt) * TQ
    m0 = jnp.clip(q0 - BAND, 0, lb - WIN)
    if d == 1:
        return pl.ds(pl.multiple_of(q0, TQ), TQ), pl.ds(pl.multiple_of(m0, BAND), WIN), m0 - q0
    return pl.ds(r + d * q0, TQ, stride=d), pl.ds(r + d * m0, WIN, stride=d), m0 - q0


def _for_row_chunks(t, fn, chunk=512):
    def step(i, carry):
        fn(pl.ds(pl.multiple_of(i * chunk, chunk), chunk))
        return carry

    lax.fori_loop(0, t // chunk, step, 0)


def _rope_into(dst_ref, src_ref, cos_ref, sin_ref, t, scale=1.0):
    def chunk(rows):
        dst_ref[rows, :] = _rope_apply(src_ref[rows, :], cos_ref[rows, :], sin_ref[rows, :]) * scale

    _for_row_chunks(t, chunk)


WINDOW_OFFSETS = (-BAND, 0, -2 * BAND)
LANE_CONTRACT = (((1,), (1,)), ((), ()))
ROW_CONTRACT = (((0,), (0,)), ((), ()))


def _fill_band_bias(bias_ref):
    rel0 = (lax.broadcasted_iota(jnp.int32, (2 * TQ, WIN), 1)
            - lax.broadcasted_iota(jnp.int32, (2 * TQ, WIN), 0) % TQ)
    for j, off in enumerate(WINDOW_OFFSETS):
        rel = rel0 + off
        bias_ref[j] = jnp.where((rel >= -BAND) & (rel <= BAND), 0.0, NEG_INF)


def _band_bias(bias_ref, off):
    return bias_ref[jnp.where(off == WINDOW_OFFSETS[0], 0, jnp.where(off == WINDOW_OFFSETS[1], 1, 2))]


def _stack_heads(a, first_head):
    return jnp.concatenate([jnp.where(first_head, a, 0.0), jnp.where(first_head, 0.0, a)], axis=0)


def _unstack_heads(a2, first_head):
    return jnp.where(first_head, a2[:TQ], a2[TQ:])


def _attn_fwd(proj, cos, sin):
    t = proj.shape[0]
    npair = ATTN_W // LANES

    def body(q_ref, k_ref, v_ref, cos_ref, sin_ref, o_ref, lse_ref, qs, ks, o1, o2, l0, l1, l2, m1, m2, bias):
        _rope_into(qs, q_ref, cos_ref, sin_ref, t, SCALE)
        _rope_into(ks, k_ref, cos_ref, sin_ref, t)
        _fill_band_bias(bias)
        outs, dens, maxs = (o_ref, o1, o2), (l0, l1, l2), (lse_ref, m1, m2)
        first_head = lax.broadcasted_iota(jnp.int32, (TQ, LANES), 1) < HEAD_DIM
        ones = jnp.ones((WIN, LANES), BF16)
        for b, d in enumerate(DILATIONS):
            lb = t // d
            nt = lb // TQ

            def tile(ti, carry, b=b, d=d, lb=lb, nt=nt):
                qrows, krows, off = _tile_rows(ti, nt, lb, d)
                q2 = _stack_heads(qs[qrows, :], first_head).astype(BF16)
                kw = ks[krows, :].astype(BF16)
                vw = jnp.concatenate([v_ref[krows, :].astype(BF16), ones], axis=1)
                s = lax.dot_general(q2, kw, LANE_CONTRACT, preferred_element_type=F32) + _band_bias(bias, off)
                m = jnp.max(s, axis=-1, keepdims=True)
                pv = jnp.dot(jnp.exp(s - m).astype(BF16), vw, preferred_element_type=F32)
                outs[b][qrows, :] = _unstack_heads(pv[:, :LANES], first_head)
                dens[b][qrows, :] = _unstack_heads(pv[:, LANES:], first_head)
                maxs[b][qrows, :] = _unstack_heads(jnp.broadcast_to(m, (2 * TQ, LANES)), first_head)
                return carry

            lax.fori_loop(0, d * nt, tile, 0, unroll=8)

        def finish(rows):
            ms = [m_b[rows, :] for m_b in maxs]
            m_all = jnp.maximum(jnp.maximum(ms[0], ms[1]), ms[2])
            ws = [jnp.exp(m_b - m_all) for m_b in ms]
            den = ws[0] * dens[0][rows, :] + ws[1] * dens[1][rows, :] + ws[2] * dens[2][rows, :]
            num = ws[0] * outs[0][rows, :] + ws[1] * outs[1][rows, :] + ws[2] * outs[2][rows, :]
            o_ref[rows, :] = num / den
            lse_ref[rows, :] = m_all + jnp.log(den)

        _for_row_chunks(t, finish, 256)

    cols = lambda base: pl.BlockSpec((t, LANES), lambda g: (0, base + g))
    return pl.pallas_call(
        body, grid=(npair,),
        in_specs=[cols(0), cols(npair), cols(2 * npair), WHOLE_VMEM, WHOLE_VMEM],
        out_specs=[cols(0), cols(0)],
        out_shape=[jax.ShapeDtypeStruct((t, ATTN_W), F32)] * 2,
        scratch_shapes=[pltpu.VMEM((t, LANES), F32)] * 9 + [pltpu.VMEM((len(WINDOW_OFFSETS), 2 * TQ, WIN), F32)],
        name="attn_fwd")(proj, proj, proj, cos, sin)


def _attn_bwd(proj, cos, sin, d_attn, lse, delta):
    t = proj.shape[0]
    npair = ATTN_W // LANES

    def body(q_ref, k_ref, v_ref, cos_ref, sin_ref, do_ref, l_ref, dl_ref, dq_ref, dk_ref, dv_ref,
             qs, ks, dq_acc, dk_acc, dv_acc, bias):
        _rope_into(qs, q_ref, cos_ref, sin_ref, t, SCALE)
        _rope_into(ks, k_ref, cos_ref, sin_ref, t)
        _fill_band_bias(bias)
        dq_acc[...] = jnp.zeros(dq_acc.shape, F32)
        dk_acc[...] = jnp.zeros(dk_acc.shape, F32)
        dv_acc[...] = jnp.zeros(dv_acc.shape, F32)
        first_head = lax.broadcasted_iota(jnp.int32, (TQ, LANES), 1) < HEAD_DIM

        def stack_column(a):
            return jnp.concatenate([a[:, 0:1], a[:, HEAD_DIM:HEAD_DIM + 1]], axis=0)

        for d in DILATIONS:
            lb = t // d
            nt = lb // TQ

            def tile(ti, carry, d=d, lb=lb, nt=nt):
                qrows, krows, off = _tile_rows(ti, nt, lb, d)
                q2 = _stack_heads(qs[qrows, :], first_head).astype(BF16)
                do2 = _stack_heads(do_ref[qrows, :], first_head).astype(BF16)
                kw = ks[krows, :].astype(BF16)
                vw = v_ref[krows, :].astype(BF16)
                s = lax.dot_general(q2, kw, LANE_CONTRACT, preferred_element_type=F32) + _band_bias(bias, off)
                p = jnp.exp(s - stack_column(l_ref[qrows, :]))
                dp = lax.dot_general(do2, vw, LANE_CONTRACT, preferred_element_type=F32)
                ds = (p * (dp - stack_column(dl_ref[qrows, :]))).astype(BF16)
                dq2 = jnp.dot(ds, kw, preferred_element_type=F32)
                dq_acc[qrows, :] += _unstack_heads(dq2, first_head) * SCALE
                dk_acc[krows, :] += lax.dot_general(ds, q2, ROW_CONTRACT, preferred_element_type=F32)
                dv_acc[krows, :] += lax.dot_general(p.astype(BF16), do2, ROW_CONTRACT, preferred_element_type=F32)
                return carry

            lax.fori_loop(0, d * nt, tile, 0, unroll=4)

        def finish(rows):
            dq_ref[rows, :] = _rope_transpose(dq_acc[rows, :], cos_ref[rows, :], sin_ref[rows, :]).astype(BF16)
            dk_ref[rows, :] = _rope_transpose(dk_acc[rows, :], cos_ref[rows, :], sin_ref[rows, :]).astype(BF16)
            dv_ref[rows, :] = dv_acc[rows, :].astype(BF16)

        _for_row_chunks(t, finish)

    cols = lambda base: pl.BlockSpec((t, LANES), lambda g: (0, base + g))
    return pl.pallas_call(
        body, grid=(npair,),
        in_specs=[cols(0), cols(npair), cols(2 * npair), WHOLE_VMEM, WHOLE_VMEM, cols(0), cols(0), cols(0)],
        out_specs=[cols(0)] * 3,
        out_shape=[jax.ShapeDtypeStruct((t, ATTN_W), BF16)] * 3,
        scratch_shapes=[pltpu.VMEM((t, LANES), F32)] * 5 + [pltpu.VMEM((len(WINDOW_OFFSETS), 2 * TQ, WIN), F32)],
        name="attn_bwd")(proj, proj, proj, cos, sin, d_attn, lse, delta)


def _norm_bwd(dres, pre, post, *, tm):
    t, w = dres.shape
    row = pl.BlockSpec((tm, w), lambda i: (i, 0))
    gsum = _const_spec((1, w), (0, 0))
    ins, in_specs, out_shape, out_specs = [dres], [row], [], []
    if pre is not None:
        dh, x, g3, l = pre
        ins += [dh, x, g3]
        in_specs += [row, row, _gain_spec(g3, l)]
        out_shape += [jax.ShapeDtypeStruct((t, w), F32), jax.ShapeDtypeStruct((1, w), F32)]
        out_specs += [row, gsum]
    if post is not None:
        y, g3, l = post
        ins += [y, g3]
        in_specs += [row, _gain_spec(g3, l)]
        out_shape += [jax.ShapeDtypeStruct((t, w), BF16), jax.ShapeDtypeStruct((1, w), F32)]
        out_specs += [row, gsum]
    n_in = len(ins)

    def body(*refs):
        first = pl.program_id(0) == 0
        ins_r, outs_r = list(refs[:n_in]), list(refs[n_in:])
        d = ins_r.pop(0)[...]
        if pre is not None:
            dh_ref, x_ref, g_ref = ins_r[:3]
            ins_r = ins_r[3:]
            dx, dg = _rms_bwd(x_ref[...], g_ref[...], dh_ref[...])
            d = d + dx
            outs_r.pop(0)[...] = d
            _accumulate(outs_r.pop(0), dg, first)
        if post is not None:
            y_ref, g_ref = ins_r
            dy, dg = _rms_bwd(y_ref[...], g_ref[...], d)
            outs_r.pop(0)[...] = dy.astype(BF16)
            _accumulate(outs_r.pop(0), dg, first)

    out = list(pl.pallas_call(body, grid=(t // tm,), in_specs=in_specs, out_specs=out_specs,
                              out_shape=out_shape, name="norm_bwd")(*ins))
    d_new, dg_pre = (out.pop(0), out.pop(0)) if pre is not None else (None, None)
    dy, dg_post = (out.pop(0), out.pop(0)) if post is not None else (None, None)
    return d_new, dy, dg_pre, dg_post


def _swiglu_bwd(gu, da, *, tm):
    t = gu.shape[0]

    def body(g_ref, u_ref, da_ref, o_ref):
        g, u, d = g_ref[...], u_ref[...], da_ref[...]
        sig = jax.nn.sigmoid(g)
        o_ref[:, :FFN] = (d * u * (sig * (1.0 + g * (1.0 - sig)))).astype(BF16)
        o_ref[:, FFN:] = (d * (g * sig)).astype(BF16)

    half = lambda j: pl.BlockSpec((tm, FFN), lambda i: (i, j))
    return pl.pallas_call(
        body, grid=(t // tm,), in_specs=[half(0), half(1), half(0)],
        out_specs=pl.BlockSpec((tm, 2 * FFN), lambda i: (i, 0)),
        out_shape=jax.ShapeDtypeStruct((t, 2 * FFN), BF16), name="swiglu_bwd")(gu, gu, da)


def _merge_bwd(d_merged, attn, conv_y, ga3, gc3, l, *, tm):
    t = attn.shape[0]
    row = pl.BlockSpec((tm, ATTN_W), lambda i: (i, 0))
    gsum = _const_spec((1, ATTN_W), (0, 0))

    def body(dma_ref, dmc_ref, a_ref, c_ref, ga_ref, gc_ref, da_ref, dl_ref, dc_ref, dga_ref, dgc_ref):
        first = pl.program_id(0) == 0
        attn_t = a_ref[...]
        da, dga = _rms_bwd(attn_t, ga_ref[...], dma_ref[...])
        dc, dgc = _rms_bwd(c_ref[...], gc_ref[...], dmc_ref[...])
        da_ref[...] = da
        dc_ref[...] = dc
        same_head = (lax.broadcasted_iota(jnp.int32, (ATTN_W, ATTN_W), 0) // HEAD_DIM
                     == lax.broadcasted_iota(jnp.int32, (ATTN_W, ATTN_W), 1) // HEAD_DIM).astype(BF16)
        rest = da * attn_t
        total = jnp.zeros(rest.shape, F32)
        for _ in range(3):
            term = rest.astype(BF16)
            total = total + jnp.dot(term, same_head, preferred_element_type=F32)
            rest = rest - term.astype(F32)
        dl_ref[...] = total
        _accumulate(dga_ref, dga, first)
        _accumulate(dgc_ref, dgc, first)

    return pl.pallas_call(
        body, grid=(t // tm,),
        in_specs=[pl.BlockSpec((tm, ATTN_W), lambda i: (i, 0)), pl.BlockSpec((tm, CONV_W), lambda i: (i, 1)),
                  row, row, _gain_spec(ga3, l), _gain_spec(gc3, l)],
        out_specs=[row, row, row, gsum, gsum],
        out_shape=[jax.ShapeDtypeStruct((t, ATTN_W), F32)] * 3 + [jax.ShapeDtypeStruct((1, ATTN_W), F32)] * 2,
        name="merge_bwd")(d_merged, d_merged, attn, conv_y, ga3, gc3)


def _conv_bwd(proj, conv_w, l, d_conv_y):
    t = proj.shape[0]
    col0 = 3 * ATTN_W // LANES
    nj = CONV_W // LANES

    def body(u_ref, gb_ref, gc_ref, w_ref, dy_ref, du_ref, dgb_ref, dgc_ref, dw_ref):
        u, gc, dy = u_ref[...], gc_ref[...], dy_ref[...]
        row = lax.broadcasted_iota(jnp.int32, u.shape, 0)
        down = lambda a: jnp.where(row == 0, 0.0, pltpu.roll(a, 1, 0))
        up = lambda a: jnp.where(row == t - 1, 0.0, pltpu.roll(a, t - 1, 0))
        w = w_ref[...]
        c = gc * u
        c_prev, c_next = down(c), up(c)
        dgb_ref[...] = (dy * (w[0:1] * c_prev + w[1:2] * c + w[2:3] * c_next)).astype(BF16)
        dz = dy * gb_ref[...]
        dc = w[0:1] * up(dz) + w[1:2] * dz + w[2:3] * down(dz)
        du_ref[...] = (dc * gc).astype(BF16)
        dgc_ref[...] = (dc * u).astype(BF16)
        dw_ref[0:1, :] = jnp.sum(dz * c_prev, axis=0, keepdims=True)
        dw_ref[1:2, :] = jnp.sum(dz * c, axis=0, keepdims=True)
        dw_ref[2:3, :] = jnp.sum(dz * c_next, axis=0, keepdims=True)

    cols = lambda base: pl.BlockSpec((t, LANES), lambda j: (0, base + j))
    return pl.pallas_call(
        body, grid=(nj,),
        in_specs=[cols(col0), cols(col0 + nj), cols(col0 + 2 * nj),
                  pl.BlockSpec((None, None, 3, LANES), lambda j: (l, j, 0, 0)), cols(0)],
        out_specs=[cols(0)] * 3 + [pl.BlockSpec((None, 3, LANES), lambda j: (j, 0, 0))],
        out_shape=[jax.ShapeDtypeStruct((t, CONV_W), BF16)] * 3 + [jax.ShapeDtypeStruct((nj, 3, LANES), F32)],
        name="conv_bwd")(proj, proj, proj, conv_w, d_conv_y)


def _place():
    x, y, c = lax.axis_index("x"), lax.axis_index("y"), lax.axis_index("c")
    other_chips = [(1 - x, y), (x, 1 - y), (1 - x, 1 - y)]
    return x, y, c, other_chips


def _remote(src, dst, send_sem, recv_sem, to):
    return pltpu.make_async_remote_copy(src_ref=src, dst_ref=dst, send_sem=send_sem, recv_sem=recv_sem,
                                        device_id=to, device_id_type=MESH)


def _own_shard_slab(w, place, dtype):
    depth, rows, cols = w.shape
    tr = rows if rows <= 704 else 512
    assert rows % tr == 0

    def body(p_ref, w_ref, o_ref):
        del p_ref
        o_ref[...] = w_ref[...].astype(dtype)

    grid_spec = pltpu.PrefetchScalarGridSpec(
        num_scalar_prefetch=1, grid=(depth, rows // tr),
        in_specs=[pl.BlockSpec((None, tr, cols), lambda l, i, p: (l, i, 0))],
        out_specs=pl.BlockSpec((None, None, tr, cols), lambda l, i, p: (l, p[0], i, 0)))
    return pl.pallas_call(body, grid_spec=grid_spec, name="own_shard_slab",
                          out_shape=jax.ShapeDtypeStruct((depth, N_CHIPS, rows, cols), dtype))(place, w)


def _allgather_weights(slabs):
    n = len(slabs)
    depth = slabs[0].shape[0]
    half = depth // 2

    def body(*refs):
        dsts = refs[n:2 * n]
        send_sems, recv_sems = refs[2 * n:]
        x, y, c, chips = _place()
        s_own = 2 * x + y
        sibling = (x, y, 1 - c)
        mine = pl.ds(c * half, half)
        theirs = pl.ds((1 - c) * half, half)
        direct, passed = [], []
        for a in range(n):
            for j, chip in enumerate(chips):
                own = dsts[a].at[mine, s_own]
                direct.append(_remote(own, own, send_sems.at[a * 3 + j], recv_sems.at[a * 3 + j], (*chip, c)))
        for cp in direct:
            cp.start()
        for a in range(n):
            for j, chip in enumerate(chips):
                landed = dsts[a].at[mine, 2 * chip[0] + chip[1]]
                _remote(landed, landed, send_sems.at[a * 3 + j], recv_sems.at[a * 3 + j], (*chip, c)).wait_recv()
                fwd = _remote(landed, landed, send_sems.at[3 * n + a * 3 + j], recv_sems.at[3 * n + a * 3 + j], sibling)
                fwd.start()
                passed.append(fwd)
        for a in range(n):
            for j, chip in enumerate(chips):
                landed = dsts[a].at[theirs, 2 * chip[0] + chip[1]]
                _remote(landed, landed, send_sems.at[3 * n + a * 3 + j], recv_sems.at[3 * n + a * 3 + j],
                        sibling).wait_recv()
        for cp in direct + passed:
            cp.wait_send()

    out_shape = [jax.ShapeDtypeStruct(s.shape, s.dtype) for s in slabs]
    return pl.pallas_call(
        body, in_specs=[ANY] * n, out_specs=[ANY] * n, out_shape=out_shape,
        input_output_aliases={a: a for a in range(n)},
        scratch_shapes=[pltpu.SemaphoreType.DMA((6 * n,)), pltpu.SemaphoreType.DMA((6 * n,))],
        name="allgather_weights")(*slabs)


def _exchange_halves(grads):
    n = len(grads)

    def body(*refs):
        srcs, dsts = refs[:n], refs[n:2 * n]
        send_sems, recv_sems = refs[2 * n:]
        x, y, c, _ = _place()
        copies = []
        for a in range(n):
            hr = grads[a].shape[2] // 2
            theirs = srcs[a].at[:, :, pl.ds(pl.multiple_of((1 - c) * hr, 8), hr), :]
            copies.append(_remote(theirs, dsts[a], send_sems.at[a], recv_sems.at[a], (x, y, 1 - c)))
        for cp in copies:
            cp.start()
        for cp in copies:
            cp.wait()

    out_shape = [jax.ShapeDtypeStruct(g.shape[:2] + (g.shape[2] // 2, g.shape[3]), F32) for g in grads]
    return pl.pallas_call(
        body, in_specs=[ANY] * n, out_specs=[ANY] * n, out_shape=out_shape,
        scratch_shapes=[pltpu.SemaphoreType.DMA((n,)), pltpu.SemaphoreType.DMA((n,))],
        name="exchange_halves")(*grads)


def _add_halves(grad, got, place):
    depth, s_n, rows, cols = grad.shape
    hr = rows // 2

    def body(p_ref, g_ref, r_ref, o_ref):
        del p_ref
        o_ref[...] = (g_ref[...] + r_ref[...]).astype(BF16)

    grid_spec = pltpu.PrefetchScalarGridSpec(
        num_scalar_prefetch=1, grid=(depth, s_n),
        in_specs=[pl.BlockSpec((None, None, hr, cols), lambda l, s, p: (l, s, p[1], 0)),
                  pl.BlockSpec((None, None, hr, cols), lambda l, s, p: (l, s, 0, 0))],
        out_specs=pl.BlockSpec((None, None, hr, cols), lambda l, s, p: (l, s, 0, 0)))
    return pl.pallas_call(body, grid_spec=grid_spec, out_shape=jax.ShapeDtypeStruct((depth, s_n, hr, cols), BF16),
                          name="add_halves")(place, grad, got)


def _exchange_partials(partials):
    n = len(partials)

    def body(*refs):
        srcs, dsts = refs[:n], refs[n:2 * n]
        send_sems, recv_sems = refs[2 * n:]
        x, y, c, chips = _place()
        copies = []
        for a in range(n):
            for k, chip in enumerate(chips):
                copies.append(_remote(srcs[a].at[:, 2 * chip[0] + chip[1]], dsts[a].at[k],
                                      send_sems.at[a * 3 + k], recv_sems.at[a * 3 + k], (*chip, c)))
        for cp in copies:
            cp.start()
        for cp in copies:
            cp.wait()

    out_shape = [jax.ShapeDtypeStruct((3, p.shape[0]) + p.shape[2:], BF16) for p in partials]
    return pl.pallas_call(
        body, in_specs=[ANY] * n, out_specs=[ANY] * n, out_shape=out_shape,
        scratch_shapes=[pltpu.SemaphoreType.DMA((3 * n,)), pltpu.SemaphoreType.DMA((3 * n,))],
        name="exchange_partials")(*partials)


def _sum_partials(partial, got, place):
    depth, _, hr, cols = partial.shape

    def body(p_ref, mine_ref, got_ref, o_ref):
        del p_ref
        acc = mine_ref[...].astype(F32)
        for k in range(3):
            acc = acc + got_ref[k].astype(F32)
        o_ref[...] = acc

    grid_spec = pltpu.PrefetchScalarGridSpec(
        num_scalar_prefetch=1, grid=(depth,),
        in_specs=[pl.BlockSpec((None, None, hr, cols), lambda l, p: (l, p[0], 0, 0)),
                  pl.BlockSpec((3, None, hr, cols), lambda l, p: (0, l, 0, 0))],
        out_specs=pl.BlockSpec((None, hr, cols), lambda l, p: (l, p[1], 0)))
    return pl.pallas_call(body, grid_spec=grid_spec, out_shape=jax.ShapeDtypeStruct((depth, 2 * hr, cols), F32),
                          name="sum_partials")(place, partial, got)


def _share_halves(grads):
    n = len(grads)

    def body(*refs):
        dsts = refs[n:2 * n]
        send_sems, recv_sems = refs[2 * n:]
        x, y, c, _ = _place()
        remote = []
        for a in range(n):
            hr = grads[a].shape[1] // 2
            mine = dsts[a].at[:, pl.ds(pl.multiple_of(c * hr, 8), hr), :]
            remote.append(_remote(mine, mine, send_sems.at[a], recv_sems.at[a], (x, y, 1 - c)))
        for cp in remote:
            cp.start()
        for a in range(n):
            hr = grads[a].shape[1] // 2
            theirs = dsts[a].at[:, pl.ds(pl.multiple_of((1 - c) * hr, 8), hr), :]
            _remote(theirs, theirs, send_sems.at[a], recv_sems.at[a], (x, y, 1 - c)).wait_recv()
        for cp in remote:
            cp.wait_send()

    out_shape = [jax.ShapeDtypeStruct(g.shape, F32) for g in grads]
    return pl.pallas_call(
        body, in_specs=[ANY] * n, out_specs=[ANY] * n, out_shape=out_shape,
        input_output_aliases={a: a for a in range(n)},
        scratch_shapes=[pltpu.SemaphoreType.DMA((n,)), pltpu.SemaphoreType.DMA((n,))],
        name="share_halves")(*grads)


def _allreduce_small(vec, loss_row):
    rows = vec.shape[0]

    def body(v_ref, o_ref, slots, send_sems, recv_sems):
        x, y, c, _ = _place()
        me = 4 * x + 2 * y + c
        slots[me] = v_ref[...]
        copies = []
        for k in range(1, N_DEV):
            flip = lambda v, bit: 1 - v if bit else v
            peer = (flip(x, k & 4), flip(y, k & 2), flip(c, k & 1))
            copies.append(_remote(v_ref, slots.at[me], send_sems.at[k - 1], recv_sems.at[k - 1], peer))
        for cp in copies:
            cp.start()
        for k in range(1, N_DEV):
            flip = lambda v, bit: 1 - v if bit else v
            peer_id = 4 * flip(x, k & 4) + 2 * flip(y, k & 2) + flip(c, k & 1)
            _remote(v_ref, slots.at[peer_id], send_sems.at[k - 1], recv_sems.at[k - 1], (x, y, c)).wait_recv()
        for cp in copies:
            cp.wait_send()
        total = slots[0]
        for dev in range(1, N_DEV):
            total = total + slots[dev]
        o_ref[...] = total
        o_ref[loss_row:loss_row + 1, :] = jnp.broadcast_to(
            jnp.sum(total[loss_row:loss_row + 1, :], axis=-1, keepdims=True), (1, LANES))

    return pl.pallas_call(
        body, in_specs=[WHOLE_VMEM], out_specs=WHOLE_VMEM, out_shape=jax.ShapeDtypeStruct((rows, LANES), F32),
        scratch_shapes=[pltpu.VMEM((N_DEV, rows, LANES), F32), pltpu.SemaphoreType.DMA((N_DEV - 1,)),
                        pltpu.SemaphoreType.DMA((N_DEV - 1,))],
        name="allreduce_small")(vec)


def _adamw(w, g, m, v, *, tr):
    depth, rows, cols = w.shape
    assert rows % tr == 0
    c1 = float(np.float32(1.0 - ADAM_B1 ** ADAM_STEP))
    c2 = float(np.float32(1.0 - ADAM_B2 ** ADAM_STEP))

    def body(w_ref, g_ref, m_ref, v_ref, d_ref, mo_ref, vo_ref):
        g_t = g_ref[...]
        m_new = ADAM_B1 * m_ref[...] + (1.0 - ADAM_B1) * g_t
        v_new = ADAM_B2 * v_ref[...] + (1.0 - ADAM_B2) * (g_t * g_t)
        mo_ref[...] = m_new
        vo_ref[...] = v_new
        d_ref[...] = -ADAM_LR * ((m_new / c1) / (jnp.sqrt(v_new / c2) + ADAM_EPS) + ADAM_WD * w_ref[...])

    blk = pl.BlockSpec((None, tr, cols), lambda l, i: (l, i, 0))
    return pl.pallas_call(
        body, grid=(depth, rows // tr), in_specs=[blk] * 4, out_specs=[blk] * 3,
        out_shape=[jax.ShapeDtypeStruct(w.shape, F32)] * 3, name="adamw")(w, g, m, v)


def _local_step(x, positions, target, gains, weights):
    t = x.shape[0]
    tm = 512
    inv_freq = ROPE_THETA ** (-jnp.arange(0, ROPE_DIM, 2, dtype=F32) / ROPE_DIM)
    lane = np.arange(LANES) % HEAD_DIM
    freq_row = jnp.where(lane < ROPE_DIM, inv_freq[lane % (ROPE_DIM // 2)], 0.0).astype(F32)[None, :]
    cos, sin = _rope_tables(positions.reshape(t, 1), freq_row)

    saved = []
    h1 = _norm_fwd(x, gains["pre_mix_norm"], 0, tm=tm)
    for l in range(DEPTH):
        proj = _mm_nn(h1, weights["w_in"], l, tm=1024, name="in_proj")
        attn, lse = _attn_fwd(proj, cos, sin)
        conv_y = _conv_fwd(proj, weights["conv_w"], l)
        merged = _merge_fwd(attn, conv_y, gains["attn_out_norm"], gains["conv_out_norm"], l, tm=tm)
        mix = _mm_nn(merged, weights["w_out"], l, tm=1024, name="out_proj")
        x1, h2 = _resnorm_fwd(x, mix, gains["post_mix_norm"], l, gains["pre_ffn_norm"], l, tm=tm)
        gu = _mm_nn(h2, weights["w_gate_up"], l, tm=1024, name="gate_up")
        act = _swiglu_fwd(gu, tm=tm)
        f = _mm_nn(act, weights["w_down"], l, tm=1024, name="down")
        nxt = (gains["pre_mix_norm"], l + 1) if l + 1 < DEPTH else (None, None)
        x2, h1_next = _resnorm_fwd(x1, f, gains["post_ffn_norm"], l, *nxt, tm=tm)
        saved.append(dict(x=x, h1=h1, proj=proj, attn=attn, lse=lse, conv_y=conv_y, merged=merged, mix=mix,
                          x1=x1, h2=h2, gu=gu, act=act, f=f))
        x, h1 = x2, h1_next

    dres, loss_lanes = _loss_fwd_bwd(x, target, tm=tm)

    gw = {k: lax.empty(weights[k].shape, F32) for k in ("w_in", "w_out", "w_gate_up", "w_down")}
    g_gain = {k: [None] * DEPTH for k in gains}
    g_conv = [None] * DEPTH
    _, df, _, g_gain["post_ffn_norm"][DEPTH - 1] = _norm_bwd(
        dres, None, (saved[-1]["f"], gains["post_ffn_norm"], DEPTH - 1), tm=tm)
    for l in reversed(range(DEPTH)):
        sv = saved[l]
        d_act = _mm_nt(df, weights["w_down"], l, tm=1024, tko=FFN // 2, name="down_dx")
        gw["w_down"] = _mm_tn(sv["act"], df, gw["w_down"], l, tka=256, name="down_dw")
        dgu = _swiglu_bwd(sv["gu"], d_act, tm=tm)
        dh2 = _mm_nt(dgu, weights["w_gate_up"], l, tm=1024, tko=D_MODEL, name="gate_up_dx")
        gw["w_gate_up"] = _mm_tn(sv["h2"], dgu, gw["w_gate_up"], l, tka=512, name="gate_up_dw")
        dx1, dmix, g_gain["pre_ffn_norm"][l], g_gain["post_mix_norm"][l] = _norm_bwd(
            dres, (dh2, sv["x1"], gains["pre_ffn_norm"], l), (sv["mix"], gains["post_mix_norm"], l), tm=tm)
        d_merged = _mm_nt(dmix, weights["w_out"], l, tm=1024, tko=D_MODEL, name="out_proj_dx")
        gw["w_out"] = _mm_tn(sv["merged"], dmix, gw["w_out"], l, tka=512, name="out_proj_dw")
        d_attn, delta, d_conv_y, g_gain["attn_out_norm"][l], g_gain["conv_out_norm"][l] = _merge_bwd(
            d_merged, sv["attn"], sv["conv_y"], gains["attn_out_norm"], gains["conv_out_norm"], l, tm=tm)
        dq, dk, dv = _attn_bwd(sv["proj"], cos, sin, d_attn, sv["lse"], delta)
        du, dgb, dgc, g_conv[l] = _conv_bwd(sv["proj"], weights["conv_w"], l, d_conv_y)
        d_proj = jnp.concatenate([dq, dk, dv, du, dgb, dgc], axis=1)
        dh1 = _mm_nt(d_proj, weights["w_in"], l, tm=1024, tko=D_MODEL, name="in_proj_dx")
        gw["w_in"] = _mm_tn(sv["h1"], d_proj, gw["w_in"], l, tka=512, name="in_proj_dw")
        below = (saved[l - 1]["f"], gains["post_ffn_norm"], l - 1) if l > 0 else None
        dres, df, g_gain["pre_mix_norm"][l], g_below = _norm_bwd(
            dx1, (dh1, sv["x"], gains["pre_mix_norm"], l), below, tm=tm)
        if l > 0:
            g_gain["post_ffn_norm"][l - 1] = g_below

    g_gain = {k: jnp.concatenate(v, axis=0) for k, v in g_gain.items()}
    return loss_lanes, dres, gw, g_gain, jnp.stack(g_conv, axis=0)


GAIN_NAMES = ("pre_mix_norm", "attn_out_norm", "conv_out_norm", "post_mix_norm", "pre_ffn_norm", "post_ffn_norm")
MATRIX_NAMES = ("w_in", "w_out", "w_gate_up", "w_down")
WEIGHT_ORDER = ("pre_mix_norm", "w_in", "conv_w", "attn_out_norm", "conv_out_norm", "w_out", "post_mix_norm",
                "pre_ffn_norm", "w_gate_up", "w_down", "post_ffn_norm")


def kernel(x, positions, pre_mix_norm, w_in, conv_w, attn_out_norm, conv_out_norm, w_out, post_mix_norm, pre_ffn_norm, w_gate_up, w_down, post_ffn_norm, loss_target, m_pre_mix_norm, m_w_in, m_conv_w, m_attn_out_norm, m_conv_out_norm, m_w_out, m_post_mix_norm, m_pre_ffn_norm, m_w_gate_up, m_w_down, m_post_ffn_norm, v_pre_mix_norm, v_w_in, v_conv_w, v_attn_out_norm, v_conv_out_norm, v_w_out, v_post_mix_norm, v_pre_ffn_norm, v_w_gate_up, v_w_down, v_post_ffn_norm):
    params = dict(pre_mix_norm=pre_mix_norm, w_in=w_in, conv_w=conv_w, attn_out_norm=attn_out_norm,
                  conv_out_norm=conv_out_norm, w_out=w_out, post_mix_norm=post_mix_norm, pre_ffn_norm=pre_ffn_norm,
                  w_gate_up=w_gate_up, w_down=w_down, post_ffn_norm=post_ffn_norm)
    mom1 = dict(pre_mix_norm=m_pre_mix_norm, w_in=m_w_in, conv_w=m_conv_w, attn_out_norm=m_attn_out_norm,
                conv_out_norm=m_conv_out_norm, w_out=m_w_out, post_mix_norm=m_post_mix_norm,
                pre_ffn_norm=m_pre_ffn_norm, w_gate_up=m_w_gate_up, w_down=m_w_down, post_ffn_norm=m_post_ffn_norm)
    mom2 = dict(pre_mix_norm=v_pre_mix_norm, w_in=v_w_in, conv_w=v_conv_w, attn_out_norm=v_attn_out_norm,
                conv_out_norm=v_conv_out_norm, w_out=v_w_out, post_mix_norm=v_post_mix_norm,
                pre_ffn_norm=v_pre_ffn_norm, w_gate_up=v_w_gate_up, w_down=v_w_down, post_ffn_norm=v_post_ffn_norm)
    xi, yi, ci = lax.axis_index("x"), lax.axis_index("y"), lax.axis_index("c")
    place = jnp.stack([2 * xi + yi, ci]).astype(jnp.int32)

    gathered = _allgather_weights([_own_shard_slab(params[k], place, BF16) for k in MATRIX_NAMES]
                                  + [_own_shard_slab(conv_w, place, F32)])
    weights = dict(zip(MATRIX_NAMES + ("conv_w",), gathered))
    weights["w_out"] = weights["w_out"].reshape(DEPTH, 1, D_MODEL, D_MODEL)
    weights["w_down"] = weights["w_down"].reshape(DEPTH, 1, FFN, D_MODEL)
    gains = {k: params[k][:, None, :] for k in GAIN_NAMES}

    loss_lanes, grad_x, gw, g_gain, g_conv = _local_step(x[0], positions[0], loss_target[0], gains, weights)
    gw["w_out"] = gw["w_out"].reshape(DEPTH, N_CHIPS, D_MODEL // N_CHIPS, D_MODEL)
    gw["w_down"] = gw["w_down"].reshape(DEPTH, N_CHIPS, FFN // N_CHIPS, D_MODEL)

    grads = [gw[k] for k in MATRIX_NAMES]
    got = _exchange_halves(grads)
    partials = [_add_halves(g, r, place) for g, r in zip(grads, got)]
    others = _exchange_partials(partials)
    halves = [_sum_partials(p, q, place) for p, q in zip(partials, others)]
    grad = dict(zip(MATRIX_NAMES, _share_halves(halves)))

    small = [g_gain[k].reshape(-1) for k in GAIN_NAMES] + [g_conv.reshape(-1), loss_lanes.reshape(-1)]
    sizes = [int(s.shape[0]) for s in small]
    flat = jnp.concatenate(small)
    loss_row = (sum(sizes) - LANES) // LANES
    rows = -(-flat.shape[0] // (8 * LANES)) * 8
    flat = jnp.pad(flat, (0, rows * LANES - flat.shape[0])).reshape(rows, LANES)
    total = _allreduce_small(flat, loss_row).reshape(-1)
    offsets = np.cumsum([0] + sizes)
    for i, k in enumerate(GAIN_NAMES):
        grad[k] = total[offsets[i]:offsets[i + 1]].reshape(params[k].shape)
    conv_all = total[offsets[6]:offsets[7]].reshape(DEPTH, N_CHIPS, 3, LANES)
    grad["conv_w"] = lax.dynamic_index_in_dim(conv_all, 2 * xi + yi, axis=1, keepdims=False)
    loss = total[offsets[7]]

    delta, new_m, new_v = {}, {}, {}
    for k in WEIGHT_ORDER:
        shape = params[k].shape
        as3 = (lambda a: a) if len(shape) == 3 else (lambda a: a[:, None, :])
        rows_k = shape[1] if len(shape) == 3 else 1
        tr = {1024: 512, 704: 352, 256: 256}.get(rows_k, rows_k)
        d, m, v = _adamw(as3(params[k]), as3(grad[k]), as3(mom1[k]), as3(mom2[k]), tr=tr)
        delta[k], new_m[k], new_v[k] = d.reshape(shape), m.reshape(shape), v.reshape(shape)

    return (loss, grad_x[None], *[grad[k] for k in WEIGHT_ORDER], *[delta[k] for k in WEIGHT_ORDER],
            *[new_m[k] for k in WEIGHT_ORDER], *[new_v[k] for k in WEIGHT_ORDER])
```

```python
import functools
from typing import Callable, NamedTuple

import numpy as np
import jax
import jax.numpy as jnp
from jax import lax
from jax.experimental import pallas as pl
from jax.experimental.pallas import tpu as pltpu

F32 = jnp.float32
BF16 = jnp.bfloat16
MESH = pl.DeviceIdType.MESH

D_MODEL = 1024
ATTN_W = 512
CONV_W = 512
HEAD_DIM = 64
ROPE_DIM = 16
ROPE_THETA = 500000.0
FFN = 2816
DEPTH = 4
RMS_EPS = 1e-6
NEG_INF = -1e30
N_CHIPS = 4
N_DEV = 8
LANES = 128
BF16_ROWS = 16
DILATIONS = (1, 4, 16)
BAND = 64
TQ = 128
WIN = TQ + 2 * BAND
SCALE = HEAD_DIM ** -0.5

ADAM_LR = 0.001
ADAM_B1 = 0.9
ADAM_B2 = 0.999
ADAM_EPS = 1e-08
ADAM_WD = 0.01
ADAM_STEP = 10

GAIN_NAMES = ("pre_mix_norm", "attn_out_norm", "conv_out_norm", "post_mix_norm", "pre_ffn_norm", "post_ffn_norm")
MATRIX_NAMES = ("w_in", "w_out", "w_gate_up", "w_down")
WEIGHT_ORDER = ("pre_mix_norm", "w_in", "conv_w", "attn_out_norm", "conv_out_norm", "w_out", "post_mix_norm",
                "pre_ffn_norm", "w_gate_up", "w_down", "post_ffn_norm")

ANY = pl.BlockSpec(memory_space=pl.ANY)
WHOLE_VMEM = pl.BlockSpec(memory_space=pltpu.VMEM)


def _const_spec(block, index):
    return pl.BlockSpec(block, lambda *_: index)


def _gain_spec(g3, l):
    return _const_spec((None, 1, g3.shape[-1]), (l, 0, 0))


class _Comm(NamedTuple):
    ins: tuple
    inouts: tuple
    out_shapes: tuple
    n_sems: int
    start: Callable
    finish: Callable


def _place():
    x, y, c = lax.axis_index("x"), lax.axis_index("y"), lax.axis_index("c")
    other_chips = [(1 - x, y), (x, 1 - y), (1 - x, 1 - y)]
    return x, y, c, other_chips


def _remote(src, dst, send_sem, recv_sem, to):
    return pltpu.make_async_remote_copy(src_ref=src, dst_ref=dst, send_sem=send_sem, recv_sem=recv_sem,
                                        device_id=to, device_id_type=MESH)


def _call(body, operands, *, name, grid, in_specs, out_specs, out_shape, scratch_shapes=(), comm=None):
    in_specs, out_specs, out_shape = list(in_specs), list(out_specs), list(out_shape)
    scratch_shapes = list(scratch_shapes)
    if comm is None:
        out = pl.pallas_call(body, grid=grid, in_specs=in_specs, out_specs=out_specs, out_shape=out_shape,
                             scratch_shapes=scratch_shapes, name=name)(*operands)
        return list(out), None
    n_in, n_out, n_scr = len(in_specs), len(out_shape), len(scratch_shapes)
    n_ci, n_cio, n_co = len(comm.ins), len(comm.inouts), len(comm.out_shapes)

    def hosted(*refs):
        refs = list(refs)
        ins, c_ins = refs[:n_in], refs[n_in:n_in + n_ci]
        base = n_in + n_ci + n_cio
        outs = refs[base:base + n_out]
        c_io = refs[base + n_out:base + n_out + n_cio]
        c_out = refs[base + n_out + n_cio:base + n_out + n_cio + n_co]
        scr = refs[base + n_out + n_cio + n_co:]
        send_sems, recv_sems = scr[n_scr], scr[n_scr + 1]
        if grid:
            first = functools.reduce(jnp.logical_and, [pl.program_id(a) == 0 for a in range(len(grid))])
            last = functools.reduce(jnp.logical_and, [pl.program_id(a) == grid[a] - 1 for a in range(len(grid))])
            pl.when(first)(lambda: comm.start(c_ins, c_io, c_out, send_sems, recv_sems))
            body(*ins, *outs, *scr[:n_scr])
            pl.when(last)(lambda: comm.finish(c_ins, c_io, c_out, send_sems, recv_sems))
        else:
            comm.start(c_ins, c_io, c_out, send_sems, recv_sems)
            body(*ins, *outs, *scr[:n_scr])
            comm.finish(c_ins, c_io, c_out, send_sems, recv_sems)

    res = pl.pallas_call(
        hosted, grid=grid, in_specs=in_specs + [ANY] * (n_ci + n_cio), out_specs=out_specs + [ANY] * (n_cio + n_co),
        out_shape=out_shape + [jax.ShapeDtypeStruct(a.shape, a.dtype) for a in comm.inouts] + list(comm.out_shapes),
        input_output_aliases={n_in + n_ci + i: n_out + i for i in range(n_cio)},
        scratch_shapes=scratch_shapes + [pltpu.SemaphoreType.DMA((comm.n_sems,))] * 2,
        name=name)(*operands, *comm.ins, *comm.inouts)
    return list(res[:n_out]), list(res[n_out:])


def _run_comm(comm, name):
    return _call(lambda: None, [], name=name, grid=(), in_specs=[], out_specs=[], out_shape=[], comm=comm)[1]


def _row_half(ref, lead, core, rows, align):
    hr = rows // 2
    return ref.at[(*lead, pl.ds(pl.multiple_of(core * hr, align), hr), slice(None))]


def _gather_comm(slabs, conv_slab=None):
    n = len(slabs)
    n_conv = 0 if conv_slab is None else 3

    def direct(ios, send, recv):
        x, y, c, chips = _place()
        copies = []
        for a in range(n):
            own = _row_half(ios[a], (2 * x + y,), c, slabs[a].shape[1], BF16_ROWS)
            copies += [_remote(own, own, send.at[a * 3 + j], recv.at[a * 3 + j], (*chip, c))
                       for j, chip in enumerate(chips)]
        if conv_slab is not None:
            own = ios[n].at[:, 2 * x + y]
            copies += [_remote(own, own, send.at[6 * n + j], recv.at[6 * n + j], (*chip, c))
                       for j, chip in enumerate(chips)]
        return copies

    def start(ins, ios, outs, send, recv):
        for cp in direct(ios, send, recv):
            cp.start()

    def finish(ins, ios, outs, send, recv):
        x, y, c, chips = _place()
        sibling = (x, y, 1 - c)
        passed = []
        for a in range(n):
            for j, chip in enumerate(chips):
                landed = _row_half(ios[a], (2 * chip[0] + chip[1],), c, slabs[a].shape[1], BF16_ROWS)
                _remote(landed, landed, send.at[a * 3 + j], recv.at[a * 3 + j], (*chip, c)).wait_recv()
                fwd = _remote(landed, landed, send.at[3 * n + a * 3 + j], recv.at[3 * n + a * 3 + j], sibling)
                fwd.start()
                passed.append(fwd)
        if conv_slab is not None:
            for j, chip in enumerate(chips):
                landed = ios[n].at[:, 2 * chip[0] + chip[1]]
                _remote(landed, landed, send.at[6 * n + j], recv.at[6 * n + j], (*chip, c)).wait_recv()
        for a in range(n):
            for j, chip in enumerate(chips):
                landed = _row_half(ios[a], (2 * chip[0] + chip[1],), 1 - c, slabs[a].shape[1], BF16_ROWS)
                _remote(landed, landed, send.at[3 * n + a * 3 + j], recv.at[3 * n + a * 3 + j], sibling).wait_recv()
        for cp in direct(ios, send, recv) + passed:
            cp.wait_send()

    inouts = tuple(slabs) + (() if conv_slab is None else (conv_slab,))
    return _Comm((), inouts, (), 6 * n + n_conv, start, finish)


def _halves_comm(grads):
    n = len(grads)

    def copies(ins, outs, send, recv):
        x, y, c, _ = _place()
        return [_remote(_row_half(ins[a], (slice(None),), 1 - c, grads[a].shape[1], 8), outs[a],
                        send.at[a], recv.at[a], (x, y, 1 - c)) for a in range(n)]

    def start(ins, ios, outs, send, recv):
        for cp in copies(ins, outs, send, recv):
            cp.start()

    def finish(ins, ios, outs, send, recv):
        for cp in copies(ins, outs, send, recv):
            cp.wait()

    out_shapes = tuple(jax.ShapeDtypeStruct((g.shape[0], g.shape[1] // 2, g.shape[2]), F32) for g in grads)
    return _Comm(tuple(grads), (), out_shapes, n, start, finish)


def _partials_comm(partials):
    n = len(partials)

    def copies(ins, outs, send, recv):
        x, y, c, chips = _place()
        return [_remote(ins[a].at[2 * chip[0] + chip[1]], outs[a].at[k], send.at[a * 3 + k], recv.at[a * 3 + k],
                        (*chip, c)) for a in range(n) for k, chip in enumerate(chips)]

    def start(ins, ios, outs, send, recv):
        for cp in copies(ins, outs, send, recv):
            cp.start()

    def finish(ins, ios, outs, send, recv):
        for cp in copies(ins, outs, send, recv):
            cp.wait()

    out_shapes = tuple(jax.ShapeDtypeStruct((3,) + p.shape[1:], BF16) for p in partials)
    return _Comm(tuple(partials), (), out_shapes, 3 * n, start, finish)


def _share_comm(grads, l):
    n = len(grads)

    def start(ins, ios, outs, send, recv):
        x, y, c, _ = _place()
        for a in range(n):
            mine = _row_half(ios[a], (l,), c, grads[a].shape[1], 8)
            _remote(mine, mine, send.at[a], recv.at[a], (x, y, 1 - c)).start()

    def finish(ins, ios, outs, send, recv):
        x, y, c, _ = _place()
        for a in range(n):
            theirs = _row_half(ios[a], (l,), 1 - c, grads[a].shape[1], 8)
            _remote(theirs, theirs, send.at[a], recv.at[a], (x, y, 1 - c)).wait()

    return _Comm((), tuple(grads), (), n, start, finish)


def _rms_fwd(x, g):
    r = lax.rsqrt(jnp.mean(x * x, axis=-1, keepdims=True) + RMS_EPS)
    return (x * r) * g


def _rms_bwd(x, g, dy):
    r = lax.rsqrt(jnp.mean(x * x, axis=-1, keepdims=True) + RMS_EPS)
    xh = x * r
    u = dy * g
    dx = r * (u - xh * jnp.mean(xh * u, axis=-1, keepdims=True))
    return dx, jnp.sum(dy * xh, axis=0, keepdims=True)


def _accumulate(ref, value, first):
    @pl.when(first)
    def _():
        ref[...] = value

    @pl.when(jnp.logical_not(first))
    def _():
        ref[...] += value


def _rope_coeffs(cos, sin):
    m = lax.broadcasted_iota(jnp.int32, cos.shape, 1) % HEAD_DIM
    a = jnp.where(m < ROPE_DIM, cos, 1.0)
    b = jnp.where(m < ROPE_DIM // 2, -sin, 0.0)
    c = jnp.where((m >= ROPE_DIM // 2) & (m < ROPE_DIM), sin, 0.0)
    return a, b, c


def _rope_apply(t, cos, sin):
    a, b, c = _rope_coeffs(cos, sin)
    n = t.shape[1]
    return a * t + b * pltpu.roll(t, n - ROPE_DIM // 2, 1) + c * pltpu.roll(t, ROPE_DIM // 2, 1)


def _rope_transpose(dt, cos, sin):
    a, b, c = _rope_coeffs(cos, sin)
    n = dt.shape[1]
    return a * dt + pltpu.roll(b * dt, ROPE_DIM // 2, 1) + pltpu.roll(c * dt, n - ROPE_DIM // 2, 1)


def _mm_nn(a, w, *, tm, name, comm=None):
    t, k = a.shape
    s_n, k2, n = w.shape
    assert k == k2 and t % tm == 0

    def body(a_ref, w_ref, o_ref):
        o_ref[...] = jnp.dot(a_ref[...], w_ref[...], preferred_element_type=F32)

    out, got = _call(
        body, [a, w], name=name, grid=(t // tm, s_n),
        in_specs=[pl.BlockSpec((tm, k), lambda i, s: (i, 0)), pl.BlockSpec((None, k, n), lambda i, s: (s, 0, 0))],
        out_specs=[pl.BlockSpec((tm, n), lambda i, s: (i, s))],
        out_shape=[jax.ShapeDtypeStruct((t, s_n * n), F32)], comm=comm)
    return out[0], got


def _mm_nt(a, w, *, tm, tko, name, comm=None):
    t, sn = a.shape
    s_n, ko, n = w.shape
    assert sn == s_n * n and t % tm == 0 and ko % tko == 0

    def body(a_ref, w_ref, o_ref):
        acc = lax.dot_general(a_ref[...], w_ref[...], (((1,), (1,)), ((), ())), preferred_element_type=F32)
        if s_n == 1:
            o_ref[...] = acc
        else:
            _accumulate(o_ref, acc, pl.program_id(2) == 0)

    out, got = _call(
        body, [a, w], name=name, grid=(t // tm, ko // tko, s_n),
        in_specs=[pl.BlockSpec((tm, n), lambda i, j, s: (i, s)),
                  pl.BlockSpec((None, tko, n), lambda i, j, s: (s, j, 0))],
        out_specs=[pl.BlockSpec((tm, tko), lambda i, j, s: (i, j))],
        out_shape=[jax.ShapeDtypeStruct((t, ko), F32)], comm=comm)
    return out[0], got


def _mm_tn(a, b, s_n, *, tka, name):
    t, ka = a.shape
    n = b.shape[1] // s_n
    assert b.shape[0] == t and ka % tka == 0

    def body(a_ref, b_ref, o_ref):
        o_ref[...] = lax.dot_general(a_ref[...], b_ref[...], (((0,), (0,)), ((), ())), preferred_element_type=F32)

    return pl.pallas_call(
        body, grid=(ka // tka, s_n),
        in_specs=[pl.BlockSpec((t, tka), lambda i, s: (0, i)), pl.BlockSpec((t, n), lambda i, s: (0, s))],
        out_specs=pl.BlockSpec((None, tka, n), lambda i, s: (s, i, 0)),
        out_shape=jax.ShapeDtypeStruct((s_n, ka, n), F32), name=name)(a, b)


def _rope_tables(positions_col, inv_freq_row):
    t = positions_col.shape[0]

    def body(pos_ref, f_ref, cos_ref, sin_ref):
        ang = pos_ref[...].astype(F32) * f_ref[...]
        cos_ref[...] = jnp.cos(ang)
        sin_ref[...] = jnp.sin(ang)

    return pl.pallas_call(
        body, out_shape=[jax.ShapeDtypeStruct((t, LANES), F32)] * 2, name="rope_tables")(positions_col, inv_freq_row)


def _norm_fwd(x, g3, l, *, tm):
    t, w = x.shape

    def body(x_ref, g_ref, h_ref):
        h_ref[...] = _rms_fwd(x_ref[...], g_ref[...]).astype(BF16)

    return pl.pallas_call(
        body, grid=(t // tm,),
        in_specs=[pl.BlockSpec((tm, w), lambda i: (i, 0)), _gain_spec(g3, l)],
        out_specs=pl.BlockSpec((tm, w), lambda i: (i, 0)),
        out_shape=jax.ShapeDtypeStruct((t, w), BF16), name="norm_fwd")(x, g3)


def _resnorm_fwd(x, y, g_post3, l_post, g_next3, l_next, *, tm):
    t, w = x.shape
    with_next = g_next3 is not None
    row = pl.BlockSpec((tm, w), lambda i: (i, 0))

    def body(x_ref, y_ref, gp_ref, *rest):
        x_new = x_ref[...] + _rms_fwd(y_ref[...], gp_ref[...])
        if with_next:
            gn_ref, xo_ref, h_ref = rest
            h_ref[...] = _rms_fwd(x_new, gn_ref[...]).astype(BF16)
        else:
            (xo_ref,) = rest
        xo_ref[...] = x_new

    ins = [x, y, g_post3] + ([g_next3] if with_next else [])
    in_specs = [row, row, _gain_spec(g_post3, l_post)] + ([_gain_spec(g_next3, l_next)] if with_next else [])
    out_shape = [jax.ShapeDtypeStruct((t, w), F32)] + ([jax.ShapeDtypeStruct((t, w), BF16)] if with_next else [])
    out = pl.pallas_call(body, grid=(t // tm,), in_specs=in_specs, out_specs=[row] * len(out_shape),
                         out_shape=out_shape, name="resnorm_fwd")(*ins)
    return (out[0], out[1]) if with_next else (out[0], None)


def _conv_fwd(proj, conv_w, l):
    t = proj.shape[0]
    col0 = 3 * ATTN_W // LANES

    def body(u_ref, gb_ref, gc_ref, w_ref, y_ref):
        c = gc_ref[...] * u_ref[...]
        row = lax.broadcasted_iota(jnp.int32, c.shape, 0)
        c_prev = jnp.where(row == 0, 0.0, pltpu.roll(c, 1, 0))
        c_next = jnp.where(row == t - 1, 0.0, pltpu.roll(c, t - 1, 0))
        w = w_ref[...]
        y_ref[...] = gb_ref[...] * (w[0:1] * c_prev + w[1:2] * c + w[2:3] * c_next)

    nj = CONV_W // LANES
    cols = lambda base: pl.BlockSpec((t, LANES), lambda j: (0, base + j))
    return pl.pallas_call(
        body, grid=(nj,),
        in_specs=[cols(col0), cols(col0 + nj), cols(col0 + 2 * nj),
                  pl.BlockSpec((None, None, 3, LANES), lambda j: (l, j, 0, 0))],
        out_specs=pl.BlockSpec((t, LANES), lambda j: (0, j)),
        out_shape=jax.ShapeDtypeStruct((t, CONV_W), F32), name="conv_fwd")(proj, proj, proj, conv_w)


def _merge_fwd(attn, conv_y, ga3, gc3, l, *, tm):
    t = attn.shape[0]
    row = pl.BlockSpec((tm, ATTN_W), lambda i: (i, 0))

    def body(a_ref, c_ref, ga_ref, gc_ref, m_ref):
        m_ref[:, :ATTN_W] = _rms_fwd(a_ref[...], ga_ref[...]).astype(BF16)
        m_ref[:, ATTN_W:] = _rms_fwd(c_ref[...], gc_ref[...]).astype(BF16)

    return pl.pallas_call(
        body, grid=(t // tm,),
        in_specs=[row, row, _gain_spec(ga3, l), _gain_spec(gc3, l)],
        out_specs=pl.BlockSpec((tm, D_MODEL), lambda i: (i, 0)),
        out_shape=jax.ShapeDtypeStruct((t, D_MODEL), BF16), name="merge_fwd")(attn, conv_y, ga3, gc3)


def _swiglu_fwd(gu, *, tm):
    t = gu.shape[0]

    def body(g_ref, u_ref, a_ref):
        g = g_ref[...]
        a_ref[...] = (g * jax.nn.sigmoid(g) * u_ref[...]).astype(BF16)

    return pl.pallas_call(
        body, grid=(t // tm,),
        in_specs=[pl.BlockSpec((tm, FFN), lambda i: (i, 0)), pl.BlockSpec((tm, FFN), lambda i: (i, 1))],
        out_specs=pl.BlockSpec((tm, FFN), lambda i: (i, 0)),
        out_shape=jax.ShapeDtypeStruct((t, FFN), BF16), name="swiglu_fwd")(gu, gu)


def _loss_fwd_bwd(y, target, *, tm):
    t, w = y.shape
    row = pl.BlockSpec((tm, w), lambda i: (i, 0))

    def body(y_ref, t_ref, dy_ref, loss_ref):
        e = y_ref[...] - t_ref[...]
        dy_ref[...] = e * (1.0 / w)
        sq = jnp.sum(e * e, axis=0, keepdims=True) * (0.5 / w)
        part = sq[:, :LANES]
        for j in range(1, w // LANES):
            part = part + sq[:, j * LANES:(j + 1) * LANES]
        _accumulate(loss_ref, part, pl.program_id(0) == 0)

    return pl.pallas_call(
        body, grid=(t // tm,), in_specs=[row, row],
        out_specs=[row, _const_spec((1, LANES), (0, 0))],
        out_shape=[jax.ShapeDtypeStruct((t, w), F32), jax.ShapeDtypeStruct((1, LANES), F32)], name="loss")(y, target)


def _tile_rows(t, nt, lb, d):
    r = t // nt
    q0 = (t % nt) * TQ
    m0 = jnp.clip(q0 - BAND, 0, lb - WIN)
    if d == 1:
        return pl.ds(pl.multiple_of(q0, TQ), TQ), pl.ds(pl.multiple_of(m0, BAND), WIN), m0 - q0
    return pl.ds(r + d * q0, TQ, stride=d), pl.ds(r + d * m0, WIN, stride=d), m0 - q0


def _for_row_chunks(t, fn, chunk=512):
    def step(i, carry):
        fn(pl.ds(pl.multiple_of(i * chunk, chunk), chunk))
        return carry

    lax.fori_loop(0, t // chunk, step, 0)


def _rope_into(dst_ref, src_ref, cos_ref, sin_ref, t, scale=1.0):
    def chunk(rows):
        dst_ref[rows, :] = _rope_apply(src_ref[rows, :], cos_ref[rows, :], sin_ref[rows, :]) * scale

    _for_row_chunks(t, chunk)


WINDOW_OFFSETS = (-BAND, 0, -2 * BAND)
LANE_CONTRACT = (((1,), (1,)), ((), ()))
ROW_CONTRACT = (((0,), (0,)), ((), ()))


def _fill_band_bias(bias_ref):
    rel0 = (lax.broadcasted_iota(jnp.int32, (2 * TQ, WIN), 1)
            - lax.broadcasted_iota(jnp.int32, (2 * TQ, WIN), 0) % TQ)
    for j, off in enumerate(WINDOW_OFFSETS):
        rel = rel0 + off
        bias_ref[j] = jnp.where((rel >= -BAND) & (rel <= BAND), 0.0, NEG_INF)


def _band_bias(bias_ref, off):
    return bias_ref[jnp.where(off == WINDOW_OFFSETS[0], 0, jnp.where(off == WINDOW_OFFSETS[1], 1, 2))]


def _stack_heads(a, first_head):
    return jnp.concatenate([jnp.where(first_head, a, 0.0), jnp.where(first_head, 0.0, a)], axis=0)


def _unstack_heads(a2, first_head):
    return jnp.where(first_head, a2[:TQ], a2[TQ:])


def _attn_fwd(proj, cos, sin, comm=None):
    t = proj.shape[0]
    npair = ATTN_W // LANES

    def body(q_ref, k_ref, v_ref, cos_ref, sin_ref, o_ref, lse_ref, qs, ks, o1, o2, l0, l1, l2, m1, m2, bias):
        _rope_into(qs, q_ref, cos_ref, sin_ref, t, SCALE)
        _rope_into(ks, k_ref, cos_ref, sin_ref, t)
        _fill_band_bias(bias)
        outs, dens, maxs = (o_ref, o1, o2), (l0, l1, l2), (lse_ref, m1, m2)
        first_head = lax.broadcasted_iota(jnp.int32, (TQ, LANES), 1) < HEAD_DIM
        ones = jnp.ones((WIN, LANES), BF16)
        for b, d in enumerate(DILATIONS):
            lb = t // d
            nt = lb // TQ

            def tile(ti, carry, b=b, d=d, lb=lb, nt=nt):
                qrows, krows, off = _tile_rows(ti, nt, lb, d)
                q2 = _stack_heads(qs[qrows, :], first_head).astype(BF16)
                kw = ks[krows, :].astype(BF16)
                vw = jnp.concatenate([v_ref[krows, :].astype(BF16), ones], axis=1)
                s = lax.dot_general(q2, kw, LANE_CONTRACT, preferred_element_type=F32) + _band_bias(bias, off)
                m = jnp.max(s, axis=-1, keepdims=True)
                pv = jnp.dot(jnp.exp(s - m).astype(BF16), vw, preferred_element_type=F32)
                outs[b][qrows, :] = _unstack_heads(pv[:, :LANES], first_head)
                dens[b][qrows, :] = _unstack_heads(pv[:, LANES:], first_head)
                maxs[b][qrows, :] = _unstack_heads(jnp.broadcast_to(m, (2 * TQ, LANES)), first_head)
                return carry

            lax.fori_loop(0, d * nt, tile, 0, unroll=8)

        def finish(rows):
            ms = [m_b[rows, :] for m_b in maxs]
            m_all = jnp.maximum(jnp.maximum(ms[0], ms[1]), ms[2])
            ws = [jnp.exp(m_b - m_all) for m_b in ms]
            den = ws[0] * dens[0][rows, :] + ws[1] * dens[1][rows, :] + ws[2] * dens[2][rows, :]
            num = ws[0] * outs[0][rows, :] + ws[1] * outs[1][rows, :] + ws[2] * outs[2][rows, :]
            o_ref[rows, :] = num / den
            lse_ref[rows, :] = m_all + jnp.log(den)

        _for_row_chunks(t, finish, 256)

    cols = lambda base: pl.BlockSpec((t, LANES), lambda g: (0, base + g))
    out, got = _call(
        body, [proj, proj, proj, cos, sin], name="attn_fwd", grid=(npair,),
        in_specs=[cols(0), cols(npair), cols(2 * npair), WHOLE_VMEM, WHOLE_VMEM],
        out_specs=[cols(0), cols(0)],
        out_shape=[jax.ShapeDtypeStruct((t, ATTN_W), F32)] * 2,
        scratch_shapes=[pltpu.VMEM((t, LANES), F32)] * 9 + [pltpu.VMEM((len(WINDOW_OFFSETS), 2 * TQ, WIN), F32)],
        comm=comm)
    return out[0], out[1], got


def _attn_bwd(proj, cos, sin, d_attn, lse, delta, comm=None):
    t = proj.shape[0]
    npair = ATTN_W // LANES

    def body(q_ref, k_ref, v_ref, cos_ref, sin_ref, do_ref, l_ref, dl_ref, dq_ref, dk_ref, dv_ref,
             qs, ks, dq_acc, dk_acc, dv_acc, bias):
        _rope_into(qs, q_ref, cos_ref, sin_ref, t, SCALE)
        _rope_into(ks, k_ref, cos_ref, sin_ref, t)
        _fill_band_bias(bias)
        dq_acc[...] = jnp.zeros(dq_acc.shape, F32)
        dk_acc[...] = jnp.zeros(dk_acc.shape, F32)
        dv_acc[...] = jnp.zeros(dv_acc.shape, F32)
        first_head = lax.broadcasted_iota(jnp.int32, (TQ, LANES), 1) < HEAD_DIM

        def stack_column(a):
            return jnp.concatenate([a[:, 0:1], a[:, HEAD_DIM:HEAD_DIM + 1]], axis=0)

        for d in DILATIONS:
            lb = t // d
            nt = lb // TQ

            def tile(ti, carry, d=d, lb=lb, nt=nt):
                qrows, krows, off = _tile_rows(ti, nt, lb, d)
                q2 = _stack_heads(qs[qrows, :], first_head).astype(BF16)
                do2 = _stack_heads(do_ref[qrows, :], first_head).astype(BF16)
                kw = ks[krows, :].astype(BF16)
                vw = v_ref[krows, :].astype(BF16)
                s = lax.dot_general(q2, kw, LANE_CONTRACT, preferred_element_type=F32) + _band_bias(bias, off)
                p = jnp.exp(s - stack_column(l_ref[qrows, :]))
                dp = lax.dot_general(do2, vw, LANE_CONTRACT, preferred_element_type=F32)
                ds = (p * (dp - stack_column(dl_ref[qrows, :]))).astype(BF16)
                dq2 = jnp.dot(ds, kw, preferred_element_type=F32)
                dq_acc[qrows, :] += _unstack_heads(dq2, first_head) * SCALE
                dk_acc[krows, :] += lax.dot_general(ds, q2, ROW_CONTRACT, preferred_element_type=F32)
                dv_acc[krows, :] += lax.dot_general(p.astype(BF16), do2, ROW_CONTRACT, preferred_element_type=F32)
                return carry

            lax.fori_loop(0, d * nt, tile, 0, unroll=4)

        def finish(rows):
            dq_ref[rows, :] = _rope_transpose(dq_acc[rows, :], cos_ref[rows, :], sin_ref[rows, :]).astype(BF16)
            dk_ref[rows, :] = _rope_transpose(dk_acc[rows, :], cos_ref[rows, :], sin_ref[rows, :]).astype(BF16)
            dv_ref[rows, :] = dv_acc[rows, :].astype(BF16)

        _for_row_chunks(t, finish)

    cols = lambda base: pl.BlockSpec((t, LANES), lambda g: (0, base + g))
    out, got = _call(
        body, [proj, proj, proj, cos, sin, d_attn, lse, delta], name="attn_bwd", grid=(npair,),
        in_specs=[cols(0), cols(npair), cols(2 * npair), WHOLE_VMEM, WHOLE_VMEM, cols(0), cols(0), cols(0)],
        out_specs=[cols(0)] * 3,
        out_shape=[jax.ShapeDtypeStruct((t, ATTN_W), BF16)] * 3,
        scratch_shapes=[pltpu.VMEM((t, LANES), F32)] * 5 + [pltpu.VMEM((len(WINDOW_OFFSETS), 2 * TQ, WIN), F32)],
        comm=comm)
    return out[0], out[1], out[2], got


def _norm_bwd(dres, pre, post, *, tm):
    t, w = dres.shape
    row = pl.BlockSpec((tm, w), lambda i: (i, 0))
    gsum = _const_spec((1, w), (0, 0))
    ins, in_specs, out_shape, out_specs = [dres], [row], [], []
    if pre is not None:
        dh, x, g3, l = pre
        ins += [dh, x, g3]
        in_specs += [row, row, _gain_spec(g3, l)]
        out_shape += [jax.ShapeDtypeStruct((t, w), F32), jax.ShapeDtypeStruct((1, w), F32)]
        out_specs += [row, gsum]
    if post is not None:
        y, g3, l = post
        ins += [y, g3]
        in_specs += [row, _gain_spec(g3, l)]
        out_shape += [jax.ShapeDtypeStruct((t, w), BF16), jax.ShapeDtypeStruct((1, w), F32)]
        out_specs += [row, gsum]
    n_in = len(ins)

    def body(*refs):
        first = pl.program_id(0) == 0
        ins_r, outs_r = list(refs[:n_in]), list(refs[n_in:])
        d = ins_r.pop(0)[...]
        if pre is not None:
            dh_ref, x_ref, g_ref = ins_r[:3]
            ins_r = ins_r[3:]
            dx, dg = _rms_bwd(x_ref[...], g_ref[...], dh_ref[...])
            d = d + dx
            outs_r.pop(0)[...] = d
            _accumulate(outs_r.pop(0), dg, first)
        if post is not None:
            y_ref, g_ref = ins_r
            dy, dg = _rms_bwd(y_ref[...], g_ref[...], d)
            outs_r.pop(0)[...] = dy.astype(BF16)
            _accumulate(outs_r.pop(0), dg, first)

    out = list(pl.pallas_call(body, grid=(t // tm,), in_specs=in_specs, out_specs=out_specs,
                              out_shape=out_shape, name="norm_bwd")(*ins))
    d_new, dg_pre = (out.pop(0), out.pop(0)) if pre is not None else (None, None)
    dy, dg_post = (out.pop(0), out.pop(0)) if post is not None else (None, None)
    return d_new, dy, dg_pre, dg_post


def _swiglu_bwd(gu, da, *, tm):
    t = gu.shape[0]

    def body(g_ref, u_ref, da_ref, o_ref):
        g, u, d = g_ref[...], u_ref[...], da_ref[...]
        sig = jax.nn.sigmoid(g)
        o_ref[:, :FFN] = (d * u * (sig * (1.0 + g * (1.0 - sig)))).astype(BF16)
        o_ref[:, FFN:] = (d * (g * sig)).astype(BF16)

    half = lambda j: pl.BlockSpec((tm, FFN), lambda i: (i, j))
    return pl.pallas_call(
        body, grid=(t // tm,), in_specs=[half(0), half(1), half(0)],
        out_specs=pl.BlockSpec((tm, 2 * FFN), lambda i: (i, 0)),
        out_shape=jax.ShapeDtypeStruct((t, 2 * FFN), BF16), name="swiglu_bwd")(gu, gu, da)


def _merge_bwd(d_merged, attn, conv_y, ga3, gc3, l, *, tm):
    t = attn.shape[0]
    row = pl.BlockSpec((tm, ATTN_W), lambda i: (i, 0))
    gsum = _const_spec((1, ATTN_W), (0, 0))

    def body(dma_ref, dmc_ref, a_ref, c_ref, ga_ref, gc_ref, da_ref, dl_ref, dc_ref, dga_ref, dgc_ref):
        first = pl.program_id(0) == 0
        attn_t = a_ref[...]
        da, dga = _rms_bwd(attn_t, ga_ref[...], dma_ref[...])
        dc, dgc = _rms_bwd(c_ref[...], gc_ref[...], dmc_ref[...])
        da_ref[...] = da
        dc_ref[...] = dc
        same_head = (lax.broadcasted_iota(jnp.int32, (ATTN_W, ATTN_W), 0) // HEAD_DIM
                     == lax.broadcasted_iota(jnp.int32, (ATTN_W, ATTN_W), 1) // HEAD_DIM).astype(BF16)
        rest = da * attn_t
        total = jnp.zeros(rest.shape, F32)
        for _ in range(3):
            term = rest.astype(BF16)
            total = total + jnp.dot(term, same_head, preferred_element_type=F32)
            rest = rest - term.astype(F32)
        dl_ref[...] = total
        _accumulate(dga_ref, dga, first)
        _accumulate(dgc_ref, dgc, first)

    return pl.pallas_call(
        body, grid=(t // tm,),
        in_specs=[pl.BlockSpec((tm, ATTN_W), lambda i: (i, 0)), pl.BlockSpec((tm, CONV_W), lambda i: (i, 1)),
                  row, row, _gain_spec(ga3, l), _gain_spec(gc3, l)],
        out_specs=[row, row, row, gsum, gsum],
        out_shape=[jax.ShapeDtypeStruct((t, ATTN_W), F32)] * 3 + [jax.ShapeDtypeStruct((1, ATTN_W), F32)] * 2,
        name="merge_bwd")(d_merged, d_merged, attn, conv_y, ga3, gc3)


def _conv_bwd(proj, conv_w, l, d_conv_y):
    t = proj.shape[0]
    col0 = 3 * ATTN_W // LANES
    nj = CONV_W // LANES

    def body(u_ref, gb_ref, gc_ref, w_ref, dy_ref, du_ref, dgb_ref, dgc_ref, dw_ref):
        u, gc, dy = u_ref[...], gc_ref[...], dy_ref[...]
        row = lax.broadcasted_iota(jnp.int32, u.shape, 0)
        down = lambda a: jnp.where(row == 0, 0.0, pltpu.roll(a, 1, 0))
        up = lambda a: jnp.where(row == t - 1, 0.0, pltpu.roll(a, t - 1, 0))
        w = w_ref[...]
        c = gc * u
        c_prev, c_next = down(c), up(c)
        dgb_ref[...] = (dy * (w[0:1] * c_prev + w[1:2] * c + w[2:3] * c_next)).astype(BF16)
        dz = dy * gb_ref[...]
        dc = w[0:1] * up(dz) + w[1:2] * dz + w[2:3] * down(dz)
        du_ref[...] = (dc * gc).astype(BF16)
        dgc_ref[...] = (dc * u).astype(BF16)
        dw_ref[0:1, :] = jnp.sum(dz * c_prev, axis=0, keepdims=True)
        dw_ref[1:2, :] = jnp.sum(dz * c, axis=0, keepdims=True)
        dw_ref[2:3, :] = jnp.sum(dz * c_next, axis=0, keepdims=True)

    cols = lambda base: pl.BlockSpec((t, LANES), lambda j: (0, base + j))
    return pl.pallas_call(
        body, grid=(nj,),
        in_specs=[cols(col0), cols(col0 + nj), cols(col0 + 2 * nj),
                  pl.BlockSpec((None, None, 3, LANES), lambda j: (l, j, 0, 0)), cols(0)],
        out_specs=[cols(0)] * 3 + [pl.BlockSpec((None, 3, LANES), lambda j: (j, 0, 0))],
        out_shape=[jax.ShapeDtypeStruct((t, CONV_W), BF16)] * 3 + [jax.ShapeDtypeStruct((nj, 3, LANES), F32)],
        name="conv_bwd")(proj, proj, proj, conv_w, d_conv_y)


def _own_shard_slab(w, l, place, dtype):
    _, rows, cols = w.shape
    tr = rows if rows <= 704 else 512
    assert rows % tr == 0

    def body(p_ref, w_ref, o_ref):
        del p_ref
        o_ref[...] = w_ref[...].astype(dtype)

    grid_spec = pltpu.PrefetchScalarGridSpec(
        num_scalar_prefetch=1, grid=(rows // tr,),
        in_specs=[pl.BlockSpec((None, tr, cols), lambda i, p: (l, i, 0))],
        out_specs=pl.BlockSpec((None, tr, cols), lambda i, p: (p[0], i, 0)))
    return pl.pallas_call(body, grid_spec=grid_spec, name="own_shard_slab",
                          out_shape=jax.ShapeDtypeStruct((N_CHIPS, rows, cols), dtype))(place, w)


def _own_conv_slab(w, place):
    depth = w.shape[0]

    def body(p_ref, w_ref, o_ref):
        del p_ref
        o_ref[...] = w_ref[...]

    grid_spec = pltpu.PrefetchScalarGridSpec(
        num_scalar_prefetch=1, grid=(depth,),
        in_specs=[pl.BlockSpec((None, 3, LANES), lambda l, p: (l, 0, 0))],
        out_specs=pl.BlockSpec((None, None, 3, LANES), lambda l, p: (l, p[0], 0, 0)))
    return pl.pallas_call(body, grid_spec=grid_spec, name="own_conv_slab",
                          out_shape=jax.ShapeDtypeStruct((depth, N_CHIPS, 3, LANES), F32))(place, w)


def _add_halves(grad, got, place):
    s_n, rows, cols = grad.shape
    hr = rows // 2

    def body(p_ref, g_ref, r_ref, o_ref):
        del p_ref
        o_ref[...] = (g_ref[...] + r_ref[...]).astype(BF16)

    grid_spec = pltpu.PrefetchScalarGridSpec(
        num_scalar_prefetch=1, grid=(s_n,),
        in_specs=[pl.BlockSpec((None, hr, cols), lambda s, p: (s, p[1], 0)),
                  pl.BlockSpec((None, hr, cols), lambda s, p: (s, 0, 0))],
        out_specs=pl.BlockSpec((None, hr, cols), lambda s, p: (s, 0, 0)))
    return pl.pallas_call(body, grid_spec=grid_spec, out_shape=jax.ShapeDtypeStruct((s_n, hr, cols), BF16),
                          name="add_halves")(place, grad, got)


def _sum_partials(partial, got, place, acc, l):
    _, hr, cols = partial.shape

    def body(p_ref, mine_ref, got_ref, acc_ref, o_ref):
        del p_ref, acc_ref
        total = mine_ref[...].astype(F32)
        for k in range(3):
            total = total + got_ref[k].astype(F32)
        o_ref[...] = total

    grid_spec = pltpu.PrefetchScalarGridSpec(
        num_scalar_prefetch=1, grid=(1,),
        in_specs=[pl.BlockSpec((None, hr, cols), lambda i, p: (p[0], 0, 0)),
                  pl.BlockSpec((3, hr, cols), lambda i, p: (0, 0, 0)), ANY],
        out_specs=pl.BlockSpec((None, hr, cols), lambda i, p: (l, p[1], 0)))
    return pl.pallas_call(body, grid_spec=grid_spec, out_shape=jax.ShapeDtypeStruct(acc.shape, F32),
                          input_output_aliases={3: 0}, name="sum_partials")(place, partial, got, acc)


def _allreduce_small(vec, loss_row):
    rows = vec.shape[0]

    def body(v_ref, o_ref, slots, send_sems, recv_sems):
        x, y, c, _ = _place()
        me = 4 * x + 2 * y + c
        slots[me] = v_ref[...]
        copies = []
        for k in range(1, N_DEV):
            flip = lambda v, bit: 1 - v if bit else v
            peer = (flip(x, k & 4), flip(y, k & 2), flip(c, k & 1))
            copies.append(_remote(v_ref, slots.at[me], send_sems.at[k - 1], recv_sems.at[k - 1], peer))
        for cp in copies:
            cp.start()
        for k in range(1, N_DEV):
            flip = lambda v, bit: 1 - v if bit else v
            peer_id = 4 * flip(x, k & 4) + 2 * flip(y, k & 2) + flip(c, k & 1)
            _remote(v_ref, slots.at[peer_id], send_sems.at[k - 1], recv_sems.at[k - 1], (x, y, c)).wait_recv()
        for cp in copies:
            cp.wait_send()
        total = slots[0]
        for dev in range(1, N_DEV):
            total = total + slots[dev]
        o_ref[...] = total
        o_ref[loss_row:loss_row + 1, :] = jnp.broadcast_to(
            jnp.sum(total[loss_row:loss_row + 1, :], axis=-1, keepdims=True), (1, LANES))

    return pl.pallas_call(
        body, in_specs=[WHOLE_VMEM], out_specs=WHOLE_VMEM, out_shape=jax.ShapeDtypeStruct((rows, LANES), F32),
        scratch_shapes=[pltpu.VMEM((N_DEV, rows, LANES), F32), pltpu.SemaphoreType.DMA((N_DEV - 1,)),
                        pltpu.SemaphoreType.DMA((N_DEV - 1,))],
        name="allreduce_small")(vec)


def _adamw(w, g, m, v, *, tr):
    depth, rows, cols = w.shape
    assert rows % tr == 0
    c1 = float(np.float32(1.0 - ADAM_B1 ** ADAM_STEP))
    c2 = float(np.float32(1.0 - ADAM_B2 ** ADAM_STEP))

    def body(w_ref, g_ref, m_ref, v_ref, d_ref, mo_ref, vo_ref):
        g_t = g_ref[...]
        m_new = ADAM_B1 * m_ref[...] + (1.0 - ADAM_B1) * g_t
        v_new = ADAM_B2 * v_ref[...] + (1.0 - ADAM_B2) * (g_t * g_t)
        mo_ref[...] = m_new
        vo_ref[...] = v_new
        d_ref[...] = -ADAM_LR * ((m_new / c1) / (jnp.sqrt(v_new / c2) + ADAM_EPS) + ADAM_WD * w_ref[...])

    blk = pl.BlockSpec((None, tr, cols), lambda l, i: (l, i, 0))
    return pl.pallas_call(
        body, grid=(depth, rows // tr), in_specs=[blk] * 4, out_specs=[blk] * 3,
        out_shape=[jax.ShapeDtypeStruct(w.shape, F32)] * 3, name="adamw")(w, g, m, v)


def _local_step(x, positions, target, gains, exchange):
    t = x.shape[0]
    tm = 512
    inv_freq = ROPE_THETA ** (-jnp.arange(0, ROPE_DIM, 2, dtype=F32) / ROPE_DIM)
    lane = np.arange(LANES) % HEAD_DIM
    freq_row = jnp.where(lane < ROPE_DIM, inv_freq[lane % (ROPE_DIM // 2)], 0.0).astype(F32)[None, :]
    cos, sin = _rope_tables(positions.reshape(t, 1), freq_row)

    def hosted(tag, fn, *args, **kwargs):
        *out, got = fn(*args, comm=exchange.host(tag), **kwargs)
        if got is not None:
            exchange.hosted(tag, got)
        return out[0] if len(out) == 1 else out

    saved = []
    h1 = _norm_fwd(x, gains["pre_mix_norm"], 0, tm=tm)
    for l in range(DEPTH):
        w = exchange.weights(l)
        proj = hosted(("fwd", l, "in_proj"), _mm_nn, h1, w["w_in"], tm=1024, name="in_proj")
        attn, lse = hosted(("fwd", l, "attn"), _attn_fwd, proj, cos, sin)
        conv_y = _conv_fwd(proj, w["conv_w"], l)
        merged = _merge_fwd(attn, conv_y, gains["attn_out_norm"], gains["conv_out_norm"], l, tm=tm)
        mix = hosted(("fwd", l, "out_proj"), _mm_nn, merged, w["w_out"], tm=1024, name="out_proj")
        x1, h2 = _resnorm_fwd(x, mix, gains["post_mix_norm"], l, gains["pre_ffn_norm"], l, tm=tm)
        gu = hosted(("fwd", l, "gate_up"), _mm_nn, h2, w["w_gate_up"], tm=1024, name="gate_up")
        act = _swiglu_fwd(gu, tm=tm)
        f = hosted(("fwd", l, "down"), _mm_nn, act, w["w_down"], tm=1024, name="down")
        nxt = (gains["pre_mix_norm"], l + 1) if l + 1 < DEPTH else (None, None)
        x2, h1_next = _resnorm_fwd(x1, f, gains["post_ffn_norm"], l, *nxt, tm=tm)
        saved.append(dict(x=x, h1=h1, proj=proj, attn=attn, lse=lse, conv_y=conv_y, merged=merged, mix=mix,
                          x1=x1, h2=h2, gu=gu, act=act, f=f))
        x, h1 = x2, h1_next

    dres, loss_lanes = _loss_fwd_bwd(x, target, tm=tm)

    g_gain = {k: [None] * DEPTH for k in gains}
    g_conv = [None] * DEPTH
    _, df, _, g_gain["post_ffn_norm"][DEPTH - 1] = _norm_bwd(
        dres, None, (saved[-1]["f"], gains["post_ffn_norm"], DEPTH - 1), tm=tm)
    for l in reversed(range(DEPTH)):
        sv, w = saved[l], exchange.weights(l)
        d_act = hosted(("bwd", l, "down_dx"), _mm_nt, df, w["w_down"], tm=1024, tko=FFN // 2, name="down_dx")
        g_down = _mm_tn(sv["act"], df, 1, tka=256, name="down_dw")
        dgu = _swiglu_bwd(sv["gu"], d_act, tm=tm)
        dh2 = hosted(("bwd", l, "gate_up_dx"), _mm_nt, dgu, w["w_gate_up"], tm=1024, tko=D_MODEL, name="gate_up_dx")
        g_gate_up = _mm_tn(sv["h2"], dgu, N_CHIPS, tka=512, name="gate_up_dw")
        exchange.grads(l, "ffn", dict(w_down=g_down.reshape(N_CHIPS, FFN // N_CHIPS, D_MODEL), w_gate_up=g_gate_up))
        dx1, dmix, g_gain["pre_ffn_norm"][l], g_gain["post_mix_norm"][l] = _norm_bwd(
            dres, (dh2, sv["x1"], gains["pre_ffn_norm"], l), (sv["mix"], gains["post_mix_norm"], l), tm=tm)
        d_merged = hosted(("bwd", l, "out_proj_dx"), _mm_nt, dmix, w["w_out"], tm=1024, tko=D_MODEL, name="out_proj_dx")
        g_out = _mm_tn(sv["merged"], dmix, 1, tka=512, name="out_proj_dw")
        d_attn, delta, d_conv_y, g_gain["attn_out_norm"][l], g_gain["conv_out_norm"][l] = _merge_bwd(
            d_merged, sv["attn"], sv["conv_y"], gains["attn_out_norm"], gains["conv_out_norm"], l, tm=tm)
        dq, dk, dv = hosted(("bwd", l, "attn"), _attn_bwd, sv["proj"], cos, sin, d_attn, sv["lse"], delta)
        du, dgb, dgc, g_conv[l] = _conv_bwd(sv["proj"], w["conv_w"], l, d_conv_y)
        d_proj = jnp.concatenate([dq, dk, dv, du, dgb, dgc], axis=1)
        dh1 = hosted(("bwd", l, "in_proj_dx"), _mm_nt, d_proj, w["w_in"], tm=1024, tko=D_MODEL, name="in_proj_dx")
        g_in = _mm_tn(sv["h1"], d_proj, N_CHIPS, tka=512, name="in_proj_dw")
        exchange.grads(l, "mix", dict(w_out=g_out.reshape(N_CHIPS, D_MODEL // N_CHIPS, D_MODEL), w_in=g_in))
        below = (saved[l - 1]["f"], gains["post_ffn_norm"], l - 1) if l > 0 else None
        dres, df, g_gain["pre_mix_norm"][l], g_below = _norm_bwd(
            dx1, (dh1, sv["x"], gains["pre_mix_norm"], l), below, tm=tm)
        if l > 0:
            g_gain["post_ffn_norm"][l - 1] = g_below

    g_gain = {k: jnp.concatenate(v, axis=0) for k, v in g_gain.items()}
    return loss_lanes, dres, g_gain, jnp.stack(g_conv, axis=0)


class _Exchange:
    GATHER_HOSTS = {"in_proj": "w_in", "attn": "w_gate_up", "gate_up": "w_down", "down": "w_out"}

    def __init__(self, params, place):
        self.place = place
        self.slabs = {k: [_own_shard_slab(params[k], l, place, BF16) for l in range(DEPTH)] for k in MATRIX_NAMES}
        first = [self.slabs[k][0] for k in MATRIX_NAMES]
        got = _run_comm(_gather_comm(first, _own_conv_slab(params["conv_w"], place)), "gather_first_layer")
        self.gathered = {k: [None] * DEPTH for k in MATRIX_NAMES}
        for k, g in zip(MATRIX_NAMES, got):
            self.gathered[k][0] = g
        self.conv_w = got[len(MATRIX_NAMES)]
        self.full = {k: lax.empty(params[k].shape, F32) for k in MATRIX_NAMES}
        self.pending = {}

    def weights(self, l):
        g = {k: self.gathered[k][l] for k in MATRIX_NAMES}
        return dict(w_in=g["w_in"], w_gate_up=g["w_gate_up"], conv_w=self.conv_w,
                    w_out=g["w_out"].reshape(1, D_MODEL, D_MODEL), w_down=g["w_down"].reshape(1, FFN, D_MODEL))

    def host(self, tag):
        phase, l, kernel = tag
        if phase == "fwd" and kernel in self.GATHER_HOSTS and l + 1 < DEPTH:
            return _gather_comm([self.slabs[self.GATHER_HOSTS[kernel]][l + 1]])
        if tag in self.pending:
            return _partials_comm(self.pending[tag][2])
        return None

    def hosted(self, tag, results):
        phase, l, kernel = tag
        if phase == "fwd":
            self.gathered[self.GATHER_HOSTS[kernel]][l + 1] = results[0]
        else:
            self._finish_reduction(*self.pending.pop(tag), results)

    def grads(self, l, group, grads):
        names = list(grads)
        got = _run_comm(_halves_comm([grads[k] for k in names]), "exchange_halves")
        partials = [_add_halves(grads[k], r, self.place) for k, r in zip(names, got)]
        tag = ("bwd", l, "attn") if group == "ffn" else ("bwd", l - 1, "gate_up_dx")
        if tag[1] >= 0:
            self.pending[tag] = (l, names, partials)
        else:
            self._finish_reduction(l, names, partials, _run_comm(_partials_comm(partials), "exchange_partials"))

    def _finish_reduction(self, l, names, partials, others):
        for k, p, q in zip(names, partials, others):
            self.full[k] = _sum_partials(p, q, self.place, self.full[k], l)
        shared = _run_comm(_share_comm([self.full[k] for k in names], l), "share_halves")
        for k, g in zip(names, shared):
            self.full[k] = g


def kernel(x, positions, pre_mix_norm, w_in, conv_w, attn_out_norm, conv_out_norm, w_out, post_mix_norm, pre_ffn_norm, w_gate_up, w_down, post_ffn_norm, loss_target, m_pre_mix_norm, m_w_in, m_conv_w, m_attn_out_norm, m_conv_out_norm, m_w_out, m_post_mix_norm, m_pre_ffn_norm, m_w_gate_up, m_w_down, m_post_ffn_norm, v_pre_mix_norm, v_w_in, v_conv_w, v_attn_out_norm, v_conv_out_norm, v_w_out, v_post_mix_norm, v_pre_ffn_norm, v_w_gate_up, v_w_down, v_post_ffn_norm):
    params = dict(pre_mix_norm=pre_mix_norm, w_in=w_in, conv_w=conv_w, attn_out_norm=attn_out_norm,
                  conv_out_norm=conv_out_norm, w_out=w_out, post_mix_norm=post_mix_norm, pre_ffn_norm=pre_ffn_norm,
                  w_gate_up=w_gate_up, w_down=w_down, post_ffn_norm=post_ffn_norm)
    mom1 = dict(pre_mix_norm=m_pre_mix_norm, w_in=m_w_in, conv_w=m_conv_w, attn_out_norm=m_attn_out_norm,
                conv_out_norm=m_conv_out_norm, w_out=m_w_out, post_mix_norm=m_post_mix_norm,
                pre_ffn_norm=m_pre_ffn_norm, w_gate_up=m_w_gate_up, w_down=m_w_down, post_ffn_norm=m_post_ffn_norm)
    mom2 = dict(pre_mix_norm=v_pre_mix_norm, w_in=v_w_in, conv_w=v_conv_w, attn_out_norm=v_attn_out_norm,
                conv_out_norm=v_conv_out_norm, w_out=v_w_out, post_mix_norm=v_post_mix_norm,
                pre_ffn_norm=v_pre_ffn_norm, w_gate_up=v_w_gate_up, w_down=v_w_down, post_ffn_norm=v_post_ffn_norm)
    xi, yi, ci = lax.axis_index("x"), lax.axis_index("y"), lax.axis_index("c")
    place = jnp.stack([2 * xi + yi, ci]).astype(jnp.int32)

    exchange = _Exchange(params, place)
    gains = {k: params[k][:, None, :] for k in GAIN_NAMES}
    loss_lanes, grad_x, g_gain, g_conv = _local_step(x[0], positions[0], loss_target[0], gains, exchange)
    grad = dict(exchange.full)

    small = [g_gain[k].reshape(-1) for k in GAIN_NAMES] + [g_conv.reshape(-1), loss_lanes.reshape(-1)]
    sizes = [int(s.shape[0]) for s in small]
    flat = jnp.concatenate(small)
    loss_row = (sum(sizes) - LANES) // LANES
    rows = -(-flat.shape[0] // (8 * LANES)) * 8
    flat = jnp.pad(flat, (0, rows * LANES - flat.shape[0])).reshape(rows, LANES)
    total = _allreduce_small(flat, loss_row).reshape(-1)
    offsets = np.cumsum([0] + sizes)
    for i, k in enumerate(GAIN_NAMES):
        grad[k] = total[offsets[i]:offsets[i + 1]].reshape(params[k].shape)
    conv_all = total[offsets[6]:offsets[7]].reshape(DEPTH, N_CHIPS, 3, LANES)
    grad["conv_w"] = lax.dynamic_index_in_dim(conv_all, 2 * xi + yi, axis=1, keepdims=False)
    loss = total[offsets[7]]

    delta, new_m, new_v = {}, {}, {}
    for k in WEIGHT_ORDER:
        shape = params[k].shape
        as3 = (lambda a: a) if len(shape) == 3 else (lambda a: a[:, None, :])
        rows_k = shape[1] if len(shape) == 3 else 1
        tr = {1024: 512, 704: 352, 256: 256}.get(rows_k, rows_k)
        d, m, v = _adamw(as3(params[k]), as3(grad[k]), as3(mom1[k]), as3(mom2[k]), tr=tr)
        delta[k], new_m[k], new_v[k] = d.reshape(shape), m.reshape(shape), v.reshape(shape)

    return (loss, grad_x[None], *[grad[k] for k in WEIGHT_ORDER], *[delta[k] for k in WEIGHT_ORDER],
            *[new_m[k] for k in WEIGHT_ORDER], *[new_v[k] for k in WEIGHT_ORDER])
```

```python
import functools
from typing import Callable, NamedTuple

import numpy as np
import jax
import jax.numpy as jnp
from jax import lax
from jax.experimental import pallas as pl
from jax.experimental.pallas import tpu as pltpu

F32 = jnp.float32
BF16 = jnp.bfloat16
MESH = pl.DeviceIdType.MESH

D_MODEL = 1024
ATTN_W = 512
CONV_W = 512
HEAD_DIM = 64
ROPE_DIM = 16
ROPE_THETA = 500000.0
FFN = 2816
DEPTH = 4
RMS_EPS = 1e-6
NEG_INF = -1e30
N_CHIPS = 4
N_DEV = 8
LANES = 128
BF16_ROWS = 16
DILATIONS = (1, 4, 16)
BAND = 64
TQ = 128
WIN = TQ + 2 * BAND
SCALE = HEAD_DIM ** -0.5

ADAM_LR = 0.001
ADAM_B1 = 0.9
ADAM_B2 = 0.999
ADAM_EPS = 1e-08
ADAM_WD = 0.01
ADAM_STEP = 10

GAIN_NAMES = ("pre_mix_norm", "attn_out_norm", "conv_out_norm", "post_mix_norm", "pre_ffn_norm", "post_ffn_norm")
MATRIX_NAMES = ("w_in", "w_out", "w_gate_up", "w_down")
WEIGHT_ORDER = ("pre_mix_norm", "w_in", "conv_w", "attn_out_norm", "conv_out_norm", "w_out", "post_mix_norm",
                "pre_ffn_norm", "w_gate_up", "w_down", "post_ffn_norm")

ANY = pl.BlockSpec(memory_space=pl.ANY)
WHOLE_VMEM = pl.BlockSpec(memory_space=pltpu.VMEM)
LANE_CONTRACT = (((1,), (1,)), ((), ()))
ROW_CONTRACT = (((0,), (0,)), ((), ()))


def _const_spec(block, index):
    return pl.BlockSpec(block, lambda *_: index)


def _gain_spec(g3, l):
    return _const_spec((None, 1, g3.shape[-1]), (l, 0, 0))


class _Comm(NamedTuple):
    ins: tuple
    inouts: tuple
    out_shapes: tuple
    n_sems: int
    start: Callable
    finish: Callable


def _place():
    x, y, c = lax.axis_index("x"), lax.axis_index("y"), lax.axis_index("c")
    other_chips = [(1 - x, y), (x, 1 - y), (1 - x, 1 - y)]
    return x, y, c, other_chips


def _remote(src, dst, send_sem, recv_sem, to):
    return pltpu.make_async_remote_copy(src_ref=src, dst_ref=dst, send_sem=send_sem, recv_sem=recv_sem,
                                        device_id=to, device_id_type=MESH)


def _call(body, operands, *, name, grid, in_specs, out_specs, out_shape, scratch_shapes=(), comm=None):
    in_specs, out_specs, out_shape = list(in_specs), list(out_specs), list(out_shape)
    scratch_shapes = list(scratch_shapes)
    if comm is None:
        out = pl.pallas_call(body, grid=grid, in_specs=in_specs, out_specs=out_specs, out_shape=out_shape,
                             scratch_shapes=scratch_shapes, name=name)(*operands)
        return list(out), None
    n_in, n_out, n_scr = len(in_specs), len(out_shape), len(scratch_shapes)
    n_ci, n_cio, n_co = len(comm.ins), len(comm.inouts), len(comm.out_shapes)

    def hosted(*refs):
        refs = list(refs)
        ins, c_ins = refs[:n_in], refs[n_in:n_in + n_ci]
        base = n_in + n_ci + n_cio
        outs = refs[base:base + n_out]
        c_io = refs[base + n_out:base + n_out + n_cio]
        c_out = refs[base + n_out + n_cio:base + n_out + n_cio + n_co]
        scr = refs[base + n_out + n_cio + n_co:]
        send_sems, recv_sems = scr[n_scr], scr[n_scr + 1]
        if grid:
            first = functools.reduce(jnp.logical_and, [pl.program_id(a) == 0 for a in range(len(grid))])
            last = functools.reduce(jnp.logical_and, [pl.program_id(a) == grid[a] - 1 for a in range(len(grid))])
            pl.when(first)(lambda: comm.start(c_ins, c_io, c_out, send_sems, recv_sems))
            body(*ins, *outs, *scr[:n_scr])
            pl.when(last)(lambda: comm.finish(c_ins, c_io, c_out, send_sems, recv_sems))
        else:
            comm.start(c_ins, c_io, c_out, send_sems, recv_sems)
            body(*ins, *outs, *scr[:n_scr])
            comm.finish(c_ins, c_io, c_out, send_sems, recv_sems)

    res = pl.pallas_call(
        hosted, grid=grid, in_specs=in_specs + [ANY] * (n_ci + n_cio), out_specs=out_specs + [ANY] * (n_cio + n_co),
        out_shape=out_shape + [jax.ShapeDtypeStruct(a.shape, a.dtype) for a in comm.inouts] + list(comm.out_shapes),
        input_output_aliases={n_in + n_ci + i: n_out + i for i in range(n_cio)},
        scratch_shapes=scratch_shapes + [pltpu.SemaphoreType.DMA((comm.n_sems,))] * 2,
        name=name)(*operands, *comm.ins, *comm.inouts)
    return list(res[:n_out]), list(res[n_out:])


def _run_comm(comm, name):
    return _call(lambda: None, [], name=name, grid=(), in_specs=[], out_specs=[], out_shape=[], comm=comm)[1]


def _row_half(ref, lead, core, rows, align):
    hr = rows // 2
    return ref.at[(*lead, pl.ds(pl.multiple_of(core * hr, align), hr), slice(None))]


def _gather_comm(slabs, conv_slab=None):
    n = len(slabs)
    n_conv = 0 if conv_slab is None else 3

    def direct(ios, send, recv):
        x, y, c, chips = _place()
        copies = []
        for a in range(n):
            own = _row_half(ios[a], (2 * x + y,), c, slabs[a].shape[1], BF16_ROWS)
            copies += [_remote(own, own, send.at[a * 3 + j], recv.at[a * 3 + j], (*chip, c))
                       for j, chip in enumerate(chips)]
        if conv_slab is not None:
            own = ios[n].at[:, 2 * x + y]
            copies += [_remote(own, own, send.at[6 * n + j], recv.at[6 * n + j], (*chip, c))
                       for j, chip in enumerate(chips)]
        return copies

    def start(ins, ios, outs, send, recv):
        for cp in direct(ios, send, recv):
            cp.start()

    def finish(ins, ios, outs, send, recv):
        x, y, c, chips = _place()
        sibling = (x, y, 1 - c)
        passed = []
        for a in range(n):
            for j, chip in enumerate(chips):
                landed = _row_half(ios[a], (2 * chip[0] + chip[1],), c, slabs[a].shape[1], BF16_ROWS)
                _remote(landed, landed, send.at[a * 3 + j], recv.at[a * 3 + j], (*chip, c)).wait_recv()
                fwd = _remote(landed, landed, send.at[3 * n + a * 3 + j], recv.at[3 * n + a * 3 + j], sibling)
                fwd.start()
                passed.append(fwd)
        if conv_slab is not None:
            for j, chip in enumerate(chips):
                landed = ios[n].at[:, 2 * chip[0] + chip[1]]
                _remote(landed, landed, send.at[6 * n + j], recv.at[6 * n + j], (*chip, c)).wait_recv()
        for a in range(n):
            for j, chip in enumerate(chips):
                landed = _row_half(ios[a], (2 * chip[0] + chip[1],), 1 - c, slabs[a].shape[1], BF16_ROWS)
                _remote(landed, landed, send.at[3 * n + a * 3 + j], recv.at[3 * n + a * 3 + j], sibling).wait_recv()
        for cp in direct(ios, send, recv) + passed:
            cp.wait_send()

    inouts = tuple(slabs) + (() if conv_slab is None else (conv_slab,))
    return _Comm((), inouts, (), 6 * n + n_conv, start, finish)


def _halves_comm(grads):
    n = len(grads)

    def copies(ins, outs, send, recv):
        x, y, c, _ = _place()
        return [_remote(_row_half(ins[a], (slice(None),), 1 - c, grads[a].shape[1], 8), outs[a],
                        send.at[a], recv.at[a], (x, y, 1 - c)) for a in range(n)]

    def start(ins, ios, outs, send, recv):
        for cp in copies(ins, outs, send, recv):
            cp.start()

    def finish(ins, ios, outs, send, recv):
        for cp in copies(ins, outs, send, recv):
            cp.wait()

    out_shapes = tuple(jax.ShapeDtypeStruct((g.shape[0], g.shape[1] // 2, g.shape[2]), F32) for g in grads)
    return _Comm(tuple(grads), (), out_shapes, n, start, finish)


def _partials_comm(partials):
    n = len(partials)

    def copies(ins, outs, send, recv):
        x, y, c, chips = _place()
        return [_remote(ins[a].at[2 * chip[0] + chip[1]], outs[a].at[k], send.at[a * 3 + k], recv.at[a * 3 + k],
                        (*chip, c)) for a in range(n) for k, chip in enumerate(chips)]

    def start(ins, ios, outs, send, recv):
        for cp in copies(ins, outs, send, recv):
            cp.start()

    def finish(ins, ios, outs, send, recv):
        for cp in copies(ins, outs, send, recv):
            cp.wait()

    out_shapes = tuple(jax.ShapeDtypeStruct((3,) + p.shape[1:], BF16) for p in partials)
    return _Comm(tuple(partials), (), out_shapes, 3 * n, start, finish)


def _share_comm(grads, l):
    n = len(grads)

    def start(ins, ios, outs, send, recv):
        x, y, c, _ = _place()
        for a in range(n):
            mine = _row_half(ios[a], (l,), c, grads[a].shape[1], 8)
            _remote(mine, mine, send.at[a], recv.at[a], (x, y, 1 - c)).start()

    def finish(ins, ios, outs, send, recv):
        x, y, c, _ = _place()
        for a in range(n):
            theirs = _row_half(ios[a], (l,), 1 - c, grads[a].shape[1], 8)
            _remote(theirs, theirs, send.at[a], recv.at[a], (x, y, 1 - c)).wait()

    return _Comm((), tuple(grads), (), n, start, finish)


def _rms_fwd(x, g):
    r = lax.rsqrt(jnp.mean(x * x, axis=-1, keepdims=True) + RMS_EPS)
    return (x * r) * g


def _rms_bwd(x, g, dy):
    r = lax.rsqrt(jnp.mean(x * x, axis=-1, keepdims=True) + RMS_EPS)
    xh = x * r
    u = dy * g
    dx = r * (u - xh * jnp.mean(xh * u, axis=-1, keepdims=True))
    return dx, jnp.sum(dy * xh, axis=0, keepdims=True)


def _accumulate(ref, value, first):
    @pl.when(first)
    def _():
        ref[...] = value

    @pl.when(jnp.logical_not(first))
    def _():
        ref[...] += value


def _rope_coeffs(cos, sin):
    m = lax.broadcasted_iota(jnp.int32, cos.shape, 1) % HEAD_DIM
    a = jnp.where(m < ROPE_DIM, cos, 1.0)
    b = jnp.where(m < ROPE_DIM // 2, -sin, 0.0)
    c = jnp.where((m >= ROPE_DIM // 2) & (m < ROPE_DIM), sin, 0.0)
    return a, b, c


def _rope_apply(t, cos, sin):
    a, b, c = _rope_coeffs(cos, sin)
    n = t.shape[1]
    return a * t + b * pltpu.roll(t, n - ROPE_DIM // 2, 1) + c * pltpu.roll(t, ROPE_DIM // 2, 1)


def _rope_transpose(dt, cos, sin):
    a, b, c = _rope_coeffs(cos, sin)
    n = dt.shape[1]
    return a * dt + pltpu.roll(b * dt, ROPE_DIM // 2, 1) + pltpu.roll(c * dt, n - ROPE_DIM // 2, 1)


def _mm_nn(a, w, *, tm, name, comm=None):
    t, k = a.shape
    s_n, k2, n = w.shape
    assert k == k2 and t % tm == 0

    def body(a_ref, w_ref, o_ref):
        o_ref[...] = jnp.dot(a_ref[...], w_ref[...], preferred_element_type=F32)

    out, got = _call(
        body, [a, w], name=name, grid=(t // tm, s_n),
        in_specs=[pl.BlockSpec((tm, k), lambda i, s: (i, 0)), pl.BlockSpec((None, k, n), lambda i, s: (s, 0, 0))],
        out_specs=[pl.BlockSpec((tm, n), lambda i, s: (i, s))],
        out_shape=[jax.ShapeDtypeStruct((t, s_n * n), F32)], comm=comm)
    return out[0], got


def _mm_nt(a, w, *, tm, tko, name, comm=None):
    t, sn = a.shape
    s_n, ko, n = w.shape
    assert sn == s_n * n and t % tm == 0 and ko % tko == 0

    def body(a_ref, w_ref, o_ref):
        acc = lax.dot_general(a_ref[...], w_ref[...], (((1,), (1,)), ((), ())), preferred_element_type=F32)
        if s_n == 1:
            o_ref[...] = acc
        else:
            _accumulate(o_ref, acc, pl.program_id(2) == 0)

    out, got = _call(
        body, [a, w], name=name, grid=(t // tm, ko // tko, s_n),
        in_specs=[pl.BlockSpec((tm, n), lambda i, j, s: (i, s)),
                  pl.BlockSpec((None, tko, n), lambda i, j, s: (s, j, 0))],
        out_specs=[pl.BlockSpec((tm, tko), lambda i, j, s: (i, j))],
        out_shape=[jax.ShapeDtypeStruct((t, ko), F32)], comm=comm)
    return out[0], got


def _mm_nt_pair(a0, a1, w, *, tm, name, comm=None):
    t = a0.shape[0]
    s_n, ko, n = w.shape
    half = s_n // 2
    assert a0.shape == a1.shape == (t, half * n) and t % tm == 0

    def body(a0_ref, a1_ref, w0_ref, w1_ref, o_ref):
        acc = (lax.dot_general(a0_ref[...], w0_ref[...], LANE_CONTRACT, preferred_element_type=F32)
               + lax.dot_general(a1_ref[...], w1_ref[...], LANE_CONTRACT, preferred_element_type=F32))
        _accumulate(o_ref, acc, pl.program_id(1) == 0)

    a_spec = pl.BlockSpec((tm, n), lambda i, s: (i, s))
    out, got = _call(
        body, [a0, a1, w, w], name=name, grid=(t // tm, half),
        in_specs=[a_spec, a_spec, pl.BlockSpec((None, ko, n), lambda i, s: (s, 0, 0)),
                  pl.BlockSpec((None, ko, n), lambda i, s: (half + s, 0, 0))],
        out_specs=[pl.BlockSpec((tm, ko), lambda i, s: (i, 0))],
        out_shape=[jax.ShapeDtypeStruct((t, ko), F32)], comm=comm)
    return out[0], got


def _mm_tn(a, b, s_n, *, tka, name, into=None, shard0=0):
    t, ka = a.shape
    n = b.shape[1] // s_n
    assert b.shape[0] == t and ka % tka == 0

    def body(a_ref, b_ref, *rest):
        rest[-1][...] = lax.dot_general(a_ref[...], b_ref[...], ROW_CONTRACT, preferred_element_type=F32)

    operands, in_specs, aliases = [a, b], [pl.BlockSpec((t, tka), lambda i, s: (0, i)),
                                           pl.BlockSpec((t, n), lambda i, s: (0, s))], {}
    out_shape = jax.ShapeDtypeStruct((s_n, ka, n), F32)
    if into is not None:
        operands, in_specs, aliases = operands + [into], in_specs + [ANY], {2: 0}
        out_shape = jax.ShapeDtypeStruct(into.shape, F32)
    return pl.pallas_call(
        body, grid=(ka // tka, s_n), in_specs=in_specs,
        out_specs=pl.BlockSpec((None, tka, n), lambda i, s: (shard0 + s, i, 0)),
        out_shape=out_shape, input_output_aliases=aliases, name=name)(*operands)


def _gate_up_swiglu(h, w, *, tm, comm=None):
    t, k = h.shape
    s_n, _, n = w.shape
    half = s_n // 2

    def body(h_ref, wg_ref, wu_ref, g_ref, u_ref, a_ref):
        g_ref[...] = jnp.dot(h_ref[...], wg_ref[...], preferred_element_type=F32)
        u_ref[...] = jnp.dot(h_ref[...], wu_ref[...], preferred_element_type=F32)
        g = g_ref[...]
        a_ref[...] = (g * jax.nn.sigmoid(g) * u_ref[...]).astype(BF16)

    col = pl.BlockSpec((tm, n), lambda i, j: (i, j))
    out, got = _call(
        body, [h, w, w], name="gate_up", grid=(t // tm, half),
        in_specs=[pl.BlockSpec((tm, k), lambda i, j: (i, 0)), pl.BlockSpec((None, k, n), lambda i, j: (j, 0, 0)),
                  pl.BlockSpec((None, k, n), lambda i, j: (half + j, 0, 0))],
        out_specs=[col, col, col],
        out_shape=[jax.ShapeDtypeStruct((t, half * n), F32)] * 2 + [jax.ShapeDtypeStruct((t, half * n), BF16)],
        comm=comm)
    return out[0], out[1], out[2], got


def _down_dx_swiglu_bwd(df, w, g, u, *, tm, tko):
    t, k = df.shape
    _, ko, _ = w.shape
    assert t % tm == 0 and ko % tko == 0

    def body(df_ref, w_ref, g_ref, u_ref, dg_ref, du_ref):
        d = lax.dot_general(df_ref[...], w_ref[...], LANE_CONTRACT, preferred_element_type=F32)
        gg = g_ref[...]
        sig = jax.nn.sigmoid(gg)
        dg_ref[...] = (d * u_ref[...] * (sig * (1.0 + gg * (1.0 - sig)))).astype(BF16)
        du_ref[...] = (d * (gg * sig)).astype(BF16)

    col = pl.BlockSpec((tm, tko), lambda i, j: (i, j))
    return pl.pallas_call(
        body, grid=(t // tm, ko // tko),
        in_specs=[pl.BlockSpec((tm, k), lambda i, j: (i, 0)), pl.BlockSpec((None, tko, k), lambda i, j: (0, j, 0)),
                  col, col],
        out_specs=[col, col], out_shape=[jax.ShapeDtypeStruct((t, ko), BF16)] * 2, name="down_dx")(df, w, g, u)


def _rope_tables(positions_col, inv_freq_row):
    t = positions_col.shape[0]

    def body(pos_ref, f_ref, cos_ref, sin_ref):
        ang = pos_ref[...].astype(F32) * f_ref[...]
        cos_ref[...] = jnp.cos(ang)
        sin_ref[...] = jnp.sin(ang)

    return pl.pallas_call(
        body, out_shape=[jax.ShapeDtypeStruct((t, LANES), F32)] * 2, name="rope_tables")(positions_col, inv_freq_row)


def _norm_fwd(x, g3, l, *, tm):
    t, w = x.shape

    def body(x_ref, g_ref, h_ref):
        h_ref[...] = _rms_fwd(x_ref[...], g_ref[...]).astype(BF16)

    return pl.pallas_call(
        body, grid=(t // tm,),
        in_specs=[pl.BlockSpec((tm, w), lambda i: (i, 0)), _gain_spec(g3, l)],
        out_specs=pl.BlockSpec((tm, w), lambda i: (i, 0)),
        out_shape=jax.ShapeDtypeStruct((t, w), BF16), name="norm_fwd")(x, g3)


def _resnorm_fwd(x, y, g_post3, l_post, g_next3, l_next, *, tm):
    t, w = x.shape
    with_next = g_next3 is not None
    row = pl.BlockSpec((tm, w), lambda i: (i, 0))

    def body(x_ref, y_ref, gp_ref, *rest):
        x_new = x_ref[...] + _rms_fwd(y_ref[...], gp_ref[...])
        if with_next:
            gn_ref, xo_ref, h_ref = rest
            h_ref[...] = _rms_fwd(x_new, gn_ref[...]).astype(BF16)
        else:
            (xo_ref,) = rest
        xo_ref[...] = x_new

    ins = [x, y, g_post3] + ([g_next3] if with_next else [])
    in_specs = [row, row, _gain_spec(g_post3, l_post)] + ([_gain_spec(g_next3, l_next)] if with_next else [])
    out_shape = [jax.ShapeDtypeStruct((t, w), F32)] + ([jax.ShapeDtypeStruct((t, w), BF16)] if with_next else [])
    out = pl.pallas_call(body, grid=(t // tm,), in_specs=in_specs, out_specs=[row] * len(out_shape),
                         out_shape=out_shape, name="resnorm_fwd")(*ins)
    return (out[0], out[1]) if with_next else (out[0], None)


def _conv_fwd(proj, conv_w, l):
    t = proj.shape[0]
    col0 = 3 * ATTN_W // LANES

    def body(u_ref, gb_ref, gc_ref, w_ref, y_ref):
        c = gc_ref[...] * u_ref[...]
        row = lax.broadcasted_iota(jnp.int32, c.shape, 0)
        c_prev = jnp.where(row == 0, 0.0, pltpu.roll(c, 1, 0))
        c_next = jnp.where(row == t - 1, 0.0, pltpu.roll(c, t - 1, 0))
        w = w_ref[...]
        y_ref[...] = gb_ref[...] * (w[0:1] * c_prev + w[1:2] * c + w[2:3] * c_next)

    nj = CONV_W // LANES
    cols = lambda base: pl.BlockSpec((t, LANES), lambda j: (0, base + j))
    return pl.pallas_call(
        body, grid=(nj,),
        in_specs=[cols(col0), cols(col0 + nj), cols(col0 + 2 * nj),
                  pl.BlockSpec((None, None, 3, LANES), lambda j: (l, j, 0, 0))],
        out_specs=pl.BlockSpec((t, LANES), lambda j: (0, j)),
        out_shape=jax.ShapeDtypeStruct((t, CONV_W), F32), name="conv_fwd")(proj, proj, proj, conv_w)


def _merge_fwd(attn, conv_y, ga3, gc3, l, *, tm):
    t = attn.shape[0]
    row = pl.BlockSpec((tm, ATTN_W), lambda i: (i, 0))

    def body(a_ref, c_ref, ga_ref, gc_ref, m_ref):
        m_ref[:, :ATTN_W] = _rms_fwd(a_ref[...], ga_ref[...]).astype(BF16)
        m_ref[:, ATTN_W:] = _rms_fwd(c_ref[...], gc_ref[...]).astype(BF16)

    return pl.pallas_call(
        body, grid=(t // tm,),
        in_specs=[row, row, _gain_spec(ga3, l), _gain_spec(gc3, l)],
        out_specs=pl.BlockSpec((tm, D_MODEL), lambda i: (i, 0)),
        out_shape=jax.ShapeDtypeStruct((t, D_MODEL), BF16), name="merge_fwd")(attn, conv_y, ga3, gc3)


def _loss_fwd_bwd(y, target, *, tm):
    t, w = y.shape
    row = pl.BlockSpec((tm, w), lambda i: (i, 0))

    def body(y_ref, t_ref, dy_ref, loss_ref):
        e = y_ref[...] - t_ref[...]
        dy_ref[...] = e * (1.0 / w)
        sq = jnp.sum(e * e, axis=0, keepdims=True) * (0.5 / w)
        part = sq[:, :LANES]
        for j in range(1, w // LANES):
            part = part + sq[:, j * LANES:(j + 1) * LANES]
        _accumulate(loss_ref, part, pl.program_id(0) == 0)

    return pl.pallas_call(
        body, grid=(t // tm,), in_specs=[row, row],
        out_specs=[row, _const_spec((1, LANES), (0, 0))],
        out_shape=[jax.ShapeDtypeStruct((t, w), F32), jax.ShapeDtypeStruct((1, LANES), F32)], name="loss")(y, target)


def _tile_rows(t, nt, lb, d):
    r = t // nt
    q0 = (t % nt) * TQ
    m0 = jnp.clip(q0 - BAND, 0, lb - WIN)
    if d == 1:
        return pl.ds(pl.multiple_of(q0, TQ), TQ), pl.ds(pl.multiple_of(m0, BAND), WIN), m0 - q0
    return pl.ds(r + d * q0, TQ, stride=d), pl.ds(r + d * m0, WIN, stride=d), m0 - q0


def _for_row_chunks(t, fn, chunk=512):
    def step(i, carry):
        fn(pl.ds(pl.multiple_of(i * chunk, chunk), chunk))
        return carry

    lax.fori_loop(0, t // chunk, step, 0)


def _rope_into(dst_ref, src_ref, cos_ref, sin_ref, t, scale=1.0):
    def chunk(rows):
        dst_ref[rows, :] = _rope_apply(src_ref[rows, :], cos_ref[rows, :], sin_ref[rows, :]) * scale

    _for_row_chunks(t, chunk)


WINDOW_OFFSETS = (-BAND, 0, -2 * BAND)


def _fill_band_bias(bias_ref):
    rel0 = (lax.broadcasted_iota(jnp.int32, (2 * TQ, WIN), 1)
            - lax.broadcasted_iota(jnp.int32, (2 * TQ, WIN), 0) % TQ)
    for j, off in enumerate(WINDOW_OFFSETS):
        rel = rel0 + off
        bias_ref[j] = jnp.where((rel >= -BAND) & (rel <= BAND), 0.0, NEG_INF)


def _band_bias(bias_ref, off):
    return bias_ref[jnp.where(off == WINDOW_OFFSETS[0], 0, jnp.where(off == WINDOW_OFFSETS[1], 1, 2))]


def _stack_heads(a, first_head):
    return jnp.concatenate([jnp.where(first_head, a, 0.0), jnp.where(first_head, 0.0, a)], axis=0)


def _unstack_heads(a2, first_head):
    return jnp.where(first_head, a2[:TQ], a2[TQ:])


def _attn_fwd(proj, cos, sin, comm=None):
    t = proj.shape[0]
    npair = ATTN_W // LANES

    def body(q_ref, k_ref, v_ref, cos_ref, sin_ref, o_ref, lse_ref, qs, ks, o1, o2, l0, l1, l2, m1, m2, bias):
        _rope_into(qs, q_ref, cos_ref, sin_ref, t, SCALE)
        _rope_into(ks, k_ref, cos_ref, sin_ref, t)
        _fill_band_bias(bias)
        outs, dens, maxs = (o_ref, o1, o2), (l0, l1, l2), (lse_ref, m1, m2)
        first_head = lax.broadcasted_iota(jnp.int32, (TQ, LANES), 1) < HEAD_DIM
        ones = jnp.ones((WIN, LANES), BF16)
        for b, d in enumerate(DILATIONS):
            lb = t // d
            nt = lb // TQ

            def tile(ti, carry, b=b, d=d, lb=lb, nt=nt):
                qrows, krows, off = _tile_rows(ti, nt, lb, d)
                q2 = _stack_heads(qs[qrows, :], first_head).astype(BF16)
                kw = ks[krows, :].astype(BF16)
                vw = jnp.concatenate([v_ref[krows, :].astype(BF16), ones], axis=1)
                s = lax.dot_general(q2, kw, LANE_CONTRACT, preferred_element_type=F32) + _band_bias(bias, off)
                m = jnp.max(s, axis=-1, keepdims=True)
                pv = jnp.dot(jnp.exp(s - m).astype(BF16), vw, preferred_element_type=F32)
                outs[b][qrows, :] = _unstack_heads(pv[:, :LANES], first_head)
                dens[b][qrows, :] = _unstack_heads(pv[:, LANES:], first_head)
                maxs[b][qrows, :] = _unstack_heads(jnp.broadcast_to(m, (2 * TQ, LANES)), first_head)
                return carry

            lax.fori_loop(0, d * nt, tile, 0, unroll=8)

        def finish(rows):
            ms = [m_b[rows, :] for m_b in maxs]
            m_all = jnp.maximum(jnp.maximum(ms[0], ms[1]), ms[2])
            ws = [jnp.exp(m_b - m_all) for m_b in ms]
            den = ws[0] * dens[0][rows, :] + ws[1] * dens[1][rows, :] + ws[2] * dens[2][rows, :]
            num = ws[0] * outs[0][rows, :] + ws[1] * outs[1][rows, :] + ws[2] * outs[2][rows, :]
            o_ref[rows, :] = num / den
            lse_ref[rows, :] = m_all + jnp.log(den)

        _for_row_chunks(t, finish, 256)

    cols = lambda base: pl.BlockSpec((t, LANES), lambda g: (0, base + g))
    out, got = _call(
        body, [proj, proj, proj, cos, sin], name="attn_fwd", grid=(npair,),
        in_specs=[cols(0), cols(npair), cols(2 * npair), WHOLE_VMEM, WHOLE_VMEM],
        out_specs=[cols(0), cols(0)],
        out_shape=[jax.ShapeDtypeStruct((t, ATTN_W), F32)] * 2,
        scratch_shapes=[pltpu.VMEM((t, LANES), F32)] * 9 + [pltpu.VMEM((len(WINDOW_OFFSETS), 2 * TQ, WIN), F32)],
        comm=comm)
    return out[0], out[1], got


def _attn_bwd(proj, cos, sin, d_attn, lse, delta, comm=None):
    t = proj.shape[0]
    npair = ATTN_W // LANES

    def body(q_ref, k_ref, v_ref, cos_ref, sin_ref, do_ref, l_ref, dl_ref, dq_ref, dk_ref, dv_ref,
             qs, ks, dq_acc, dk_acc, dv_acc, bias):
        _rope_into(qs, q_ref, cos_ref, sin_ref, t, SCALE)
        _rope_into(ks, k_ref, cos_ref, sin_ref, t)
        _fill_band_bias(bias)
        dq_acc[...] = jnp.zeros(dq_acc.shape, F32)
        dk_acc[...] = jnp.zeros(dk_acc.shape, F32)
        dv_acc[...] = jnp.zeros(dv_acc.shape, F32)
        first_head = lax.broadcasted_iota(jnp.int32, (TQ, LANES), 1) < HEAD_DIM

        def stack_column(a):
            return jnp.concatenate([a[:, 0:1], a[:, HEAD_DIM:HEAD_DIM + 1]], axis=0)

        for d in DILATIONS:
            lb = t // d
            nt = lb // TQ

            def tile(ti, carry, d=d, lb=lb, nt=nt):
                qrows, krows, off = _tile_rows(ti, nt, lb, d)
                q2 = _stack_heads(qs[qrows, :], first_head).astype(BF16)
                do2 = _stack_heads(do_ref[qrows, :], first_head).astype(BF16)
                kw = ks[krows, :].astype(BF16)
                vw = v_ref[krows, :].astype(BF16)
                s = lax.dot_general(q2, kw, LANE_CONTRACT, preferred_element_type=F32) + _band_bias(bias, off)
                p = jnp.exp(s - stack_column(l_ref[qrows, :]))
                dp = lax.dot_general(do2, vw, LANE_CONTRACT, preferred_element_type=F32)
                ds = (p * (dp - stack_column(dl_ref[qrows, :]))).astype(BF16)
                dq2 = jnp.dot(ds, kw, preferred_element_type=F32)
                dq_acc[qrows, :] += _unstack_heads(dq2, first_head) * SCALE
                dk_acc[krows, :] += lax.dot_general(ds, q2, ROW_CONTRACT, preferred_element_type=F32)
                dv_acc[krows, :] += lax.dot_general(p.astype(BF16), do2, ROW_CONTRACT, preferred_element_type=F32)
                return carry

            lax.fori_loop(0, d * nt, tile, 0, unroll=4)

        def finish(rows):
            dq_ref[rows, :] = _rope_transpose(dq_acc[rows, :], cos_ref[rows, :], sin_ref[rows, :]).astype(BF16)
            dk_ref[rows, :] = _rope_transpose(dk_acc[rows, :], cos_ref[rows, :], sin_ref[rows, :]).astype(BF16)
            dv_ref[rows, :] = dv_acc[rows, :].astype(BF16)

        _for_row_chunks(t, finish)

    cols = lambda base: pl.BlockSpec((t, LANES), lambda g: (0, base + g))
    out, got = _call(
        body, [proj, proj, proj, cos, sin, d_attn, lse, delta], name="attn_bwd", grid=(npair,),
        in_specs=[cols(0), cols(npair), cols(2 * npair), WHOLE_VMEM, WHOLE_VMEM, cols(0), cols(0), cols(0)],
        out_specs=[cols(0)] * 3,
        out_shape=[jax.ShapeDtypeStruct((t, ATTN_W), BF16)] * 3,
        scratch_shapes=[pltpu.VMEM((t, LANES), F32)] * 5 + [pltpu.VMEM((len(WINDOW_OFFSETS), 2 * TQ, WIN), F32)],
        comm=comm)
    return out[0], out[1], out[2], got


def _norm_bwd(dres, pre, post, *, tm, comm=None):
    t, w = dres.shape
    row = pl.BlockSpec((tm, w), lambda i: (i, 0))
    gsum = _const_spec((1, w), (0, 0))
    ins, in_specs, out_shape, out_specs = [dres], [row], [], []
    if pre is not None:
        dh, x, g3, l = pre
        ins += [dh, x, g3]
        in_specs += [row, row, _gain_spec(g3, l)]
        out_shape += [jax.ShapeDtypeStruct((t, w), F32), jax.ShapeDtypeStruct((1, w), F32)]
        out_specs += [row, gsum]
    if post is not None:
        y, g3, l = post
        ins += [y, g3]
        in_specs += [row, _gain_spec(g3, l)]
        out_shape += [jax.ShapeDtypeStruct((t, w), BF16), jax.ShapeDtypeStruct((1, w), F32)]
        out_specs += [row, gsum]
    n_in = len(ins)

    def body(*refs):
        first = pl.program_id(0) == 0
        ins_r, outs_r = list(refs[:n_in]), list(refs[n_in:])
        d = ins_r.pop(0)[...]
        if pre is not None:
            dh_ref, x_ref, g_ref = ins_r[:3]
            ins_r = ins_r[3:]
            dx, dg = _rms_bwd(x_ref[...], g_ref[...], dh_ref[...])
            d = d + dx
            outs_r.pop(0)[...] = d
            _accumulate(outs_r.pop(0), dg, first)
        if post is not None:
            y_ref, g_ref = ins_r
            dy, dg = _rms_bwd(y_ref[...], g_ref[...], d)
            outs_r.pop(0)[...] = dy.astype(BF16)
            _accumulate(outs_r.pop(0), dg, first)

    out, got = _call(body, ins, name="norm_bwd", grid=(t // tm,), in_specs=in_specs, out_specs=out_specs,
                     out_shape=out_shape, comm=comm)
    d_new, dg_pre = (out.pop(0), out.pop(0)) if pre is not None else (None, None)
    dy, dg_post = (out.pop(0), out.pop(0)) if post is not None else (None, None)
    return d_new, dy, dg_pre, dg_post, got


def _merge_bwd(d_merged, attn, conv_y, ga3, gc3, l, *, tm):
    t = attn.shape[0]
    row = pl.BlockSpec((tm, ATTN_W), lambda i: (i, 0))
    gsum = _const_spec((1, ATTN_W), (0, 0))

    def body(dma_ref, dmc_ref, a_ref, c_ref, ga_ref, gc_ref, da_ref, dl_ref, dc_ref, dga_ref, dgc_ref):
        first = pl.program_id(0) == 0
        attn_t = a_ref[...]
        da, dga = _rms_bwd(attn_t, ga_ref[...], dma_ref[...])
        dc, dgc = _rms_bwd(c_ref[...], gc_ref[...], dmc_ref[...])
        da_ref[...] = da
        dc_ref[...] = dc
        same_head = (lax.broadcasted_iota(jnp.int32, (ATTN_W, ATTN_W), 0) // HEAD_DIM
                     == lax.broadcasted_iota(jnp.int32, (ATTN_W, ATTN_W), 1) // HEAD_DIM).astype(BF16)
        rest = da * attn_t
        total = jnp.zeros(rest.shape, F32)
        for _ in range(3):
            term = rest.astype(BF16)
            total = total + jnp.dot(term, same_head, preferred_element_type=F32)
            rest = rest - term.astype(F32)
        dl_ref[...] = total
        _accumulate(dga_ref, dga, first)
        _accumulate(dgc_ref, dgc, first)

    return pl.pallas_call(
        body, grid=(t // tm,),
        in_specs=[pl.BlockSpec((tm, ATTN_W), lambda i: (i, 0)), pl.BlockSpec((tm, CONV_W), lambda i: (i, 1)),
                  row, row, _gain_spec(ga3, l), _gain_spec(gc3, l)],
        out_specs=[row, row, row, gsum, gsum],
        out_shape=[jax.ShapeDtypeStruct((t, ATTN_W), F32)] * 3 + [jax.ShapeDtypeStruct((1, ATTN_W), F32)] * 2,
        name="merge_bwd")(d_merged, d_merged, attn, conv_y, ga3, gc3)


def _conv_bwd(proj, conv_w, l, d_conv_y):
    t = proj.shape[0]
    col0 = 3 * ATTN_W // LANES
    nj = CONV_W // LANES

    def body(u_ref, gb_ref, gc_ref, w_ref, dy_ref, du_ref, dgb_ref, dgc_ref, dw_ref):
        u, gc, dy = u_ref[...], gc_ref[...], dy_ref[...]
        row = lax.broadcasted_iota(jnp.int32, u.shape, 0)
        down = lambda a: jnp.where(row == 0, 0.0, pltpu.roll(a, 1, 0))
        up = lambda a: jnp.where(row == t - 1, 0.0, pltpu.roll(a, t - 1, 0))
        w = w_ref[...]
        c = gc * u
        c_prev, c_next = down(c), up(c)
        dgb_ref[...] = (dy * (w[0:1] * c_prev + w[1:2] * c + w[2:3] * c_next)).astype(BF16)
        dz = dy * gb_ref[...]
        dc = w[0:1] * up(dz) + w[1:2] * dz + w[2:3] * down(dz)
        du_ref[...] = (dc * gc).astype(BF16)
        dgc_ref[...] = (dc * u).astype(BF16)
        dw_ref[0:1, :] = jnp.sum(dz * c_prev, axis=0, keepdims=True)
        dw_ref[1:2, :] = jnp.sum(dz * c, axis=0, keepdims=True)
        dw_ref[2:3, :] = jnp.sum(dz * c_next, axis=0, keepdims=True)

    cols = lambda base: pl.BlockSpec((t, LANES), lambda j: (0, base + j))
    return pl.pallas_call(
        body, grid=(nj,),
        in_specs=[cols(col0), cols(col0 + nj), cols(col0 + 2 * nj),
                  pl.BlockSpec((None, None, 3, LANES), lambda j: (l, j, 0, 0)), cols(0)],
        out_specs=[cols(0)] * 3 + [pl.BlockSpec((None, 3, LANES), lambda j: (j, 0, 0))],
        out_shape=[jax.ShapeDtypeStruct((t, CONV_W), BF16)] * 3 + [jax.ShapeDtypeStruct((nj, 3, LANES), F32)],
        name="conv_bwd")(proj, proj, proj, conv_w, d_conv_y)


def _own_shard_slab(w, l, place, dtype):
    _, rows, cols = w.shape
    tr = rows if rows <= 704 else 512
    assert rows % tr == 0

    def body(p_ref, w_ref, o_ref):
        del p_ref
        o_ref[...] = w_ref[...].astype(dtype)

    grid_spec = pltpu.PrefetchScalarGridSpec(
        num_scalar_prefetch=1, grid=(rows // tr,),
        in_specs=[pl.BlockSpec((None, tr, cols), lambda i, p: (l, i, 0))],
        out_specs=pl.BlockSpec((None, tr, cols), lambda i, p: (p[0], i, 0)))
    return pl.pallas_call(body, grid_spec=grid_spec, name="own_shard_slab",
                          out_shape=jax.ShapeDtypeStruct((N_CHIPS, rows, cols), dtype))(place, w)


def _own_conv_slab(w, place):
    depth = w.shape[0]

    def body(p_ref, w_ref, o_ref):
        del p_ref
        o_ref[...] = w_ref[...]

    grid_spec = pltpu.PrefetchScalarGridSpec(
        num_scalar_prefetch=1, grid=(depth,),
        in_specs=[pl.BlockSpec((None, 3, LANES), lambda l, p: (l, 0, 0))],
        out_specs=pl.BlockSpec((None, None, 3, LANES), lambda l, p: (l, p[0], 0, 0)))
    return pl.pallas_call(body, grid_spec=grid_spec, name="own_conv_slab",
                          out_shape=jax.ShapeDtypeStruct((depth, N_CHIPS, 3, LANES), F32))(place, w)


def _add_halves(grad, got, place):
    s_n, rows, cols = grad.shape
    hr = rows // 2

    def body(p_ref, g_ref, r_ref, o_ref):
        del p_ref
        o_ref[...] = (g_ref[...] + r_ref[...]).astype(BF16)

    grid_spec = pltpu.PrefetchScalarGridSpec(
        num_scalar_prefetch=1, grid=(s_n,),
        in_specs=[pl.BlockSpec((None, hr, cols), lambda s, p: (s, p[1], 0)),
                  pl.BlockSpec((None, hr, cols), lambda s, p: (s, 0, 0))],
        out_specs=pl.BlockSpec((None, hr, cols), lambda s, p: (s, 0, 0)))
    return pl.pallas_call(body, grid_spec=grid_spec, out_shape=jax.ShapeDtypeStruct((s_n, hr, cols), BF16),
                          name="add_halves")(place, grad, got)


def _sum_partials(partial, got, place, acc, l):
    _, hr, cols = partial.shape

    def body(p_ref, mine_ref, got_ref, acc_ref, o_ref):
        del p_ref, acc_ref
        total = mine_ref[...].astype(F32)
        for k in range(3):
            total = total + got_ref[k].astype(F32)
        o_ref[...] = total

    grid_spec = pltpu.PrefetchScalarGridSpec(
        num_scalar_prefetch=1, grid=(1,),
        in_specs=[pl.BlockSpec((None, hr, cols), lambda i, p: (p[0], 0, 0)),
                  pl.BlockSpec((3, hr, cols), lambda i, p: (0, 0, 0)), ANY],
        out_specs=pl.BlockSpec((None, hr, cols), lambda i, p: (l, p[1], 0)))
    return pl.pallas_call(body, grid_spec=grid_spec, out_shape=jax.ShapeDtypeStruct(acc.shape, F32),
                          input_output_aliases={3: 0}, name="sum_partials")(place, partial, got, acc)


def _allreduce_small(vec, loss_row):
    rows = vec.shape[0]

    def body(v_ref, o_ref, slots, send_sems, recv_sems):
        x, y, c, _ = _place()
        me = 4 * x + 2 * y + c
        slots[me] = v_ref[...]
        copies = []
        for k in range(1, N_DEV):
            flip = lambda v, bit: 1 - v if bit else v
            peer = (flip(x, k & 4), flip(y, k & 2), flip(c, k & 1))
            copies.append(_remote(v_ref, slots.at[me], send_sems.at[k - 1], recv_sems.at[k - 1], peer))
        for cp in copies:
            cp.start()
        for k in range(1, N_DEV):
            flip = lambda v, bit: 1 - v if bit else v
            peer_id = 4 * flip(x, k & 4) + 2 * flip(y, k & 2) + flip(c, k & 1)
            _remote(v_ref, slots.at[peer_id], send_sems.at[k - 1], recv_sems.at[k - 1], (x, y, c)).wait_recv()
        for cp in copies:
            cp.wait_send()
        total = slots[0]
        for dev in range(1, N_DEV):
            total = total + slots[dev]
        o_ref[...] = total
        o_ref[loss_row:loss_row + 1, :] = jnp.broadcast_to(
            jnp.sum(total[loss_row:loss_row + 1, :], axis=-1, keepdims=True), (1, LANES))

    return pl.pallas_call(
        body, in_specs=[WHOLE_VMEM], out_specs=WHOLE_VMEM, out_shape=jax.ShapeDtypeStruct((rows, LANES), F32),
        scratch_shapes=[pltpu.VMEM((N_DEV, rows, LANES), F32), pltpu.SemaphoreType.DMA((N_DEV - 1,)),
                        pltpu.SemaphoreType.DMA((N_DEV - 1,))],
        name="allreduce_small")(vec)


def _adamw(w, g, m, v, *, tr):
    depth, rows, cols = w.shape
    assert rows % tr == 0
    c1 = float(np.float32(1.0 - ADAM_B1 ** ADAM_STEP))
    c2 = float(np.float32(1.0 - ADAM_B2 ** ADAM_STEP))

    def body(w_ref, g_ref, m_ref, v_ref, d_ref, mo_ref, vo_ref):
        g_t = g_ref[...]
        m_new = ADAM_B1 * m_ref[...] + (1.0 - ADAM_B1) * g_t
        v_new = ADAM_B2 * v_ref[...] + (1.0 - ADAM_B2) * (g_t * g_t)
        mo_ref[...] = m_new
        vo_ref[...] = v_new
        d_ref[...] = -ADAM_LR * ((m_new / c1) / (jnp.sqrt(v_new / c2) + ADAM_EPS) + ADAM_WD * w_ref[...])

    blk = pl.BlockSpec((None, tr, cols), lambda l, i: (l, i, 0))
    return pl.pallas_call(
        body, grid=(depth, rows // tr), in_specs=[blk] * 4, out_specs=[blk] * 3,
        out_shape=[jax.ShapeDtypeStruct(w.shape, F32)] * 3, name="adamw")(w, g, m, v)


def _local_step(x, positions, target, gains, exchange):
    t = x.shape[0]
    tm = 512
    inv_freq = ROPE_THETA ** (-jnp.arange(0, ROPE_DIM, 2, dtype=F32) / ROPE_DIM)
    lane = np.arange(LANES) % HEAD_DIM
    freq_row = jnp.where(lane < ROPE_DIM, inv_freq[lane % (ROPE_DIM // 2)], 0.0).astype(F32)[None, :]
    cos, sin = _rope_tables(positions.reshape(t, 1), freq_row)

    def hosted(tag, fn, *args, **kwargs):
        *out, got = fn(*args, comm=exchange.host(tag), **kwargs)
        if got is not None:
            exchange.hosted(tag, got)
        return out[0] if len(out) == 1 else out

    saved = []
    h1 = _norm_fwd(x, gains["pre_mix_norm"], 0, tm=tm)
    for l in range(DEPTH):
        proj = hosted(("fwd", l, "in_proj"), _mm_nn, h1, exchange.weight("w_in", l), tm=1024, name="in_proj")
        attn, lse = hosted(("fwd", l, "attn"), _attn_fwd, proj, cos, sin)
        conv_y = _conv_fwd(proj, exchange.weight("conv_w", l), l)
        merged = _merge_fwd(attn, conv_y, gains["attn_out_norm"], gains["conv_out_norm"], l, tm=tm)
        mix = hosted(("fwd", l, "out_proj"), _mm_nn, merged, exchange.weight("w_out", l), tm=1024, name="out_proj")
        x1, h2 = _resnorm_fwd(x, mix, gains["post_mix_norm"], l, gains["pre_ffn_norm"], l, tm=tm)
        g, u, act = hosted(("fwd", l, "gate_up"), _gate_up_swiglu, h2, exchange.weight("w_gate_up", l), tm=1024)
        f = hosted(("fwd", l, "down"), _mm_nn, act, exchange.weight("w_down", l), tm=1024, name="down")
        nxt = (gains["pre_mix_norm"], l + 1) if l + 1 < DEPTH else (None, None)
        x2, h1_next = _resnorm_fwd(x1, f, gains["post_ffn_norm"], l, *nxt, tm=tm)
        saved.append(dict(x=x, h1=h1, proj=proj, attn=attn, lse=lse, conv_y=conv_y, merged=merged, mix=mix,
                          x1=x1, h2=h2, g=g, u=u, act=act, f=f))
        x, h1 = x2, h1_next

    dres, loss_lanes = _loss_fwd_bwd(x, target, tm=tm)

    g_gain = {k: [None] * DEPTH for k in gains}
    g_conv = [None] * DEPTH
    _, df, _, g_gain["post_ffn_norm"][DEPTH - 1], _ = _norm_bwd(
        dres, None, (saved[-1]["f"], gains["post_ffn_norm"], DEPTH - 1), tm=tm)
    for l in reversed(range(DEPTH)):
        sv = saved[l]
        w = {k: exchange.weight(k, l) for k in MATRIX_NAMES + ("conv_w",)}
        dg, du = _down_dx_swiglu_bwd(df, w["w_down"], sv["g"], sv["u"], tm=1024, tko=FFN // 2)
        g_down = _mm_tn(sv["act"], df, 1, tka=256, name="down_dw")
        dh2 = hosted(("bwd", l, "gate_up_dx"), _mm_nt_pair, dg, du, w["w_gate_up"], tm=1024, name="gate_up_dx")
        g_gate_up = _mm_tn(sv["h2"], dg, N_CHIPS // 2, tka=512, name="gate_up_dw",
                           into=lax.empty(w["w_gate_up"].shape, F32))
        g_gate_up = _mm_tn(sv["h2"], du, N_CHIPS // 2, tka=512, name="gate_up_dw", into=g_gate_up,
                           shard0=N_CHIPS // 2)
        exchange.grads(l, "ffn", dict(w_down=g_down.reshape(N_CHIPS, FFN // N_CHIPS, D_MODEL), w_gate_up=g_gate_up))
        dx1, dmix, g_gain["pre_ffn_norm"][l], g_gain["post_mix_norm"][l] = hosted(
            ("bwd", l, "norm_mid"), _norm_bwd,
            dres, (dh2, sv["x1"], gains["pre_ffn_norm"], l), (sv["mix"], gains["post_mix_norm"], l), tm=tm)
        d_merged = hosted(("bwd", l, "out_proj_dx"), _mm_nt, dmix, w["w_out"], tm=1024, tko=D_MODEL, name="out_proj_dx")
        g_out = _mm_tn(sv["merged"], dmix, 1, tka=512, name="out_proj_dw")
        d_attn, delta, d_conv_y, g_gain["attn_out_norm"][l], g_gain["conv_out_norm"][l] = _merge_bwd(
            d_merged, sv["attn"], sv["conv_y"], gains["attn_out_norm"], gains["conv_out_norm"], l, tm=tm)
        dq, dk, dv = hosted(("bwd", l, "attn"), _attn_bwd, sv["proj"], cos, sin, d_attn, sv["lse"], delta)
        du, dgb, dgc, g_conv[l] = _conv_bwd(sv["proj"], w["conv_w"], l, d_conv_y)
        d_proj = jnp.concatenate([dq, dk, dv, du, dgb, dgc], axis=1)
        dh1 = hosted(("bwd", l, "in_proj_dx"), _mm_nt, d_proj, w["w_in"], tm=1024, tko=D_MODEL, name="in_proj_dx")
        g_in = _mm_tn(sv["h1"], d_proj, N_CHIPS, tka=512, name="in_proj_dw")
        exchange.grads(l, "mix", dict(w_out=g_out.reshape(N_CHIPS, D_MODEL // N_CHIPS, D_MODEL), w_in=g_in))
        below = (saved[l - 1]["f"], gains["post_ffn_norm"], l - 1) if l > 0 else None
        dres, df, g_gain["pre_mix_norm"][l], g_below = hosted(
            ("bwd", l, "norm_low"), _norm_bwd, dx1, (dh1, sv["x"], gains["pre_mix_norm"], l), below, tm=tm)
        if l > 0:
            g_gain["post_ffn_norm"][l - 1] = g_below

    g_gain = {k: jnp.concatenate(v, axis=0) for k, v in g_gain.items()}
    return loss_lanes, dres, g_gain, jnp.stack(g_conv, axis=0)


class _Exchange:
    GATHER_HOSTS = {"in_proj": ("w_out", 0), "attn": ("w_gate_up", 0), "gate_up": ("w_down", 0), "down": ("w_in", 1)}
    REDUCE_HOSTS = {"ffn": ("norm_mid", "attn", 0), "mix": ("norm_low", "gate_up_dx", -1)}

    def __init__(self, params, place):
        self.place = place
        self.slabs = {k: [_own_shard_slab(params[k], l, place, BF16) for l in range(DEPTH)] for k in MATRIX_NAMES}
        self.gathered = {k: [None] * DEPTH for k in MATRIX_NAMES}
        self.gathered["w_in"][0], self.conv_w = _run_comm(
            _gather_comm([self.slabs["w_in"][0]], _own_conv_slab(params["conv_w"], place)), "gather_first")
        self.full = {k: lax.empty(params[k].shape, F32) for k in MATRIX_NAMES}
        self.pending = {}
        self.raw = {}

    def weight(self, name, l):
        if name == "conv_w":
            return self.conv_w
        g = self.gathered[name][l]
        return g.reshape(1, g.shape[0] * g.shape[1], g.shape[2]) if name in ("w_out", "w_down") else g

    def host(self, tag):
        phase, l, kernel = tag
        if phase == "fwd":
            name, ahead = self.GATHER_HOSTS.get(kernel, (None, 0))
            return _gather_comm([self.slabs[name][l + ahead]]) if name and l + ahead < DEPTH else None
        if tag in self.pending:
            stage, _, _, arrays = self.pending[tag]
            return _halves_comm(arrays) if stage == "halves" else _partials_comm(arrays)
        return None

    def hosted(self, tag, results):
        phase, l, kernel = tag
        if phase == "fwd":
            name, ahead = self.GATHER_HOSTS[kernel]
            self.gathered[name][l + ahead] = results[0]
            return
        stage, gl, group, arrays = self.pending.pop(tag)
        names = list(self.raw[(gl, group)])
        if stage == "partials":
            self._finish_reduction(gl, names, arrays, results)
            return
        partials = [_add_halves(self.raw[(gl, group)][k], r, self.place) for k, r in zip(names, results)]
        _, ici_kernel, ici_layer = self.REDUCE_HOSTS[group]
        if gl + ici_layer >= 0:
            self.pending[("bwd", gl + ici_layer, ici_kernel)] = ("partials", gl, group, partials)
        else:
            self._finish_reduction(gl, names, partials, _run_comm(_partials_comm(partials), "exchange_partials"))

    def grads(self, l, group, grads):
        self.raw[(l, group)] = grads
        self.pending[("bwd", l, self.REDUCE_HOSTS[group][0])] = ("halves", l, group, [grads[k] for k in grads])

    def _finish_reduction(self, l, names, partials, others):
        for k, p, q in zip(names, partials, others):
            self.full[k] = _sum_partials(p, q, self.place, self.full[k], l)
        shared = _run_comm(_share_comm([self.full[k] for k in names], l), "share_halves")
        for k, g in zip(names, shared):
            self.full[k] = g


def kernel(x, positions, pre_mix_norm, w_in, conv_w, attn_out_norm, conv_out_norm, w_out, post_mix_norm, pre_ffn_norm, w_gate_up, w_down, post_ffn_norm, loss_target, m_pre_mix_norm, m_w_in, m_conv_w, m_attn_out_norm, m_conv_out_norm, m_w_out, m_post_mix_norm, m_pre_ffn_norm, m_w_gate_up, m_w_down, m_post_ffn_norm, v_pre_mix_norm, v_w_in, v_conv_w, v_attn_out_norm, v_conv_out_norm, v_w_out, v_post_mix_norm, v_pre_ffn_norm, v_w_gate_up, v_w_down, v_post_ffn_norm):
    params = dict(pre_mix_norm=pre_mix_norm, w_in=w_in, conv_w=conv_w, attn_out_norm=attn_out_norm,
                  conv_out_norm=conv_out_norm, w_out=w_out, post_mix_norm=post_mix_norm, pre_ffn_norm=pre_ffn_norm,
                  w_gate_up=w_gate_up, w_down=w_down, post_ffn_norm=post_ffn_norm)
    mom1 = dict(pre_mix_norm=m_pre_mix_norm, w_in=m_w_in, conv_w=m_conv_w, attn_out_norm=m_attn_out_norm,
                conv_out_norm=m_conv_out_norm, w_out=m_w_out, post_mix_norm=m_post_mix_norm,
                pre_ffn_norm=m_pre_ffn_norm, w_gate_up=m_w_gate_up, w_down=m_w_down, post_ffn_norm=m_post_ffn_norm)
    mom2 = dict(pre_mix_norm=v_pre_mix_norm, w_in=v_w_in, conv_w=v_conv_w, attn_out_norm=v_attn_out_norm,
                conv_out_norm=v_conv_out_norm, w_out=v_w_out, post_mix_norm=v_post_mix_norm,
                pre_ffn_norm=v_pre_ffn_norm, w_gate_up=v_w_gate_up, w_down=v_w_down, post_ffn_norm=v_post_ffn_norm)
    xi, yi, ci = lax.axis_index("x"), lax.axis_index("y"), lax.axis_index("c")
    place = jnp.stack([2 * xi + yi, ci]).astype(jnp.int32)

    exchange = _Exchange(params, place)
    gains = {k: params[k][:, None, :] for k in GAIN_NAMES}
    loss_lanes, grad_x, g_gain, g_conv = _local_step(x[0], positions[0], loss_target[0], gains, exchange)
    grad = dict(exchange.full)

    small = [g_gain[k].reshape(-1) for k in GAIN_NAMES] + [g_conv.reshape(-1), loss_lanes.reshape(-1)]
    sizes = [int(s.shape[0]) for s in small]
    flat = jnp.concatenate(small)
    loss_row = (sum(sizes) - LANES) // LANES
    rows = -(-flat.shape[0] // (8 * LANES)) * 8
    flat = jnp.pad(flat, (0, rows * LANES - flat.shape[0])).reshape(rows, LANES)
    total = _allreduce_small(flat, loss_row).reshape(-1)
    offsets = np.cumsum([0] + sizes)
    for i, k in enumerate(GAIN_NAMES):
        grad[k] = total[offsets[i]:offsets[i + 1]].reshape(params[k].shape)
    conv_all = total[offsets[6]:offsets[7]].reshape(DEPTH, N_CHIPS, 3, LANES)
    grad["conv_w"] = lax.dynamic_index_in_dim(conv_all, 2 * xi + yi, axis=1, keepdims=False)
    loss = total[offsets[7]]

    delta, new_m, new_v = {}, {}, {}
    for k in WEIGHT_ORDER:
        shape = params[k].shape
        as3 = (lambda a: a) if len(shape) == 3 else (lambda a: a[:, None, :])
        rows_k = shape[1] if len(shape) == 3 else 1
        tr = {1024: 512, 704: 352, 256: 256}.get(rows_k, rows_k)
        d, m, v = _adamw(as3(params[k]), as3(grad[k]), as3(mom1[k]), as3(mom2[k]), tr=tr)
        delta[k], new_m[k], new_v[k] = d.reshape(shape), m.reshape(shape), v.reshape(shape)

    return (loss, grad_x[None], *[grad[k] for k in WEIGHT_ORDER], *[delta[k] for k in WEIGHT_ORDER],
            *[new_m[k] for k in WEIGHT_ORDER], *[new_v[k] for k in WEIGHT_ORDER])
```

```python
import functools
from typing import Callable, NamedTuple

import numpy as np
import jax
import jax.numpy as jnp
from jax import lax
from jax.experimental import pallas as pl
from jax.experimental.pallas import tpu as pltpu

F32 = jnp.float32
BF16 = jnp.bfloat16
MESH = pl.DeviceIdType.MESH

D_MODEL = 1024
ATTN_W = 512
CONV_W = 512
HEAD_DIM = 64
ROPE_DIM = 16
ROPE_THETA = 500000.0
FFN = 2816
DEPTH = 4
RMS_EPS = 1e-6
NEG_INF = -1e30
N_CHIPS = 4
N_DEV = 8
LANES = 128
BF16_ROWS = 16
DILATIONS = (1, 4, 16)
BAND = 64
TQ = 128
WIN = TQ + 2 * BAND
SCALE = HEAD_DIM ** -0.5

ADAM_LR = 0.001
ADAM_B1 = 0.9
ADAM_B2 = 0.999
ADAM_EPS = 1e-08
ADAM_WD = 0.01
ADAM_STEP = 10

GAIN_NAMES = ("pre_mix_norm", "attn_out_norm", "conv_out_norm", "post_mix_norm", "pre_ffn_norm", "post_ffn_norm")
MATRIX_NAMES = ("w_in", "w_out", "w_gate_up", "w_down")
WEIGHT_ORDER = ("pre_mix_norm", "w_in", "conv_w", "attn_out_norm", "conv_out_norm", "w_out", "post_mix_norm",
                "pre_ffn_norm", "w_gate_up", "w_down", "post_ffn_norm")

ANY = pl.BlockSpec(memory_space=pl.ANY)
WHOLE_VMEM = pl.BlockSpec(memory_space=pltpu.VMEM)
LANE_CONTRACT = (((1,), (1,)), ((), ()))
ROW_CONTRACT = (((0,), (0,)), ((), ()))


def _const_spec(block, index):
    return pl.BlockSpec(block, lambda *_: index)


def _gain_spec(g3, l):
    return _const_spec((None, 1, g3.shape[-1]), (l, 0, 0))


class _Comm(NamedTuple):
    ins: tuple
    inouts: tuple
    out_shapes: tuple
    n_sems: int
    start: Callable
    finish: Callable


def _place():
    x, y, c = lax.axis_index("x"), lax.axis_index("y"), lax.axis_index("c")
    other_chips = [(1 - x, y), (x, 1 - y), (1 - x, 1 - y)]
    return x, y, c, other_chips


def _remote(src, dst, send_sem, recv_sem, to):
    return pltpu.make_async_remote_copy(src_ref=src, dst_ref=dst, send_sem=send_sem, recv_sem=recv_sem,
                                        device_id=to, device_id_type=MESH)


def _call(body, operands, *, name, grid, in_specs, out_specs, out_shape, scratch_shapes=(), comm=None):
    in_specs, out_specs, out_shape = list(in_specs), list(out_specs), list(out_shape)
    scratch_shapes = list(scratch_shapes)
    if comm is None:
        out = pl.pallas_call(body, grid=grid, in_specs=in_specs, out_specs=out_specs, out_shape=out_shape,
                             scratch_shapes=scratch_shapes, name=name)(*operands)
        return list(out), None
    n_in, n_out, n_scr = len(in_specs), len(out_shape), len(scratch_shapes)
    n_ci, n_cio, n_co = len(comm.ins), len(comm.inouts), len(comm.out_shapes)

    def hosted(*refs):
        refs = list(refs)
        ins, c_ins = refs[:n_in], refs[n_in:n_in + n_ci]
        base = n_in + n_ci + n_cio
        outs = refs[base:base + n_out]
        c_io = refs[base + n_out:base + n_out + n_cio]
        c_out = refs[base + n_out + n_cio:base + n_out + n_cio + n_co]
        scr = refs[base + n_out + n_cio + n_co:]
        send_sems, recv_sems = scr[n_scr], scr[n_scr + 1]
        if grid:
            first = functools.reduce(jnp.logical_and, [pl.program_id(a) == 0 for a in range(len(grid))])
            last = functools.reduce(jnp.logical_and, [pl.program_id(a) == grid[a] - 1 for a in range(len(grid))])
            pl.when(first)(lambda: comm.start(c_ins, c_io, c_out, send_sems, recv_sems))
            body(*ins, *outs, *scr[:n_scr])
            pl.when(last)(lambda: comm.finish(c_ins, c_io, c_out, send_sems, recv_sems))
        else:
            comm.start(c_ins, c_io, c_out, send_sems, recv_sems)
            body(*ins, *outs, *scr[:n_scr])
            comm.finish(c_ins, c_io, c_out, send_sems, recv_sems)

    res = pl.pallas_call(
        hosted, grid=grid, in_specs=in_specs + [ANY] * (n_ci + n_cio), out_specs=out_specs + [ANY] * (n_cio + n_co),
        out_shape=out_shape + [jax.ShapeDtypeStruct(a.shape, a.dtype) for a in comm.inouts] + list(comm.out_shapes),
        input_output_aliases={n_in + n_ci + i: n_out + i for i in range(n_cio)},
        scratch_shapes=scratch_shapes + [pltpu.SemaphoreType.DMA((comm.n_sems,))] * 2,
        name=name)(*operands, *comm.ins, *comm.inouts)
    return list(res[:n_out]), list(res[n_out:])


def _run_comm(comm, name):
    return _call(lambda: None, [], name=name, grid=(), in_specs=[], out_specs=[], out_shape=[], comm=comm)[1]


def _row_half(ref, lead, core, rows, align):
    hr = rows // 2
    return ref.at[(*lead, pl.ds(pl.multiple_of(core * hr, align), hr), slice(None))]


def _gather_comm(slabs, conv_slab=None):
    n = len(slabs)
    n_conv = 0 if conv_slab is None else 3

    def direct(ios, send, recv):
        x, y, c, chips = _place()
        copies = []
        for a in range(n):
            own = _row_half(ios[a], (2 * x + y,), c, slabs[a].shape[1], BF16_ROWS)
            copies += [_remote(own, own, send.at[a * 3 + j], recv.at[a * 3 + j], (*chip, c))
                       for j, chip in enumerate(chips)]
        if conv_slab is not None:
            own = ios[n].at[:, 2 * x + y]
            copies += [_remote(own, own, send.at[6 * n + j], recv.at[6 * n + j], (*chip, c))
                       for j, chip in enumerate(chips)]
        return copies

    def start(ins, ios, outs, send, recv):
        for cp in direct(ios, send, recv):
            cp.start()

    def finish(ins, ios, outs, send, recv):
        x, y, c, chips = _place()
        sibling = (x, y, 1 - c)
        passed = []
        for a in range(n):
            for j, chip in enumerate(chips):
                landed = _row_half(ios[a], (2 * chip[0] + chip[1],), c, slabs[a].shape[1], BF16_ROWS)
                _remote(landed, landed, send.at[a * 3 + j], recv.at[a * 3 + j], (*chip, c)).wait_recv()
                fwd = _remote(landed, landed, send.at[3 * n + a * 3 + j], recv.at[3 * n + a * 3 + j], sibling)
                fwd.start()
                passed.append(fwd)
        if conv_slab is not None:
            for j, chip in enumerate(chips):
                landed = ios[n].at[:, 2 * chip[0] + chip[1]]
                _remote(landed, landed, send.at[6 * n + j], recv.at[6 * n + j], (*chip, c)).wait_recv()
        for a in range(n):
            for j, chip in enumerate(chips):
                landed = _row_half(ios[a], (2 * chip[0] + chip[1],), 1 - c, slabs[a].shape[1], BF16_ROWS)
                _remote(landed, landed, send.at[3 * n + a * 3 + j], recv.at[3 * n + a * 3 + j], sibling).wait_recv()
        for cp in direct(ios, send, recv) + passed:
            cp.wait_send()

    inouts = tuple(slabs) + (() if conv_slab is None else (conv_slab,))
    return _Comm((), inouts, (), 6 * n + n_conv, start, finish)


def _halves_comm(grads):
    n = len(grads)

    def copies(ins, outs, send, recv):
        x, y, c, _ = _place()
        return [_remote(_row_half(ins[a], (slice(None),), 1 - c, grads[a].shape[1], 8), outs[a],
                        send.at[a], recv.at[a], (x, y, 1 - c)) for a in range(n)]

    def start(ins, ios, outs, send, recv):
        for cp in copies(ins, outs, send, recv):
            cp.start()

    def finish(ins, ios, outs, send, recv):
        for cp in copies(ins, outs, send, recv):
            cp.wait()

    out_shapes = tuple(jax.ShapeDtypeStruct((g.shape[0], g.shape[1] // 2, g.shape[2]), F32) for g in grads)
    return _Comm(tuple(grads), (), out_shapes, n, start, finish)


def _partials_comm(partials):
    n = len(partials)

    def copies(ins, outs, send, recv):
        x, y, c, chips = _place()
        return [_remote(ins[a].at[2 * chip[0] + chip[1]], outs[a].at[k], send.at[a * 3 + k], recv.at[a * 3 + k],
                        (*chip, c)) for a in range(n) for k, chip in enumerate(chips)]

    def start(ins, ios, outs, send, recv):
        for cp in copies(ins, outs, send, recv):
            cp.start()

    def finish(ins, ios, outs, send, recv):
        for cp in copies(ins, outs, send, recv):
            cp.wait()

    out_shapes = tuple(jax.ShapeDtypeStruct((3,) + p.shape[1:], BF16) for p in partials)
    return _Comm(tuple(partials), (), out_shapes, 3 * n, start, finish)


def _share_comm(grads, l):
    n = len(grads)

    def start(ins, ios, outs, send, recv):
        x, y, c, _ = _place()
        for a in range(n):
            mine = _row_half(ios[a], (l,), c, grads[a].shape[1], 8)
            _remote(mine, mine, send.at[a], recv.at[a], (x, y, 1 - c)).start()

    def finish(ins, ios, outs, send, recv):
        x, y, c, _ = _place()
        for a in range(n):
            theirs = _row_half(ios[a], (l,), 1 - c, grads[a].shape[1], 8)
            _remote(theirs, theirs, send.at[a], recv.at[a], (x, y, 1 - c)).wait()

    return _Comm((), tuple(grads), (), n, start, finish)


def _rms_fwd(x, g):
    r = lax.rsqrt(jnp.mean(x * x, axis=-1, keepdims=True) + RMS_EPS)
    return (x * r) * g


def _rms_bwd(x, g, dy):
    r = lax.rsqrt(jnp.mean(x * x, axis=-1, keepdims=True) + RMS_EPS)
    xh = x * r
    u = dy * g
    dx = r * (u - xh * jnp.mean(xh * u, axis=-1, keepdims=True))
    return dx, jnp.sum(dy * xh, axis=0, keepdims=True)


def _accumulate(ref, value, first):
    @pl.when(first)
    def _():
        ref[...] = value

    @pl.when(jnp.logical_not(first))
    def _():
        ref[...] += value


def _rope_coeffs(cos, sin):
    m = lax.broadcasted_iota(jnp.int32, cos.shape, 1) % HEAD_DIM
    a = jnp.where(m < ROPE_DIM, cos, 1.0)
    b = jnp.where(m < ROPE_DIM // 2, -sin, 0.0)
    c = jnp.where((m >= ROPE_DIM // 2) & (m < ROPE_DIM), sin, 0.0)
    return a, b, c


def _rope_apply(t, cos, sin):
    a, b, c = _rope_coeffs(cos, sin)
    n = t.shape[1]
    return a * t + b * pltpu.roll(t, n - ROPE_DIM // 2, 1) + c * pltpu.roll(t, ROPE_DIM // 2, 1)


def _rope_transpose(dt, cos, sin):
    a, b, c = _rope_coeffs(cos, sin)
    n = dt.shape[1]
    return a * dt + pltpu.roll(b * dt, ROPE_DIM // 2, 1) + pltpu.roll(c * dt, n - ROPE_DIM // 2, 1)


def _mm_nn(a, w, *, tm, name, comm=None):
    t, k = a.shape
    s_n, k2, n = w.shape
    assert k == k2 and t % tm == 0

    def body(a_ref, w_ref, o_ref):
        o_ref[...] = jnp.dot(a_ref[...], w_ref[...], preferred_element_type=F32)

    out, got = _call(
        body, [a, w], name=name, grid=(t // tm, s_n),
        in_specs=[pl.BlockSpec((tm, k), lambda i, s: (i, 0)), pl.BlockSpec((None, k, n), lambda i, s: (s, 0, 0))],
        out_specs=[pl.BlockSpec((tm, n), lambda i, s: (i, s))],
        out_shape=[jax.ShapeDtypeStruct((t, s_n * n), F32)], comm=comm)
    return out[0], got


def _mm_nt(a, w, *, tm, tko, name, comm=None):
    t, sn = a.shape
    s_n, ko, n = w.shape
    assert sn == s_n * n and t % tm == 0 and ko % tko == 0

    def body(a_ref, w_ref, o_ref):
        acc = lax.dot_general(a_ref[...], w_ref[...], (((1,), (1,)), ((), ())), preferred_element_type=F32)
        if s_n == 1:
            o_ref[...] = acc
        else:
            _accumulate(o_ref, acc, pl.program_id(2) == 0)

    out, got = _call(
        body, [a, w], name=name, grid=(t // tm, ko // tko, s_n),
        in_specs=[pl.BlockSpec((tm, n), lambda i, j, s: (i, s)),
                  pl.BlockSpec((None, tko, n), lambda i, j, s: (s, j, 0))],
        out_specs=[pl.BlockSpec((tm, tko), lambda i, j, s: (i, j))],
        out_shape=[jax.ShapeDtypeStruct((t, ko), F32)], comm=comm)
    return out[0], got


def _mm_nt_pair(a0, a1, w, *, tm, name, comm=None):
    t = a0.shape[0]
    s_n, ko, n = w.shape
    half = s_n // 2
    assert a0.shape == a1.shape == (t, half * n) and t % tm == 0

    def body(a0_ref, a1_ref, w0_ref, w1_ref, o_ref):
        acc = (lax.dot_general(a0_ref[...], w0_ref[...], LANE_CONTRACT, preferred_element_type=F32)
               + lax.dot_general(a1_ref[...], w1_ref[...], LANE_CONTRACT, preferred_element_type=F32))
        _accumulate(o_ref, acc, pl.program_id(1) == 0)

    a_spec = pl.BlockSpec((tm, n), lambda i, s: (i, s))
    out, got = _call(
        body, [a0, a1, w, w], name=name, grid=(t // tm, half),
        in_specs=[a_spec, a_spec, pl.BlockSpec((None, ko, n), lambda i, s: (s, 0, 0)),
                  pl.BlockSpec((None, ko, n), lambda i, s: (half + s, 0, 0))],
        out_specs=[pl.BlockSpec((tm, ko), lambda i, s: (i, 0))],
        out_shape=[jax.ShapeDtypeStruct((t, ko), F32)], comm=comm)
    return out[0], got


def _mm_tn(a, b, s_n, *, tka, name, into=None, shard0=0):
    t, ka = a.shape
    n = b.shape[1] // s_n
    assert b.shape[0] == t and ka % tka == 0

    def body(a_ref, b_ref, *rest):
        rest[-1][...] = lax.dot_general(a_ref[...], b_ref[...], ROW_CONTRACT, preferred_element_type=F32)

    operands, in_specs, aliases = [a, b], [pl.BlockSpec((t, tka), lambda i, s: (0, i)),
                                           pl.BlockSpec((t, n), lambda i, s: (0, s))], {}
    out_shape = jax.ShapeDtypeStruct((s_n, ka, n), F32)
    if into is not None:
        operands, in_specs, aliases = operands + [into], in_specs + [ANY], {2: 0}
        out_shape = jax.ShapeDtypeStruct(into.shape, F32)
    return pl.pallas_call(
        body, grid=(ka // tka, s_n), in_specs=in_specs,
        out_specs=pl.BlockSpec((None, tka, n), lambda i, s: (shard0 + s, i, 0)),
        out_shape=out_shape, input_output_aliases=aliases, name=name)(*operands)


def _gate_up_swiglu(h, w, *, tm, comm=None):
    t, k = h.shape
    s_n, _, n = w.shape
    half = s_n // 2

    def body(h_ref, wg_ref, wu_ref, g_ref, u_ref, a_ref):
        g_ref[...] = jnp.dot(h_ref[...], wg_ref[...], preferred_element_type=F32)
        u_ref[...] = jnp.dot(h_ref[...], wu_ref[...], preferred_element_type=F32)
        g = g_ref[...]
        a_ref[...] = (g * jax.nn.sigmoid(g) * u_ref[...]).astype(BF16)

    col = pl.BlockSpec((tm, n), lambda i, j: (i, j))
    out, got = _call(
        body, [h, w, w], name="gate_up", grid=(t // tm, half),
        in_specs=[pl.BlockSpec((tm, k), lambda i, j: (i, 0)), pl.BlockSpec((None, k, n), lambda i, j: (j, 0, 0)),
                  pl.BlockSpec((None, k, n), lambda i, j: (half + j, 0, 0))],
        out_specs=[col, col, col],
        out_shape=[jax.ShapeDtypeStruct((t, half * n), F32)] * 2 + [jax.ShapeDtypeStruct((t, half * n), BF16)],
        comm=comm)
    return out[0], out[1], out[2], got


def _down_dx_swiglu_bwd(df, w, g, u, *, tm, tko):
    t, k = df.shape
    _, ko, _ = w.shape
    assert t % tm == 0 and ko % tko == 0

    def body(df_ref, w_ref, g_ref, u_ref, dg_ref, du_ref):
        d = lax.dot_general(df_ref[...], w_ref[...], LANE_CONTRACT, preferred_element_type=F32)
        gg = g_ref[...]
        sig = jax.nn.sigmoid(gg)
        dg_ref[...] = (d * u_ref[...] * (sig * (1.0 + gg * (1.0 - sig)))).astype(BF16)
        du_ref[...] = (d * (gg * sig)).astype(BF16)

    col = pl.BlockSpec((tm, tko), lambda i, j: (i, j))
    return pl.pallas_call(
        body, grid=(t // tm, ko // tko),
        in_specs=[pl.BlockSpec((tm, k), lambda i, j: (i, 0)), pl.BlockSpec((None, tko, k), lambda i, j: (0, j, 0)),
                  col, col],
        out_specs=[col, col], out_shape=[jax.ShapeDtypeStruct((t, ko), BF16)] * 2, name="down_dx")(df, w, g, u)


def _rope_tables(positions_col, inv_freq_row):
    t = positions_col.shape[0]

    def body(pos_ref, f_ref, cos_ref, sin_ref):
        ang = pos_ref[...].astype(F32) * f_ref[...]
        cos_ref[...] = jnp.cos(ang)
        sin_ref[...] = jnp.sin(ang)

    return pl.pallas_call(
        body, out_shape=[jax.ShapeDtypeStruct((t, LANES), F32)] * 2, name="rope_tables")(positions_col, inv_freq_row)


def _norm_fwd(x, g3, l, *, tm):
    t, w = x.shape

    def body(x_ref, g_ref, h_ref):
        h_ref[...] = _rms_fwd(x_ref[...], g_ref[...]).astype(BF16)

    return pl.pallas_call(
        body, grid=(t // tm,),
        in_specs=[pl.BlockSpec((tm, w), lambda i: (i, 0)), _gain_spec(g3, l)],
        out_specs=pl.BlockSpec((tm, w), lambda i: (i, 0)),
        out_shape=jax.ShapeDtypeStruct((t, w), BF16), name="norm_fwd")(x, g3)


def _resnorm_fwd(x, y, g_post3, l_post, g_next3, l_next, *, tm):
    t, w = x.shape
    with_next = g_next3 is not None
    row = pl.BlockSpec((tm, w), lambda i: (i, 0))

    def body(x_ref, y_ref, gp_ref, *rest):
        x_new = x_ref[...] + _rms_fwd(y_ref[...], gp_ref[...])
        if with_next:
            gn_ref, xo_ref, h_ref = rest
            h_ref[...] = _rms_fwd(x_new, gn_ref[...]).astype(BF16)
        else:
            (xo_ref,) = rest
        xo_ref[...] = x_new

    ins = [x, y, g_post3] + ([g_next3] if with_next else [])
    in_specs = [row, row, _gain_spec(g_post3, l_post)] + ([_gain_spec(g_next3, l_next)] if with_next else [])
    out_shape = [jax.ShapeDtypeStruct((t, w), F32)] + ([jax.ShapeDtypeStruct((t, w), BF16)] if with_next else [])
    out = pl.pallas_call(body, grid=(t // tm,), in_specs=in_specs, out_specs=[row] * len(out_shape),
                         out_shape=out_shape, name="resnorm_fwd")(*ins)
    return (out[0], out[1]) if with_next else (out[0], None)


def _conv_fwd(proj, conv_w, l):
    t = proj.shape[0]
    col0 = 3 * ATTN_W // LANES

    def body(u_ref, gb_ref, gc_ref, w_ref, y_ref):
        c = gc_ref[...] * u_ref[...]
        row = lax.broadcasted_iota(jnp.int32, c.shape, 0)
        c_prev = jnp.where(row == 0, 0.0, pltpu.roll(c, 1, 0))
        c_next = jnp.where(row == t - 1, 0.0, pltpu.roll(c, t - 1, 0))
        w = w_ref[...]
        y_ref[...] = gb_ref[...] * (w[0:1] * c_prev + w[1:2] * c + w[2:3] * c_next)

    nj = CONV_W // LANES
    cols = lambda base: pl.BlockSpec((t, LANES), lambda j: (0, base + j))
    return pl.pallas_call(
        body, grid=(nj,),
        in_specs=[cols(col0), cols(col0 + nj), cols(col0 + 2 * nj),
                  pl.BlockSpec((None, None, 3, LANES), lambda j: (l, j, 0, 0))],
        out_specs=pl.BlockSpec((t, LANES), lambda j: (0, j)),
        out_shape=jax.ShapeDtypeStruct((t, CONV_W), F32), name="conv_fwd")(proj, proj, proj, conv_w)


def _merge_fwd(attn, conv_y, ga3, gc3, l, *, tm):
    t = attn.shape[0]
    row = pl.BlockSpec((tm, ATTN_W), lambda i: (i, 0))

    def body(a_ref, c_ref, ga_ref, gc_ref, m_ref):
        m_ref[:, :ATTN_W] = _rms_fwd(a_ref[...], ga_ref[...]).astype(BF16)
        m_ref[:, ATTN_W:] = _rms_fwd(c_ref[...], gc_ref[...]).astype(BF16)

    return pl.pallas_call(
        body, grid=(t // tm,),
        in_specs=[row, row, _gain_spec(ga3, l), _gain_spec(gc3, l)],
        out_specs=pl.BlockSpec((tm, D_MODEL), lambda i: (i, 0)),
        out_shape=jax.ShapeDtypeStruct((t, D_MODEL), BF16), name="merge_fwd")(attn, conv_y, ga3, gc3)


def _loss_fwd_bwd(y, target, *, tm):
    t, w = y.shape
    row = pl.BlockSpec((tm, w), lambda i: (i, 0))

    def body(y_ref, t_ref, dy_ref, loss_ref):
        e = y_ref[...] - t_ref[...]
        dy_ref[...] = e * (1.0 / w)
        sq = jnp.sum(e * e, axis=0, keepdims=True) * (0.5 / w)
        part = sq[:, :LANES]
        for j in range(1, w // LANES):
            part = part + sq[:, j * LANES:(j + 1) * LANES]
        _accumulate(loss_ref, part, pl.program_id(0) == 0)

    return pl.pallas_call(
        body, grid=(t // tm,), in_specs=[row, row],
        out_specs=[row, _const_spec((1, LANES), (0, 0))],
        out_shape=[jax.ShapeDtypeStruct((t, w), F32), jax.ShapeDtypeStruct((1, LANES), F32)], name="loss")(y, target)


def _tile_rows(t, nt, lb, d):
    r = t // nt
    q0 = (t % nt) * TQ
    m0 = jnp.clip(q0 - BAND, 0, lb - WIN)
    if d == 1:
        return pl.ds(pl.multiple_of(q0, TQ), TQ), pl.ds(pl.multiple_of(m0, BAND), WIN), m0 - q0
    return pl.ds(r + d * q0, TQ, stride=d), pl.ds(r + d * m0, WIN, stride=d), m0 - q0


def _for_row_chunks(t, fn, chunk=512):
    def step(i, carry):
        fn(pl.ds(pl.multiple_of(i * chunk, chunk), chunk))
        return carry

    lax.fori_loop(0, t // chunk, step, 0)


def _rope_into(dst_ref, src_ref, cos_ref, sin_ref, t, scale=1.0):
    def chunk(rows):
        dst_ref[rows, :] = _rope_apply(src_ref[rows, :], cos_ref[rows, :], sin_ref[rows, :]) * scale

    _for_row_chunks(t, chunk)


WINDOW_OFFSETS = (-BAND, 0, -2 * BAND)


def _fill_band_bias(bias_ref):
    rel0 = (lax.broadcasted_iota(jnp.int32, (2 * TQ, WIN), 1)
            - lax.broadcasted_iota(jnp.int32, (2 * TQ, WIN), 0) % TQ)
    for j, off in enumerate(WINDOW_OFFSETS):
        rel = rel0 + off
        bias_ref[j] = jnp.where((rel >= -BAND) & (rel <= BAND), 0.0, NEG_INF)


def _fill_sequence_bias(bias_ref):
    rel = (lax.broadcasted_iota(jnp.int32, (2 * WIN, WIN), 1) - lax.broadcasted_iota(jnp.int32, (2 * WIN, WIN), 0) % WIN)
    bias_ref[...] = jnp.where((rel >= -BAND) & (rel <= BAND), 0.0, NEG_INF)


def _band_bias(bias_ref, off):
    return bias_ref[jnp.where(off == WINDOW_OFFSETS[0], 0, jnp.where(off == WINDOW_OFFSETS[1], 1, 2))]


def _stack_heads(a, first_head):
    return jnp.concatenate([jnp.where(first_head, a, 0.0), jnp.where(first_head, 0.0, a)], axis=0)


def _unstack_heads(a2, first_head):
    n = a2.shape[0] // 2
    return jnp.where(first_head, a2[:n], a2[n:])


def _attn_fwd(proj, cos, sin, comm=None):
    t = proj.shape[0]
    npair = ATTN_W // LANES

    def body(q_ref, k_ref, v_ref, cos_ref, sin_ref, o_ref, lse_ref, qs, ks, o1, o2, l0, l1, l2, m1, m2, bias, bias_seq):
        _rope_into(qs, q_ref, cos_ref, sin_ref, t, SCALE)
        _rope_into(ks, k_ref, cos_ref, sin_ref, t)
        _fill_band_bias(bias)
        _fill_sequence_bias(bias_seq)
        outs, dens, maxs = (o_ref, o1, o2), (l0, l1, l2), (lse_ref, m1, m2)

        def softmax_tile(b, qrows, krows, n_q, band_bias):
            first_head = lax.broadcasted_iota(jnp.int32, (n_q, LANES), 1) < HEAD_DIM
            q2 = _stack_heads(qs[qrows, :], first_head).astype(BF16)
            kw = ks[krows, :].astype(BF16)
            vw = jnp.concatenate([v_ref[krows, :].astype(BF16), jnp.ones((WIN, LANES), BF16)], axis=1)
            s = lax.dot_general(q2, kw, LANE_CONTRACT, preferred_element_type=F32) + band_bias
            m = jnp.max(s, axis=-1, keepdims=True)
            pv = jnp.dot(jnp.exp(s - m).astype(BF16), vw, preferred_element_type=F32)
            outs[b][qrows, :] = _unstack_heads(pv[:, :LANES], first_head)
            dens[b][qrows, :] = _unstack_heads(pv[:, LANES:], first_head)
            maxs[b][qrows, :] = _unstack_heads(jnp.broadcast_to(m, (2 * n_q, LANES)), first_head)

        for b, d in enumerate(DILATIONS):
            lb = t // d
            if lb == WIN:
                def sequence(r, carry, b=b, d=d):
                    rows = pl.ds(r, WIN, stride=d)
                    softmax_tile(b, rows, rows, WIN, bias_seq[...])
                    return carry

                lax.fori_loop(0, d, sequence, 0, unroll=4)
                continue
            nt = lb // TQ

            def tile(ti, carry, b=b, d=d, lb=lb, nt=nt):
                qrows, krows, off = _tile_rows(ti, nt, lb, d)
                softmax_tile(b, qrows, krows, TQ, _band_bias(bias, off))
                return carry

            lax.fori_loop(0, d * nt, tile, 0, unroll=8)

        def finish(rows):
            ms = [m_b[rows, :] for m_b in maxs]
            m_all = jnp.maximum(jnp.maximum(ms[0], ms[1]), ms[2])
            ws = [jnp.exp(m_b - m_all) for m_b in ms]
            den = ws[0] * dens[0][rows, :] + ws[1] * dens[1][rows, :] + ws[2] * dens[2][rows, :]
            num = ws[0] * outs[0][rows, :] + ws[1] * outs[1][rows, :] + ws[2] * outs[2][rows, :]
            o_ref[rows, :] = num / den
            lse_ref[rows, :] = m_all + jnp.log(den)

        _for_row_chunks(t, finish, 256)

    cols = lambda base: pl.BlockSpec((t, LANES), lambda g: (0, base + g))
    out, got = _call(
        body, [proj, proj, proj, cos, sin], name="attn_fwd", grid=(npair,),
        in_specs=[cols(0), cols(npair), cols(2 * npair), WHOLE_VMEM, WHOLE_VMEM],
        out_specs=[cols(0), cols(0)],
        out_shape=[jax.ShapeDtypeStruct((t, ATTN_W), F32)] * 2,
        scratch_shapes=[pltpu.VMEM((t, LANES), F32)] * 9 + [pltpu.VMEM((len(WINDOW_OFFSETS), 2 * TQ, WIN), F32),
                                                            pltpu.VMEM((2 * WIN, WIN), F32)],
        comm=comm)
    return out[0], out[1], got


def _attn_bwd(proj, cos, sin, d_attn, lse, delta, comm=None):
    t = proj.shape[0]
    npair = ATTN_W // LANES

    def body(q_ref, k_ref, v_ref, cos_ref, sin_ref, do_ref, l_ref, dl_ref, dq_ref, dk_ref, dv_ref,
             qs, ks, dq_acc, dk_acc, dv_acc, bias, bias_seq):
        _rope_into(qs, q_ref, cos_ref, sin_ref, t, SCALE)
        _rope_into(ks, k_ref, cos_ref, sin_ref, t)
        _fill_band_bias(bias)
        _fill_sequence_bias(bias_seq)
        dq_acc[...] = jnp.zeros(dq_acc.shape, F32)
        dk_acc[...] = jnp.zeros(dk_acc.shape, F32)
        dv_acc[...] = jnp.zeros(dv_acc.shape, F32)
        def stack_column(a):
            return jnp.concatenate([a[:, 0:1], a[:, HEAD_DIM:HEAD_DIM + 1]], axis=0)

        def grad_tile(qrows, krows, n_q, band_bias):
            first_head = lax.broadcasted_iota(jnp.int32, (n_q, LANES), 1) < HEAD_DIM
            q2 = _stack_heads(qs[qrows, :], first_head).astype(BF16)
            do2 = _stack_heads(do_ref[qrows, :], first_head).astype(BF16)
            kw = ks[krows, :].astype(BF16)
            vw = v_ref[krows, :].astype(BF16)
            s = lax.dot_general(q2, kw, LANE_CONTRACT, preferred_element_type=F32) + band_bias
            p = jnp.exp(s - stack_column(l_ref[qrows, :]))
            dp = lax.dot_general(do2, vw, LANE_CONTRACT, preferred_element_type=F32)
            ds = (p * (dp - stack_column(dl_ref[qrows, :]))).astype(BF16)
            dq2 = jnp.dot(ds, kw, preferred_element_type=F32)
            dq_acc[qrows, :] += _unstack_heads(dq2, first_head) * SCALE
            dk_acc[krows, :] += lax.dot_general(ds, q2, ROW_CONTRACT, preferred_element_type=F32)
            dv_acc[krows, :] += lax.dot_general(p.astype(BF16), do2, ROW_CONTRACT, preferred_element_type=F32)

        for d in DILATIONS:
            lb = t // d
            if lb == WIN:
                def sequence(r, carry, d=d):
                    rows = pl.ds(r, WIN, stride=d)
                    grad_tile(rows, rows, WIN, bias_seq[...])
                    return carry

                lax.fori_loop(0, d, sequence, 0, unroll=2)
                continue
            nt = lb // TQ

            def tile(ti, carry, d=d, lb=lb, nt=nt):
                qrows, krows, off = _tile_rows(ti, nt, lb, d)
                grad_tile(qrows, krows, TQ, _band_bias(bias, off))
                return carry

            lax.fori_loop(0, d * nt, tile, 0, unroll=4)

        def finish(rows):
            dq_ref[rows, :] = _rope_transpose(dq_acc[rows, :], cos_ref[rows, :], sin_ref[rows, :]).astype(BF16)
            dk_ref[rows, :] = _rope_transpose(dk_acc[rows, :], cos_ref[rows, :], sin_ref[rows, :]).astype(BF16)
            dv_ref[rows, :] = dv_acc[rows, :].astype(BF16)

        _for_row_chunks(t, finish)

    cols = lambda base: pl.BlockSpec((t, LANES), lambda g: (0, base + g))
    out, got = _call(
        body, [proj, proj, proj, cos, sin, d_attn, lse, delta], name="attn_bwd", grid=(npair,),
        in_specs=[cols(0), cols(npair), cols(2 * npair), WHOLE_VMEM, WHOLE_VMEM, cols(0), cols(0), cols(0)],
        out_specs=[cols(0)] * 3,
        out_shape=[jax.ShapeDtypeStruct((t, ATTN_W), BF16)] * 3,
        scratch_shapes=[pltpu.VMEM((t, LANES), F32)] * 5 + [pltpu.VMEM((len(WINDOW_OFFSETS), 2 * TQ, WIN), F32),
                                                            pltpu.VMEM((2 * WIN, WIN), F32)],
        comm=comm)
    return out[0], out[1], out[2], got


def _norm_bwd(dres, pre, post, *, tm, comm=None):
    t, w = dres.shape
    row = pl.BlockSpec((tm, w), lambda i: (i, 0))
    gsum = _const_spec((1, w), (0, 0))
    ins, in_specs, out_shape, out_specs = [dres], [row], [], []
    if pre is not None:
        dh, x, g3, l = pre
        ins += [dh, x, g3]
        in_specs += [row, row, _gain_spec(g3, l)]
        out_shape += [jax.ShapeDtypeStruct((t, w), F32), jax.ShapeDtypeStruct((1, w), F32)]
        out_specs += [row, gsum]
    if post is not None:
        y, g3, l = post
        ins += [y, g3]
        in_specs += [row, _gain_spec(g3, l)]
        out_shape += [jax.ShapeDtypeStruct((t, w), BF16), jax.ShapeDtypeStruct((1, w), F32)]
        out_specs += [row, gsum]
    n_in = len(ins)

    def body(*refs):
        first = pl.program_id(0) == 0
        ins_r, outs_r = list(refs[:n_in]), list(refs[n_in:])
        d = ins_r.pop(0)[...]
        if pre is not None:
            dh_ref, x_ref, g_ref = ins_r[:3]
            ins_r = ins_r[3:]
            dx, dg = _rms_bwd(x_ref[...], g_ref[...], dh_ref[...])
            d = d + dx
            outs_r.pop(0)[...] = d
            _accumulate(outs_r.pop(0), dg, first)
        if post is not None:
            y_ref, g_ref = ins_r
            dy, dg = _rms_bwd(y_ref[...], g_ref[...], d)
            outs_r.pop(0)[...] = dy.astype(BF16)
            _accumulate(outs_r.pop(0), dg, first)

    out, got = _call(body, ins, name="norm_bwd", grid=(t // tm,), in_specs=in_specs, out_specs=out_specs,
                     out_shape=out_shape, comm=comm)
    d_new, dg_pre = (out.pop(0), out.pop(0)) if pre is not None else (None, None)
    dy, dg_post = (out.pop(0), out.pop(0)) if post is not None else (None, None)
    return d_new, dy, dg_pre, dg_post, got


def _merge_bwd(d_merged, attn, conv_y, ga3, gc3, l, *, tm):
    t = attn.shape[0]
    row = pl.BlockSpec((tm, ATTN_W), lambda i: (i, 0))
    gsum = _const_spec((1, ATTN_W), (0, 0))

    def body(dma_ref, dmc_ref, a_ref, c_ref, ga_ref, gc_ref, da_ref, dl_ref, dc_ref, dga_ref, dgc_ref):
        first = pl.program_id(0) == 0
        attn_t = a_ref[...]
        da, dga = _rms_bwd(attn_t, ga_ref[...], dma_ref[...])
        dc, dgc = _rms_bwd(c_ref[...], gc_ref[...], dmc_ref[...])
        da_ref[...] = da
        dc_ref[...] = dc
        same_head = (lax.broadcasted_iota(jnp.int32, (ATTN_W, ATTN_W), 0) // HEAD_DIM
                     == lax.broadcasted_iota(jnp.int32, (ATTN_W, ATTN_W), 1) // HEAD_DIM).astype(BF16)
        rest = da * attn_t
        total = jnp.zeros(rest.shape, F32)
        for _ in range(3):
            term = rest.astype(BF16)
            total = total + jnp.dot(term, same_head, preferred_element_type=F32)
            rest = rest - term.astype(F32)
        dl_ref[...] = total
        _accumulate(dga_ref, dga, first)
        _accumulate(dgc_ref, dgc, first)

    return pl.pallas_call(
        body, grid=(t // tm,),
        in_specs=[pl.BlockSpec((tm, ATTN_W), lambda i: (i, 0)), pl.BlockSpec((tm, CONV_W), lambda i: (i, 1)),
                  row, row, _gain_spec(ga3, l), _gain_spec(gc3, l)],
        out_specs=[row, row, row, gsum, gsum],
        out_shape=[jax.ShapeDtypeStruct((t, ATTN_W), F32)] * 3 + [jax.ShapeDtypeStruct((1, ATTN_W), F32)] * 2,
        name="merge_bwd")(d_merged, d_merged, attn, conv_y, ga3, gc3)


def _conv_bwd(proj, conv_w, l, d_conv_y):
    t = proj.shape[0]
    col0 = 3 * ATTN_W // LANES
    nj = CONV_W // LANES

    def body(u_ref, gb_ref, gc_ref, w_ref, dy_ref, du_ref, dgb_ref, dgc_ref, dw_ref):
        u, gc, dy = u_ref[...], gc_ref[...], dy_ref[...]
        row = lax.broadcasted_iota(jnp.int32, u.shape, 0)
        down = lambda a: jnp.where(row == 0, 0.0, pltpu.roll(a, 1, 0))
        up = lambda a: jnp.where(row == t - 1, 0.0, pltpu.roll(a, t - 1, 0))
        w = w_ref[...]
        c = gc * u
        c_prev, c_next = down(c), up(c)
        dgb_ref[...] = (dy * (w[0:1] * c_prev + w[1:2] * c + w[2:3] * c_next)).astype(BF16)
        dz = dy * gb_ref[...]
        dc = w[0:1] * up(dz) + w[1:2] * dz + w[2:3] * down(dz)
        du_ref[...] = (dc * gc).astype(BF16)
        dgc_ref[...] = (dc * u).astype(BF16)
        dw_ref[0:1, :] = jnp.sum(dz * c_prev, axis=0, keepdims=True)
        dw_ref[1:2, :] = jnp.sum(dz * c, axis=0, keepdims=True)
        dw_ref[2:3, :] = jnp.sum(dz * c_next, axis=0, keepdims=True)

    cols = lambda base: pl.BlockSpec((t, LANES), lambda j: (0, base + j))
    return pl.pallas_call(
        body, grid=(nj,),
        in_specs=[cols(col0), cols(col0 + nj), cols(col0 + 2 * nj),
                  pl.BlockSpec((None, None, 3, LANES), lambda j: (l, j, 0, 0)), cols(0)],
        out_specs=[cols(0)] * 3 + [pl.BlockSpec((None, 3, LANES), lambda j: (j, 0, 0))],
        out_shape=[jax.ShapeDtypeStruct((t, CONV_W), BF16)] * 3 + [jax.ShapeDtypeStruct((nj, 3, LANES), F32)],
        name="conv_bwd")(proj, proj, proj, conv_w, d_conv_y)


def _own_shard_slab(w, l, place, dtype):
    _, rows, cols = w.shape
    tr = rows if rows <= 704 else 512
    assert rows % tr == 0

    def body(p_ref, w_ref, o_ref):
        del p_ref
        o_ref[...] = w_ref[...].astype(dtype)

    grid_spec = pltpu.PrefetchScalarGridSpec(
        num_scalar_prefetch=1, grid=(rows // tr,),
        in_specs=[pl.BlockSpec((None, tr, cols), lambda i, p: (l, i, 0))],
        out_specs=pl.BlockSpec((None, tr, cols), lambda i, p: (p[0], i, 0)))
    return pl.pallas_call(body, grid_spec=grid_spec, name="own_shard_slab",
                          out_shape=jax.ShapeDtypeStruct((N_CHIPS, rows, cols), dtype))(place, w)


def _own_conv_slab(w, place):
    depth = w.shape[0]

    def body(p_ref, w_ref, o_ref):
        del p_ref
        o_ref[...] = w_ref[...]

    grid_spec = pltpu.PrefetchScalarGridSpec(
        num_scalar_prefetch=1, grid=(depth,),
        in_specs=[pl.BlockSpec((None, 3, LANES), lambda l, p: (l, 0, 0))],
        out_specs=pl.BlockSpec((None, None, 3, LANES), lambda l, p: (l, p[0], 0, 0)))
    return pl.pallas_call(body, grid_spec=grid_spec, name="own_conv_slab",
                          out_shape=jax.ShapeDtypeStruct((depth, N_CHIPS, 3, LANES), F32))(place, w)


def _add_halves(grad, got, place):
    s_n, rows, cols = grad.shape
    hr = rows // 2

    def body(p_ref, g_ref, r_ref, o_ref):
        del p_ref
        o_ref[...] = (g_ref[...] + r_ref[...]).astype(BF16)

    grid_spec = pltpu.PrefetchScalarGridSpec(
        num_scalar_prefetch=1, grid=(s_n,),
        in_specs=[pl.BlockSpec((None, hr, cols), lambda s, p: (s, p[1], 0)),
                  pl.BlockSpec((None, hr, cols), lambda s, p: (s, 0, 0))],
        out_specs=pl.BlockSpec((None, hr, cols), lambda s, p: (s, 0, 0)))
    return pl.pallas_call(body, grid_spec=grid_spec, out_shape=jax.ShapeDtypeStruct((s_n, hr, cols), BF16),
                          name="add_halves")(place, grad, got)


def _sum_partials(partial, got, place, acc, l):
    _, hr, cols = partial.shape

    def body(p_ref, mine_ref, got_ref, acc_ref, o_ref):
        del p_ref, acc_ref
        total = mine_ref[...].astype(F32)
        for k in range(3):
            total = total + got_ref[k].astype(F32)
        o_ref[...] = total

    grid_spec = pltpu.PrefetchScalarGridSpec(
        num_scalar_prefetch=1, grid=(1,),
        in_specs=[pl.BlockSpec((None, hr, cols), lambda i, p: (p[0], 0, 0)),
                  pl.BlockSpec((3, hr, cols), lambda i, p: (0, 0, 0)), ANY],
        out_specs=pl.BlockSpec((None, hr, cols), lambda i, p: (l, p[1], 0)))
    return pl.pallas_call(body, grid_spec=grid_spec, out_shape=jax.ShapeDtypeStruct(acc.shape, F32),
                          input_output_aliases={3: 0}, name="sum_partials")(place, partial, got, acc)


def _allreduce_small(vec, loss_row):
    rows = vec.shape[0]

    def body(v_ref, o_ref, slots, send_sems, recv_sems):
        x, y, c, _ = _place()
        me = 4 * x + 2 * y + c
        slots[me] = v_ref[...]
        copies = []
        for k in range(1, N_DEV):
            flip = lambda v, bit: 1 - v if bit else v
            peer = (flip(x, k & 4), flip(y, k & 2), flip(c, k & 1))
            copies.append(_remote(v_ref, slots.at[me], send_sems.at[k - 1], recv_sems.at[k - 1], peer))
        for cp in copies:
            cp.start()
        for k in range(1, N_DEV):
            flip = lambda v, bit: 1 - v if bit else v
            peer_id = 4 * flip(x, k & 4) + 2 * flip(y, k & 2) + flip(c, k & 1)
            _remote(v_ref, slots.at[peer_id], send_sems.at[k - 1], recv_sems.at[k - 1], (x, y, c)).wait_recv()
        for cp in copies:
            cp.wait_send()
        total = slots[0]
        for dev in range(1, N_DEV):
            total = total + slots[dev]
        o_ref[...] = total
        o_ref[loss_row:loss_row + 1, :] = jnp.broadcast_to(
            jnp.sum(total[loss_row:loss_row + 1, :], axis=-1, keepdims=True), (1, LANES))

    return pl.pallas_call(
        body, in_specs=[WHOLE_VMEM], out_specs=WHOLE_VMEM, out_shape=jax.ShapeDtypeStruct((rows, LANES), F32),
        scratch_shapes=[pltpu.VMEM((N_DEV, rows, LANES), F32), pltpu.SemaphoreType.DMA((N_DEV - 1,)),
                        pltpu.SemaphoreType.DMA((N_DEV - 1,))],
        name="allreduce_small")(vec)


def _adamw(w, g, m, v, *, tr):
    depth, rows, cols = w.shape
    assert rows % tr == 0
    c1 = float(np.float32(1.0 - ADAM_B1 ** ADAM_STEP))
    c2 = float(np.float32(1.0 - ADAM_B2 ** ADAM_STEP))

    def body(w_ref, g_ref, m_ref, v_ref, d_ref, mo_ref, vo_ref):
        g_t = g_ref[...]
        m_new = ADAM_B1 * m_ref[...] + (1.0 - ADAM_B1) * g_t
        v_new = ADAM_B2 * v_ref[...] + (1.0 - ADAM_B2) * (g_t * g_t)
        mo_ref[...] = m_new
        vo_ref[...] = v_new
        d_ref[...] = -ADAM_LR * ((m_new / c1) / (jnp.sqrt(v_new / c2) + ADAM_EPS) + ADAM_WD * w_ref[...])

    blk = pl.BlockSpec((None, tr, cols), lambda l, i: (l, i, 0))
    return pl.pallas_call(
        body, grid=(depth, rows // tr), in_specs=[blk] * 4, out_specs=[blk] * 3,
        out_shape=[jax.ShapeDtypeStruct(w.shape, F32)] * 3, name="adamw")(w, g, m, v)


def _local_step(x, positions, target, gains, exchange):
    t = x.shape[0]
    tm = 512
    inv_freq = ROPE_THETA ** (-jnp.arange(0, ROPE_DIM, 2, dtype=F32) / ROPE_DIM)
    lane = np.arange(LANES) % HEAD_DIM
    freq_row = jnp.where(lane < ROPE_DIM, inv_freq[lane % (ROPE_DIM // 2)], 0.0).astype(F32)[None, :]
    cos, sin = _rope_tables(positions.reshape(t, 1), freq_row)

    def hosted(tag, fn, *args, **kwargs):
        *out, got = fn(*args, comm=exchange.host(tag), **kwargs)
        if got is not None:
            exchange.hosted(tag, got)
        return out[0] if len(out) == 1 else out

    saved = []
    h1 = _norm_fwd(x, gains["pre_mix_norm"], 0, tm=tm)
    for l in range(DEPTH):
        proj = hosted(("fwd", l, "in_proj"), _mm_nn, h1, exchange.weight("w_in", l), tm=1024, name="in_proj")
        attn, lse = hosted(("fwd", l, "attn"), _attn_fwd, proj, cos, sin)
        conv_y = _conv_fwd(proj, exchange.weight("conv_w", l), l)
        merged = _merge_fwd(attn, conv_y, gains["attn_out_norm"], gains["conv_out_norm"], l, tm=tm)
        mix = hosted(("fwd", l, "out_proj"), _mm_nn, merged, exchange.weight("w_out", l), tm=1024, name="out_proj")
        x1, h2 = _resnorm_fwd(x, mix, gains["post_mix_norm"], l, gains["pre_ffn_norm"], l, tm=tm)
        g, u, act = hosted(("fwd", l, "gate_up"), _gate_up_swiglu, h2, exchange.weight("w_gate_up", l), tm=1024)
        f = hosted(("fwd", l, "down"), _mm_nn, act, exchange.weight("w_down", l), tm=1024, name="down")
        nxt = (gains["pre_mix_norm"], l + 1) if l + 1 < DEPTH else (None, None)
        x2, h1_next = _resnorm_fwd(x1, f, gains["post_ffn_norm"], l, *nxt, tm=tm)
        saved.append(dict(x=x, h1=h1, proj=proj, attn=attn, lse=lse, conv_y=conv_y, merged=merged, mix=mix,
                          x1=x1, h2=h2, g=g, u=u, act=act, f=f))
        x, h1 = x2, h1_next

    dres, loss_lanes = _loss_fwd_bwd(x, target, tm=tm)

    g_gain = {k: [None] * DEPTH for k in gains}
    g_conv = [None] * DEPTH
    _, df, _, g_gain["post_ffn_norm"][DEPTH - 1], _ = _norm_bwd(
        dres, None, (saved[-1]["f"], gains["post_ffn_norm"], DEPTH - 1), tm=tm)
    for l in reversed(range(DEPTH)):
        sv = saved[l]
        w = {k: exchange.weight(k, l) for k in MATRIX_NAMES + ("conv_w",)}
        dg, du = _down_dx_swiglu_bwd(df, w["w_down"], sv["g"], sv["u"], tm=1024, tko=FFN // 2)
        g_down = _mm_tn(sv["act"], df, 1, tka=256, name="down_dw")
        dh2 = hosted(("bwd", l, "gate_up_dx"), _mm_nt_pair, dg, du, w["w_gate_up"], tm=1024, name="gate_up_dx")
        g_gate_up = _mm_tn(sv["h2"], dg, N_CHIPS // 2, tka=512, name="gate_up_dw",
                           into=lax.empty(w["w_gate_up"].shape, F32))
        g_gate_up = _mm_tn(sv["h2"], du, N_CHIPS // 2, tka=512, name="gate_up_dw", into=g_gate_up,
                           shard0=N_CHIPS // 2)
        exchange.grads(l, "ffn", dict(w_down=g_down.reshape(N_CHIPS, FFN // N_CHIPS, D_MODEL), w_gate_up=g_gate_up))
        dx1, dmix, g_gain["pre_ffn_norm"][l], g_gain["post_mix_norm"][l] = hosted(
            ("bwd", l, "norm_mid"), _norm_bwd,
            dres, (dh2, sv["x1"], gains["pre_ffn_norm"], l), (sv["mix"], gains["post_mix_norm"], l), tm=tm)
        d_merged = hosted(("bwd", l, "out_proj_dx"), _mm_nt, dmix, w["w_out"], tm=1024, tko=D_MODEL, name="out_proj_dx")
        g_out = _mm_tn(sv["merged"], dmix, 1, tka=512, name="out_proj_dw")
        d_attn, delta, d_conv_y, g_gain["attn_out_norm"][l], g_gain["conv_out_norm"][l] = _merge_bwd(
            d_merged, sv["attn"], sv["conv_y"], gains["attn_out_norm"], gains["conv_out_norm"], l, tm=tm)
        dq, dk, dv = hosted(("bwd", l, "attn"), _attn_bwd, sv["proj"], cos, sin, d_attn, sv["lse"], delta)
        du, dgb, dgc, g_conv[l] = _conv_bwd(sv["proj"], w["conv_w"], l, d_conv_y)
        d_proj = jnp.concatenate([dq, dk, dv, du, dgb, dgc], axis=1)
        g_in = _mm_tn(sv["h1"], d_proj, N_CHIPS, tka=512, name="in_proj_dw")
        exchange.grads(l, "mix", dict(w_out=g_out.reshape(N_CHIPS, D_MODEL // N_CHIPS, D_MODEL), w_in=g_in))
        dh1 = hosted(("bwd", l, "in_proj_dx"), _mm_nt, d_proj, w["w_in"], tm=1024, tko=D_MODEL, name="in_proj_dx")
        below = (saved[l - 1]["f"], gains["post_ffn_norm"], l - 1) if l > 0 else None
        dres, df, g_gain["pre_mix_norm"][l], g_below = hosted(
            ("bwd", l, "norm_low"), _norm_bwd, dx1, (dh1, sv["x"], gains["pre_mix_norm"], l), below, tm=tm)
        if l > 0:
            g_gain["post_ffn_norm"][l - 1] = g_below

    g_gain = {k: jnp.concatenate(v, axis=0) for k, v in g_gain.items()}
    return loss_lanes, dres, g_gain, jnp.stack(g_conv, axis=0)


class _Exchange:
    GATHER_HOSTS = {"in_proj": ("w_out", 0), "attn": ("w_gate_up", 0), "gate_up": ("w_down", 0), "down": ("w_in", 1)}
    @staticmethod
    def _reduce_hosts(group, l):
        if group == "ffn":
            return "norm_mid", "attn", l
        if l > 0:
            return "norm_low", "gate_up_dx", l - 1
        return "in_proj_dx", "norm_low", l

    def __init__(self, params, place):
        self.place = place
        self.slabs = {k: [_own_shard_slab(params[k], l, place, BF16) for l in range(DEPTH)] for k in MATRIX_NAMES}
        self.gathered = {k: [None] * DEPTH for k in MATRIX_NAMES}
        self.gathered["w_in"][0], self.conv_w = _run_comm(
            _gather_comm([self.slabs["w_in"][0]], _own_conv_slab(params["conv_w"], place)), "gather_first")
        self.full = {k: lax.empty(params[k].shape, F32) for k in MATRIX_NAMES}
        self.pending = {}
        self.raw = {}

    def weight(self, name, l):
        if name == "conv_w":
            return self.conv_w
        g = self.gathered[name][l]
        return g.reshape(1, g.shape[0] * g.shape[1], g.shape[2]) if name in ("w_out", "w_down") else g

    def host(self, tag):
        phase, l, kernel = tag
        if phase == "fwd":
            name, ahead = self.GATHER_HOSTS.get(kernel, (None, 0))
            return _gather_comm([self.slabs[name][l + ahead]]) if name and l + ahead < DEPTH else None
        if tag in self.pending:
            stage, _, _, arrays = self.pending[tag]
            return _halves_comm(arrays) if stage == "halves" else _partials_comm(arrays)
        return None

    def hosted(self, tag, results):
        phase, l, kernel = tag
        if phase == "fwd":
            name, ahead = self.GATHER_HOSTS[kernel]
            self.gathered[name][l + ahead] = results[0]
            return
        stage, gl, group, arrays = self.pending.pop(tag)
        names = list(self.raw[(gl, group)])
        if stage == "partials":
            self._finish_reduction(gl, names, arrays, results)
            return
        partials = [_add_halves(self.raw[(gl, group)][k], r, self.place) for k, r in zip(names, results)]
        _, ici_kernel, ici_layer = self._reduce_hosts(group, gl)
        self.pending[("bwd", ici_layer, ici_kernel)] = ("partials", gl, group, partials)

    def grads(self, l, group, grads):
        self.raw[(l, group)] = grads
        self.pending[("bwd", l, self._reduce_hosts(group, l)[0])] = ("halves", l, group, [grads[k] for k in grads])

    def _finish_reduction(self, l, names, partials, others):
        for k, p, q in zip(names, partials, others):
            self.full[k] = _sum_partials(p, q, self.place, self.full[k], l)
        shared = _run_comm(_share_comm([self.full[k] for k in names], l), "share_halves")
        for k, g in zip(names, shared):
            self.full[k] = g


def kernel(x, positions, pre_mix_norm, w_in, conv_w, attn_out_norm, conv_out_norm, w_out, post_mix_norm, pre_ffn_norm, w_gate_up, w_down, post_ffn_norm, loss_target, m_pre_mix_norm, m_w_in, m_conv_w, m_attn_out_norm, m_conv_out_norm, m_w_out, m_post_mix_norm, m_pre_ffn_norm, m_w_gate_up, m_w_down, m_post_ffn_norm, v_pre_mix_norm, v_w_in, v_conv_w, v_attn_out_norm, v_conv_out_norm, v_w_out, v_post_mix_norm, v_pre_ffn_norm, v_w_gate_up, v_w_down, v_post_ffn_norm):
    params = dict(pre_mix_norm=pre_mix_norm, w_in=w_in, conv_w=conv_w, attn_out_norm=attn_out_norm,
                  conv_out_norm=conv_out_norm, w_out=w_out, post_mix_norm=post_mix_norm, pre_ffn_norm=pre_ffn_norm,
                  w_gate_up=w_gate_up, w_down=w_down, post_ffn_norm=post_ffn_norm)
    mom1 = dict(pre_mix_norm=m_pre_mix_norm, w_in=m_w_in, conv_w=m_conv_w, attn_out_norm=m_attn_out_norm,
                conv_out_norm=m_conv_out_norm, w_out=m_w_out, post_mix_norm=m_post_mix_norm,
                pre_ffn_norm=m_pre_ffn_norm, w_gate_up=m_w_gate_up, w_down=m_w_down, post_ffn_norm=m_post_ffn_norm)
    mom2 = dict(pre_mix_norm=v_pre_mix_norm, w_in=v_w_in, conv_w=v_conv_w, attn_out_norm=v_attn_out_norm,
                conv_out_norm=v_conv_out_norm, w_out=v_w_out, post_mix_norm=v_post_mix_norm,
                pre_ffn_norm=v_pre_ffn_norm, w_gate_up=v_w_gate_up, w_down=v_w_down, post_ffn_norm=v_post_ffn_norm)
    xi, yi, ci = lax.axis_index("x"), lax.axis_index("y"), lax.axis_index("c")
    place = jnp.stack([2 * xi + yi, ci]).astype(jnp.int32)

    exchange = _Exchange(params, place)
    gains = {k: params[k][:, None, :] for k in GAIN_NAMES}
    loss_lanes, grad_x, g_gain, g_conv = _local_step(x[0], positions[0], loss_target[0], gains, exchange)
    grad = dict(exchange.full)

    small = [g_gain[k].reshape(-1) for k in GAIN_NAMES] + [g_conv.reshape(-1), loss_lanes.reshape(-1)]
    sizes = [int(s.shape[0]) for s in small]
    flat = jnp.concatenate(small)
    loss_row = (sum(sizes) - LANES) // LANES
    rows = -(-flat.shape[0] // (8 * LANES)) * 8
    flat = jnp.pad(flat, (0, rows * LANES - flat.shape[0])).reshape(rows, LANES)
    total = _allreduce_small(flat, loss_row).reshape(-1)
    offsets = np.cumsum([0] + sizes)
    for i, k in enumerate(GAIN_NAMES):
        grad[k] = total[offsets[i]:offsets[i + 1]].reshape(params[k].shape)
    conv_all = total[offsets[6]:offsets[7]].reshape(DEPTH, N_CHIPS, 3, LANES)
    grad["conv_w"] = lax.dynamic_index_in_dim(conv_all, 2 * xi + yi, axis=1, keepdims=False)
    loss = total[offsets[7]]

    delta, new_m, new_v = {}, {}, {}
    for k in WEIGHT_ORDER:
        shape = params[k].shape
        as3 = (lambda a: a) if len(shape) == 3 else (lambda a: a[:, None, :])
        rows_k = shape[1] if len(shape) == 3 else 1
        tr = {1024: 512, 704: 352, 256: 256}.get(rows_k, rows_k)
        d, m, v = _adamw(as3(params[k]), as3(grad[k]), as3(mom1[k]), as3(mom2[k]), tr=tr)
        delta[k], new_m[k], new_v[k] = d.reshape(shape), m.reshape(shape), v.reshape(shape)

    return (loss, grad_x[None], *[grad[k] for k in WEIGHT_ORDER], *[delta[k] for k in WEIGHT_ORDER],
            *[new_m[k] for k in WEIGHT_ORDER], *[new_v[k] for k in WEIGHT_ORDER])
```

```python
import functools
from typing import Callable, NamedTuple

import numpy as np
import jax
import jax.numpy as jnp
from jax import lax
from jax.experimental import pallas as pl
from jax.experimental.pallas import tpu as pltpu

F32 = jnp.float32
BF16 = jnp.bfloat16
MESH = pl.DeviceIdType.MESH

D_MODEL = 1024
ATTN_W = 512
CONV_W = 512
HEAD_DIM = 64
ROPE_DIM = 16
ROPE_THETA = 500000.0
FFN = 2816
DEPTH = 4
RMS_EPS = 1e-6
NEG_INF = -1e30
N_CHIPS = 4
N_DEV = 8
LANES = 128
BF16_ROWS = 16
DILATIONS = (1, 4, 16)
BAND = 64
TQ = 128
WIN = TQ + 2 * BAND
SCALE = HEAD_DIM ** -0.5

ADAM_LR = 0.001
ADAM_B1 = 0.9
ADAM_B2 = 0.999
ADAM_EPS = 1e-08
ADAM_WD = 0.01
ADAM_STEP = 10

GAIN_NAMES = ("pre_mix_norm", "attn_out_norm", "conv_out_norm", "post_mix_norm", "pre_ffn_norm", "post_ffn_norm")
MATRIX_NAMES = ("w_in", "w_out", "w_gate_up", "w_down")
WEIGHT_ORDER = ("pre_mix_norm", "w_in", "conv_w", "attn_out_norm", "conv_out_norm", "w_out", "post_mix_norm",
                "pre_ffn_norm", "w_gate_up", "w_down", "post_ffn_norm")

ANY = pl.BlockSpec(memory_space=pl.ANY)
WHOLE_VMEM = pl.BlockSpec(memory_space=pltpu.VMEM)
LANE_CONTRACT = (((1,), (1,)), ((), ()))
ROW_CONTRACT = (((0,), (0,)), ((), ()))


def _const_spec(block, index):
    return pl.BlockSpec(block, lambda *_: index)


def _gain_spec(g3, l):
    return _const_spec((None, 1, g3.shape[-1]), (l, 0, 0))


class _Comm(NamedTuple):
    ins: tuple
    inouts: tuple
    out_shapes: tuple
    n_sems: int
    start: Callable
    finish: Callable


def _place():
    x, y, c = lax.axis_index("x"), lax.axis_index("y"), lax.axis_index("c")
    other_chips = [(1 - x, y), (x, 1 - y), (1 - x, 1 - y)]
    return x, y, c, other_chips


def _remote(src, dst, send_sem, recv_sem, to):
    return pltpu.make_async_remote_copy(src_ref=src, dst_ref=dst, send_sem=send_sem, recv_sem=recv_sem,
                                        device_id=to, device_id_type=MESH)


def _call(body, operands, *, name, grid, in_specs, out_specs, out_shape, scratch_shapes=(), comm=None):
    in_specs, out_specs, out_shape = list(in_specs), list(out_specs), list(out_shape)
    scratch_shapes = list(scratch_shapes)
    if comm is None:
        out = pl.pallas_call(body, grid=grid, in_specs=in_specs, out_specs=out_specs, out_shape=out_shape,
                             scratch_shapes=scratch_shapes, name=name)(*operands)
        return list(out), None
    n_in, n_out, n_scr = len(in_specs), len(out_shape), len(scratch_shapes)
    n_ci, n_cio, n_co = len(comm.ins), len(comm.inouts), len(comm.out_shapes)

    def hosted(*refs):
        refs = list(refs)
        ins, c_ins = refs[:n_in], refs[n_in:n_in + n_ci]
        base = n_in + n_ci + n_cio
        outs = refs[base:base + n_out]
        c_io = refs[base + n_out:base + n_out + n_cio]
        c_out = refs[base + n_out + n_cio:base + n_out + n_cio + n_co]
        scr = refs[base + n_out + n_cio + n_co:]
        send_sems, recv_sems = scr[n_scr], scr[n_scr + 1]
        if grid:
            first = functools.reduce(jnp.logical_and, [pl.program_id(a) == 0 for a in range(len(grid))])
            last = functools.reduce(jnp.logical_and, [pl.program_id(a) == grid[a] - 1 for a in range(len(grid))])
            pl.when(first)(lambda: comm.start(c_ins, c_io, c_out, send_sems, recv_sems))
            body(*ins, *outs, *scr[:n_scr])
            pl.when(last)(lambda: comm.finish(c_ins, c_io, c_out, send_sems, recv_sems))
        else:
            comm.start(c_ins, c_io, c_out, send_sems, recv_sems)
            body(*ins, *outs, *scr[:n_scr])
            comm.finish(c_ins, c_io, c_out, send_sems, recv_sems)

    res = pl.pallas_call(
        hosted, grid=grid, in_specs=in_specs + [ANY] * (n_ci + n_cio), out_specs=out_specs + [ANY] * (n_cio + n_co),
        out_shape=out_shape + [jax.ShapeDtypeStruct(a.shape, a.dtype) for a in comm.inouts] + list(comm.out_shapes),
        input_output_aliases={n_in + n_ci + i: n_out + i for i in range(n_cio)},
        scratch_shapes=scratch_shapes + [pltpu.SemaphoreType.DMA((comm.n_sems,))] * 2,
        name=name)(*operands, *comm.ins, *comm.inouts)
    return list(res[:n_out]), list(res[n_out:])


def _run_comm(comm, name):
    return _call(lambda: None, [], name=name, grid=(), in_specs=[], out_specs=[], out_shape=[], comm=comm)[1]


def _row_half(ref, lead, core, rows, align):
    hr = rows // 2
    return ref.at[(*lead, pl.ds(pl.multiple_of(core * hr, align), hr), slice(None))]


def _gather_comm(slabs, conv_slab=None):
    n = len(slabs)
    n_conv = 0 if conv_slab is None else 3

    def direct(ios, send, recv):
        x, y, c, chips = _place()
        copies = []
        for a in range(n):
            own = _row_half(ios[a], (2 * x + y,), c, slabs[a].shape[1], BF16_ROWS)
            copies += [_remote(own, own, send.at[a * 3 + j], recv.at[a * 3 + j], (*chip, c))
                       for j, chip in enumerate(chips)]
        if conv_slab is not None:
            own = ios[n].at[:, 2 * x + y]
            copies += [_remote(own, own, send.at[6 * n + j], recv.at[6 * n + j], (*chip, c))
                       for j, chip in enumerate(chips)]
        return copies

    def start(ins, ios, outs, send, recv):
        for cp in direct(ios, send, recv):
            cp.start()

    def finish(ins, ios, outs, send, recv):
        x, y, c, chips = _place()
        sibling = (x, y, 1 - c)
        passed = []
        for a in range(n):
            for j, chip in enumerate(chips):
                landed = _row_half(ios[a], (2 * chip[0] + chip[1],), c, slabs[a].shape[1], BF16_ROWS)
                _remote(landed, landed, send.at[a * 3 + j], recv.at[a * 3 + j], (*chip, c)).wait_recv()
                fwd = _remote(landed, landed, send.at[3 * n + a * 3 + j], recv.at[3 * n + a * 3 + j], sibling)
                fwd.start()
                passed.append(fwd)
        if conv_slab is not None:
            for j, chip in enumerate(chips):
                landed = ios[n].at[:, 2 * chip[0] + chip[1]]
                _remote(landed, landed, send.at[6 * n + j], recv.at[6 * n + j], (*chip, c)).wait_recv()
        for a in range(n):
            for j, chip in enumerate(chips):
                landed = _row_half(ios[a], (2 * chip[0] + chip[1],), 1 - c, slabs[a].shape[1], BF16_ROWS)
                _remote(landed, landed, send.at[3 * n + a * 3 + j], recv.at[3 * n + a * 3 + j], sibling).wait_recv()
        for cp in direct(ios, send, recv) + passed:
            cp.wait_send()

    inouts = tuple(slabs) + (() if conv_slab is None else (conv_slab,))
    return _Comm((), inouts, (), 6 * n + n_conv, start, finish)


def _halves_comm(grads):
    n = len(grads)

    def copies(ins, outs, send, recv):
        x, y, c, _ = _place()
        return [_remote(_row_half(ins[a], (slice(None),), 1 - c, grads[a].shape[1], 8), outs[a],
                        send.at[a], recv.at[a], (x, y, 1 - c)) for a in range(n)]

    def start(ins, ios, outs, send, recv):
        for cp in copies(ins, outs, send, recv):
            cp.start()

    def finish(ins, ios, outs, send, recv):
        for cp in copies(ins, outs, send, recv):
            cp.wait()

    out_shapes = tuple(jax.ShapeDtypeStruct((g.shape[0], g.shape[1] // 2, g.shape[2]), F32) for g in grads)
    return _Comm(tuple(grads), (), out_shapes, n, start, finish)


def _partials_comm(partials):
    n = len(partials)

    def copies(ins, outs, send, recv):
        x, y, c, chips = _place()
        return [_remote(ins[a].at[2 * chip[0] + chip[1]], outs[a].at[k], send.at[a * 3 + k], recv.at[a * 3 + k],
                        (*chip, c)) for a in range(n) for k, chip in enumerate(chips)]

    def start(ins, ios, outs, send, recv):
        for cp in copies(ins, outs, send, recv):
            cp.start()

    def finish(ins, ios, outs, send, recv):
        for cp in copies(ins, outs, send, recv):
            cp.wait()

    out_shapes = tuple(jax.ShapeDtypeStruct((3,) + p.shape[1:], BF16) for p in partials)
    return _Comm(tuple(partials), (), out_shapes, 3 * n, start, finish)


def _share_comm(grads, l):
    n = len(grads)

    def start(ins, ios, outs, send, recv):
        x, y, c, _ = _place()
        for a in range(n):
            mine = _row_half(ios[a], (l,), c, grads[a].shape[1], 8)
            _remote(mine, mine, send.at[a], recv.at[a], (x, y, 1 - c)).start()

    def finish(ins, ios, outs, send, recv):
        x, y, c, _ = _place()
        for a in range(n):
            theirs = _row_half(ios[a], (l,), 1 - c, grads[a].shape[1], 8)
            _remote(theirs, theirs, send.at[a], recv.at[a], (x, y, 1 - c)).wait()

    return _Comm((), tuple(grads), (), n, start, finish)


def _rms_fwd(x, g):
    r = lax.rsqrt(jnp.mean(x * x, axis=-1, keepdims=True) + RMS_EPS)
    return (x * r) * g


def _rms_bwd(x, g, dy):
    r = lax.rsqrt(jnp.mean(x * x, axis=-1, keepdims=True) + RMS_EPS)
    xh = x * r
    u = dy * g
    dx = r * (u - xh * jnp.mean(xh * u, axis=-1, keepdims=True))
    return dx, jnp.sum(dy * xh, axis=0, keepdims=True)


def _accumulate(ref, value, first):
    @pl.when(first)
    def _():
        ref[...] = value

    @pl.when(jnp.logical_not(first))
    def _():
        ref[...] += value


def _rope_coeffs(cos, sin):
    m = lax.broadcasted_iota(jnp.int32, cos.shape, 1) % HEAD_DIM
    a = jnp.where(m < ROPE_DIM, cos, 1.0)
    b = jnp.where(m < ROPE_DIM // 2, -sin, 0.0)
    c = jnp.where((m >= ROPE_DIM // 2) & (m < ROPE_DIM), sin, 0.0)
    return a, b, c


def _rope_apply(t, cos, sin):
    a, b, c = _rope_coeffs(cos, sin)
    n = t.shape[1]
    return a * t + b * pltpu.roll(t, n - ROPE_DIM // 2, 1) + c * pltpu.roll(t, ROPE_DIM // 2, 1)


def _rope_transpose(dt, cos, sin):
    a, b, c = _rope_coeffs(cos, sin)
    n = dt.shape[1]
    return a * dt + pltpu.roll(b * dt, ROPE_DIM // 2, 1) + pltpu.roll(c * dt, n - ROPE_DIM // 2, 1)


def _mm_nn(a, w, *, tm, name, comm=None):
    t, k = a.shape
    s_n, k2, n = w.shape
    assert k == k2 and t % tm == 0

    def body(a_ref, w_ref, o_ref):
        o_ref[...] = jnp.dot(a_ref[...], w_ref[...], preferred_element_type=F32)

    out, got = _call(
        body, [a, w], name=name, grid=(t // tm, s_n),
        in_specs=[pl.BlockSpec((tm, k), lambda i, s: (i, 0)), pl.BlockSpec((None, k, n), lambda i, s: (s, 0, 0))],
        out_specs=[pl.BlockSpec((tm, n), lambda i, s: (i, s))],
        out_shape=[jax.ShapeDtypeStruct((t, s_n * n), F32)], comm=comm)
    return out[0], got


def _mm_nt(a, w, *, tm, tko, name, comm=None):
    t, sn = a.shape
    s_n, ko, n = w.shape
    assert sn == s_n * n and t % tm == 0 and ko % tko == 0

    def body(a_ref, w_ref, o_ref):
        acc = lax.dot_general(a_ref[...], w_ref[...], (((1,), (1,)), ((), ())), preferred_element_type=F32)
        if s_n == 1:
            o_ref[...] = acc
        else:
            _accumulate(o_ref, acc, pl.program_id(2) == 0)

    out, got = _call(
        body, [a, w], name=name, grid=(t // tm, ko // tko, s_n),
        in_specs=[pl.BlockSpec((tm, n), lambda i, j, s: (i, s)),
                  pl.BlockSpec((None, tko, n), lambda i, j, s: (s, j, 0))],
        out_specs=[pl.BlockSpec((tm, tko), lambda i, j, s: (i, j))],
        out_shape=[jax.ShapeDtypeStruct((t, ko), F32)], comm=comm)
    return out[0], got


def _mm_nt_pair(a0, a1, w, *, tm, name, comm=None):
    t = a0.shape[0]
    s_n, ko, n = w.shape
    half = s_n // 2
    assert a0.shape == a1.shape == (t, half * n) and t % tm == 0

    def body(a0_ref, a1_ref, w0_ref, w1_ref, o_ref):
        acc = (lax.dot_general(a0_ref[...], w0_ref[...], LANE_CONTRACT, preferred_element_type=F32)
               + lax.dot_general(a1_ref[...], w1_ref[...], LANE_CONTRACT, preferred_element_type=F32))
        _accumulate(o_ref, acc, pl.program_id(1) == 0)

    a_spec = pl.BlockSpec((tm, n), lambda i, s: (i, s))
    out, got = _call(
        body, [a0, a1, w, w], name=name, grid=(t // tm, half),
        in_specs=[a_spec, a_spec, pl.BlockSpec((None, ko, n), lambda i, s: (s, 0, 0)),
                  pl.BlockSpec((None, ko, n), lambda i, s: (half + s, 0, 0))],
        out_specs=[pl.BlockSpec((tm, ko), lambda i, s: (i, 0))],
        out_shape=[jax.ShapeDtypeStruct((t, ko), F32)], comm=comm)
    return out[0], got


def _mm_tn(a, b, s_n, *, tka, name, into=None, shard0=0):
    t, ka = a.shape
    n = b.shape[1] // s_n
    assert b.shape[0] == t and ka % tka == 0

    def body(a_ref, b_ref, *rest):
        rest[-1][...] = lax.dot_general(a_ref[...], b_ref[...], ROW_CONTRACT, preferred_element_type=F32)

    operands, in_specs, aliases = [a, b], [pl.BlockSpec((t, tka), lambda i, s: (0, i)),
                                           pl.BlockSpec((t, n), lambda i, s: (0, s))], {}
    out_shape = jax.ShapeDtypeStruct((s_n, ka, n), F32)
    if into is not None:
        operands, in_specs, aliases = operands + [into], in_specs + [ANY], {2: 0}
        out_shape = jax.ShapeDtypeStruct(into.shape, F32)
    return pl.pallas_call(
        body, grid=(ka // tka, s_n), in_specs=in_specs,
        out_specs=pl.BlockSpec((None, tka, n), lambda i, s: (shard0 + s, i, 0)),
        out_shape=out_shape, input_output_aliases=aliases, name=name)(*operands)


def _gate_up_swiglu(h, w, *, tm, comm=None):
    t, k = h.shape
    s_n, _, n = w.shape
    half = s_n // 2

    def body(h_ref, wg_ref, wu_ref, g_ref, u_ref, a_ref):
        g_ref[...] = jnp.dot(h_ref[...], wg_ref[...], preferred_element_type=F32)
        u_ref[...] = jnp.dot(h_ref[...], wu_ref[...], preferred_element_type=F32)
        g = g_ref[...]
        a_ref[...] = (g * jax.nn.sigmoid(g) * u_ref[...]).astype(BF16)

    col = pl.BlockSpec((tm, n), lambda i, j: (i, j))
    out, got = _call(
        body, [h, w, w], name="gate_up", grid=(t // tm, half),
        in_specs=[pl.BlockSpec((tm, k), lambda i, j: (i, 0)), pl.BlockSpec((None, k, n), lambda i, j: (j, 0, 0)),
                  pl.BlockSpec((None, k, n), lambda i, j: (half + j, 0, 0))],
        out_specs=[col, col, col],
        out_shape=[jax.ShapeDtypeStruct((t, half * n), F32)] * 2 + [jax.ShapeDtypeStruct((t, half * n), BF16)],
        comm=comm)
    return out[0], out[1], out[2], got


def _down_dx_swiglu_bwd(df, w, g, u, *, tm, tko):
    t, k = df.shape
    _, ko, _ = w.shape
    assert t % tm == 0 and ko % tko == 0

    def body(df_ref, w_ref, g_ref, u_ref, dg_ref, du_ref):
        d = lax.dot_general(df_ref[...], w_ref[...], LANE_CONTRACT, preferred_element_type=F32)
        gg = g_ref[...]
        sig = jax.nn.sigmoid(gg)
        dg_ref[...] = (d * u_ref[...] * (sig * (1.0 + gg * (1.0 - sig)))).astype(BF16)
        du_ref[...] = (d * (gg * sig)).astype(BF16)

    col = pl.BlockSpec((tm, tko), lambda i, j: (i, j))
    return pl.pallas_call(
        body, grid=(t // tm, ko // tko),
        in_specs=[pl.BlockSpec((tm, k), lambda i, j: (i, 0)), pl.BlockSpec((None, tko, k), lambda i, j: (0, j, 0)),
                  col, col],
        out_specs=[col, col], out_shape=[jax.ShapeDtypeStruct((t, ko), BF16)] * 2, name="down_dx")(df, w, g, u)


CHUNK = 256


def _in_proj_dx(d_attn3, d_conv3, w, *, tm, comm=None):
    _, t, _ = d_attn3.shape
    s_n, ko, n = w.shape
    half, per = s_n // 2, n // CHUNK
    assert t % tm == 0

    def body(*refs):
        a_refs, b_refs, wa_ref, wb_ref, o_ref = refs[:per], refs[per:2 * per], refs[2 * per], refs[2 * per + 1], refs[-1]
        acc = jnp.zeros(o_ref.shape, F32)
        for r in range(per):
            cols = slice(r * CHUNK, (r + 1) * CHUNK)
            acc = acc + lax.dot_general(a_refs[r][...], wa_ref[:, cols], LANE_CONTRACT, preferred_element_type=F32)
            acc = acc + lax.dot_general(b_refs[r][...], wb_ref[:, cols], LANE_CONTRACT, preferred_element_type=F32)
        _accumulate(o_ref, acc, pl.program_id(1) == 0)

    piece = lambda r: pl.BlockSpec((None, tm, CHUNK), lambda i, s: ((per * s + r) // 2, i, (per * s + r) % 2))
    out, got = _call(
        body, [d_attn3] * per + [d_conv3] * per + [w, w], name="in_proj_dx", grid=(t // tm, half),
        in_specs=[piece(r) for r in range(per)] * 2
        + [pl.BlockSpec((None, ko, n), lambda i, s: (s, 0, 0)), pl.BlockSpec((None, ko, n), lambda i, s: (half + s, 0, 0))],
        out_specs=[pl.BlockSpec((tm, ko), lambda i, s: (i, 0))],
        out_shape=[jax.ShapeDtypeStruct((t, ko), F32)], comm=comm)
    return out[0], got


def _in_proj_dw(h, d_attn3, d_conv3, s_n, *, tka):
    t, ka = h.shape
    half = s_n // 2
    n = 3 * d_attn3.shape[2] // half
    per = n // CHUNK
    assert ka % tka == 0

    def body(*refs):
        h_ref, o_ref = refs[0], refs[-1]
        for side in range(2):
            for r in range(per):
                o_ref[side, :, r * CHUNK:(r + 1) * CHUNK] = lax.dot_general(
                    h_ref[...], refs[1 + side * per + r][...], ROW_CONTRACT, preferred_element_type=F32)

    piece = lambda r: pl.BlockSpec((None, t, CHUNK), lambda i, s: ((per * s + r) // 2, 0, (per * s + r) % 2))
    out = pl.pallas_call(
        body, grid=(ka // tka, half),
        in_specs=[pl.BlockSpec((t, tka), lambda i, s: (0, i))] + [piece(r) for r in range(per)] * 2,
        out_specs=pl.BlockSpec((2, None, tka, n), lambda i, s: (0, s, i, 0)),
        out_shape=jax.ShapeDtypeStruct((2, half, ka, n), F32), name="in_proj_dw")(h, *[d_attn3] * per, *[d_conv3] * per)
    return out.reshape(s_n, ka, n)


def _rope_tables(positions_col, inv_freq_row):
    t = positions_col.shape[0]

    def body(pos_ref, f_ref, cos_ref, sin_ref):
        ang = pos_ref[...].astype(F32) * f_ref[...]
        cos_ref[...] = jnp.cos(ang)
        sin_ref[...] = jnp.sin(ang)

    return pl.pallas_call(
        body, out_shape=[jax.ShapeDtypeStruct((t, LANES), F32)] * 2, name="rope_tables")(positions_col, inv_freq_row)


def _norm_fwd(x, g3, l, *, tm):
    t, w = x.shape

    def body(x_ref, g_ref, h_ref):
        h_ref[...] = _rms_fwd(x_ref[...], g_ref[...]).astype(BF16)

    return pl.pallas_call(
        body, grid=(t // tm,),
        in_specs=[pl.BlockSpec((tm, w), lambda i: (i, 0)), _gain_spec(g3, l)],
        out_specs=pl.BlockSpec((tm, w), lambda i: (i, 0)),
        out_shape=jax.ShapeDtypeStruct((t, w), BF16), name="norm_fwd")(x, g3)


def _resnorm_fwd(x, y, g_post3, l_post, g_next3, l_next, *, tm):
    t, w = x.shape
    with_next = g_next3 is not None
    row = pl.BlockSpec((tm, w), lambda i: (i, 0))

    def body(x_ref, y_ref, gp_ref, *rest):
        x_new = x_ref[...] + _rms_fwd(y_ref[...], gp_ref[...])
        if with_next:
            gn_ref, xo_ref, h_ref = rest
            h_ref[...] = _rms_fwd(x_new, gn_ref[...]).astype(BF16)
        else:
            (xo_ref,) = rest
        xo_ref[...] = x_new

    ins = [x, y, g_post3] + ([g_next3] if with_next else [])
    in_specs = [row, row, _gain_spec(g_post3, l_post)] + ([_gain_spec(g_next3, l_next)] if with_next else [])
    out_shape = [jax.ShapeDtypeStruct((t, w), F32)] + ([jax.ShapeDtypeStruct((t, w), BF16)] if with_next else [])
    out = pl.pallas_call(body, grid=(t // tm,), in_specs=in_specs, out_specs=[row] * len(out_shape),
                         out_shape=out_shape, name="resnorm_fwd")(*ins)
    return (out[0], out[1]) if with_next else (out[0], None)


def _conv_fwd(proj, conv_w, l):
    t = proj.shape[0]
    col0 = 3 * ATTN_W // LANES

    def body(u_ref, gb_ref, gc_ref, w_ref, y_ref):
        c = gc_ref[...] * u_ref[...]
        row = lax.broadcasted_iota(jnp.int32, c.shape, 0)
        c_prev = jnp.where(row == 0, 0.0, pltpu.roll(c, 1, 0))
        c_next = jnp.where(row == t - 1, 0.0, pltpu.roll(c, t - 1, 0))
        w = w_ref[...]
        y_ref[...] = gb_ref[...] * (w[0:1] * c_prev + w[1:2] * c + w[2:3] * c_next)

    nj = CONV_W // LANES
    cols = lambda base: pl.BlockSpec((t, LANES), lambda j: (0, base + j))
    return pl.pallas_call(
        body, grid=(nj,),
        in_specs=[cols(col0), cols(col0 + nj), cols(col0 + 2 * nj),
                  pl.BlockSpec((None, None, 3, LANES), lambda j: (l, j, 0, 0))],
        out_specs=pl.BlockSpec((t, LANES), lambda j: (0, j)),
        out_shape=jax.ShapeDtypeStruct((t, CONV_W), F32), name="conv_fwd")(proj, proj, proj, conv_w)


def _merge_fwd(attn, conv_y, ga3, gc3, l, *, tm):
    t = attn.shape[0]
    row = pl.BlockSpec((tm, ATTN_W), lambda i: (i, 0))

    def body(a_ref, c_ref, ga_ref, gc_ref, m_ref):
        m_ref[:, :ATTN_W] = _rms_fwd(a_ref[...], ga_ref[...]).astype(BF16)
        m_ref[:, ATTN_W:] = _rms_fwd(c_ref[...], gc_ref[...]).astype(BF16)

    return pl.pallas_call(
        body, grid=(t // tm,),
        in_specs=[row, row, _gain_spec(ga3, l), _gain_spec(gc3, l)],
        out_specs=pl.BlockSpec((tm, D_MODEL), lambda i: (i, 0)),
        out_shape=jax.ShapeDtypeStruct((t, D_MODEL), BF16), name="merge_fwd")(attn, conv_y, ga3, gc3)


def _loss_fwd_bwd(y, target, *, tm):
    t, w = y.shape
    row = pl.BlockSpec((tm, w), lambda i: (i, 0))

    def body(y_ref, t_ref, dy_ref, loss_ref):
        e = y_ref[...] - t_ref[...]
        dy_ref[...] = e * (1.0 / w)
        sq = jnp.sum(e * e, axis=0, keepdims=True) * (0.5 / w)
        part = sq[:, :LANES]
        for j in range(1, w // LANES):
            part = part + sq[:, j * LANES:(j + 1) * LANES]
        _accumulate(loss_ref, part, pl.program_id(0) == 0)

    return pl.pallas_call(
        body, grid=(t // tm,), in_specs=[row, row],
        out_specs=[row, _const_spec((1, LANES), (0, 0))],
        out_shape=[jax.ShapeDtypeStruct((t, w), F32), jax.ShapeDtypeStruct((1, LANES), F32)], name="loss")(y, target)


def _tile_rows(t, nt, lb, d):
    r = t // nt
    q0 = (t % nt) * TQ
    m0 = jnp.clip(q0 - BAND, 0, lb - WIN)
    if d == 1:
        return pl.ds(pl.multiple_of(q0, TQ), TQ), pl.ds(pl.multiple_of(m0, BAND), WIN), m0 - q0
    return pl.ds(r + d * q0, TQ, stride=d), pl.ds(r + d * m0, WIN, stride=d), m0 - q0


def _for_row_chunks(t, fn, chunk=512):
    def step(i, carry):
        fn(pl.ds(pl.multiple_of(i * chunk, chunk), chunk))
        return carry

    lax.fori_loop(0, t // chunk, step, 0)


def _rope_into(dst_ref, src_ref, cos_ref, sin_ref, t, scale=1.0):
    def chunk(rows):
        dst_ref[rows, :] = _rope_apply(src_ref[rows, :], cos_ref[rows, :], sin_ref[rows, :]) * scale

    _for_row_chunks(t, chunk)


WINDOW_OFFSETS = (-BAND, 0, -2 * BAND)


def _fill_band_bias(bias_ref):
    rel0 = (lax.broadcasted_iota(jnp.int32, (2 * TQ, WIN), 1)
            - lax.broadcasted_iota(jnp.int32, (2 * TQ, WIN), 0) % TQ)
    for j, off in enumerate(WINDOW_OFFSETS):
        rel = rel0 + off
        bias_ref[j] = jnp.where((rel >= -BAND) & (rel <= BAND), 0.0, NEG_INF)


def _fill_sequence_bias(bias_ref):
    rel = (lax.broadcasted_iota(jnp.int32, (2 * WIN, WIN), 1) - lax.broadcasted_iota(jnp.int32, (2 * WIN, WIN), 0) % WIN)
    bias_ref[...] = jnp.where((rel >= -BAND) & (rel <= BAND), 0.0, NEG_INF)


def _band_bias(bias_ref, off):
    return bias_ref[jnp.where(off == WINDOW_OFFSETS[0], 0, jnp.where(off == WINDOW_OFFSETS[1], 1, 2))]


def _stack_heads(a, first_head):
    return jnp.concatenate([jnp.where(first_head, a, 0.0), jnp.where(first_head, 0.0, a)], axis=0)


def _unstack_heads(a2, first_head):
    n = a2.shape[0] // 2
    return jnp.where(first_head, a2[:n], a2[n:])


def _attn_fwd(proj, cos, sin, comm=None):
    t = proj.shape[0]
    npair = ATTN_W // LANES

    def body(q_ref, k_ref, v_ref, cos_ref, sin_ref, o_ref, lse_ref, qs, ks, o1, o2, l0, l1, l2, m1, m2, bias, bias_seq):
        _rope_into(qs, q_ref, cos_ref, sin_ref, t, SCALE)
        _rope_into(ks, k_ref, cos_ref, sin_ref, t)
        _fill_band_bias(bias)
        _fill_sequence_bias(bias_seq)
        outs, dens, maxs = (o_ref, o1, o2), (l0, l1, l2), (lse_ref, m1, m2)

        def softmax_tile(b, qrows, krows, n_q, band_bias):
            first_head = lax.broadcasted_iota(jnp.int32, (n_q, LANES), 1) < HEAD_DIM
            q2 = _stack_heads(qs[qrows, :], first_head).astype(BF16)
            kw = ks[krows, :].astype(BF16)
            vw = jnp.concatenate([v_ref[krows, :].astype(BF16), jnp.ones((WIN, LANES), BF16)], axis=1)
            s = lax.dot_general(q2, kw, LANE_CONTRACT, preferred_element_type=F32) + band_bias
            m = jnp.max(s, axis=-1, keepdims=True)
            pv = jnp.dot(jnp.exp(s - m).astype(BF16), vw, preferred_element_type=F32)
            outs[b][qrows, :] = _unstack_heads(pv[:, :LANES], first_head)
            dens[b][qrows, :] = _unstack_heads(pv[:, LANES:], first_head)
            maxs[b][qrows, :] = _unstack_heads(jnp.broadcast_to(m, (2 * n_q, LANES)), first_head)

        for b, d in enumerate(DILATIONS):
            lb = t // d
            if lb == WIN:
                def sequence(r, carry, b=b, d=d):
                    rows = pl.ds(r, WIN, stride=d)
                    softmax_tile(b, rows, rows, WIN, bias_seq[...])
                    return carry

                lax.fori_loop(0, d, sequence, 0, unroll=4)
                continue
            nt = lb // TQ

            def tile(ti, carry, b=b, d=d, lb=lb, nt=nt):
                qrows, krows, off = _tile_rows(ti, nt, lb, d)
                softmax_tile(b, qrows, krows, TQ, _band_bias(bias, off))
                return carry

            lax.fori_loop(0, d * nt, tile, 0, unroll=8)

        def finish(rows):
            ms = [m_b[rows, :] for m_b in maxs]
            m_all = jnp.maximum(jnp.maximum(ms[0], ms[1]), ms[2])
            ws = [jnp.exp(m_b - m_all) for m_b in ms]
            den = ws[0] * dens[0][rows, :] + ws[1] * dens[1][rows, :] + ws[2] * dens[2][rows, :]
            num = ws[0] * outs[0][rows, :] + ws[1] * outs[1][rows, :] + ws[2] * outs[2][rows, :]
            o_ref[rows, :] = num / den
            lse_ref[rows, :] = m_all + jnp.log(den)

        _for_row_chunks(t, finish, 256)

    cols = lambda base: pl.BlockSpec((t, LANES), lambda g: (0, base + g))
    out, got = _call(
        body, [proj, proj, proj, cos, sin], name="attn_fwd", grid=(npair,),
        in_specs=[cols(0), cols(npair), cols(2 * npair), WHOLE_VMEM, WHOLE_VMEM],
        out_specs=[cols(0), cols(0)],
        out_shape=[jax.ShapeDtypeStruct((t, ATTN_W), F32)] * 2,
        scratch_shapes=[pltpu.VMEM((t, LANES), F32)] * 9 + [pltpu.VMEM((len(WINDOW_OFFSETS), 2 * TQ, WIN), F32),
                                                            pltpu.VMEM((2 * WIN, WIN), F32)],
        comm=comm)
    return out[0], out[1], got


def _attn_bwd(proj, cos, sin, d_attn, lse, delta, comm=None):
    t = proj.shape[0]
    npair = ATTN_W // LANES

    def body(q_ref, k_ref, v_ref, cos_ref, sin_ref, do_ref, l_ref, dl_ref, dqkv_ref,
             qs, ks, dq_acc, dk_acc, dv_acc, bias, bias_seq):
        _rope_into(qs, q_ref, cos_ref, sin_ref, t, SCALE)
        _rope_into(ks, k_ref, cos_ref, sin_ref, t)
        _fill_band_bias(bias)
        _fill_sequence_bias(bias_seq)
        dq_acc[...] = jnp.zeros(dq_acc.shape, F32)
        dk_acc[...] = jnp.zeros(dk_acc.shape, F32)
        dv_acc[...] = jnp.zeros(dv_acc.shape, F32)
        def stack_column(a):
            return jnp.concatenate([a[:, 0:1], a[:, HEAD_DIM:HEAD_DIM + 1]], axis=0)

        def grad_tile(qrows, krows, n_q, band_bias):
            first_head = lax.broadcasted_iota(jnp.int32, (n_q, LANES), 1) < HEAD_DIM
            q2 = _stack_heads(qs[qrows, :], first_head).astype(BF16)
            do2 = _stack_heads(do_ref[qrows, :], first_head).astype(BF16)
            kw = ks[krows, :].astype(BF16)
            vw = v_ref[krows, :].astype(BF16)
            s = lax.dot_general(q2, kw, LANE_CONTRACT, preferred_element_type=F32) + band_bias
            p = jnp.exp(s - stack_column(l_ref[qrows, :]))
            dp = lax.dot_general(do2, vw, LANE_CONTRACT, preferred_element_type=F32)
            ds = (p * (dp - stack_column(dl_ref[qrows, :]))).astype(BF16)
            dq2 = jnp.dot(ds, kw, preferred_element_type=F32)
            dq_acc[qrows, :] += _unstack_heads(dq2, first_head) * SCALE
            dk_acc[krows, :] += lax.dot_general(ds, q2, ROW_CONTRACT, preferred_element_type=F32)
            dv_acc[krows, :] += lax.dot_general(p.astype(BF16), do2, ROW_CONTRACT, preferred_element_type=F32)

        for d in DILATIONS:
            lb = t // d
            if lb == WIN:
                def sequence(r, carry, d=d):
                    rows = pl.ds(r, WIN, stride=d)
                    grad_tile(rows, rows, WIN, bias_seq[...])
                    return carry

                lax.fori_loop(0, d, sequence, 0, unroll=2)
                continue
            nt = lb // TQ

            def tile(ti, carry, d=d, lb=lb, nt=nt):
                qrows, krows, off = _tile_rows(ti, nt, lb, d)
                grad_tile(qrows, krows, TQ, _band_bias(bias, off))
                return carry

            lax.fori_loop(0, d * nt, tile, 0, unroll=4)

        def finish(rows):
            dqkv_ref[0, rows, :] = _rope_transpose(dq_acc[rows, :], cos_ref[rows, :], sin_ref[rows, :]).astype(BF16)
            dqkv_ref[1, rows, :] = _rope_transpose(dk_acc[rows, :], cos_ref[rows, :], sin_ref[rows, :]).astype(BF16)
            dqkv_ref[2, rows, :] = dv_acc[rows, :].astype(BF16)

        _for_row_chunks(t, finish)

    cols = lambda base: pl.BlockSpec((t, LANES), lambda g: (0, base + g))
    out, got = _call(
        body, [proj, proj, proj, cos, sin, d_attn, lse, delta], name="attn_bwd", grid=(npair,),
        in_specs=[cols(0), cols(npair), cols(2 * npair), WHOLE_VMEM, WHOLE_VMEM, cols(0), cols(0), cols(0)],
        out_specs=[pl.BlockSpec((3, t, LANES), lambda g: (0, 0, g))],
        out_shape=[jax.ShapeDtypeStruct((3, t, ATTN_W), BF16)],
        scratch_shapes=[pltpu.VMEM((t, LANES), F32)] * 5 + [pltpu.VMEM((len(WINDOW_OFFSETS), 2 * TQ, WIN), F32),
                                                            pltpu.VMEM((2 * WIN, WIN), F32)],
        comm=comm)
    return out[0], got


def _norm_bwd(dres, pre, post, *, tm, comm=None):
    t, w = dres.shape
    row = pl.BlockSpec((tm, w), lambda i: (i, 0))
    gsum = _const_spec((1, w), (0, 0))
    ins, in_specs, out_shape, out_specs = [dres], [row], [], []
    if pre is not None:
        dh, x, g3, l = pre
        ins += [dh, x, g3]
        in_specs += [row, row, _gain_spec(g3, l)]
        out_shape += [jax.ShapeDtypeStruct((t, w), F32), jax.ShapeDtypeStruct((1, w), F32)]
        out_specs += [row, gsum]
    if post is not None:
        y, g3, l = post
        ins += [y, g3]
        in_specs += [row, _gain_spec(g3, l)]
        out_shape += [jax.ShapeDtypeStruct((t, w), BF16), jax.ShapeDtypeStruct((1, w), F32)]
        out_specs += [row, gsum]
    n_in = len(ins)

    def body(*refs):
        first = pl.program_id(0) == 0
        ins_r, outs_r = list(refs[:n_in]), list(refs[n_in:])
        d = ins_r.pop(0)[...]
        if pre is not None:
            dh_ref, x_ref, g_ref = ins_r[:3]
            ins_r = ins_r[3:]
            dx, dg = _rms_bwd(x_ref[...], g_ref[...], dh_ref[...])
            d = d + dx
            outs_r.pop(0)[...] = d
            _accumulate(outs_r.pop(0), dg, first)
        if post is not None:
            y_ref, g_ref = ins_r
            dy, dg = _rms_bwd(y_ref[...], g_ref[...], d)
            outs_r.pop(0)[...] = dy.astype(BF16)
            _accumulate(outs_r.pop(0), dg, first)

    out, got = _call(body, ins, name="norm_bwd", grid=(t // tm,), in_specs=in_specs, out_specs=out_specs,
                     out_shape=out_shape, comm=comm)
    d_new, dg_pre = (out.pop(0), out.pop(0)) if pre is not None else (None, None)
    dy, dg_post = (out.pop(0), out.pop(0)) if post is not None else (None, None)
    return d_new, dy, dg_pre, dg_post, got


def _merge_bwd(d_merged, attn, conv_y, ga3, gc3, l, *, tm):
    t = attn.shape[0]
    row = pl.BlockSpec((tm, ATTN_W), lambda i: (i, 0))
    gsum = _const_spec((1, ATTN_W), (0, 0))

    def body(dma_ref, dmc_ref, a_ref, c_ref, ga_ref, gc_ref, da_ref, dl_ref, dc_ref, dga_ref, dgc_ref):
        first = pl.program_id(0) == 0
        attn_t = a_ref[...]
        da, dga = _rms_bwd(attn_t, ga_ref[...], dma_ref[...])
        dc, dgc = _rms_bwd(c_ref[...], gc_ref[...], dmc_ref[...])
        da_ref[...] = da
        dc_ref[...] = dc
        same_head = (lax.broadcasted_iota(jnp.int32, (ATTN_W, ATTN_W), 0) // HEAD_DIM
                     == lax.broadcasted_iota(jnp.int32, (ATTN_W, ATTN_W), 1) // HEAD_DIM).astype(BF16)
        rest = da * attn_t
        total = jnp.zeros(rest.shape, F32)
        for _ in range(3):
            term = rest.astype(BF16)
            total = total + jnp.dot(term, same_head, preferred_element_type=F32)
            rest = rest - term.astype(F32)
        dl_ref[...] = total
        _accumulate(dga_ref, dga, first)
        _accumulate(dgc_ref, dgc, first)

    return pl.pallas_call(
        body, grid=(t // tm,),
        in_specs=[pl.BlockSpec((tm, ATTN_W), lambda i: (i, 0)), pl.BlockSpec((tm, CONV_W), lambda i: (i, 1)),
                  row, row, _gain_spec(ga3, l), _gain_spec(gc3, l)],
        out_specs=[row, row, row, gsum, gsum],
        out_shape=[jax.ShapeDtypeStruct((t, ATTN_W), F32)] * 3 + [jax.ShapeDtypeStruct((1, ATTN_W), F32)] * 2,
        name="merge_bwd")(d_merged, d_merged, attn, conv_y, ga3, gc3)


def _conv_bwd(proj, conv_w, l, d_conv_y):
    t = proj.shape[0]
    col0 = 3 * ATTN_W // LANES
    nj = CONV_W // LANES

    def body(u_ref, gb_ref, gc_ref, w_ref, dy_ref, d3_ref, dw_ref):
        u, gc, dy = u_ref[...], gc_ref[...], dy_ref[...]
        row = lax.broadcasted_iota(jnp.int32, u.shape, 0)
        down = lambda a: jnp.where(row == 0, 0.0, pltpu.roll(a, 1, 0))
        up = lambda a: jnp.where(row == t - 1, 0.0, pltpu.roll(a, t - 1, 0))
        w = w_ref[...]
        c = gc * u
        c_prev, c_next = down(c), up(c)
        d3_ref[1] = (dy * (w[0:1] * c_prev + w[1:2] * c + w[2:3] * c_next)).astype(BF16)
        dz = dy * gb_ref[...]
        dc = w[0:1] * up(dz) + w[1:2] * dz + w[2:3] * down(dz)
        d3_ref[0] = (dc * gc).astype(BF16)
        d3_ref[2] = (dc * u).astype(BF16)
        dw_ref[0:1, :] = jnp.sum(dz * c_prev, axis=0, keepdims=True)
        dw_ref[1:2, :] = jnp.sum(dz * c, axis=0, keepdims=True)
        dw_ref[2:3, :] = jnp.sum(dz * c_next, axis=0, keepdims=True)

    cols = lambda base: pl.BlockSpec((t, LANES), lambda j: (0, base + j))
    return pl.pallas_call(
        body, grid=(nj,),
        in_specs=[cols(col0), cols(col0 + nj), cols(col0 + 2 * nj),
                  pl.BlockSpec((None, None, 3, LANES), lambda j: (l, j, 0, 0)), cols(0)],
        out_specs=[pl.BlockSpec((3, t, LANES), lambda j: (0, 0, j)), pl.BlockSpec((None, 3, LANES), lambda j: (j, 0, 0))],
        out_shape=[jax.ShapeDtypeStruct((3, t, CONV_W), BF16), jax.ShapeDtypeStruct((nj, 3, LANES), F32)],
        name="conv_bwd")(proj, proj, proj, conv_w, d_conv_y)


def _own_shard_slab(w, l, place, dtype):
    _, rows, cols = w.shape
    tr = rows if rows <= 704 else 512
    assert rows % tr == 0

    def body(p_ref, w_ref, o_ref):
        del p_ref
        o_ref[...] = w_ref[...].astype(dtype)

    grid_spec = pltpu.PrefetchScalarGridSpec(
        num_scalar_prefetch=1, grid=(rows // tr,),
        in_specs=[pl.BlockSpec((None, tr, cols), lambda i, p: (l, i, 0))],
        out_specs=pl.BlockSpec((None, tr, cols), lambda i, p: (p[0], i, 0)))
    return pl.pallas_call(body, grid_spec=grid_spec, name="own_shard_slab",
                          out_shape=jax.ShapeDtypeStruct((N_CHIPS, rows, cols), dtype))(place, w)


def _own_conv_slab(w, place):
    depth = w.shape[0]

    def body(p_ref, w_ref, o_ref):
        del p_ref
        o_ref[...] = w_ref[...]

    grid_spec = pltpu.PrefetchScalarGridSpec(
        num_scalar_prefetch=1, grid=(depth,),
        in_specs=[pl.BlockSpec((None, 3, LANES), lambda l, p: (l, 0, 0))],
        out_specs=pl.BlockSpec((None, None, 3, LANES), lambda l, p: (l, p[0], 0, 0)))
    return pl.pallas_call(body, grid_spec=grid_spec, name="own_conv_slab",
                          out_shape=jax.ShapeDtypeStruct((depth, N_CHIPS, 3, LANES), F32))(place, w)


def _add_halves(grad, got, place):
    s_n, rows, cols = grad.shape
    hr = rows // 2

    def body(p_ref, g_ref, r_ref, o_ref):
        del p_ref
        o_ref[...] = (g_ref[...] + r_ref[...]).astype(BF16)

    grid_spec = pltpu.PrefetchScalarGridSpec(
        num_scalar_prefetch=1, grid=(s_n,),
        in_specs=[pl.BlockSpec((None, hr, cols), lambda s, p: (s, p[1], 0)),
                  pl.BlockSpec((None, hr, cols), lambda s, p: (s, 0, 0))],
        out_specs=pl.BlockSpec((None, hr, cols), lambda s, p: (s, 0, 0)))
    return pl.pallas_call(body, grid_spec=grid_spec, out_shape=jax.ShapeDtypeStruct((s_n, hr, cols), BF16),
                          name="add_halves")(place, grad, got)


def _sum_partials(partial, got, place, acc, l):
    _, hr, cols = partial.shape

    def body(p_ref, mine_ref, got_ref, acc_ref, o_ref):
        del p_ref, acc_ref
        total = mine_ref[...].astype(F32)
        for k in range(3):
            total = total + got_ref[k].astype(F32)
        o_ref[...] = total

    grid_spec = pltpu.PrefetchScalarGridSpec(
        num_scalar_prefetch=1, grid=(1,),
        in_specs=[pl.BlockSpec((None, hr, cols), lambda i, p: (p[0], 0, 0)),
                  pl.BlockSpec((3, hr, cols), lambda i, p: (0, 0, 0)), ANY],
        out_specs=pl.BlockSpec((None, hr, cols), lambda i, p: (l, p[1], 0)))
    return pl.pallas_call(body, grid_spec=grid_spec, out_shape=jax.ShapeDtypeStruct(acc.shape, F32),
                          input_output_aliases={3: 0}, name="sum_partials")(place, partial, got, acc)


def _allreduce_small(vec, loss_row):
    rows = vec.shape[0]

    def body(v_ref, o_ref, slots, send_sems, recv_sems):
        x, y, c, _ = _place()
        me = 4 * x + 2 * y + c
        slots[me] = v_ref[...]
        copies = []
        for k in range(1, N_DEV):
            flip = lambda v, bit: 1 - v if bit else v
            peer = (flip(x, k & 4), flip(y, k & 2), flip(c, k & 1))
            copies.append(_remote(v_ref, slots.at[me], send_sems.at[k - 1], recv_sems.at[k - 1], peer))
        for cp in copies:
            cp.start()
        for k in range(1, N_DEV):
            flip = lambda v, bit: 1 - v if bit else v
            peer_id = 4 * flip(x, k & 4) + 2 * flip(y, k & 2) + flip(c, k & 1)
            _remote(v_ref, slots.at[peer_id], send_sems.at[k - 1], recv_sems.at[k - 1], (x, y, c)).wait_recv()
        for cp in copies:
            cp.wait_send()
        total = slots[0]
        for dev in range(1, N_DEV):
            total = total + slots[dev]
        o_ref[...] = total
        o_ref[loss_row:loss_row + 1, :] = jnp.broadcast_to(
            jnp.sum(total[loss_row:loss_row + 1, :], axis=-1, keepdims=True), (1, LANES))

    return pl.pallas_call(
        body, in_specs=[WHOLE_VMEM], out_specs=WHOLE_VMEM, out_shape=jax.ShapeDtypeStruct((rows, LANES), F32),
        scratch_shapes=[pltpu.VMEM((N_DEV, rows, LANES), F32), pltpu.SemaphoreType.DMA((N_DEV - 1,)),
                        pltpu.SemaphoreType.DMA((N_DEV - 1,))],
        name="allreduce_small")(vec)


def _adamw(w, g, m, v, *, tr, emit_grad=False):
    depth, rows, cols = w.shape
    assert rows % tr == 0
    c1 = float(np.float32(1.0 - ADAM_B1 ** ADAM_STEP))
    c2 = float(np.float32(1.0 - ADAM_B2 ** ADAM_STEP))

    def body(w_ref, g_ref, m_ref, v_ref, d_ref, mo_ref, vo_ref, *go_ref):
        g_t = g_ref[...]
        if emit_grad:
            go_ref[0][...] = g_t
        m_new = ADAM_B1 * m_ref[...] + (1.0 - ADAM_B1) * g_t
        v_new = ADAM_B2 * v_ref[...] + (1.0 - ADAM_B2) * (g_t * g_t)
        mo_ref[...] = m_new
        vo_ref[...] = v_new
        d_ref[...] = -ADAM_LR * ((m_new / c1) / (jnp.sqrt(v_new / c2) + ADAM_EPS) + ADAM_WD * w_ref[...])

    blk = pl.BlockSpec((None, tr, cols), lambda l, i: (l, i, 0))
    return pl.pallas_call(
        body, grid=(depth, rows // tr), in_specs=[blk] * 4, out_specs=[blk] * (4 if emit_grad else 3),
        out_shape=[jax.ShapeDtypeStruct(w.shape, F32)] * (4 if emit_grad else 3), name="adamw")(w, g, m, v)


def _local_step(x, positions, target, gains, exchange):
    t = x.shape[0]
    tm = 512
    inv_freq = ROPE_THETA ** (-jnp.arange(0, ROPE_DIM, 2, dtype=F32) / ROPE_DIM)
    lane = np.arange(LANES) % HEAD_DIM
    freq_row = jnp.where(lane < ROPE_DIM, inv_freq[lane % (ROPE_DIM // 2)], 0.0).astype(F32)[None, :]
    cos, sin = _rope_tables(positions.reshape(t, 1), freq_row)

    def hosted(tag, fn, *args, **kwargs):
        *out, got = fn(*args, comm=exchange.host(tag), **kwargs)
        if got is not None:
            exchange.hosted(tag, got)
        return out[0] if len(out) == 1 else out

    saved = []
    h1 = _norm_fwd(x, gains["pre_mix_norm"], 0, tm=tm)
    for l in range(DEPTH):
        proj = hosted(("fwd", l, "in_proj"), _mm_nn, h1, exchange.weight("w_in", l), tm=1024, name="in_proj")
        attn, lse = hosted(("fwd", l, "attn"), _attn_fwd, proj, cos, sin)
        conv_y = _conv_fwd(proj, exchange.weight("conv_w", l), l)
        merged = _merge_fwd(attn, conv_y, gains["attn_out_norm"], gains["conv_out_norm"], l, tm=tm)
        mix = hosted(("fwd", l, "out_proj"), _mm_nn, merged, exchange.weight("w_out", l), tm=1024, name="out_proj")
        x1, h2 = _resnorm_fwd(x, mix, gains["post_mix_norm"], l, gains["pre_ffn_norm"], l, tm=tm)
        g, u, act = hosted(("fwd", l, "gate_up"), _gate_up_swiglu, h2, exchange.weight("w_gate_up", l), tm=1024)
        f = hosted(("fwd", l, "down"), _mm_nn, act, exchange.weight("w_down", l), tm=1024, name="down")
        nxt = (gains["pre_mix_norm"], l + 1) if l + 1 < DEPTH else (None, None)
        x2, h1_next = _resnorm_fwd(x1, f, gains["post_ffn_norm"], l, *nxt, tm=tm)
        saved.append(dict(x=x, h1=h1, proj=proj, attn=attn, lse=lse, conv_y=conv_y, merged=merged, mix=mix,
                          x1=x1, h2=h2, g=g, u=u, act=act, f=f))
        x, h1 = x2, h1_next

    dres, loss_lanes = _loss_fwd_bwd(x, target, tm=tm)

    g_gain = {k: [None] * DEPTH for k in gains}
    g_conv = [None] * DEPTH
    _, df, _, g_gain["post_ffn_norm"][DEPTH - 1], _ = _norm_bwd(
        dres, None, (saved[-1]["f"], gains["post_ffn_norm"], DEPTH - 1), tm=tm)
    for l in reversed(range(DEPTH)):
        sv = saved[l]
        w = {k: exchange.weight(k, l) for k in MATRIX_NAMES + ("conv_w",)}
        dg, du = _down_dx_swiglu_bwd(df, w["w_down"], sv["g"], sv["u"], tm=1024, tko=FFN // 2)
        g_down = _mm_tn(sv["act"], df, 1, tka=256, name="down_dw")
        dh2 = hosted(("bwd", l, "gate_up_dx"), _mm_nt_pair, dg, du, w["w_gate_up"], tm=1024, name="gate_up_dx")
        g_gate_up = _mm_tn(sv["h2"], dg, N_CHIPS // 2, tka=512, name="gate_up_dw",
                           into=lax.empty(w["w_gate_up"].shape, F32))
        g_gate_up = _mm_tn(sv["h2"], du, N_CHIPS // 2, tka=512, name="gate_up_dw", into=g_gate_up,
                           shard0=N_CHIPS // 2)
        exchange.grads(l, "ffn", dict(w_down=g_down.reshape(N_CHIPS, FFN // N_CHIPS, D_MODEL), w_gate_up=g_gate_up))
        dx1, dmix, g_gain["pre_ffn_norm"][l], g_gain["post_mix_norm"][l] = hosted(
            ("bwd", l, "norm_mid"), _norm_bwd,
            dres, (dh2, sv["x1"], gains["pre_ffn_norm"], l), (sv["mix"], gains["post_mix_norm"], l), tm=tm)
        d_merged = hosted(("bwd", l, "out_proj_dx"), _mm_nt, dmix, w["w_out"], tm=1024, tko=D_MODEL, name="out_proj_dx")
        g_out = _mm_tn(sv["merged"], dmix, 1, tka=512, name="out_proj_dw")
        d_attn, delta, d_conv_y, g_gain["attn_out_norm"][l], g_gain["conv_out_norm"][l] = _merge_bwd(
            d_merged, sv["attn"], sv["conv_y"], gains["attn_out_norm"], gains["conv_out_norm"], l, tm=tm)
        d_attn3 = hosted(("bwd", l, "attn"), _attn_bwd, sv["proj"], cos, sin, d_attn, sv["lse"], delta)
        d_conv3, g_conv[l] = _conv_bwd(sv["proj"], w["conv_w"], l, d_conv_y)
        g_in = _in_proj_dw(sv["h1"], d_attn3, d_conv3, N_CHIPS, tka=512)
        exchange.grads(l, "mix", dict(w_out=g_out.reshape(N_CHIPS, D_MODEL // N_CHIPS, D_MODEL), w_in=g_in))
        dh1 = hosted(("bwd", l, "in_proj_dx"), _in_proj_dx, d_attn3, d_conv3, w["w_in"], tm=1024)
        below = (saved[l - 1]["f"], gains["post_ffn_norm"], l - 1) if l > 0 else None
        dres, df, g_gain["pre_mix_norm"][l], g_below = hosted(
            ("bwd", l, "norm_low"), _norm_bwd, dx1, (dh1, sv["x"], gains["pre_mix_norm"], l), below, tm=tm)
        if l > 0:
            g_gain["post_ffn_norm"][l - 1] = g_below

    g_gain = {k: jnp.concatenate(v, axis=0) for k, v in g_gain.items()}
    return loss_lanes, dres, g_gain, jnp.stack(g_conv, axis=0)


class _Exchange:
    GATHER_HOSTS = {"in_proj": ("w_out", 0), "attn": ("w_gate_up", 0), "gate_up": ("w_down", 0), "down": ("w_in", 1)}
    @staticmethod
    def _reduce_hosts(group, l):
        if group == "ffn":
            return "norm_mid", "attn", l
        if l > 0:
            return "norm_low", "gate_up_dx", l - 1
        return "in_proj_dx", "norm_low", l

    def __init__(self, params, place):
        self.place = place
        self.slabs = {k: [_own_shard_slab(params[k], l, place, BF16) for l in range(DEPTH)] for k in MATRIX_NAMES}
        self.gathered = {k: [None] * DEPTH for k in MATRIX_NAMES}
        self.gathered["w_in"][0], self.conv_w = _run_comm(
            _gather_comm([self.slabs["w_in"][0]], _own_conv_slab(params["conv_w"], place)), "gather_first")
        self.full = {k: lax.empty(params[k].shape, F32) for k in MATRIX_NAMES}
        self.pending = {}
        self.raw = {}

    def weight(self, name, l):
        if name == "conv_w":
            return self.conv_w
        g = self.gathered[name][l]
        return g.reshape(1, g.shape[0] * g.shape[1], g.shape[2]) if name in ("w_out", "w_down") else g

    def host(self, tag):
        phase, l, kernel = tag
        if phase == "fwd":
            name, ahead = self.GATHER_HOSTS.get(kernel, (None, 0))
            return _gather_comm([self.slabs[name][l + ahead]]) if name and l + ahead < DEPTH else None
        if tag in self.pending:
            stage, _, _, arrays = self.pending[tag]
            return _halves_comm(arrays) if stage == "halves" else _partials_comm(arrays)
        return None

    def hosted(self, tag, results):
        phase, l, kernel = tag
        if phase == "fwd":
            name, ahead = self.GATHER_HOSTS[kernel]
            self.gathered[name][l + ahead] = results[0]
            return
        stage, gl, group, arrays = self.pending.pop(tag)
        names = list(self.raw[(gl, group)])
        if stage == "partials":
            self._finish_reduction(gl, names, arrays, results)
            return
        partials = [_add_halves(self.raw[(gl, group)][k], r, self.place) for k, r in zip(names, results)]
        _, ici_kernel, ici_layer = self._reduce_hosts(group, gl)
        self.pending[("bwd", ici_layer, ici_kernel)] = ("partials", gl, group, partials)

    def grads(self, l, group, grads):
        self.raw[(l, group)] = grads
        self.pending[("bwd", l, self._reduce_hosts(group, l)[0])] = ("halves", l, group, [grads[k] for k in grads])

    def _finish_reduction(self, l, names, partials, others):
        for k, p, q in zip(names, partials, others):
            self.full[k] = _sum_partials(p, q, self.place, self.full[k], l)
        shared = _run_comm(_share_comm([self.full[k] for k in names], l), "share_halves")
        for k, g in zip(names, shared):
            self.full[k] = g


def kernel(x, positions, pre_mix_norm, w_in, conv_w, attn_out_norm, conv_out_norm, w_out, post_mix_norm, pre_ffn_norm, w_gate_up, w_down, post_ffn_norm, loss_target, m_pre_mix_norm, m_w_in, m_conv_w, m_attn_out_norm, m_conv_out_norm, m_w_out, m_post_mix_norm, m_pre_ffn_norm, m_w_gate_up, m_w_down, m_post_ffn_norm, v_pre_mix_norm, v_w_in, v_conv_w, v_attn_out_norm, v_conv_out_norm, v_w_out, v_post_mix_norm, v_pre_ffn_norm, v_w_gate_up, v_w_down, v_post_ffn_norm):
    params = dict(pre_mix_norm=pre_mix_norm, w_in=w_in, conv_w=conv_w, attn_out_norm=attn_out_norm,
                  conv_out_norm=conv_out_norm, w_out=w_out, post_mix_norm=post_mix_norm, pre_ffn_norm=pre_ffn_norm,
                  w_gate_up=w_gate_up, w_down=w_down, post_ffn_norm=post_ffn_norm)
    mom1 = dict(pre_mix_norm=m_pre_mix_norm, w_in=m_w_in, conv_w=m_conv_w, attn_out_norm=m_attn_out_norm,
                conv_out_norm=m_conv_out_norm, w_out=m_w_out, post_mix_norm=m_post_mix_norm,
                pre_ffn_norm=m_pre_ffn_norm, w_gate_up=m_w_gate_up, w_down=m_w_down, post_ffn_norm=m_post_ffn_norm)
    mom2 = dict(pre_mix_norm=v_pre_mix_norm, w_in=v_w_in, conv_w=v_conv_w, attn_out_norm=v_attn_out_norm,
                conv_out_norm=v_conv_out_norm, w_out=v_w_out, post_mix_norm=v_post_mix_norm,
                pre_ffn_norm=v_pre_ffn_norm, w_gate_up=v_w_gate_up, w_down=v_w_down, post_ffn_norm=v_post_ffn_norm)
    xi, yi, ci = lax.axis_index("x"), lax.axis_index("y"), lax.axis_index("c")
    place = jnp.stack([2 * xi + yi, ci]).astype(jnp.int32)

    exchange = _Exchange(params, place)
    gains = {k: params[k][:, None, :] for k in GAIN_NAMES}
    loss_lanes, grad_x, g_gain, g_conv = _local_step(x[0], positions[0], loss_target[0], gains, exchange)
    grad = dict(exchange.full)

    small = [g_gain[k].reshape(-1) for k in GAIN_NAMES] + [g_conv.reshape(-1), loss_lanes.reshape(-1)]
    sizes = [int(s.shape[0]) for s in small]
    flat = jnp.concatenate(small)
    loss_row = (sum(sizes) - LANES) // LANES
    rows = -(-flat.shape[0] // (8 * LANES)) * 8
    flat = jnp.pad(flat, (0, rows * LANES - flat.shape[0])).reshape(rows, LANES)
    total = _allreduce_small(flat, loss_row).reshape(-1)
    offsets = np.cumsum([0] + sizes)
    for i, k in enumerate(GAIN_NAMES):
        grad[k] = total[offsets[i]:offsets[i + 1]].reshape(params[k].shape)
    conv_all = total[offsets[6]:offsets[7]].reshape(DEPTH, N_CHIPS, 3, LANES)
    grad["conv_w"] = lax.dynamic_index_in_dim(conv_all, 2 * xi + yi, axis=1, keepdims=False)
    loss = total[offsets[7]]

    delta, new_m, new_v = {}, {}, {}
    for k in WEIGHT_ORDER:
        shape = params[k].shape
        if k in MATRIX_NAMES:
            tr = {1024: 512, 704: 352, 256: 256}[shape[1]]
            delta[k], new_m[k], new_v[k], grad[k] = _adamw(params[k], grad[k], mom1[k], mom2[k], tr=tr, emit_grad=True)
        else:
            as3 = (lambda a: a) if len(shape) == 3 else (lambda a: a[None])
            d, m, v = _adamw(as3(params[k]), as3(grad[k]), as3(mom1[k]), as3(mom2[k]), tr=as3(params[k]).shape[1])
            delta[k], new_m[k], new_v[k] = d.reshape(shape), m.reshape(shape), v.reshape(shape)

    return (loss, grad_x[None], *[grad[k] for k in WEIGHT_ORDER], *[delta[k] for k in WEIGHT_ORDER],
            *[new_m[k] for k in WEIGHT_ORDER], *[new_v[k] for k in WEIGHT_ORDER])
```

```python
import functools
from typing import Callable, NamedTuple

import numpy as np
import jax
import jax.numpy as jnp
from jax import lax
from jax.experimental import pallas as pl
from jax.experimental.pallas import tpu as pltpu

F32 = jnp.float32
BF16 = jnp.bfloat16
MESH = pl.DeviceIdType.MESH

D_MODEL = 1024
ATTN_W = 512
CONV_W = 512
HEAD_DIM = 64
ROPE_DIM = 16
ROPE_THETA = 500000.0
FFN = 2816
DEPTH = 4
RMS_EPS = 1e-6
NEG_INF = -1e30
N_CHIPS = 4
N_DEV = 8
LANES = 128
BF16_ROWS = 16
DILATIONS = (1, 4, 16)
BAND = 64
TQ = 128
WIN = TQ + 2 * BAND
SCALE = HEAD_DIM ** -0.5

ADAM_LR = 0.001
ADAM_B1 = 0.9
ADAM_B2 = 0.999
ADAM_EPS = 1e-08
ADAM_WD = 0.01
ADAM_STEP = 10

GAIN_NAMES = ("pre_mix_norm", "attn_out_norm", "conv_out_norm", "post_mix_norm", "pre_ffn_norm", "post_ffn_norm")
MATRIX_NAMES = ("w_in", "w_out", "w_gate_up", "w_down")
WEIGHT_ORDER = ("pre_mix_norm", "w_in", "conv_w", "attn_out_norm", "conv_out_norm", "w_out", "post_mix_norm",
                "pre_ffn_norm", "w_gate_up", "w_down", "post_ffn_norm")

ANY = pl.BlockSpec(memory_space=pl.ANY)
WHOLE_VMEM = pl.BlockSpec(memory_space=pltpu.VMEM)
LANE_CONTRACT = (((1,), (1,)), ((), ()))
ROW_CONTRACT = (((0,), (0,)), ((), ()))


def _const_spec(block, index):
    return pl.BlockSpec(block, lambda *_: index)


def _gain_spec(g3, l):
    return _const_spec((None, 1, g3.shape[-1]), (l, 0, 0))


class _Comm(NamedTuple):
    ins: tuple
    inouts: tuple
    out_shapes: tuple
    n_sems: int
    start: Callable
    finish: Callable


def _place():
    x, y, c = lax.axis_index("x"), lax.axis_index("y"), lax.axis_index("c")
    other_chips = [(1 - x, y), (x, 1 - y), (1 - x, 1 - y)]
    return x, y, c, other_chips


def _remote(src, dst, send_sem, recv_sem, to):
    return pltpu.make_async_remote_copy(src_ref=src, dst_ref=dst, send_sem=send_sem, recv_sem=recv_sem,
                                        device_id=to, device_id_type=MESH)


def _call(body, operands, *, name, grid, in_specs, out_specs, out_shape, scratch_shapes=(), comm=None):
    in_specs, out_specs, out_shape = list(in_specs), list(out_specs), list(out_shape)
    scratch_shapes = list(scratch_shapes)
    if comm is None:
        out = pl.pallas_call(body, grid=grid, in_specs=in_specs, out_specs=out_specs, out_shape=out_shape,
                             scratch_shapes=scratch_shapes, name=name)(*operands)
        return list(out), None
    n_in, n_out, n_scr = len(in_specs), len(out_shape), len(scratch_shapes)
    n_ci, n_cio, n_co = len(comm.ins), len(comm.inouts), len(comm.out_shapes)

    def hosted(*refs):
        refs = list(refs)
        ins, c_ins = refs[:n_in], refs[n_in:n_in + n_ci]
        base = n_in + n_ci + n_cio
        outs = refs[base:base + n_out]
        c_io = refs[base + n_out:base + n_out + n_cio]
        c_out = refs[base + n_out + n_cio:base + n_out + n_cio + n_co]
        scr = refs[base + n_out + n_cio + n_co:]
        send_sems, recv_sems = scr[n_scr], scr[n_scr + 1]
        if grid:
            first = functools.reduce(jnp.logical_and, [pl.program_id(a) == 0 for a in range(len(grid))])
            last = functools.reduce(jnp.logical_and, [pl.program_id(a) == grid[a] - 1 for a in range(len(grid))])
            pl.when(first)(lambda: comm.start(c_ins, c_io, c_out, send_sems, recv_sems))
            body(*ins, *outs, *scr[:n_scr])
            pl.when(last)(lambda: comm.finish(c_ins, c_io, c_out, send_sems, recv_sems))
        else:
            comm.start(c_ins, c_io, c_out, send_sems, recv_sems)
            body(*ins, *outs, *scr[:n_scr])
            comm.finish(c_ins, c_io, c_out, send_sems, recv_sems)

    res = pl.pallas_call(
        hosted, grid=grid, in_specs=in_specs + [ANY] * (n_ci + n_cio), out_specs=out_specs + [ANY] * (n_cio + n_co),
        out_shape=out_shape + [jax.ShapeDtypeStruct(a.shape, a.dtype) for a in comm.inouts] + list(comm.out_shapes),
        input_output_aliases={n_in + n_ci + i: n_out + i for i in range(n_cio)},
        scratch_shapes=scratch_shapes + [pltpu.SemaphoreType.DMA((comm.n_sems,))] * 2,
        name=name)(*operands, *comm.ins, *comm.inouts)
    return list(res[:n_out]), list(res[n_out:])


def _run_comm(comm, name):
    return _call(lambda: None, [], name=name, grid=(), in_specs=[], out_specs=[], out_shape=[], comm=comm)[1]


def _row_half(ref, lead, core, rows, align):
    hr = rows // 2
    return ref.at[(*lead, pl.ds(pl.multiple_of(core * hr, align), hr), slice(None))]


def _gather_comm(slabs, conv_slab=None):
    n = len(slabs)
    n_conv = 0 if conv_slab is None else 3

    def direct(ios, send, recv):
        x, y, c, chips = _place()
        copies = []
        for a in range(n):
            own = _row_half(ios[a], (2 * x + y,), c, slabs[a].shape[1], BF16_ROWS)
            copies += [_remote(own, own, send.at[a * 3 + j], recv.at[a * 3 + j], (*chip, c))
                       for j, chip in enumerate(chips)]
        if conv_slab is not None:
            own = ios[n].at[:, 2 * x + y]
            copies += [_remote(own, own, send.at[6 * n + j], recv.at[6 * n + j], (*chip, c))
                       for j, chip in enumerate(chips)]
        return copies

    def start(ins, ios, outs, send, recv):
        for cp in direct(ios, send, recv):
            cp.start()

    def finish(ins, ios, outs, send, recv):
        x, y, c, chips = _place()
        sibling = (x, y, 1 - c)
        passed = []
        for a in range(n):
            for j, chip in enumerate(chips):
                landed = _row_half(ios[a], (2 * chip[0] + chip[1],), c, slabs[a].shape[1], BF16_ROWS)
                _remote(landed, landed, send.at[a * 3 + j], recv.at[a * 3 + j], (*chip, c)).wait_recv()
                fwd = _remote(landed, landed, send.at[3 * n + a * 3 + j], recv.at[3 * n + a * 3 + j], sibling)
                fwd.start()
                passed.append(fwd)
        if conv_slab is not None:
            for j, chip in enumerate(chips):
                landed = ios[n].at[:, 2 * chip[0] + chip[1]]
                _remote(landed, landed, send.at[6 * n + j], recv.at[6 * n + j], (*chip, c)).wait_recv()
        for a in range(n):
            for j, chip in enumerate(chips):
                landed = _row_half(ios[a], (2 * chip[0] + chip[1],), 1 - c, slabs[a].shape[1], BF16_ROWS)
                _remote(landed, landed, send.at[3 * n + a * 3 + j], recv.at[3 * n + a * 3 + j], sibling).wait_recv()
        for cp in direct(ios, send, recv) + passed:
            cp.wait_send()

    inouts = tuple(slabs) + (() if conv_slab is None else (conv_slab,))
    return _Comm((), inouts, (), 6 * n + n_conv, start, finish)


def _halves_comm(grads):
    n = len(grads)

    def copies(ins, outs, send, recv):
        x, y, c, _ = _place()
        return [_remote(_row_half(ins[a], (slice(None),), 1 - c, grads[a].shape[1], 8), outs[a],
                        send.at[a], recv.at[a], (x, y, 1 - c)) for a in range(n)]

    def start(ins, ios, outs, send, recv):
        for cp in copies(ins, outs, send, recv):
            cp.start()

    def finish(ins, ios, outs, send, recv):
        for cp in copies(ins, outs, send, recv):
            cp.wait()

    out_shapes = tuple(jax.ShapeDtypeStruct((g.shape[0], g.shape[1] // 2, g.shape[2]), F32) for g in grads)
    return _Comm(tuple(grads), (), out_shapes, n, start, finish)


def _partials_comm(partials):
    n = len(partials)

    def copies(ins, outs, send, recv):
        x, y, c, chips = _place()
        return [_remote(ins[a].at[2 * chip[0] + chip[1]], outs[a].at[k], send.at[a * 3 + k], recv.at[a * 3 + k],
                        (*chip, c)) for a in range(n) for k, chip in enumerate(chips)]

    def start(ins, ios, outs, send, recv):
        for cp in copies(ins, outs, send, recv):
            cp.start()

    def finish(ins, ios, outs, send, recv):
        for cp in copies(ins, outs, send, recv):
            cp.wait()

    out_shapes = tuple(jax.ShapeDtypeStruct((3,) + p.shape[1:], BF16) for p in partials)
    return _Comm(tuple(partials), (), out_shapes, 3 * n, start, finish)


def _share_comm(grads, l):
    n = len(grads)

    def start(ins, ios, outs, send, recv):
        x, y, c, _ = _place()
        for a in range(n):
            mine = _row_half(ios[a], (l,), c, grads[a].shape[1], 8)
            _remote(mine, mine, send.at[a], recv.at[a], (x, y, 1 - c)).start()

    def finish(ins, ios, outs, send, recv):
        x, y, c, _ = _place()
        for a in range(n):
            theirs = _row_half(ios[a], (l,), 1 - c, grads[a].shape[1], 8)
            _remote(theirs, theirs, send.at[a], recv.at[a], (x, y, 1 - c)).wait()

    return _Comm((), tuple(grads), (), n, start, finish)


def _rms_fwd(x, g):
    r = lax.rsqrt(jnp.mean(x * x, axis=-1, keepdims=True) + RMS_EPS)
    return (x * r) * g


def _rms_bwd(x, g, dy):
    r = lax.rsqrt(jnp.mean(x * x, axis=-1, keepdims=True) + RMS_EPS)
    xh = x * r
    u = dy * g
    dx = r * (u - xh * jnp.mean(xh * u, axis=-1, keepdims=True))
    return dx, jnp.sum(dy * xh, axis=0, keepdims=True)


def _accumulate(ref, value, first):
    @pl.when(first)
    def _():
        ref[...] = value

    @pl.when(jnp.logical_not(first))
    def _():
        ref[...] += value


def _rope_coeffs(cos, sin):
    m = lax.broadcasted_iota(jnp.int32, cos.shape, 1) % HEAD_DIM
    a = jnp.where(m < ROPE_DIM, cos, 1.0)
    b = jnp.where(m < ROPE_DIM // 2, -sin, 0.0)
    c = jnp.where((m >= ROPE_DIM // 2) & (m < ROPE_DIM), sin, 0.0)
    return a, b, c


def _rope_apply(t, cos, sin):
    a, b, c = _rope_coeffs(cos, sin)
    n = t.shape[1]
    return a * t + b * pltpu.roll(t, n - ROPE_DIM // 2, 1) + c * pltpu.roll(t, ROPE_DIM // 2, 1)


def _rope_transpose(dt, cos, sin):
    a, b, c = _rope_coeffs(cos, sin)
    n = dt.shape[1]
    return a * dt + pltpu.roll(b * dt, ROPE_DIM // 2, 1) + pltpu.roll(c * dt, n - ROPE_DIM // 2, 1)


def _mm_nn(a, w, *, tm, name, comm=None):
    t, k = a.shape
    s_n, k2, n = w.shape
    assert k == k2 and t % tm == 0

    def body(a_ref, w_ref, o_ref):
        o_ref[...] = jnp.dot(a_ref[...], w_ref[...], preferred_element_type=F32)

    out, got = _call(
        body, [a, w], name=name, grid=(t // tm, s_n),
        in_specs=[pl.BlockSpec((tm, k), lambda i, s: (i, 0)), pl.BlockSpec((None, k, n), lambda i, s: (s, 0, 0))],
        out_specs=[pl.BlockSpec((tm, n), lambda i, s: (i, s))],
        out_shape=[jax.ShapeDtypeStruct((t, s_n * n), F32)], comm=comm)
    return out[0], got


def _in_proj(h, w, cos, sin, *, tm, comm=None):
    t, k = h.shape
    s_n, _, n = w.shape
    assert t % tm == 0 and n % LANES == 0

    def body(h_ref, w_ref, cos_ref, sin_ref, o_ref):
        o_ref[...] = jnp.dot(h_ref[...], w_ref[...], preferred_element_type=F32)
        for s in range(s_n):
            rotary_cols = min(max(2 * ATTN_W - s * n, 0), n)
            if rotary_cols:
                @pl.when(pl.program_id(1) == s)
                def _():
                    for c0 in range(0, rotary_cols, LANES):
                        cols = slice(c0, c0 + LANES)
                        o_ref[:, cols] = _rope_apply(o_ref[:, cols], cos_ref[...], sin_ref[...])

    lane_tile = pl.BlockSpec((tm, LANES), lambda i, s: (i, 0))
    out, got = _call(
        body, [h, w, cos, sin], name="in_proj", grid=(t // tm, s_n),
        in_specs=[pl.BlockSpec((tm, k), lambda i, s: (i, 0)), pl.BlockSpec((None, k, n), lambda i, s: (s, 0, 0)),
                  lane_tile, lane_tile],
        out_specs=[pl.BlockSpec((tm, n), lambda i, s: (i, s))],
        out_shape=[jax.ShapeDtypeStruct((t, s_n * n), F32)], comm=comm)
    return out[0], got


def _mm_nt(a, w, *, tm, tko, name, comm=None):
    t, sn = a.shape
    s_n, ko, n = w.shape
    assert sn == s_n * n and t % tm == 0 and ko % tko == 0

    def body(a_ref, w_ref, o_ref):
        acc = lax.dot_general(a_ref[...], w_ref[...], (((1,), (1,)), ((), ())), preferred_element_type=F32)
        if s_n == 1:
            o_ref[...] = acc
        else:
            _accumulate(o_ref, acc, pl.program_id(2) == 0)

    out, got = _call(
        body, [a, w], name=name, grid=(t // tm, ko // tko, s_n),
        in_specs=[pl.BlockSpec((tm, n), lambda i, j, s: (i, s)),
                  pl.BlockSpec((None, tko, n), lambda i, j, s: (s, j, 0))],
        out_specs=[pl.BlockSpec((tm, tko), lambda i, j, s: (i, j))],
        out_shape=[jax.ShapeDtypeStruct((t, ko), F32)], comm=comm)
    return out[0], got


def _mm_nt_pair(a0, a1, w, *, tm, name, comm=None):
    t = a0.shape[0]
    s_n, ko, n = w.shape
    half = s_n // 2
    assert a0.shape == a1.shape == (t, half * n) and t % tm == 0

    def body(a0_ref, a1_ref, w0_ref, w1_ref, o_ref):
        acc = (lax.dot_general(a0_ref[...], w0_ref[...], LANE_CONTRACT, preferred_element_type=F32)
               + lax.dot_general(a1_ref[...], w1_ref[...], LANE_CONTRACT, preferred_element_type=F32))
        _accumulate(o_ref, acc, pl.program_id(1) == 0)

    a_spec = pl.BlockSpec((tm, n), lambda i, s: (i, s))
    out, got = _call(
        body, [a0, a1, w, w], name=name, grid=(t // tm, half),
        in_specs=[a_spec, a_spec, pl.BlockSpec((None, ko, n), lambda i, s: (s, 0, 0)),
                  pl.BlockSpec((None, ko, n), lambda i, s: (half + s, 0, 0))],
        out_specs=[pl.BlockSpec((tm, ko), lambda i, s: (i, 0))],
        out_shape=[jax.ShapeDtypeStruct((t, ko), F32)], comm=comm)
    return out[0], got


def _mm_tn(a, b, s_n, *, tka, name, into=None, shard0=0):
    t, ka = a.shape
    n = b.shape[1] // s_n
    assert b.shape[0] == t and ka % tka == 0

    def body(a_ref, b_ref, *rest):
        rest[-1][...] = lax.dot_general(a_ref[...], b_ref[...], ROW_CONTRACT, preferred_element_type=F32)

    operands, in_specs, aliases = [a, b], [pl.BlockSpec((t, tka), lambda i, s: (0, i)),
                                           pl.BlockSpec((t, n), lambda i, s: (0, s))], {}
    out_shape = jax.ShapeDtypeStruct((s_n, ka, n), F32)
    if into is not None:
        operands, in_specs, aliases = operands + [into], in_specs + [ANY], {2: 0}
        out_shape = jax.ShapeDtypeStruct(into.shape, F32)
    return pl.pallas_call(
        body, grid=(ka // tka, s_n), in_specs=in_specs,
        out_specs=pl.BlockSpec((None, tka, n), lambda i, s: (shard0 + s, i, 0)),
        out_shape=out_shape, input_output_aliases=aliases, name=name)(*operands)


def _gate_up_swiglu(h, w, *, tm, comm=None):
    t, k = h.shape
    s_n, _, n = w.shape
    half = s_n // 2

    def body(h_ref, wg_ref, wu_ref, g_ref, u_ref, a_ref):
        g = jnp.dot(h_ref[...], wg_ref[...], preferred_element_type=F32)
        u = jnp.dot(h_ref[...], wu_ref[...], preferred_element_type=F32)
        g_ref[...] = g.astype(BF16)
        u_ref[...] = u.astype(BF16)
        a_ref[...] = (g * jax.nn.sigmoid(g) * u).astype(BF16)

    col = pl.BlockSpec((tm, n), lambda i, j: (i, j))
    out, got = _call(
        body, [h, w, w], name="gate_up", grid=(t // tm, half),
        in_specs=[pl.BlockSpec((tm, k), lambda i, j: (i, 0)), pl.BlockSpec((None, k, n), lambda i, j: (j, 0, 0)),
                  pl.BlockSpec((None, k, n), lambda i, j: (half + j, 0, 0))],
        out_specs=[col, col, col],
        out_shape=[jax.ShapeDtypeStruct((t, half * n), BF16)] * 3, comm=comm)
    return out[0], out[1], out[2], got


def _down_dx_swiglu_bwd(df, w, g, u, *, tm, tko):
    t, k = df.shape
    _, ko, _ = w.shape
    assert t % tm == 0 and ko % tko == 0

    def body(df_ref, w_ref, g_ref, u_ref, dg_ref, du_ref):
        d = lax.dot_general(df_ref[...], w_ref[...], LANE_CONTRACT, preferred_element_type=F32)
        gg = g_ref[...].astype(F32)
        sig = jax.nn.sigmoid(gg)
        dg_ref[...] = (d * u_ref[...].astype(F32) * (sig * (1.0 + gg * (1.0 - sig)))).astype(BF16)
        du_ref[...] = (d * (gg * sig)).astype(BF16)

    col = pl.BlockSpec((tm, tko), lambda i, j: (i, j))
    return pl.pallas_call(
        body, grid=(t // tm, ko // tko),
        in_specs=[pl.BlockSpec((tm, k), lambda i, j: (i, 0)), pl.BlockSpec((None, tko, k), lambda i, j: (0, j, 0)),
                  col, col],
        out_specs=[col, col], out_shape=[jax.ShapeDtypeStruct((t, ko), BF16)] * 2, name="down_dx")(df, w, g, u)


CHUNK = 256


def _in_proj_dx(d_attn3, d_conv3, w, *, tm, comm=None):
    _, t, _ = d_attn3.shape
    s_n, ko, n = w.shape
    half, per = s_n // 2, n // CHUNK
    assert t % tm == 0

    def body(*refs):
        a_refs, b_refs, wa_ref, wb_ref, o_ref = refs[:per], refs[per:2 * per], refs[2 * per], refs[2 * per + 1], refs[-1]
        acc = jnp.zeros(o_ref.shape, F32)
        for r in range(per):
            cols = slice(r * CHUNK, (r + 1) * CHUNK)
            acc = acc + lax.dot_general(a_refs[r][...], wa_ref[:, cols], LANE_CONTRACT, preferred_element_type=F32)
            acc = acc + lax.dot_general(b_refs[r][...], wb_ref[:, cols], LANE_CONTRACT, preferred_element_type=F32)
        _accumulate(o_ref, acc, pl.program_id(1) == 0)

    piece = lambda r: pl.BlockSpec((None, tm, CHUNK), lambda i, s: ((per * s + r) // 2, i, (per * s + r) % 2))
    out, got = _call(
        body, [d_attn3] * per + [d_conv3] * per + [w, w], name="in_proj_dx", grid=(t // tm, half),
        in_specs=[piece(r) for r in range(per)] * 2
        + [pl.BlockSpec((None, ko, n), lambda i, s: (s, 0, 0)), pl.BlockSpec((None, ko, n), lambda i, s: (half + s, 0, 0))],
        out_specs=[pl.BlockSpec((tm, ko), lambda i, s: (i, 0))],
        out_shape=[jax.ShapeDtypeStruct((t, ko), F32)], comm=comm)
    return out[0], got


def _in_proj_dw(h, d_attn3, d_conv3, s_n, *, tka):
    t, ka = h.shape
    half = s_n // 2
    n = 3 * d_attn3.shape[2] // half
    per = n // CHUNK
    assert ka % tka == 0

    def body(*refs):
        h_ref, o_ref = refs[0], refs[-1]
        for side in range(2):
            for r in range(per):
                o_ref[side, :, r * CHUNK:(r + 1) * CHUNK] = lax.dot_general(
                    h_ref[...], refs[1 + side * per + r][...], ROW_CONTRACT, preferred_element_type=F32)

    piece = lambda r: pl.BlockSpec((None, t, CHUNK), lambda i, s: ((per * s + r) // 2, 0, (per * s + r) % 2))
    out = pl.pallas_call(
        body, grid=(ka // tka, half),
        in_specs=[pl.BlockSpec((t, tka), lambda i, s: (0, i))] + [piece(r) for r in range(per)] * 2,
        out_specs=pl.BlockSpec((2, None, tka, n), lambda i, s: (0, s, i, 0)),
        out_shape=jax.ShapeDtypeStruct((2, half, ka, n), F32), name="in_proj_dw")(h, *[d_attn3] * per, *[d_conv3] * per)
    return out.reshape(s_n, ka, n)


def _rope_tables(positions_col, inv_freq_row):
    t = positions_col.shape[0]

    def body(pos_ref, f_ref, cos_ref, sin_ref):
        ang = pos_ref[...].astype(F32) * f_ref[...]
        cos_ref[...] = jnp.cos(ang)
        sin_ref[...] = jnp.sin(ang)

    return pl.pallas_call(
        body, out_shape=[jax.ShapeDtypeStruct((t, LANES), F32)] * 2, name="rope_tables")(positions_col, inv_freq_row)


def _norm_fwd(x, g3, l, *, tm):
    t, w = x.shape

    def body(x_ref, g_ref, h_ref):
        h_ref[...] = _rms_fwd(x_ref[...], g_ref[...]).astype(BF16)

    return pl.pallas_call(
        body, grid=(t // tm,),
        in_specs=[pl.BlockSpec((tm, w), lambda i: (i, 0)), _gain_spec(g3, l)],
        out_specs=pl.BlockSpec((tm, w), lambda i: (i, 0)),
        out_shape=jax.ShapeDtypeStruct((t, w), BF16), name="norm_fwd")(x, g3)


def _resnorm_fwd(x, y, g_post3, l_post, g_next3, l_next, *, tm):
    t, w = x.shape
    with_next = g_next3 is not None
    row = pl.BlockSpec((tm, w), lambda i: (i, 0))

    def body(x_ref, y_ref, gp_ref, *rest):
        x_new = x_ref[...] + _rms_fwd(y_ref[...], gp_ref[...])
        if with_next:
            gn_ref, xo_ref, h_ref = rest
            h_ref[...] = _rms_fwd(x_new, gn_ref[...]).astype(BF16)
        else:
            (xo_ref,) = rest
        xo_ref[...] = x_new

    ins = [x, y, g_post3] + ([g_next3] if with_next else [])
    in_specs = [row, row, _gain_spec(g_post3, l_post)] + ([_gain_spec(g_next3, l_next)] if with_next else [])
    out_shape = [jax.ShapeDtypeStruct((t, w), F32)] + ([jax.ShapeDtypeStruct((t, w), BF16)] if with_next else [])
    out = pl.pallas_call(body, grid=(t // tm,), in_specs=in_specs, out_specs=[row] * len(out_shape),
                         out_shape=out_shape, name="resnorm_fwd")(*ins)
    return (out[0], out[1]) if with_next else (out[0], None)


def _conv_fwd(proj, conv_w, l):
    t = proj.shape[0]
    col0 = 3 * ATTN_W // LANES

    def body(u_ref, gb_ref, gc_ref, w_ref, y_ref):
        c = gc_ref[...] * u_ref[...]
        row = lax.broadcasted_iota(jnp.int32, c.shape, 0)
        c_prev = jnp.where(row == 0, 0.0, pltpu.roll(c, 1, 0))
        c_next = jnp.where(row == t - 1, 0.0, pltpu.roll(c, t - 1, 0))
        w = w_ref[...]
        y_ref[...] = gb_ref[...] * (w[0:1] * c_prev + w[1:2] * c + w[2:3] * c_next)

    nj = CONV_W // LANES
    cols = lambda base: pl.BlockSpec((t, LANES), lambda j: (0, base + j))
    return pl.pallas_call(
        body, grid=(nj,),
        in_specs=[cols(col0), cols(col0 + nj), cols(col0 + 2 * nj),
                  pl.BlockSpec((None, None, 3, LANES), lambda j: (l, j, 0, 0))],
        out_specs=pl.BlockSpec((t, LANES), lambda j: (0, j)),
        out_shape=jax.ShapeDtypeStruct((t, CONV_W), F32), name="conv_fwd")(proj, proj, proj, conv_w)


def _merge_fwd(attn, conv_y, ga3, gc3, l, *, tm):
    t = attn.shape[0]
    row = pl.BlockSpec((tm, ATTN_W), lambda i: (i, 0))

    def body(a_ref, c_ref, ga_ref, gc_ref, m_ref):
        m_ref[:, :ATTN_W] = _rms_fwd(a_ref[...], ga_ref[...]).astype(BF16)
        m_ref[:, ATTN_W:] = _rms_fwd(c_ref[...], gc_ref[...]).astype(BF16)

    return pl.pallas_call(
        body, grid=(t // tm,),
        in_specs=[row, row, _gain_spec(ga3, l), _gain_spec(gc3, l)],
        out_specs=pl.BlockSpec((tm, D_MODEL), lambda i: (i, 0)),
        out_shape=jax.ShapeDtypeStruct((t, D_MODEL), BF16), name="merge_fwd")(attn, conv_y, ga3, gc3)


def _loss_fwd_bwd(y, target, *, tm):
    t, w = y.shape
    row = pl.BlockSpec((tm, w), lambda i: (i, 0))

    def body(y_ref, t_ref, dy_ref, loss_ref):
        e = y_ref[...] - t_ref[...]
        dy_ref[...] = e * (1.0 / w)
        sq = jnp.sum(e * e, axis=0, keepdims=True) * (0.5 / w)
        part = sq[:, :LANES]
        for j in range(1, w // LANES):
            part = part + sq[:, j * LANES:(j + 1) * LANES]
        _accumulate(loss_ref, part, pl.program_id(0) == 0)

    return pl.pallas_call(
        body, grid=(t // tm,), in_specs=[row, row],
        out_specs=[row, _const_spec((1, LANES), (0, 0))],
        out_shape=[jax.ShapeDtypeStruct((t, w), F32), jax.ShapeDtypeStruct((1, LANES), F32)], name="loss")(y, target)


def _tile_rows(t, nt, lb, d):
    r = t // nt
    q0 = (t % nt) * TQ
    m0 = jnp.clip(q0 - BAND, 0, lb - WIN)
    if d == 1:
        return pl.ds(pl.multiple_of(q0, TQ), TQ), pl.ds(pl.multiple_of(m0, BAND), WIN), m0 - q0
    return pl.ds(r + d * q0, TQ, stride=d), pl.ds(r + d * m0, WIN, stride=d), m0 - q0


def _for_row_chunks(t, fn, chunk=512):
    def step(i, carry):
        fn(pl.ds(pl.multiple_of(i * chunk, chunk), chunk))
        return carry

    lax.fori_loop(0, t // chunk, step, 0)


WINDOW_OFFSETS = (-BAND, 0, -2 * BAND)


def _fill_band_bias(bias_ref):
    rel0 = (lax.broadcasted_iota(jnp.int32, (2 * TQ, WIN), 1)
            - lax.broadcasted_iota(jnp.int32, (2 * TQ, WIN), 0) % TQ)
    for j, off in enumerate(WINDOW_OFFSETS):
        rel = rel0 + off
        bias_ref[j] = jnp.where((rel >= -BAND) & (rel <= BAND), 0.0, NEG_INF)


def _fill_sequence_bias(bias_ref):
    rel = (lax.broadcasted_iota(jnp.int32, (2 * WIN, WIN), 1) - lax.broadcasted_iota(jnp.int32, (2 * WIN, WIN), 0) % WIN)
    bias_ref[...] = jnp.where((rel >= -BAND) & (rel <= BAND), 0.0, NEG_INF)


def _band_bias(bias_ref, off):
    return bias_ref[jnp.where(off == WINDOW_OFFSETS[0], 0, jnp.where(off == WINDOW_OFFSETS[1], 1, 2))]


def _stack_heads(a, first_head):
    return jnp.concatenate([jnp.where(first_head, a, 0.0), jnp.where(first_head, 0.0, a)], axis=0)


def _unstack_heads(a2, first_head):
    n = a2.shape[0] // 2
    return jnp.where(first_head, a2[:n], a2[n:])


def _attn_fwd(proj, comm=None):
    t = proj.shape[0]
    npair = ATTN_W // LANES

    def body(q_ref, k_ref, v_ref, o_ref, lse_ref, o1, o2, l0, l1, l2, m1, m2, bias, bias_seq):
        _fill_band_bias(bias)
        _fill_sequence_bias(bias_seq)
        outs, dens, maxs = (o_ref, o1, o2), (l0, l1, l2), (lse_ref, m1, m2)

        def softmax_tile(b, qrows, krows, n_q, band_bias):
            first_head = lax.broadcasted_iota(jnp.int32, (n_q, LANES), 1) < HEAD_DIM
            q2 = _stack_heads(q_ref[qrows, :] * SCALE, first_head).astype(BF16)
            kw = k_ref[krows, :].astype(BF16)
            vw = jnp.concatenate([v_ref[krows, :].astype(BF16), jnp.ones((WIN, LANES), BF16)], axis=1)
            s = lax.dot_general(q2, kw, LANE_CONTRACT, preferred_element_type=F32) + band_bias
            m = jnp.max(s, axis=-1, keepdims=True)
            pv = jnp.dot(jnp.exp(s - m).astype(BF16), vw, preferred_element_type=F32)
            outs[b][qrows, :] = _unstack_heads(pv[:, :LANES], first_head)
            dens[b][qrows, :] = _unstack_heads(pv[:, LANES:], first_head)
            maxs[b][qrows, :] = _unstack_heads(jnp.broadcast_to(m, (2 * n_q, LANES)), first_head)

        for b, d in enumerate(DILATIONS):
            lb = t // d
            if lb == WIN:
                def sequence(r, carry, b=b, d=d):
                    rows = pl.ds(r, WIN, stride=d)
                    softmax_tile(b, rows, rows, WIN, bias_seq[...])
                    return carry

                lax.fori_loop(0, d, sequence, 0, unroll=4)
                continue
            nt = lb // TQ

            def tile(ti, carry, b=b, d=d, lb=lb, nt=nt):
                qrows, krows, off = _tile_rows(ti, nt, lb, d)
                softmax_tile(b, qrows, krows, TQ, _band_bias(bias, off))
                return carry

            lax.fori_loop(0, d * nt, tile, 0, unroll=8)

        def finish(rows):
            ms = [m_b[rows, :] for m_b in maxs]
            m_all = jnp.maximum(jnp.maximum(ms[0], ms[1]), ms[2])
            ws = [jnp.exp(m_b - m_all) for m_b in ms]
            den = ws[0] * dens[0][rows, :] + ws[1] * dens[1][rows, :] + ws[2] * dens[2][rows, :]
            num = ws[0] * outs[0][rows, :] + ws[1] * outs[1][rows, :] + ws[2] * outs[2][rows, :]
            o_ref[rows, :] = num / den
            lse_ref[rows, :] = m_all + jnp.log(den)

        _for_row_chunks(t, finish, 256)

    cols = lambda base: pl.BlockSpec((t, LANES), lambda g: (0, base + g))
    out, got = _call(
        body, [proj, proj, proj], name="attn_fwd", grid=(npair,),
        in_specs=[cols(0), cols(npair), cols(2 * npair)],
        out_specs=[cols(0), cols(0)],
        out_shape=[jax.ShapeDtypeStruct((t, ATTN_W), F32)] * 2,
        scratch_shapes=[pltpu.VMEM((t, LANES), F32)] * 7 + [pltpu.VMEM((len(WINDOW_OFFSETS), 2 * TQ, WIN), F32),
                                                            pltpu.VMEM((2 * WIN, WIN), F32)],
        comm=comm)
    return out[0], out[1], got


def _attn_bwd(proj, cos, sin, d_attn, lse, delta, comm=None):
    t = proj.shape[0]
    npair = ATTN_W // LANES

    def body(q_ref, k_ref, v_ref, cos_ref, sin_ref, do_ref, l_ref, dl_ref, dqkv_ref,
             dq_acc, dk_acc, dv_acc, bias, bias_seq):
        _fill_band_bias(bias)
        _fill_sequence_bias(bias_seq)
        dq_acc[...] = jnp.zeros(dq_acc.shape, F32)
        dk_acc[...] = jnp.zeros(dk_acc.shape, F32)
        dv_acc[...] = jnp.zeros(dv_acc.shape, F32)
        def stack_column(a):
            return jnp.concatenate([a[:, 0:1], a[:, HEAD_DIM:HEAD_DIM + 1]], axis=0)

        def grad_tile(qrows, krows, n_q, band_bias):
            first_head = lax.broadcasted_iota(jnp.int32, (n_q, LANES), 1) < HEAD_DIM
            q2 = _stack_heads(q_ref[qrows, :] * SCALE, first_head).astype(BF16)
            do2 = _stack_heads(do_ref[qrows, :], first_head).astype(BF16)
            kw = k_ref[krows, :].astype(BF16)
            vw = v_ref[krows, :].astype(BF16)
            s = lax.dot_general(q2, kw, LANE_CONTRACT, preferred_element_type=F32) + band_bias
            p = jnp.exp(s - stack_column(l_ref[qrows, :]))
            dp = lax.dot_general(do2, vw, LANE_CONTRACT, preferred_element_type=F32)
            ds = (p * (dp - stack_column(dl_ref[qrows, :]))).astype(BF16)
            dq2 = jnp.dot(ds, kw, preferred_element_type=F32)
            dq_acc[qrows, :] += _unstack_heads(dq2, first_head) * SCALE
            dk_acc[krows, :] += lax.dot_general(ds, q2, ROW_CONTRACT, preferred_element_type=F32)
            dv_acc[krows, :] += lax.dot_general(p.astype(BF16), do2, ROW_CONTRACT, preferred_element_type=F32)

        for d in DILATIONS:
            lb = t // d
            if lb == WIN:
                def sequence(r, carry, d=d):
                    rows = pl.ds(r, WIN, stride=d)
                    grad_tile(rows, rows, WIN, bias_seq[...])
                    return carry

                lax.fori_loop(0, d, sequence, 0, unroll=2)
                continue
            nt = lb // TQ

            def tile(ti, carry, d=d, lb=lb, nt=nt):
                qrows, krows, off = _tile_rows(ti, nt, lb, d)
                grad_tile(qrows, krows, TQ, _band_bias(bias, off))
                return carry

            lax.fori_loop(0, d * nt, tile, 0, unroll=4)

        def finish(rows):
            dqkv_ref[0, rows, :] = _rope_transpose(dq_acc[rows, :], cos_ref[rows, :], sin_ref[rows, :]).astype(BF16)
            dqkv_ref[1, rows, :] = _rope_transpose(dk_acc[rows, :], cos_ref[rows, :], sin_ref[rows, :]).astype(BF16)
            dqkv_ref[2, rows, :] = dv_acc[rows, :].astype(BF16)

        _for_row_chunks(t, finish)

    cols = lambda base: pl.BlockSpec((t, LANES), lambda g: (0, base + g))
    out, got = _call(
        body, [proj, proj, proj, cos, sin, d_attn, lse, delta], name="attn_bwd", grid=(npair,),
        in_specs=[cols(0), cols(npair), cols(2 * npair), WHOLE_VMEM, WHOLE_VMEM, cols(0), cols(0), cols(0)],
        out_specs=[pl.BlockSpec((3, t, LANES), lambda g: (0, 0, g))],
        out_shape=[jax.ShapeDtypeStruct((3, t, ATTN_W), BF16)],
        scratch_shapes=[pltpu.VMEM((t, LANES), F32)] * 3 + [pltpu.VMEM((len(WINDOW_OFFSETS), 2 * TQ, WIN), F32),
                                                            pltpu.VMEM((2 * WIN, WIN), F32)],
        comm=comm)
    return out[0], got


def _norm_bwd(dres, pre, post, *, tm, comm=None):
    t, w = dres.shape
    row = pl.BlockSpec((tm, w), lambda i: (i, 0))
    gsum = _const_spec((1, w), (0, 0))
    ins, in_specs, out_shape, out_specs = [dres], [row], [], []
    if pre is not None:
        dh, x, g3, l = pre
        ins += [dh, x, g3]
        in_specs += [row, row, _gain_spec(g3, l)]
        out_shape += [jax.ShapeDtypeStruct((t, w), F32), jax.ShapeDtypeStruct((1, w), F32)]
        out_specs += [row, gsum]
    if post is not None:
        y, g3, l = post
        ins += [y, g3]
        in_specs += [row, _gain_spec(g3, l)]
        out_shape += [jax.ShapeDtypeStruct((t, w), BF16), jax.ShapeDtypeStruct((1, w), F32)]
        out_specs += [row, gsum]
    n_in = len(ins)

    def body(*refs):
        first = pl.program_id(0) == 0
        ins_r, outs_r = list(refs[:n_in]), list(refs[n_in:])
        d = ins_r.pop(0)[...]
        if pre is not None:
            dh_ref, x_ref, g_ref = ins_r[:3]
            ins_r = ins_r[3:]
            dx, dg = _rms_bwd(x_ref[...], g_ref[...], dh_ref[...])
            d = d + dx
            outs_r.pop(0)[...] = d
            _accumulate(outs_r.pop(0), dg, first)
        if post is not None:
            y_ref, g_ref = ins_r
            dy, dg = _rms_bwd(y_ref[...], g_ref[...], d)
            outs_r.pop(0)[...] = dy.astype(BF16)
            _accumulate(outs_r.pop(0), dg, first)

    out, got = _call(body, ins, name="norm_bwd", grid=(t // tm,), in_specs=in_specs, out_specs=out_specs,
                     out_shape=out_shape, comm=comm)
    d_new, dg_pre = (out.pop(0), out.pop(0)) if pre is not None else (None, None)
    dy, dg_post = (out.pop(0), out.pop(0)) if post is not None else (None, None)
    return d_new, dy, dg_pre, dg_post, got


def _merge_bwd(d_merged, attn, conv_y, ga3, gc3, l, *, tm):
    t = attn.shape[0]
    row = pl.BlockSpec((tm, ATTN_W), lambda i: (i, 0))
    gsum = _const_spec((1, ATTN_W), (0, 0))

    def body(dma_ref, dmc_ref, a_ref, c_ref, ga_ref, gc_ref, da_ref, dl_ref, dc_ref, dga_ref, dgc_ref):
        first = pl.program_id(0) == 0
        attn_t = a_ref[...]
        da, dga = _rms_bwd(attn_t, ga_ref[...], dma_ref[...])
        dc, dgc = _rms_bwd(c_ref[...], gc_ref[...], dmc_ref[...])
        da_ref[...] = da
        dc_ref[...] = dc
        same_head = (lax.broadcasted_iota(jnp.int32, (ATTN_W, ATTN_W), 0) // HEAD_DIM
                     == lax.broadcasted_iota(jnp.int32, (ATTN_W, ATTN_W), 1) // HEAD_DIM).astype(BF16)
        rest = da * attn_t
        total = jnp.zeros(rest.shape, F32)
        for _ in range(3):
            term = rest.astype(BF16)
            total = total + jnp.dot(term, same_head, preferred_element_type=F32)
            rest = rest - term.astype(F32)
        dl_ref[...] = total
        _accumulate(dga_ref, dga, first)
        _accumulate(dgc_ref, dgc, first)

    return pl.pallas_call(
        body, grid=(t // tm,),
        in_specs=[pl.BlockSpec((tm, ATTN_W), lambda i: (i, 0)), pl.BlockSpec((tm, CONV_W), lambda i: (i, 1)),
                  row, row, _gain_spec(ga3, l), _gain_spec(gc3, l)],
        out_specs=[row, row, row, gsum, gsum],
        out_shape=[jax.ShapeDtypeStruct((t, ATTN_W), F32)] * 3 + [jax.ShapeDtypeStruct((1, ATTN_W), F32)] * 2,
        name="merge_bwd")(d_merged, d_merged, attn, conv_y, ga3, gc3)


def _conv_bwd(proj, conv_w, l, d_conv_y):
    t = proj.shape[0]
    col0 = 3 * ATTN_W // LANES
    nj = CONV_W // LANES

    def body(u_ref, gb_ref, gc_ref, w_ref, dy_ref, d3_ref, dw_ref):
        u, gc, dy = u_ref[...], gc_ref[...], dy_ref[...]
        row = lax.broadcasted_iota(jnp.int32, u.shape, 0)
        down = lambda a: jnp.where(row == 0, 0.0, pltpu.roll(a, 1, 0))
        up = lambda a: jnp.where(row == t - 1, 0.0, pltpu.roll(a, t - 1, 0))
        w = w_ref[...]
        c = gc * u
        c_prev, c_next = down(c), up(c)
        d3_ref[1] = (dy * (w[0:1] * c_prev + w[1:2] * c + w[2:3] * c_next)).astype(BF16)
        dz = dy * gb_ref[...]
        dc = w[0:1] * up(dz) + w[1:2] * dz + w[2:3] * down(dz)
        d3_ref[0] = (dc * gc).astype(BF16)
        d3_ref[2] = (dc * u).astype(BF16)
        dw_ref[0:1, :] = jnp.sum(dz * c_prev, axis=0, keepdims=True)
        dw_ref[1:2, :] = jnp.sum(dz * c, axis=0, keepdims=True)
        dw_ref[2:3, :] = jnp.sum(dz * c_next, axis=0, keepdims=True)

    cols = lambda base: pl.BlockSpec((t, LANES), lambda j: (0, base + j))
    return pl.pallas_call(
        body, grid=(nj,),
        in_specs=[cols(col0), cols(col0 + nj), cols(col0 + 2 * nj),
                  pl.BlockSpec((None, None, 3, LANES), lambda j: (l, j, 0, 0)), cols(0)],
        out_specs=[pl.BlockSpec((3, t, LANES), lambda j: (0, 0, j)), pl.BlockSpec((None, 3, LANES), lambda j: (j, 0, 0))],
        out_shape=[jax.ShapeDtypeStruct((3, t, CONV_W), BF16), jax.ShapeDtypeStruct((nj, 3, LANES), F32)],
        name="conv_bwd")(proj, proj, proj, conv_w, d_conv_y)


def _own_shard_slab(w, l, place, dtype):
    _, rows, cols = w.shape
    tr = rows if rows <= 704 else 512
    assert rows % tr == 0

    def body(p_ref, w_ref, o_ref):
        del p_ref
        o_ref[...] = w_ref[...].astype(dtype)

    grid_spec = pltpu.PrefetchScalarGridSpec(
        num_scalar_prefetch=1, grid=(rows // tr,),
        in_specs=[pl.BlockSpec((None, tr, cols), lambda i, p: (l, i, 0))],
        out_specs=pl.BlockSpec((None, tr, cols), lambda i, p: (p[0], i, 0)))
    return pl.pallas_call(body, grid_spec=grid_spec, name="own_shard_slab",
                          out_shape=jax.ShapeDtypeStruct((N_CHIPS, rows, cols), dtype))(place, w)


def _own_conv_slab(w, place):
    depth = w.shape[0]

    def body(p_ref, w_ref, o_ref):
        del p_ref
        o_ref[...] = w_ref[...]

    grid_spec = pltpu.PrefetchScalarGridSpec(
        num_scalar_prefetch=1, grid=(depth,),
        in_specs=[pl.BlockSpec((None, 3, LANES), lambda l, p: (l, 0, 0))],
        out_specs=pl.BlockSpec((None, None, 3, LANES), lambda l, p: (l, p[0], 0, 0)))
    return pl.pallas_call(body, grid_spec=grid_spec, name="own_conv_slab",
                          out_shape=jax.ShapeDtypeStruct((depth, N_CHIPS, 3, LANES), F32))(place, w)


def _add_halves(grad, got, place):
    s_n, rows, cols = grad.shape
    hr = rows // 2

    def body(p_ref, g_ref, r_ref, o_ref):
        del p_ref
        o_ref[...] = (g_ref[...] + r_ref[...]).astype(BF16)

    grid_spec = pltpu.PrefetchScalarGridSpec(
        num_scalar_prefetch=1, grid=(s_n,),
        in_specs=[pl.BlockSpec((None, hr, cols), lambda s, p: (s, p[1], 0)),
                  pl.BlockSpec((None, hr, cols), lambda s, p: (s, 0, 0))],
        out_specs=pl.BlockSpec((None, hr, cols), lambda s, p: (s, 0, 0)))
    return pl.pallas_call(body, grid_spec=grid_spec, out_shape=jax.ShapeDtypeStruct((s_n, hr, cols), BF16),
                          name="add_halves")(place, grad, got)


def _sum_partials(partial, got, place, acc, l):
    _, hr, cols = partial.shape

    def body(p_ref, mine_ref, got_ref, acc_ref, o_ref):
        del p_ref, acc_ref
        total = mine_ref[...].astype(F32)
        for k in range(3):
            total = total + got_ref[k].astype(F32)
        o_ref[...] = total

    grid_spec = pltpu.PrefetchScalarGridSpec(
        num_scalar_prefetch=1, grid=(1,),
        in_specs=[pl.BlockSpec((None, hr, cols), lambda i, p: (p[0], 0, 0)),
                  pl.BlockSpec((3, hr, cols), lambda i, p: (0, 0, 0)), ANY],
        out_specs=pl.BlockSpec((None, hr, cols), lambda i, p: (l, p[1], 0)))
    return pl.pallas_call(body, grid_spec=grid_spec, out_shape=jax.ShapeDtypeStruct(acc.shape, F32),
                          input_output_aliases={3: 0}, name="sum_partials")(place, partial, got, acc)


def _allreduce_small(vec, loss_row):
    rows = vec.shape[0]

    def body(v_ref, o_ref, slots, send_sems, recv_sems):
        x, y, c, _ = _place()
        me = 4 * x + 2 * y + c
        slots[me] = v_ref[...]
        copies = []
        for k in range(1, N_DEV):
            flip = lambda v, bit: 1 - v if bit else v
            peer = (flip(x, k & 4), flip(y, k & 2), flip(c, k & 1))
            copies.append(_remote(v_ref, slots.at[me], send_sems.at[k - 1], recv_sems.at[k - 1], peer))
        for cp in copies:
            cp.start()
        for k in range(1, N_DEV):
            flip = lambda v, bit: 1 - v if bit else v
            peer_id = 4 * flip(x, k & 4) + 2 * flip(y, k & 2) + flip(c, k & 1)
            _remote(v_ref, slots.at[peer_id], send_sems.at[k - 1], recv_sems.at[k - 1], (x, y, c)).wait_recv()
        for cp in copies:
            cp.wait_send()
        total = slots[0]
        for dev in range(1, N_DEV):
            total = total + slots[dev]
        o_ref[...] = total
        o_ref[loss_row:loss_row + 1, :] = jnp.broadcast_to(
            jnp.sum(total[loss_row:loss_row + 1, :], axis=-1, keepdims=True), (1, LANES))

    return pl.pallas_call(
        body, in_specs=[WHOLE_VMEM], out_specs=WHOLE_VMEM, out_shape=jax.ShapeDtypeStruct((rows, LANES), F32),
        scratch_shapes=[pltpu.VMEM((N_DEV, rows, LANES), F32), pltpu.SemaphoreType.DMA((N_DEV - 1,)),
                        pltpu.SemaphoreType.DMA((N_DEV - 1,))],
        name="allreduce_small")(vec)


def _adamw(w, g, m, v, *, tr, emit_grad=False):
    depth, rows, cols = w.shape
    assert rows % tr == 0
    c1 = float(np.float32(1.0 - ADAM_B1 ** ADAM_STEP))
    c2 = float(np.float32(1.0 - ADAM_B2 ** ADAM_STEP))

    def body(w_ref, g_ref, m_ref, v_ref, d_ref, mo_ref, vo_ref, *go_ref):
        g_t = g_ref[...]
        if emit_grad:
            go_ref[0][...] = g_t
        m_new = ADAM_B1 * m_ref[...] + (1.0 - ADAM_B1) * g_t
        v_new = ADAM_B2 * v_ref[...] + (1.0 - ADAM_B2) * (g_t * g_t)
        mo_ref[...] = m_new
        vo_ref[...] = v_new
        d_ref[...] = -ADAM_LR * ((m_new / c1) / (jnp.sqrt(v_new / c2) + ADAM_EPS) + ADAM_WD * w_ref[...])

    blk = pl.BlockSpec((None, tr, cols), lambda l, i: (l, i, 0))
    return pl.pallas_call(
        body, grid=(depth, rows // tr), in_specs=[blk] * 4, out_specs=[blk] * (4 if emit_grad else 3),
        out_shape=[jax.ShapeDtypeStruct(w.shape, F32)] * (4 if emit_grad else 3), name="adamw")(w, g, m, v)


def _local_step(x, positions, target, gains, exchange):
    t = x.shape[0]
    tm = 512
    inv_freq = ROPE_THETA ** (-jnp.arange(0, ROPE_DIM, 2, dtype=F32) / ROPE_DIM)
    lane = np.arange(LANES) % HEAD_DIM
    freq_row = jnp.where(lane < ROPE_DIM, inv_freq[lane % (ROPE_DIM // 2)], 0.0).astype(F32)[None, :]
    cos, sin = _rope_tables(positions.reshape(t, 1), freq_row)

    def hosted(tag, fn, *args, **kwargs):
        *out, got = fn(*args, comm=exchange.host(tag), **kwargs)
        if got is not None:
            exchange.hosted(tag, got)
        return out[0] if len(out) == 1 else out

    saved = []
    h1 = _norm_fwd(x, gains["pre_mix_norm"], 0, tm=tm)
    for l in range(DEPTH):
        proj = hosted(("fwd", l, "in_proj"), _in_proj, h1, exchange.weight("w_in", l), cos, sin, tm=1024)
        attn, lse = hosted(("fwd", l, "attn"), _attn_fwd, proj)
        conv_y = _conv_fwd(proj, exchange.weight("conv_w", l), l)
        merged = _merge_fwd(attn, conv_y, gains["attn_out_norm"], gains["conv_out_norm"], l, tm=tm)
        mix = hosted(("fwd", l, "out_proj"), _mm_nn, merged, exchange.weight("w_out", l), tm=1024, name="out_proj")
        x1, h2 = _resnorm_fwd(x, mix, gains["post_mix_norm"], l, gains["pre_ffn_norm"], l, tm=tm)
        g, u, act = hosted(("fwd", l, "gate_up"), _gate_up_swiglu, h2, exchange.weight("w_gate_up", l), tm=1024)
        f = hosted(("fwd", l, "down"), _mm_nn, act, exchange.weight("w_down", l), tm=1024, name="down")
        nxt = (gains["pre_mix_norm"], l + 1) if l + 1 < DEPTH else (None, None)
        x2, h1_next = _resnorm_fwd(x1, f, gains["post_ffn_norm"], l, *nxt, tm=tm)
        saved.append(dict(x=x, h1=h1, proj=proj, attn=attn, lse=lse, conv_y=conv_y, merged=merged, mix=mix,
                          x1=x1, h2=h2, g=g, u=u, act=act, f=f))
        x, h1 = x2, h1_next

    dres, loss_lanes = _loss_fwd_bwd(x, target, tm=tm)

    g_gain = {k: [None] * DEPTH for k in gains}
    g_conv = [None] * DEPTH
    _, df, _, g_gain["post_ffn_norm"][DEPTH - 1], _ = _norm_bwd(
        dres, None, (saved[-1]["f"], gains["post_ffn_norm"], DEPTH - 1), tm=tm)
    for l in reversed(range(DEPTH)):
        sv = saved[l]
        w = {k: exchange.weight(k, l) for k in MATRIX_NAMES + ("conv_w",)}
        dg, du = _down_dx_swiglu_bwd(df, w["w_down"], sv["g"], sv["u"], tm=1024, tko=FFN // 2)
        g_down = _mm_tn(sv["act"], df, 1, tka=256, name="down_dw")
        dh2 = hosted(("bwd", l, "gate_up_dx"), _mm_nt_pair, dg, du, w["w_gate_up"], tm=1024, name="gate_up_dx")
        g_gate_up = _mm_tn(sv["h2"], dg, N_CHIPS // 2, tka=512, name="gate_up_dw",
                           into=lax.empty(w["w_gate_up"].shape, F32))
        g_gate_up = _mm_tn(sv["h2"], du, N_CHIPS // 2, tka=512, name="gate_up_dw", into=g_gate_up,
                           shard0=N_CHIPS // 2)
        exchange.grads(l, "ffn", dict(w_down=g_down.reshape(N_CHIPS, FFN // N_CHIPS, D_MODEL), w_gate_up=g_gate_up))
        dx1, dmix, g_gain["pre_ffn_norm"][l], g_gain["post_mix_norm"][l] = hosted(
            ("bwd", l, "norm_mid"), _norm_bwd,
            dres, (dh2, sv["x1"], gains["pre_ffn_norm"], l), (sv["mix"], gains["post_mix_norm"], l), tm=tm)
        d_merged = hosted(("bwd", l, "out_proj_dx"), _mm_nt, dmix, w["w_out"], tm=1024, tko=D_MODEL, name="out_proj_dx")
        g_out = _mm_tn(sv["merged"], dmix, 1, tka=512, name="out_proj_dw")
        d_attn, delta, d_conv_y, g_gain["attn_out_norm"][l], g_gain["conv_out_norm"][l] = _merge_bwd(
            d_merged, sv["attn"], sv["conv_y"], gains["attn_out_norm"], gains["conv_out_norm"], l, tm=tm)
        d_attn3 = hosted(("bwd", l, "attn"), _attn_bwd, sv["proj"], cos, sin, d_attn, sv["lse"], delta)
        d_conv3, g_conv[l] = _conv_bwd(sv["proj"], w["conv_w"], l, d_conv_y)
        g_in = _in_proj_dw(sv["h1"], d_attn3, d_conv3, N_CHIPS, tka=512)
        exchange.grads(l, "mix", dict(w_out=g_out.reshape(N_CHIPS, D_MODEL // N_CHIPS, D_MODEL), w_in=g_in))
        dh1 = hosted(("bwd", l, "in_proj_dx"), _in_proj_dx, d_attn3, d_conv3, w["w_in"], tm=1024)
        below = (saved[l - 1]["f"], gains["post_ffn_norm"], l - 1) if l > 0 else None
        dres, df, g_gain["pre_mix_norm"][l], g_below = hosted(
            ("bwd", l, "norm_low"), _norm_bwd, dx1, (dh1, sv["x"], gains["pre_mix_norm"], l), below, tm=tm)
        if l > 0:
            g_gain["post_ffn_norm"][l - 1] = g_below

    g_gain = {k: jnp.concatenate(v, axis=0) for k, v in g_gain.items()}
    return loss_lanes, dres, g_gain, jnp.stack(g_conv, axis=0)


class _Exchange:
    GATHER_HOSTS = {"in_proj": ("w_out", 0), "attn": ("w_gate_up", 0), "gate_up": ("w_down", 0), "down": ("w_in", 1)}
    @staticmethod
    def _reduce_hosts(group, l):
        if group == "ffn":
            return "norm_mid", "attn", l
        if l > 0:
            return "norm_low", "gate_up_dx", l - 1
        return "in_proj_dx", "norm_low", l

    def __init__(self, params, place):
        self.place = place
        self.slabs = {k: [_own_shard_slab(params[k], l, place, BF16) for l in range(DEPTH)] for k in MATRIX_NAMES}
        self.gathered = {k: [None] * DEPTH for k in MATRIX_NAMES}
        self.gathered["w_in"][0], self.conv_w = _run_comm(
            _gather_comm([self.slabs["w_in"][0]], _own_conv_slab(params["conv_w"], place)), "gather_first")
        self.full = {k: lax.empty(params[k].shape, F32) for k in MATRIX_NAMES}
        self.pending = {}
        self.raw = {}

    def weight(self, name, l):
        if name == "conv_w":
            return self.conv_w
        g = self.gathered[name][l]
        return g.reshape(1, g.shape[0] * g.shape[1], g.shape[2]) if name in ("w_out", "w_down") else g

    def host(self, tag):
        phase, l, kernel = tag
        if phase == "fwd":
            name, ahead = self.GATHER_HOSTS.get(kernel, (None, 0))
            return _gather_comm([self.slabs[name][l + ahead]]) if name and l + ahead < DEPTH else None
        if tag in self.pending:
            stage, _, _, arrays = self.pending[tag]
            return _halves_comm(arrays) if stage == "halves" else _partials_comm(arrays)
        return None

    def hosted(self, tag, results):
        phase, l, kernel = tag
        if phase == "fwd":
            name, ahead = self.GATHER_HOSTS[kernel]
            self.gathered[name][l + ahead] = results[0]
            return
        stage, gl, group, arrays = self.pending.pop(tag)
        names = list(self.raw[(gl, group)])
        if stage == "partials":
            self._finish_reduction(gl, names, arrays, results)
            return
        partials = [_add_halves(self.raw[(gl, group)][k], r, self.place) for k, r in zip(names, results)]
        _, ici_kernel, ici_layer = self._reduce_hosts(group, gl)
        self.pending[("bwd", ici_layer, ici_kernel)] = ("partials", gl, group, partials)

    def grads(self, l, group, grads):
        self.raw[(l, group)] = grads
        self.pending[("bwd", l, self._reduce_hosts(group, l)[0])] = ("halves", l, group, [grads[k] for k in grads])

    def _finish_reduction(self, l, names, partials, others):
        for k, p, q in zip(names, partials, others):
            self.full[k] = _sum_partials(p, q, self.place, self.full[k], l)
        shared = _run_comm(_share_comm([self.full[k] for k in names], l), "share_halves")
        for k, g in zip(names, shared):
            self.full[k] = g


def kernel(x, positions, pre_mix_norm, w_in, conv_w, attn_out_norm, conv_out_norm, w_out, post_mix_norm, pre_ffn_norm, w_gate_up, w_down, post_ffn_norm, loss_target, m_pre_mix_norm, m_w_in, m_conv_w, m_attn_out_norm, m_conv_out_norm, m_w_out, m_post_mix_norm, m_pre_ffn_norm, m_w_gate_up, m_w_down, m_post_ffn_norm, v_pre_mix_norm, v_w_in, v_conv_w, v_attn_out_norm, v_conv_out_norm, v_w_out, v_post_mix_norm, v_pre_ffn_norm, v_w_gate_up, v_w_down, v_post_ffn_norm):
    params = dict(pre_mix_norm=pre_mix_norm, w_in=w_in, conv_w=conv_w, attn_out_norm=attn_out_norm,
                  conv_out_norm=conv_out_norm, w_out=w_out, post_mix_norm=post_mix_norm, pre_ffn_norm=pre_ffn_norm,
                  w_gate_up=w_gate_up, w_down=w_down, post_ffn_norm=post_ffn_norm)
    mom1 = dict(pre_mix_norm=m_pre_mix_norm, w_in=m_w_in, conv_w=m_conv_w, attn_out_norm=m_attn_out_norm,
                conv_out_norm=m_conv_out_norm, w_out=m_w_out, post_mix_norm=m_post_mix_norm,
                pre_ffn_norm=m_pre_ffn_norm, w_gate_up=m_w_gate_up, w_down=m_w_down, post_ffn_norm=m_post_ffn_norm)
    mom2 = dict(pre_mix_norm=v_pre_mix_norm, w_in=v_w_in, conv_w=v_conv_w, attn_out_norm=v_attn_out_norm,
                conv_out_norm=v_conv_out_norm, w_out=v_w_out, post_mix_norm=v_post_mix_norm,
                pre_ffn_norm=v_pre_ffn_norm, w_gate_up=v_w_gate_up, w_down=v_w_down, post_ffn_norm=v_post_ffn_norm)
    xi, yi, ci = lax.axis_index("x"), lax.axis_index("y"), lax.axis_index("c")
    place = jnp.stack([2 * xi + yi, ci]).astype(jnp.int32)

    exchange = _Exchange(params, place)
    gains = {k: params[k][:, None, :] for k in GAIN_NAMES}
    loss_lanes, grad_x, g_gain, g_conv = _local_step(x[0], positions[0], loss_target[0], gains, exchange)
    grad = dict(exchange.full)

    small = [g_gain[k].reshape(-1) for k in GAIN_NAMES] + [g_conv.reshape(-1), loss_lanes.reshape(-1)]
    sizes = [int(s.shape[0]) for s in small]
    flat = jnp.concatenate(small)
    loss_row = (sum(sizes) - LANES) // LANES
    rows = -(-flat.shape[0] // (8 * LANES)) * 8
    flat = jnp.pad(flat, (0, rows * LANES - flat.shape[0])).reshape(rows, LANES)
    total = _allreduce_small(flat, loss_row).reshape(-1)
    offsets = np.cumsum([0] + sizes)
    for i, k in enumerate(GAIN_NAMES):
        grad[k] = total[offsets[i]:offsets[i + 1]].reshape(params[k].shape)
    conv_all = total[offsets[6]:offsets[7]].reshape(DEPTH, N_CHIPS, 3, LANES)
    grad["conv_w"] = lax.dynamic_index_in_dim(conv_all, 2 * xi + yi, axis=1, keepdims=False)
    loss = total[offsets[7]]

    delta, new_m, new_v = {}, {}, {}
    for k in WEIGHT_ORDER:
        shape = params[k].shape
        if k in MATRIX_NAMES:
            tr = {1024: 512, 704: 352, 256: 256}[shape[1]]
            delta[k], new_m[k], new_v[k], grad[k] = _adamw(params[k], grad[k], mom1[k], mom2[k], tr=tr, emit_grad=True)
        else:
            as3 = (lambda a: a) if len(shape) == 3 else (lambda a: a[None])
            d, m, v = _adamw(as3(params[k]), as3(grad[k]), as3(mom1[k]), as3(mom2[k]), tr=as3(params[k]).shape[1])
            delta[k], new_m[k], new_v[k] = d.reshape(shape), m.reshape(shape), v.reshape(shape)

    return (loss, grad_x[None], *[grad[k] for k in WEIGHT_ORDER], *[delta[k] for k in WEIGHT_ORDER],
            *[new_m[k] for k in WEIGHT_ORDER], *[new_v[k] for k in WEIGHT_ORDER])
```

```python
import functools
from typing import Callable, NamedTuple

import numpy as np
import jax
import jax.numpy as jnp
from jax import lax
from jax.experimental import pallas as pl
from jax.experimental.pallas import tpu as pltpu

F32 = jnp.float32
BF16 = jnp.bfloat16
MESH = pl.DeviceIdType.MESH

D_MODEL = 1024
ATTN_W = 512
CONV_W = 512
HEAD_DIM = 64
ROPE_DIM = 16
ROPE_THETA = 500000.0
FFN = 2816
DEPTH = 4
RMS_EPS = 1e-6
NEG_INF = -1e30
N_CHIPS = 4
N_DEV = 8
LANES = 128
BF16_ROWS = 16
DILATIONS = (1, 4, 16)
BAND = 64
TQ = 128
WIN = TQ + 2 * BAND
SCALE = HEAD_DIM ** -0.5

ADAM_LR = 0.001
ADAM_B1 = 0.9
ADAM_B2 = 0.999
ADAM_EPS = 1e-08
ADAM_WD = 0.01
ADAM_STEP = 10

GAIN_NAMES = ("pre_mix_norm", "attn_out_norm", "conv_out_norm", "post_mix_norm", "pre_ffn_norm", "post_ffn_norm")
MATRIX_NAMES = ("w_in", "w_out", "w_gate_up", "w_down")
WEIGHT_ORDER = ("pre_mix_norm", "w_in", "conv_w", "attn_out_norm", "conv_out_norm", "w_out", "post_mix_norm",
                "pre_ffn_norm", "w_gate_up", "w_down", "post_ffn_norm")

ANY = pl.BlockSpec(memory_space=pl.ANY)
WHOLE_VMEM = pl.BlockSpec(memory_space=pltpu.VMEM)
LANE_CONTRACT = (((1,), (1,)), ((), ()))
ROW_CONTRACT = (((0,), (0,)), ((), ()))


def _const_spec(block, index):
    return pl.BlockSpec(block, lambda *_: index)


def _gain_spec(g3, l):
    return _const_spec((None, 1, g3.shape[-1]), (l, 0, 0))


class _Comm(NamedTuple):
    ins: tuple
    inouts: tuple
    out_shapes: tuple
    n_sems: int
    start: Callable
    finish: Callable


def _place():
    x, y, c = lax.axis_index("x"), lax.axis_index("y"), lax.axis_index("c")
    other_chips = [(1 - x, y), (x, 1 - y), (1 - x, 1 - y)]
    return x, y, c, other_chips


def _remote(src, dst, send_sem, recv_sem, to):
    return pltpu.make_async_remote_copy(src_ref=src, dst_ref=dst, send_sem=send_sem, recv_sem=recv_sem,
                                        device_id=to, device_id_type=MESH)


def _call(body, operands, *, name, grid, in_specs, out_specs, out_shape, scratch_shapes=(), comm=None):
    in_specs, out_specs, out_shape = list(in_specs), list(out_specs), list(out_shape)
    scratch_shapes = list(scratch_shapes)
    if comm is None:
        out = pl.pallas_call(body, grid=grid, in_specs=in_specs, out_specs=out_specs, out_shape=out_shape,
                             scratch_shapes=scratch_shapes, name=name)(*operands)
        return list(out), None
    n_in, n_out, n_scr = len(in_specs), len(out_shape), len(scratch_shapes)
    n_ci, n_cio, n_co = len(comm.ins), len(comm.inouts), len(comm.out_shapes)

    def hosted(*refs):
        refs = list(refs)
        ins, c_ins = refs[:n_in], refs[n_in:n_in + n_ci]
        base = n_in + n_ci + n_cio
        outs = refs[base:base + n_out]
        c_io = refs[base + n_out:base + n_out + n_cio]
        c_out = refs[base + n_out + n_cio:base + n_out + n_cio + n_co]
        scr = refs[base + n_out + n_cio + n_co:]
        send_sems, recv_sems = scr[n_scr], scr[n_scr + 1]
        if grid:
            first = functools.reduce(jnp.logical_and, [pl.program_id(a) == 0 for a in range(len(grid))])
            last = functools.reduce(jnp.logical_and, [pl.program_id(a) == grid[a] - 1 for a in range(len(grid))])
            pl.when(first)(lambda: comm.start(c_ins, c_io, c_out, send_sems, recv_sems))
            body(*ins, *outs, *scr[:n_scr])
            pl.when(last)(lambda: comm.finish(c_ins, c_io, c_out, send_sems, recv_sems))
        else:
            comm.start(c_ins, c_io, c_out, send_sems, recv_sems)
            body(*ins, *outs, *scr[:n_scr])
            comm.finish(c_ins, c_io, c_out, send_sems, recv_sems)

    res = pl.pallas_call(
        hosted, grid=grid, in_specs=in_specs + [ANY] * (n_ci + n_cio), out_specs=out_specs + [ANY] * (n_cio + n_co),
        out_shape=out_shape + [jax.ShapeDtypeStruct(a.shape, a.dtype) for a in comm.inouts] + list(comm.out_shapes),
        input_output_aliases={n_in + n_ci + i: n_out + i for i in range(n_cio)},
        scratch_shapes=scratch_shapes + [pltpu.SemaphoreType.DMA((comm.n_sems,))] * 2,
        name=name)(*operands, *comm.ins, *comm.inouts)
    return list(res[:n_out]), list(res[n_out:])


def _run_comm(comm, name):
    return _call(lambda: None, [], name=name, grid=(), in_specs=[], out_specs=[], out_shape=[], comm=comm)[1]


def _row_half(ref, lead, core, rows, align):
    hr = rows // 2
    return ref.at[(*lead, pl.ds(pl.multiple_of(core * hr, align), hr), slice(None))]


def _gather_comm(slabs, conv_slab=None):
    n = len(slabs)
    n_conv = 0 if conv_slab is None else 3

    def direct(ios, send, recv):
        x, y, c, chips = _place()
        copies = []
        for a in range(n):
            own = _row_half(ios[a], (2 * x + y,), c, slabs[a].shape[1], BF16_ROWS)
            copies += [_remote(own, own, send.at[a * 3 + j], recv.at[a * 3 + j], (*chip, c))
                       for j, chip in enumerate(chips)]
        if conv_slab is not None:
            own = ios[n].at[:, 2 * x + y]
            copies += [_remote(own, own, send.at[6 * n + j], recv.at[6 * n + j], (*chip, c))
                       for j, chip in enumerate(chips)]
        return copies

    def start(ins, ios, outs, send, recv):
        for cp in direct(ios, send, recv):
            cp.start()

    def finish(ins, ios, outs, send, recv):
        x, y, c, chips = _place()
        sibling = (x, y, 1 - c)
        passed = []
        for a in range(n):
            for j, chip in enumerate(chips):
                landed = _row_half(ios[a], (2 * chip[0] + chip[1],), c, slabs[a].shape[1], BF16_ROWS)
                _remote(landed, landed, send.at[a * 3 + j], recv.at[a * 3 + j], (*chip, c)).wait_recv()
                fwd = _remote(landed, landed, send.at[3 * n + a * 3 + j], recv.at[3 * n + a * 3 + j], sibling)
                fwd.start()
                passed.append(fwd)
        if conv_slab is not None:
            for j, chip in enumerate(chips):
                landed = ios[n].at[:, 2 * chip[0] + chip[1]]
                _remote(landed, landed, send.at[6 * n + j], recv.at[6 * n + j], (*chip, c)).wait_recv()
        for a in range(n):
            for j, chip in enumerate(chips):
                landed = _row_half(ios[a], (2 * chip[0] + chip[1],), 1 - c, slabs[a].shape[1], BF16_ROWS)
                _remote(landed, landed, send.at[3 * n + a * 3 + j], recv.at[3 * n + a * 3 + j], sibling).wait_recv()
        for cp in direct(ios, send, recv) + passed:
            cp.wait_send()

    inouts = tuple(slabs) + (() if conv_slab is None else (conv_slab,))
    return _Comm((), inouts, (), 6 * n + n_conv, start, finish)


def _halves_comm(grads):
    n = len(grads)

    def copies(ins, outs, send, recv):
        x, y, c, _ = _place()
        return [_remote(_row_half(ins[a], (slice(None),), 1 - c, grads[a].shape[1], 8), outs[a],
                        send.at[a], recv.at[a], (x, y, 1 - c)) for a in range(n)]

    def start(ins, ios, outs, send, recv):
        for cp in copies(ins, outs, send, recv):
            cp.start()

    def finish(ins, ios, outs, send, recv):
        for cp in copies(ins, outs, send, recv):
            cp.wait()

    out_shapes = tuple(jax.ShapeDtypeStruct((g.shape[0], g.shape[1] // 2, g.shape[2]), F32) for g in grads)
    return _Comm(tuple(grads), (), out_shapes, n, start, finish)


def _partials_comm(partials):
    n = len(partials)

    def copies(ins, outs, send, recv):
        x, y, c, chips = _place()
        return [_remote(ins[a].at[2 * chip[0] + chip[1]], outs[a].at[k], send.at[a * 3 + k], recv.at[a * 3 + k],
                        (*chip, c)) for a in range(n) for k, chip in enumerate(chips)]

    def start(ins, ios, outs, send, recv):
        for cp in copies(ins, outs, send, recv):
            cp.start()

    def finish(ins, ios, outs, send, recv):
        for cp in copies(ins, outs, send, recv):
            cp.wait()

    out_shapes = tuple(jax.ShapeDtypeStruct((3,) + p.shape[1:], BF16) for p in partials)
    return _Comm(tuple(partials), (), out_shapes, 3 * n, start, finish)


def _share_comm(grads, l):
    n = len(grads)

    def start(ins, ios, outs, send, recv):
        x, y, c, _ = _place()
        for a in range(n):
            mine = _row_half(ios[a], (l,), c, grads[a].shape[1], 8)
            _remote(mine, mine, send.at[a], recv.at[a], (x, y, 1 - c)).start()

    def finish(ins, ios, outs, send, recv):
        x, y, c, _ = _place()
        for a in range(n):
            theirs = _row_half(ios[a], (l,), 1 - c, grads[a].shape[1], 8)
            _remote(theirs, theirs, send.at[a], recv.at[a], (x, y, 1 - c)).wait()

    return _Comm((), tuple(grads), (), n, start, finish)


def _rms_fwd(x, g):
    r = lax.rsqrt(jnp.mean(x * x, axis=-1, keepdims=True) + RMS_EPS)
    return (x * r) * g


def _rms_bwd(x, g, dy):
    r = lax.rsqrt(jnp.mean(x * x, axis=-1, keepdims=True) + RMS_EPS)
    xh = x * r
    u = dy * g
    dx = r * (u - xh * jnp.mean(xh * u, axis=-1, keepdims=True))
    return dx, jnp.sum(dy * xh, axis=0, keepdims=True)


def _accumulate(ref, value, first):
    @pl.when(first)
    def _():
        ref[...] = value

    @pl.when(jnp.logical_not(first))
    def _():
        ref[...] += value


def _rope_coeffs(cos, sin):
    m = lax.broadcasted_iota(jnp.int32, cos.shape, 1) % HEAD_DIM
    a = jnp.where(m < ROPE_DIM, cos, 1.0)
    b = jnp.where(m < ROPE_DIM // 2, -sin, 0.0)
    c = jnp.where((m >= ROPE_DIM // 2) & (m < ROPE_DIM), sin, 0.0)
    return a, b, c


def _rope_apply(t, cos, sin):
    a, b, c = _rope_coeffs(cos, sin)
    n = t.shape[1]
    return a * t + b * pltpu.roll(t, n - ROPE_DIM // 2, 1) + c * pltpu.roll(t, ROPE_DIM // 2, 1)


def _rope_transpose(dt, cos, sin):
    a, b, c = _rope_coeffs(cos, sin)
    n = dt.shape[1]
    return a * dt + pltpu.roll(b * dt, ROPE_DIM // 2, 1) + pltpu.roll(c * dt, n - ROPE_DIM // 2, 1)


def _mm_nn(a, w, *, tm, name, comm=None):
    t, k = a.shape
    s_n, k2, n = w.shape
    assert k == k2 and t % tm == 0

    def body(a_ref, w_ref, o_ref):
        o_ref[...] = jnp.dot(a_ref[...], w_ref[...], preferred_element_type=F32)

    out, got = _call(
        body, [a, w], name=name, grid=(t // tm, s_n),
        in_specs=[pl.BlockSpec((tm, k), lambda i, s: (i, 0)), pl.BlockSpec((None, k, n), lambda i, s: (s, 0, 0))],
        out_specs=[pl.BlockSpec((tm, n), lambda i, s: (i, s))],
        out_shape=[jax.ShapeDtypeStruct((t, s_n * n), F32)], comm=comm)
    return out[0], got


def _in_proj(h, w, cos, sin, *, tm, comm=None):
    t, k = h.shape
    s_n, _, n = w.shape
    assert t % tm == 0 and n % LANES == 0

    def body(h_ref, w_ref, cos_ref, sin_ref, o_ref):
        o_ref[...] = jnp.dot(h_ref[...], w_ref[...], preferred_element_type=F32)
        for s in range(s_n):
            rotary_cols = min(max(2 * ATTN_W - s * n, 0), n)
            if rotary_cols:
                @pl.when(pl.program_id(1) == s)
                def _():
                    for c0 in range(0, rotary_cols, LANES):
                        cols = slice(c0, c0 + LANES)
                        o_ref[:, cols] = _rope_apply(o_ref[:, cols], cos_ref[...], sin_ref[...])

    lane_tile = pl.BlockSpec((tm, LANES), lambda i, s: (i, 0))
    out, got = _call(
        body, [h, w, cos, sin], name="in_proj", grid=(t // tm, s_n),
        in_specs=[pl.BlockSpec((tm, k), lambda i, s: (i, 0)), pl.BlockSpec((None, k, n), lambda i, s: (s, 0, 0)),
                  lane_tile, lane_tile],
        out_specs=[pl.BlockSpec((tm, n), lambda i, s: (i, s))],
        out_shape=[jax.ShapeDtypeStruct((t, s_n * n), F32)], comm=comm)
    return out[0], got


def _mm_nt(a, w, *, tm, tko, name, comm=None):
    t, sn = a.shape
    s_n, ko, n = w.shape
    assert sn == s_n * n and t % tm == 0 and ko % tko == 0

    def body(a_ref, w_ref, o_ref):
        acc = lax.dot_general(a_ref[...], w_ref[...], (((1,), (1,)), ((), ())), preferred_element_type=F32)
        if s_n == 1:
            o_ref[...] = acc
        else:
            _accumulate(o_ref, acc, pl.program_id(2) == 0)

    out, got = _call(
        body, [a, w], name=name, grid=(t // tm, ko // tko, s_n),
        in_specs=[pl.BlockSpec((tm, n), lambda i, j, s: (i, s)),
                  pl.BlockSpec((None, tko, n), lambda i, j, s: (s, j, 0))],
        out_specs=[pl.BlockSpec((tm, tko), lambda i, j, s: (i, j))],
        out_shape=[jax.ShapeDtypeStruct((t, ko), F32)], comm=comm)
    return out[0], got


def _mm_nt_pair(a0, a1, w, *, tm, name, comm=None):
    t = a0.shape[0]
    s_n, ko, n = w.shape
    half = s_n // 2
    assert a0.shape == a1.shape == (t, half * n) and t % tm == 0

    def body(a0_ref, a1_ref, w0_ref, w1_ref, o_ref):
        acc = (lax.dot_general(a0_ref[...], w0_ref[...], LANE_CONTRACT, preferred_element_type=F32)
               + lax.dot_general(a1_ref[...], w1_ref[...], LANE_CONTRACT, preferred_element_type=F32))
        _accumulate(o_ref, acc, pl.program_id(1) == 0)

    a_spec = pl.BlockSpec((tm, n), lambda i, s: (i, s))
    out, got = _call(
        body, [a0, a1, w, w], name=name, grid=(t // tm, half),
        in_specs=[a_spec, a_spec, pl.BlockSpec((None, ko, n), lambda i, s: (s, 0, 0)),
                  pl.BlockSpec((None, ko, n), lambda i, s: (half + s, 0, 0))],
        out_specs=[pl.BlockSpec((tm, ko), lambda i, s: (i, 0))],
        out_shape=[jax.ShapeDtypeStruct((t, ko), F32)], comm=comm)
    return out[0], got


def _mm_tn(a, b, s_n, *, tka, name, into=None, shard0=0):
    t, ka = a.shape
    n = b.shape[1] // s_n
    assert b.shape[0] == t and ka % tka == 0

    def body(a_ref, b_ref, *rest):
        rest[-1][...] = lax.dot_general(a_ref[...], b_ref[...], ROW_CONTRACT, preferred_element_type=F32)

    operands, in_specs, aliases = [a, b], [pl.BlockSpec((t, tka), lambda i, s: (0, i)),
                                           pl.BlockSpec((t, n), lambda i, s: (0, s))], {}
    out_shape = jax.ShapeDtypeStruct((s_n, ka, n), F32)
    if into is not None:
        operands, in_specs, aliases = operands + [into], in_specs + [ANY], {2: 0}
        out_shape = jax.ShapeDtypeStruct(into.shape, F32)
    return pl.pallas_call(
        body, grid=(ka // tka, s_n), in_specs=in_specs,
        out_specs=pl.BlockSpec((None, tka, n), lambda i, s: (shard0 + s, i, 0)),
        out_shape=out_shape, input_output_aliases=aliases, name=name)(*operands)


def _gate_up_swiglu(h, w, *, tm, comm=None):
    t, k = h.shape
    s_n, _, n = w.shape
    half = s_n // 2

    def body(h_ref, wg_ref, wu_ref, g_ref, u_ref, a_ref):
        g = jnp.dot(h_ref[...], wg_ref[...], preferred_element_type=F32)
        u = jnp.dot(h_ref[...], wu_ref[...], preferred_element_type=F32)
        g_ref[...] = g.astype(BF16)
        u_ref[...] = u.astype(BF16)
        a_ref[...] = (g * jax.nn.sigmoid(g) * u).astype(BF16)

    col = pl.BlockSpec((tm, n), lambda i, j: (i, j))
    out, got = _call(
        body, [h, w, w], name="gate_up", grid=(t // tm, half),
        in_specs=[pl.BlockSpec((tm, k), lambda i, j: (i, 0)), pl.BlockSpec((None, k, n), lambda i, j: (j, 0, 0)),
                  pl.BlockSpec((None, k, n), lambda i, j: (half + j, 0, 0))],
        out_specs=[col, col, col],
        out_shape=[jax.ShapeDtypeStruct((t, half * n), BF16)] * 3, comm=comm)
    return out[0], out[1], out[2], got


def _down_dx_swiglu_bwd(df, w, g, u, *, tm, tko):
    t, k = df.shape
    _, ko, _ = w.shape
    assert t % tm == 0 and ko % tko == 0

    def body(df_ref, w_ref, g_ref, u_ref, dg_ref, du_ref):
        d = lax.dot_general(df_ref[...], w_ref[...], LANE_CONTRACT, preferred_element_type=F32)
        gg = g_ref[...].astype(F32)
        sig = jax.nn.sigmoid(gg)
        dg_ref[...] = (d * u_ref[...].astype(F32) * (sig * (1.0 + gg * (1.0 - sig)))).astype(BF16)
        du_ref[...] = (d * (gg * sig)).astype(BF16)

    col = pl.BlockSpec((tm, tko), lambda i, j: (i, j))
    return pl.pallas_call(
        body, grid=(t // tm, ko // tko),
        in_specs=[pl.BlockSpec((tm, k), lambda i, j: (i, 0)), pl.BlockSpec((None, tko, k), lambda i, j: (0, j, 0)),
                  col, col],
        out_specs=[col, col], out_shape=[jax.ShapeDtypeStruct((t, ko), BF16)] * 2, name="down_dx")(df, w, g, u)


CHUNK = 256


def _in_proj_dx(d_attn3, d_conv3, w, *, tm, comm=None):
    _, t, _ = d_attn3.shape
    s_n, ko, n = w.shape
    half, per = s_n // 2, n // CHUNK
    assert t % tm == 0

    def body(*refs):
        a_refs, b_refs, wa_ref, wb_ref, o_ref = refs[:per], refs[per:2 * per], refs[2 * per], refs[2 * per + 1], refs[-1]
        acc = jnp.zeros(o_ref.shape, F32)
        for r in range(per):
            cols = slice(r * CHUNK, (r + 1) * CHUNK)
            acc = acc + lax.dot_general(a_refs[r][...], wa_ref[:, cols], LANE_CONTRACT, preferred_element_type=F32)
            acc = acc + lax.dot_general(b_refs[r][...], wb_ref[:, cols], LANE_CONTRACT, preferred_element_type=F32)
        _accumulate(o_ref, acc, pl.program_id(1) == 0)

    piece = lambda r: pl.BlockSpec((None, tm, CHUNK), lambda i, s: ((per * s + r) // 2, i, (per * s + r) % 2))
    out, got = _call(
        body, [d_attn3] * per + [d_conv3] * per + [w, w], name="in_proj_dx", grid=(t // tm, half),
        in_specs=[piece(r) for r in range(per)] * 2
        + [pl.BlockSpec((None, ko, n), lambda i, s: (s, 0, 0)), pl.BlockSpec((None, ko, n), lambda i, s: (half + s, 0, 0))],
        out_specs=[pl.BlockSpec((tm, ko), lambda i, s: (i, 0))],
        out_shape=[jax.ShapeDtypeStruct((t, ko), F32)], comm=comm)
    return out[0], got


def _in_proj_dw(h, d_attn3, d_conv3, s_n, *, tka):
    t, ka = h.shape
    half = s_n // 2
    n = 3 * d_attn3.shape[2] // half
    per = n // CHUNK
    assert ka % tka == 0

    def body(*refs):
        h_ref, o_ref = refs[0], refs[-1]
        for side in range(2):
            for r in range(per):
                o_ref[side, :, r * CHUNK:(r + 1) * CHUNK] = lax.dot_general(
                    h_ref[...], refs[1 + side * per + r][...], ROW_CONTRACT, preferred_element_type=F32)

    piece = lambda r: pl.BlockSpec((None, t, CHUNK), lambda i, s: ((per * s + r) // 2, 0, (per * s + r) % 2))
    out = pl.pallas_call(
        body, grid=(ka // tka, half),
        in_specs=[pl.BlockSpec((t, tka), lambda i, s: (0, i))] + [piece(r) for r in range(per)] * 2,
        out_specs=pl.BlockSpec((2, None, tka, n), lambda i, s: (0, s, i, 0)),
        out_shape=jax.ShapeDtypeStruct((2, half, ka, n), F32), name="in_proj_dw")(h, *[d_attn3] * per, *[d_conv3] * per)
    return out.reshape(s_n, ka, n)


def _rope_tables(positions_col, inv_freq_row):
    t = positions_col.shape[0]

    def body(pos_ref, f_ref, cos_ref, sin_ref):
        ang = pos_ref[...].astype(F32) * f_ref[...]
        cos_ref[...] = jnp.cos(ang)
        sin_ref[...] = jnp.sin(ang)

    return pl.pallas_call(
        body, out_shape=[jax.ShapeDtypeStruct((t, LANES), F32)] * 2, name="rope_tables")(positions_col, inv_freq_row)


def _norm_fwd(x, g3, l, *, tm):
    t, w = x.shape

    def body(x_ref, g_ref, h_ref):
        h_ref[...] = _rms_fwd(x_ref[...], g_ref[...]).astype(BF16)

    return pl.pallas_call(
        body, grid=(t // tm,),
        in_specs=[pl.BlockSpec((tm, w), lambda i: (i, 0)), _gain_spec(g3, l)],
        out_specs=pl.BlockSpec((tm, w), lambda i: (i, 0)),
        out_shape=jax.ShapeDtypeStruct((t, w), BF16), name="norm_fwd")(x, g3)


def _resnorm_fwd(x, y, g_post3, l_post, g_next3, l_next, *, tm):
    t, w = x.shape
    with_next = g_next3 is not None
    row = pl.BlockSpec((tm, w), lambda i: (i, 0))

    def body(x_ref, y_ref, gp_ref, *rest):
        x_new = x_ref[...] + _rms_fwd(y_ref[...], gp_ref[...])
        if with_next:
            gn_ref, xo_ref, h_ref = rest
            h_ref[...] = _rms_fwd(x_new, gn_ref[...]).astype(BF16)
        else:
            (xo_ref,) = rest
        xo_ref[...] = x_new

    ins = [x, y, g_post3] + ([g_next3] if with_next else [])
    in_specs = [row, row, _gain_spec(g_post3, l_post)] + ([_gain_spec(g_next3, l_next)] if with_next else [])
    out_shape = [jax.ShapeDtypeStruct((t, w), F32)] + ([jax.ShapeDtypeStruct((t, w), BF16)] if with_next else [])
    out = pl.pallas_call(body, grid=(t // tm,), in_specs=in_specs, out_specs=[row] * len(out_shape),
                         out_shape=out_shape, name="resnorm_fwd")(*ins)
    return (out[0], out[1]) if with_next else (out[0], None)


def _conv_fwd(proj, conv_w, l):
    t = proj.shape[0]
    col0 = 3 * ATTN_W // LANES

    def body(u_ref, gb_ref, gc_ref, w_ref, y_ref):
        c = gc_ref[...] * u_ref[...]
        row = lax.broadcasted_iota(jnp.int32, c.shape, 0)
        c_prev = jnp.where(row == 0, 0.0, pltpu.roll(c, 1, 0))
        c_next = jnp.where(row == t - 1, 0.0, pltpu.roll(c, t - 1, 0))
        w = w_ref[...]
        y_ref[...] = gb_ref[...] * (w[0:1] * c_prev + w[1:2] * c + w[2:3] * c_next)

    nj = CONV_W // LANES
    cols = lambda base: pl.BlockSpec((t, LANES), lambda j: (0, base + j))
    return pl.pallas_call(
        body, grid=(nj,),
        in_specs=[cols(col0), cols(col0 + nj), cols(col0 + 2 * nj),
                  pl.BlockSpec((None, None, 3, LANES), lambda j: (l, j, 0, 0))],
        out_specs=pl.BlockSpec((t, LANES), lambda j: (0, j)),
        out_shape=jax.ShapeDtypeStruct((t, CONV_W), F32), name="conv_fwd")(proj, proj, proj, conv_w)


def _merge_fwd(attn, conv_y, ga3, gc3, l, *, tm):
    t = attn.shape[0]
    row = pl.BlockSpec((tm, ATTN_W), lambda i: (i, 0))

    def body(a_ref, c_ref, ga_ref, gc_ref, m_ref):
        m_ref[:, :ATTN_W] = _rms_fwd(a_ref[...], ga_ref[...]).astype(BF16)
        m_ref[:, ATTN_W:] = _rms_fwd(c_ref[...], gc_ref[...]).astype(BF16)

    return pl.pallas_call(
        body, grid=(t // tm,),
        in_specs=[row, row, _gain_spec(ga3, l), _gain_spec(gc3, l)],
        out_specs=pl.BlockSpec((tm, D_MODEL), lambda i: (i, 0)),
        out_shape=jax.ShapeDtypeStruct((t, D_MODEL), BF16), name="merge_fwd")(attn, conv_y, ga3, gc3)


def _loss_fwd_bwd(y, target, *, tm):
    t, w = y.shape
    row = pl.BlockSpec((tm, w), lambda i: (i, 0))

    def body(y_ref, t_ref, dy_ref, loss_ref):
        e = y_ref[...] - t_ref[...]
        dy_ref[...] = e * (1.0 / w)
        sq = jnp.sum(e * e, axis=0, keepdims=True) * (0.5 / w)
        part = sq[:, :LANES]
        for j in range(1, w // LANES):
            part = part + sq[:, j * LANES:(j + 1) * LANES]
        _accumulate(loss_ref, part, pl.program_id(0) == 0)

    return pl.pallas_call(
        body, grid=(t // tm,), in_specs=[row, row],
        out_specs=[row, _const_spec((1, LANES), (0, 0))],
        out_shape=[jax.ShapeDtypeStruct((t, w), F32), jax.ShapeDtypeStruct((1, LANES), F32)], name="loss")(y, target)


def _tile_rows(t, nt, lb, d):
    r = t // nt
    q0 = (t % nt) * TQ
    m0 = jnp.clip(q0 - BAND, 0, lb - WIN)
    if d == 1:
        return pl.ds(pl.multiple_of(q0, TQ), TQ), pl.ds(pl.multiple_of(m0, BAND), WIN), m0 - q0
    return pl.ds(r + d * q0, TQ, stride=d), pl.ds(r + d * m0, WIN, stride=d), m0 - q0


def _for_row_chunks(t, fn, chunk=512):
    def step(i, carry):
        fn(pl.ds(pl.multiple_of(i * chunk, chunk), chunk))
        return carry

    lax.fori_loop(0, t // chunk, step, 0)


WINDOW_OFFSETS = (-BAND, 0, -2 * BAND)


def _fill_band_bias(bias_ref):
    rel0 = (lax.broadcasted_iota(jnp.int32, (2 * TQ, WIN), 1)
            - lax.broadcasted_iota(jnp.int32, (2 * TQ, WIN), 0) % TQ)
    for j, off in enumerate(WINDOW_OFFSETS):
        rel = rel0 + off
        bias_ref[j] = jnp.where((rel >= -BAND) & (rel <= BAND), 0.0, NEG_INF)


def _fill_sequence_bias(bias_ref):
    rel = (lax.broadcasted_iota(jnp.int32, (2 * WIN, WIN), 1) - lax.broadcasted_iota(jnp.int32, (2 * WIN, WIN), 0) % WIN)
    bias_ref[...] = jnp.where((rel >= -BAND) & (rel <= BAND), 0.0, NEG_INF)


def _band_bias(bias_ref, off):
    return bias_ref[jnp.where(off == WINDOW_OFFSETS[0], 0, jnp.where(off == WINDOW_OFFSETS[1], 1, 2))]


def _stack_heads(a, first_head):
    return jnp.concatenate([jnp.where(first_head, a, 0.0), jnp.where(first_head, 0.0, a)], axis=0)


def _unstack_heads(a2, first_head):
    n = a2.shape[0] // 2
    return jnp.where(first_head, a2[:n], a2[n:])


def _attn_fwd(proj, comm=None):
    t = proj.shape[0]
    npair = ATTN_W // LANES

    def body(q_ref, k_ref, v_ref, o_ref, lse_ref, o1, o2, l0, l1, l2, m1, m2, bias, bias_seq):
        _fill_band_bias(bias)
        _fill_sequence_bias(bias_seq)
        outs, dens, maxs = (o_ref, o1, o2), (l0, l1, l2), (lse_ref, m1, m2)

        def softmax_tile(b, qrows, krows, n_q, band_bias):
            first_head = lax.broadcasted_iota(jnp.int32, (n_q, LANES), 1) < HEAD_DIM
            q2 = _stack_heads(q_ref[qrows, :] * SCALE, first_head).astype(BF16)
            kw = k_ref[krows, :].astype(BF16)
            vw = jnp.concatenate([v_ref[krows, :].astype(BF16), jnp.ones((WIN, LANES), BF16)], axis=1)
            s = lax.dot_general(q2, kw, LANE_CONTRACT, preferred_element_type=F32) + band_bias
            m = jnp.max(s, axis=-1, keepdims=True)
            pv = jnp.dot(jnp.exp(s - m).astype(BF16), vw, preferred_element_type=F32)
            outs[b][qrows, :] = _unstack_heads(pv[:, :LANES], first_head)
            dens[b][qrows, :] = _unstack_heads(pv[:, LANES:], first_head)
            maxs[b][qrows, :] = _unstack_heads(jnp.broadcast_to(m, (2 * n_q, LANES)), first_head)

        for b, d in enumerate(DILATIONS):
            lb = t // d
            if lb == WIN:
                def sequence(r, carry, b=b, d=d):
                    rows = pl.ds(r, WIN, stride=d)
                    softmax_tile(b, rows, rows, WIN, bias_seq[...])
                    return carry

                lax.fori_loop(0, d, sequence, 0, unroll=4)
                continue
            nt = lb // TQ

            def tile(ti, carry, b=b, d=d, lb=lb, nt=nt):
                qrows, krows, off = _tile_rows(ti, nt, lb, d)
                softmax_tile(b, qrows, krows, TQ, _band_bias(bias, off))
                return carry

            lax.fori_loop(0, d * nt, tile, 0, unroll=8)

        def finish(rows):
            ms = [m_b[rows, :] for m_b in maxs]
            m_all = jnp.maximum(jnp.maximum(ms[0], ms[1]), ms[2])
            ws = [jnp.exp(m_b - m_all) for m_b in ms]
            den = ws[0] * dens[0][rows, :] + ws[1] * dens[1][rows, :] + ws[2] * dens[2][rows, :]
            num = ws[0] * outs[0][rows, :] + ws[1] * outs[1][rows, :] + ws[2] * outs[2][rows, :]
            o_ref[rows, :] = num / den
            lse_ref[rows, :] = m_all + jnp.log(den)

        _for_row_chunks(t, finish, 256)

    cols = lambda base: pl.BlockSpec((t, LANES), lambda g: (0, base + g))
    out, got = _call(
        body, [proj, proj, proj], name="attn_fwd", grid=(npair,),
        in_specs=[cols(0), cols(npair), cols(2 * npair)],
        out_specs=[cols(0), cols(0)],
        out_shape=[jax.ShapeDtypeStruct((t, ATTN_W), F32)] * 2,
        scratch_shapes=[pltpu.VMEM((t, LANES), F32)] * 7 + [pltpu.VMEM((len(WINDOW_OFFSETS), 2 * TQ, WIN), F32),
                                                            pltpu.VMEM((2 * WIN, WIN), F32)],
        comm=comm)
    return out[0], out[1], got


def _attn_bwd(proj, cos, sin, d_attn, lse, delta, comm=None):
    t = proj.shape[0]
    npair = ATTN_W // LANES

    def body(q_ref, k_ref, v_ref, cos_ref, sin_ref, do_ref, l_ref, dl_ref, dqkv_ref,
             dq_acc, dk_acc, dv_acc, bias, bias_seq):
        _fill_band_bias(bias)
        _fill_sequence_bias(bias_seq)
        dq_acc[...] = jnp.zeros(dq_acc.shape, F32)
        dk_acc[...] = jnp.zeros(dk_acc.shape, F32)
        dv_acc[...] = jnp.zeros(dv_acc.shape, F32)
        def stack_column(a):
            return jnp.concatenate([a[:, 0:1], a[:, HEAD_DIM:HEAD_DIM + 1]], axis=0)

        def grad_tile(qrows, krows, n_q, band_bias):
            first_head = lax.broadcasted_iota(jnp.int32, (n_q, LANES), 1) < HEAD_DIM
            q2 = _stack_heads(q_ref[qrows, :] * SCALE, first_head).astype(BF16)
            do2 = _stack_heads(do_ref[qrows, :], first_head).astype(BF16)
            kw = k_ref[krows, :].astype(BF16)
            vw = v_ref[krows, :].astype(BF16)
            s = lax.dot_general(q2, kw, LANE_CONTRACT, preferred_element_type=F32) + band_bias
            p = jnp.exp(s - stack_column(l_ref[qrows, :]))
            dp = lax.dot_general(do2, vw, LANE_CONTRACT, preferred_element_type=F32)
            ds = (p * (dp - stack_column(dl_ref[qrows, :]))).astype(BF16)
            dq2 = jnp.dot(ds, kw, preferred_element_type=F32)
            dq_acc[qrows, :] += _unstack_heads(dq2, first_head) * SCALE
            dk_acc[krows, :] += lax.dot_general(ds, q2, ROW_CONTRACT, preferred_element_type=F32)
            dv_acc[krows, :] += lax.dot_general(p.astype(BF16), do2, ROW_CONTRACT, preferred_element_type=F32)

        for d in DILATIONS:
            lb = t // d
            if lb == WIN:
                def sequence(r, carry, d=d):
                    rows = pl.ds(r, WIN, stride=d)
                    grad_tile(rows, rows, WIN, bias_seq[...])
                    return carry

                lax.fori_loop(0, d, sequence, 0, unroll=2)
                continue
            nt = lb // TQ

            def tile(ti, carry, d=d, lb=lb, nt=nt):
                qrows, krows, off = _tile_rows(ti, nt, lb, d)
                grad_tile(qrows, krows, TQ, _band_bias(bias, off))
                return carry

            lax.fori_loop(0, d * nt, tile, 0, unroll=4)

        def finish(rows):
            dqkv_ref[0, rows, :] = _rope_transpose(dq_acc[rows, :], cos_ref[rows, :], sin_ref[rows, :]).astype(BF16)
            dqkv_ref[1, rows, :] = _rope_transpose(dk_acc[rows, :], cos_ref[rows, :], sin_ref[rows, :]).astype(BF16)
            dqkv_ref[2, rows, :] = dv_acc[rows, :].astype(BF16)

        _for_row_chunks(t, finish)

    cols = lambda base: pl.BlockSpec((t, LANES), lambda g: (0, base + g))
    out, got = _call(
        body, [proj, proj, proj, cos, sin, d_attn, lse, delta], name="attn_bwd", grid=(npair,),
        in_specs=[cols(0), cols(npair), cols(2 * npair), WHOLE_VMEM, WHOLE_VMEM, cols(0), cols(0), cols(0)],
        out_specs=[pl.BlockSpec((3, t, LANES), lambda g: (0, 0, g))],
        out_shape=[jax.ShapeDtypeStruct((3, t, ATTN_W), BF16)],
        scratch_shapes=[pltpu.VMEM((t, LANES), F32)] * 3 + [pltpu.VMEM((len(WINDOW_OFFSETS), 2 * TQ, WIN), F32),
                                                            pltpu.VMEM((2 * WIN, WIN), F32)],
        comm=comm)
    return out[0], got


def _norm_bwd(dres, pre, post, *, tm, comm=None):
    t, w = dres.shape
    row = pl.BlockSpec((tm, w), lambda i: (i, 0))
    gsum = _const_spec((1, w), (0, 0))
    ins, in_specs, out_shape, out_specs = [dres], [row], [], []
    if pre is not None:
        dh, x, g3, l = pre
        ins += [dh, x, g3]
        in_specs += [row, row, _gain_spec(g3, l)]
        out_shape += [jax.ShapeDtypeStruct((t, w), F32), jax.ShapeDtypeStruct((1, w), F32)]
        out_specs += [row, gsum]
    if post is not None:
        y, g3, l = post
        ins += [y, g3]
        in_specs += [row, _gain_spec(g3, l)]
        out_shape += [jax.ShapeDtypeStruct((t, w), BF16), jax.ShapeDtypeStruct((1, w), F32)]
        out_specs += [row, gsum]
    n_in = len(ins)

    def body(*refs):
        first = pl.program_id(0) == 0
        ins_r, outs_r = list(refs[:n_in]), list(refs[n_in:])
        d = ins_r.pop(0)[...]
        if pre is not None:
            dh_ref, x_ref, g_ref = ins_r[:3]
            ins_r = ins_r[3:]
            dx, dg = _rms_bwd(x_ref[...], g_ref[...], dh_ref[...])
            d = d + dx
            outs_r.pop(0)[...] = d
            _accumulate(outs_r.pop(0), dg, first)
        if post is not None:
            y_ref, g_ref = ins_r
            dy, dg = _rms_bwd(y_ref[...], g_ref[...], d)
            outs_r.pop(0)[...] = dy.astype(BF16)
            _accumulate(outs_r.pop(0), dg, first)

    out, got = _call(body, ins, name="norm_bwd", grid=(t // tm,), in_specs=in_specs, out_specs=out_specs,
                     out_shape=out_shape, comm=comm)
    d_new, dg_pre = (out.pop(0), out.pop(0)) if pre is not None else (None, None)
    dy, dg_post = (out.pop(0), out.pop(0)) if post is not None else (None, None)
    return d_new, dy, dg_pre, dg_post, got


def _merge_bwd(d_merged, attn, conv_y, ga3, gc3, l, *, tm):
    t = attn.shape[0]
    row = pl.BlockSpec((tm, ATTN_W), lambda i: (i, 0))
    gsum = _const_spec((1, ATTN_W), (0, 0))

    def body(dma_ref, dmc_ref, a_ref, c_ref, ga_ref, gc_ref, da_ref, dl_ref, dc_ref, dga_ref, dgc_ref):
        first = pl.program_id(0) == 0
        attn_t = a_ref[...]
        da, dga = _rms_bwd(attn_t, ga_ref[...], dma_ref[...])
        dc, dgc = _rms_bwd(c_ref[...], gc_ref[...], dmc_ref[...])
        da_ref[...] = da
        dc_ref[...] = dc
        same_head = (lax.broadcasted_iota(jnp.int32, (ATTN_W, ATTN_W), 0) // HEAD_DIM
                     == lax.broadcasted_iota(jnp.int32, (ATTN_W, ATTN_W), 1) // HEAD_DIM).astype(BF16)
        rest = da * attn_t
        total = jnp.zeros(rest.shape, F32)
        for _ in range(3):
            term = rest.astype(BF16)
            total = total + jnp.dot(term, same_head, preferred_element_type=F32)
            rest = rest - term.astype(F32)
        dl_ref[...] = total
        _accumulate(dga_ref, dga, first)
        _accumulate(dgc_ref, dgc, first)

    return pl.pallas_call(
        body, grid=(t // tm,),
        in_specs=[pl.BlockSpec((tm, ATTN_W), lambda i: (i, 0)), pl.BlockSpec((tm, CONV_W), lambda i: (i, 1)),
                  row, row, _gain_spec(ga3, l), _gain_spec(gc3, l)],
        out_specs=[row, row, row, gsum, gsum],
        out_shape=[jax.ShapeDtypeStruct((t, ATTN_W), F32)] * 3 + [jax.ShapeDtypeStruct((1, ATTN_W), F32)] * 2,
        name="merge_bwd")(d_merged, d_merged, attn, conv_y, ga3, gc3)


def _conv_bwd(proj, conv_w, l, d_conv_y):
    t = proj.shape[0]
    col0 = 3 * ATTN_W // LANES
    nj = CONV_W // LANES

    def body(u_ref, gb_ref, gc_ref, w_ref, dy_ref, d3_ref, dw_ref):
        u, gc, dy = u_ref[...], gc_ref[...], dy_ref[...]
        row = lax.broadcasted_iota(jnp.int32, u.shape, 0)
        down = lambda a: jnp.where(row == 0, 0.0, pltpu.roll(a, 1, 0))
        up = lambda a: jnp.where(row == t - 1, 0.0, pltpu.roll(a, t - 1, 0))
        w = w_ref[...]
        c = gc * u
        c_prev, c_next = down(c), up(c)
        d3_ref[1] = (dy * (w[0:1] * c_prev + w[1:2] * c + w[2:3] * c_next)).astype(BF16)
        dz = dy * gb_ref[...]
        dc = w[0:1] * up(dz) + w[1:2] * dz + w[2:3] * down(dz)
        d3_ref[0] = (dc * gc).astype(BF16)
        d3_ref[2] = (dc * u).astype(BF16)
        dw_ref[0:1, :] = jnp.sum(dz * c_prev, axis=0, keepdims=True)
        dw_ref[1:2, :] = jnp.sum(dz * c, axis=0, keepdims=True)
        dw_ref[2:3, :] = jnp.sum(dz * c_next, axis=0, keepdims=True)

    cols = lambda base: pl.BlockSpec((t, LANES), lambda j: (0, base + j))
    return pl.pallas_call(
        body, grid=(nj,),
        in_specs=[cols(col0), cols(col0 + nj), cols(col0 + 2 * nj),
                  pl.BlockSpec((None, None, 3, LANES), lambda j: (l, j, 0, 0)), cols(0)],
        out_specs=[pl.BlockSpec((3, t, LANES), lambda j: (0, 0, j)), pl.BlockSpec((None, 3, LANES), lambda j: (j, 0, 0))],
        out_shape=[jax.ShapeDtypeStruct((3, t, CONV_W), BF16), jax.ShapeDtypeStruct((nj, 3, LANES), F32)],
        name="conv_bwd")(proj, proj, proj, conv_w, d_conv_y)


def _own_shard_slab(w, l, place, dtype):
    _, rows, cols = w.shape
    tr = rows if rows <= 704 else 512
    assert rows % tr == 0

    def body(p_ref, w_ref, o_ref):
        del p_ref
        o_ref[...] = w_ref[...].astype(dtype)

    grid_spec = pltpu.PrefetchScalarGridSpec(
        num_scalar_prefetch=1, grid=(rows // tr,),
        in_specs=[pl.BlockSpec((None, tr, cols), lambda i, p: (l, i, 0))],
        out_specs=pl.BlockSpec((None, tr, cols), lambda i, p: (p[0], i, 0)))
    return pl.pallas_call(body, grid_spec=grid_spec, name="own_shard_slab",
                          out_shape=jax.ShapeDtypeStruct((N_CHIPS, rows, cols), dtype))(place, w)


def _own_conv_slab(w, place):
    depth = w.shape[0]

    def body(p_ref, w_ref, o_ref):
        del p_ref
        o_ref[...] = w_ref[...]

    grid_spec = pltpu.PrefetchScalarGridSpec(
        num_scalar_prefetch=1, grid=(depth,),
        in_specs=[pl.BlockSpec((None, 3, LANES), lambda l, p: (l, 0, 0))],
        out_specs=pl.BlockSpec((None, None, 3, LANES), lambda l, p: (l, p[0], 0, 0)))
    return pl.pallas_call(body, grid_spec=grid_spec, name="own_conv_slab",
                          out_shape=jax.ShapeDtypeStruct((depth, N_CHIPS, 3, LANES), F32))(place, w)


def _add_halves(grad, got, place):
    s_n, rows, cols = grad.shape
    hr = rows // 2

    def body(p_ref, g_ref, r_ref, o_ref):
        del p_ref
        o_ref[...] = (g_ref[...] + r_ref[...]).astype(BF16)

    grid_spec = pltpu.PrefetchScalarGridSpec(
        num_scalar_prefetch=1, grid=(s_n,),
        in_specs=[pl.BlockSpec((None, hr, cols), lambda s, p: (s, p[1], 0)),
                  pl.BlockSpec((None, hr, cols), lambda s, p: (s, 0, 0))],
        out_specs=pl.BlockSpec((None, hr, cols), lambda s, p: (s, 0, 0)))
    return pl.pallas_call(body, grid_spec=grid_spec, out_shape=jax.ShapeDtypeStruct((s_n, hr, cols), BF16),
                          name="add_halves")(place, grad, got)


def _sum_partials(partial, got, place, acc, l):
    _, hr, cols = partial.shape

    def body(p_ref, mine_ref, got_ref, acc_ref, o_ref):
        del p_ref, acc_ref
        total = mine_ref[...].astype(F32)
        for k in range(3):
            total = total + got_ref[k].astype(F32)
        o_ref[...] = total

    grid_spec = pltpu.PrefetchScalarGridSpec(
        num_scalar_prefetch=1, grid=(1,),
        in_specs=[pl.BlockSpec((None, hr, cols), lambda i, p: (p[0], 0, 0)),
                  pl.BlockSpec((3, hr, cols), lambda i, p: (0, 0, 0)), ANY],
        out_specs=pl.BlockSpec((None, hr, cols), lambda i, p: (l, p[1], 0)))
    return pl.pallas_call(body, grid_spec=grid_spec, out_shape=jax.ShapeDtypeStruct(acc.shape, F32),
                          input_output_aliases={3: 0}, name="sum_partials")(place, partial, got, acc)


def _allreduce_small(vec, loss_row):
    rows = vec.shape[0]

    def body(v_ref, o_ref, slots, send_sems, recv_sems):
        x, y, c, _ = _place()
        me = 4 * x + 2 * y + c
        slots[me] = v_ref[...]
        copies = []
        for k in range(1, N_DEV):
            flip = lambda v, bit: 1 - v if bit else v
            peer = (flip(x, k & 4), flip(y, k & 2), flip(c, k & 1))
            copies.append(_remote(v_ref, slots.at[me], send_sems.at[k - 1], recv_sems.at[k - 1], peer))
        for cp in copies:
            cp.start()
        for k in range(1, N_DEV):
            flip = lambda v, bit: 1 - v if bit else v
            peer_id = 4 * flip(x, k & 4) + 2 * flip(y, k & 2) + flip(c, k & 1)
            _remote(v_ref, slots.at[peer_id], send_sems.at[k - 1], recv_sems.at[k - 1], (x, y, c)).wait_recv()
        for cp in copies:
            cp.wait_send()
        total = slots[0]
        for dev in range(1, N_DEV):
            total = total + slots[dev]
        o_ref[...] = total
        o_ref[loss_row:loss_row + 1, :] = jnp.broadcast_to(
            jnp.sum(total[loss_row:loss_row + 1, :], axis=-1, keepdims=True), (1, LANES))

    return pl.pallas_call(
        body, in_specs=[WHOLE_VMEM], out_specs=WHOLE_VMEM, out_shape=jax.ShapeDtypeStruct((rows, LANES), F32),
        scratch_shapes=[pltpu.VMEM((N_DEV, rows, LANES), F32), pltpu.SemaphoreType.DMA((N_DEV - 1,)),
                        pltpu.SemaphoreType.DMA((N_DEV - 1,))],
        name="allreduce_small")(vec)


def _adamw(w, g, m, v, *, tr, emit_grad=False):
    depth, rows, cols = w.shape
    assert rows % tr == 0
    c1 = float(np.float32(1.0 - ADAM_B1 ** ADAM_STEP))
    c2 = float(np.float32(1.0 - ADAM_B2 ** ADAM_STEP))

    def body(w_ref, g_ref, m_ref, v_ref, d_ref, mo_ref, vo_ref, *go_ref):
        g_t = g_ref[...]
        if emit_grad:
            go_ref[0][...] = g_t
        m_new = ADAM_B1 * m_ref[...] + (1.0 - ADAM_B1) * g_t
        v_new = ADAM_B2 * v_ref[...] + (1.0 - ADAM_B2) * (g_t * g_t)
        mo_ref[...] = m_new
        vo_ref[...] = v_new
        d_ref[...] = -ADAM_LR * ((m_new / c1) / (jnp.sqrt(v_new / c2) + ADAM_EPS) + ADAM_WD * w_ref[...])

    blk = pl.BlockSpec((None, tr, cols), lambda l, i: (l, i, 0))
    return pl.pallas_call(
        body, grid=(depth, rows // tr), in_specs=[blk] * 4, out_specs=[blk] * (4 if emit_grad else 3),
        out_shape=[jax.ShapeDtypeStruct(w.shape, F32)] * (4 if emit_grad else 3), name="adamw")(w, g, m, v)


def _local_step(x, positions, target, gains, exchange):
    t = x.shape[0]
    tm = 512
    inv_freq = ROPE_THETA ** (-jnp.arange(0, ROPE_DIM, 2, dtype=F32) / ROPE_DIM)
    lane = np.arange(LANES) % HEAD_DIM
    freq_row = jnp.where(lane < ROPE_DIM, inv_freq[lane % (ROPE_DIM // 2)], 0.0).astype(F32)[None, :]
    cos, sin = _rope_tables(positions.reshape(t, 1), freq_row)

    def hosted(tag, fn, *args, **kwargs):
        *out, got = fn(*args, comm=exchange.host(tag), **kwargs)
        if got is not None:
            exchange.hosted(tag, got)
        return out[0] if len(out) == 1 else out

    saved = []
    h1 = _norm_fwd(x, gains["pre_mix_norm"], 0, tm=tm)
    for l in range(DEPTH):
        proj = hosted(("fwd", l, "in_proj"), _in_proj, h1, exchange.weight("w_in", l), cos, sin, tm=1024)
        attn, lse = hosted(("fwd", l, "attn"), _attn_fwd, proj)
        conv_y = _conv_fwd(proj, exchange.weight("conv_w", l), l)
        merged = _merge_fwd(attn, conv_y, gains["attn_out_norm"], gains["conv_out_norm"], l, tm=tm)
        mix = hosted(("fwd", l, "out_proj"), _mm_nn, merged, exchange.weight("w_out", l), tm=1024, name="out_proj")
        x1, h2 = _resnorm_fwd(x, mix, gains["post_mix_norm"], l, gains["pre_ffn_norm"], l, tm=tm)
        g, u, act = hosted(("fwd", l, "gate_up"), _gate_up_swiglu, h2, exchange.weight("w_gate_up", l), tm=1024)
        f = hosted(("fwd", l, "down"), _mm_nn, act, exchange.weight("w_down", l), tm=1024, name="down")
        nxt = (gains["pre_mix_norm"], l + 1) if l + 1 < DEPTH else (None, None)
        x2, h1_next = _resnorm_fwd(x1, f, gains["post_ffn_norm"], l, *nxt, tm=tm)
        saved.append(dict(x=x, h1=h1, proj=proj, attn=attn, lse=lse, conv_y=conv_y, merged=merged, mix=mix,
                          x1=x1, h2=h2, g=g, u=u, act=act, f=f))
        x, h1 = x2, h1_next

    dres, loss_lanes = _loss_fwd_bwd(x, target, tm=tm)

    g_gain = {k: [None] * DEPTH for k in gains}
    g_conv = [None] * DEPTH
    _, df, _, g_gain["post_ffn_norm"][DEPTH - 1], _ = _norm_bwd(
        dres, None, (saved[-1]["f"], gains["post_ffn_norm"], DEPTH - 1), tm=tm)
    for l in reversed(range(DEPTH)):
        sv = saved[l]
        w = {k: exchange.weight(k, l) for k in MATRIX_NAMES + ("conv_w",)}
        dg, du = _down_dx_swiglu_bwd(df, w["w_down"], sv["g"], sv["u"], tm=512, tko=FFN // 2)
        g_down = _mm_tn(sv["act"], df, 1, tka=256, name="down_dw")
        dh2 = hosted(("bwd", l, "gate_up_dx"), _mm_nt_pair, dg, du, w["w_gate_up"], tm=1024, name="gate_up_dx")
        g_gate_up = _mm_tn(sv["h2"], dg, N_CHIPS // 2, tka=512, name="gate_up_dw",
                           into=lax.empty(w["w_gate_up"].shape, F32))
        g_gate_up = _mm_tn(sv["h2"], du, N_CHIPS // 2, tka=512, name="gate_up_dw", into=g_gate_up,
                           shard0=N_CHIPS // 2)
        exchange.grads(l, "ffn", dict(w_down=g_down.reshape(N_CHIPS, FFN // N_CHIPS, D_MODEL), w_gate_up=g_gate_up))
        dx1, dmix, g_gain["pre_ffn_norm"][l], g_gain["post_mix_norm"][l] = hosted(
            ("bwd", l, "norm_mid"), _norm_bwd,
            dres, (dh2, sv["x1"], gains["pre_ffn_norm"], l), (sv["mix"], gains["post_mix_norm"], l), tm=tm)
        d_merged = hosted(("bwd", l, "out_proj_dx"), _mm_nt, dmix, w["w_out"], tm=1024, tko=D_MODEL, name="out_proj_dx")
        g_out = _mm_tn(sv["merged"], dmix, 1, tka=512, name="out_proj_dw")
        d_attn, delta, d_conv_y, g_gain["attn_out_norm"][l], g_gain["conv_out_norm"][l] = _merge_bwd(
            d_merged, sv["attn"], sv["conv_y"], gains["attn_out_norm"], gains["conv_out_norm"], l, tm=tm)
        d_attn3 = hosted(("bwd", l, "attn"), _attn_bwd, sv["proj"], cos, sin, d_attn, sv["lse"], delta)
        d_conv3, g_conv[l] = _conv_bwd(sv["proj"], w["conv_w"], l, d_conv_y)
        g_in = _in_proj_dw(sv["h1"], d_attn3, d_conv3, N_CHIPS, tka=512)
        exchange.grads(l, "mix", dict(w_out=g_out.reshape(N_CHIPS, D_MODEL // N_CHIPS, D_MODEL), w_in=g_in))
        dh1 = hosted(("bwd", l, "in_proj_dx"), _in_proj_dx, d_attn3, d_conv3, w["w_in"], tm=1024)
        below = (saved[l - 1]["f"], gains["post_ffn_norm"], l - 1) if l > 0 else None
        dres, df, g_gain["pre_mix_norm"][l], g_below = hosted(
            ("bwd", l, "norm_low"), _norm_bwd, dx1, (dh1, sv["x"], gains["pre_mix_norm"], l), below, tm=tm)
        if l > 0:
            g_gain["post_ffn_norm"][l - 1] = g_below

    g_gain = {k: jnp.concatenate(v, axis=0) for k, v in g_gain.items()}
    return loss_lanes, dres, g_gain, jnp.stack(g_conv, axis=0)


class _Exchange:
    GATHER_HOSTS = {"in_proj": (("w_out", 0), ("w_down", 0)), "attn": (("w_gate_up", 0),), "gate_up": (("w_in", 1),)}

    @staticmethod
    def _reduce_hosts(group, l):
        if group == "ffn":
            return "norm_mid", "attn", l
        if l > 0:
            return "norm_low", "gate_up_dx", l - 1
        return "in_proj_dx", "norm_low", l

    def __init__(self, params, place):
        self.place = place
        self.slabs = {k: [_own_shard_slab(params[k], l, place, BF16) for l in range(DEPTH)] for k in MATRIX_NAMES}
        self.gathered = {k: [None] * DEPTH for k in MATRIX_NAMES}
        self.gathered["w_in"][0], self.conv_w = _run_comm(
            _gather_comm([self.slabs["w_in"][0]], _own_conv_slab(params["conv_w"], place)), "gather_first")
        self.full = {k: lax.empty(params[k].shape, F32) for k in MATRIX_NAMES}
        self.pending = {}
        self.raw = {}

    def weight(self, name, l):
        if name == "conv_w":
            return self.conv_w
        g = self.gathered[name][l]
        return g.reshape(1, g.shape[0] * g.shape[1], g.shape[2]) if name in ("w_out", "w_down") else g

    def host(self, tag):
        phase, l, kernel = tag
        if phase == "fwd":
            carried = [(name, l + ahead) for name, ahead in self.GATHER_HOSTS.get(kernel, ()) if l + ahead < DEPTH]
            return _gather_comm([self.slabs[name][layer] for name, layer in carried]) if carried else None
        if tag in self.pending:
            stage, _, _, arrays = self.pending[tag]
            return _halves_comm(arrays) if stage == "halves" else _partials_comm(arrays)
        return None

    def hosted(self, tag, results):
        phase, l, kernel = tag
        if phase == "fwd":
            carried = [(name, l + ahead) for name, ahead in self.GATHER_HOSTS[kernel] if l + ahead < DEPTH]
            for (name, layer), slab in zip(carried, results):
                self.gathered[name][layer] = slab
            return
        stage, gl, group, arrays = self.pending.pop(tag)
        names = list(self.raw[(gl, group)])
        if stage == "partials":
            self._finish_reduction(gl, names, arrays, results)
            return
        partials = [_add_halves(self.raw[(gl, group)][k], r, self.place) for k, r in zip(names, results)]
        _, ici_kernel, ici_layer = self._reduce_hosts(group, gl)
        self.pending[("bwd", ici_layer, ici_kernel)] = ("partials", gl, group, partials)

    def grads(self, l, group, grads):
        self.raw[(l, group)] = grads
        self.pending[("bwd", l, self._reduce_hosts(group, l)[0])] = ("halves", l, group, [grads[k] for k in grads])

    def _finish_reduction(self, l, names, partials, others):
        for k, p, q in zip(names, partials, others):
            self.full[k] = _sum_partials(p, q, self.place, self.full[k], l)
        shared = _run_comm(_share_comm([self.full[k] for k in names], l), "share_halves")
        for k, g in zip(names, shared):
            self.full[k] = g


def kernel(x, positions, pre_mix_norm, w_in, conv_w, attn_out_norm, conv_out_norm, w_out, post_mix_norm, pre_ffn_norm, w_gate_up, w_down, post_ffn_norm, loss_target, m_pre_mix_norm, m_w_in, m_conv_w, m_attn_out_norm, m_conv_out_norm, m_w_out, m_post_mix_norm, m_pre_ffn_norm, m_w_gate_up, m_w_down, m_post_ffn_norm, v_pre_mix_norm, v_w_in, v_conv_w, v_attn_out_norm, v_conv_out_norm, v_w_out, v_post_mix_norm, v_pre_ffn_norm, v_w_gate_up, v_w_down, v_post_ffn_norm):
    params = dict(pre_mix_norm=pre_mix_norm, w_in=w_in, conv_w=conv_w, attn_out_norm=attn_out_norm,
                  conv_out_norm=conv_out_norm, w_out=w_out, post_mix_norm=post_mix_norm, pre_ffn_norm=pre_ffn_norm,
                  w_gate_up=w_gate_up, w_down=w_down, post_ffn_norm=post_ffn_norm)
    mom1 = dict(pre_mix_norm=m_pre_mix_norm, w_in=m_w_in, conv_w=m_conv_w, attn_out_norm=m_attn_out_norm,
                conv_out_norm=m_conv_out_norm, w_out=m_w_out, post_mix_norm=m_post_mix_norm,
                pre_ffn_norm=m_pre_ffn_norm, w_gate_up=m_w_gate_up, w_down=m_w_down, post_ffn_norm=m_post_ffn_norm)
    mom2 = dict(pre_mix_norm=v_pre_mix_norm, w_in=v_w_in, conv_w=v_conv_w, attn_out_norm=v_attn_out_norm,
                conv_out_norm=v_conv_out_norm, w_out=v_w_out, post_mix_norm=v_post_mix_norm,
                pre_ffn_norm=v_pre_ffn_norm, w_gate_up=v_w_gate_up, w_down=v_w_down, post_ffn_norm=v_post_ffn_norm)
    xi, yi, ci = lax.axis_index("x"), lax.axis_index("y"), lax.axis_index("c")
    place = jnp.stack([2 * xi + yi, ci]).astype(jnp.int32)

    exchange = _Exchange(params, place)
    gains = {k: params[k][:, None, :] for k in GAIN_NAMES}
    loss_lanes, grad_x, g_gain, g_conv = _local_step(x[0], positions[0], loss_target[0], gains, exchange)
    grad = dict(exchange.full)

    small = [g_gain[k].reshape(-1) for k in GAIN_NAMES] + [g_conv.reshape(-1), loss_lanes.reshape(-1)]
    sizes = [int(s.shape[0]) for s in small]
    flat = jnp.concatenate(small)
    loss_row = (sum(sizes) - LANES) // LANES
    rows = -(-flat.shape[0] // (8 * LANES)) * 8
    flat = jnp.pad(flat, (0, rows * LANES - flat.shape[0])).reshape(rows, LANES)
    total = _allreduce_small(flat, loss_row).reshape(-1)
    offsets = np.cumsum([0] + sizes)
    for i, k in enumerate(GAIN_NAMES):
        grad[k] = total[offsets[i]:offsets[i + 1]].reshape(params[k].shape)
    conv_all = total[offsets[6]:offsets[7]].reshape(DEPTH, N_CHIPS, 3, LANES)
    grad["conv_w"] = lax.dynamic_index_in_dim(conv_all, 2 * xi + yi, axis=1, keepdims=False)
    loss = total[offsets[7]]

    delta, new_m, new_v = {}, {}, {}
    for k in WEIGHT_ORDER:
        shape = params[k].shape
        if k in MATRIX_NAMES:
            tr = {1024: 512, 704: 352, 256: 256}[shape[1]]
            delta[k], new_m[k], new_v[k], grad[k] = _adamw(params[k], grad[k], mom1[k], mom2[k], tr=tr, emit_grad=True)
        else:
            as3 = (lambda a: a) if len(shape) == 3 else (lambda a: a[None])
            d, m, v = _adamw(as3(params[k]), as3(grad[k]), as3(mom1[k]), as3(mom2[k]), tr=as3(params[k]).shape[1])
            delta[k], new_m[k], new_v[k] = d.reshape(shape), m.reshape(shape), v.reshape(shape)

    return (loss, grad_x[None], *[grad[k] for k in WEIGHT_ORDER], *[delta[k] for k in WEIGHT_ORDER],
            *[new_m[k] for k in WEIGHT_ORDER], *[new_v[k] for k in WEIGHT_ORDER])
```

```python
import functools
from typing import Callable, NamedTuple

import numpy as np
import jax
import jax.numpy as jnp
from jax import lax
from jax.experimental import pallas as pl
from jax.experimental.pallas import tpu as pltpu

F32 = jnp.float32
BF16 = jnp.bfloat16
MESH = pl.DeviceIdType.MESH

D_MODEL = 1024
ATTN_W = 512
CONV_W = 512
HEAD_DIM = 64
ROPE_DIM = 16
ROPE_THETA = 500000.0
FFN = 2816
DEPTH = 4
RMS_EPS = 1e-6
NEG_INF = -1e30
N_CHIPS = 4
N_DEV = 8
LANES = 128
BF16_ROWS = 16
DILATIONS = (1, 4, 16)
BAND = 64
TQ = 128
WIN = TQ + 2 * BAND
SCALE = HEAD_DIM ** -0.5

ADAM_LR = 0.001
ADAM_B1 = 0.9
ADAM_B2 = 0.999
ADAM_EPS = 1e-08
ADAM_WD = 0.01
ADAM_STEP = 10

GAIN_NAMES = ("pre_mix_norm", "attn_out_norm", "conv_out_norm", "post_mix_norm", "pre_ffn_norm", "post_ffn_norm")
MATRIX_NAMES = ("w_in", "w_out", "w_gate_up", "w_down")
WEIGHT_ORDER = ("pre_mix_norm", "w_in", "conv_w", "attn_out_norm", "conv_out_norm", "w_out", "post_mix_norm",
                "pre_ffn_norm", "w_gate_up", "w_down", "post_ffn_norm")

ANY = pl.BlockSpec(memory_space=pl.ANY)
WHOLE_VMEM = pl.BlockSpec(memory_space=pltpu.VMEM)
LANE_CONTRACT = (((1,), (1,)), ((), ()))
ROW_CONTRACT = (((0,), (0,)), ((), ()))


def _const_spec(block, index):
    return pl.BlockSpec(block, lambda *_: index)


def _gain_spec(g3, l):
    return _const_spec((None, 1, g3.shape[-1]), (l, 0, 0))


class _Comm(NamedTuple):
    ins: tuple
    inouts: tuple
    out_shapes: tuple
    n_sems: int
    start: Callable
    finish: Callable


def _place():
    x, y, c = lax.axis_index("x"), lax.axis_index("y"), lax.axis_index("c")
    other_chips = [(1 - x, y), (x, 1 - y), (1 - x, 1 - y)]
    return x, y, c, other_chips


def _remote(src, dst, send_sem, recv_sem, to):
    return pltpu.make_async_remote_copy(src_ref=src, dst_ref=dst, send_sem=send_sem, recv_sem=recv_sem,
                                        device_id=to, device_id_type=MESH)


def _call(body, operands, *, name, grid, in_specs, out_specs, out_shape, scratch_shapes=(), comm=None):
    in_specs, out_specs, out_shape = list(in_specs), list(out_specs), list(out_shape)
    scratch_shapes = list(scratch_shapes)
    if comm is None:
        out = pl.pallas_call(body, grid=grid, in_specs=in_specs, out_specs=out_specs, out_shape=out_shape,
                             scratch_shapes=scratch_shapes, name=name)(*operands)
        return list(out), None
    n_in, n_out, n_scr = len(in_specs), len(out_shape), len(scratch_shapes)
    n_ci, n_cio, n_co = len(comm.ins), len(comm.inouts), len(comm.out_shapes)

    def hosted(*refs):
        refs = list(refs)
        ins, c_ins = refs[:n_in], refs[n_in:n_in + n_ci]
        base = n_in + n_ci + n_cio
        outs = refs[base:base + n_out]
        c_io = refs[base + n_out:base + n_out + n_cio]
        c_out = refs[base + n_out + n_cio:base + n_out + n_cio + n_co]
        scr = refs[base + n_out + n_cio + n_co:]
        send_sems, recv_sems = scr[n_scr], scr[n_scr + 1]
        if grid:
            first = functools.reduce(jnp.logical_and, [pl.program_id(a) == 0 for a in range(len(grid))])
            last = functools.reduce(jnp.logical_and, [pl.program_id(a) == grid[a] - 1 for a in range(len(grid))])
            pl.when(first)(lambda: comm.start(c_ins, c_io, c_out, send_sems, recv_sems))
            body(*ins, *outs, *scr[:n_scr])
            pl.when(last)(lambda: comm.finish(c_ins, c_io, c_out, send_sems, recv_sems))
        else:
            comm.start(c_ins, c_io, c_out, send_sems, recv_sems)
            body(*ins, *outs, *scr[:n_scr])
            comm.finish(c_ins, c_io, c_out, send_sems, recv_sems)

    res = pl.pallas_call(
        hosted, grid=grid, in_specs=in_specs + [ANY] * (n_ci + n_cio), out_specs=out_specs + [ANY] * (n_cio + n_co),
        out_shape=out_shape + [jax.ShapeDtypeStruct(a.shape, a.dtype) for a in comm.inouts] + list(comm.out_shapes),
        input_output_aliases={n_in + n_ci + i: n_out + i for i in range(n_cio)},
        scratch_shapes=scratch_shapes + [pltpu.SemaphoreType.DMA((comm.n_sems,))] * 2,
        name=name)(*operands, *comm.ins, *comm.inouts)
    return list(res[:n_out]), list(res[n_out:])


def _run_comm(comm, name):
    return _call(lambda: None, [], name=name, grid=(), in_specs=[], out_specs=[], out_shape=[], comm=comm)[1]


def _row_half(ref, lead, core, rows, align):
    hr = rows // 2
    return ref.at[(*lead, pl.ds(pl.multiple_of(core * hr, align), hr), slice(None))]


def _gather_comm(slabs, conv_slab=None):
    n = len(slabs)
    n_conv = 0 if conv_slab is None else 3

    def direct(ios, send, recv):
        x, y, c, chips = _place()
        copies = []
        for a in range(n):
            own = _row_half(ios[a], (2 * x + y,), c, slabs[a].shape[1], BF16_ROWS)
            copies += [_remote(own, own, send.at[a * 3 + j], recv.at[a * 3 + j], (*chip, c))
                       for j, chip in enumerate(chips)]
        if conv_slab is not None:
            own = ios[n].at[:, 2 * x + y]
            copies += [_remote(own, own, send.at[6 * n + j], recv.at[6 * n + j], (*chip, c))
                       for j, chip in enumerate(chips)]
        return copies

    def start(ins, ios, outs, send, recv):
        for cp in direct(ios, send, recv):
            cp.start()

    def finish(ins, ios, outs, send, recv):
        x, y, c, chips = _place()
        sibling = (x, y, 1 - c)
        passed = []
        for a in range(n):
            for j, chip in enumerate(chips):
                landed = _row_half(ios[a], (2 * chip[0] + chip[1],), c, slabs[a].shape[1], BF16_ROWS)
                _remote(landed, landed, send.at[a * 3 + j], recv.at[a * 3 + j], (*chip, c)).wait_recv()
                fwd = _remote(landed, landed, send.at[3 * n + a * 3 + j], recv.at[3 * n + a * 3 + j], sibling)
                fwd.start()
                passed.append(fwd)
        if conv_slab is not None:
            for j, chip in enumerate(chips):
                landed = ios[n].at[:, 2 * chip[0] + chip[1]]
                _remote(landed, landed, send.at[6 * n + j], recv.at[6 * n + j], (*chip, c)).wait_recv()
        for a in range(n):
            for j, chip in enumerate(chips):
                landed = _row_half(ios[a], (2 * chip[0] + chip[1],), 1 - c, slabs[a].shape[1], BF16_ROWS)
                _remote(landed, landed, send.at[3 * n + a * 3 + j], recv.at[3 * n + a * 3 + j], sibling).wait_recv()
        for cp in direct(ios, send, recv) + passed:
            cp.wait_send()

    inouts = tuple(slabs) + (() if conv_slab is None else (conv_slab,))
    return _Comm((), inouts, (), 6 * n + n_conv, start, finish)


def _halves_comm(grads):
    n = len(grads)

    def copies(ins, outs, send, recv):
        x, y, c, _ = _place()
        return [_remote(_row_half(ins[a], (slice(None),), 1 - c, grads[a].shape[1], 8), outs[a],
                        send.at[a], recv.at[a], (x, y, 1 - c)) for a in range(n)]

    def start(ins, ios, outs, send, recv):
        for cp in copies(ins, outs, send, recv):
            cp.start()

    def finish(ins, ios, outs, send, recv):
        for cp in copies(ins, outs, send, recv):
            cp.wait()

    out_shapes = tuple(jax.ShapeDtypeStruct((g.shape[0], g.shape[1] // 2, g.shape[2]), F32) for g in grads)
    return _Comm(tuple(grads), (), out_shapes, n, start, finish)


def _partials_comm(partials):
    n = len(partials)

    def copies(ins, outs, send, recv):
        x, y, c, chips = _place()
        return [_remote(ins[a].at[2 * chip[0] + chip[1]], outs[a].at[k], send.at[a * 3 + k], recv.at[a * 3 + k],
                        (*chip, c)) for a in range(n) for k, chip in enumerate(chips)]

    def start(ins, ios, outs, send, recv):
        for cp in copies(ins, outs, send, recv):
            cp.start()

    def finish(ins, ios, outs, send, recv):
        for cp in copies(ins, outs, send, recv):
            cp.wait()

    out_shapes = tuple(jax.ShapeDtypeStruct((3,) + p.shape[1:], BF16) for p in partials)
    return _Comm(tuple(partials), (), out_shapes, 3 * n, start, finish)


def _share_comm(grads, l):
    n = len(grads)

    def start(ins, ios, outs, send, recv):
        x, y, c, _ = _place()
        for a in range(n):
            mine = _row_half(ios[a], (l,), c, grads[a].shape[1], 8)
            _remote(mine, mine, send.at[a], recv.at[a], (x, y, 1 - c)).start()

    def finish(ins, ios, outs, send, recv):
        x, y, c, _ = _place()
        for a in range(n):
            theirs = _row_half(ios[a], (l,), 1 - c, grads[a].shape[1], 8)
            _remote(theirs, theirs, send.at[a], recv.at[a], (x, y, 1 - c)).wait()

    return _Comm((), tuple(grads), (), n, start, finish)


def _rms_fwd(x, g):
    r = lax.rsqrt(jnp.mean(x * x, axis=-1, keepdims=True) + RMS_EPS)
    return (x * r) * g


def _rms_bwd(x, g, dy):
    r = lax.rsqrt(jnp.mean(x * x, axis=-1, keepdims=True) + RMS_EPS)
    xh = x * r
    u = dy * g
    dx = r * (u - xh * jnp.mean(xh * u, axis=-1, keepdims=True))
    return dx, jnp.sum(dy * xh, axis=0, keepdims=True)


def _accumulate(ref, value, first):
    @pl.when(first)
    def _():
        ref[...] = value

    @pl.when(jnp.logical_not(first))
    def _():
        ref[...] += value


def _rope_coeffs(cos, sin):
    m = lax.broadcasted_iota(jnp.int32, cos.shape, 1) % HEAD_DIM
    a = jnp.where(m < ROPE_DIM, cos, 1.0)
    b = jnp.where(m < ROPE_DIM // 2, -sin, 0.0)
    c = jnp.where((m >= ROPE_DIM // 2) & (m < ROPE_DIM), sin, 0.0)
    return a, b, c


def _rope_apply(t, cos, sin):
    a, b, c = _rope_coeffs(cos, sin)
    n = t.shape[1]
    return a * t + b * pltpu.roll(t, n - ROPE_DIM // 2, 1) + c * pltpu.roll(t, ROPE_DIM // 2, 1)


def _rope_transpose(dt, cos, sin):
    a, b, c = _rope_coeffs(cos, sin)
    n = dt.shape[1]
    return a * dt + pltpu.roll(b * dt, ROPE_DIM // 2, 1) + pltpu.roll(c * dt, n - ROPE_DIM // 2, 1)


def _in_proj(h, w, cos, sin, *, tm, comm=None):
    t, k = h.shape
    s_n, _, n = w.shape
    assert t % tm == 0 and n % LANES == 0

    def body(h_ref, w_ref, cos_ref, sin_ref, o_ref):
        o_ref[...] = jnp.dot(h_ref[...], w_ref[...], preferred_element_type=F32)
        for s in range(s_n):
            rotary_cols = min(max(2 * ATTN_W - s * n, 0), n)
            if rotary_cols:
                @pl.when(pl.program_id(1) == s)
                def _():
                    for c0 in range(0, rotary_cols, LANES):
                        cols = slice(c0, c0 + LANES)
                        o_ref[:, cols] = _rope_apply(o_ref[:, cols], cos_ref[...], sin_ref[...])

    lane_tile = pl.BlockSpec((tm, LANES), lambda i, s: (i, 0))
    out, got = _call(
        body, [h, w, cos, sin], name="in_proj", grid=(t // tm, s_n),
        in_specs=[pl.BlockSpec((tm, k), lambda i, s: (i, 0)), pl.BlockSpec((None, k, n), lambda i, s: (s, 0, 0)),
                  lane_tile, lane_tile],
        out_specs=[pl.BlockSpec((tm, n), lambda i, s: (i, s))],
        out_shape=[jax.ShapeDtypeStruct((t, s_n * n), F32)], comm=comm)
    return out[0], got


def _mm_nt(a, w, *, tm, tko, name, comm=None):
    t, sn = a.shape
    s_n, ko, n = w.shape
    assert sn == s_n * n and t % tm == 0 and ko % tko == 0

    def body(a_ref, w_ref, o_ref):
        acc = lax.dot_general(a_ref[...], w_ref[...], (((1,), (1,)), ((), ())), preferred_element_type=F32)
        if s_n == 1:
            o_ref[...] = acc
        else:
            _accumulate(o_ref, acc, pl.program_id(2) == 0)

    out, got = _call(
        body, [a, w], name=name, grid=(t // tm, ko // tko, s_n),
        in_specs=[pl.BlockSpec((tm, n), lambda i, j, s: (i, s)),
                  pl.BlockSpec((None, tko, n), lambda i, j, s: (s, j, 0))],
        out_specs=[pl.BlockSpec((tm, tko), lambda i, j, s: (i, j))],
        out_shape=[jax.ShapeDtypeStruct((t, ko), F32)], comm=comm)
    return out[0], got


def _mm_nt_pair(a0, a1, w, *, tm, name, comm=None):
    t = a0.shape[0]
    s_n, ko, n = w.shape
    half = s_n // 2
    assert a0.shape == a1.shape == (t, half * n) and t % tm == 0

    def body(a0_ref, a1_ref, w0_ref, w1_ref, o_ref):
        acc = (lax.dot_general(a0_ref[...], w0_ref[...], LANE_CONTRACT, preferred_element_type=F32)
               + lax.dot_general(a1_ref[...], w1_ref[...], LANE_CONTRACT, preferred_element_type=F32))
        _accumulate(o_ref, acc, pl.program_id(1) == 0)

    a_spec = pl.BlockSpec((tm, n), lambda i, s: (i, s))
    out, got = _call(
        body, [a0, a1, w, w], name=name, grid=(t // tm, half),
        in_specs=[a_spec, a_spec, pl.BlockSpec((None, ko, n), lambda i, s: (s, 0, 0)),
                  pl.BlockSpec((None, ko, n), lambda i, s: (half + s, 0, 0))],
        out_specs=[pl.BlockSpec((tm, ko), lambda i, s: (i, 0))],
        out_shape=[jax.ShapeDtypeStruct((t, ko), F32)], comm=comm)
    return out[0], got


def _mm_tn(a, b, s_n, *, tka, name, into=None, shard0=0):
    t, ka = a.shape
    n = b.shape[1] // s_n
    assert b.shape[0] == t and ka % tka == 0

    def body(a_ref, b_ref, *rest):
        rest[-1][...] = lax.dot_general(a_ref[...], b_ref[...], ROW_CONTRACT, preferred_element_type=F32)

    operands, in_specs, aliases = [a, b], [pl.BlockSpec((t, tka), lambda i, s: (0, i)),
                                           pl.BlockSpec((t, n), lambda i, s: (0, s))], {}
    out_shape = jax.ShapeDtypeStruct((s_n, ka, n), F32)
    if into is not None:
        operands, in_specs, aliases = operands + [into], in_specs + [ANY], {2: 0}
        out_shape = jax.ShapeDtypeStruct(into.shape, F32)
    return pl.pallas_call(
        body, grid=(ka // tka, s_n), in_specs=in_specs,
        out_specs=pl.BlockSpec((None, tka, n), lambda i, s: (shard0 + s, i, 0)),
        out_shape=out_shape, input_output_aliases=aliases, name=name)(*operands)


def _gate_up_swiglu(h, w, *, tm, comm=None):
    t, k = h.shape
    s_n, _, n = w.shape
    half = s_n // 2

    def body(h_ref, wg_ref, wu_ref, g_ref, u_ref, a_ref):
        g = jnp.dot(h_ref[...], wg_ref[...], preferred_element_type=F32)
        u = jnp.dot(h_ref[...], wu_ref[...], preferred_element_type=F32)
        g_ref[...] = g.astype(BF16)
        u_ref[...] = u.astype(BF16)
        a_ref[...] = (g * jax.nn.sigmoid(g) * u).astype(BF16)

    col = pl.BlockSpec((tm, n), lambda i, j: (i, j))
    out, got = _call(
        body, [h, w, w], name="gate_up", grid=(t // tm, half),
        in_specs=[pl.BlockSpec((tm, k), lambda i, j: (i, 0)), pl.BlockSpec((None, k, n), lambda i, j: (j, 0, 0)),
                  pl.BlockSpec((None, k, n), lambda i, j: (half + j, 0, 0))],
        out_specs=[col, col, col],
        out_shape=[jax.ShapeDtypeStruct((t, half * n), BF16)] * 3, comm=comm)
    return out[0], out[1], out[2], got


def _down_dx_swiglu_bwd(df, w, g, u, *, tm, tko):
    t, k = df.shape
    _, ko, _ = w.shape
    assert t % tm == 0 and ko % tko == 0

    def body(df_ref, w_ref, g_ref, u_ref, dg_ref, du_ref):
        d = lax.dot_general(df_ref[...], w_ref[...], LANE_CONTRACT, preferred_element_type=F32)
        gg = g_ref[...].astype(F32)
        sig = jax.nn.sigmoid(gg)
        dg_ref[...] = (d * u_ref[...].astype(F32) * (sig * (1.0 + gg * (1.0 - sig)))).astype(BF16)
        du_ref[...] = (d * (gg * sig)).astype(BF16)

    col = pl.BlockSpec((tm, tko), lambda i, j: (i, j))
    return pl.pallas_call(
        body, grid=(t // tm, ko // tko),
        in_specs=[pl.BlockSpec((tm, k), lambda i, j: (i, 0)), pl.BlockSpec((None, tko, k), lambda i, j: (0, j, 0)),
                  col, col],
        out_specs=[col, col], out_shape=[jax.ShapeDtypeStruct((t, ko), BF16)] * 2, name="down_dx")(df, w, g, u)


CHUNK = 256


def _in_proj_dx(d_attn3, d_conv3, w, *, tm, comm=None):
    _, t, _ = d_attn3.shape
    s_n, ko, n = w.shape
    half, per = s_n // 2, n // CHUNK
    assert t % tm == 0

    def body(*refs):
        a_refs, b_refs, wa_ref, wb_ref, o_ref = refs[:per], refs[per:2 * per], refs[2 * per], refs[2 * per + 1], refs[-1]
        acc = jnp.zeros(o_ref.shape, F32)
        for r in range(per):
            cols = slice(r * CHUNK, (r + 1) * CHUNK)
            acc = acc + lax.dot_general(a_refs[r][...], wa_ref[:, cols], LANE_CONTRACT, preferred_element_type=F32)
            acc = acc + lax.dot_general(b_refs[r][...], wb_ref[:, cols], LANE_CONTRACT, preferred_element_type=F32)
        _accumulate(o_ref, acc, pl.program_id(1) == 0)

    piece = lambda r: pl.BlockSpec((None, tm, CHUNK), lambda i, s: ((per * s + r) // 2, i, (per * s + r) % 2))
    out, got = _call(
        body, [d_attn3] * per + [d_conv3] * per + [w, w], name="in_proj_dx", grid=(t // tm, half),
        in_specs=[piece(r) for r in range(per)] * 2
        + [pl.BlockSpec((None, ko, n), lambda i, s: (s, 0, 0)), pl.BlockSpec((None, ko, n), lambda i, s: (half + s, 0, 0))],
        out_specs=[pl.BlockSpec((tm, ko), lambda i, s: (i, 0))],
        out_shape=[jax.ShapeDtypeStruct((t, ko), F32)], comm=comm)
    return out[0], got


def _in_proj_dw(h, d_attn3, d_conv3, s_n, *, tka):
    t, ka = h.shape
    half = s_n // 2
    n = 3 * d_attn3.shape[2] // half
    per = n // CHUNK
    assert ka % tka == 0

    def body(*refs):
        h_ref, o_ref = refs[0], refs[-1]
        for side in range(2):
            for r in range(per):
                o_ref[side, :, r * CHUNK:(r + 1) * CHUNK] = lax.dot_general(
                    h_ref[...], refs[1 + side * per + r][...], ROW_CONTRACT, preferred_element_type=F32)

    piece = lambda r: pl.BlockSpec((None, t, CHUNK), lambda i, s: ((per * s + r) // 2, 0, (per * s + r) % 2))
    out = pl.pallas_call(
        body, grid=(ka // tka, half),
        in_specs=[pl.BlockSpec((t, tka), lambda i, s: (0, i))] + [piece(r) for r in range(per)] * 2,
        out_specs=pl.BlockSpec((2, None, tka, n), lambda i, s: (0, s, i, 0)),
        out_shape=jax.ShapeDtypeStruct((2, half, ka, n), F32), name="in_proj_dw")(h, *[d_attn3] * per, *[d_conv3] * per)
    return out.reshape(s_n, ka, n)


def _rope_tables(positions_col, inv_freq_row):
    t = positions_col.shape[0]

    def body(pos_ref, f_ref, cos_ref, sin_ref):
        ang = pos_ref[...].astype(F32) * f_ref[...]
        cos_ref[...] = jnp.cos(ang)
        sin_ref[...] = jnp.sin(ang)

    return pl.pallas_call(
        body, out_shape=[jax.ShapeDtypeStruct((t, LANES), F32)] * 2, name="rope_tables")(positions_col, inv_freq_row)


def _norm_fwd(x, g3, l, *, tm):
    t, w = x.shape

    def body(x_ref, g_ref, h_ref):
        h_ref[...] = _rms_fwd(x_ref[...], g_ref[...]).astype(BF16)

    return pl.pallas_call(
        body, grid=(t // tm,),
        in_specs=[pl.BlockSpec((tm, w), lambda i: (i, 0)), _gain_spec(g3, l)],
        out_specs=pl.BlockSpec((tm, w), lambda i: (i, 0)),
        out_shape=jax.ShapeDtypeStruct((t, w), BF16), name="norm_fwd")(x, g3)


def _mm_resnorm(a, w, x, g_post3, l_post, g_next3, l_next, *, tm, name, comm=None):
    t, k = a.shape
    _, _, n = w.shape
    with_next = g_next3 is not None
    row = pl.BlockSpec((tm, n), lambda i: (i, 0))

    def body(a_ref, w_ref, x_ref, gp_ref, *rest):
        y = jnp.dot(a_ref[...], w_ref[...], preferred_element_type=F32)
        x_new = x_ref[...] + _rms_fwd(y, gp_ref[...])
        if with_next:
            gn_ref, y_ref, xo_ref, h_ref = rest
            h_ref[...] = _rms_fwd(x_new, gn_ref[...]).astype(BF16)
        else:
            y_ref, xo_ref = rest
        y_ref[...] = y
        xo_ref[...] = x_new

    ins = [a, w, x, g_post3] + ([g_next3] if with_next else [])
    in_specs = ([pl.BlockSpec((tm, k), lambda i: (i, 0)), _const_spec((None, k, n), (0, 0, 0)), row,
                 _gain_spec(g_post3, l_post)] + ([_gain_spec(g_next3, l_next)] if with_next else []))
    out_shape = [jax.ShapeDtypeStruct((t, n), F32)] * 2 + ([jax.ShapeDtypeStruct((t, n), BF16)] if with_next else [])
    out, got = _call(body, ins, name=name, grid=(t // tm,), in_specs=in_specs, out_specs=[row] * len(out_shape),
                     out_shape=out_shape, comm=comm)
    return out[0], out[1], (out[2] if with_next else None), got


def _conv_fwd(proj, conv_w, l):
    t = proj.shape[0]
    col0 = 3 * ATTN_W // LANES

    def body(u_ref, gb_ref, gc_ref, w_ref, y_ref):
        c = gc_ref[...] * u_ref[...]
        row = lax.broadcasted_iota(jnp.int32, c.shape, 0)
        c_prev = jnp.where(row == 0, 0.0, pltpu.roll(c, 1, 0))
        c_next = jnp.where(row == t - 1, 0.0, pltpu.roll(c, t - 1, 0))
        w = w_ref[...]
        y_ref[...] = gb_ref[...] * (w[0:1] * c_prev + w[1:2] * c + w[2:3] * c_next)

    nj = CONV_W // LANES
    cols = lambda base: pl.BlockSpec((t, LANES), lambda j: (0, base + j))
    return pl.pallas_call(
        body, grid=(nj,),
        in_specs=[cols(col0), cols(col0 + nj), cols(col0 + 2 * nj),
                  pl.BlockSpec((None, None, 3, LANES), lambda j: (l, j, 0, 0))],
        out_specs=pl.BlockSpec((t, LANES), lambda j: (0, j)),
        out_shape=jax.ShapeDtypeStruct((t, CONV_W), F32), name="conv_fwd")(proj, proj, proj, conv_w)


def _merge_fwd(attn, conv_y, ga3, gc3, l, *, tm):
    t = attn.shape[0]
    row = pl.BlockSpec((tm, ATTN_W), lambda i: (i, 0))

    def body(a_ref, c_ref, ga_ref, gc_ref, m_ref):
        m_ref[:, :ATTN_W] = _rms_fwd(a_ref[...], ga_ref[...]).astype(BF16)
        m_ref[:, ATTN_W:] = _rms_fwd(c_ref[...], gc_ref[...]).astype(BF16)

    return pl.pallas_call(
        body, grid=(t // tm,),
        in_specs=[row, row, _gain_spec(ga3, l), _gain_spec(gc3, l)],
        out_specs=pl.BlockSpec((tm, D_MODEL), lambda i: (i, 0)),
        out_shape=jax.ShapeDtypeStruct((t, D_MODEL), BF16), name="merge_fwd")(attn, conv_y, ga3, gc3)


def _loss_fwd_bwd(y, target, *, tm):
    t, w = y.shape
    row = pl.BlockSpec((tm, w), lambda i: (i, 0))

    def body(y_ref, t_ref, dy_ref, loss_ref):
        e = y_ref[...] - t_ref[...]
        dy_ref[...] = e * (1.0 / w)
        sq = jnp.sum(e * e, axis=0, keepdims=True) * (0.5 / w)
        part = sq[:, :LANES]
        for j in range(1, w // LANES):
            part = part + sq[:, j * LANES:(j + 1) * LANES]
        _accumulate(loss_ref, part, pl.program_id(0) == 0)

    return pl.pallas_call(
        body, grid=(t // tm,), in_specs=[row, row],
        out_specs=[row, _const_spec((1, LANES), (0, 0))],
        out_shape=[jax.ShapeDtypeStruct((t, w), F32), jax.ShapeDtypeStruct((1, LANES), F32)], name="loss")(y, target)


def _tile_rows(t, nt, lb, d):
    r = t // nt
    q0 = (t % nt) * TQ
    m0 = jnp.clip(q0 - BAND, 0, lb - WIN)
    if d == 1:
        return pl.ds(pl.multiple_of(q0, TQ), TQ), pl.ds(pl.multiple_of(m0, BAND), WIN), m0 - q0
    return pl.ds(r + d * q0, TQ, stride=d), pl.ds(r + d * m0, WIN, stride=d), m0 - q0


def _for_row_chunks(t, fn, chunk=512):
    def step(i, carry):
        fn(pl.ds(pl.multiple_of(i * chunk, chunk), chunk))
        return carry

    lax.fori_loop(0, t // chunk, step, 0)


WINDOW_OFFSETS = (-BAND, 0, -2 * BAND)


def _fill_band_bias(bias_ref):
    rel0 = (lax.broadcasted_iota(jnp.int32, (2 * TQ, WIN), 1)
            - lax.broadcasted_iota(jnp.int32, (2 * TQ, WIN), 0) % TQ)
    for j, off in enumerate(WINDOW_OFFSETS):
        rel = rel0 + off
        bias_ref[j] = jnp.where((rel >= -BAND) & (rel <= BAND), 0.0, NEG_INF)


def _fill_sequence_bias(bias_ref):
    rel = (lax.broadcasted_iota(jnp.int32, (2 * WIN, WIN), 1) - lax.broadcasted_iota(jnp.int32, (2 * WIN, WIN), 0) % WIN)
    bias_ref[...] = jnp.where((rel >= -BAND) & (rel <= BAND), 0.0, NEG_INF)


def _band_bias(bias_ref, off):
    return bias_ref[jnp.where(off == WINDOW_OFFSETS[0], 0, jnp.where(off == WINDOW_OFFSETS[1], 1, 2))]


def _stack_heads(a, first_head):
    return jnp.concatenate([jnp.where(first_head, a, 0.0), jnp.where(first_head, 0.0, a)], axis=0)


def _unstack_heads(a2, first_head):
    n = a2.shape[0] // 2
    return jnp.where(first_head, a2[:n], a2[n:])


def _attn_fwd(proj, comm=None):
    t = proj.shape[0]
    npair = ATTN_W // LANES

    def body(q_ref, k_ref, v_ref, o_ref, lse_ref, o1, o2, l0, l1, l2, m1, m2, bias, bias_seq):
        _fill_band_bias(bias)
        _fill_sequence_bias(bias_seq)
        outs, dens, maxs = (o_ref, o1, o2), (l0, l1, l2), (lse_ref, m1, m2)

        def softmax_tile(b, qrows, krows, n_q, band_bias):
            first_head = lax.broadcasted_iota(jnp.int32, (n_q, LANES), 1) < HEAD_DIM
            q2 = _stack_heads(q_ref[qrows, :] * SCALE, first_head).astype(BF16)
            kw = k_ref[krows, :].astype(BF16)
            vw = jnp.concatenate([v_ref[krows, :].astype(BF16), jnp.ones((WIN, LANES), BF16)], axis=1)
            s = lax.dot_general(q2, kw, LANE_CONTRACT, preferred_element_type=F32) + band_bias
            m = jnp.max(s, axis=-1, keepdims=True)
            pv = jnp.dot(jnp.exp(s - m).astype(BF16), vw, preferred_element_type=F32)
            outs[b][qrows, :] = _unstack_heads(pv[:, :LANES], first_head)
            dens[b][qrows, :] = _unstack_heads(pv[:, LANES:], first_head)
            maxs[b][qrows, :] = _unstack_heads(jnp.broadcast_to(m, (2 * n_q, LANES)), first_head)

        for b, d in enumerate(DILATIONS):
            lb = t // d
            if lb == WIN:
                def sequence(r, carry, b=b, d=d):
                    rows = pl.ds(r, WIN, stride=d)
                    softmax_tile(b, rows, rows, WIN, bias_seq[...])
                    return carry

                lax.fori_loop(0, d, sequence, 0, unroll=4)
                continue
            nt = lb // TQ

            def tile(ti, carry, b=b, d=d, lb=lb, nt=nt):
                qrows, krows, off = _tile_rows(ti, nt, lb, d)
                softmax_tile(b, qrows, krows, TQ, _band_bias(bias, off))
                return carry

            lax.fori_loop(0, d * nt, tile, 0, unroll=8)

        def finish(rows):
            ms = [m_b[rows, :] for m_b in maxs]
            m_all = jnp.maximum(jnp.maximum(ms[0], ms[1]), ms[2])
            ws = [jnp.exp(m_b - m_all) for m_b in ms]
            den = ws[0] * dens[0][rows, :] + ws[1] * dens[1][rows, :] + ws[2] * dens[2][rows, :]
            num = ws[0] * outs[0][rows, :] + ws[1] * outs[1][rows, :] + ws[2] * outs[2][rows, :]
            o_ref[rows, :] = num / den
            lse_ref[rows, :] = m_all + jnp.log(den)

        _for_row_chunks(t, finish, 256)

    cols = lambda base: pl.BlockSpec((t, LANES), lambda g: (0, base + g))
    out, got = _call(
        body, [proj, proj, proj], name="attn_fwd", grid=(npair,),
        in_specs=[cols(0), cols(npair), cols(2 * npair)],
        out_specs=[cols(0), cols(0)],
        out_shape=[jax.ShapeDtypeStruct((t, ATTN_W), F32)] * 2,
        scratch_shapes=[pltpu.VMEM((t, LANES), F32)] * 7 + [pltpu.VMEM((len(WINDOW_OFFSETS), 2 * TQ, WIN), F32),
                                                            pltpu.VMEM((2 * WIN, WIN), F32)],
        comm=comm)
    return out[0], out[1], got


def _attn_bwd(proj, cos, sin, d_attn, lse, delta, comm=None):
    t = proj.shape[0]
    npair = ATTN_W // LANES

    def body(q_ref, k_ref, v_ref, cos_ref, sin_ref, do_ref, l_ref, dl_ref, dqkv_ref,
             dq_acc, dk_acc, dv_acc, bias, bias_seq):
        _fill_band_bias(bias)
        _fill_sequence_bias(bias_seq)
        dq_acc[...] = jnp.zeros(dq_acc.shape, F32)
        dk_acc[...] = jnp.zeros(dk_acc.shape, F32)
        dv_acc[...] = jnp.zeros(dv_acc.shape, F32)
        def stack_column(a):
            return jnp.concatenate([a[:, 0:1], a[:, HEAD_DIM:HEAD_DIM + 1]], axis=0)

        def grad_tile(qrows, krows, n_q, band_bias):
            first_head = lax.broadcasted_iota(jnp.int32, (n_q, LANES), 1) < HEAD_DIM
            q2 = _stack_heads(q_ref[qrows, :] * SCALE, first_head).astype(BF16)
            do2 = _stack_heads(do_ref[qrows, :], first_head).astype(BF16)
            kw = k_ref[krows, :].astype(BF16)
            vw = v_ref[krows, :].astype(BF16)
            s = lax.dot_general(q2, kw, LANE_CONTRACT, preferred_element_type=F32) + band_bias
            p = jnp.exp(s - stack_column(l_ref[qrows, :]))
            dp = lax.dot_general(do2, vw, LANE_CONTRACT, preferred_element_type=F32)
            ds = (p * (dp - stack_column(dl_ref[qrows, :]))).astype(BF16)
            dq2 = jnp.dot(ds, kw, preferred_element_type=F32)
            dq_acc[qrows, :] += _unstack_heads(dq2, first_head) * SCALE
            dk_acc[krows, :] += lax.dot_general(ds, q2, ROW_CONTRACT, preferred_element_type=F32)
            dv_acc[krows, :] += lax.dot_general(p.astype(BF16), do2, ROW_CONTRACT, preferred_element_type=F32)

        for d in DILATIONS:
            lb = t // d
            if lb == WIN:
                def sequence(r, carry, d=d):
                    rows = pl.ds(r, WIN, stride=d)
                    grad_tile(rows, rows, WIN, bias_seq[...])
                    return carry

                lax.fori_loop(0, d, sequence, 0, unroll=2)
                continue
            nt = lb // TQ

            def tile(ti, carry, d=d, lb=lb, nt=nt):
                qrows, krows, off = _tile_rows(ti, nt, lb, d)
                grad_tile(qrows, krows, TQ, _band_bias(bias, off))
                return carry

            lax.fori_loop(0, d * nt, tile, 0, unroll=4)

        def finish(rows):
            dqkv_ref[0, rows, :] = _rope_transpose(dq_acc[rows, :], cos_ref[rows, :], sin_ref[rows, :]).astype(BF16)
            dqkv_ref[1, rows, :] = _rope_transpose(dk_acc[rows, :], cos_ref[rows, :], sin_ref[rows, :]).astype(BF16)
            dqkv_ref[2, rows, :] = dv_acc[rows, :].astype(BF16)

        _for_row_chunks(t, finish)

    cols = lambda base: pl.BlockSpec((t, LANES), lambda g: (0, base + g))
    out, got = _call(
        body, [proj, proj, proj, cos, sin, d_attn, lse, delta], name="attn_bwd", grid=(npair,),
        in_specs=[cols(0), cols(npair), cols(2 * npair), WHOLE_VMEM, WHOLE_VMEM, cols(0), cols(0), cols(0)],
        out_specs=[pl.BlockSpec((3, t, LANES), lambda g: (0, 0, g))],
        out_shape=[jax.ShapeDtypeStruct((3, t, ATTN_W), BF16)],
        scratch_shapes=[pltpu.VMEM((t, LANES), F32)] * 3 + [pltpu.VMEM((len(WINDOW_OFFSETS), 2 * TQ, WIN), F32),
                                                            pltpu.VMEM((2 * WIN, WIN), F32)],
        comm=comm)
    return out[0], got


def _norm_bwd(dres, pre, post, *, tm, comm=None):
    t, w = dres.shape
    row = pl.BlockSpec((tm, w), lambda i: (i, 0))
    gsum = _const_spec((1, w), (0, 0))
    ins, in_specs, out_shape, out_specs = [dres], [row], [], []
    if pre is not None:
        dh, x, g3, l = pre
        ins += [dh, x, g3]
        in_specs += [row, row, _gain_spec(g3, l)]
        out_shape += [jax.ShapeDtypeStruct((t, w), F32), jax.ShapeDtypeStruct((1, w), F32)]
        out_specs += [row, gsum]
    if post is not None:
        y, g3, l = post
        ins += [y, g3]
        in_specs += [row, _gain_spec(g3, l)]
        out_shape += [jax.ShapeDtypeStruct((t, w), BF16), jax.ShapeDtypeStruct((1, w), F32)]
        out_specs += [row, gsum]
    n_in = len(ins)

    def body(*refs):
        first = pl.program_id(0) == 0
        ins_r, outs_r = list(refs[:n_in]), list(refs[n_in:])
        d = ins_r.pop(0)[...]
        if pre is not None:
            dh_ref, x_ref, g_ref = ins_r[:3]
            ins_r = ins_r[3:]
            dx, dg = _rms_bwd(x_ref[...], g_ref[...], dh_ref[...])
            d = d + dx
            outs_r.pop(0)[...] = d
            _accumulate(outs_r.pop(0), dg, first)
        if post is not None:
            y_ref, g_ref = ins_r
            dy, dg = _rms_bwd(y_ref[...], g_ref[...], d)
            outs_r.pop(0)[...] = dy.astype(BF16)
            _accumulate(outs_r.pop(0), dg, first)

    out, got = _call(body, ins, name="norm_bwd", grid=(t // tm,), in_specs=in_specs, out_specs=out_specs,
                     out_shape=out_shape, comm=comm)
    d_new, dg_pre = (out.pop(0), out.pop(0)) if pre is not None else (None, None)
    dy, dg_post = (out.pop(0), out.pop(0)) if post is not None else (None, None)
    return d_new, dy, dg_pre, dg_post, got


def _merge_bwd(d_merged, attn, conv_y, ga3, gc3, l, *, tm):
    t = attn.shape[0]
    row = pl.BlockSpec((tm, ATTN_W), lambda i: (i, 0))
    gsum = _const_spec((1, ATTN_W), (0, 0))

    def body(dma_ref, dmc_ref, a_ref, c_ref, ga_ref, gc_ref, da_ref, dl_ref, dc_ref, dga_ref, dgc_ref):
        first = pl.program_id(0) == 0
        attn_t = a_ref[...]
        da, dga = _rms_bwd(attn_t, ga_ref[...], dma_ref[...])
        dc, dgc = _rms_bwd(c_ref[...], gc_ref[...], dmc_ref[...])
        da_ref[...] = da
        dc_ref[...] = dc
        same_head = (lax.broadcasted_iota(jnp.int32, (ATTN_W, ATTN_W), 0) // HEAD_DIM
                     == lax.broadcasted_iota(jnp.int32, (ATTN_W, ATTN_W), 1) // HEAD_DIM).astype(BF16)
        rest = da * attn_t
        total = jnp.zeros(rest.shape, F32)
        for _ in range(3):
            term = rest.astype(BF16)
            total = total + jnp.dot(term, same_head, preferred_element_type=F32)
            rest = rest - term.astype(F32)
        dl_ref[...] = total
        _accumulate(dga_ref, dga, first)
        _accumulate(dgc_ref, dgc, first)

    return pl.pallas_call(
        body, grid=(t // tm,),
        in_specs=[pl.BlockSpec((tm, ATTN_W), lambda i: (i, 0)), pl.BlockSpec((tm, CONV_W), lambda i: (i, 1)),
                  row, row, _gain_spec(ga3, l), _gain_spec(gc3, l)],
        out_specs=[row, row, row, gsum, gsum],
        out_shape=[jax.ShapeDtypeStruct((t, ATTN_W), F32)] * 3 + [jax.ShapeDtypeStruct((1, ATTN_W), F32)] * 2,
        name="merge_bwd")(d_merged, d_merged, attn, conv_y, ga3, gc3)


def _conv_bwd(proj, conv_w, l, d_conv_y):
    t = proj.shape[0]
    col0 = 3 * ATTN_W // LANES
    nj = CONV_W // LANES

    def body(u_ref, gb_ref, gc_ref, w_ref, dy_ref, d3_ref, dw_ref):
        u, gc, dy = u_ref[...], gc_ref[...], dy_ref[...]
        row = lax.broadcasted_iota(jnp.int32, u.shape, 0)
        down = lambda a: jnp.where(row == 0, 0.0, pltpu.roll(a, 1, 0))
        up = lambda a: jnp.where(row == t - 1, 0.0, pltpu.roll(a, t - 1, 0))
        w = w_ref[...]
        c = gc * u
        c_prev, c_next = down(c), up(c)
        d3_ref[1] = (dy * (w[0:1] * c_prev + w[1:2] * c + w[2:3] * c_next)).astype(BF16)
        dz = dy * gb_ref[...]
        dc = w[0:1] * up(dz) + w[1:2] * dz + w[2:3] * down(dz)
        d3_ref[0] = (dc * gc).astype(BF16)
        d3_ref[2] = (dc * u).astype(BF16)
        dw_ref[0:1, :] = jnp.sum(dz * c_prev, axis=0, keepdims=True)
        dw_ref[1:2, :] = jnp.sum(dz * c, axis=0, keepdims=True)
        dw_ref[2:3, :] = jnp.sum(dz * c_next, axis=0, keepdims=True)

    cols = lambda base: pl.BlockSpec((t, LANES), lambda j: (0, base + j))
    return pl.pallas_call(
        body, grid=(nj,),
        in_specs=[cols(col0), cols(col0 + nj), cols(col0 + 2 * nj),
                  pl.BlockSpec((None, None, 3, LANES), lambda j: (l, j, 0, 0)), cols(0)],
        out_specs=[pl.BlockSpec((3, t, LANES), lambda j: (0, 0, j)), pl.BlockSpec((None, 3, LANES), lambda j: (j, 0, 0))],
        out_shape=[jax.ShapeDtypeStruct((3, t, CONV_W), BF16), jax.ShapeDtypeStruct((nj, 3, LANES), F32)],
        name="conv_bwd")(proj, proj, proj, conv_w, d_conv_y)


def _own_shard_slab(w, l, place, dtype):
    _, rows, cols = w.shape
    tr = rows if rows <= 704 else 512
    assert rows % tr == 0

    def body(p_ref, w_ref, o_ref):
        del p_ref
        o_ref[...] = w_ref[...].astype(dtype)

    grid_spec = pltpu.PrefetchScalarGridSpec(
        num_scalar_prefetch=1, grid=(rows // tr,),
        in_specs=[pl.BlockSpec((None, tr, cols), lambda i, p: (l, i, 0))],
        out_specs=pl.BlockSpec((None, tr, cols), lambda i, p: (p[0], i, 0)))
    return pl.pallas_call(body, grid_spec=grid_spec, name="own_shard_slab",
                          out_shape=jax.ShapeDtypeStruct((N_CHIPS, rows, cols), dtype))(place, w)


def _own_conv_slab(w, place):
    depth = w.shape[0]

    def body(p_ref, w_ref, o_ref):
        del p_ref
        o_ref[...] = w_ref[...]

    grid_spec = pltpu.PrefetchScalarGridSpec(
        num_scalar_prefetch=1, grid=(depth,),
        in_specs=[pl.BlockSpec((None, 3, LANES), lambda l, p: (l, 0, 0))],
        out_specs=pl.BlockSpec((None, None, 3, LANES), lambda l, p: (l, p[0], 0, 0)))
    return pl.pallas_call(body, grid_spec=grid_spec, name="own_conv_slab",
                          out_shape=jax.ShapeDtypeStruct((depth, N_CHIPS, 3, LANES), F32))(place, w)


def _add_halves(grad, got, place):
    s_n, rows, cols = grad.shape
    hr = rows // 2

    def body(p_ref, g_ref, r_ref, o_ref):
        del p_ref
        o_ref[...] = (g_ref[...] + r_ref[...]).astype(BF16)

    grid_spec = pltpu.PrefetchScalarGridSpec(
        num_scalar_prefetch=1, grid=(s_n,),
        in_specs=[pl.BlockSpec((None, hr, cols), lambda s, p: (s, p[1], 0)),
                  pl.BlockSpec((None, hr, cols), lambda s, p: (s, 0, 0))],
        out_specs=pl.BlockSpec((None, hr, cols), lambda s, p: (s, 0, 0)))
    return pl.pallas_call(body, grid_spec=grid_spec, out_shape=jax.ShapeDtypeStruct((s_n, hr, cols), BF16),
                          name="add_halves")(place, grad, got)


def _sum_partials(partial, got, place, acc, l):
    _, hr, cols = partial.shape

    def body(p_ref, mine_ref, got_ref, acc_ref, o_ref):
        del p_ref, acc_ref
        total = mine_ref[...].astype(F32)
        for k in range(3):
            total = total + got_ref[k].astype(F32)
        o_ref[...] = total

    grid_spec = pltpu.PrefetchScalarGridSpec(
        num_scalar_prefetch=1, grid=(1,),
        in_specs=[pl.BlockSpec((None, hr, cols), lambda i, p: (p[0], 0, 0)),
                  pl.BlockSpec((3, hr, cols), lambda i, p: (0, 0, 0)), ANY],
        out_specs=pl.BlockSpec((None, hr, cols), lambda i, p: (l, p[1], 0)))
    return pl.pallas_call(body, grid_spec=grid_spec, out_shape=jax.ShapeDtypeStruct(acc.shape, F32),
                          input_output_aliases={3: 0}, name="sum_partials")(place, partial, got, acc)


def _allreduce_small(vec, loss_row):
    rows = vec.shape[0]

    def body(v_ref, o_ref, slots, send_sems, recv_sems):
        x, y, c, _ = _place()
        me = 4 * x + 2 * y + c
        slots[me] = v_ref[...]
        copies = []
        for k in range(1, N_DEV):
            flip = lambda v, bit: 1 - v if bit else v
            peer = (flip(x, k & 4), flip(y, k & 2), flip(c, k & 1))
            copies.append(_remote(v_ref, slots.at[me], send_sems.at[k - 1], recv_sems.at[k - 1], peer))
        for cp in copies:
            cp.start()
        for k in range(1, N_DEV):
            flip = lambda v, bit: 1 - v if bit else v
            peer_id = 4 * flip(x, k & 4) + 2 * flip(y, k & 2) + flip(c, k & 1)
            _remote(v_ref, slots.at[peer_id], send_sems.at[k - 1], recv_sems.at[k - 1], (x, y, c)).wait_recv()
        for cp in copies:
            cp.wait_send()
        total = slots[0]
        for dev in range(1, N_DEV):
            total = total + slots[dev]
        o_ref[...] = total
        o_ref[loss_row:loss_row + 1, :] = jnp.broadcast_to(
            jnp.sum(total[loss_row:loss_row + 1, :], axis=-1, keepdims=True), (1, LANES))

    return pl.pallas_call(
        body, in_specs=[WHOLE_VMEM], out_specs=WHOLE_VMEM, out_shape=jax.ShapeDtypeStruct((rows, LANES), F32),
        scratch_shapes=[pltpu.VMEM((N_DEV, rows, LANES), F32), pltpu.SemaphoreType.DMA((N_DEV - 1,)),
                        pltpu.SemaphoreType.DMA((N_DEV - 1,))],
        name="allreduce_small")(vec)


def _adamw(w, g, m, v, *, tr, emit_grad=False):
    depth, rows, cols = w.shape
    assert rows % tr == 0
    c1 = float(np.float32(1.0 - ADAM_B1 ** ADAM_STEP))
    c2 = float(np.float32(1.0 - ADAM_B2 ** ADAM_STEP))

    def body(w_ref, g_ref, m_ref, v_ref, d_ref, mo_ref, vo_ref, *go_ref):
        g_t = g_ref[...]
        if emit_grad:
            go_ref[0][...] = g_t
        m_new = ADAM_B1 * m_ref[...] + (1.0 - ADAM_B1) * g_t
        v_new = ADAM_B2 * v_ref[...] + (1.0 - ADAM_B2) * (g_t * g_t)
        mo_ref[...] = m_new
        vo_ref[...] = v_new
        d_ref[...] = -ADAM_LR * ((m_new / c1) / (jnp.sqrt(v_new / c2) + ADAM_EPS) + ADAM_WD * w_ref[...])

    blk = pl.BlockSpec((None, tr, cols), lambda l, i: (l, i, 0))
    return pl.pallas_call(
        body, grid=(depth, rows // tr), in_specs=[blk] * 4, out_specs=[blk] * (4 if emit_grad else 3),
        out_shape=[jax.ShapeDtypeStruct(w.shape, F32)] * (4 if emit_grad else 3), name="adamw")(w, g, m, v)


def _local_step(x, positions, target, gains, exchange):
    t = x.shape[0]
    tm = 512
    inv_freq = ROPE_THETA ** (-jnp.arange(0, ROPE_DIM, 2, dtype=F32) / ROPE_DIM)
    lane = np.arange(LANES) % HEAD_DIM
    freq_row = jnp.where(lane < ROPE_DIM, inv_freq[lane % (ROPE_DIM // 2)], 0.0).astype(F32)[None, :]
    cos, sin = _rope_tables(positions.reshape(t, 1), freq_row)

    def hosted(tag, fn, *args, **kwargs):
        *out, got = fn(*args, comm=exchange.host(tag), **kwargs)
        if got is not None:
            exchange.hosted(tag, got)
        return out[0] if len(out) == 1 else out

    saved = []
    h1 = _norm_fwd(x, gains["pre_mix_norm"], 0, tm=tm)
    for l in range(DEPTH):
        proj = hosted(("fwd", l, "in_proj"), _in_proj, h1, exchange.weight("w_in", l), cos, sin, tm=1024)
        attn, lse = hosted(("fwd", l, "attn"), _attn_fwd, proj)
        conv_y = _conv_fwd(proj, exchange.weight("conv_w", l), l)
        merged = _merge_fwd(attn, conv_y, gains["attn_out_norm"], gains["conv_out_norm"], l, tm=tm)
        mix, x1, h2 = hosted(("fwd", l, "out_proj"), _mm_resnorm, merged, exchange.weight("w_out", l), x,
                             gains["post_mix_norm"], l, gains["pre_ffn_norm"], l, tm=tm, name="out_proj")
        g, u, act = hosted(("fwd", l, "gate_up"), _gate_up_swiglu, h2, exchange.weight("w_gate_up", l), tm=1024)
        nxt = (gains["pre_mix_norm"], l + 1) if l + 1 < DEPTH else (None, None)
        f, x2, h1_next = hosted(("fwd", l, "down"), _mm_resnorm, act, exchange.weight("w_down", l), x1,
                                gains["post_ffn_norm"], l, *nxt, tm=tm, name="down")
        saved.append(dict(x=x, h1=h1, proj=proj, attn=attn, lse=lse, conv_y=conv_y, merged=merged, mix=mix,
                          x1=x1, h2=h2, g=g, u=u, act=act, f=f))
        x, h1 = x2, h1_next

    dres, loss_lanes = _loss_fwd_bwd(x, target, tm=tm)

    g_gain = {k: [None] * DEPTH for k in gains}
    g_conv = [None] * DEPTH
    _, df, _, g_gain["post_ffn_norm"][DEPTH - 1], _ = _norm_bwd(
        dres, None, (saved[-1]["f"], gains["post_ffn_norm"], DEPTH - 1), tm=tm)
    for l in reversed(range(DEPTH)):
        sv = saved[l]
        w = {k: exchange.weight(k, l) for k in MATRIX_NAMES + ("conv_w",)}
        dg, du = _down_dx_swiglu_bwd(df, w["w_down"], sv["g"], sv["u"], tm=1024, tko=FFN // 2)
        g_down = _mm_tn(sv["act"], df, 1, tka=256, name="down_dw")
        dh2 = hosted(("bwd", l, "gate_up_dx"), _mm_nt_pair, dg, du, w["w_gate_up"], tm=1024, name="gate_up_dx")
        g_gate_up = _mm_tn(sv["h2"], dg, N_CHIPS // 2, tka=512, name="gate_up_dw",
                           into=lax.empty(w["w_gate_up"].shape, F32))
        g_gate_up = _mm_tn(sv["h2"], du, N_CHIPS // 2, tka=512, name="gate_up_dw", into=g_gate_up,
                           shard0=N_CHIPS // 2)
        exchange.grads(l, "ffn", dict(w_down=g_down.reshape(N_CHIPS, FFN // N_CHIPS, D_MODEL), w_gate_up=g_gate_up))
        dx1, dmix, g_gain["pre_ffn_norm"][l], g_gain["post_mix_norm"][l] = hosted(
            ("bwd", l, "norm_mid"), _norm_bwd,
            dres, (dh2, sv["x1"], gains["pre_ffn_norm"], l), (sv["mix"], gains["post_mix_norm"], l), tm=tm)
        d_merged = hosted(("bwd", l, "out_proj_dx"), _mm_nt, dmix, w["w_out"], tm=1024, tko=D_MODEL, name="out_proj_dx")
        g_out = _mm_tn(sv["merged"], dmix, 1, tka=512, name="out_proj_dw")
        d_attn, delta, d_conv_y, g_gain["attn_out_norm"][l], g_gain["conv_out_norm"][l] = _merge_bwd(
            d_merged, sv["attn"], sv["conv_y"], gains["attn_out_norm"], gains["conv_out_norm"], l, tm=tm)
        d_attn3 = hosted(("bwd", l, "attn"), _attn_bwd, sv["proj"], cos, sin, d_attn, sv["lse"], delta)
        d_conv3, g_conv[l] = _conv_bwd(sv["proj"], w["conv_w"], l, d_conv_y)
        g_in = _in_proj_dw(sv["h1"], d_attn3, d_conv3, N_CHIPS, tka=512)
        exchange.grads(l, "mix", dict(w_out=g_out.reshape(N_CHIPS, D_MODEL // N_CHIPS, D_MODEL), w_in=g_in))
        dh1 = hosted(("bwd", l, "in_proj_dx"), _in_proj_dx, d_attn3, d_conv3, w["w_in"], tm=1024)
        below = (saved[l - 1]["f"], gains["post_ffn_norm"], l - 1) if l > 0 else None
        dres, df, g_gain["pre_mix_norm"][l], g_below = hosted(
            ("bwd", l, "norm_low"), _norm_bwd, dx1, (dh1, sv["x"], gains["pre_mix_norm"], l), below, tm=tm)
        if l > 0:
            g_gain["post_ffn_norm"][l - 1] = g_below

    g_gain = {k: jnp.concatenate(v, axis=0) for k, v in g_gain.items()}
    return loss_lanes, dres, g_gain, jnp.stack(g_conv, axis=0)


class _Exchange:
    GATHER_HOSTS = {"in_proj": (("w_out", 0), ("w_down", 0)), "attn": (("w_gate_up", 0),), "gate_up": (("w_in", 1),)}

    @staticmethod
    def _reduce_hosts(group, l):
        if group == "ffn":
            return "norm_mid", "attn", l
        if l > 0:
            return "norm_low", "gate_up_dx", l - 1
        return "in_proj_dx", "norm_low", l

    def __init__(self, params, place):
        self.place = place
        self.slabs = {k: [_own_shard_slab(params[k], l, place, BF16) for l in range(DEPTH)] for k in MATRIX_NAMES}
        self.gathered = {k: [None] * DEPTH for k in MATRIX_NAMES}
        self.gathered["w_in"][0], self.conv_w = _run_comm(
            _gather_comm([self.slabs["w_in"][0]], _own_conv_slab(params["conv_w"], place)), "gather_first")
        self.full = {k: lax.empty(params[k].shape, F32) for k in MATRIX_NAMES}
        self.pending = {}
        self.raw = {}

    def weight(self, name, l):
        if name == "conv_w":
            return self.conv_w
        g = self.gathered[name][l]
        return g.reshape(1, g.shape[0] * g.shape[1], g.shape[2]) if name in ("w_out", "w_down") else g

    def host(self, tag):
        phase, l, kernel = tag
        if phase == "fwd":
            carried = [(name, l + ahead) for name, ahead in self.GATHER_HOSTS.get(kernel, ()) if l + ahead < DEPTH]
            return _gather_comm([self.slabs[name][layer] for name, layer in carried]) if carried else None
        if tag in self.pending:
            stage, _, _, arrays = self.pending[tag]
            return _halves_comm(arrays) if stage == "halves" else _partials_comm(arrays)
        return None

    def hosted(self, tag, results):
        phase, l, kernel = tag
        if phase == "fwd":
            carried = [(name, l + ahead) for name, ahead in self.GATHER_HOSTS[kernel] if l + ahead < DEPTH]
            for (name, layer), slab in zip(carried, results):
                self.gathered[name][layer] = slab
            return
        stage, gl, group, arrays = self.pending.pop(tag)
        names = list(self.raw[(gl, group)])
        if stage == "partials":
            self._finish_reduction(gl, names, arrays, results)
            return
        partials = [_add_halves(self.raw[(gl, group)][k], r, self.place) for k, r in zip(names, results)]
        _, ici_kernel, ici_layer = self._reduce_hosts(group, gl)
        self.pending[("bwd", ici_layer, ici_kernel)] = ("partials", gl, group, partials)

    def grads(self, l, group, grads):
        self.raw[(l, group)] = grads
        self.pending[("bwd", l, self._reduce_hosts(group, l)[0])] = ("halves", l, group, [grads[k] for k in grads])

    def _finish_reduction(self, l, names, partials, others):
        for k, p, q in zip(names, partials, others):
            self.full[k] = _sum_partials(p, q, self.place, self.full[k], l)
        shared = _run_comm(_share_comm([self.full[k] for k in names], l), "share_halves")
        for k, g in zip(names, shared):
            self.full[k] = g


def kernel(x, positions, pre_mix_norm, w_in, conv_w, attn_out_norm, conv_out_norm, w_out, post_mix_norm, pre_ffn_norm, w_gate_up, w_down, post_ffn_norm, loss_target, m_pre_mix_norm, m_w_in, m_conv_w, m_attn_out_norm, m_conv_out_norm, m_w_out, m_post_mix_norm, m_pre_ffn_norm, m_w_gate_up, m_w_down, m_post_ffn_norm, v_pre_mix_norm, v_w_in, v_conv_w, v_attn_out_norm, v_conv_out_norm, v_w_out, v_post_mix_norm, v_pre_ffn_norm, v_w_gate_up, v_w_down, v_post_ffn_norm):
    params = dict(pre_mix_norm=pre_mix_norm, w_in=w_in, conv_w=conv_w, attn_out_norm=attn_out_norm,
                  conv_out_norm=conv_out_norm, w_out=w_out, post_mix_norm=post_mix_norm, pre_ffn_norm=pre_ffn_norm,
                  w_gate_up=w_gate_up, w_down=w_down, post_ffn_norm=post_ffn_norm)
    mom1 = dict(pre_mix_norm=m_pre_mix_norm, w_in=m_w_in, conv_w=m_conv_w, attn_out_norm=m_attn_out_norm,
                conv_out_norm=m_conv_out_norm, w_out=m_w_out, post_mix_norm=m_post_mix_norm,
                pre_ffn_norm=m_pre_ffn_norm, w_gate_up=m_w_gate_up, w_down=m_w_down, post_ffn_norm=m_post_ffn_norm)
    mom2 = dict(pre_mix_norm=v_pre_mix_norm, w_in=v_w_in, conv_w=v_conv_w, attn_out_norm=v_attn_out_norm,
                conv_out_norm=v_conv_out_norm, w_out=v_w_out, post_mix_norm=v_post_mix_norm,
                pre_ffn_norm=v_pre_ffn_norm, w_gate_up=v_w_gate_up, w_down=v_w_down, post_ffn_norm=v_post_ffn_norm)
    xi, yi, ci = lax.axis_index("x"), lax.axis_index("y"), lax.axis_index("c")
    place = jnp.stack([2 * xi + yi, ci]).astype(jnp.int32)

    exchange = _Exchange(params, place)
    gains = {k: params[k][:, None, :] for k in GAIN_NAMES}
    loss_lanes, grad_x, g_gain, g_conv = _local_step(x[0], positions[0], loss_target[0], gains, exchange)
    grad = dict(exchange.full)

    small = [g_gain[k].reshape(-1) for k in GAIN_NAMES] + [g_conv.reshape(-1), loss_lanes.reshape(-1)]
    sizes = [int(s.shape[0]) for s in small]
    flat = jnp.concatenate(small)
    loss_row = (sum(sizes) - LANES) // LANES
    rows = -(-flat.shape[0] // (8 * LANES)) * 8
    flat = jnp.pad(flat, (0, rows * LANES - flat.shape[0])).reshape(rows, LANES)
    total = _allreduce_small(flat, loss_row).reshape(-1)
    offsets = np.cumsum([0] + sizes)
    for i, k in enumerate(GAIN_NAMES):
        grad[k] = total[offsets[i]:offsets[i + 1]].reshape(params[k].shape)
    conv_all = total[offsets[6]:offsets[7]].reshape(DEPTH, N_CHIPS, 3, LANES)
    grad["conv_w"] = lax.dynamic_index_in_dim(conv_all, 2 * xi + yi, axis=1, keepdims=False)
    loss = total[offsets[7]]

    delta, new_m, new_v = {}, {}, {}
    for k in WEIGHT_ORDER:
        shape = params[k].shape
        if k in MATRIX_NAMES:
            tr = {1024: 512, 704: 352, 256: 256}[shape[1]]
            delta[k], new_m[k], new_v[k], grad[k] = _adamw(params[k], grad[k], mom1[k], mom2[k], tr=tr, emit_grad=True)
        else:
            as3 = (lambda a: a) if len(shape) == 3 else (lambda a: a[None])
            d, m, v = _adamw(as3(params[k]), as3(grad[k]), as3(mom1[k]), as3(mom2[k]), tr=as3(params[k]).shape[1])
            delta[k], new_m[k], new_v[k] = d.reshape(shape), m.reshape(shape), v.reshape(shape)

    return (loss, grad_x[None], *[grad[k] for k in WEIGHT_ORDER], *[delta[k] for k in WEIGHT_ORDER],
            *[new_m[k] for k in WEIGHT_ORDER], *[new_v[k] for k in WEIGHT_ORDER])
```

```python
import functools
from typing import Callable, NamedTuple

import numpy as np
import jax
import jax.numpy as jnp
from jax import lax
from jax.experimental import pallas as pl
from jax.experimental.pallas import tpu as pltpu

F32 = jnp.float32
BF16 = jnp.bfloat16
MESH = pl.DeviceIdType.MESH

D_MODEL = 1024
ATTN_W = 512
CONV_W = 512
HEAD_DIM = 64
ROPE_DIM = 16
ROPE_THETA = 500000.0
FFN = 2816
DEPTH = 4
RMS_EPS = 1e-6
NEG_INF = -1e30
N_CHIPS = 4
N_DEV = 8
LANES = 128
BF16_ROWS = 16
DILATIONS = (1, 4, 16)
BAND = 64
TQ = 128
WIN = TQ + 2 * BAND
SCALE = HEAD_DIM ** -0.5

ADAM_LR = 0.001
ADAM_B1 = 0.9
ADAM_B2 = 0.999
ADAM_EPS = 1e-08
ADAM_WD = 0.01
ADAM_STEP = 10

GAIN_NAMES = ("pre_mix_norm", "attn_out_norm", "conv_out_norm", "post_mix_norm", "pre_ffn_norm", "post_ffn_norm")
MATRIX_NAMES = ("w_in", "w_out", "w_gate_up", "w_down")
WEIGHT_ORDER = ("pre_mix_norm", "w_in", "conv_w", "attn_out_norm", "conv_out_norm", "w_out", "post_mix_norm",
                "pre_ffn_norm", "w_gate_up", "w_down", "post_ffn_norm")

ANY = pl.BlockSpec(memory_space=pl.ANY)
WHOLE_VMEM = pl.BlockSpec(memory_space=pltpu.VMEM)
LANE_CONTRACT = (((1,), (1,)), ((), ()))
ROW_CONTRACT = (((0,), (0,)), ((), ()))
CHUNK = 256


def _const_spec(block, index):
    return pl.BlockSpec(block, lambda *_: index)


def _gain_spec(g3, l):
    return _const_spec((None, 1, g3.shape[-1]), (l, 0, 0))


class _Comm(NamedTuple):
    ins: tuple
    inouts: tuple
    out_shapes: tuple
    n_sems: int
    start: Callable
    finish: Callable


def _place():
    x, y, c = lax.axis_index("x"), lax.axis_index("y"), lax.axis_index("c")
    other_chips = [(1 - x, y), (x, 1 - y), (1 - x, 1 - y)]
    return x, y, c, other_chips


def _remote(src, dst, send_sem, recv_sem, to):
    return pltpu.make_async_remote_copy(src_ref=src, dst_ref=dst, send_sem=send_sem, recv_sem=recv_sem,
                                        device_id=to, device_id_type=MESH)


def _call(body, operands, *, name, grid, in_specs, out_specs, out_shape, scratch_shapes=(), comm=None):
    in_specs, out_specs, out_shape = list(in_specs), list(out_specs), list(out_shape)
    scratch_shapes = list(scratch_shapes)
    if comm is None:
        out = pl.pallas_call(body, grid=grid, in_specs=in_specs, out_specs=out_specs, out_shape=out_shape,
                             scratch_shapes=scratch_shapes, name=name)(*operands)
        return list(out), None
    n_in, n_out, n_scr = len(in_specs), len(out_shape), len(scratch_shapes)
    n_ci, n_cio, n_co = len(comm.ins), len(comm.inouts), len(comm.out_shapes)

    def hosted(*refs):
        refs = list(refs)
        ins, c_ins = refs[:n_in], refs[n_in:n_in + n_ci]
        base = n_in + n_ci + n_cio
        outs = refs[base:base + n_out]
        c_io = refs[base + n_out:base + n_out + n_cio]
        c_out = refs[base + n_out + n_cio:base + n_out + n_cio + n_co]
        scr = refs[base + n_out + n_cio + n_co:]
        send_sems, recv_sems = scr[n_scr], scr[n_scr + 1]
        if grid:
            first = functools.reduce(jnp.logical_and, [pl.program_id(a) == 0 for a in range(len(grid))])
            last = functools.reduce(jnp.logical_and, [pl.program_id(a) == grid[a] - 1 for a in range(len(grid))])
            pl.when(first)(lambda: comm.start(c_ins, c_io, c_out, send_sems, recv_sems))
            body(*ins, *outs, *scr[:n_scr])
            pl.when(last)(lambda: comm.finish(c_ins, c_io, c_out, send_sems, recv_sems))
        else:
            comm.start(c_ins, c_io, c_out, send_sems, recv_sems)
            body(*ins, *outs, *scr[:n_scr])
            comm.finish(c_ins, c_io, c_out, send_sems, recv_sems)

    res = pl.pallas_call(
        hosted, grid=grid, in_specs=in_specs + [ANY] * (n_ci + n_cio), out_specs=out_specs + [ANY] * (n_cio + n_co),
        out_shape=out_shape + [jax.ShapeDtypeStruct(a.shape, a.dtype) for a in comm.inouts] + list(comm.out_shapes),
        input_output_aliases={n_in + n_ci + i: n_out + i for i in range(n_cio)},
        scratch_shapes=scratch_shapes + [pltpu.SemaphoreType.DMA((comm.n_sems,))] * 2,
        name=name)(*operands, *comm.ins, *comm.inouts)
    return list(res[:n_out]), list(res[n_out:])


def _run_comm(comm, name):
    return _call(lambda: None, [], name=name, grid=(), in_specs=[], out_specs=[], out_shape=[], comm=comm)[1]


def _row_half(ref, lead, core, rows, align):
    hr = rows // 2
    return ref.at[(*lead, pl.ds(pl.multiple_of(core * hr, align), hr), slice(None))]


def _gather_comm(slabs, conv_slab=None):
    n = len(slabs)
    n_conv = 0 if conv_slab is None else 3

    def direct(ios, send, recv):
        x, y, c, chips = _place()
        copies = []
        for a in range(n):
            own = _row_half(ios[a], (2 * x + y,), c, slabs[a].shape[1], BF16_ROWS)
            copies += [_remote(own, own, send.at[a * 3 + j], recv.at[a * 3 + j], (*chip, c))
                       for j, chip in enumerate(chips)]
        if conv_slab is not None:
            own = ios[n].at[:, 2 * x + y]
            copies += [_remote(own, own, send.at[6 * n + j], recv.at[6 * n + j], (*chip, c))
                       for j, chip in enumerate(chips)]
        return copies

    def start(ins, ios, outs, send, recv):
        for cp in direct(ios, send, recv):
            cp.start()

    def finish(ins, ios, outs, send, recv):
        x, y, c, chips = _place()
        sibling = (x, y, 1 - c)
        passed = []
        for a in range(n):
            for j, chip in enumerate(chips):
                landed = _row_half(ios[a], (2 * chip[0] + chip[1],), c, slabs[a].shape[1], BF16_ROWS)
                _remote(landed, landed, send.at[a * 3 + j], recv.at[a * 3 + j], (*chip, c)).wait_recv()
                fwd = _remote(landed, landed, send.at[3 * n + a * 3 + j], recv.at[3 * n + a * 3 + j], sibling)
                fwd.start()
                passed.append(fwd)
        if conv_slab is not None:
            for j, chip in enumerate(chips):
                landed = ios[n].at[:, 2 * chip[0] + chip[1]]
                _remote(landed, landed, send.at[6 * n + j], recv.at[6 * n + j], (*chip, c)).wait_recv()
        for a in range(n):
            for j, chip in enumerate(chips):
                landed = _row_half(ios[a], (2 * chip[0] + chip[1],), 1 - c, slabs[a].shape[1], BF16_ROWS)
                _remote(landed, landed, send.at[3 * n + a * 3 + j], recv.at[3 * n + a * 3 + j], sibling).wait_recv()
        for cp in direct(ios, send, recv) + passed:
            cp.wait_send()

    inouts = tuple(slabs) + (() if conv_slab is None else (conv_slab,))
    return _Comm((), inouts, (), 6 * n + n_conv, start, finish)


def _halves_comm(grads):
    n = len(grads)

    def copies(ins, outs, send, recv):
        x, y, c, _ = _place()
        return [_remote(_row_half(ins[a], (slice(None),), 1 - c, grads[a].shape[1], 8), outs[a],
                        send.at[a], recv.at[a], (x, y, 1 - c)) for a in range(n)]

    def start(ins, ios, outs, send, recv):
        for cp in copies(ins, outs, send, recv):
            cp.start()

    def finish(ins, ios, outs, send, recv):
        for cp in copies(ins, outs, send, recv):
            cp.wait()

    out_shapes = tuple(jax.ShapeDtypeStruct((g.shape[0], g.shape[1] // 2, g.shape[2]), F32) for g in grads)
    return _Comm(tuple(grads), (), out_shapes, n, start, finish)


def _partials_comm(partials):
    n = len(partials)

    def copies(ins, outs, send, recv):
        x, y, c, chips = _place()
        return [_remote(ins[a].at[2 * chip[0] + chip[1]], outs[a].at[k], send.at[a * 3 + k], recv.at[a * 3 + k],
                        (*chip, c)) for a in range(n) for k, chip in enumerate(chips)]

    def start(ins, ios, outs, send, recv):
        for cp in copies(ins, outs, send, recv):
            cp.start()

    def finish(ins, ios, outs, send, recv):
        for cp in copies(ins, outs, send, recv):
            cp.wait()

    out_shapes = tuple(jax.ShapeDtypeStruct((3,) + p.shape[1:], BF16) for p in partials)
    return _Comm(tuple(partials), (), out_shapes, 3 * n, start, finish)


def _share_comm(grads, l):
    n = len(grads)

    def start(ins, ios, outs, send, recv):
        x, y, c, _ = _place()
        for a in range(n):
            mine = _row_half(ios[a], (l,), c, grads[a].shape[1], 8)
            _remote(mine, mine, send.at[a], recv.at[a], (x, y, 1 - c)).start()

    def finish(ins, ios, outs, send, recv):
        x, y, c, _ = _place()
        for a in range(n):
            theirs = _row_half(ios[a], (l,), 1 - c, grads[a].shape[1], 8)
            _remote(theirs, theirs, send.at[a], recv.at[a], (x, y, 1 - c)).wait()

    return _Comm((), tuple(grads), (), n, start, finish)


def _rms_fwd(x, g):
    r = lax.rsqrt(jnp.mean(x * x, axis=-1, keepdims=True) + RMS_EPS)
    return (x * r) * g


def _rms_bwd(x, g, dy):
    r = lax.rsqrt(jnp.mean(x * x, axis=-1, keepdims=True) + RMS_EPS)
    xh = x * r
    u = dy * g
    dx = r * (u - xh * jnp.mean(xh * u, axis=-1, keepdims=True))
    return dx, jnp.sum(dy * xh, axis=0, keepdims=True)


def _accumulate(ref, value, first):
    @pl.when(first)
    def _():
        ref[...] = value

    @pl.when(jnp.logical_not(first))
    def _():
        ref[...] += value


def _rope_coeffs(cos, sin):
    m = lax.broadcasted_iota(jnp.int32, cos.shape, 1) % HEAD_DIM
    a = jnp.where(m < ROPE_DIM, cos, 1.0)
    b = jnp.where(m < ROPE_DIM // 2, -sin, 0.0)
    c = jnp.where((m >= ROPE_DIM // 2) & (m < ROPE_DIM), sin, 0.0)
    return a, b, c


def _rope_apply(t, cos, sin):
    a, b, c = _rope_coeffs(cos, sin)
    n = t.shape[1]
    return a * t + b * pltpu.roll(t, n - ROPE_DIM // 2, 1) + c * pltpu.roll(t, ROPE_DIM // 2, 1)


def _rope_transpose(dt, cos, sin):
    a, b, c = _rope_coeffs(cos, sin)
    n = dt.shape[1]
    return a * dt + pltpu.roll(b * dt, ROPE_DIM // 2, 1) + pltpu.roll(c * dt, n - ROPE_DIM // 2, 1)


def _in_proj(h, w, cos, sin, *, tm, comm=None):
    t, k = h.shape
    s_n, _, n = w.shape
    assert t % tm == 0 and n % LANES == 0

    def body(h_ref, w_ref, cos_ref, sin_ref, o_ref):
        o_ref[...] = jnp.dot(h_ref[...], w_ref[...], preferred_element_type=F32)
        for s in range(s_n):
            rotary_cols = min(max(2 * ATTN_W - s * n, 0), n)
            if rotary_cols:
                @pl.when(pl.program_id(1) == s)
                def _():
                    for c0 in range(0, rotary_cols, LANES):
                        cols = slice(c0, c0 + LANES)
                        o_ref[:, cols] = _rope_apply(o_ref[:, cols], cos_ref[...], sin_ref[...])

    lane_tile = pl.BlockSpec((tm, LANES), lambda i, s: (i, 0))
    out, got = _call(
        body, [h, w, cos, sin], name="in_proj", grid=(t // tm, s_n),
        in_specs=[pl.BlockSpec((tm, k), lambda i, s: (i, 0)), pl.BlockSpec((None, k, n), lambda i, s: (s, 0, 0)),
                  lane_tile, lane_tile],
        out_specs=[pl.BlockSpec((tm, n), lambda i, s: (i, s))],
        out_shape=[jax.ShapeDtypeStruct((t, s_n * n), F32)], comm=comm)
    return out[0], got


def _mm_nt(a, w, *, tm, tko, name, comm=None):
    t, sn = a.shape
    s_n, ko, n = w.shape
    assert sn == s_n * n and t % tm == 0 and ko % tko == 0

    def body(a_ref, w_ref, o_ref):
        acc = lax.dot_general(a_ref[...], w_ref[...], (((1,), (1,)), ((), ())), preferred_element_type=F32)
        if s_n == 1:
            o_ref[...] = acc
        else:
            _accumulate(o_ref, acc, pl.program_id(2) == 0)

    out, got = _call(
        body, [a, w], name=name, grid=(t // tm, ko // tko, s_n),
        in_specs=[pl.BlockSpec((tm, n), lambda i, j, s: (i, s)),
                  pl.BlockSpec((None, tko, n), lambda i, j, s: (s, j, 0))],
        out_specs=[pl.BlockSpec((tm, tko), lambda i, j, s: (i, j))],
        out_shape=[jax.ShapeDtypeStruct((t, ko), F32)], comm=comm)
    return out[0], got


def _dx_through_norms(operands, in_specs, partial, n_steps, dres, pre, post, *, tm, name, comm=None):
    t, d_model = dres.shape
    (x, gx3, lx), (y, gy3, ly) = pre, post
    n_op = len(operands)
    row = pl.BlockSpec((tm, d_model), lambda i, s: (i, 0))
    gsum = _const_spec((1, d_model), (0, 0))

    def body(*refs):
        op_refs = refs[:n_op]
        d_ref, x_ref, gx_ref, y_ref, gy_ref, dn_ref, dgx_ref, dy_ref, dgy_ref, acc = refs[n_op:]
        first, step = pl.program_id(0) == 0, pl.program_id(1)
        _accumulate(acc, partial(op_refs), step == 0)

        @pl.when(step == n_steps - 1)
        def _():
            dx, dgx = _rms_bwd(x_ref[...], gx_ref[...], acc[...])
            d_new = d_ref[...] + dx
            dn_ref[...] = d_new
            _accumulate(dgx_ref, dgx, first)
            dy, dgy = _rms_bwd(y_ref[...], gy_ref[...], d_new)
            dy_ref[...] = dy.astype(BF16)
            _accumulate(dgy_ref, dgy, first)

    out, got = _call(
        body, list(operands) + [dres, x, gx3, y, gy3], name=name, grid=(t // tm, n_steps),
        in_specs=list(in_specs) + [row, row, _gain_spec(gx3, lx), row, _gain_spec(gy3, ly)],
        out_specs=[row, gsum, row, gsum],
        out_shape=[jax.ShapeDtypeStruct((t, d_model), F32), jax.ShapeDtypeStruct((1, d_model), F32),
                   jax.ShapeDtypeStruct((t, d_model), BF16), jax.ShapeDtypeStruct((1, d_model), F32)],
        scratch_shapes=[pltpu.VMEM((tm, d_model), F32)], comm=comm)
    return out[0], out[2], out[1], out[3], got


def _gate_up_dx_norms(dg, du, w, dres, pre, post, *, tm, comm=None):
    s_n, ko, n = w.shape
    half = s_n // 2

    def partial(refs):
        a0_ref, a1_ref, w0_ref, w1_ref = refs
        return (lax.dot_general(a0_ref[...], w0_ref[...], LANE_CONTRACT, preferred_element_type=F32)
                + lax.dot_general(a1_ref[...], w1_ref[...], LANE_CONTRACT, preferred_element_type=F32))

    a_spec = pl.BlockSpec((tm, n), lambda i, s: (i, s))
    in_specs = [a_spec, a_spec, pl.BlockSpec((None, ko, n), lambda i, s: (s, 0, 0)),
                pl.BlockSpec((None, ko, n), lambda i, s: (half + s, 0, 0))]
    return _dx_through_norms([dg, du, w, w], in_specs, partial, half, dres, pre, post, tm=tm, name="gate_up_dx",
                             comm=comm)


def _in_proj_dx_norms(d_attn3, d_conv3, w, dres, pre, post, *, tm, comm=None):
    s_n, ko, n = w.shape
    half, per = s_n // 2, n // CHUNK

    def partial(refs):
        a_refs, b_refs, wa_ref, wb_ref = refs[:per], refs[per:2 * per], refs[2 * per], refs[2 * per + 1]
        acc = jnp.zeros((tm, ko), F32)
        for r in range(per):
            cols = slice(r * CHUNK, (r + 1) * CHUNK)
            acc = acc + lax.dot_general(a_refs[r][...], wa_ref[:, cols], LANE_CONTRACT, preferred_element_type=F32)
            acc = acc + lax.dot_general(b_refs[r][...], wb_ref[:, cols], LANE_CONTRACT, preferred_element_type=F32)
        return acc

    piece = lambda r: pl.BlockSpec((None, tm, CHUNK), lambda i, s: ((per * s + r) // 2, i, (per * s + r) % 2))
    in_specs = ([piece(r) for r in range(per)] * 2
                + [pl.BlockSpec((None, ko, n), lambda i, s: (s, 0, 0)), pl.BlockSpec((None, ko, n), lambda i, s: (half + s, 0, 0))])
    return _dx_through_norms([d_attn3] * per + [d_conv3] * per + [w, w], in_specs, partial, half, dres, pre, post,
                             tm=tm, name="in_proj_dx", comm=comm)


def _mm_tn(a, b, s_n, *, tka, name, into=None, shard0=0):
    t, ka = a.shape
    n = b.shape[1] // s_n
    assert b.shape[0] == t and ka % tka == 0

    def body(a_ref, b_ref, *rest):
        rest[-1][...] = lax.dot_general(a_ref[...], b_ref[...], ROW_CONTRACT, preferred_element_type=F32)

    operands, in_specs, aliases = [a, b], [pl.BlockSpec((t, tka), lambda i, s: (0, i)),
                                           pl.BlockSpec((t, n), lambda i, s: (0, s))], {}
    out_shape = jax.ShapeDtypeStruct((s_n, ka, n), F32)
    if into is not None:
        operands, in_specs, aliases = operands + [into], in_specs + [ANY], {2: 0}
        out_shape = jax.ShapeDtypeStruct(into.shape, F32)
    return pl.pallas_call(
        body, grid=(ka // tka, s_n), in_specs=in_specs,
        out_specs=pl.BlockSpec((None, tka, n), lambda i, s: (shard0 + s, i, 0)),
        out_shape=out_shape, input_output_aliases=aliases, name=name)(*operands)


def _gate_up_swiglu(h, w, *, tm, comm=None):
    t, k = h.shape
    s_n, _, n = w.shape
    half = s_n // 2

    def body(h_ref, wg_ref, wu_ref, g_ref, u_ref, a_ref):
        g = jnp.dot(h_ref[...], wg_ref[...], preferred_element_type=F32)
        u = jnp.dot(h_ref[...], wu_ref[...], preferred_element_type=F32)
        g_ref[...] = g.astype(BF16)
        u_ref[...] = u.astype(BF16)
        a_ref[...] = (g * jax.nn.sigmoid(g) * u).astype(BF16)

    col = pl.BlockSpec((tm, n), lambda i, j: (i, j))
    out, got = _call(
        body, [h, w, w], name="gate_up", grid=(t // tm, half),
        in_specs=[pl.BlockSpec((tm, k), lambda i, j: (i, 0)), pl.BlockSpec((None, k, n), lambda i, j: (j, 0, 0)),
                  pl.BlockSpec((None, k, n), lambda i, j: (half + j, 0, 0))],
        out_specs=[col, col, col],
        out_shape=[jax.ShapeDtypeStruct((t, half * n), BF16)] * 3, comm=comm)
    return out[0], out[1], out[2], got


def _down_dx_swiglu_bwd(df, w, g, u, *, tm, tko):
    t, k = df.shape
    _, ko, _ = w.shape
    assert t % tm == 0 and ko % tko == 0

    def body(df_ref, w_ref, g_ref, u_ref, dg_ref, du_ref):
        d = lax.dot_general(df_ref[...], w_ref[...], LANE_CONTRACT, preferred_element_type=F32)
        gg = g_ref[...].astype(F32)
        sig = jax.nn.sigmoid(gg)
        dg_ref[...] = (d * u_ref[...].astype(F32) * (sig * (1.0 + gg * (1.0 - sig)))).astype(BF16)
        du_ref[...] = (d * (gg * sig)).astype(BF16)

    col = pl.BlockSpec((tm, tko), lambda i, j: (i, j))
    return pl.pallas_call(
        body, grid=(t // tm, ko // tko),
        in_specs=[pl.BlockSpec((tm, k), lambda i, j: (i, 0)), pl.BlockSpec((None, tko, k), lambda i, j: (0, j, 0)),
                  col, col],
        out_specs=[col, col], out_shape=[jax.ShapeDtypeStruct((t, ko), BF16)] * 2, name="down_dx")(df, w, g, u)


def _in_proj_dx(d_attn3, d_conv3, w, *, tm, comm=None):
    _, t, _ = d_attn3.shape
    s_n, ko, n = w.shape
    half, per = s_n // 2, n // CHUNK
    assert t % tm == 0

    def body(*refs):
        a_refs, b_refs, wa_ref, wb_ref, o_ref = refs[:per], refs[per:2 * per], refs[2 * per], refs[2 * per + 1], refs[-1]
        acc = jnp.zeros(o_ref.shape, F32)
        for r in range(per):
            cols = slice(r * CHUNK, (r + 1) * CHUNK)
            acc = acc + lax.dot_general(a_refs[r][...], wa_ref[:, cols], LANE_CONTRACT, preferred_element_type=F32)
            acc = acc + lax.dot_general(b_refs[r][...], wb_ref[:, cols], LANE_CONTRACT, preferred_element_type=F32)
        _accumulate(o_ref, acc, pl.program_id(1) == 0)

    piece = lambda r: pl.BlockSpec((None, tm, CHUNK), lambda i, s: ((per * s + r) // 2, i, (per * s + r) % 2))
    out, got = _call(
        body, [d_attn3] * per + [d_conv3] * per + [w, w], name="in_proj_dx", grid=(t // tm, half),
        in_specs=[piece(r) for r in range(per)] * 2
        + [pl.BlockSpec((None, ko, n), lambda i, s: (s, 0, 0)), pl.BlockSpec((None, ko, n), lambda i, s: (half + s, 0, 0))],
        out_specs=[pl.BlockSpec((tm, ko), lambda i, s: (i, 0))],
        out_shape=[jax.ShapeDtypeStruct((t, ko), F32)], comm=comm)
    return out[0], got


def _in_proj_dw(h, d_attn3, d_conv3, s_n, *, tka):
    t, ka = h.shape
    half = s_n // 2
    n = 3 * d_attn3.shape[2] // half
    per = n // CHUNK
    assert ka % tka == 0

    def body(*refs):
        h_ref, o_ref = refs[0], refs[-1]
        for side in range(2):
            for r in range(per):
                o_ref[side, :, r * CHUNK:(r + 1) * CHUNK] = lax.dot_general(
                    h_ref[...], refs[1 + side * per + r][...], ROW_CONTRACT, preferred_element_type=F32)

    piece = lambda r: pl.BlockSpec((None, t, CHUNK), lambda i, s: ((per * s + r) // 2, 0, (per * s + r) % 2))
    out = pl.pallas_call(
        body, grid=(ka // tka, half),
        in_specs=[pl.BlockSpec((t, tka), lambda i, s: (0, i))] + [piece(r) for r in range(per)] * 2,
        out_specs=pl.BlockSpec((2, None, tka, n), lambda i, s: (0, s, i, 0)),
        out_shape=jax.ShapeDtypeStruct((2, half, ka, n), F32), name="in_proj_dw")(h, *[d_attn3] * per, *[d_conv3] * per)
    return out.reshape(s_n, ka, n)


def _rope_tables(positions_col, inv_freq_row):
    t = positions_col.shape[0]

    def body(pos_ref, f_ref, cos_ref, sin_ref):
        ang = pos_ref[...].astype(F32) * f_ref[...]
        cos_ref[...] = jnp.cos(ang)
        sin_ref[...] = jnp.sin(ang)

    return pl.pallas_call(
        body, out_shape=[jax.ShapeDtypeStruct((t, LANES), F32)] * 2, name="rope_tables")(positions_col, inv_freq_row)


def _norm_fwd(x, g3, l, *, tm):
    t, w = x.shape

    def body(x_ref, g_ref, h_ref):
        h_ref[...] = _rms_fwd(x_ref[...], g_ref[...]).astype(BF16)

    return pl.pallas_call(
        body, grid=(t // tm,),
        in_specs=[pl.BlockSpec((tm, w), lambda i: (i, 0)), _gain_spec(g3, l)],
        out_specs=pl.BlockSpec((tm, w), lambda i: (i, 0)),
        out_shape=jax.ShapeDtypeStruct((t, w), BF16), name="norm_fwd")(x, g3)


def _mm_resnorm(a, w, x, g_post3, l_post, g_next3, l_next, *, tm, name, comm=None):
    t, k = a.shape
    _, _, n = w.shape
    with_next = g_next3 is not None
    row = pl.BlockSpec((tm, n), lambda i: (i, 0))

    def body(a_ref, w_ref, x_ref, gp_ref, *rest):
        y = jnp.dot(a_ref[...], w_ref[...], preferred_element_type=F32)
        x_new = x_ref[...] + _rms_fwd(y, gp_ref[...])
        if with_next:
            gn_ref, y_ref, xo_ref, h_ref = rest
            h_ref[...] = _rms_fwd(x_new, gn_ref[...]).astype(BF16)
        else:
            y_ref, xo_ref = rest
        y_ref[...] = y
        xo_ref[...] = x_new

    ins = [a, w, x, g_post3] + ([g_next3] if with_next else [])
    in_specs = ([pl.BlockSpec((tm, k), lambda i: (i, 0)), _const_spec((None, k, n), (0, 0, 0)), row,
                 _gain_spec(g_post3, l_post)] + ([_gain_spec(g_next3, l_next)] if with_next else []))
    out_shape = [jax.ShapeDtypeStruct((t, n), F32)] * 2 + ([jax.ShapeDtypeStruct((t, n), BF16)] if with_next else [])
    out, got = _call(body, ins, name=name, grid=(t // tm,), in_specs=in_specs, out_specs=[row] * len(out_shape),
                     out_shape=out_shape, comm=comm)
    return out[0], out[1], (out[2] if with_next else None), got


def _conv_fwd(proj, conv_w, l):
    t = proj.shape[0]
    col0 = 3 * ATTN_W // LANES

    def body(u_ref, gb_ref, gc_ref, w_ref, y_ref):
        c = gc_ref[...] * u_ref[...]
        row = lax.broadcasted_iota(jnp.int32, c.shape, 0)
        c_prev = jnp.where(row == 0, 0.0, pltpu.roll(c, 1, 0))
        c_next = jnp.where(row == t - 1, 0.0, pltpu.roll(c, t - 1, 0))
        w = w_ref[...]
        y_ref[...] = gb_ref[...] * (w[0:1] * c_prev + w[1:2] * c + w[2:3] * c_next)

    nj = CONV_W // LANES
    cols = lambda base: pl.BlockSpec((t, LANES), lambda j: (0, base + j))
    return pl.pallas_call(
        body, grid=(nj,),
        in_specs=[cols(col0), cols(col0 + nj), cols(col0 + 2 * nj),
                  pl.BlockSpec((None, None, 3, LANES), lambda j: (l, j, 0, 0))],
        out_specs=pl.BlockSpec((t, LANES), lambda j: (0, j)),
        out_shape=jax.ShapeDtypeStruct((t, CONV_W), F32), name="conv_fwd")(proj, proj, proj, conv_w)


def _merge_fwd(attn, conv_y, ga3, gc3, l, *, tm):
    t = attn.shape[0]
    row = pl.BlockSpec((tm, ATTN_W), lambda i: (i, 0))

    def body(a_ref, c_ref, ga_ref, gc_ref, m_ref):
        m_ref[:, :ATTN_W] = _rms_fwd(a_ref[...], ga_ref[...]).astype(BF16)
        m_ref[:, ATTN_W:] = _rms_fwd(c_ref[...], gc_ref[...]).astype(BF16)

    return pl.pallas_call(
        body, grid=(t // tm,),
        in_specs=[row, row, _gain_spec(ga3, l), _gain_spec(gc3, l)],
        out_specs=pl.BlockSpec((tm, D_MODEL), lambda i: (i, 0)),
        out_shape=jax.ShapeDtypeStruct((t, D_MODEL), BF16), name="merge_fwd")(attn, conv_y, ga3, gc3)


def _loss_fwd_bwd(y, target, *, tm):
    t, w = y.shape
    row = pl.BlockSpec((tm, w), lambda i: (i, 0))

    def body(y_ref, t_ref, dy_ref, loss_ref):
        e = y_ref[...] - t_ref[...]
        dy_ref[...] = e * (1.0 / w)
        sq = jnp.sum(e * e, axis=0, keepdims=True) * (0.5 / w)
        part = sq[:, :LANES]
        for j in range(1, w // LANES):
            part = part + sq[:, j * LANES:(j + 1) * LANES]
        _accumulate(loss_ref, part, pl.program_id(0) == 0)

    return pl.pallas_call(
        body, grid=(t // tm,), in_specs=[row, row],
        out_specs=[row, _const_spec((1, LANES), (0, 0))],
        out_shape=[jax.ShapeDtypeStruct((t, w), F32), jax.ShapeDtypeStruct((1, LANES), F32)], name="loss")(y, target)


def _tile_rows(t, nt, lb, d):
    r = t // nt
    q0 = (t % nt) * TQ
    m0 = jnp.clip(q0 - BAND, 0, lb - WIN)
    if d == 1:
        return pl.ds(pl.multiple_of(q0, TQ), TQ), pl.ds(pl.multiple_of(m0, BAND), WIN), m0 - q0
    return pl.ds(r + d * q0, TQ, stride=d), pl.ds(r + d * m0, WIN, stride=d), m0 - q0


def _for_row_chunks(t, fn, chunk=512):
    def step(i, carry):
        fn(pl.ds(pl.multiple_of(i * chunk, chunk), chunk))
        return carry

    lax.fori_loop(0, t // chunk, step, 0)


WINDOW_OFFSETS = (-BAND, 0, -2 * BAND)


def _fill_band_bias(bias_ref):
    rel0 = (lax.broadcasted_iota(jnp.int32, (2 * TQ, WIN), 1)
            - lax.broadcasted_iota(jnp.int32, (2 * TQ, WIN), 0) % TQ)
    for j, off in enumerate(WINDOW_OFFSETS):
        rel = rel0 + off
        bias_ref[j] = jnp.where((rel >= -BAND) & (rel <= BAND), 0.0, NEG_INF)


def _fill_sequence_bias(bias_ref):
    rel = (lax.broadcasted_iota(jnp.int32, (2 * WIN, WIN), 1) - lax.broadcasted_iota(jnp.int32, (2 * WIN, WIN), 0) % WIN)
    bias_ref[...] = jnp.where((rel >= -BAND) & (rel <= BAND), 0.0, NEG_INF)


def _band_bias(bias_ref, off):
    return bias_ref[jnp.where(off == WINDOW_OFFSETS[0], 0, jnp.where(off == WINDOW_OFFSETS[1], 1, 2))]


def _stack_heads(a, first_head):
    return jnp.concatenate([jnp.where(first_head, a, 0.0), jnp.where(first_head, 0.0, a)], axis=0)


def _unstack_heads(a2, first_head):
    n = a2.shape[0] // 2
    return jnp.where(first_head, a2[:n], a2[n:])


def _attn_fwd(proj, comm=None):
    t = proj.shape[0]
    npair = ATTN_W // LANES

    def body(q_ref, k_ref, v_ref, o_ref, lse_ref, o1, o2, l0, l1, l2, m1, m2, bias, bias_seq):
        _fill_band_bias(bias)
        _fill_sequence_bias(bias_seq)
        outs, dens, maxs = (o_ref, o1, o2), (l0, l1, l2), (lse_ref, m1, m2)

        def softmax_tile(b, qrows, krows, n_q, band_bias):
            first_head = lax.broadcasted_iota(jnp.int32, (n_q, LANES), 1) < HEAD_DIM
            q2 = _stack_heads(q_ref[qrows, :] * SCALE, first_head).astype(BF16)
            kw = k_ref[krows, :].astype(BF16)
            vw = jnp.concatenate([v_ref[krows, :].astype(BF16), jnp.ones((WIN, LANES), BF16)], axis=1)
            s = lax.dot_general(q2, kw, LANE_CONTRACT, preferred_element_type=F32) + band_bias
            m = jnp.max(s, axis=-1, keepdims=True)
            pv = jnp.dot(jnp.exp(s - m).astype(BF16), vw, preferred_element_type=F32)
            outs[b][qrows, :] = _unstack_heads(pv[:, :LANES], first_head)
            dens[b][qrows, :] = _unstack_heads(pv[:, LANES:], first_head)
            maxs[b][qrows, :] = _unstack_heads(jnp.broadcast_to(m, (2 * n_q, LANES)), first_head)

        for b, d in enumerate(DILATIONS):
            lb = t // d
            if lb == WIN:
                def sequence(r, carry, b=b, d=d):
                    rows = pl.ds(r, WIN, stride=d)
                    softmax_tile(b, rows, rows, WIN, bias_seq[...])
                    return carry

                lax.fori_loop(0, d, sequence, 0, unroll=4)
                continue
            nt = lb // TQ

            def tile(ti, carry, b=b, d=d, lb=lb, nt=nt):
                qrows, krows, off = _tile_rows(ti, nt, lb, d)
                softmax_tile(b, qrows, krows, TQ, _band_bias(bias, off))
                return carry

            lax.fori_loop(0, d * nt, tile, 0, unroll=8)

        def finish(rows):
            ms = [m_b[rows, :] for m_b in maxs]
            m_all = jnp.maximum(jnp.maximum(ms[0], ms[1]), ms[2])
            ws = [jnp.exp(m_b - m_all) for m_b in ms]
            den = ws[0] * dens[0][rows, :] + ws[1] * dens[1][rows, :] + ws[2] * dens[2][rows, :]
            num = ws[0] * outs[0][rows, :] + ws[1] * outs[1][rows, :] + ws[2] * outs[2][rows, :]
            o_ref[rows, :] = num / den
            lse_ref[rows, :] = m_all + jnp.log(den)

        _for_row_chunks(t, finish, 256)

    cols = lambda base: pl.BlockSpec((t, LANES), lambda g: (0, base + g))
    out, got = _call(
        body, [proj, proj, proj], name="attn_fwd", grid=(npair,),
        in_specs=[cols(0), cols(npair), cols(2 * npair)],
        out_specs=[cols(0), cols(0)],
        out_shape=[jax.ShapeDtypeStruct((t, ATTN_W), F32)] * 2,
        scratch_shapes=[pltpu.VMEM((t, LANES), F32)] * 7 + [pltpu.VMEM((len(WINDOW_OFFSETS), 2 * TQ, WIN), F32),
                                                            pltpu.VMEM((2 * WIN, WIN), F32)],
        comm=comm)
    return out[0], out[1], got


def _attn_bwd(proj, cos, sin, d_attn, lse, delta, comm=None):
    t = proj.shape[0]
    npair = ATTN_W // LANES

    def body(q_ref, k_ref, v_ref, cos_ref, sin_ref, do_ref, l_ref, dl_ref, dqkv_ref,
             dq_acc, dk_acc, dv_acc, bias, bias_seq):
        _fill_band_bias(bias)
        _fill_sequence_bias(bias_seq)
        dq_acc[...] = jnp.zeros(dq_acc.shape, F32)
        dk_acc[...] = jnp.zeros(dk_acc.shape, F32)
        dv_acc[...] = jnp.zeros(dv_acc.shape, F32)
        def stack_column(a):
            return jnp.concatenate([a[:, 0:1], a[:, HEAD_DIM:HEAD_DIM + 1]], axis=0)

        def grad_tile(qrows, krows, n_q, band_bias):
            first_head = lax.broadcasted_iota(jnp.int32, (n_q, LANES), 1) < HEAD_DIM
            q2 = _stack_heads(q_ref[qrows, :] * SCALE, first_head).astype(BF16)
            do2 = _stack_heads(do_ref[qrows, :], first_head).astype(BF16)
            kw = k_ref[krows, :].astype(BF16)
            vw = v_ref[krows, :].astype(BF16)
            s = lax.dot_general(q2, kw, LANE_CONTRACT, preferred_element_type=F32) + band_bias
            p = jnp.exp(s - stack_column(l_ref[qrows, :]))
            dp = lax.dot_general(do2, vw, LANE_CONTRACT, preferred_element_type=F32)
            ds = (p * (dp - stack_column(dl_ref[qrows, :]))).astype(BF16)
            dq2 = jnp.dot(ds, kw, preferred_element_type=F32)
            dq_acc[qrows, :] += _unstack_heads(dq2, first_head) * SCALE
            dk_acc[krows, :] += lax.dot_general(ds, q2, ROW_CONTRACT, preferred_element_type=F32)
            dv_acc[krows, :] += lax.dot_general(p.astype(BF16), do2, ROW_CONTRACT, preferred_element_type=F32)

        for d in DILATIONS:
            lb = t // d
            if lb == WIN:
                def sequence(r, carry, d=d):
                    rows = pl.ds(r, WIN, stride=d)
                    grad_tile(rows, rows, WIN, bias_seq[...])
                    return carry

                lax.fori_loop(0, d, sequence, 0, unroll=2)
                continue
            nt = lb // TQ

            def tile(ti, carry, d=d, lb=lb, nt=nt):
                qrows, krows, off = _tile_rows(ti, nt, lb, d)
                grad_tile(qrows, krows, TQ, _band_bias(bias, off))
                return carry

            lax.fori_loop(0, d * nt, tile, 0, unroll=4)

        def finish(rows):
            dqkv_ref[0, rows, :] = _rope_transpose(dq_acc[rows, :], cos_ref[rows, :], sin_ref[rows, :]).astype(BF16)
            dqkv_ref[1, rows, :] = _rope_transpose(dk_acc[rows, :], cos_ref[rows, :], sin_ref[rows, :]).astype(BF16)
            dqkv_ref[2, rows, :] = dv_acc[rows, :].astype(BF16)

        _for_row_chunks(t, finish)

    cols = lambda base: pl.BlockSpec((t, LANES), lambda g: (0, base + g))
    out, got = _call(
        body, [proj, proj, proj, cos, sin, d_attn, lse, delta], name="attn_bwd", grid=(npair,),
        in_specs=[cols(0), cols(npair), cols(2 * npair), WHOLE_VMEM, WHOLE_VMEM, cols(0), cols(0), cols(0)],
        out_specs=[pl.BlockSpec((3, t, LANES), lambda g: (0, 0, g))],
        out_shape=[jax.ShapeDtypeStruct((3, t, ATTN_W), BF16)],
        scratch_shapes=[pltpu.VMEM((t, LANES), F32)] * 3 + [pltpu.VMEM((len(WINDOW_OFFSETS), 2 * TQ, WIN), F32),
                                                            pltpu.VMEM((2 * WIN, WIN), F32)],
        comm=comm)
    return out[0], got


def _norm_bwd(dres, pre, post, *, tm, comm=None):
    t, w = dres.shape
    row = pl.BlockSpec((tm, w), lambda i: (i, 0))
    gsum = _const_spec((1, w), (0, 0))
    ins, in_specs, out_shape, out_specs = [dres], [row], [], []
    if pre is not None:
        dh, x, g3, l = pre
        ins += [dh, x, g3]
        in_specs += [row, row, _gain_spec(g3, l)]
        out_shape += [jax.ShapeDtypeStruct((t, w), F32), jax.ShapeDtypeStruct((1, w), F32)]
        out_specs += [row, gsum]
    if post is not None:
        y, g3, l = post
        ins += [y, g3]
        in_specs += [row, _gain_spec(g3, l)]
        out_shape += [jax.ShapeDtypeStruct((t, w), BF16), jax.ShapeDtypeStruct((1, w), F32)]
        out_specs += [row, gsum]
    n_in = len(ins)

    def body(*refs):
        first = pl.program_id(0) == 0
        ins_r, outs_r = list(refs[:n_in]), list(refs[n_in:])
        d = ins_r.pop(0)[...]
        if pre is not None:
            dh_ref, x_ref, g_ref = ins_r[:3]
            ins_r = ins_r[3:]
            dx, dg = _rms_bwd(x_ref[...], g_ref[...], dh_ref[...])
            d = d + dx
            outs_r.pop(0)[...] = d
            _accumulate(outs_r.pop(0), dg, first)
        if post is not None:
            y_ref, g_ref = ins_r
            dy, dg = _rms_bwd(y_ref[...], g_ref[...], d)
            outs_r.pop(0)[...] = dy.astype(BF16)
            _accumulate(outs_r.pop(0), dg, first)

    out, got = _call(body, ins, name="norm_bwd", grid=(t // tm,), in_specs=in_specs, out_specs=out_specs,
                     out_shape=out_shape, comm=comm)
    d_new, dg_pre = (out.pop(0), out.pop(0)) if pre is not None else (None, None)
    dy, dg_post = (out.pop(0), out.pop(0)) if post is not None else (None, None)
    return d_new, dy, dg_pre, dg_post, got


def _merge_bwd(d_merged, attn, conv_y, ga3, gc3, l, *, tm, comm=None):
    t = attn.shape[0]
    row = pl.BlockSpec((tm, ATTN_W), lambda i: (i, 0))
    gsum = _const_spec((1, ATTN_W), (0, 0))

    def body(dma_ref, dmc_ref, a_ref, c_ref, ga_ref, gc_ref, da_ref, dl_ref, dc_ref, dga_ref, dgc_ref):
        first = pl.program_id(0) == 0
        attn_t = a_ref[...]
        da, dga = _rms_bwd(attn_t, ga_ref[...], dma_ref[...])
        dc, dgc = _rms_bwd(c_ref[...], gc_ref[...], dmc_ref[...])
        da_ref[...] = da
        dc_ref[...] = dc
        same_head = (lax.broadcasted_iota(jnp.int32, (ATTN_W, ATTN_W), 0) // HEAD_DIM
                     == lax.broadcasted_iota(jnp.int32, (ATTN_W, ATTN_W), 1) // HEAD_DIM).astype(BF16)
        rest = da * attn_t
        total = jnp.zeros(rest.shape, F32)
        for _ in range(3):
            term = rest.astype(BF16)
            total = total + jnp.dot(term, same_head, preferred_element_type=F32)
            rest = rest - term.astype(F32)
        dl_ref[...] = total
        _accumulate(dga_ref, dga, first)
        _accumulate(dgc_ref, dgc, first)

    out, got = _call(
        body, [d_merged, d_merged, attn, conv_y, ga3, gc3], name="merge_bwd", grid=(t // tm,),
        in_specs=[pl.BlockSpec((tm, ATTN_W), lambda i: (i, 0)), pl.BlockSpec((tm, CONV_W), lambda i: (i, 1)),
                  row, row, _gain_spec(ga3, l), _gain_spec(gc3, l)],
        out_specs=[row, row, row, gsum, gsum],
        out_shape=[jax.ShapeDtypeStruct((t, ATTN_W), F32)] * 3 + [jax.ShapeDtypeStruct((1, ATTN_W), F32)] * 2,
        comm=comm)
    return (*out, got)


def _conv_bwd(proj, conv_w, l, d_conv_y):
    t = proj.shape[0]
    col0 = 3 * ATTN_W // LANES
    nj = CONV_W // LANES

    def body(u_ref, gb_ref, gc_ref, w_ref, dy_ref, d3_ref, dw_ref):
        u, gc, dy = u_ref[...], gc_ref[...], dy_ref[...]
        row = lax.broadcasted_iota(jnp.int32, u.shape, 0)
        down = lambda a: jnp.where(row == 0, 0.0, pltpu.roll(a, 1, 0))
        up = lambda a: jnp.where(row == t - 1, 0.0, pltpu.roll(a, t - 1, 0))
        w = w_ref[...]
        c = gc * u
        c_prev, c_next = down(c), up(c)
        d3_ref[1] = (dy * (w[0:1] * c_prev + w[1:2] * c + w[2:3] * c_next)).astype(BF16)
        dz = dy * gb_ref[...]
        dc = w[0:1] * up(dz) + w[1:2] * dz + w[2:3] * down(dz)
        d3_ref[0] = (dc * gc).astype(BF16)
        d3_ref[2] = (dc * u).astype(BF16)
        dw_ref[0:1, :] = jnp.sum(dz * c_prev, axis=0, keepdims=True)
        dw_ref[1:2, :] = jnp.sum(dz * c, axis=0, keepdims=True)
        dw_ref[2:3, :] = jnp.sum(dz * c_next, axis=0, keepdims=True)

    cols = lambda base: pl.BlockSpec((t, LANES), lambda j: (0, base + j))
    return pl.pallas_call(
        body, grid=(nj,),
        in_specs=[cols(col0), cols(col0 + nj), cols(col0 + 2 * nj),
                  pl.BlockSpec((None, None, 3, LANES), lambda j: (l, j, 0, 0)), cols(0)],
        out_specs=[pl.BlockSpec((3, t, LANES), lambda j: (0, 0, j)), pl.BlockSpec((None, 3, LANES), lambda j: (j, 0, 0))],
        out_shape=[jax.ShapeDtypeStruct((3, t, CONV_W), BF16), jax.ShapeDtypeStruct((nj, 3, LANES), F32)],
        name="conv_bwd")(proj, proj, proj, conv_w, d_conv_y)


def _own_shard_slab(w, l, place, dtype):
    _, rows, cols = w.shape
    tr = rows if rows <= 704 else 512
    assert rows % tr == 0

    def body(p_ref, w_ref, o_ref):
        del p_ref
        o_ref[...] = w_ref[...].astype(dtype)

    grid_spec = pltpu.PrefetchScalarGridSpec(
        num_scalar_prefetch=1, grid=(rows // tr,),
        in_specs=[pl.BlockSpec((None, tr, cols), lambda i, p: (l, i, 0))],
        out_specs=pl.BlockSpec((None, tr, cols), lambda i, p: (p[0], i, 0)))
    return pl.pallas_call(body, grid_spec=grid_spec, name="own_shard_slab",
                          out_shape=jax.ShapeDtypeStruct((N_CHIPS, rows, cols), dtype))(place, w)


def _own_conv_slab(w, place):
    depth = w.shape[0]

    def body(p_ref, w_ref, o_ref):
        del p_ref
        o_ref[...] = w_ref[...]

    grid_spec = pltpu.PrefetchScalarGridSpec(
        num_scalar_prefetch=1, grid=(depth,),
        in_specs=[pl.BlockSpec((None, 3, LANES), lambda l, p: (l, 0, 0))],
        out_specs=pl.BlockSpec((None, None, 3, LANES), lambda l, p: (l, p[0], 0, 0)))
    return pl.pallas_call(body, grid_spec=grid_spec, name="own_conv_slab",
                          out_shape=jax.ShapeDtypeStruct((depth, N_CHIPS, 3, LANES), F32))(place, w)


def _add_halves(grad, got, place):
    s_n, rows, cols = grad.shape
    hr = rows // 2

    def body(p_ref, g_ref, r_ref, o_ref):
        del p_ref
        o_ref[...] = (g_ref[...] + r_ref[...]).astype(BF16)

    grid_spec = pltpu.PrefetchScalarGridSpec(
        num_scalar_prefetch=1, grid=(s_n,),
        in_specs=[pl.BlockSpec((None, hr, cols), lambda s, p: (s, p[1], 0)),
                  pl.BlockSpec((None, hr, cols), lambda s, p: (s, 0, 0))],
        out_specs=pl.BlockSpec((None, hr, cols), lambda s, p: (s, 0, 0)))
    return pl.pallas_call(body, grid_spec=grid_spec, out_shape=jax.ShapeDtypeStruct((s_n, hr, cols), BF16),
                          name="add_halves")(place, grad, got)


def _sum_partials(partial, got, place, acc, l):
    _, hr, cols = partial.shape

    def body(p_ref, mine_ref, got_ref, acc_ref, o_ref):
        del p_ref, acc_ref
        total = mine_ref[...].astype(F32)
        for k in range(3):
            total = total + got_ref[k].astype(F32)
        o_ref[...] = total

    grid_spec = pltpu.PrefetchScalarGridSpec(
        num_scalar_prefetch=1, grid=(1,),
        in_specs=[pl.BlockSpec((None, hr, cols), lambda i, p: (p[0], 0, 0)),
                  pl.BlockSpec((3, hr, cols), lambda i, p: (0, 0, 0)), ANY],
        out_specs=pl.BlockSpec((None, hr, cols), lambda i, p: (l, p[1], 0)))
    return pl.pallas_call(body, grid_spec=grid_spec, out_shape=jax.ShapeDtypeStruct(acc.shape, F32),
                          input_output_aliases={3: 0}, name="sum_partials")(place, partial, got, acc)


def _allreduce_small(vec, loss_row):
    rows = vec.shape[0]

    def body(v_ref, o_ref, slots, send_sems, recv_sems):
        x, y, c, _ = _place()
        me = 4 * x + 2 * y + c
        slots[me] = v_ref[...]
        copies = []
        for k in range(1, N_DEV):
            flip = lambda v, bit: 1 - v if bit else v
            peer = (flip(x, k & 4), flip(y, k & 2), flip(c, k & 1))
            copies.append(_remote(v_ref, slots.at[me], send_sems.at[k - 1], recv_sems.at[k - 1], peer))
        for cp in copies:
            cp.start()
        for k in range(1, N_DEV):
            flip = lambda v, bit: 1 - v if bit else v
            peer_id = 4 * flip(x, k & 4) + 2 * flip(y, k & 2) + flip(c, k & 1)
            _remote(v_ref, slots.at[peer_id], send_sems.at[k - 1], recv_sems.at[k - 1], (x, y, c)).wait_recv()
        for cp in copies:
            cp.wait_send()
        total = slots[0]
        for dev in range(1, N_DEV):
            total = total + slots[dev]
        o_ref[...] = total
        o_ref[loss_row:loss_row + 1, :] = jnp.broadcast_to(
            jnp.sum(total[loss_row:loss_row + 1, :], axis=-1, keepdims=True), (1, LANES))

    return pl.pallas_call(
        body, in_specs=[WHOLE_VMEM], out_specs=WHOLE_VMEM, out_shape=jax.ShapeDtypeStruct((rows, LANES), F32),
        scratch_shapes=[pltpu.VMEM((N_DEV, rows, LANES), F32), pltpu.SemaphoreType.DMA((N_DEV - 1,)),
                        pltpu.SemaphoreType.DMA((N_DEV - 1,))],
        name="allreduce_small")(vec)


def _adamw(w, g, m, v, *, tr, emit_grad=False):
    depth, rows, cols = w.shape
    assert rows % tr == 0
    c1 = float(np.float32(1.0 - ADAM_B1 ** ADAM_STEP))
    c2 = float(np.float32(1.0 - ADAM_B2 ** ADAM_STEP))

    def body(w_ref, g_ref, m_ref, v_ref, d_ref, mo_ref, vo_ref, *go_ref):
        g_t = g_ref[...]
        if emit_grad:
            go_ref[0][...] = g_t
        m_new = ADAM_B1 * m_ref[...] + (1.0 - ADAM_B1) * g_t
        v_new = ADAM_B2 * v_ref[...] + (1.0 - ADAM_B2) * (g_t * g_t)
        mo_ref[...] = m_new
        vo_ref[...] = v_new
        d_ref[...] = -ADAM_LR * ((m_new / c1) / (jnp.sqrt(v_new / c2) + ADAM_EPS) + ADAM_WD * w_ref[...])

    blk = pl.BlockSpec((None, tr, cols), lambda l, i: (l, i, 0))
    return pl.pallas_call(
        body, grid=(depth, rows // tr), in_specs=[blk] * 4, out_specs=[blk] * (4 if emit_grad else 3),
        out_shape=[jax.ShapeDtypeStruct(w.shape, F32)] * (4 if emit_grad else 3), name="adamw")(w, g, m, v)


def _local_step(x, positions, target, gains, exchange):
    t = x.shape[0]
    tm = 512
    inv_freq = ROPE_THETA ** (-jnp.arange(0, ROPE_DIM, 2, dtype=F32) / ROPE_DIM)
    lane = np.arange(LANES) % HEAD_DIM
    freq_row = jnp.where(lane < ROPE_DIM, inv_freq[lane % (ROPE_DIM // 2)], 0.0).astype(F32)[None, :]
    cos, sin = _rope_tables(positions.reshape(t, 1), freq_row)

    def hosted(tag, fn, *args, **kwargs):
        *out, got = fn(*args, comm=exchange.host(tag), **kwargs)
        if got is not None:
            exchange.hosted(tag, got)
        return out[0] if len(out) == 1 else out

    saved = []
    h1 = _norm_fwd(x, gains["pre_mix_norm"], 0, tm=tm)
    for l in range(DEPTH):
        proj = hosted(("fwd", l, "in_proj"), _in_proj, h1, exchange.weight("w_in", l), cos, sin, tm=1024)
        attn, lse = hosted(("fwd", l, "attn"), _attn_fwd, proj)
        conv_y = _conv_fwd(proj, exchange.weight("conv_w", l), l)
        merged = _merge_fwd(attn, conv_y, gains["attn_out_norm"], gains["conv_out_norm"], l, tm=tm)
        mix, x1, h2 = hosted(("fwd", l, "out_proj"), _mm_resnorm, merged, exchange.weight("w_out", l), x,
                             gains["post_mix_norm"], l, gains["pre_ffn_norm"], l, tm=tm, name="out_proj")
        g, u, act = hosted(("fwd", l, "gate_up"), _gate_up_swiglu, h2, exchange.weight("w_gate_up", l), tm=1024)
        nxt = (gains["pre_mix_norm"], l + 1) if l + 1 < DEPTH else (None, None)
        f, x2, h1_next = hosted(("fwd", l, "down"), _mm_resnorm, act, exchange.weight("w_down", l), x1,
                                gains["post_ffn_norm"], l, *nxt, tm=tm, name="down")
        saved.append(dict(x=x, h1=h1, proj=proj, attn=attn, lse=lse, conv_y=conv_y, merged=merged, mix=mix,
                          x1=x1, h2=h2, g=g, u=u, act=act, f=f))
        x, h1 = x2, h1_next

    dres, loss_lanes = _loss_fwd_bwd(x, target, tm=tm)

    g_gain = {k: [None] * DEPTH for k in gains}
    g_conv = [None] * DEPTH
    _, df, _, g_gain["post_ffn_norm"][DEPTH - 1], _ = _norm_bwd(
        dres, None, (saved[-1]["f"], gains["post_ffn_norm"], DEPTH - 1), tm=tm)
    for l in reversed(range(DEPTH)):
        sv = saved[l]
        w = {k: exchange.weight(k, l) for k in MATRIX_NAMES + ("conv_w",)}
        dg, du = _down_dx_swiglu_bwd(df, w["w_down"], sv["g"], sv["u"], tm=1024, tko=FFN // 2)
        g_down = _mm_tn(sv["act"], df, 1, tka=256, name="down_dw")
        dx1, dmix, g_gain["pre_ffn_norm"][l], g_gain["post_mix_norm"][l] = hosted(
            ("bwd", l, "gate_up_dx"), _gate_up_dx_norms, dg, du, w["w_gate_up"], dres,
            (sv["x1"], gains["pre_ffn_norm"], l), (sv["mix"], gains["post_mix_norm"], l), tm=tm)
        g_gate_up = _mm_tn(sv["h2"], dg, N_CHIPS // 2, tka=512, name="gate_up_dw",
                           into=lax.empty(w["w_gate_up"].shape, F32))
        g_gate_up = _mm_tn(sv["h2"], du, N_CHIPS // 2, tka=512, name="gate_up_dw", into=g_gate_up,
                           shard0=N_CHIPS // 2)
        exchange.grads(l, "ffn", dict(w_down=g_down.reshape(N_CHIPS, FFN // N_CHIPS, D_MODEL), w_gate_up=g_gate_up))
        d_merged = hosted(("bwd", l, "out_proj_dx"), _mm_nt, dmix, w["w_out"], tm=1024, tko=D_MODEL, name="out_proj_dx")
        g_out = _mm_tn(sv["merged"], dmix, 1, tka=512, name="out_proj_dw")
        d_attn, delta, d_conv_y, g_gain["attn_out_norm"][l], g_gain["conv_out_norm"][l] = hosted(
            ("bwd", l, "merge"), _merge_bwd,
            d_merged, sv["attn"], sv["conv_y"], gains["attn_out_norm"], gains["conv_out_norm"], l, tm=tm)
        d_attn3 = hosted(("bwd", l, "attn"), _attn_bwd, sv["proj"], cos, sin, d_attn, sv["lse"], delta)
        d_conv3, g_conv[l] = _conv_bwd(sv["proj"], w["conv_w"], l, d_conv_y)
        g_in = _in_proj_dw(sv["h1"], d_attn3, d_conv3, N_CHIPS, tka=512)
        exchange.grads(l, "mix", dict(w_out=g_out.reshape(N_CHIPS, D_MODEL // N_CHIPS, D_MODEL), w_in=g_in))
        if l > 0:
            dres, df, g_gain["pre_mix_norm"][l], g_gain["post_ffn_norm"][l - 1] = hosted(
                ("bwd", l, "in_proj_dx"), _in_proj_dx_norms, d_attn3, d_conv3, w["w_in"], dx1,
                (sv["x"], gains["pre_mix_norm"], l), (saved[l - 1]["f"], gains["post_ffn_norm"], l - 1), tm=tm)
        else:
            dh1 = hosted(("bwd", l, "in_proj_dx"), _in_proj_dx, d_attn3, d_conv3, w["w_in"], tm=1024)
            dres, _, g_gain["pre_mix_norm"][l], _ = hosted(
                ("bwd", l, "norm_low"), _norm_bwd, dx1, (dh1, sv["x"], gains["pre_mix_norm"], l), None, tm=tm)

    g_gain = {k: jnp.concatenate(v, axis=0) for k, v in g_gain.items()}
    return loss_lanes, dres, g_gain, jnp.stack(g_conv, axis=0)


class _Exchange:
    GATHER_HOSTS = {"in_proj": (("w_out", 0), ("w_down", 0)), "attn": (("w_gate_up", 0),), "gate_up": (("w_in", 1),)}

    @staticmethod
    def _reduce_hosts(group, l):
        if group == "ffn":
            return "merge", "attn", l
        if l > 0:
            return "in_proj_dx", "gate_up_dx", l - 1
        return "in_proj_dx", "norm_low", l

    def __init__(self, params, place):
        self.place = place
        self.slabs = {k: [_own_shard_slab(params[k], l, place, BF16) for l in range(DEPTH)] for k in MATRIX_NAMES}
        self.gathered = {k: [None] * DEPTH for k in MATRIX_NAMES}
        self.gathered["w_in"][0], self.conv_w = _run_comm(
            _gather_comm([self.slabs["w_in"][0]], _own_conv_slab(params["conv_w"], place)), "gather_first")
        self.full = {k: lax.empty(params[k].shape, F32) for k in MATRIX_NAMES}
        self.pending = {}
        self.raw = {}

    def weight(self, name, l):
        if name == "conv_w":
            return self.conv_w
        g = self.gathered[name][l]
        return g.reshape(1, g.shape[0] * g.shape[1], g.shape[2]) if name in ("w_out", "w_down") else g

    def host(self, tag):
        phase, l, kernel = tag
        if phase == "fwd":
            carried = [(name, l + ahead) for name, ahead in self.GATHER_HOSTS.get(kernel, ()) if l + ahead < DEPTH]
            return _gather_comm([self.slabs[name][layer] for name, layer in carried]) if carried else None
        if tag in self.pending:
            stage, _, _, arrays = self.pending[tag]
            return _halves_comm(arrays) if stage == "halves" else _partials_comm(arrays)
        return None

    def hosted(self, tag, results):
        phase, l, kernel = tag
        if phase == "fwd":
            carried = [(name, l + ahead) for name, ahead in self.GATHER_HOSTS[kernel] if l + ahead < DEPTH]
            for (name, layer), slab in zip(carried, results):
                self.gathered[name][layer] = slab
            return
        stage, gl, group, arrays = self.pending.pop(tag)
        names = list(self.raw[(gl, group)])
        if stage == "partials":
            self._finish_reduction(gl, names, arrays, results)
            return
        partials = [_add_halves(self.raw[(gl, group)][k], r, self.place) for k, r in zip(names, results)]
        _, ici_kernel, ici_layer = self._reduce_hosts(group, gl)
        self.pending[("bwd", ici_layer, ici_kernel)] = ("partials", gl, group, partials)

    def grads(self, l, group, grads):
        self.raw[(l, group)] = grads
        self.pending[("bwd", l, self._reduce_hosts(group, l)[0])] = ("halves", l, group, [grads[k] for k in grads])

    def _finish_reduction(self, l, names, partials, others):
        for k, p, q in zip(names, partials, others):
            self.full[k] = _sum_partials(p, q, self.place, self.full[k], l)
        shared = _run_comm(_share_comm([self.full[k] for k in names], l), "share_halves")
        for k, g in zip(names, shared):
            self.full[k] = g


def kernel(x, positions, pre_mix_norm, w_in, conv_w, attn_out_norm, conv_out_norm, w_out, post_mix_norm, pre_ffn_norm, w_gate_up, w_down, post_ffn_norm, loss_target, m_pre_mix_norm, m_w_in, m_conv_w, m_attn_out_norm, m_conv_out_norm, m_w_out, m_post_mix_norm, m_pre_ffn_norm, m_w_gate_up, m_w_down, m_post_ffn_norm, v_pre_mix_norm, v_w_in, v_conv_w, v_attn_out_norm, v_conv_out_norm, v_w_out, v_post_mix_norm, v_pre_ffn_norm, v_w_gate_up, v_w_down, v_post_ffn_norm):
    params = dict(pre_mix_norm=pre_mix_norm, w_in=w_in, conv_w=conv_w, attn_out_norm=attn_out_norm,
                  conv_out_norm=conv_out_norm, w_out=w_out, post_mix_norm=post_mix_norm, pre_ffn_norm=pre_ffn_norm,
                  w_gate_up=w_gate_up, w_down=w_down, post_ffn_norm=post_ffn_norm)
    mom1 = dict(pre_mix_norm=m_pre_mix_norm, w_in=m_w_in, conv_w=m_conv_w, attn_out_norm=m_attn_out_norm,
                conv_out_norm=m_conv_out_norm, w_out=m_w_out, post_mix_norm=m_post_mix_norm,
                pre_ffn_norm=m_pre_ffn_norm, w_gate_up=m_w_gate_up, w_down=m_w_down, post_ffn_norm=m_post_ffn_norm)
    mom2 = dict(pre_mix_norm=v_pre_mix_norm, w_in=v_w_in, conv_w=v_conv_w, attn_out_norm=v_attn_out_norm,
                conv_out_norm=v_conv_out_norm, w_out=v_w_out, post_mix_norm=v_post_mix_norm,
                pre_ffn_norm=v_pre_ffn_norm, w_gate_up=v_w_gate_up, w_down=v_w_down, post_ffn_norm=v_post_ffn_norm)
    xi, yi, ci = lax.axis_index("x"), lax.axis_index("y"), lax.axis_index("c")
    place = jnp.stack([2 * xi + yi, ci]).astype(jnp.int32)

    exchange = _Exchange(params, place)
    gains = {k: params[k][:, None, :] for k in GAIN_NAMES}
    loss_lanes, grad_x, g_gain, g_conv = _local_step(x[0], positions[0], loss_target[0], gains, exchange)
    grad = dict(exchange.full)

    small = [g_gain[k].reshape(-1) for k in GAIN_NAMES] + [g_conv.reshape(-1), loss_lanes.reshape(-1)]
    sizes = [int(s.shape[0]) for s in small]
    flat = jnp.concatenate(small)
    loss_row = (sum(sizes) - LANES) // LANES
    rows = -(-flat.shape[0] // (8 * LANES)) * 8
    flat = jnp.pad(flat, (0, rows * LANES - flat.shape[0])).reshape(rows, LANES)
    total = _allreduce_small(flat, loss_row).reshape(-1)
    offsets = np.cumsum([0] + sizes)
    for i, k in enumerate(GAIN_NAMES):
        grad[k] = total[offsets[i]:offsets[i + 1]].reshape(params[k].shape)
    conv_all = total[offsets[6]:offsets[7]].reshape(DEPTH, N_CHIPS, 3, LANES)
    grad["conv_w"] = lax.dynamic_index_in_dim(conv_all, 2 * xi + yi, axis=1, keepdims=False)
    loss = total[offsets[7]]

    delta, new_m, new_v = {}, {}, {}
    for k in WEIGHT_ORDER:
        shape = params[k].shape
        if k in MATRIX_NAMES:
            tr = {1024: 512, 704: 352, 256: 256}[shape[1]]
            delta[k], new_m[k], new_v[k], grad[k] = _adamw(params[k], grad[k], mom1[k], mom2[k], tr=tr, emit_grad=True)
        else:
            as3 = (lambda a: a) if len(shape) == 3 else (lambda a: a[None])
            d, m, v = _adamw(as3(params[k]), as3(grad[k]), as3(mom1[k]), as3(mom2[k]), tr=as3(params[k]).shape[1])
            delta[k], new_m[k], new_v[k] = d.reshape(shape), m.reshape(shape), v.reshape(shape)

    return (loss, grad_x[None], *[grad[k] for k in WEIGHT_ORDER], *[delta[k] for k in WEIGHT_ORDER],
            *[new_m[k] for k in WEIGHT_ORDER], *[new_v[k] for k in WEIGHT_ORDER])
```

```python
import functools
from typing import Callable, NamedTuple

import numpy as np
import jax
import jax.numpy as jnp
from jax import lax
from jax.experimental import pallas as pl
from jax.experimental.pallas import tpu as pltpu

F32 = jnp.float32
BF16 = jnp.bfloat16
MESH = pl.DeviceIdType.MESH

D_MODEL = 1024
ATTN_W = 512
CONV_W = 512
HEAD_DIM = 64
ROPE_DIM = 16
ROPE_THETA = 500000.0
FFN = 2816
DEPTH = 4
RMS_EPS = 1e-6
NEG_INF = -1e30
N_CHIPS = 4
N_DEV = 8
LANES = 128
BF16_ROWS = 16
DILATIONS = (1, 4, 16)
BAND = 64
TQ = 128
WIN = TQ + 2 * BAND
SCALE = HEAD_DIM ** -0.5

ADAM_LR = 0.001
ADAM_B1 = 0.9
ADAM_B2 = 0.999
ADAM_EPS = 1e-08
ADAM_WD = 0.01
ADAM_STEP = 10

GAIN_NAMES = ("pre_mix_norm", "attn_out_norm", "conv_out_norm", "post_mix_norm", "pre_ffn_norm", "post_ffn_norm")
MATRIX_NAMES = ("w_in", "w_out", "w_gate_up", "w_down")
WEIGHT_ORDER = ("pre_mix_norm", "w_in", "conv_w", "attn_out_norm", "conv_out_norm", "w_out", "post_mix_norm",
                "pre_ffn_norm", "w_gate_up", "w_down", "post_ffn_norm")

ANY = pl.BlockSpec(memory_space=pl.ANY)
WHOLE_VMEM = pl.BlockSpec(memory_space=pltpu.VMEM)
LANE_CONTRACT = (((1,), (1,)), ((), ()))
ROW_CONTRACT = (((0,), (0,)), ((), ()))
CHUNK = 256


def _const_spec(block, index):
    return pl.BlockSpec(block, lambda *_: index)


def _gain_spec(g3, l):
    return _const_spec((None, 1, g3.shape[-1]), (l, 0, 0))


class _Comm(NamedTuple):
    ins: tuple
    inouts: tuple
    out_shapes: tuple
    n_sems: int
    start: Callable
    finish: Callable


def _place():
    x, y, c = lax.axis_index("x"), lax.axis_index("y"), lax.axis_index("c")
    other_chips = [(1 - x, y), (x, 1 - y), (1 - x, 1 - y)]
    return x, y, c, other_chips


def _remote(src, dst, send_sem, recv_sem, to):
    return pltpu.make_async_remote_copy(src_ref=src, dst_ref=dst, send_sem=send_sem, recv_sem=recv_sem,
                                        device_id=to, device_id_type=MESH)


def _call(body, operands, *, name, grid, in_specs, out_specs, out_shape, scratch_shapes=(), comm=None):
    in_specs, out_specs, out_shape = list(in_specs), list(out_specs), list(out_shape)
    scratch_shapes = list(scratch_shapes)
    if comm is None:
        out = pl.pallas_call(body, grid=grid, in_specs=in_specs, out_specs=out_specs, out_shape=out_shape,
                             scratch_shapes=scratch_shapes, name=name)(*operands)
        return list(out), None
    n_in, n_out, n_scr = len(in_specs), len(out_shape), len(scratch_shapes)
    n_ci, n_cio, n_co = len(comm.ins), len(comm.inouts), len(comm.out_shapes)

    def hosted(*refs):
        refs = list(refs)
        ins, c_ins = refs[:n_in], refs[n_in:n_in + n_ci]
        base = n_in + n_ci + n_cio
        outs = refs[base:base + n_out]
        c_io = refs[base + n_out:base + n_out + n_cio]
        c_out = refs[base + n_out + n_cio:base + n_out + n_cio + n_co]
        scr = refs[base + n_out + n_cio + n_co:]
        send_sems, recv_sems = scr[n_scr], scr[n_scr + 1]
        if grid:
            first = functools.reduce(jnp.logical_and, [pl.program_id(a) == 0 for a in range(len(grid))])
            last = functools.reduce(jnp.logical_and, [pl.program_id(a) == grid[a] - 1 for a in range(len(grid))])
            pl.when(first)(lambda: comm.start(c_ins, c_io, c_out, send_sems, recv_sems))
            body(*ins, *outs, *scr[:n_scr])
            pl.when(last)(lambda: comm.finish(c_ins, c_io, c_out, send_sems, recv_sems))
        else:
            comm.start(c_ins, c_io, c_out, send_sems, recv_sems)
            body(*ins, *outs, *scr[:n_scr])
            comm.finish(c_ins, c_io, c_out, send_sems, recv_sems)

    res = pl.pallas_call(
        hosted, grid=grid, in_specs=in_specs + [ANY] * (n_ci + n_cio), out_specs=out_specs + [ANY] * (n_cio + n_co),
        out_shape=out_shape + [jax.ShapeDtypeStruct(a.shape, a.dtype) for a in comm.inouts] + list(comm.out_shapes),
        input_output_aliases={n_in + n_ci + i: n_out + i for i in range(n_cio)},
        scratch_shapes=scratch_shapes + [pltpu.SemaphoreType.DMA((comm.n_sems,))] * 2,
        name=name)(*operands, *comm.ins, *comm.inouts)
    return list(res[:n_out]), list(res[n_out:])


def _run_comm(comm, name):
    return _call(lambda: None, [], name=name, grid=(), in_specs=[], out_specs=[], out_shape=[], comm=comm)[1]


def _row_half(ref, lead, core, rows, align):
    hr = rows // 2
    return ref.at[(*lead, pl.ds(pl.multiple_of(core * hr, align), hr), slice(None))]


def _gather_comm(slabs, conv_slab=None):
    n = len(slabs)
    n_conv = 0 if conv_slab is None else 3

    def direct(ios, send, recv):
        x, y, c, chips = _place()
        copies = []
        for a in range(n):
            own = _row_half(ios[a], (2 * x + y,), c, slabs[a].shape[1], BF16_ROWS)
            copies += [_remote(own, own, send.at[a * 3 + j], recv.at[a * 3 + j], (*chip, c))
                       for j, chip in enumerate(chips)]
        if conv_slab is not None:
            own = ios[n].at[:, 2 * x + y]
            copies += [_remote(own, own, send.at[6 * n + j], recv.at[6 * n + j], (*chip, c))
                       for j, chip in enumerate(chips)]
        return copies

    def start(ins, ios, outs, send, recv):
        for cp in direct(ios, send, recv):
            cp.start()

    def finish(ins, ios, outs, send, recv):
        x, y, c, chips = _place()
        sibling = (x, y, 1 - c)
        passed = []
        for a in range(n):
            for j, chip in enumerate(chips):
                landed = _row_half(ios[a], (2 * chip[0] + chip[1],), c, slabs[a].shape[1], BF16_ROWS)
                _remote(landed, landed, send.at[a * 3 + j], recv.at[a * 3 + j], (*chip, c)).wait_recv()
                fwd = _remote(landed, landed, send.at[3 * n + a * 3 + j], recv.at[3 * n + a * 3 + j], sibling)
                fwd.start()
                passed.append(fwd)
        if conv_slab is not None:
            for j, chip in enumerate(chips):
                landed = ios[n].at[:, 2 * chip[0] + chip[1]]
                _remote(landed, landed, send.at[6 * n + j], recv.at[6 * n + j], (*chip, c)).wait_recv()
        for a in range(n):
            for j, chip in enumerate(chips):
                landed = _row_half(ios[a], (2 * chip[0] + chip[1],), 1 - c, slabs[a].shape[1], BF16_ROWS)
                _remote(landed, landed, send.at[3 * n + a * 3 + j], recv.at[3 * n + a * 3 + j], sibling).wait_recv()
        for cp in direct(ios, send, recv) + passed:
            cp.wait_send()

    inouts = tuple(slabs) + (() if conv_slab is None else (conv_slab,))
    return _Comm((), inouts, (), 6 * n + n_conv, start, finish)


def _halves_comm(grads):
    n = len(grads)

    def copies(ins, outs, send, recv):
        x, y, c, _ = _place()
        return [_remote(_row_half(ins[a], (slice(None),), 1 - c, grads[a].shape[1], 8), outs[a],
                        send.at[a], recv.at[a], (x, y, 1 - c)) for a in range(n)]

    def start(ins, ios, outs, send, recv):
        for cp in copies(ins, outs, send, recv):
            cp.start()

    def finish(ins, ios, outs, send, recv):
        for cp in copies(ins, outs, send, recv):
            cp.wait()

    out_shapes = tuple(jax.ShapeDtypeStruct((g.shape[0], g.shape[1] // 2, g.shape[2]), F32) for g in grads)
    return _Comm(tuple(grads), (), out_shapes, n, start, finish)


def _partials_comm(partials):
    n = len(partials)

    def copies(ins, outs, send, recv):
        x, y, c, chips = _place()
        return [_remote(ins[a].at[2 * chip[0] + chip[1]], outs[a].at[k], send.at[a * 3 + k], recv.at[a * 3 + k],
                        (*chip, c)) for a in range(n) for k, chip in enumerate(chips)]

    def start(ins, ios, outs, send, recv):
        for cp in copies(ins, outs, send, recv):
            cp.start()

    def finish(ins, ios, outs, send, recv):
        for cp in copies(ins, outs, send, recv):
            cp.wait()

    out_shapes = tuple(jax.ShapeDtypeStruct((3,) + p.shape[1:], BF16) for p in partials)
    return _Comm(tuple(partials), (), out_shapes, 3 * n, start, finish)


def _share_comm(grads, l):
    n = len(grads)

    def start(ins, ios, outs, send, recv):
        x, y, c, _ = _place()
        for a in range(n):
            mine = _row_half(ios[a], (l,), c, grads[a].shape[1], 8)
            _remote(mine, mine, send.at[a], recv.at[a], (x, y, 1 - c)).start()

    def finish(ins, ios, outs, send, recv):
        x, y, c, _ = _place()
        for a in range(n):
            theirs = _row_half(ios[a], (l,), 1 - c, grads[a].shape[1], 8)
            _remote(theirs, theirs, send.at[a], recv.at[a], (x, y, 1 - c)).wait()

    return _Comm((), tuple(grads), (), n, start, finish)


def _rms_fwd(x, g):
    r = lax.rsqrt(jnp.mean(x * x, axis=-1, keepdims=True) + RMS_EPS)
    return (x * r) * g


def _rms_bwd(x, g, dy):
    r = lax.rsqrt(jnp.mean(x * x, axis=-1, keepdims=True) + RMS_EPS)
    xh = x * r
    u = dy * g
    dx = r * (u - xh * jnp.mean(xh * u, axis=-1, keepdims=True))
    return dx, jnp.sum(dy * xh, axis=0, keepdims=True)


def _accumulate(ref, value, first):
    @pl.when(first)
    def _():
        ref[...] = value

    @pl.when(jnp.logical_not(first))
    def _():
        ref[...] += value


def _rope_coeffs(cos, sin):
    m = lax.broadcasted_iota(jnp.int32, cos.shape, 1) % HEAD_DIM
    a = jnp.where(m < ROPE_DIM, cos, 1.0)
    b = jnp.where(m < ROPE_DIM // 2, -sin, 0.0)
    c = jnp.where((m >= ROPE_DIM // 2) & (m < ROPE_DIM), sin, 0.0)
    return a, b, c


def _rope_apply(t, cos, sin):
    a, b, c = _rope_coeffs(cos, sin)
    n = t.shape[1]
    return a * t + b * pltpu.roll(t, n - ROPE_DIM // 2, 1) + c * pltpu.roll(t, ROPE_DIM // 2, 1)


def _rope_transpose(dt, cos, sin):
    a, b, c = _rope_coeffs(cos, sin)
    n = dt.shape[1]
    return a * dt + pltpu.roll(b * dt, ROPE_DIM // 2, 1) + pltpu.roll(c * dt, n - ROPE_DIM // 2, 1)


def _in_proj(h, w, cos, sin, *, tm, comm=None):
    t, k = h.shape
    s_n, _, n = w.shape
    assert t % tm == 0 and n % LANES == 0

    def body(h_ref, w_ref, cos_ref, sin_ref, o_ref):
        o_ref[...] = jnp.dot(h_ref[...], w_ref[...], preferred_element_type=F32)
        for s in range(s_n):
            rotary_cols = min(max(2 * ATTN_W - s * n, 0), n)
            if rotary_cols:
                @pl.when(pl.program_id(1) == s)
                def _():
                    for c0 in range(0, rotary_cols, LANES):
                        cols = slice(c0, c0 + LANES)
                        o_ref[:, cols] = _rope_apply(o_ref[:, cols], cos_ref[...], sin_ref[...])

    lane_tile = pl.BlockSpec((tm, LANES), lambda i, s: (i, 0))
    out, got = _call(
        body, [h, w, cos, sin], name="in_proj", grid=(t // tm, s_n),
        in_specs=[pl.BlockSpec((tm, k), lambda i, s: (i, 0)), pl.BlockSpec((None, k, n), lambda i, s: (s, 0, 0)),
                  lane_tile, lane_tile],
        out_specs=[pl.BlockSpec((tm, n), lambda i, s: (i, s))],
        out_shape=[jax.ShapeDtypeStruct((t, s_n * n), F32)], comm=comm)
    return out[0], got


def _mm_nt(a, w, *, tm, tko, name, comm=None):
    t, sn = a.shape
    s_n, ko, n = w.shape
    assert sn == s_n * n and t % tm == 0 and ko % tko == 0

    def body(a_ref, w_ref, o_ref):
        acc = lax.dot_general(a_ref[...], w_ref[...], (((1,), (1,)), ((), ())), preferred_element_type=F32)
        if s_n == 1:
            o_ref[...] = acc
        else:
            _accumulate(o_ref, acc, pl.program_id(2) == 0)

    out, got = _call(
        body, [a, w], name=name, grid=(t // tm, ko // tko, s_n),
        in_specs=[pl.BlockSpec((tm, n), lambda i, j, s: (i, s)),
                  pl.BlockSpec((None, tko, n), lambda i, j, s: (s, j, 0))],
        out_specs=[pl.BlockSpec((tm, tko), lambda i, j, s: (i, j))],
        out_shape=[jax.ShapeDtypeStruct((t, ko), F32)], comm=comm)
    return out[0], got


def _dx_through_norms(operands, make_specs, partial, n_steps, dres, pre, post, *, tm, name, comm=None):
    t, d_model = dres.shape
    (x, gx3, lx), (y, gy3, ly) = pre, post
    n_op = len(operands)
    nt = t // tm
    rps = tm // n_steps
    behind = pl.BlockSpec((tm, d_model), lambda i, s: (jnp.maximum(i - 1, 0), 0))
    gsum = _const_spec((1, d_model), (0, 0))

    def body(*refs):
        op_refs = refs[:n_op]
        d_ref, x_ref, gx_ref, y_ref, gy_ref, dn_ref, dgx_ref, dy_ref, dgy_ref, acc = refs[n_op:]
        i, s = pl.program_id(0), pl.program_id(1)
        cur = i % 2

        def norms_of_previous_tile():
            has_previous = i >= 1
            rows = pl.ds(pl.multiple_of(s * rps, rps), rps)
            dh = jnp.where(has_previous, acc[1 - cur, rows, :], 0.0)
            dx, dgx = _rms_bwd(x_ref[rows, :], gx_ref[...], dh)
            d_new = d_ref[rows, :] + dx
            dn_ref[rows, :] = d_new
            dy, dgy = _rms_bwd(y_ref[rows, :], gy_ref[...], d_new)
            dy_ref[rows, :] = dy.astype(BF16)
            dgx_ref[...] += jnp.where(has_previous, dgx, 0.0)
            dgy_ref[...] += jnp.where(has_previous, dgy, 0.0)

        @pl.when(jnp.logical_and(i == 0, s == 0))
        def _():
            acc[...] = jnp.zeros(acc.shape, F32)
            dgx_ref[...] = jnp.zeros(dgx_ref.shape, F32)
            dgy_ref[...] = jnp.zeros(dgy_ref.shape, F32)

        @pl.when(i < nt)
        def _():
            norms_of_previous_tile()
            acc[cur] = partial(op_refs) + jnp.where(s == 0, 0.0, acc[cur])

        @pl.when(i == nt)
        def _():
            norms_of_previous_tile()

    out, got = _call(
        body, list(operands) + [dres, x, gx3, y, gy3], name=name, grid=(nt + 1, n_steps),
        in_specs=list(make_specs(lambda i: jnp.minimum(i, nt - 1)))
        + [behind, behind, _gain_spec(gx3, lx), behind, _gain_spec(gy3, ly)],
        out_specs=[behind, gsum, behind, gsum],
        out_shape=[jax.ShapeDtypeStruct((t, d_model), F32), jax.ShapeDtypeStruct((1, d_model), F32),
                   jax.ShapeDtypeStruct((t, d_model), BF16), jax.ShapeDtypeStruct((1, d_model), F32)],
        scratch_shapes=[pltpu.VMEM((2, tm, d_model), F32)], comm=comm)
    return out[0], out[2], out[1], out[3], got


def _gate_up_dx_norms(dg, du, w, dres, pre, post, *, tm, comm=None):
    s_n, ko, n = w.shape
    half = s_n // 2

    def partial(refs):
        a0_ref, a1_ref, w0_ref, w1_ref = refs
        return (lax.dot_general(a0_ref[...], w0_ref[...], LANE_CONTRACT, preferred_element_type=F32)
                + lax.dot_general(a1_ref[...], w1_ref[...], LANE_CONTRACT, preferred_element_type=F32))

    def make_specs(row):
        a_spec = pl.BlockSpec((tm, n), lambda i, s: (row(i), s))
        return [a_spec, a_spec, pl.BlockSpec((None, ko, n), lambda i, s: (s, 0, 0)),
                pl.BlockSpec((None, ko, n), lambda i, s: (half + s, 0, 0))]

    return _dx_through_norms([dg, du, w, w], make_specs, partial, half, dres, pre, post, tm=tm, name="gate_up_dx",
                             comm=comm)


def _in_proj_dx_norms(d_attn3, d_conv3, w, dres, pre, post, *, tm, comm=None):
    s_n, ko, n = w.shape
    half, per = s_n // 2, n // CHUNK

    def partial(refs):
        a_refs, b_refs, wa_ref, wb_ref = refs[:per], refs[per:2 * per], refs[2 * per], refs[2 * per + 1]
        acc = jnp.zeros((tm, ko), F32)
        for r in range(per):
            cols = slice(r * CHUNK, (r + 1) * CHUNK)
            acc = acc + lax.dot_general(a_refs[r][...], wa_ref[:, cols], LANE_CONTRACT, preferred_element_type=F32)
            acc = acc + lax.dot_general(b_refs[r][...], wb_ref[:, cols], LANE_CONTRACT, preferred_element_type=F32)
        return acc

    def make_specs(row):
        piece = lambda r: pl.BlockSpec((None, tm, CHUNK), lambda i, s: ((per * s + r) // 2, row(i), (per * s + r) % 2))
        return ([piece(r) for r in range(per)] * 2
                + [pl.BlockSpec((None, ko, n), lambda i, s: (s, 0, 0)),
                   pl.BlockSpec((None, ko, n), lambda i, s: (half + s, 0, 0))])

    return _dx_through_norms([d_attn3] * per + [d_conv3] * per + [w, w], make_specs, partial, half, dres, pre, post,
                             tm=tm, name="in_proj_dx", comm=comm)


def _mm_tn(a, b, s_n, *, tka, name, into=None, shard0=0):
    t, ka = a.shape
    n = b.shape[1] // s_n
    assert b.shape[0] == t and ka % tka == 0

    def body(a_ref, b_ref, *rest):
        rest[-1][...] = lax.dot_general(a_ref[...], b_ref[...], ROW_CONTRACT, preferred_element_type=F32)

    operands, in_specs, aliases = [a, b], [pl.BlockSpec((t, tka), lambda i, s: (0, i)),
                                           pl.BlockSpec((t, n), lambda i, s: (0, s))], {}
    out_shape = jax.ShapeDtypeStruct((s_n, ka, n), F32)
    if into is not None:
        operands, in_specs, aliases = operands + [into], in_specs + [ANY], {2: 0}
        out_shape = jax.ShapeDtypeStruct(into.shape, F32)
    return pl.pallas_call(
        body, grid=(ka // tka, s_n), in_specs=in_specs,
        out_specs=pl.BlockSpec((None, tka, n), lambda i, s: (shard0 + s, i, 0)),
        out_shape=out_shape, input_output_aliases=aliases, name=name)(*operands)


def _gate_up_swiglu(h, w, *, tm, comm=None):
    t, k = h.shape
    s_n, _, n = w.shape
    half = s_n // 2

    def body(h_ref, wg_ref, wu_ref, g_ref, u_ref, a_ref):
        g = jnp.dot(h_ref[...], wg_ref[...], preferred_element_type=F32)
        u = jnp.dot(h_ref[...], wu_ref[...], preferred_element_type=F32)
        g_ref[...] = g.astype(BF16)
        u_ref[...] = u.astype(BF16)
        a_ref[...] = (g * jax.nn.sigmoid(g) * u).astype(BF16)

    col = pl.BlockSpec((tm, n), lambda i, j: (i, j))
    out, got = _call(
        body, [h, w, w], name="gate_up", grid=(t // tm, half),
        in_specs=[pl.BlockSpec((tm, k), lambda i, j: (i, 0)), pl.BlockSpec((None, k, n), lambda i, j: (j, 0, 0)),
                  pl.BlockSpec((None, k, n), lambda i, j: (half + j, 0, 0))],
        out_specs=[col, col, col],
        out_shape=[jax.ShapeDtypeStruct((t, half * n), BF16)] * 3, comm=comm)
    return out[0], out[1], out[2], got


def _down_dx_swiglu_bwd(df, w, g, u, *, tm, tko):
    t, k = df.shape
    _, ko, _ = w.shape
    assert t % tm == 0 and ko % tko == 0

    def body(df_ref, w_ref, g_ref, u_ref, dg_ref, du_ref):
        d = lax.dot_general(df_ref[...], w_ref[...], LANE_CONTRACT, preferred_element_type=F32)
        gg = g_ref[...].astype(F32)
        sig = jax.nn.sigmoid(gg)
        dg_ref[...] = (d * u_ref[...].astype(F32) * (sig * (1.0 + gg * (1.0 - sig)))).astype(BF16)
        du_ref[...] = (d * (gg * sig)).astype(BF16)

    col = pl.BlockSpec((tm, tko), lambda i, j: (i, j))
    return pl.pallas_call(
        body, grid=(t // tm, ko // tko),
        in_specs=[pl.BlockSpec((tm, k), lambda i, j: (i, 0)), pl.BlockSpec((None, tko, k), lambda i, j: (0, j, 0)),
                  col, col],
        out_specs=[col, col], out_shape=[jax.ShapeDtypeStruct((t, ko), BF16)] * 2, name="down_dx")(df, w, g, u)


def _in_proj_dx(d_attn3, d_conv3, w, *, tm, comm=None):
    _, t, _ = d_attn3.shape
    s_n, ko, n = w.shape
    half, per = s_n // 2, n // CHUNK
    assert t % tm == 0

    def body(*refs):
        a_refs, b_refs, wa_ref, wb_ref, o_ref = refs[:per], refs[per:2 * per], refs[2 * per], refs[2 * per + 1], refs[-1]
        acc = jnp.zeros(o_ref.shape, F32)
        for r in range(per):
            cols = slice(r * CHUNK, (r + 1) * CHUNK)
            acc = acc + lax.dot_general(a_refs[r][...], wa_ref[:, cols], LANE_CONTRACT, preferred_element_type=F32)
            acc = acc + lax.dot_general(b_refs[r][...], wb_ref[:, cols], LANE_CONTRACT, preferred_element_type=F32)
        _accumulate(o_ref, acc, pl.program_id(1) == 0)

    piece = lambda r: pl.BlockSpec((None, tm, CHUNK), lambda i, s: ((per * s + r) // 2, i, (per * s + r) % 2))
    out, got = _call(
        body, [d_attn3] * per + [d_conv3] * per + [w, w], name="in_proj_dx", grid=(t // tm, half),
        in_specs=[piece(r) for r in range(per)] * 2
        + [pl.BlockSpec((None, ko, n), lambda i, s: (s, 0, 0)), pl.BlockSpec((None, ko, n), lambda i, s: (half + s, 0, 0))],
        out_specs=[pl.BlockSpec((tm, ko), lambda i, s: (i, 0))],
        out_shape=[jax.ShapeDtypeStruct((t, ko), F32)], comm=comm)
    return out[0], got


def _in_proj_dw(h, d_attn3, d_conv3, s_n, *, tka):
    t, ka = h.shape
    half = s_n // 2
    n = 3 * d_attn3.shape[2] // half
    per = n // CHUNK
    assert ka % tka == 0

    def body(*refs):
        h_ref, o_ref = refs[0], refs[-1]
        for side in range(2):
            for r in range(per):
                o_ref[side, :, r * CHUNK:(r + 1) * CHUNK] = lax.dot_general(
                    h_ref[...], refs[1 + side * per + r][...], ROW_CONTRACT, preferred_element_type=F32)

    piece = lambda r: pl.BlockSpec((None, t, CHUNK), lambda i, s: ((per * s + r) // 2, 0, (per * s + r) % 2))
    out = pl.pallas_call(
        body, grid=(ka // tka, half),
        in_specs=[pl.BlockSpec((t, tka), lambda i, s: (0, i))] + [piece(r) for r in range(per)] * 2,
        out_specs=pl.BlockSpec((2, None, tka, n), lambda i, s: (0, s, i, 0)),
        out_shape=jax.ShapeDtypeStruct((2, half, ka, n), F32), name="in_proj_dw")(h, *[d_attn3] * per, *[d_conv3] * per)
    return out.reshape(s_n, ka, n)


def _rope_tables(positions_col, inv_freq_row):
    t = positions_col.shape[0]

    def body(pos_ref, f_ref, cos_ref, sin_ref):
        ang = pos_ref[...].astype(F32) * f_ref[...]
        cos_ref[...] = jnp.cos(ang)
        sin_ref[...] = jnp.sin(ang)

    return pl.pallas_call(
        body, out_shape=[jax.ShapeDtypeStruct((t, LANES), F32)] * 2, name="rope_tables")(positions_col, inv_freq_row)


def _norm_fwd(x, g3, l, *, tm):
    t, w = x.shape

    def body(x_ref, g_ref, h_ref):
        h_ref[...] = _rms_fwd(x_ref[...], g_ref[...]).astype(BF16)

    return pl.pallas_call(
        body, grid=(t // tm,),
        in_specs=[pl.BlockSpec((tm, w), lambda i: (i, 0)), _gain_spec(g3, l)],
        out_specs=pl.BlockSpec((tm, w), lambda i: (i, 0)),
        out_shape=jax.ShapeDtypeStruct((t, w), BF16), name="norm_fwd")(x, g3)


def _mm_resnorm(a, w, x, g_post3, l_post, g_next3, l_next, *, tm, name, comm=None):
    t, k = a.shape
    _, _, n = w.shape
    with_next = g_next3 is not None
    row = pl.BlockSpec((tm, n), lambda i: (i, 0))

    def body(a_ref, w_ref, x_ref, gp_ref, *rest):
        y = jnp.dot(a_ref[...], w_ref[...], preferred_element_type=F32)
        x_new = x_ref[...] + _rms_fwd(y, gp_ref[...])
        if with_next:
            gn_ref, y_ref, xo_ref, h_ref = rest
            h_ref[...] = _rms_fwd(x_new, gn_ref[...]).astype(BF16)
        else:
            y_ref, xo_ref = rest
        y_ref[...] = y
        xo_ref[...] = x_new

    ins = [a, w, x, g_post3] + ([g_next3] if with_next else [])
    in_specs = ([pl.BlockSpec((tm, k), lambda i: (i, 0)), _const_spec((None, k, n), (0, 0, 0)), row,
                 _gain_spec(g_post3, l_post)] + ([_gain_spec(g_next3, l_next)] if with_next else []))
    out_shape = [jax.ShapeDtypeStruct((t, n), F32)] * 2 + ([jax.ShapeDtypeStruct((t, n), BF16)] if with_next else [])
    out, got = _call(body, ins, name=name, grid=(t // tm,), in_specs=in_specs, out_specs=[row] * len(out_shape),
                     out_shape=out_shape, comm=comm)
    return out[0], out[1], (out[2] if with_next else None), got


def _conv_fwd(proj, conv_w, l):
    t = proj.shape[0]
    col0 = 3 * ATTN_W // LANES

    def body(u_ref, gb_ref, gc_ref, w_ref, y_ref):
        c = gc_ref[...] * u_ref[...]
        row = lax.broadcasted_iota(jnp.int32, c.shape, 0)
        c_prev = jnp.where(row == 0, 0.0, pltpu.roll(c, 1, 0))
        c_next = jnp.where(row == t - 1, 0.0, pltpu.roll(c, t - 1, 0))
        w = w_ref[...]
        y_ref[...] = gb_ref[...] * (w[0:1] * c_prev + w[1:2] * c + w[2:3] * c_next)

    nj = CONV_W // LANES
    cols = lambda base: pl.BlockSpec((t, LANES), lambda j: (0, base + j))
    return pl.pallas_call(
        body, grid=(nj,),
        in_specs=[cols(col0), cols(col0 + nj), cols(col0 + 2 * nj),
                  pl.BlockSpec((None, None, 3, LANES), lambda j: (l, j, 0, 0))],
        out_specs=pl.BlockSpec((t, LANES), lambda j: (0, j)),
        out_shape=jax.ShapeDtypeStruct((t, CONV_W), F32), name="conv_fwd")(proj, proj, proj, conv_w)


def _merge_fwd(attn, conv_y, ga3, gc3, l, *, tm):
    t = attn.shape[0]
    row = pl.BlockSpec((tm, ATTN_W), lambda i: (i, 0))

    def body(a_ref, c_ref, ga_ref, gc_ref, m_ref):
        m_ref[:, :ATTN_W] = _rms_fwd(a_ref[...], ga_ref[...]).astype(BF16)
        m_ref[:, ATTN_W:] = _rms_fwd(c_ref[...], gc_ref[...]).astype(BF16)

    return pl.pallas_call(
        body, grid=(t // tm,),
        in_specs=[row, row, _gain_spec(ga3, l), _gain_spec(gc3, l)],
        out_specs=pl.BlockSpec((tm, D_MODEL), lambda i: (i, 0)),
        out_shape=jax.ShapeDtypeStruct((t, D_MODEL), BF16), name="merge_fwd")(attn, conv_y, ga3, gc3)


def _loss_fwd_bwd(y, target, *, tm):
    t, w = y.shape
    row = pl.BlockSpec((tm, w), lambda i: (i, 0))

    def body(y_ref, t_ref, dy_ref, loss_ref):
        e = y_ref[...] - t_ref[...]
        dy_ref[...] = e * (1.0 / w)
        sq = jnp.sum(e * e, axis=0, keepdims=True) * (0.5 / w)
        part = sq[:, :LANES]
        for j in range(1, w // LANES):
            part = part + sq[:, j * LANES:(j + 1) * LANES]
        _accumulate(loss_ref, part, pl.program_id(0) == 0)

    return pl.pallas_call(
        body, grid=(t // tm,), in_specs=[row, row],
        out_specs=[row, _const_spec((1, LANES), (0, 0))],
        out_shape=[jax.ShapeDtypeStruct((t, w), F32), jax.ShapeDtypeStruct((1, LANES), F32)], name="loss")(y, target)


def _tile_rows(t, nt, lb, d):
    r = t // nt
    q0 = (t % nt) * TQ
    m0 = jnp.clip(q0 - BAND, 0, lb - WIN)
    if d == 1:
        return pl.ds(pl.multiple_of(q0, TQ), TQ), pl.ds(pl.multiple_of(m0, BAND), WIN), m0 - q0
    return pl.ds(r + d * q0, TQ, stride=d), pl.ds(r + d * m0, WIN, stride=d), m0 - q0


def _for_row_chunks(t, fn, chunk=512):
    def step(i, carry):
        fn(pl.ds(pl.multiple_of(i * chunk, chunk), chunk))
        return carry

    lax.fori_loop(0, t // chunk, step, 0)


WINDOW_OFFSETS = (-BAND, 0, -2 * BAND)


def _fill_band_bias(bias_ref):
    rel0 = (lax.broadcasted_iota(jnp.int32, (2 * TQ, WIN), 1)
            - lax.broadcasted_iota(jnp.int32, (2 * TQ, WIN), 0) % TQ)
    for j, off in enumerate(WINDOW_OFFSETS):
        rel = rel0 + off
        bias_ref[j] = jnp.where((rel >= -BAND) & (rel <= BAND), 0.0, NEG_INF)


def _fill_sequence_bias(bias_ref):
    rel = (lax.broadcasted_iota(jnp.int32, (2 * WIN, WIN), 1) - lax.broadcasted_iota(jnp.int32, (2 * WIN, WIN), 0) % WIN)
    bias_ref[...] = jnp.where((rel >= -BAND) & (rel <= BAND), 0.0, NEG_INF)


def _band_bias(bias_ref, off):
    return bias_ref[jnp.where(off == WINDOW_OFFSETS[0], 0, jnp.where(off == WINDOW_OFFSETS[1], 1, 2))]


def _stack_heads(a, first_head):
    return jnp.concatenate([jnp.where(first_head, a, 0.0), jnp.where(first_head, 0.0, a)], axis=0)


def _unstack_heads(a2, first_head):
    n = a2.shape[0] // 2
    return jnp.where(first_head, a2[:n], a2[n:])


def _attn_fwd(proj, comm=None):
    t = proj.shape[0]
    npair = ATTN_W // LANES

    def body(q_ref, k_ref, v_ref, o_ref, lse_ref, o1, o2, l0, l1, l2, m1, m2, bias, bias_seq):
        _fill_band_bias(bias)
        _fill_sequence_bias(bias_seq)
        outs, dens, maxs = (o_ref, o1, o2), (l0, l1, l2), (lse_ref, m1, m2)

        def softmax_tile(b, qrows, krows, n_q, band_bias):
            first_head = lax.broadcasted_iota(jnp.int32, (n_q, LANES), 1) < HEAD_DIM
            q2 = _stack_heads(q_ref[qrows, :] * SCALE, first_head).astype(BF16)
            kw = k_ref[krows, :].astype(BF16)
            vw = jnp.concatenate([v_ref[krows, :].astype(BF16), jnp.ones((WIN, LANES), BF16)], axis=1)
            s = lax.dot_general(q2, kw, LANE_CONTRACT, preferred_element_type=F32) + band_bias
            m = jnp.max(s, axis=-1, keepdims=True)
            pv = jnp.dot(jnp.exp(s - m).astype(BF16), vw, preferred_element_type=F32)
            outs[b][qrows, :] = _unstack_heads(pv[:, :LANES], first_head)
            dens[b][qrows, :] = _unstack_heads(pv[:, LANES:], first_head)
            maxs[b][qrows, :] = _unstack_heads(jnp.broadcast_to(m, (2 * n_q, LANES)), first_head)

        for b, d in enumerate(DILATIONS):
            lb = t // d
            if lb == WIN:
                def sequence(r, carry, b=b, d=d):
                    rows = pl.ds(r, WIN, stride=d)
                    softmax_tile(b, rows, rows, WIN, bias_seq[...])
                    return carry

                lax.fori_loop(0, d, sequence, 0, unroll=4)
                continue
            nt = lb // TQ

            def tile(ti, carry, b=b, d=d, lb=lb, nt=nt):
                qrows, krows, off = _tile_rows(ti, nt, lb, d)
                softmax_tile(b, qrows, krows, TQ, _band_bias(bias, off))
                return carry

            lax.fori_loop(0, d * nt, tile, 0, unroll=8)

        def finish(rows):
            ms = [m_b[rows, :] for m_b in maxs]
            m_all = jnp.maximum(jnp.maximum(ms[0], ms[1]), ms[2])
            ws = [jnp.exp(m_b - m_all) for m_b in ms]
            den = ws[0] * dens[0][rows, :] + ws[1] * dens[1][rows, :] + ws[2] * dens[2][rows, :]
            num = ws[0] * outs[0][rows, :] + ws[1] * outs[1][rows, :] + ws[2] * outs[2][rows, :]
            o_ref[rows, :] = num / den
            lse_ref[rows, :] = m_all + jnp.log(den)

        _for_row_chunks(t, finish, 256)

    cols = lambda base: pl.BlockSpec((t, LANES), lambda g: (0, base + g))
    out, got = _call(
        body, [proj, proj, proj], name="attn_fwd", grid=(npair,),
        in_specs=[cols(0), cols(npair), cols(2 * npair)],
        out_specs=[cols(0), cols(0)],
        out_shape=[jax.ShapeDtypeStruct((t, ATTN_W), F32)] * 2,
        scratch_shapes=[pltpu.VMEM((t, LANES), F32)] * 7 + [pltpu.VMEM((len(WINDOW_OFFSETS), 2 * TQ, WIN), F32),
                                                            pltpu.VMEM((2 * WIN, WIN), F32)],
        comm=comm)
    return out[0], out[1], got


def _attn_bwd(proj, cos, sin, d_attn, lse, delta, comm=None):
    t = proj.shape[0]
    npair = ATTN_W // LANES

    def body(q_ref, k_ref, v_ref, cos_ref, sin_ref, do_ref, l_ref, dl_ref, dqkv_ref,
             dq_acc, dk_acc, dv_acc, bias, bias_seq):
        _fill_band_bias(bias)
        _fill_sequence_bias(bias_seq)
        dq_acc[...] = jnp.zeros(dq_acc.shape, F32)
        dk_acc[...] = jnp.zeros(dk_acc.shape, F32)
        dv_acc[...] = jnp.zeros(dv_acc.shape, F32)
        def stack_column(a):
            return jnp.concatenate([a[:, 0:1], a[:, HEAD_DIM:HEAD_DIM + 1]], axis=0)

        def grad_tile(qrows, krows, n_q, band_bias):
            first_head = lax.broadcasted_iota(jnp.int32, (n_q, LANES), 1) < HEAD_DIM
            q2 = _stack_heads(q_ref[qrows, :] * SCALE, first_head).astype(BF16)
            do2 = _stack_heads(do_ref[qrows, :], first_head).astype(BF16)
            kw = k_ref[krows, :].astype(BF16)
            vw = v_ref[krows, :].astype(BF16)
            s = lax.dot_general(q2, kw, LANE_CONTRACT, preferred_element_type=F32) + band_bias
            p = jnp.exp(s - stack_column(l_ref[qrows, :]))
            dp = lax.dot_general(do2, vw, LANE_CONTRACT, preferred_element_type=F32)
            ds = (p * (dp - stack_column(dl_ref[qrows, :]))).astype(BF16)
            dq2 = jnp.dot(ds, kw, preferred_element_type=F32)
            dq_acc[qrows, :] += _unstack_heads(dq2, first_head) * SCALE
            dk_acc[krows, :] += lax.dot_general(ds, q2, ROW_CONTRACT, preferred_element_type=F32)
            dv_acc[krows, :] += lax.dot_general(p.astype(BF16), do2, ROW_CONTRACT, preferred_element_type=F32)

        for d in DILATIONS:
            lb = t // d
            if lb == WIN:
                def sequence(r, carry, d=d):
                    rows = pl.ds(r, WIN, stride=d)
                    grad_tile(rows, rows, WIN, bias_seq[...])
                    return carry

                lax.fori_loop(0, d, sequence, 0, unroll=2)
                continue
            nt = lb // TQ

            def tile(ti, carry, d=d, lb=lb, nt=nt):
                qrows, krows, off = _tile_rows(ti, nt, lb, d)
                grad_tile(qrows, krows, TQ, _band_bias(bias, off))
                return carry

            lax.fori_loop(0, d * nt, tile, 0, unroll=4)

        def finish(rows):
            dqkv_ref[0, rows, :] = _rope_transpose(dq_acc[rows, :], cos_ref[rows, :], sin_ref[rows, :]).astype(BF16)
            dqkv_ref[1, rows, :] = _rope_transpose(dk_acc[rows, :], cos_ref[rows, :], sin_ref[rows, :]).astype(BF16)
            dqkv_ref[2, rows, :] = dv_acc[rows, :].astype(BF16)

        _for_row_chunks(t, finish)

    cols = lambda base: pl.BlockSpec((t, LANES), lambda g: (0, base + g))
    out, got = _call(
        body, [proj, proj, proj, cos, sin, d_attn, lse, delta], name="attn_bwd", grid=(npair,),
        in_specs=[cols(0), cols(npair), cols(2 * npair), WHOLE_VMEM, WHOLE_VMEM, cols(0), cols(0), cols(0)],
        out_specs=[pl.BlockSpec((3, t, LANES), lambda g: (0, 0, g))],
        out_shape=[jax.ShapeDtypeStruct((3, t, ATTN_W), BF16)],
        scratch_shapes=[pltpu.VMEM((t, LANES), F32)] * 3 + [pltpu.VMEM((len(WINDOW_OFFSETS), 2 * TQ, WIN), F32),
                                                            pltpu.VMEM((2 * WIN, WIN), F32)],
        comm=comm)
    return out[0], got


def _norm_bwd(dres, pre, post, *, tm, comm=None):
    t, w = dres.shape
    row = pl.BlockSpec((tm, w), lambda i: (i, 0))
    gsum = _const_spec((1, w), (0, 0))
    ins, in_specs, out_shape, out_specs = [dres], [row], [], []
    if pre is not None:
        dh, x, g3, l = pre
        ins += [dh, x, g3]
        in_specs += [row, row, _gain_spec(g3, l)]
        out_shape += [jax.ShapeDtypeStruct((t, w), F32), jax.ShapeDtypeStruct((1, w), F32)]
        out_specs += [row, gsum]
    if post is not None:
        y, g3, l = post
        ins += [y, g3]
        in_specs += [row, _gain_spec(g3, l)]
        out_shape += [jax.ShapeDtypeStruct((t, w), BF16), jax.ShapeDtypeStruct((1, w), F32)]
        out_specs += [row, gsum]
    n_in = len(ins)

    def body(*refs):
        first = pl.program_id(0) == 0
        ins_r, outs_r = list(refs[:n_in]), list(refs[n_in:])
        d = ins_r.pop(0)[...]
        if pre is not None:
            dh_ref, x_ref, g_ref = ins_r[:3]
            ins_r = ins_r[3:]
            dx, dg = _rms_bwd(x_ref[...], g_ref[...], dh_ref[...])
            d = d + dx
            outs_r.pop(0)[...] = d
            _accumulate(outs_r.pop(0), dg, first)
        if post is not None:
            y_ref, g_ref = ins_r
            dy, dg = _rms_bwd(y_ref[...], g_ref[...], d)
            outs_r.pop(0)[...] = dy.astype(BF16)
            _accumulate(outs_r.pop(0), dg, first)

    out, got = _call(body, ins, name="norm_bwd", grid=(t // tm,), in_specs=in_specs, out_specs=out_specs,
                     out_shape=out_shape, comm=comm)
    d_new, dg_pre = (out.pop(0), out.pop(0)) if pre is not None else (None, None)
    dy, dg_post = (out.pop(0), out.pop(0)) if post is not None else (None, None)
    return d_new, dy, dg_pre, dg_post, got


def _merge_bwd(d_merged, attn, conv_y, ga3, gc3, l, *, tm, comm=None):
    t = attn.shape[0]
    row = pl.BlockSpec((tm, ATTN_W), lambda i: (i, 0))
    gsum = _const_spec((1, ATTN_W), (0, 0))

    def body(dma_ref, dmc_ref, a_ref, c_ref, ga_ref, gc_ref, da_ref, dl_ref, dc_ref, dga_ref, dgc_ref):
        first = pl.program_id(0) == 0
        attn_t = a_ref[...]
        da, dga = _rms_bwd(attn_t, ga_ref[...], dma_ref[...])
        dc, dgc = _rms_bwd(c_ref[...], gc_ref[...], dmc_ref[...])
        da_ref[...] = da
        dc_ref[...] = dc
        same_head = (lax.broadcasted_iota(jnp.int32, (ATTN_W, ATTN_W), 0) // HEAD_DIM
                     == lax.broadcasted_iota(jnp.int32, (ATTN_W, ATTN_W), 1) // HEAD_DIM).astype(BF16)
        rest = da * attn_t
        total = jnp.zeros(rest.shape, F32)
        for _ in range(3):
            term = rest.astype(BF16)
            total = total + jnp.dot(term, same_head, preferred_element_type=F32)
            rest = rest - term.astype(F32)
        dl_ref[...] = total
        _accumulate(dga_ref, dga, first)
        _accumulate(dgc_ref, dgc, first)

    out, got = _call(
        body, [d_merged, d_merged, attn, conv_y, ga3, gc3], name="merge_bwd", grid=(t // tm,),
        in_specs=[pl.BlockSpec((tm, ATTN_W), lambda i: (i, 0)), pl.BlockSpec((tm, CONV_W), lambda i: (i, 1)),
                  row, row, _gain_spec(ga3, l), _gain_spec(gc3, l)],
        out_specs=[row, row, row, gsum, gsum],
        out_shape=[jax.ShapeDtypeStruct((t, ATTN_W), F32)] * 3 + [jax.ShapeDtypeStruct((1, ATTN_W), F32)] * 2,
        comm=comm)
    return (*out, got)


def _conv_bwd(proj, conv_w, l, d_conv_y):
    t = proj.shape[0]
    col0 = 3 * ATTN_W // LANES
    nj = CONV_W // LANES

    def body(u_ref, gb_ref, gc_ref, w_ref, dy_ref, d3_ref, dw_ref):
        u, gc, dy = u_ref[...], gc_ref[...], dy_ref[...]
        row = lax.broadcasted_iota(jnp.int32, u.shape, 0)
        down = lambda a: jnp.where(row == 0, 0.0, pltpu.roll(a, 1, 0))
        up = lambda a: jnp.where(row == t - 1, 0.0, pltpu.roll(a, t - 1, 0))
        w = w_ref[...]
        c = gc * u
        c_prev, c_next = down(c), up(c)
        d3_ref[1] = (dy * (w[0:1] * c_prev + w[1:2] * c + w[2:3] * c_next)).astype(BF16)
        dz = dy * gb_ref[...]
        dc = w[0:1] * up(dz) + w[1:2] * dz + w[2:3] * down(dz)
        d3_ref[0] = (dc * gc).astype(BF16)
        d3_ref[2] = (dc * u).astype(BF16)
        dw_ref[0:1, :] = jnp.sum(dz * c_prev, axis=0, keepdims=True)
        dw_ref[1:2, :] = jnp.sum(dz * c, axis=0, keepdims=True)
        dw_ref[2:3, :] = jnp.sum(dz * c_next, axis=0, keepdims=True)

    cols = lambda base: pl.BlockSpec((t, LANES), lambda j: (0, base + j))
    return pl.pallas_call(
        body, grid=(nj,),
        in_specs=[cols(col0), cols(col0 + nj), cols(col0 + 2 * nj),
                  pl.BlockSpec((None, None, 3, LANES), lambda j: (l, j, 0, 0)), cols(0)],
        out_specs=[pl.BlockSpec((3, t, LANES), lambda j: (0, 0, j)), pl.BlockSpec((None, 3, LANES), lambda j: (j, 0, 0))],
        out_shape=[jax.ShapeDtypeStruct((3, t, CONV_W), BF16), jax.ShapeDtypeStruct((nj, 3, LANES), F32)],
        name="conv_bwd")(proj, proj, proj, conv_w, d_conv_y)


def _own_shard_slab(w, l, place, dtype):
    _, rows, cols = w.shape
    tr = rows if rows <= 704 else 512
    assert rows % tr == 0

    def body(p_ref, w_ref, o_ref):
        del p_ref
        o_ref[...] = w_ref[...].astype(dtype)

    grid_spec = pltpu.PrefetchScalarGridSpec(
        num_scalar_prefetch=1, grid=(rows // tr,),
        in_specs=[pl.BlockSpec((None, tr, cols), lambda i, p: (l, i, 0))],
        out_specs=pl.BlockSpec((None, tr, cols), lambda i, p: (p[0], i, 0)))
    return pl.pallas_call(body, grid_spec=grid_spec, name="own_shard_slab",
                          out_shape=jax.ShapeDtypeStruct((N_CHIPS, rows, cols), dtype))(place, w)


def _own_conv_slab(w, place):
    depth = w.shape[0]

    def body(p_ref, w_ref, o_ref):
        del p_ref
        o_ref[...] = w_ref[...]

    grid_spec = pltpu.PrefetchScalarGridSpec(
        num_scalar_prefetch=1, grid=(depth,),
        in_specs=[pl.BlockSpec((None, 3, LANES), lambda l, p: (l, 0, 0))],
        out_specs=pl.BlockSpec((None, None, 3, LANES), lambda l, p: (l, p[0], 0, 0)))
    return pl.pallas_call(body, grid_spec=grid_spec, name="own_conv_slab",
                          out_shape=jax.ShapeDtypeStruct((depth, N_CHIPS, 3, LANES), F32))(place, w)


def _add_halves(grad, got, place):
    s_n, rows, cols = grad.shape
    hr = rows // 2

    def body(p_ref, g_ref, r_ref, o_ref):
        del p_ref
        o_ref[...] = (g_ref[...] + r_ref[...]).astype(BF16)

    grid_spec = pltpu.PrefetchScalarGridSpec(
        num_scalar_prefetch=1, grid=(s_n,),
        in_specs=[pl.BlockSpec((None, hr, cols), lambda s, p: (s, p[1], 0)),
                  pl.BlockSpec((None, hr, cols), lambda s, p: (s, 0, 0))],
        out_specs=pl.BlockSpec((None, hr, cols), lambda s, p: (s, 0, 0)))
    return pl.pallas_call(body, grid_spec=grid_spec, out_shape=jax.ShapeDtypeStruct((s_n, hr, cols), BF16),
                          name="add_halves")(place, grad, got)


def _sum_partials(partial, got, place, acc, l):
    _, hr, cols = partial.shape

    def body(p_ref, mine_ref, got_ref, acc_ref, o_ref):
        del p_ref, acc_ref
        total = mine_ref[...].astype(F32)
        for k in range(3):
            total = total + got_ref[k].astype(F32)
        o_ref[...] = total

    grid_spec = pltpu.PrefetchScalarGridSpec(
        num_scalar_prefetch=1, grid=(1,),
        in_specs=[pl.BlockSpec((None, hr, cols), lambda i, p: (p[0], 0, 0)),
                  pl.BlockSpec((3, hr, cols), lambda i, p: (0, 0, 0)), ANY],
        out_specs=pl.BlockSpec((None, hr, cols), lambda i, p: (l, p[1], 0)))
    return pl.pallas_call(body, grid_spec=grid_spec, out_shape=jax.ShapeDtypeStruct(acc.shape, F32),
                          input_output_aliases={3: 0}, name="sum_partials")(place, partial, got, acc)


def _allreduce_small(vec, loss_row):
    rows = vec.shape[0]

    def body(v_ref, o_ref, slots, send_sems, recv_sems):
        x, y, c, _ = _place()
        me = 4 * x + 2 * y + c
        slots[me] = v_ref[...]
        copies = []
        for k in range(1, N_DEV):
            flip = lambda v, bit: 1 - v if bit else v
            peer = (flip(x, k & 4), flip(y, k & 2), flip(c, k & 1))
            copies.append(_remote(v_ref, slots.at[me], send_sems.at[k - 1], recv_sems.at[k - 1], peer))
        for cp in copies:
            cp.start()
        for k in range(1, N_DEV):
            flip = lambda v, bit: 1 - v if bit else v
            peer_id = 4 * flip(x, k & 4) + 2 * flip(y, k & 2) + flip(c, k & 1)
            _remote(v_ref, slots.at[peer_id], send_sems.at[k - 1], recv_sems.at[k - 1], (x, y, c)).wait_recv()
        for cp in copies:
            cp.wait_send()
        total = slots[0]
        for dev in range(1, N_DEV):
            total = total + slots[dev]
        o_ref[...] = total
        o_ref[loss_row:loss_row + 1, :] = jnp.broadcast_to(
            jnp.sum(total[loss_row:loss_row + 1, :], axis=-1, keepdims=True), (1, LANES))

    return pl.pallas_call(
        body, in_specs=[WHOLE_VMEM], out_specs=WHOLE_VMEM, out_shape=jax.ShapeDtypeStruct((rows, LANES), F32),
        scratch_shapes=[pltpu.VMEM((N_DEV, rows, LANES), F32), pltpu.SemaphoreType.DMA((N_DEV - 1,)),
                        pltpu.SemaphoreType.DMA((N_DEV - 1,))],
        name="allreduce_small")(vec)


def _adamw(w, g, m, v, *, tr, emit_grad=False):
    depth, rows, cols = w.shape
    assert rows % tr == 0
    c1 = float(np.float32(1.0 - ADAM_B1 ** ADAM_STEP))
    c2 = float(np.float32(1.0 - ADAM_B2 ** ADAM_STEP))

    def body(w_ref, g_ref, m_ref, v_ref, d_ref, mo_ref, vo_ref, *go_ref):
        g_t = g_ref[...]
        if emit_grad:
            go_ref[0][...] = g_t
        m_new = ADAM_B1 * m_ref[...] + (1.0 - ADAM_B1) * g_t
        v_new = ADAM_B2 * v_ref[...] + (1.0 - ADAM_B2) * (g_t * g_t)
        mo_ref[...] = m_new
        vo_ref[...] = v_new
        d_ref[...] = -ADAM_LR * ((m_new / c1) / (jnp.sqrt(v_new / c2) + ADAM_EPS) + ADAM_WD * w_ref[...])

    blk = pl.BlockSpec((None, tr, cols), lambda l, i: (l, i, 0))
    return pl.pallas_call(
        body, grid=(depth, rows // tr), in_specs=[blk] * 4, out_specs=[blk] * (4 if emit_grad else 3),
        out_shape=[jax.ShapeDtypeStruct(w.shape, F32)] * (4 if emit_grad else 3), name="adamw")(w, g, m, v)


def _local_step(x, positions, target, gains, exchange):
    t = x.shape[0]
    tm = 512
    inv_freq = ROPE_THETA ** (-jnp.arange(0, ROPE_DIM, 2, dtype=F32) / ROPE_DIM)
    lane = np.arange(LANES) % HEAD_DIM
    freq_row = jnp.where(lane < ROPE_DIM, inv_freq[lane % (ROPE_DIM // 2)], 0.0).astype(F32)[None, :]
    cos, sin = _rope_tables(positions.reshape(t, 1), freq_row)

    def hosted(tag, fn, *args, **kwargs):
        *out, got = fn(*args, comm=exchange.host(tag), **kwargs)
        if got is not None:
            exchange.hosted(tag, got)
        return out[0] if len(out) == 1 else out

    saved = []
    h1 = _norm_fwd(x, gains["pre_mix_norm"], 0, tm=tm)
    for l in range(DEPTH):
        proj = hosted(("fwd", l, "in_proj"), _in_proj, h1, exchange.weight("w_in", l), cos, sin, tm=1024)
        attn, lse = hosted(("fwd", l, "attn"), _attn_fwd, proj)
        conv_y = _conv_fwd(proj, exchange.weight("conv_w", l), l)
        merged = _merge_fwd(attn, conv_y, gains["attn_out_norm"], gains["conv_out_norm"], l, tm=tm)
        mix, x1, h2 = hosted(("fwd", l, "out_proj"), _mm_resnorm, merged, exchange.weight("w_out", l), x,
                             gains["post_mix_norm"], l, gains["pre_ffn_norm"], l, tm=tm, name="out_proj")
        g, u, act = hosted(("fwd", l, "gate_up"), _gate_up_swiglu, h2, exchange.weight("w_gate_up", l), tm=1024)
        nxt = (gains["pre_mix_norm"], l + 1) if l + 1 < DEPTH else (None, None)
        f, x2, h1_next = hosted(("fwd", l, "down"), _mm_resnorm, act, exchange.weight("w_down", l), x1,
                                gains["post_ffn_norm"], l, *nxt, tm=tm, name="down")
        saved.append(dict(x=x, h1=h1, proj=proj, attn=attn, lse=lse, conv_y=conv_y, merged=merged, mix=mix,
                          x1=x1, h2=h2, g=g, u=u, act=act, f=f))
        x, h1 = x2, h1_next

    dres, loss_lanes = _loss_fwd_bwd(x, target, tm=tm)

    g_gain = {k: [None] * DEPTH for k in gains}
    g_conv = [None] * DEPTH
    _, df, _, g_gain["post_ffn_norm"][DEPTH - 1], _ = _norm_bwd(
        dres, None, (saved[-1]["f"], gains["post_ffn_norm"], DEPTH - 1), tm=tm)
    for l in reversed(range(DEPTH)):
        sv = saved[l]
        w = {k: exchange.weight(k, l) for k in MATRIX_NAMES + ("conv_w",)}
        dg, du = _down_dx_swiglu_bwd(df, w["w_down"], sv["g"], sv["u"], tm=1024, tko=FFN // 2)
        g_down = _mm_tn(sv["act"], df, 1, tka=256, name="down_dw")
        dx1, dmix, g_gain["pre_ffn_norm"][l], g_gain["post_mix_norm"][l] = hosted(
            ("bwd", l, "gate_up_dx"), _gate_up_dx_norms, dg, du, w["w_gate_up"], dres,
            (sv["x1"], gains["pre_ffn_norm"], l), (sv["mix"], gains["post_mix_norm"], l), tm=tm)
        g_gate_up = _mm_tn(sv["h2"], dg, N_CHIPS // 2, tka=512, name="gate_up_dw",
                           into=lax.empty(w["w_gate_up"].shape, F32))
        g_gate_up = _mm_tn(sv["h2"], du, N_CHIPS // 2, tka=512, name="gate_up_dw", into=g_gate_up,
                           shard0=N_CHIPS // 2)
        exchange.grads(l, "ffn", dict(w_down=g_down.reshape(N_CHIPS, FFN // N_CHIPS, D_MODEL), w_gate_up=g_gate_up))
        d_merged = hosted(("bwd", l, "out_proj_dx"), _mm_nt, dmix, w["w_out"], tm=1024, tko=D_MODEL, name="out_proj_dx")
        g_out = _mm_tn(sv["merged"], dmix, 1, tka=512, name="out_proj_dw")
        d_attn, delta, d_conv_y, g_gain["attn_out_norm"][l], g_gain["conv_out_norm"][l] = hosted(
            ("bwd", l, "merge"), _merge_bwd,
            d_merged, sv["attn"], sv["conv_y"], gains["attn_out_norm"], gains["conv_out_norm"], l, tm=tm)
        d_attn3 = hosted(("bwd", l, "attn"), _attn_bwd, sv["proj"], cos, sin, d_attn, sv["lse"], delta)
        d_conv3, g_conv[l] = _conv_bwd(sv["proj"], w["conv_w"], l, d_conv_y)
        g_in = _in_proj_dw(sv["h1"], d_attn3, d_conv3, N_CHIPS, tka=512)
        exchange.grads(l, "mix", dict(w_out=g_out.reshape(N_CHIPS, D_MODEL // N_CHIPS, D_MODEL), w_in=g_in))
        if l > 0:
            dres, df, g_gain["pre_mix_norm"][l], g_gain["post_ffn_norm"][l - 1] = hosted(
                ("bwd", l, "in_proj_dx"), _in_proj_dx_norms, d_attn3, d_conv3, w["w_in"], dx1,
                (sv["x"], gains["pre_mix_norm"], l), (saved[l - 1]["f"], gains["post_ffn_norm"], l - 1), tm=tm)
        else:
            dh1 = hosted(("bwd", l, "in_proj_dx"), _in_proj_dx, d_attn3, d_conv3, w["w_in"], tm=1024)
            dres, _, g_gain["pre_mix_norm"][l], _ = hosted(
                ("bwd", l, "norm_low"), _norm_bwd, dx1, (dh1, sv["x"], gains["pre_mix_norm"], l), None, tm=tm)

    g_gain = {k: jnp.concatenate(v, axis=0) for k, v in g_gain.items()}
    return loss_lanes, dres, g_gain, jnp.stack(g_conv, axis=0)


class _Exchange:
    GATHER_HOSTS = {"in_proj": (("w_out", 0), ("w_down", 0)), "attn": (("w_gate_up", 0),), "gate_up": (("w_in", 1),)}

    @staticmethod
    def _reduce_hosts(group, l):
        if group == "ffn":
            return "merge", "attn", l
        if l > 0:
            return "in_proj_dx", "gate_up_dx", l - 1
        return "in_proj_dx", "norm_low", l

    def __init__(self, params, place):
        self.place = place
        self.slabs = {k: [_own_shard_slab(params[k], l, place, BF16) for l in range(DEPTH)] for k in MATRIX_NAMES}
        self.gathered = {k: [None] * DEPTH for k in MATRIX_NAMES}
        self.gathered["w_in"][0], self.conv_w = _run_comm(
            _gather_comm([self.slabs["w_in"][0]], _own_conv_slab(params["conv_w"], place)), "gather_first")
        self.full = {k: lax.empty(params[k].shape, F32) for k in MATRIX_NAMES}
        self.pending = {}
        self.raw = {}

    def weight(self, name, l):
        if name == "conv_w":
            return self.conv_w
        g = self.gathered[name][l]
        return g.reshape(1, g.shape[0] * g.shape[1], g.shape[2]) if name in ("w_out", "w_down") else g

    def host(self, tag):
        phase, l, kernel = tag
        if phase == "fwd":
            carried = [(name, l + ahead) for name, ahead in self.GATHER_HOSTS.get(kernel, ()) if l + ahead < DEPTH]
            return _gather_comm([self.slabs[name][layer] for name, layer in carried]) if carried else None
        if tag in self.pending:
            stage, _, _, arrays = self.pending[tag]
            return _halves_comm(arrays) if stage == "halves" else _partials_comm(arrays)
        return None

    def hosted(self, tag, results):
        phase, l, kernel = tag
        if phase == "fwd":
            carried = [(name, l + ahead) for name, ahead in self.GATHER_HOSTS[kernel] if l + ahead < DEPTH]
            for (name, layer), slab in zip(carried, results):
                self.gathered[name][layer] = slab
            return
        stage, gl, group, arrays = self.pending.pop(tag)
        names = list(self.raw[(gl, group)])
        if stage == "partials":
            self._finish_reduction(gl, names, arrays, results)
            return
        partials = [_add_halves(self.raw[(gl, group)][k], r, self.place) for k, r in zip(names, results)]
        _, ici_kernel, ici_layer = self._reduce_hosts(group, gl)
        self.pending[("bwd", ici_layer, ici_kernel)] = ("partials", gl, group, partials)

    def grads(self, l, group, grads):
        self.raw[(l, group)] = grads
        self.pending[("bwd", l, self._reduce_hosts(group, l)[0])] = ("halves", l, group, [grads[k] for k in grads])

    def _finish_reduction(self, l, names, partials, others):
        for k, p, q in zip(names, partials, others):
            self.full[k] = _sum_partials(p, q, self.place, self.full[k], l)
        shared = _run_comm(_share_comm([self.full[k] for k in names], l), "share_halves")
        for k, g in zip(names, shared):
            self.full[k] = g


def kernel(x, positions, pre_mix_norm, w_in, conv_w, attn_out_norm, conv_out_norm, w_out, post_mix_norm, pre_ffn_norm, w_gate_up, w_down, post_ffn_norm, loss_target, m_pre_mix_norm, m_w_in, m_conv_w, m_attn_out_norm, m_conv_out_norm, m_w_out, m_post_mix_norm, m_pre_ffn_norm, m_w_gate_up, m_w_down, m_post_ffn_norm, v_pre_mix_norm, v_w_in, v_conv_w, v_attn_out_norm, v_conv_out_norm, v_w_out, v_post_mix_norm, v_pre_ffn_norm, v_w_gate_up, v_w_down, v_post_ffn_norm):
    params = dict(pre_mix_norm=pre_mix_norm, w_in=w_in, conv_w=conv_w, attn_out_norm=attn_out_norm,
                  conv_out_norm=conv_out_norm, w_out=w_out, post_mix_norm=post_mix_norm, pre_ffn_norm=pre_ffn_norm,
                  w_gate_up=w_gate_up, w_down=w_down, post_ffn_norm=post_ffn_norm)
    mom1 = dict(pre_mix_norm=m_pre_mix_norm, w_in=m_w_in, conv_w=m_conv_w, attn_out_norm=m_attn_out_norm,
                conv_out_norm=m_conv_out_norm, w_out=m_w_out, post_mix_norm=m_post_mix_norm,
                pre_ffn_norm=m_pre_ffn_norm, w_gate_up=m_w_gate_up, w_down=m_w_down, post_ffn_norm=m_post_ffn_norm)
    mom2 = dict(pre_mix_norm=v_pre_mix_norm, w_in=v_w_in, conv_w=v_conv_w, attn_out_norm=v_attn_out_norm,
                conv_out_norm=v_conv_out_norm, w_out=v_w_out, post_mix_norm=v_post_mix_norm,
                pre_ffn_norm=v_pre_ffn_norm, w_gate_up=v_w_gate_up, w_down=v_w_down, post_ffn_norm=v_post_ffn_norm)
    xi, yi, ci = lax.axis_index("x"), lax.axis_index("y"), lax.axis_index("c")
    place = jnp.stack([2 * xi + yi, ci]).astype(jnp.int32)

    exchange = _Exchange(params, place)
    gains = {k: params[k][:, None, :] for k in GAIN_NAMES}
    loss_lanes, grad_x, g_gain, g_conv = _local_step(x[0], positions[0], loss_target[0], gains, exchange)
    grad = dict(exchange.full)

    small = [g_gain[k].reshape(-1) for k in GAIN_NAMES] + [g_conv.reshape(-1), loss_lanes.reshape(-1)]
    sizes = [int(s.shape[0]) for s in small]
    flat = jnp.concatenate(small)
    loss_row = (sum(sizes) - LANES) // LANES
    rows = -(-flat.shape[0] // (8 * LANES)) * 8
    flat = jnp.pad(flat, (0, rows * LANES - flat.shape[0])).reshape(rows, LANES)
    total = _allreduce_small(flat, loss_row).reshape(-1)
    offsets = np.cumsum([0] + sizes)
    for i, k in enumerate(GAIN_NAMES):
        grad[k] = total[offsets[i]:offsets[i + 1]].reshape(params[k].shape)
    conv_all = total[offsets[6]:offsets[7]].reshape(DEPTH, N_CHIPS, 3, LANES)
    grad["conv_w"] = lax.dynamic_index_in_dim(conv_all, 2 * xi + yi, axis=1, keepdims=False)
    loss = total[offsets[7]]

    delta, new_m, new_v = {}, {}, {}
    for k in WEIGHT_ORDER:
        shape = params[k].shape
        if k in MATRIX_NAMES:
            tr = {1024: 512, 704: 352, 256: 256}[shape[1]]
            delta[k], new_m[k], new_v[k], grad[k] = _adamw(params[k], grad[k], mom1[k], mom2[k], tr=tr, emit_grad=True)
        else:
            as3 = (lambda a: a) if len(shape) == 3 else (lambda a: a[None])
            d, m, v = _adamw(as3(params[k]), as3(grad[k]), as3(mom1[k]), as3(mom2[k]), tr=as3(params[k]).shape[1])
            delta[k], new_m[k], new_v[k] = d.reshape(shape), m.reshape(shape), v.reshape(shape)

    return (loss, grad_x[None], *[grad[k] for k in WEIGHT_ORDER], *[delta[k] for k in WEIGHT_ORDER],
            *[new_m[k] for k in WEIGHT_ORDER], *[new_v[k] for k in WEIGHT_ORDER])
```

```python
import functools
from typing import Callable, NamedTuple

import numpy as np
import jax
import jax.numpy as jnp
from jax import lax
from jax.experimental import pallas as pl
from jax.experimental.pallas import tpu as pltpu

F32 = jnp.float32
BF16 = jnp.bfloat16
MESH = pl.DeviceIdType.MESH

D_MODEL = 1024
ATTN_W = 512
CONV_W = 512
HEAD_DIM = 64
ROPE_DIM = 16
ROPE_THETA = 500000.0
FFN = 2816
DEPTH = 4
RMS_EPS = 1e-6
NEG_INF = -1e30
N_CHIPS = 4
N_DEV = 8
LANES = 128
BF16_ROWS = 16
DILATIONS = (1, 4, 16)
BAND = 64
TQ = 128
WIN = TQ + 2 * BAND
SCALE = HEAD_DIM ** -0.5

ADAM_LR = 0.001
ADAM_B1 = 0.9
ADAM_B2 = 0.999
ADAM_EPS = 1e-08
ADAM_WD = 0.01
ADAM_STEP = 10

GAIN_NAMES = ("pre_mix_norm", "attn_out_norm", "conv_out_norm", "post_mix_norm", "pre_ffn_norm", "post_ffn_norm")
MATRIX_NAMES = ("w_in", "w_out", "w_gate_up", "w_down")
WEIGHT_ORDER = ("pre_mix_norm", "w_in", "conv_w", "attn_out_norm", "conv_out_norm", "w_out", "post_mix_norm",
                "pre_ffn_norm", "w_gate_up", "w_down", "post_ffn_norm")

ANY = pl.BlockSpec(memory_space=pl.ANY)
WHOLE_VMEM = pl.BlockSpec(memory_space=pltpu.VMEM)
LANE_CONTRACT = (((1,), (1,)), ((), ()))
ROW_CONTRACT = (((0,), (0,)), ((), ()))
CHUNK = 256


def _const_spec(block, index):
    return pl.BlockSpec(block, lambda *_: index)


def _gain_spec(g3, l):
    return _const_spec((None, 1, g3.shape[-1]), (l, 0, 0))


class _Comm(NamedTuple):
    ins: tuple
    inouts: tuple
    out_shapes: tuple
    n_sems: int
    start: Callable
    finish: Callable


def _place():
    x, y, c = lax.axis_index("x"), lax.axis_index("y"), lax.axis_index("c")
    other_chips = [(1 - x, y), (x, 1 - y), (1 - x, 1 - y)]
    return x, y, c, other_chips


def _remote(src, dst, send_sem, recv_sem, to):
    return pltpu.make_async_remote_copy(src_ref=src, dst_ref=dst, send_sem=send_sem, recv_sem=recv_sem,
                                        device_id=to, device_id_type=MESH)


def _call(body, operands, *, name, grid, in_specs, out_specs, out_shape, scratch_shapes=(), comm=None):
    in_specs, out_specs, out_shape = list(in_specs), list(out_specs), list(out_shape)
    scratch_shapes = list(scratch_shapes)
    if comm is None:
        out = pl.pallas_call(body, grid=grid, in_specs=in_specs, out_specs=out_specs, out_shape=out_shape,
                             scratch_shapes=scratch_shapes, name=name)(*operands)
        return list(out), None
    n_in, n_out, n_scr = len(in_specs), len(out_shape), len(scratch_shapes)
    n_ci, n_cio, n_co = len(comm.ins), len(comm.inouts), len(comm.out_shapes)

    def hosted(*refs):
        refs = list(refs)
        ins, c_ins = refs[:n_in], refs[n_in:n_in + n_ci]
        base = n_in + n_ci + n_cio
        outs = refs[base:base + n_out]
        c_io = refs[base + n_out:base + n_out + n_cio]
        c_out = refs[base + n_out + n_cio:base + n_out + n_cio + n_co]
        scr = refs[base + n_out + n_cio + n_co:]
        send_sems, recv_sems = scr[n_scr], scr[n_scr + 1]
        if grid:
            first = functools.reduce(jnp.logical_and, [pl.program_id(a) == 0 for a in range(len(grid))])
            last = functools.reduce(jnp.logical_and, [pl.program_id(a) == grid[a] - 1 for a in range(len(grid))])
            pl.when(first)(lambda: comm.start(c_ins, c_io, c_out, send_sems, recv_sems))
            body(*ins, *outs, *scr[:n_scr])
            pl.when(last)(lambda: comm.finish(c_ins, c_io, c_out, send_sems, recv_sems))
        else:
            comm.start(c_ins, c_io, c_out, send_sems, recv_sems)
            body(*ins, *outs, *scr[:n_scr])
            comm.finish(c_ins, c_io, c_out, send_sems, recv_sems)

    res = pl.pallas_call(
        hosted, grid=grid, in_specs=in_specs + [ANY] * (n_ci + n_cio), out_specs=out_specs + [ANY] * (n_cio + n_co),
        out_shape=out_shape + [jax.ShapeDtypeStruct(a.shape, a.dtype) for a in comm.inouts] + list(comm.out_shapes),
        input_output_aliases={n_in + n_ci + i: n_out + i for i in range(n_cio)},
        scratch_shapes=scratch_shapes + [pltpu.SemaphoreType.DMA((comm.n_sems,))] * 2,
        name=name)(*operands, *comm.ins, *comm.inouts)
    return list(res[:n_out]), list(res[n_out:])


def _run_comm(comm, name):
    return _call(lambda: None, [], name=name, grid=(), in_specs=[], out_specs=[], out_shape=[], comm=comm)[1]


def _row_half(ref, lead, core, rows, align):
    hr = rows // 2
    return ref.at[(*lead, pl.ds(pl.multiple_of(core * hr, align), hr), slice(None))]


def _gather_comm(slabs, conv_slab=None):
    n = len(slabs)
    n_conv = 0 if conv_slab is None else 3

    def direct(ios, send, recv):
        x, y, c, chips = _place()
        copies = []
        for a in range(n):
            own = _row_half(ios[a], (2 * x + y,), c, slabs[a].shape[1], BF16_ROWS)
            copies += [_remote(own, own, send.at[a * 3 + j], recv.at[a * 3 + j], (*chip, c))
                       for j, chip in enumerate(chips)]
        if conv_slab is not None:
            own = ios[n].at[:, 2 * x + y]
            copies += [_remote(own, own, send.at[6 * n + j], recv.at[6 * n + j], (*chip, c))
                       for j, chip in enumerate(chips)]
        return copies

    def start(ins, ios, outs, send, recv):
        for cp in direct(ios, send, recv):
            cp.start()

    def finish(ins, ios, outs, send, recv):
        x, y, c, chips = _place()
        sibling = (x, y, 1 - c)
        passed = []
        for a in range(n):
            for j, chip in enumerate(chips):
                landed = _row_half(ios[a], (2 * chip[0] + chip[1],), c, slabs[a].shape[1], BF16_ROWS)
                _remote(landed, landed, send.at[a * 3 + j], recv.at[a * 3 + j], (*chip, c)).wait_recv()
                fwd = _remote(landed, landed, send.at[3 * n + a * 3 + j], recv.at[3 * n + a * 3 + j], sibling)
                fwd.start()
                passed.append(fwd)
        if conv_slab is not None:
            for j, chip in enumerate(chips):
                landed = ios[n].at[:, 2 * chip[0] + chip[1]]
                _remote(landed, landed, send.at[6 * n + j], recv.at[6 * n + j], (*chip, c)).wait_recv()
        for a in range(n):
            for j, chip in enumerate(chips):
                landed = _row_half(ios[a], (2 * chip[0] + chip[1],), 1 - c, slabs[a].shape[1], BF16_ROWS)
                _remote(landed, landed, send.at[3 * n + a * 3 + j], recv.at[3 * n + a * 3 + j], sibling).wait_recv()
        for cp in direct(ios, send, recv) + passed:
            cp.wait_send()

    inouts = tuple(slabs) + (() if conv_slab is None else (conv_slab,))
    return _Comm((), inouts, (), 6 * n + n_conv, start, finish)


def _halves_comm(grads):
    n = len(grads)

    def copies(ins, outs, send, recv):
        x, y, c, _ = _place()
        return [_remote(_row_half(ins[a], (slice(None),), 1 - c, grads[a].shape[1], 8), outs[a],
                        send.at[a], recv.at[a], (x, y, 1 - c)) for a in range(n)]

    def start(ins, ios, outs, send, recv):
        for cp in copies(ins, outs, send, recv):
            cp.start()

    def finish(ins, ios, outs, send, recv):
        for cp in copies(ins, outs, send, recv):
            cp.wait()

    out_shapes = tuple(jax.ShapeDtypeStruct((g.shape[0], g.shape[1] // 2, g.shape[2]), F32) for g in grads)
    return _Comm(tuple(grads), (), out_shapes, n, start, finish)


def _partials_comm(partials):
    n = len(partials)

    def copies(ins, outs, send, recv):
        x, y, c, chips = _place()
        return [_remote(ins[a].at[2 * chip[0] + chip[1]], outs[a].at[k], send.at[a * 3 + k], recv.at[a * 3 + k],
                        (*chip, c)) for a in range(n) for k, chip in enumerate(chips)]

    def start(ins, ios, outs, send, recv):
        for cp in copies(ins, outs, send, recv):
            cp.start()

    def finish(ins, ios, outs, send, recv):
        for cp in copies(ins, outs, send, recv):
            cp.wait()

    out_shapes = tuple(jax.ShapeDtypeStruct((3,) + p.shape[1:], BF16) for p in partials)
    return _Comm(tuple(partials), (), out_shapes, 3 * n, start, finish)


def _share_comm(grads, l):
    n = len(grads)

    def start(ins, ios, outs, send, recv):
        x, y, c, _ = _place()
        for a in range(n):
            mine = _row_half(ios[a], (l,), c, grads[a].shape[1], 8)
            _remote(mine, mine, send.at[a], recv.at[a], (x, y, 1 - c)).start()

    def finish(ins, ios, outs, send, recv):
        x, y, c, _ = _place()
        for a in range(n):
            theirs = _row_half(ios[a], (l,), 1 - c, grads[a].shape[1], 8)
            _remote(theirs, theirs, send.at[a], recv.at[a], (x, y, 1 - c)).wait()

    return _Comm((), tuple(grads), (), n, start, finish)


def _rms_fwd(x, g):
    r = lax.rsqrt(jnp.mean(x * x, axis=-1, keepdims=True) + RMS_EPS)
    return (x * r) * g


def _rms_bwd(x, g, dy):
    r = lax.rsqrt(jnp.mean(x * x, axis=-1, keepdims=True) + RMS_EPS)
    xh = x * r
    u = dy * g
    dx = r * (u - xh * jnp.mean(xh * u, axis=-1, keepdims=True))
    return dx, jnp.sum(dy * xh, axis=0, keepdims=True)


def _accumulate(ref, value, first):
    @pl.when(first)
    def _():
        ref[...] = value

    @pl.when(jnp.logical_not(first))
    def _():
        ref[...] += value


def _rope_coeffs(cos, sin):
    m = lax.broadcasted_iota(jnp.int32, cos.shape, 1) % HEAD_DIM
    a = jnp.where(m < ROPE_DIM, cos, 1.0)
    b = jnp.where(m < ROPE_DIM // 2, -sin, 0.0)
    c = jnp.where((m >= ROPE_DIM // 2) & (m < ROPE_DIM), sin, 0.0)
    return a, b, c


def _rope_apply(t, cos, sin):
    a, b, c = _rope_coeffs(cos, sin)
    n = t.shape[1]
    return a * t + b * pltpu.roll(t, n - ROPE_DIM // 2, 1) + c * pltpu.roll(t, ROPE_DIM // 2, 1)


def _rope_transpose(dt, cos, sin):
    a, b, c = _rope_coeffs(cos, sin)
    n = dt.shape[1]
    return a * dt + pltpu.roll(b * dt, ROPE_DIM // 2, 1) + pltpu.roll(c * dt, n - ROPE_DIM // 2, 1)


def _in_proj(h, w, cos, sin, *, tm, comm=None):
    t, k = h.shape
    s_n, _, n = w.shape
    assert t % tm == 0 and n % LANES == 0

    def body(h_ref, w_ref, cos_ref, sin_ref, o_ref):
        o_ref[...] = jnp.dot(h_ref[...], w_ref[...], preferred_element_type=F32)
        for s in range(s_n):
            rotary_cols = min(max(2 * ATTN_W - s * n, 0), n)
            if rotary_cols:
                @pl.when(pl.program_id(1) == s)
                def _():
                    for c0 in range(0, rotary_cols, LANES):
                        cols = slice(c0, c0 + LANES)
                        o_ref[:, cols] = _rope_apply(o_ref[:, cols], cos_ref[...], sin_ref[...])

    lane_tile = pl.BlockSpec((tm, LANES), lambda i, s: (i, 0))
    out, got = _call(
        body, [h, w, cos, sin], name="in_proj", grid=(t // tm, s_n),
        in_specs=[pl.BlockSpec((tm, k), lambda i, s: (i, 0)), pl.BlockSpec((None, k, n), lambda i, s: (s, 0, 0)),
                  lane_tile, lane_tile],
        out_specs=[pl.BlockSpec((tm, n), lambda i, s: (i, s))],
        out_shape=[jax.ShapeDtypeStruct((t, s_n * n), F32)], comm=comm)
    return out[0], got


def _mm_nt(a, w, *, tm, tko, name, comm=None):
    t, sn = a.shape
    s_n, ko, n = w.shape
    assert sn == s_n * n and t % tm == 0 and ko % tko == 0

    def body(a_ref, w_ref, o_ref):
        acc = lax.dot_general(a_ref[...], w_ref[...], (((1,), (1,)), ((), ())), preferred_element_type=F32)
        if s_n == 1:
            o_ref[...] = acc
        else:
            _accumulate(o_ref, acc, pl.program_id(2) == 0)

    out, got = _call(
        body, [a, w], name=name, grid=(t // tm, ko // tko, s_n),
        in_specs=[pl.BlockSpec((tm, n), lambda i, j, s: (i, s)),
                  pl.BlockSpec((None, tko, n), lambda i, j, s: (s, j, 0))],
        out_specs=[pl.BlockSpec((tm, tko), lambda i, j, s: (i, j))],
        out_shape=[jax.ShapeDtypeStruct((t, ko), F32)], comm=comm)
    return out[0], got


def _dx_through_norms(operands, in_specs, dx_rows, dres, pre, post, *, tm, name, comm=None):
    t, d_model = dres.shape
    (x, gx3, lx), (y, gy3, ly) = pre, post
    n_op = len(operands)
    row = pl.BlockSpec((tm, d_model), lambda i: (i, 0))
    gsum = _const_spec((1, d_model), (0, 0))

    def body(*refs):
        d_ref, x_ref, gx_ref, y_ref, gy_ref, dn_ref, dgx_ref, dy_ref, dgy_ref = refs[n_op:]
        first = pl.program_id(0) == 0
        dx, dgx = _rms_bwd(x_ref[...], gx_ref[...], dx_rows(refs[:n_op]))
        d_new = d_ref[...] + dx
        dn_ref[...] = d_new
        _accumulate(dgx_ref, dgx, first)
        dy, dgy = _rms_bwd(y_ref[...], gy_ref[...], d_new)
        dy_ref[...] = dy.astype(BF16)
        _accumulate(dgy_ref, dgy, first)

    out, got = _call(
        body, list(operands) + [dres, x, gx3, y, gy3], name=name, grid=(t // tm,),
        in_specs=list(in_specs) + [row, row, _gain_spec(gx3, lx), row, _gain_spec(gy3, ly)],
        out_specs=[row, gsum, row, gsum],
        out_shape=[jax.ShapeDtypeStruct((t, d_model), F32), jax.ShapeDtypeStruct((1, d_model), F32),
                   jax.ShapeDtypeStruct((t, d_model), BF16), jax.ShapeDtypeStruct((1, d_model), F32)],
        comm=comm)
    return out[0], out[2], out[1], out[3], got


def _gate_up_dx_norms(dg, du, w, dres, pre, post, *, tm, comm=None):
    s_n, ko, n = w.shape
    half = s_n // 2

    def dx_rows(refs):
        dg_ref, du_ref, w_ref = refs
        acc = jnp.zeros((tm, ko), F32)
        for s in range(half):
            cols = slice(s * n, (s + 1) * n)
            acc = acc + lax.dot_general(dg_ref[:, cols], w_ref[s], LANE_CONTRACT, preferred_element_type=F32)
            acc = acc + lax.dot_general(du_ref[:, cols], w_ref[half + s], LANE_CONTRACT, preferred_element_type=F32)
        return acc

    a_spec = pl.BlockSpec((tm, half * n), lambda i: (i, 0))
    return _dx_through_norms([dg, du, w], [a_spec, a_spec, _const_spec(w.shape, (0, 0, 0))], dx_rows, dres, pre, post,
                             tm=tm, name="gate_up_dx", comm=comm)


def _in_proj_dx_norms(d_attn3, d_conv3, w, dres, pre, post, *, tm, comm=None):
    s_n, ko, n = w.shape
    per = n // CHUNK
    pieces = d_attn3.shape[0]
    width = d_attn3.shape[2]

    def dx_rows(refs):
        a_ref, b_ref, w_ref = refs
        acc = jnp.zeros((tm, ko), F32)
        for c in range(s_n * per):
            src = a_ref if c // 2 < pieces else b_ref
            piece, c0 = (c // 2) % pieces, (c % 2) * CHUNK
            acc = acc + lax.dot_general(src[piece, :, c0:c0 + CHUNK], w_ref[c // per, :, (c % per) * CHUNK:(c % per + 1) * CHUNK],
                                        LANE_CONTRACT, preferred_element_type=F32)
        return acc

    stack = pl.BlockSpec((pieces, tm, width), lambda i: (0, i, 0))
    return _dx_through_norms([d_attn3, d_conv3, w], [stack, stack, _const_spec(w.shape, (0, 0, 0))], dx_rows, dres, pre,
                             post, tm=tm, name="in_proj_dx", comm=comm)


def _mm_tn(a, b, s_n, *, tka, name, into=None, shard0=0):
    t, ka = a.shape
    n = b.shape[1] // s_n
    assert b.shape[0] == t and ka % tka == 0

    def body(a_ref, b_ref, *rest):
        rest[-1][...] = lax.dot_general(a_ref[...], b_ref[...], ROW_CONTRACT, preferred_element_type=F32)

    operands, in_specs, aliases = [a, b], [pl.BlockSpec((t, tka), lambda i, s: (0, i)),
                                           pl.BlockSpec((t, n), lambda i, s: (0, s))], {}
    out_shape = jax.ShapeDtypeStruct((s_n, ka, n), F32)
    if into is not None:
        operands, in_specs, aliases = operands + [into], in_specs + [ANY], {2: 0}
        out_shape = jax.ShapeDtypeStruct(into.shape, F32)
    return pl.pallas_call(
        body, grid=(ka // tka, s_n), in_specs=in_specs,
        out_specs=pl.BlockSpec((None, tka, n), lambda i, s: (shard0 + s, i, 0)),
        out_shape=out_shape, input_output_aliases=aliases, name=name)(*operands)


def _gate_up_swiglu(h, w, *, tm, comm=None):
    t, k = h.shape
    s_n, _, n = w.shape
    half = s_n // 2

    def body(h_ref, wg_ref, wu_ref, g_ref, u_ref, a_ref):
        g = jnp.dot(h_ref[...], wg_ref[...], preferred_element_type=F32)
        u = jnp.dot(h_ref[...], wu_ref[...], preferred_element_type=F32)
        g_ref[...] = g.astype(BF16)
        u_ref[...] = u.astype(BF16)
        a_ref[...] = (g * jax.nn.sigmoid(g) * u).astype(BF16)

    col = pl.BlockSpec((tm, n), lambda i, j: (i, j))
    out, got = _call(
        body, [h, w, w], name="gate_up", grid=(t // tm, half),
        in_specs=[pl.BlockSpec((tm, k), lambda i, j: (i, 0)), pl.BlockSpec((None, k, n), lambda i, j: (j, 0, 0)),
                  pl.BlockSpec((None, k, n), lambda i, j: (half + j, 0, 0))],
        out_specs=[col, col, col],
        out_shape=[jax.ShapeDtypeStruct((t, half * n), BF16)] * 3, comm=comm)
    return out[0], out[1], out[2], got


def _down_dx_swiglu_bwd(df, w, g, u, *, tm, tko):
    t, k = df.shape
    _, ko, _ = w.shape
    assert t % tm == 0 and ko % tko == 0

    def body(df_ref, w_ref, g_ref, u_ref, dg_ref, du_ref):
        d = lax.dot_general(df_ref[...], w_ref[...], LANE_CONTRACT, preferred_element_type=F32)
        gg = g_ref[...].astype(F32)
        sig = jax.nn.sigmoid(gg)
        dg_ref[...] = (d * u_ref[...].astype(F32) * (sig * (1.0 + gg * (1.0 - sig)))).astype(BF16)
        du_ref[...] = (d * (gg * sig)).astype(BF16)

    col = pl.BlockSpec((tm, tko), lambda i, j: (i, j))
    return pl.pallas_call(
        body, grid=(t // tm, ko // tko),
        in_specs=[pl.BlockSpec((tm, k), lambda i, j: (i, 0)), pl.BlockSpec((None, tko, k), lambda i, j: (0, j, 0)),
                  col, col],
        out_specs=[col, col], out_shape=[jax.ShapeDtypeStruct((t, ko), BF16)] * 2, name="down_dx")(df, w, g, u)


def _in_proj_dx(d_attn3, d_conv3, w, *, tm, comm=None):
    _, t, _ = d_attn3.shape
    s_n, ko, n = w.shape
    half, per = s_n // 2, n // CHUNK
    assert t % tm == 0

    def body(*refs):
        a_refs, b_refs, wa_ref, wb_ref, o_ref = refs[:per], refs[per:2 * per], refs[2 * per], refs[2 * per + 1], refs[-1]
        acc = jnp.zeros(o_ref.shape, F32)
        for r in range(per):
            cols = slice(r * CHUNK, (r + 1) * CHUNK)
            acc = acc + lax.dot_general(a_refs[r][...], wa_ref[:, cols], LANE_CONTRACT, preferred_element_type=F32)
            acc = acc + lax.dot_general(b_refs[r][...], wb_ref[:, cols], LANE_CONTRACT, preferred_element_type=F32)
        _accumulate(o_ref, acc, pl.program_id(1) == 0)

    piece = lambda r: pl.BlockSpec((None, tm, CHUNK), lambda i, s: ((per * s + r) // 2, i, (per * s + r) % 2))
    out, got = _call(
        body, [d_attn3] * per + [d_conv3] * per + [w, w], name="in_proj_dx", grid=(t // tm, half),
        in_specs=[piece(r) for r in range(per)] * 2
        + [pl.BlockSpec((None, ko, n), lambda i, s: (s, 0, 0)), pl.BlockSpec((None, ko, n), lambda i, s: (half + s, 0, 0))],
        out_specs=[pl.BlockSpec((tm, ko), lambda i, s: (i, 0))],
        out_shape=[jax.ShapeDtypeStruct((t, ko), F32)], comm=comm)
    return out[0], got


def _in_proj_dw(h, d_attn3, d_conv3, s_n, *, tka):
    t, ka = h.shape
    half = s_n // 2
    n = 3 * d_attn3.shape[2] // half
    per = n // CHUNK
    assert ka % tka == 0

    def body(*refs):
        h_ref, o_ref = refs[0], refs[-1]
        for side in range(2):
            for r in range(per):
                o_ref[side, :, r * CHUNK:(r + 1) * CHUNK] = lax.dot_general(
                    h_ref[...], refs[1 + side * per + r][...], ROW_CONTRACT, preferred_element_type=F32)

    piece = lambda r: pl.BlockSpec((None, t, CHUNK), lambda i, s: ((per * s + r) // 2, 0, (per * s + r) % 2))
    out = pl.pallas_call(
        body, grid=(ka // tka, half),
        in_specs=[pl.BlockSpec((t, tka), lambda i, s: (0, i))] + [piece(r) for r in range(per)] * 2,
        out_specs=pl.BlockSpec((2, None, tka, n), lambda i, s: (0, s, i, 0)),
        out_shape=jax.ShapeDtypeStruct((2, half, ka, n), F32), name="in_proj_dw")(h, *[d_attn3] * per, *[d_conv3] * per)
    return out.reshape(s_n, ka, n)


def _rope_tables(positions_col, inv_freq_row):
    t = positions_col.shape[0]

    def body(pos_ref, f_ref, cos_ref, sin_ref):
        ang = pos_ref[...].astype(F32) * f_ref[...]
        cos_ref[...] = jnp.cos(ang)
        sin_ref[...] = jnp.sin(ang)

    return pl.pallas_call(
        body, out_shape=[jax.ShapeDtypeStruct((t, LANES), F32)] * 2, name="rope_tables")(positions_col, inv_freq_row)


def _norm_fwd(x, g3, l, *, tm):
    t, w = x.shape

    def body(x_ref, g_ref, h_ref):
        h_ref[...] = _rms_fwd(x_ref[...], g_ref[...]).astype(BF16)

    return pl.pallas_call(
        body, grid=(t // tm,),
        in_specs=[pl.BlockSpec((tm, w), lambda i: (i, 0)), _gain_spec(g3, l)],
        out_specs=pl.BlockSpec((tm, w), lambda i: (i, 0)),
        out_shape=jax.ShapeDtypeStruct((t, w), BF16), name="norm_fwd")(x, g3)


def _mm_resnorm(a, w, x, g_post3, l_post, g_next3, l_next, *, tm, name, comm=None):
    t, k = a.shape
    _, _, n = w.shape
    with_next = g_next3 is not None
    row = pl.BlockSpec((tm, n), lambda i: (i, 0))

    def body(a_ref, w_ref, x_ref, gp_ref, *rest):
        y = jnp.dot(a_ref[...], w_ref[...], preferred_element_type=F32)
        x_new = x_ref[...] + _rms_fwd(y, gp_ref[...])
        if with_next:
            gn_ref, y_ref, xo_ref, h_ref = rest
            h_ref[...] = _rms_fwd(x_new, gn_ref[...]).astype(BF16)
        else:
            y_ref, xo_ref = rest
        y_ref[...] = y
        xo_ref[...] = x_new

    ins = [a, w, x, g_post3] + ([g_next3] if with_next else [])
    in_specs = ([pl.BlockSpec((tm, k), lambda i: (i, 0)), _const_spec((None, k, n), (0, 0, 0)), row,
                 _gain_spec(g_post3, l_post)] + ([_gain_spec(g_next3, l_next)] if with_next else []))
    out_shape = [jax.ShapeDtypeStruct((t, n), F32)] * 2 + ([jax.ShapeDtypeStruct((t, n), BF16)] if with_next else [])
    out, got = _call(body, ins, name=name, grid=(t // tm,), in_specs=in_specs, out_specs=[row] * len(out_shape),
                     out_shape=out_shape, comm=comm)
    return out[0], out[1], (out[2] if with_next else None), got


def _conv_fwd(proj, conv_w, l):
    t = proj.shape[0]
    col0 = 3 * ATTN_W // LANES

    def body(u_ref, gb_ref, gc_ref, w_ref, y_ref):
        c = gc_ref[...] * u_ref[...]
        row = lax.broadcasted_iota(jnp.int32, c.shape, 0)
        c_prev = jnp.where(row == 0, 0.0, pltpu.roll(c, 1, 0))
        c_next = jnp.where(row == t - 1, 0.0, pltpu.roll(c, t - 1, 0))
        w = w_ref[...]
        y_ref[...] = gb_ref[...] * (w[0:1] * c_prev + w[1:2] * c + w[2:3] * c_next)

    nj = CONV_W // LANES
    cols = lambda base: pl.BlockSpec((t, LANES), lambda j: (0, base + j))
    return pl.pallas_call(
        body, grid=(nj,),
        in_specs=[cols(col0), cols(col0 + nj), cols(col0 + 2 * nj),
                  pl.BlockSpec((None, None, 3, LANES), lambda j: (l, j, 0, 0))],
        out_specs=pl.BlockSpec((t, LANES), lambda j: (0, j)),
        out_shape=jax.ShapeDtypeStruct((t, CONV_W), F32), name="conv_fwd")(proj, proj, proj, conv_w)


def _merge_fwd(attn, conv_y, ga3, gc3, l, *, tm):
    t = attn.shape[0]
    row = pl.BlockSpec((tm, ATTN_W), lambda i: (i, 0))

    def body(a_ref, c_ref, ga_ref, gc_ref, m_ref):
        m_ref[:, :ATTN_W] = _rms_fwd(a_ref[...], ga_ref[...]).astype(BF16)
        m_ref[:, ATTN_W:] = _rms_fwd(c_ref[...], gc_ref[...]).astype(BF16)

    return pl.pallas_call(
        body, grid=(t // tm,),
        in_specs=[row, row, _gain_spec(ga3, l), _gain_spec(gc3, l)],
        out_specs=pl.BlockSpec((tm, D_MODEL), lambda i: (i, 0)),
        out_shape=jax.ShapeDtypeStruct((t, D_MODEL), BF16), name="merge_fwd")(attn, conv_y, ga3, gc3)


def _loss_fwd_bwd(y, target, *, tm):
    t, w = y.shape
    row = pl.BlockSpec((tm, w), lambda i: (i, 0))

    def body(y_ref, t_ref, dy_ref, loss_ref):
        e = y_ref[...] - t_ref[...]
        dy_ref[...] = e * (1.0 / w)
        sq = jnp.sum(e * e, axis=0, keepdims=True) * (0.5 / w)
        part = sq[:, :LANES]
        for j in range(1, w // LANES):
            part = part + sq[:, j * LANES:(j + 1) * LANES]
        _accumulate(loss_ref, part, pl.program_id(0) == 0)

    return pl.pallas_call(
        body, grid=(t // tm,), in_specs=[row, row],
        out_specs=[row, _const_spec((1, LANES), (0, 0))],
        out_shape=[jax.ShapeDtypeStruct((t, w), F32), jax.ShapeDtypeStruct((1, LANES), F32)], name="loss")(y, target)


def _tile_rows(t, nt, lb, d):
    r = t // nt
    q0 = (t % nt) * TQ
    m0 = jnp.clip(q0 - BAND, 0, lb - WIN)
    if d == 1:
        return pl.ds(pl.multiple_of(q0, TQ), TQ), pl.ds(pl.multiple_of(m0, BAND), WIN), m0 - q0
    return pl.ds(r + d * q0, TQ, stride=d), pl.ds(r + d * m0, WIN, stride=d), m0 - q0


def _for_row_chunks(t, fn, chunk=512):
    def step(i, carry):
        fn(pl.ds(pl.multiple_of(i * chunk, chunk), chunk))
        return carry

    lax.fori_loop(0, t // chunk, step, 0)


WINDOW_OFFSETS = (-BAND, 0, -2 * BAND)


def _fill_band_bias(bias_ref):
    rel0 = (lax.broadcasted_iota(jnp.int32, (2 * TQ, WIN), 1)
            - lax.broadcasted_iota(jnp.int32, (2 * TQ, WIN), 0) % TQ)
    for j, off in enumerate(WINDOW_OFFSETS):
        rel = rel0 + off
        bias_ref[j] = jnp.where((rel >= -BAND) & (rel <= BAND), 0.0, NEG_INF)


def _fill_sequence_bias(bias_ref):
    rel = (lax.broadcasted_iota(jnp.int32, (2 * WIN, WIN), 1) - lax.broadcasted_iota(jnp.int32, (2 * WIN, WIN), 0) % WIN)
    bias_ref[...] = jnp.where((rel >= -BAND) & (rel <= BAND), 0.0, NEG_INF)


def _band_bias(bias_ref, off):
    return bias_ref[jnp.where(off == WINDOW_OFFSETS[0], 0, jnp.where(off == WINDOW_OFFSETS[1], 1, 2))]


def _stack_heads(a, first_head):
    return jnp.concatenate([jnp.where(first_head, a, 0.0), jnp.where(first_head, 0.0, a)], axis=0)


def _unstack_heads(a2, first_head):
    n = a2.shape[0] // 2
    return jnp.where(first_head, a2[:n], a2[n:])


def _attn_fwd(proj, comm=None):
    t = proj.shape[0]
    npair = ATTN_W // LANES

    def body(q_ref, k_ref, v_ref, o_ref, lse_ref, o1, o2, l0, l1, l2, m1, m2, bias, bias_seq):
        _fill_band_bias(bias)
        _fill_sequence_bias(bias_seq)
        outs, dens, maxs = (o_ref, o1, o2), (l0, l1, l2), (lse_ref, m1, m2)

        def softmax_tile(b, qrows, krows, n_q, band_bias):
            first_head = lax.broadcasted_iota(jnp.int32, (n_q, LANES), 1) < HEAD_DIM
            q2 = _stack_heads(q_ref[qrows, :] * SCALE, first_head).astype(BF16)
            kw = k_ref[krows, :].astype(BF16)
            vw = jnp.concatenate([v_ref[krows, :].astype(BF16), jnp.ones((WIN, LANES), BF16)], axis=1)
            s = lax.dot_general(q2, kw, LANE_CONTRACT, preferred_element_type=F32) + band_bias
            m = jnp.max(s, axis=-1, keepdims=True)
            pv = jnp.dot(jnp.exp(s - m).astype(BF16), vw, preferred_element_type=F32)
            outs[b][qrows, :] = _unstack_heads(pv[:, :LANES], first_head)
            dens[b][qrows, :] = _unstack_heads(pv[:, LANES:], first_head)
            maxs[b][qrows, :] = _unstack_heads(jnp.broadcast_to(m, (2 * n_q, LANES)), first_head)

        for b, d in enumerate(DILATIONS):
            lb = t // d
            if lb == WIN:
                def sequence(r, carry, b=b, d=d):
                    rows = pl.ds(r, WIN, stride=d)
                    softmax_tile(b, rows, rows, WIN, bias_seq[...])
                    return carry

                lax.fori_loop(0, d, sequence, 0, unroll=4)
                continue
            nt = lb // TQ

            def tile(ti, carry, b=b, d=d, lb=lb, nt=nt):
                qrows, krows, off = _tile_rows(ti, nt, lb, d)
                softmax_tile(b, qrows, krows, TQ, _band_bias(bias, off))
                return carry

            lax.fori_loop(0, d * nt, tile, 0, unroll=8)

        def finish(rows):
            ms = [m_b[rows, :] for m_b in maxs]
            m_all = jnp.maximum(jnp.maximum(ms[0], ms[1]), ms[2])
            ws = [jnp.exp(m_b - m_all) for m_b in ms]
            den = ws[0] * dens[0][rows, :] + ws[1] * dens[1][rows, :] + ws[2] * dens[2][rows, :]
            num = ws[0] * outs[0][rows, :] + ws[1] * outs[1][rows, :] + ws[2] * outs[2][rows, :]
            o_ref[rows, :] = num / den
            lse_ref[rows, :] = m_all + jnp.log(den)

        _for_row_chunks(t, finish, 256)

    cols = lambda base: pl.BlockSpec((t, LANES), lambda g: (0, base + g))
    out, got = _call(
        body, [proj, proj, proj], name="attn_fwd", grid=(npair,),
        in_specs=[cols(0), cols(npair), cols(2 * npair)],
        out_specs=[cols(0), cols(0)],
        out_shape=[jax.ShapeDtypeStruct((t, ATTN_W), F32)] * 2,
        scratch_shapes=[pltpu.VMEM((t, LANES), F32)] * 7 + [pltpu.VMEM((len(WINDOW_OFFSETS), 2 * TQ, WIN), F32),
                                                            pltpu.VMEM((2 * WIN, WIN), F32)],
        comm=comm)
    return out[0], out[1], got


def _attn_bwd(proj, cos, sin, d_attn, lse, delta, comm=None):
    t = proj.shape[0]
    npair = ATTN_W // LANES

    def body(q_ref, k_ref, v_ref, cos_ref, sin_ref, do_ref, l_ref, dl_ref, dqkv_ref,
             dq_acc, dk_acc, dv_acc, bias, bias_seq):
        _fill_band_bias(bias)
        _fill_sequence_bias(bias_seq)
        dq_acc[...] = jnp.zeros(dq_acc.shape, F32)
        dk_acc[...] = jnp.zeros(dk_acc.shape, F32)
        dv_acc[...] = jnp.zeros(dv_acc.shape, F32)
        def stack_column(a):
            return jnp.concatenate([a[:, 0:1], a[:, HEAD_DIM:HEAD_DIM + 1]], axis=0)

        def grad_tile(qrows, krows, n_q, band_bias):
            first_head = lax.broadcasted_iota(jnp.int32, (n_q, LANES), 1) < HEAD_DIM
            q2 = _stack_heads(q_ref[qrows, :] * SCALE, first_head).astype(BF16)
            do2 = _stack_heads(do_ref[qrows, :], first_head).astype(BF16)
            kw = k_ref[krows, :].astype(BF16)
            vw = v_ref[krows, :].astype(BF16)
            s = lax.dot_general(q2, kw, LANE_CONTRACT, preferred_element_type=F32) + band_bias
            p = jnp.exp(s - stack_column(l_ref[qrows, :]))
            dp = lax.dot_general(do2, vw, LANE_CONTRACT, preferred_element_type=F32)
            ds = (p * (dp - stack_column(dl_ref[qrows, :]))).astype(BF16)
            dq2 = jnp.dot(ds, kw, preferred_element_type=F32)
            dq_acc[qrows, :] += _unstack_heads(dq2, first_head) * SCALE
            dk_acc[krows, :] += lax.dot_general(ds, q2, ROW_CONTRACT, preferred_element_type=F32)
            dv_acc[krows, :] += lax.dot_general(p.astype(BF16), do2, ROW_CONTRACT, preferred_element_type=F32)

        for d in DILATIONS:
            lb = t // d
            if lb == WIN:
                def sequence(r, carry, d=d):
                    rows = pl.ds(r, WIN, stride=d)
                    grad_tile(rows, rows, WIN, bias_seq[...])
                    return carry

                lax.fori_loop(0, d, sequence, 0, unroll=2)
                continue
            nt = lb // TQ

            def tile(ti, carry, d=d, lb=lb, nt=nt):
                qrows, krows, off = _tile_rows(ti, nt, lb, d)
                grad_tile(qrows, krows, TQ, _band_bias(bias, off))
                return carry

            lax.fori_loop(0, d * nt, tile, 0, unroll=4)

        def finish(rows):
            dqkv_ref[0, rows, :] = _rope_transpose(dq_acc[rows, :], cos_ref[rows, :], sin_ref[rows, :]).astype(BF16)
            dqkv_ref[1, rows, :] = _rope_transpose(dk_acc[rows, :], cos_ref[rows, :], sin_ref[rows, :]).astype(BF16)
            dqkv_ref[2, rows, :] = dv_acc[rows, :].astype(BF16)

        _for_row_chunks(t, finish)

    cols = lambda base: pl.BlockSpec((t, LANES), lambda g: (0, base + g))
    out, got = _call(
        body, [proj, proj, proj, cos, sin, d_attn, lse, delta], name="attn_bwd", grid=(npair,),
        in_specs=[cols(0), cols(npair), cols(2 * npair), WHOLE_VMEM, WHOLE_VMEM, cols(0), cols(0), cols(0)],
        out_specs=[pl.BlockSpec((3, t, LANES), lambda g: (0, 0, g))],
        out_shape=[jax.ShapeDtypeStruct((3, t, ATTN_W), BF16)],
        scratch_shapes=[pltpu.VMEM((t, LANES), F32)] * 3 + [pltpu.VMEM((len(WINDOW_OFFSETS), 2 * TQ, WIN), F32),
                                                            pltpu.VMEM((2 * WIN, WIN), F32)],
        comm=comm)
    return out[0], got


def _norm_bwd(dres, pre, post, *, tm, comm=None):
    t, w = dres.shape
    row = pl.BlockSpec((tm, w), lambda i: (i, 0))
    gsum = _const_spec((1, w), (0, 0))
    ins, in_specs, out_shape, out_specs = [dres], [row], [], []
    if pre is not None:
        dh, x, g3, l = pre
        ins += [dh, x, g3]
        in_specs += [row, row, _gain_spec(g3, l)]
        out_shape += [jax.ShapeDtypeStruct((t, w), F32), jax.ShapeDtypeStruct((1, w), F32)]
        out_specs += [row, gsum]
    if post is not None:
        y, g3, l = post
        ins += [y, g3]
        in_specs += [row, _gain_spec(g3, l)]
        out_shape += [jax.ShapeDtypeStruct((t, w), BF16), jax.ShapeDtypeStruct((1, w), F32)]
        out_specs += [row, gsum]
    n_in = len(ins)

    def body(*refs):
        first = pl.program_id(0) == 0
        ins_r, outs_r = list(refs[:n_in]), list(refs[n_in:])
        d = ins_r.pop(0)[...]
        if pre is not None:
            dh_ref, x_ref, g_ref = ins_r[:3]
            ins_r = ins_r[3:]
            dx, dg = _rms_bwd(x_ref[...], g_ref[...], dh_ref[...])
            d = d + dx
            outs_r.pop(0)[...] = d
            _accumulate(outs_r.pop(0), dg, first)
        if post is not None:
            y_ref, g_ref = ins_r
            dy, dg = _rms_bwd(y_ref[...], g_ref[...], d)
            outs_r.pop(0)[...] = dy.astype(BF16)
            _accumulate(outs_r.pop(0), dg, first)

    out, got = _call(body, ins, name="norm_bwd", grid=(t // tm,), in_specs=in_specs, out_specs=out_specs,
                     out_shape=out_shape, comm=comm)
    d_new, dg_pre = (out.pop(0), out.pop(0)) if pre is not None else (None, None)
    dy, dg_post = (out.pop(0), out.pop(0)) if post is not None else (None, None)
    return d_new, dy, dg_pre, dg_post, got


def _merge_bwd(d_merged, attn, conv_y, ga3, gc3, l, *, tm, comm=None):
    t = attn.shape[0]
    row = pl.BlockSpec((tm, ATTN_W), lambda i: (i, 0))
    gsum = _const_spec((1, ATTN_W), (0, 0))

    def body(dma_ref, dmc_ref, a_ref, c_ref, ga_ref, gc_ref, da_ref, dl_ref, dc_ref, dga_ref, dgc_ref):
        first = pl.program_id(0) == 0
        attn_t = a_ref[...]
        da, dga = _rms_bwd(attn_t, ga_ref[...], dma_ref[...])
        dc, dgc = _rms_bwd(c_ref[...], gc_ref[...], dmc_ref[...])
        da_ref[...] = da
        dc_ref[...] = dc
        same_head = (lax.broadcasted_iota(jnp.int32, (ATTN_W, ATTN_W), 0) // HEAD_DIM
                     == lax.broadcasted_iota(jnp.int32, (ATTN_W, ATTN_W), 1) // HEAD_DIM).astype(BF16)
        rest = da * attn_t
        total = jnp.zeros(rest.shape, F32)
        for _ in range(3):
            term = rest.astype(BF16)
            total = total + jnp.dot(term, same_head, preferred_element_type=F32)
            rest = rest - term.astype(F32)
        dl_ref[...] = total
        _accumulate(dga_ref, dga, first)
        _accumulate(dgc_ref, dgc, first)

    out, got = _call(
        body, [d_merged, d_merged, attn, conv_y, ga3, gc3], name="merge_bwd", grid=(t // tm,),
        in_specs=[pl.BlockSpec((tm, ATTN_W), lambda i: (i, 0)), pl.BlockSpec((tm, CONV_W), lambda i: (i, 1)),
                  row, row, _gain_spec(ga3, l), _gain_spec(gc3, l)],
        out_specs=[row, row, row, gsum, gsum],
        out_shape=[jax.ShapeDtypeStruct((t, ATTN_W), F32)] * 3 + [jax.ShapeDtypeStruct((1, ATTN_W), F32)] * 2,
        comm=comm)
    return (*out, got)


def _conv_bwd(proj, conv_w, l, d_conv_y):
    t = proj.shape[0]
    col0 = 3 * ATTN_W // LANES
    nj = CONV_W // LANES

    def body(u_ref, gb_ref, gc_ref, w_ref, dy_ref, d3_ref, dw_ref):
        u, gc, dy = u_ref[...], gc_ref[...], dy_ref[...]
        row = lax.broadcasted_iota(jnp.int32, u.shape, 0)
        down = lambda a: jnp.where(row == 0, 0.0, pltpu.roll(a, 1, 0))
        up = lambda a: jnp.where(row == t - 1, 0.0, pltpu.roll(a, t - 1, 0))
        w = w_ref[...]
        c = gc * u
        c_prev, c_next = down(c), up(c)
        d3_ref[1] = (dy * (w[0:1] * c_prev + w[1:2] * c + w[2:3] * c_next)).astype(BF16)
        dz = dy * gb_ref[...]
        dc = w[0:1] * up(dz) + w[1:2] * dz + w[2:3] * down(dz)
        d3_ref[0] = (dc * gc).astype(BF16)
        d3_ref[2] = (dc * u).astype(BF16)
        dw_ref[0:1, :] = jnp.sum(dz * c_prev, axis=0, keepdims=True)
        dw_ref[1:2, :] = jnp.sum(dz * c, axis=0, keepdims=True)
        dw_ref[2:3, :] = jnp.sum(dz * c_next, axis=0, keepdims=True)

    cols = lambda base: pl.BlockSpec((t, LANES), lambda j: (0, base + j))
    return pl.pallas_call(
        body, grid=(nj,),
        in_specs=[cols(col0), cols(col0 + nj), cols(col0 + 2 * nj),
                  pl.BlockSpec((None, None, 3, LANES), lambda j: (l, j, 0, 0)), cols(0)],
        out_specs=[pl.BlockSpec((3, t, LANES), lambda j: (0, 0, j)), pl.BlockSpec((None, 3, LANES), lambda j: (j, 0, 0))],
        out_shape=[jax.ShapeDtypeStruct((3, t, CONV_W), BF16), jax.ShapeDtypeStruct((nj, 3, LANES), F32)],
        name="conv_bwd")(proj, proj, proj, conv_w, d_conv_y)


def _own_shard_slab(w, l, place, dtype):
    _, rows, cols = w.shape
    tr = rows if rows <= 704 else 512
    assert rows % tr == 0

    def body(p_ref, w_ref, o_ref):
        del p_ref
        o_ref[...] = w_ref[...].astype(dtype)

    grid_spec = pltpu.PrefetchScalarGridSpec(
        num_scalar_prefetch=1, grid=(rows // tr,),
        in_specs=[pl.BlockSpec((None, tr, cols), lambda i, p: (l, i, 0))],
        out_specs=pl.BlockSpec((None, tr, cols), lambda i, p: (p[0], i, 0)))
    return pl.pallas_call(body, grid_spec=grid_spec, name="own_shard_slab",
                          out_shape=jax.ShapeDtypeStruct((N_CHIPS, rows, cols), dtype))(place, w)


def _own_conv_slab(w, place):
    depth = w.shape[0]

    def body(p_ref, w_ref, o_ref):
        del p_ref
        o_ref[...] = w_ref[...]

    grid_spec = pltpu.PrefetchScalarGridSpec(
        num_scalar_prefetch=1, grid=(depth,),
        in_specs=[pl.BlockSpec((None, 3, LANES), lambda l, p: (l, 0, 0))],
        out_specs=pl.BlockSpec((None, None, 3, LANES), lambda l, p: (l, p[0], 0, 0)))
    return pl.pallas_call(body, grid_spec=grid_spec, name="own_conv_slab",
                          out_shape=jax.ShapeDtypeStruct((depth, N_CHIPS, 3, LANES), F32))(place, w)


def _add_halves(grad, got, place):
    s_n, rows, cols = grad.shape
    hr = rows // 2

    def body(p_ref, g_ref, r_ref, o_ref):
        del p_ref
        o_ref[...] = (g_ref[...] + r_ref[...]).astype(BF16)

    grid_spec = pltpu.PrefetchScalarGridSpec(
        num_scalar_prefetch=1, grid=(s_n,),
        in_specs=[pl.BlockSpec((None, hr, cols), lambda s, p: (s, p[1], 0)),
                  pl.BlockSpec((None, hr, cols), lambda s, p: (s, 0, 0))],
        out_specs=pl.BlockSpec((None, hr, cols), lambda s, p: (s, 0, 0)))
    return pl.pallas_call(body, grid_spec=grid_spec, out_shape=jax.ShapeDtypeStruct((s_n, hr, cols), BF16),
                          name="add_halves")(place, grad, got)


def _sum_partials(partial, got, place, acc, l):
    _, hr, cols = partial.shape

    def body(p_ref, mine_ref, got_ref, acc_ref, o_ref):
        del p_ref, acc_ref
        total = mine_ref[...].astype(F32)
        for k in range(3):
            total = total + got_ref[k].astype(F32)
        o_ref[...] = total

    grid_spec = pltpu.PrefetchScalarGridSpec(
        num_scalar_prefetch=1, grid=(1,),
        in_specs=[pl.BlockSpec((None, hr, cols), lambda i, p: (p[0], 0, 0)),
                  pl.BlockSpec((3, hr, cols), lambda i, p: (0, 0, 0)), ANY],
        out_specs=pl.BlockSpec((None, hr, cols), lambda i, p: (l, p[1], 0)))
    return pl.pallas_call(body, grid_spec=grid_spec, out_shape=jax.ShapeDtypeStruct(acc.shape, F32),
                          input_output_aliases={3: 0}, name="sum_partials")(place, partial, got, acc)


def _allreduce_small(vec, loss_row):
    rows = vec.shape[0]

    def body(v_ref, o_ref, slots, send_sems, recv_sems):
        x, y, c, _ = _place()
        me = 4 * x + 2 * y + c
        slots[me] = v_ref[...]
        copies = []
        for k in range(1, N_DEV):
            flip = lambda v, bit: 1 - v if bit else v
            peer = (flip(x, k & 4), flip(y, k & 2), flip(c, k & 1))
            copies.append(_remote(v_ref, slots.at[me], send_sems.at[k - 1], recv_sems.at[k - 1], peer))
        for cp in copies:
            cp.start()
        for k in range(1, N_DEV):
            flip = lambda v, bit: 1 - v if bit else v
            peer_id = 4 * flip(x, k & 4) + 2 * flip(y, k & 2) + flip(c, k & 1)
            _remote(v_ref, slots.at[peer_id], send_sems.at[k - 1], recv_sems.at[k - 1], (x, y, c)).wait_recv()
        for cp in copies:
            cp.wait_send()
        total = slots[0]
        for dev in range(1, N_DEV):
            total = total + slots[dev]
        o_ref[...] = total
        o_ref[loss_row:loss_row + 1, :] = jnp.broadcast_to(
            jnp.sum(total[loss_row:loss_row + 1, :], axis=-1, keepdims=True), (1, LANES))

    return pl.pallas_call(
        body, in_specs=[WHOLE_VMEM], out_specs=WHOLE_VMEM, out_shape=jax.ShapeDtypeStruct((rows, LANES), F32),
        scratch_shapes=[pltpu.VMEM((N_DEV, rows, LANES), F32), pltpu.SemaphoreType.DMA((N_DEV - 1,)),
                        pltpu.SemaphoreType.DMA((N_DEV - 1,))],
        name="allreduce_small")(vec)


def _adamw(w, g, m, v, *, tr, emit_grad=False):
    depth, rows, cols = w.shape
    assert rows % tr == 0
    c1 = float(np.float32(1.0 - ADAM_B1 ** ADAM_STEP))
    c2 = float(np.float32(1.0 - ADAM_B2 ** ADAM_STEP))

    def body(w_ref, g_ref, m_ref, v_ref, d_ref, mo_ref, vo_ref, *go_ref):
        g_t = g_ref[...]
        if emit_grad:
            go_ref[0][...] = g_t
        m_new = ADAM_B1 * m_ref[...] + (1.0 - ADAM_B1) * g_t
        v_new = ADAM_B2 * v_ref[...] + (1.0 - ADAM_B2) * (g_t * g_t)
        mo_ref[...] = m_new
        vo_ref[...] = v_new
        d_ref[...] = -ADAM_LR * ((m_new / c1) / (jnp.sqrt(v_new / c2) + ADAM_EPS) + ADAM_WD * w_ref[...])

    blk = pl.BlockSpec((None, tr, cols), lambda l, i: (l, i, 0))
    return pl.pallas_call(
        body, grid=(depth, rows // tr), in_specs=[blk] * 4, out_specs=[blk] * (4 if emit_grad else 3),
        out_shape=[jax.ShapeDtypeStruct(w.shape, F32)] * (4 if emit_grad else 3), name="adamw")(w, g, m, v)


def _local_step(x, positions, target, gains, exchange):
    t = x.shape[0]
    tm = 512
    inv_freq = ROPE_THETA ** (-jnp.arange(0, ROPE_DIM, 2, dtype=F32) / ROPE_DIM)
    lane = np.arange(LANES) % HEAD_DIM
    freq_row = jnp.where(lane < ROPE_DIM, inv_freq[lane % (ROPE_DIM // 2)], 0.0).astype(F32)[None, :]
    cos, sin = _rope_tables(positions.reshape(t, 1), freq_row)

    def hosted(tag, fn, *args, **kwargs):
        *out, got = fn(*args, comm=exchange.host(tag), **kwargs)
        if got is not None:
            exchange.hosted(tag, got)
        return out[0] if len(out) == 1 else out

    saved = []
    h1 = _norm_fwd(x, gains["pre_mix_norm"], 0, tm=tm)
    for l in range(DEPTH):
        proj = hosted(("fwd", l, "in_proj"), _in_proj, h1, exchange.weight("w_in", l), cos, sin, tm=1024)
        attn, lse = hosted(("fwd", l, "attn"), _attn_fwd, proj)
        conv_y = _conv_fwd(proj, exchange.weight("conv_w", l), l)
        merged = _merge_fwd(attn, conv_y, gains["attn_out_norm"], gains["conv_out_norm"], l, tm=tm)
        mix, x1, h2 = hosted(("fwd", l, "out_proj"), _mm_resnorm, merged, exchange.weight("w_out", l), x,
                             gains["post_mix_norm"], l, gains["pre_ffn_norm"], l, tm=tm, name="out_proj")
        g, u, act = hosted(("fwd", l, "gate_up"), _gate_up_swiglu, h2, exchange.weight("w_gate_up", l), tm=1024)
        nxt = (gains["pre_mix_norm"], l + 1) if l + 1 < DEPTH else (None, None)
        f, x2, h1_next = hosted(("fwd", l, "down"), _mm_resnorm, act, exchange.weight("w_down", l), x1,
                                gains["post_ffn_norm"], l, *nxt, tm=tm, name="down")
        saved.append(dict(x=x, h1=h1, proj=proj, attn=attn, lse=lse, conv_y=conv_y, merged=merged, mix=mix,
                          x1=x1, h2=h2, g=g, u=u, act=act, f=f))
        x, h1 = x2, h1_next

    dres, loss_lanes = _loss_fwd_bwd(x, target, tm=tm)

    g_gain = {k: [None] * DEPTH for k in gains}
    g_conv = [None] * DEPTH
    _, df, _, g_gain["post_ffn_norm"][DEPTH - 1], _ = _norm_bwd(
        dres, None, (saved[-1]["f"], gains["post_ffn_norm"], DEPTH - 1), tm=tm)
    for l in reversed(range(DEPTH)):
        sv = saved[l]
        w = {k: exchange.weight(k, l) for k in MATRIX_NAMES + ("conv_w",)}
        dg, du = _down_dx_swiglu_bwd(df, w["w_down"], sv["g"], sv["u"], tm=1024, tko=FFN // 2)
        g_down = _mm_tn(sv["act"], df, 1, tka=256, name="down_dw")
        dx1, dmix, g_gain["pre_ffn_norm"][l], g_gain["post_mix_norm"][l] = hosted(
            ("bwd", l, "gate_up_dx"), _gate_up_dx_norms, dg, du, w["w_gate_up"], dres,
            (sv["x1"], gains["pre_ffn_norm"], l), (sv["mix"], gains["post_mix_norm"], l), tm=tm)
        g_gate_up = _mm_tn(sv["h2"], dg, N_CHIPS // 2, tka=512, name="gate_up_dw",
                           into=lax.empty(w["w_gate_up"].shape, F32))
        g_gate_up = _mm_tn(sv["h2"], du, N_CHIPS // 2, tka=512, name="gate_up_dw", into=g_gate_up,
                           shard0=N_CHIPS // 2)
        exchange.grads(l, "ffn", dict(w_down=g_down.reshape(N_CHIPS, FFN // N_CHIPS, D_MODEL), w_gate_up=g_gate_up))
        d_merged = hosted(("bwd", l, "out_proj_dx"), _mm_nt, dmix, w["w_out"], tm=1024, tko=D_MODEL, name="out_proj_dx")
        g_out = _mm_tn(sv["merged"], dmix, 1, tka=512, name="out_proj_dw")
        d_attn, delta, d_conv_y, g_gain["attn_out_norm"][l], g_gain["conv_out_norm"][l] = hosted(
            ("bwd", l, "merge"), _merge_bwd,
            d_merged, sv["attn"], sv["conv_y"], gains["attn_out_norm"], gains["conv_out_norm"], l, tm=tm)
        d_attn3 = hosted(("bwd", l, "attn"), _attn_bwd, sv["proj"], cos, sin, d_attn, sv["lse"], delta)
        d_conv3, g_conv[l] = _conv_bwd(sv["proj"], w["conv_w"], l, d_conv_y)
        g_in = _in_proj_dw(sv["h1"], d_attn3, d_conv3, N_CHIPS, tka=512)
        exchange.grads(l, "mix", dict(w_out=g_out.reshape(N_CHIPS, D_MODEL // N_CHIPS, D_MODEL), w_in=g_in))
        if l > 0:
            dres, df, g_gain["pre_mix_norm"][l], g_gain["post_ffn_norm"][l - 1] = hosted(
                ("bwd", l, "in_proj_dx"), _in_proj_dx_norms, d_attn3, d_conv3, w["w_in"], dx1,
                (sv["x"], gains["pre_mix_norm"], l), (saved[l - 1]["f"], gains["post_ffn_norm"], l - 1), tm=tm)
        else:
            dh1 = hosted(("bwd", l, "in_proj_dx"), _in_proj_dx, d_attn3, d_conv3, w["w_in"], tm=1024)
            dres, _, g_gain["pre_mix_norm"][l], _ = hosted(
                ("bwd", l, "norm_low"), _norm_bwd, dx1, (dh1, sv["x"], gains["pre_mix_norm"], l), None, tm=tm)

    g_gain = {k: jnp.concatenate(v, axis=0) for k, v in g_gain.items()}
    return loss_lanes, dres, g_gain, jnp.stack(g_conv, axis=0)


class _Exchange:
    GATHER_HOSTS = {"in_proj": (("w_out", 0), ("w_down", 0)), "attn": (("w_gate_up", 0),), "gate_up": (("w_in", 1),)}

    @staticmethod
    def _reduce_hosts(group, l):
        if group == "ffn":
            return "merge", "attn", l
        if l > 0:
            return "in_proj_dx", "gate_up_dx", l - 1
        return "in_proj_dx", "norm_low", l

    def __init__(self, params, place):
        self.place = place
        self.slabs = {k: [_own_shard_slab(params[k], l, place, BF16) for l in range(DEPTH)] for k in MATRIX_NAMES}
        self.gathered = {k: [None] * DEPTH for k in MATRIX_NAMES}
        self.gathered["w_in"][0], self.conv_w = _run_comm(
            _gather_comm([self.slabs["w_in"][0]], _own_conv_slab(params["conv_w"], place)), "gather_first")
        self.full = {k: lax.empty(params[k].shape, F32) for k in MATRIX_NAMES}
        self.pending = {}
        self.raw = {}

    def weight(self, name, l):
        if name == "conv_w":
            return self.conv_w
        g = self.gathered[name][l]
        return g.reshape(1, g.shape[0] * g.shape[1], g.shape[2]) if name in ("w_out", "w_down") else g

    def host(self, tag):
        phase, l, kernel = tag
        if phase == "fwd":
            carried = [(name, l + ahead) for name, ahead in self.GATHER_HOSTS.get(kernel, ()) if l + ahead < DEPTH]
            return _gather_comm([self.slabs[name][layer] for name, layer in carried]) if carried else None
        if tag in self.pending:
            stage, _, _, arrays = self.pending[tag]
            return _halves_comm(arrays) if stage == "halves" else _partials_comm(arrays)
        return None

    def hosted(self, tag, results):
        phase, l, kernel = tag
        if phase == "fwd":
            carried = [(name, l + ahead) for name, ahead in self.GATHER_HOSTS[kernel] if l + ahead < DEPTH]
            for (name, layer), slab in zip(carried, results):
                self.gathered[name][layer] = slab
            return
        stage, gl, group, arrays = self.pending.pop(tag)
        names = list(self.raw[(gl, group)])
        if stage == "partials":
            self._finish_reduction(gl, names, arrays, results)
            return
        partials = [_add_halves(self.raw[(gl, group)][k], r, self.place) for k, r in zip(names, results)]
        _, ici_kernel, ici_layer = self._reduce_hosts(group, gl)
        self.pending[("bwd", ici_layer, ici_kernel)] = ("partials", gl, group, partials)

    def grads(self, l, group, grads):
        self.raw[(l, group)] = grads
        self.pending[("bwd", l, self._reduce_hosts(group, l)[0])] = ("halves", l, group, [grads[k] for k in grads])

    def _finish_reduction(self, l, names, partials, others):
        for k, p, q in zip(names, partials, others):
            self.full[k] = _sum_partials(p, q, self.place, self.full[k], l)
        shared = _run_comm(_share_comm([self.full[k] for k in names], l), "share_halves")
        for k, g in zip(names, shared):
            self.full[k] = g


def kernel(x, positions, pre_mix_norm, w_in, conv_w, attn_out_norm, conv_out_norm, w_out, post_mix_norm, pre_ffn_norm, w_gate_up, w_down, post_ffn_norm, loss_target, m_pre_mix_norm, m_w_in, m_conv_w, m_attn_out_norm, m_conv_out_norm, m_w_out, m_post_mix_norm, m_pre_ffn_norm, m_w_gate_up, m_w_down, m_post_ffn_norm, v_pre_mix_norm, v_w_in, v_conv_w, v_attn_out_norm, v_conv_out_norm, v_w_out, v_post_mix_norm, v_pre_ffn_norm, v_w_gate_up, v_w_down, v_post_ffn_norm):
    params = dict(pre_mix_norm=pre_mix_norm, w_in=w_in, conv_w=conv_w, attn_out_norm=attn_out_norm,
                  conv_out_norm=conv_out_norm, w_out=w_out, post_mix_norm=post_mix_norm, pre_ffn_norm=pre_ffn_norm,
                  w_gate_up=w_gate_up, w_down=w_down, post_ffn_norm=post_ffn_norm)
    mom1 = dict(pre_mix_norm=m_pre_mix_norm, w_in=m_w_in, conv_w=m_conv_w, attn_out_norm=m_attn_out_norm,
                conv_out_norm=m_conv_out_norm, w_out=m_w_out, post_mix_norm=m_post_mix_norm,
                pre_ffn_norm=m_pre_ffn_norm, w_gate_up=m_w_gate_up, w_down=m_w_down, post_ffn_norm=m_post_ffn_norm)
    mom2 = dict(pre_mix_norm=v_pre_mix_norm, w_in=v_w_in, conv_w=v_conv_w, attn_out_norm=v_attn_out_norm,
                conv_out_norm=v_conv_out_norm, w_out=v_w_out, post_mix_norm=v_post_mix_norm,
                pre_ffn_norm=v_pre_ffn_norm, w_gate_up=v_w_gate_up, w_down=v_w_down, post_ffn_norm=v_post_ffn_norm)
    xi, yi, ci = lax.axis_index("x"), lax.axis_index("y"), lax.axis_index("c")
    place = jnp.stack([2 * xi + yi, ci]).astype(jnp.int32)

    exchange = _Exchange(params, place)
    gains = {k: params[k][:, None, :] for k in GAIN_NAMES}
    loss_lanes, grad_x, g_gain, g_conv = _local_step(x[0], positions[0], loss_target[0], gains, exchange)
    grad = dict(exchange.full)

    small = [g_gain[k].reshape(-1) for k in GAIN_NAMES] + [g_conv.reshape(-1), loss_lanes.reshape(-1)]
    sizes = [int(s.shape[0]) for s in small]
    flat = jnp.concatenate(small)
    loss_row = (sum(sizes) - LANES) // LANES
    rows = -(-flat.shape[0] // (8 * LANES)) * 8
    flat = jnp.pad(flat, (0, rows * LANES - flat.shape[0])).reshape(rows, LANES)
    total = _allreduce_small(flat, loss_row).reshape(-1)
    offsets = np.cumsum([0] + sizes)
    for i, k in enumerate(GAIN_NAMES):
        grad[k] = total[offsets[i]:offsets[i + 1]].reshape(params[k].shape)
    conv_all = total[offsets[6]:offsets[7]].reshape(DEPTH, N_CHIPS, 3, LANES)
    grad["conv_w"] = lax.dynamic_index_in_dim(conv_all, 2 * xi + yi, axis=1, keepdims=False)
    loss = total[offsets[7]]

    delta, new_m, new_v = {}, {}, {}
    for k in WEIGHT_ORDER:
        shape = params[k].shape
        if k in MATRIX_NAMES:
            tr = {1024: 512, 704: 352, 256: 256}[shape[1]]
            delta[k], new_m[k], new_v[k], grad[k] = _adamw(params[k], grad[k], mom1[k], mom2[k], tr=tr, emit_grad=True)
        else:
            as3 = (lambda a: a) if len(shape) == 3 else (lambda a: a[None])
            d, m, v = _adamw(as3(params[k]), as3(grad[k]), as3(mom1[k]), as3(mom2[k]), tr=as3(params[k]).shape[1])
            delta[k], new_m[k], new_v[k] = d.reshape(shape), m.reshape(shape), v.reshape(shape)

    return (loss, grad_x[None], *[grad[k] for k in WEIGHT_ORDER], *[delta[k] for k in WEIGHT_ORDER],
            *[new_m[k] for k in WEIGHT_ORDER], *[new_v[k] for k in WEIGHT_ORDER])
```

```python
import functools
from typing import Callable, NamedTuple

import numpy as np
import jax
import jax.numpy as jnp
from jax import lax
from jax.experimental import pallas as pl
from jax.experimental.pallas import tpu as pltpu

F32 = jnp.float32
BF16 = jnp.bfloat16
MESH = pl.DeviceIdType.MESH

D_MODEL = 1024
ATTN_W = 512
CONV_W = 512
HEAD_DIM = 64
ROPE_DIM = 16
ROPE_THETA = 500000.0
FFN = 2816
DEPTH = 4
RMS_EPS = 1e-6
NEG_INF = -1e30
N_CHIPS = 4
N_DEV = 8
LANES = 128
BF16_ROWS = 16
DILATIONS = (1, 4, 16)
BAND = 64
TQ = 128
WIN = TQ + 2 * BAND
SCALE = HEAD_DIM ** -0.5

ADAM_LR = 0.001
ADAM_B1 = 0.9
ADAM_B2 = 0.999
ADAM_EPS = 1e-08
ADAM_WD = 0.01
ADAM_STEP = 10

GAIN_NAMES = ("pre_mix_norm", "attn_out_norm", "conv_out_norm", "post_mix_norm", "pre_ffn_norm", "post_ffn_norm")
MATRIX_NAMES = ("w_in", "w_out", "w_gate_up", "w_down")
WEIGHT_ORDER = ("pre_mix_norm", "w_in", "conv_w", "attn_out_norm", "conv_out_norm", "w_out", "post_mix_norm",
                "pre_ffn_norm", "w_gate_up", "w_down", "post_ffn_norm")

ANY = pl.BlockSpec(memory_space=pl.ANY)
WHOLE_VMEM = pl.BlockSpec(memory_space=pltpu.VMEM)
LANE_CONTRACT = (((1,), (1,)), ((), ()))
ROW_CONTRACT = (((0,), (0,)), ((), ()))
CHUNK = 256


def _const_spec(block, index):
    return pl.BlockSpec(block, lambda *_: index)


def _gain_spec(g3, l):
    return _const_spec((None, 1, g3.shape[-1]), (l, 0, 0))


class _Comm(NamedTuple):
    ins: tuple
    inouts: tuple
    out_shapes: tuple
    n_sems: int
    start: Callable
    finish: Callable


def _place():
    x, y, c = lax.axis_index("x"), lax.axis_index("y"), lax.axis_index("c")
    other_chips = [(1 - x, y), (x, 1 - y), (1 - x, 1 - y)]
    return x, y, c, other_chips


def _remote(src, dst, send_sem, recv_sem, to):
    return pltpu.make_async_remote_copy(src_ref=src, dst_ref=dst, send_sem=send_sem, recv_sem=recv_sem,
                                        device_id=to, device_id_type=MESH)


def _call(body, operands, *, name, grid, in_specs, out_specs, out_shape, scratch_shapes=(), comm=None):
    in_specs, out_specs, out_shape = list(in_specs), list(out_specs), list(out_shape)
    scratch_shapes = list(scratch_shapes)
    if comm is None:
        out = pl.pallas_call(body, grid=grid, in_specs=in_specs, out_specs=out_specs, out_shape=out_shape,
                             scratch_shapes=scratch_shapes, name=name)(*operands)
        return list(out), None
    n_in, n_out, n_scr = len(in_specs), len(out_shape), len(scratch_shapes)
    n_ci, n_cio, n_co = len(comm.ins), len(comm.inouts), len(comm.out_shapes)

    def hosted(*refs):
        refs = list(refs)
        ins, c_ins = refs[:n_in], refs[n_in:n_in + n_ci]
        base = n_in + n_ci + n_cio
        outs = refs[base:base + n_out]
        c_io = refs[base + n_out:base + n_out + n_cio]
        c_out = refs[base + n_out + n_cio:base + n_out + n_cio + n_co]
        scr = refs[base + n_out + n_cio + n_co:]
        send_sems, recv_sems = scr[n_scr], scr[n_scr + 1]
        if grid:
            first = functools.reduce(jnp.logical_and, [pl.program_id(a) == 0 for a in range(len(grid))])
            last = functools.reduce(jnp.logical_and, [pl.program_id(a) == grid[a] - 1 for a in range(len(grid))])
            pl.when(first)(lambda: comm.start(c_ins, c_io, c_out, send_sems, recv_sems))
            body(*ins, *outs, *scr[:n_scr])
            pl.when(last)(lambda: comm.finish(c_ins, c_io, c_out, send_sems, recv_sems))
        else:
            comm.start(c_ins, c_io, c_out, send_sems, recv_sems)
            body(*ins, *outs, *scr[:n_scr])
            comm.finish(c_ins, c_io, c_out, send_sems, recv_sems)

    res = pl.pallas_call(
        hosted, grid=grid, in_specs=in_specs + [ANY] * (n_ci + n_cio), out_specs=out_specs + [ANY] * (n_cio + n_co),
        out_shape=out_shape + [jax.ShapeDtypeStruct(a.shape, a.dtype) for a in comm.inouts] + list(comm.out_shapes),
        input_output_aliases={n_in + n_ci + i: n_out + i for i in range(n_cio)},
        scratch_shapes=scratch_shapes + [pltpu.SemaphoreType.DMA((comm.n_sems,))] * 2,
        name=name)(*operands, *comm.ins, *comm.inouts)
    return list(res[:n_out]), list(res[n_out:])


def _run_comm(comm, name):
    return _call(lambda: None, [], name=name, grid=(), in_specs=[], out_specs=[], out_shape=[], comm=comm)[1]


def _row_half(ref, lead, core, rows, align):
    hr = rows // 2
    return ref.at[(*lead, pl.ds(pl.multiple_of(core * hr, align), hr), slice(None))]


def _gather_comm(slabs, conv_slab=None):
    n = len(slabs)
    n_conv = 0 if conv_slab is None else 3

    def direct(ios, send, recv):
        x, y, c, chips = _place()
        copies = []
        for a in range(n):
            own = _row_half(ios[a], (2 * x + y,), c, slabs[a].shape[1], BF16_ROWS)
            copies += [_remote(own, own, send.at[a * 3 + j], recv.at[a * 3 + j], (*chip, c))
                       for j, chip in enumerate(chips)]
        if conv_slab is not None:
            own = ios[n].at[:, 2 * x + y]
            copies += [_remote(own, own, send.at[6 * n + j], recv.at[6 * n + j], (*chip, c))
                       for j, chip in enumerate(chips)]
        return copies

    def start(ins, ios, outs, send, recv):
        for cp in direct(ios, send, recv):
            cp.start()

    def finish(ins, ios, outs, send, recv):
        x, y, c, chips = _place()
        sibling = (x, y, 1 - c)
        passed = []
        for a in range(n):
            for j, chip in enumerate(chips):
                landed = _row_half(ios[a], (2 * chip[0] + chip[1],), c, slabs[a].shape[1], BF16_ROWS)
                _remote(landed, landed, send.at[a * 3 + j], recv.at[a * 3 + j], (*chip, c)).wait_recv()
                fwd = _remote(landed, landed, send.at[3 * n + a * 3 + j], recv.at[3 * n + a * 3 + j], sibling)
                fwd.start()
                passed.append(fwd)
        if conv_slab is not None:
            for j, chip in enumerate(chips):
                landed = ios[n].at[:, 2 * chip[0] + chip[1]]
                _remote(landed, landed, send.at[6 * n + j], recv.at[6 * n + j], (*chip, c)).wait_recv()
        for a in range(n):
            for j, chip in enumerate(chips):
                landed = _row_half(ios[a], (2 * chip[0] + chip[1],), 1 - c, slabs[a].shape[1], BF16_ROWS)
                _remote(landed, landed, send.at[3 * n + a * 3 + j], recv.at[3 * n + a * 3 + j], sibling).wait_recv()
        for cp in direct(ios, send, recv) + passed:
            cp.wait_send()

    inouts = tuple(slabs) + (() if conv_slab is None else (conv_slab,))
    return _Comm((), inouts, (), 6 * n + n_conv, start, finish)


def _halves_comm(grads):
    n = len(grads)

    def copies(ins, outs, send, recv):
        x, y, c, _ = _place()
        return [_remote(_row_half(ins[a], (slice(None),), 1 - c, grads[a].shape[1], 8), outs[a],
                        send.at[a], recv.at[a], (x, y, 1 - c)) for a in range(n)]

    def start(ins, ios, outs, send, recv):
        for cp in copies(ins, outs, send, recv):
            cp.start()

    def finish(ins, ios, outs, send, recv):
        for cp in copies(ins, outs, send, recv):
            cp.wait()

    out_shapes = tuple(jax.ShapeDtypeStruct((g.shape[0], g.shape[1] // 2, g.shape[2]), F32) for g in grads)
    return _Comm(tuple(grads), (), out_shapes, n, start, finish)


def _partials_comm(partials):
    n = len(partials)

    def copies(ins, outs, send, recv):
        x, y, c, chips = _place()
        return [_remote(ins[a].at[2 * chip[0] + chip[1]], outs[a].at[k], send.at[a * 3 + k], recv.at[a * 3 + k],
                        (*chip, c)) for a in range(n) for k, chip in enumerate(chips)]

    def start(ins, ios, outs, send, recv):
        for cp in copies(ins, outs, send, recv):
            cp.start()

    def finish(ins, ios, outs, send, recv):
        for cp in copies(ins, outs, send, recv):
            cp.wait()

    out_shapes = tuple(jax.ShapeDtypeStruct((3,) + p.shape[1:], BF16) for p in partials)
    return _Comm(tuple(partials), (), out_shapes, 3 * n, start, finish)


def _share_comm(grads, l):
    n = len(grads)

    def start(ins, ios, outs, send, recv):
        x, y, c, _ = _place()
        for a in range(n):
            mine = _row_half(ios[a], (l,), c, grads[a].shape[1], 8)
            _remote(mine, mine, send.at[a], recv.at[a], (x, y, 1 - c)).start()

    def finish(ins, ios, outs, send, recv):
        x, y, c, _ = _place()
        for a in range(n):
            theirs = _row_half(ios[a], (l,), 1 - c, grads[a].shape[1], 8)
            _remote(theirs, theirs, send.at[a], recv.at[a], (x, y, 1 - c)).wait()

    return _Comm((), tuple(grads), (), n, start, finish)


def _rms_fwd(x, g):
    r = lax.rsqrt(jnp.mean(x * x, axis=-1, keepdims=True) + RMS_EPS)
    return (x * r) * g


def _rms_bwd(x, g, dy):
    r = lax.rsqrt(jnp.mean(x * x, axis=-1, keepdims=True) + RMS_EPS)
    xh = x * r
    u = dy * g
    dx = r * (u - xh * jnp.mean(xh * u, axis=-1, keepdims=True))
    return dx, jnp.sum(dy * xh, axis=0, keepdims=True)


def _accumulate(ref, value, first):
    @pl.when(first)
    def _():
        ref[...] = value

    @pl.when(jnp.logical_not(first))
    def _():
        ref[...] += value


def _rope_coeffs(cos, sin):
    m = lax.broadcasted_iota(jnp.int32, cos.shape, 1) % HEAD_DIM
    a = jnp.where(m < ROPE_DIM, cos, 1.0)
    b = jnp.where(m < ROPE_DIM // 2, -sin, 0.0)
    c = jnp.where((m >= ROPE_DIM // 2) & (m < ROPE_DIM), sin, 0.0)
    return a, b, c


def _rope_apply(t, cos, sin):
    a, b, c = _rope_coeffs(cos, sin)
    n = t.shape[1]
    return a * t + b * pltpu.roll(t, n - ROPE_DIM // 2, 1) + c * pltpu.roll(t, ROPE_DIM // 2, 1)


def _rope_transpose(dt, cos, sin):
    a, b, c = _rope_coeffs(cos, sin)
    n = dt.shape[1]
    return a * dt + pltpu.roll(b * dt, ROPE_DIM // 2, 1) + pltpu.roll(c * dt, n - ROPE_DIM // 2, 1)


def _in_proj(h, w, cos, sin, *, tm, comm=None):
    t, k = h.shape
    s_n, _, n = w.shape
    assert t % tm == 0 and n % LANES == 0

    rotary_shards = -(-2 * ATTN_W // n)
    rotary = list(range(0, 2 * ATTN_W, LANES))
    per_shard = -(-len(rotary) // max(s_n - rotary_shards, 1))

    def body(h_ref, w_ref, cos_ref, sin_ref, o_ref):
        def shard(s):
            o_ref[:, s * n:(s + 1) * n] = jnp.dot(h_ref[...], w_ref[s], preferred_element_type=F32)

        def rope(c0):
            cols = slice(c0, c0 + LANES)
            o_ref[:, cols] = _rope_apply(o_ref[:, cols], cos_ref[...], sin_ref[...])

        for s in range(rotary_shards):
            shard(s)
        pending = list(rotary)
        for s in range(rotary_shards, s_n):
            for c0 in pending[:per_shard]:
                rope(c0)
            pending = pending[per_shard:]
            shard(s)
        for c0 in pending:
            rope(c0)

    lane_tile = pl.BlockSpec((tm, LANES), lambda i: (i, 0))
    out, got = _call(
        body, [h, w, cos, sin], name="in_proj", grid=(t // tm,),
        in_specs=[pl.BlockSpec((tm, k), lambda i: (i, 0)), _const_spec(w.shape, (0, 0, 0)), lane_tile, lane_tile],
        out_specs=[pl.BlockSpec((tm, s_n * n), lambda i: (i, 0))],
        out_shape=[jax.ShapeDtypeStruct((t, s_n * n), F32)], comm=comm)
    return out[0], got


def _mm_nt(a, w, *, tm, tko, name, comm=None):
    t, sn = a.shape
    s_n, ko, n = w.shape
    assert sn == s_n * n and t % tm == 0 and ko % tko == 0

    def body(a_ref, w_ref, o_ref):
        acc = lax.dot_general(a_ref[...], w_ref[...], (((1,), (1,)), ((), ())), preferred_element_type=F32)
        if s_n == 1:
            o_ref[...] = acc
        else:
            _accumulate(o_ref, acc, pl.program_id(2) == 0)

    out, got = _call(
        body, [a, w], name=name, grid=(t // tm, ko // tko, s_n),
        in_specs=[pl.BlockSpec((tm, n), lambda i, j, s: (i, s)),
                  pl.BlockSpec((None, tko, n), lambda i, j, s: (s, j, 0))],
        out_specs=[pl.BlockSpec((tm, tko), lambda i, j, s: (i, j))],
        out_shape=[jax.ShapeDtypeStruct((t, ko), F32)], comm=comm)
    return out[0], got


def _dx_through_norms(operands, in_specs, dx_rows, dres, pre, post, *, tm, name, comm=None):
    t, d_model = dres.shape
    (x, gx3, lx), (y, gy3, ly) = pre, post
    n_op = len(operands)
    row = pl.BlockSpec((tm, d_model), lambda i: (i, 0))
    gsum = _const_spec((1, d_model), (0, 0))

    def body(*refs):
        d_ref, x_ref, gx_ref, y_ref, gy_ref, dn_ref, dgx_ref, dy_ref, dgy_ref = refs[n_op:]
        first = pl.program_id(0) == 0
        dx, dgx = _rms_bwd(x_ref[...], gx_ref[...], dx_rows(refs[:n_op]))
        d_new = d_ref[...] + dx
        dn_ref[...] = d_new
        _accumulate(dgx_ref, dgx, first)
        dy, dgy = _rms_bwd(y_ref[...], gy_ref[...], d_new)
        dy_ref[...] = dy.astype(BF16)
        _accumulate(dgy_ref, dgy, first)

    out, got = _call(
        body, list(operands) + [dres, x, gx3, y, gy3], name=name, grid=(t // tm,),
        in_specs=list(in_specs) + [row, row, _gain_spec(gx3, lx), row, _gain_spec(gy3, ly)],
        out_specs=[row, gsum, row, gsum],
        out_shape=[jax.ShapeDtypeStruct((t, d_model), F32), jax.ShapeDtypeStruct((1, d_model), F32),
                   jax.ShapeDtypeStruct((t, d_model), BF16), jax.ShapeDtypeStruct((1, d_model), F32)],
        comm=comm)
    return out[0], out[2], out[1], out[3], got


def _gate_up_dx_norms(dg, du, w, dres, pre, post, *, tm, comm=None):
    s_n, ko, n = w.shape
    half = s_n // 2

    def dx_rows(refs):
        dg_ref, du_ref, w_ref = refs
        acc = jnp.zeros((tm, ko), F32)
        for s in range(half):
            cols = slice(s * n, (s + 1) * n)
            acc = acc + lax.dot_general(dg_ref[:, cols], w_ref[s], LANE_CONTRACT, preferred_element_type=F32)
            acc = acc + lax.dot_general(du_ref[:, cols], w_ref[half + s], LANE_CONTRACT, preferred_element_type=F32)
        return acc

    a_spec = pl.BlockSpec((tm, half * n), lambda i: (i, 0))
    return _dx_through_norms([dg, du, w], [a_spec, a_spec, _const_spec(w.shape, (0, 0, 0))], dx_rows, dres, pre, post,
                             tm=tm, name="gate_up_dx", comm=comm)


def _in_proj_dx_norms(d_attn3, d_conv3, w, dres, pre, post, *, tm, comm=None):
    s_n, ko, n = w.shape
    per = n // CHUNK
    pieces = d_attn3.shape[0]
    width = d_attn3.shape[2]

    def dx_rows(refs):
        a_ref, b_ref, w_ref = refs
        acc = jnp.zeros((tm, ko), F32)
        for c in range(s_n * per):
            src = a_ref if c // 2 < pieces else b_ref
            piece, c0 = (c // 2) % pieces, (c % 2) * CHUNK
            acc = acc + lax.dot_general(src[piece, :, c0:c0 + CHUNK], w_ref[c // per, :, (c % per) * CHUNK:(c % per + 1) * CHUNK],
                                        LANE_CONTRACT, preferred_element_type=F32)
        return acc

    stack = pl.BlockSpec((pieces, tm, width), lambda i: (0, i, 0))
    return _dx_through_norms([d_attn3, d_conv3, w], [stack, stack, _const_spec(w.shape, (0, 0, 0))], dx_rows, dres, pre,
                             post, tm=tm, name="in_proj_dx", comm=comm)


def _mm_tn(a, b, s_n, *, tka, name, into=None, shard0=0):
    t, ka = a.shape
    n = b.shape[1] // s_n
    assert b.shape[0] == t and ka % tka == 0

    def body(a_ref, b_ref, *rest):
        rest[-1][...] = lax.dot_general(a_ref[...], b_ref[...], ROW_CONTRACT, preferred_element_type=F32)

    operands, in_specs, aliases = [a, b], [pl.BlockSpec((t, tka), lambda i, s: (0, i)),
                                           pl.BlockSpec((t, n), lambda i, s: (0, s))], {}
    out_shape = jax.ShapeDtypeStruct((s_n, ka, n), F32)
    if into is not None:
        operands, in_specs, aliases = operands + [into], in_specs + [ANY], {2: 0}
        out_shape = jax.ShapeDtypeStruct(into.shape, F32)
    return pl.pallas_call(
        body, grid=(ka // tka, s_n), in_specs=in_specs,
        out_specs=pl.BlockSpec((None, tka, n), lambda i, s: (shard0 + s, i, 0)),
        out_shape=out_shape, input_output_aliases=aliases, name=name)(*operands)


def _gate_up_swiglu(h, w, *, tm, comm=None):
    t, k = h.shape
    s_n, _, n = w.shape
    half = s_n // 2

    def body(h_ref, wg_ref, wu_ref, g_ref, u_ref, a_ref):
        g = jnp.dot(h_ref[...], wg_ref[...], preferred_element_type=F32)
        u = jnp.dot(h_ref[...], wu_ref[...], preferred_element_type=F32)
        g_ref[...] = g.astype(BF16)
        u_ref[...] = u.astype(BF16)
        a_ref[...] = (g * jax.nn.sigmoid(g) * u).astype(BF16)

    col = pl.BlockSpec((tm, n), lambda i, j: (i, j))
    out, got = _call(
        body, [h, w, w], name="gate_up", grid=(t // tm, half),
        in_specs=[pl.BlockSpec((tm, k), lambda i, j: (i, 0)), pl.BlockSpec((None, k, n), lambda i, j: (j, 0, 0)),
                  pl.BlockSpec((None, k, n), lambda i, j: (half + j, 0, 0))],
        out_specs=[col, col, col],
        out_shape=[jax.ShapeDtypeStruct((t, half * n), BF16)] * 3, comm=comm)
    return out[0], out[1], out[2], got


def _down_dx_swiglu_bwd(df, w, g, u, *, tm, tko):
    t, k = df.shape
    _, ko, _ = w.shape
    assert t % tm == 0 and ko % tko == 0

    def body(df_ref, w_ref, g_ref, u_ref, dg_ref, du_ref):
        d = lax.dot_general(df_ref[...], w_ref[...], LANE_CONTRACT, preferred_element_type=F32)
        gg = g_ref[...].astype(F32)
        sig = jax.nn.sigmoid(gg)
        dg_ref[...] = (d * u_ref[...].astype(F32) * (sig * (1.0 + gg * (1.0 - sig)))).astype(BF16)
        du_ref[...] = (d * (gg * sig)).astype(BF16)

    col = pl.BlockSpec((tm, tko), lambda i, j: (i, j))
    return pl.pallas_call(
        body, grid=(t // tm, ko // tko),
        in_specs=[pl.BlockSpec((tm, k), lambda i, j: (i, 0)), pl.BlockSpec((None, tko, k), lambda i, j: (0, j, 0)),
                  col, col],
        out_specs=[col, col], out_shape=[jax.ShapeDtypeStruct((t, ko), BF16)] * 2, name="down_dx")(df, w, g, u)


def _in_proj_dx(d_attn3, d_conv3, w, *, tm, comm=None):
    _, t, _ = d_attn3.shape
    s_n, ko, n = w.shape
    half, per = s_n // 2, n // CHUNK
    assert t % tm == 0

    def body(*refs):
        a_refs, b_refs, wa_ref, wb_ref, o_ref = refs[:per], refs[per:2 * per], refs[2 * per], refs[2 * per + 1], refs[-1]
        acc = jnp.zeros(o_ref.shape, F32)
        for r in range(per):
            cols = slice(r * CHUNK, (r + 1) * CHUNK)
            acc = acc + lax.dot_general(a_refs[r][...], wa_ref[:, cols], LANE_CONTRACT, preferred_element_type=F32)
            acc = acc + lax.dot_general(b_refs[r][...], wb_ref[:, cols], LANE_CONTRACT, preferred_element_type=F32)
        _accumulate(o_ref, acc, pl.program_id(1) == 0)

    piece = lambda r: pl.BlockSpec((None, tm, CHUNK), lambda i, s: ((per * s + r) // 2, i, (per * s + r) % 2))
    out, got = _call(
        body, [d_attn3] * per + [d_conv3] * per + [w, w], name="in_proj_dx", grid=(t // tm, half),
        in_specs=[piece(r) for r in range(per)] * 2
        + [pl.BlockSpec((None, ko, n), lambda i, s: (s, 0, 0)), pl.BlockSpec((None, ko, n), lambda i, s: (half + s, 0, 0))],
        out_specs=[pl.BlockSpec((tm, ko), lambda i, s: (i, 0))],
        out_shape=[jax.ShapeDtypeStruct((t, ko), F32)], comm=comm)
    return out[0], got


def _in_proj_dw(h, d_attn3, d_conv3, s_n, *, tka):
    t, ka = h.shape
    half = s_n // 2
    n = 3 * d_attn3.shape[2] // half
    per = n // CHUNK
    assert ka % tka == 0

    def body(*refs):
        h_ref, o_ref = refs[0], refs[-1]
        for side in range(2):
            for r in range(per):
                o_ref[side, :, r * CHUNK:(r + 1) * CHUNK] = lax.dot_general(
                    h_ref[...], refs[1 + side * per + r][...], ROW_CONTRACT, preferred_element_type=F32)

    piece = lambda r: pl.BlockSpec((None, t, CHUNK), lambda i, s: ((per * s + r) // 2, 0, (per * s + r) % 2))
    out = pl.pallas_call(
        body, grid=(ka // tka, half),
        in_specs=[pl.BlockSpec((t, tka), lambda i, s: (0, i))] + [piece(r) for r in range(per)] * 2,
        out_specs=pl.BlockSpec((2, None, tka, n), lambda i, s: (0, s, i, 0)),
        out_shape=jax.ShapeDtypeStruct((2, half, ka, n), F32), name="in_proj_dw")(h, *[d_attn3] * per, *[d_conv3] * per)
    return out.reshape(s_n, ka, n)


def _rope_tables(positions_col, inv_freq_row):
    t = positions_col.shape[0]

    def body(pos_ref, f_ref, cos_ref, sin_ref):
        ang = pos_ref[...].astype(F32) * f_ref[...]
        cos_ref[...] = jnp.cos(ang)
        sin_ref[...] = jnp.sin(ang)

    return pl.pallas_call(
        body, out_shape=[jax.ShapeDtypeStruct((t, LANES), F32)] * 2, name="rope_tables")(positions_col, inv_freq_row)


def _norm_fwd(x, g3, l, *, tm):
    t, w = x.shape

    def body(x_ref, g_ref, h_ref):
        h_ref[...] = _rms_fwd(x_ref[...], g_ref[...]).astype(BF16)

    return pl.pallas_call(
        body, grid=(t // tm,),
        in_specs=[pl.BlockSpec((tm, w), lambda i: (i, 0)), _gain_spec(g3, l)],
        out_specs=pl.BlockSpec((tm, w), lambda i: (i, 0)),
        out_shape=jax.ShapeDtypeStruct((t, w), BF16), name="norm_fwd")(x, g3)


def _mm_resnorm(a, w, x, g_post3, l_post, g_next3, l_next, *, tm, name, comm=None):
    t, k = a.shape
    _, _, n = w.shape
    with_next = g_next3 is not None
    row = pl.BlockSpec((tm, n), lambda i: (i, 0))

    def body(a_ref, w_ref, x_ref, gp_ref, *rest):
        y = jnp.dot(a_ref[...], w_ref[...], preferred_element_type=F32)
        x_new = x_ref[...] + _rms_fwd(y, gp_ref[...])
        if with_next:
            gn_ref, y_ref, xo_ref, h_ref = rest
            h_ref[...] = _rms_fwd(x_new, gn_ref[...]).astype(BF16)
        else:
            y_ref, xo_ref = rest
        y_ref[...] = y
        xo_ref[...] = x_new

    ins = [a, w, x, g_post3] + ([g_next3] if with_next else [])
    in_specs = ([pl.BlockSpec((tm, k), lambda i: (i, 0)), _const_spec((None, k, n), (0, 0, 0)), row,
                 _gain_spec(g_post3, l_post)] + ([_gain_spec(g_next3, l_next)] if with_next else []))
    out_shape = [jax.ShapeDtypeStruct((t, n), F32)] * 2 + ([jax.ShapeDtypeStruct((t, n), BF16)] if with_next else [])
    out, got = _call(body, ins, name=name, grid=(t // tm,), in_specs=in_specs, out_specs=[row] * len(out_shape),
                     out_shape=out_shape, comm=comm)
    return out[0], out[1], (out[2] if with_next else None), got


def _conv_fwd(proj, conv_w, l):
    t = proj.shape[0]
    col0 = 3 * ATTN_W // LANES

    def body(u_ref, gb_ref, gc_ref, w_ref, y_ref):
        c = gc_ref[...] * u_ref[...]
        row = lax.broadcasted_iota(jnp.int32, c.shape, 0)
        c_prev = jnp.where(row == 0, 0.0, pltpu.roll(c, 1, 0))
        c_next = jnp.where(row == t - 1, 0.0, pltpu.roll(c, t - 1, 0))
        w = w_ref[...]
        y_ref[...] = gb_ref[...] * (w[0:1] * c_prev + w[1:2] * c + w[2:3] * c_next)

    nj = CONV_W // LANES
    cols = lambda base: pl.BlockSpec((t, LANES), lambda j: (0, base + j))
    return pl.pallas_call(
        body, grid=(nj,),
        in_specs=[cols(col0), cols(col0 + nj), cols(col0 + 2 * nj),
                  pl.BlockSpec((None, None, 3, LANES), lambda j: (l, j, 0, 0))],
        out_specs=pl.BlockSpec((t, LANES), lambda j: (0, j)),
        out_shape=jax.ShapeDtypeStruct((t, CONV_W), F32), name="conv_fwd")(proj, proj, proj, conv_w)


def _merge_fwd(attn, conv_y, ga3, gc3, l, *, tm):
    t = attn.shape[0]
    row = pl.BlockSpec((tm, ATTN_W), lambda i: (i, 0))

    def body(a_ref, c_ref, ga_ref, gc_ref, m_ref):
        m_ref[:, :ATTN_W] = _rms_fwd(a_ref[...], ga_ref[...]).astype(BF16)
        m_ref[:, ATTN_W:] = _rms_fwd(c_ref[...], gc_ref[...]).astype(BF16)

    return pl.pallas_call(
        body, grid=(t // tm,),
        in_specs=[row, row, _gain_spec(ga3, l), _gain_spec(gc3, l)],
        out_specs=pl.BlockSpec((tm, D_MODEL), lambda i: (i, 0)),
        out_shape=jax.ShapeDtypeStruct((t, D_MODEL), BF16), name="merge_fwd")(attn, conv_y, ga3, gc3)


def _loss_fwd_bwd(y, target, *, tm):
    t, w = y.shape
    row = pl.BlockSpec((tm, w), lambda i: (i, 0))

    def body(y_ref, t_ref, dy_ref, loss_ref):
        e = y_ref[...] - t_ref[...]
        dy_ref[...] = e * (1.0 / w)
        sq = jnp.sum(e * e, axis=0, keepdims=True) * (0.5 / w)
        part = sq[:, :LANES]
        for j in range(1, w // LANES):
            part = part + sq[:, j * LANES:(j + 1) * LANES]
        _accumulate(loss_ref, part, pl.program_id(0) == 0)

    return pl.pallas_call(
        body, grid=(t // tm,), in_specs=[row, row],
        out_specs=[row, _const_spec((1, LANES), (0, 0))],
        out_shape=[jax.ShapeDtypeStruct((t, w), F32), jax.ShapeDtypeStruct((1, LANES), F32)], name="loss")(y, target)


def _tile_rows(t, nt, lb, d):
    r = t // nt
    q0 = (t % nt) * TQ
    m0 = jnp.clip(q0 - BAND, 0, lb - WIN)
    if d == 1:
        return pl.ds(pl.multiple_of(q0, TQ), TQ), pl.ds(pl.multiple_of(m0, BAND), WIN), m0 - q0
    return pl.ds(r + d * q0, TQ, stride=d), pl.ds(r + d * m0, WIN, stride=d), m0 - q0


def _for_row_chunks(t, fn, chunk=512):
    def step(i, carry):
        fn(pl.ds(pl.multiple_of(i * chunk, chunk), chunk))
        return carry

    lax.fori_loop(0, t // chunk, step, 0)


WINDOW_OFFSETS = (-BAND, 0, -2 * BAND)


def _fill_band_bias(bias_ref):
    rel0 = (lax.broadcasted_iota(jnp.int32, (2 * TQ, WIN), 1)
            - lax.broadcasted_iota(jnp.int32, (2 * TQ, WIN), 0) % TQ)
    for j, off in enumerate(WINDOW_OFFSETS):
        rel = rel0 + off
        bias_ref[j] = jnp.where((rel >= -BAND) & (rel <= BAND), 0.0, NEG_INF)


def _fill_sequence_bias(bias_ref):
    rel = (lax.broadcasted_iota(jnp.int32, (2 * WIN, WIN), 1) - lax.broadcasted_iota(jnp.int32, (2 * WIN, WIN), 0) % WIN)
    bias_ref[...] = jnp.where((rel >= -BAND) & (rel <= BAND), 0.0, NEG_INF)


def _band_bias(bias_ref, off):
    return bias_ref[jnp.where(off == WINDOW_OFFSETS[0], 0, jnp.where(off == WINDOW_OFFSETS[1], 1, 2))]


def _stack_heads(a, first_head):
    return jnp.concatenate([jnp.where(first_head, a, 0.0), jnp.where(first_head, 0.0, a)], axis=0)


def _unstack_heads(a2, first_head):
    n = a2.shape[0] // 2
    return jnp.where(first_head, a2[:n], a2[n:])


def _attn_fwd(proj, comm=None):
    t = proj.shape[0]
    npair = ATTN_W // LANES

    def body(q_ref, k_ref, v_ref, o_ref, lse_ref, o1, o2, l0, l1, l2, m1, m2, bias, bias_seq):
        _fill_band_bias(bias)
        _fill_sequence_bias(bias_seq)
        outs, dens, maxs = (o_ref, o1, o2), (l0, l1, l2), (lse_ref, m1, m2)

        def softmax_tile(b, qrows, krows, n_q, band_bias):
            first_head = lax.broadcasted_iota(jnp.int32, (n_q, LANES), 1) < HEAD_DIM
            q2 = _stack_heads(q_ref[qrows, :] * SCALE, first_head).astype(BF16)
            kw = k_ref[krows, :].astype(BF16)
            vw = jnp.concatenate([v_ref[krows, :].astype(BF16), jnp.ones((WIN, LANES), BF16)], axis=1)
            s = lax.dot_general(q2, kw, LANE_CONTRACT, preferred_element_type=F32) + band_bias
            m = jnp.max(s, axis=-1, keepdims=True)
            pv = jnp.dot(jnp.exp(s - m).astype(BF16), vw, preferred_element_type=F32)
            outs[b][qrows, :] = _unstack_heads(pv[:, :LANES], first_head)
            dens[b][qrows, :] = _unstack_heads(pv[:, LANES:], first_head)
            maxs[b][qrows, :] = _unstack_heads(jnp.broadcast_to(m, (2 * n_q, LANES)), first_head)

        for b, d in enumerate(DILATIONS):
            lb = t // d
            if lb == WIN:
                def sequence(r, carry, b=b, d=d):
                    rows = pl.ds(r, WIN, stride=d)
                    softmax_tile(b, rows, rows, WIN, bias_seq[...])
                    return carry

                lax.fori_loop(0, d, sequence, 0, unroll=4)
                continue
            nt = lb // TQ

            def tile(ti, carry, b=b, d=d, lb=lb, nt=nt):
                qrows, krows, off = _tile_rows(ti, nt, lb, d)
                softmax_tile(b, qrows, krows, TQ, _band_bias(bias, off))
                return carry

            lax.fori_loop(0, d * nt, tile, 0, unroll=8)

        def finish(rows):
            ms = [m_b[rows, :] for m_b in maxs]
            m_all = jnp.maximum(jnp.maximum(ms[0], ms[1]), ms[2])
            ws = [jnp.exp(m_b - m_all) for m_b in ms]
            den = ws[0] * dens[0][rows, :] + ws[1] * dens[1][rows, :] + ws[2] * dens[2][rows, :]
            num = ws[0] * outs[0][rows, :] + ws[1] * outs[1][rows, :] + ws[2] * outs[2][rows, :]
            o_ref[rows, :] = num / den
            lse_ref[rows, :] = m_all + jnp.log(den)

        _for_row_chunks(t, finish, 256)

    cols = lambda base: pl.BlockSpec((t, LANES), lambda g: (0, base + g))
    out, got = _call(
        body, [proj, proj, proj], name="attn_fwd", grid=(npair,),
        in_specs=[cols(0), cols(npair), cols(2 * npair)],
        out_specs=[cols(0), cols(0)],
        out_shape=[jax.ShapeDtypeStruct((t, ATTN_W), F32)] * 2,
        scratch_shapes=[pltpu.VMEM((t, LANES), F32)] * 7 + [pltpu.VMEM((len(WINDOW_OFFSETS), 2 * TQ, WIN), F32),
                                                            pltpu.VMEM((2 * WIN, WIN), F32)],
        comm=comm)
    return out[0], out[1], got


def _attn_bwd(proj, cos, sin, d_attn, lse, delta, comm=None):
    t = proj.shape[0]
    npair = ATTN_W // LANES

    def body(q_ref, k_ref, v_ref, cos_ref, sin_ref, do_ref, l_ref, dl_ref, dqkv_ref,
             dq_acc, dk_acc, dv_acc, bias, bias_seq):
        _fill_band_bias(bias)
        _fill_sequence_bias(bias_seq)
        dq_acc[...] = jnp.zeros(dq_acc.shape, F32)
        dk_acc[...] = jnp.zeros(dk_acc.shape, F32)
        dv_acc[...] = jnp.zeros(dv_acc.shape, F32)
        def stack_column(a):
            return jnp.concatenate([a[:, 0:1], a[:, HEAD_DIM:HEAD_DIM + 1]], axis=0)

        def grad_tile(qrows, krows, n_q, band_bias):
            first_head = lax.broadcasted_iota(jnp.int32, (n_q, LANES), 1) < HEAD_DIM
            q2 = _stack_heads(q_ref[qrows, :] * SCALE, first_head).astype(BF16)
            do2 = _stack_heads(do_ref[qrows, :], first_head).astype(BF16)
            kw = k_ref[krows, :].astype(BF16)
            vw = v_ref[krows, :].astype(BF16)
            s = lax.dot_general(q2, kw, LANE_CONTRACT, preferred_element_type=F32) + band_bias
            p = jnp.exp(s - stack_column(l_ref[qrows, :]))
            dp = lax.dot_general(do2, vw, LANE_CONTRACT, preferred_element_type=F32)
            ds = (p * (dp - stack_column(dl_ref[qrows, :]))).astype(BF16)
            dq2 = jnp.dot(ds, kw, preferred_element_type=F32)
            dq_acc[qrows, :] += _unstack_heads(dq2, first_head) * SCALE
            dk_acc[krows, :] += lax.dot_general(ds, q2, ROW_CONTRACT, preferred_element_type=F32)
            dv_acc[krows, :] += lax.dot_general(p.astype(BF16), do2, ROW_CONTRACT, preferred_element_type=F32)

        for d in DILATIONS:
            lb = t // d
            if lb == WIN:
                def sequence(r, carry, d=d):
                    rows = pl.ds(r, WIN, stride=d)
                    grad_tile(rows, rows, WIN, bias_seq[...])
                    return carry

                lax.fori_loop(0, d, sequence, 0, unroll=2)
                continue
            nt = lb // TQ

            def tile(ti, carry, d=d, lb=lb, nt=nt):
                qrows, krows, off = _tile_rows(ti, nt, lb, d)
                grad_tile(qrows, krows, TQ, _band_bias(bias, off))
                return carry

            lax.fori_loop(0, d * nt, tile, 0, unroll=4)

        def finish(rows):
            dqkv_ref[0, rows, :] = _rope_transpose(dq_acc[rows, :], cos_ref[rows, :], sin_ref[rows, :]).astype(BF16)
            dqkv_ref[1, rows, :] = _rope_transpose(dk_acc[rows, :], cos_ref[rows, :], sin_ref[rows, :]).astype(BF16)
            dqkv_ref[2, rows, :] = dv_acc[rows, :].astype(BF16)

        _for_row_chunks(t, finish)

    cols = lambda base: pl.BlockSpec((t, LANES), lambda g: (0, base + g))
    out, got = _call(
        body, [proj, proj, proj, cos, sin, d_attn, lse, delta], name="attn_bwd", grid=(npair,),
        in_specs=[cols(0), cols(npair), cols(2 * npair), WHOLE_VMEM, WHOLE_VMEM, cols(0), cols(0), cols(0)],
        out_specs=[pl.BlockSpec((3, t, LANES), lambda g: (0, 0, g))],
        out_shape=[jax.ShapeDtypeStruct((3, t, ATTN_W), BF16)],
        scratch_shapes=[pltpu.VMEM((t, LANES), F32)] * 3 + [pltpu.VMEM((len(WINDOW_OFFSETS), 2 * TQ, WIN), F32),
                                                            pltpu.VMEM((2 * WIN, WIN), F32)],
        comm=comm)
    return out[0], got


def _norm_bwd(dres, pre, post, *, tm, comm=None):
    t, w = dres.shape
    row = pl.BlockSpec((tm, w), lambda i: (i, 0))
    gsum = _const_spec((1, w), (0, 0))
    ins, in_specs, out_shape, out_specs = [dres], [row], [], []
    if pre is not None:
        dh, x, g3, l = pre
        ins += [dh, x, g3]
        in_specs += [row, row, _gain_spec(g3, l)]
        out_shape += [jax.ShapeDtypeStruct((t, w), F32), jax.ShapeDtypeStruct((1, w), F32)]
        out_specs += [row, gsum]
    if post is not None:
        y, g3, l = post
        ins += [y, g3]
        in_specs += [row, _gain_spec(g3, l)]
        out_shape += [jax.ShapeDtypeStruct((t, w), BF16), jax.ShapeDtypeStruct((1, w), F32)]
        out_specs += [row, gsum]
    n_in = len(ins)

    def body(*refs):
        first = pl.program_id(0) == 0
        ins_r, outs_r = list(refs[:n_in]), list(refs[n_in:])
        d = ins_r.pop(0)[...]
        if pre is not None:
            dh_ref, x_ref, g_ref = ins_r[:3]
            ins_r = ins_r[3:]
            dx, dg = _rms_bwd(x_ref[...], g_ref[...], dh_ref[...])
            d = d + dx
            outs_r.pop(0)[...] = d
            _accumulate(outs_r.pop(0), dg, first)
        if post is not None:
            y_ref, g_ref = ins_r
            dy, dg = _rms_bwd(y_ref[...], g_ref[...], d)
            outs_r.pop(0)[...] = dy.astype(BF16)
            _accumulate(outs_r.pop(0), dg, first)

    out, got = _call(body, ins, name="norm_bwd", grid=(t // tm,), in_specs=in_specs, out_specs=out_specs,
                     out_shape=out_shape, comm=comm)
    d_new, dg_pre = (out.pop(0), out.pop(0)) if pre is not None else (None, None)
    dy, dg_post = (out.pop(0), out.pop(0)) if post is not None else (None, None)
    return d_new, dy, dg_pre, dg_post, got


def _merge_bwd(d_merged, attn, conv_y, ga3, gc3, l, *, tm, comm=None):
    t = attn.shape[0]
    row = pl.BlockSpec((tm, ATTN_W), lambda i: (i, 0))
    gsum = _const_spec((1, ATTN_W), (0, 0))

    def body(dma_ref, dmc_ref, a_ref, c_ref, ga_ref, gc_ref, da_ref, dl_ref, dc_ref, dga_ref, dgc_ref):
        first = pl.program_id(0) == 0
        attn_t = a_ref[...]
        da, dga = _rms_bwd(attn_t, ga_ref[...], dma_ref[...])
        dc, dgc = _rms_bwd(c_ref[...], gc_ref[...], dmc_ref[...])
        da_ref[...] = da
        dc_ref[...] = dc
        same_head = (lax.broadcasted_iota(jnp.int32, (ATTN_W, ATTN_W), 0) // HEAD_DIM
                     == lax.broadcasted_iota(jnp.int32, (ATTN_W, ATTN_W), 1) // HEAD_DIM).astype(BF16)
        rest = da * attn_t
        total = jnp.zeros(rest.shape, F32)
        for _ in range(3):
            term = rest.astype(BF16)
            total = total + jnp.dot(term, same_head, preferred_element_type=F32)
            rest = rest - term.astype(F32)
        dl_ref[...] = total
        _accumulate(dga_ref, dga, first)
        _accumulate(dgc_ref, dgc, first)

    out, got = _call(
        body, [d_merged, d_merged, attn, conv_y, ga3, gc3], name="merge_bwd", grid=(t // tm,),
        in_specs=[pl.BlockSpec((tm, ATTN_W), lambda i: (i, 0)), pl.BlockSpec((tm, CONV_W), lambda i: (i, 1)),
                  row, row, _gain_spec(ga3, l), _gain_spec(gc3, l)],
        out_specs=[row, row, row, gsum, gsum],
        out_shape=[jax.ShapeDtypeStruct((t, ATTN_W), F32)] * 3 + [jax.ShapeDtypeStruct((1, ATTN_W), F32)] * 2,
        comm=comm)
    return (*out, got)


def _conv_bwd(proj, conv_w, l, d_conv_y):
    t = proj.shape[0]
    col0 = 3 * ATTN_W // LANES
    nj = CONV_W // LANES

    def body(u_ref, gb_ref, gc_ref, w_ref, dy_ref, d3_ref, dw_ref):
        u, gc, dy = u_ref[...], gc_ref[...], dy_ref[...]
        row = lax.broadcasted_iota(jnp.int32, u.shape, 0)
        down = lambda a: jnp.where(row == 0, 0.0, pltpu.roll(a, 1, 0))
        up = lambda a: jnp.where(row == t - 1, 0.0, pltpu.roll(a, t - 1, 0))
        w = w_ref[...]
        c = gc * u
        c_prev, c_next = down(c), up(c)
        d3_ref[1] = (dy * (w[0:1] * c_prev + w[1:2] * c + w[2:3] * c_next)).astype(BF16)
        dz = dy * gb_ref[...]
        dc = w[0:1] * up(dz) + w[1:2] * dz + w[2:3] * down(dz)
        d3_ref[0] = (dc * gc).astype(BF16)
        d3_ref[2] = (dc * u).astype(BF16)
        dw_ref[0:1, :] = jnp.sum(dz * c_prev, axis=0, keepdims=True)
        dw_ref[1:2, :] = jnp.sum(dz * c, axis=0, keepdims=True)
        dw_ref[2:3, :] = jnp.sum(dz * c_next, axis=0, keepdims=True)

    cols = lambda base: pl.BlockSpec((t, LANES), lambda j: (0, base + j))
    return pl.pallas_call(
        body, grid=(nj,),
        in_specs=[cols(col0), cols(col0 + nj), cols(col0 + 2 * nj),
                  pl.BlockSpec((None, None, 3, LANES), lambda j: (l, j, 0, 0)), cols(0)],
        out_specs=[pl.BlockSpec((3, t, LANES), lambda j: (0, 0, j)), pl.BlockSpec((None, 3, LANES), lambda j: (j, 0, 0))],
        out_shape=[jax.ShapeDtypeStruct((3, t, CONV_W), BF16), jax.ShapeDtypeStruct((nj, 3, LANES), F32)],
        name="conv_bwd")(proj, proj, proj, conv_w, d_conv_y)


def _own_shard_slab(w, l, place, dtype):
    _, rows, cols = w.shape
    tr = rows if rows <= 704 else 512
    assert rows % tr == 0

    def body(p_ref, w_ref, o_ref):
        del p_ref
        o_ref[...] = w_ref[...].astype(dtype)

    grid_spec = pltpu.PrefetchScalarGridSpec(
        num_scalar_prefetch=1, grid=(rows // tr,),
        in_specs=[pl.BlockSpec((None, tr, cols), lambda i, p: (l, i, 0))],
        out_specs=pl.BlockSpec((None, tr, cols), lambda i, p: (p[0], i, 0)))
    return pl.pallas_call(body, grid_spec=grid_spec, name="own_shard_slab",
                          out_shape=jax.ShapeDtypeStruct((N_CHIPS, rows, cols), dtype))(place, w)


def _own_conv_slab(w, place):
    depth = w.shape[0]

    def body(p_ref, w_ref, o_ref):
        del p_ref
        o_ref[...] = w_ref[...]

    grid_spec = pltpu.PrefetchScalarGridSpec(
        num_scalar_prefetch=1, grid=(depth,),
        in_specs=[pl.BlockSpec((None, 3, LANES), lambda l, p: (l, 0, 0))],
        out_specs=pl.BlockSpec((None, None, 3, LANES), lambda l, p: (l, p[0], 0, 0)))
    return pl.pallas_call(body, grid_spec=grid_spec, name="own_conv_slab",
                          out_shape=jax.ShapeDtypeStruct((depth, N_CHIPS, 3, LANES), F32))(place, w)


def _add_halves(grad, got, place):
    s_n, rows, cols = grad.shape
    hr = rows // 2

    def body(p_ref, g_ref, r_ref, o_ref):
        del p_ref
        o_ref[...] = (g_ref[...] + r_ref[...]).astype(BF16)

    grid_spec = pltpu.PrefetchScalarGridSpec(
        num_scalar_prefetch=1, grid=(s_n,),
        in_specs=[pl.BlockSpec((None, hr, cols), lambda s, p: (s, p[1], 0)),
                  pl.BlockSpec((None, hr, cols), lambda s, p: (s, 0, 0))],
        out_specs=pl.BlockSpec((None, hr, cols), lambda s, p: (s, 0, 0)))
    return pl.pallas_call(body, grid_spec=grid_spec, out_shape=jax.ShapeDtypeStruct((s_n, hr, cols), BF16),
                          name="add_halves")(place, grad, got)


def _sum_partials(partial, got, place, acc, l):
    _, hr, cols = partial.shape

    def body(p_ref, mine_ref, got_ref, acc_ref, o_ref):
        del p_ref, acc_ref
        total = mine_ref[...].astype(F32)
        for k in range(3):
            total = total + got_ref[k].astype(F32)
        o_ref[...] = total

    grid_spec = pltpu.PrefetchScalarGridSpec(
        num_scalar_prefetch=1, grid=(1,),
        in_specs=[pl.BlockSpec((None, hr, cols), lambda i, p: (p[0], 0, 0)),
                  pl.BlockSpec((3, hr, cols), lambda i, p: (0, 0, 0)), ANY],
        out_specs=pl.BlockSpec((None, hr, cols), lambda i, p: (l, p[1], 0)))
    return pl.pallas_call(body, grid_spec=grid_spec, out_shape=jax.ShapeDtypeStruct(acc.shape, F32),
                          input_output_aliases={3: 0}, name="sum_partials")(place, partial, got, acc)


def _allreduce_small(vec, loss_row):
    rows = vec.shape[0]

    def body(v_ref, o_ref, slots, send_sems, recv_sems):
        x, y, c, _ = _place()
        me = 4 * x + 2 * y + c
        slots[me] = v_ref[...]
        copies = []
        for k in range(1, N_DEV):
            flip = lambda v, bit: 1 - v if bit else v
            peer = (flip(x, k & 4), flip(y, k & 2), flip(c, k & 1))
            copies.append(_remote(v_ref, slots.at[me], send_sems.at[k - 1], recv_sems.at[k - 1], peer))
        for cp in copies:
            cp.start()
        for k in range(1, N_DEV):
            flip = lambda v, bit: 1 - v if bit else v
            peer_id = 4 * flip(x, k & 4) + 2 * flip(y, k & 2) + flip(c, k & 1)
            _remote(v_ref, slots.at[peer_id], send_sems.at[k - 1], recv_sems.at[k - 1], (x, y, c)).wait_recv()
        for cp in copies:
            cp.wait_send()
        total = slots[0]
        for dev in range(1, N_DEV):
            total = total + slots[dev]
        o_ref[...] = total
        o_ref[loss_row:loss_row + 1, :] = jnp.broadcast_to(
            jnp.sum(total[loss_row:loss_row + 1, :], axis=-1, keepdims=True), (1, LANES))

    return pl.pallas_call(
        body, in_specs=[WHOLE_VMEM], out_specs=WHOLE_VMEM, out_shape=jax.ShapeDtypeStruct((rows, LANES), F32),
        scratch_shapes=[pltpu.VMEM((N_DEV, rows, LANES), F32), pltpu.SemaphoreType.DMA((N_DEV - 1,)),
                        pltpu.SemaphoreType.DMA((N_DEV - 1,))],
        name="allreduce_small")(vec)


def _adamw(w, g, m, v, *, tr, emit_grad=False):
    depth, rows, cols = w.shape
    assert rows % tr == 0
    c1 = float(np.float32(1.0 - ADAM_B1 ** ADAM_STEP))
    c2 = float(np.float32(1.0 - ADAM_B2 ** ADAM_STEP))

    def body(w_ref, g_ref, m_ref, v_ref, d_ref, mo_ref, vo_ref, *go_ref):
        g_t = g_ref[...]
        if emit_grad:
            go_ref[0][...] = g_t
        m_new = ADAM_B1 * m_ref[...] + (1.0 - ADAM_B1) * g_t
        v_new = ADAM_B2 * v_ref[...] + (1.0 - ADAM_B2) * (g_t * g_t)
        mo_ref[...] = m_new
        vo_ref[...] = v_new
        d_ref[...] = -ADAM_LR * ((m_new / c1) / (jnp.sqrt(v_new / c2) + ADAM_EPS) + ADAM_WD * w_ref[...])

    blk = pl.BlockSpec((None, tr, cols), lambda l, i: (l, i, 0))
    return pl.pallas_call(
        body, grid=(depth, rows // tr), in_specs=[blk] * 4, out_specs=[blk] * (4 if emit_grad else 3),
        out_shape=[jax.ShapeDtypeStruct(w.shape, F32)] * (4 if emit_grad else 3), name="adamw")(w, g, m, v)


def _local_step(x, positions, target, gains, exchange):
    t = x.shape[0]
    tm = 512
    inv_freq = ROPE_THETA ** (-jnp.arange(0, ROPE_DIM, 2, dtype=F32) / ROPE_DIM)
    lane = np.arange(LANES) % HEAD_DIM
    freq_row = jnp.where(lane < ROPE_DIM, inv_freq[lane % (ROPE_DIM // 2)], 0.0).astype(F32)[None, :]
    cos, sin = _rope_tables(positions.reshape(t, 1), freq_row)

    def hosted(tag, fn, *args, **kwargs):
        *out, got = fn(*args, comm=exchange.host(tag), **kwargs)
        if got is not None:
            exchange.hosted(tag, got)
        return out[0] if len(out) == 1 else out

    saved = []
    h1 = _norm_fwd(x, gains["pre_mix_norm"], 0, tm=tm)
    for l in range(DEPTH):
        proj = hosted(("fwd", l, "in_proj"), _in_proj, h1, exchange.weight("w_in", l), cos, sin, tm=tm)
        attn, lse = hosted(("fwd", l, "attn"), _attn_fwd, proj)
        conv_y = _conv_fwd(proj, exchange.weight("conv_w", l), l)
        merged = _merge_fwd(attn, conv_y, gains["attn_out_norm"], gains["conv_out_norm"], l, tm=tm)
        mix, x1, h2 = hosted(("fwd", l, "out_proj"), _mm_resnorm, merged, exchange.weight("w_out", l), x,
                             gains["post_mix_norm"], l, gains["pre_ffn_norm"], l, tm=tm, name="out_proj")
        g, u, act = hosted(("fwd", l, "gate_up"), _gate_up_swiglu, h2, exchange.weight("w_gate_up", l), tm=1024)
        nxt = (gains["pre_mix_norm"], l + 1) if l + 1 < DEPTH else (None, None)
        f, x2, h1_next = hosted(("fwd", l, "down"), _mm_resnorm, act, exchange.weight("w_down", l), x1,
                                gains["post_ffn_norm"], l, *nxt, tm=tm, name="down")
        saved.append(dict(x=x, h1=h1, proj=proj, attn=attn, lse=lse, conv_y=conv_y, merged=merged, mix=mix,
                          x1=x1, h2=h2, g=g, u=u, act=act, f=f))
        x, h1 = x2, h1_next

    dres, loss_lanes = _loss_fwd_bwd(x, target, tm=tm)

    g_gain = {k: [None] * DEPTH for k in gains}
    g_conv = [None] * DEPTH
    _, df, _, g_gain["post_ffn_norm"][DEPTH - 1], _ = _norm_bwd(
        dres, None, (saved[-1]["f"], gains["post_ffn_norm"], DEPTH - 1), tm=tm)
    for l in reversed(range(DEPTH)):
        sv = saved[l]
        w = {k: exchange.weight(k, l) for k in MATRIX_NAMES + ("conv_w",)}
        dg, du = _down_dx_swiglu_bwd(df, w["w_down"], sv["g"], sv["u"], tm=1024, tko=FFN // 2)
        g_down = _mm_tn(sv["act"], df, 1, tka=256, name="down_dw")
        dx1, dmix, g_gain["pre_ffn_norm"][l], g_gain["post_mix_norm"][l] = hosted(
            ("bwd", l, "gate_up_dx"), _gate_up_dx_norms, dg, du, w["w_gate_up"], dres,
            (sv["x1"], gains["pre_ffn_norm"], l), (sv["mix"], gains["post_mix_norm"], l), tm=tm)
        g_gate_up = _mm_tn(sv["h2"], dg, N_CHIPS // 2, tka=512, name="gate_up_dw",
                           into=lax.empty(w["w_gate_up"].shape, F32))
        g_gate_up = _mm_tn(sv["h2"], du, N_CHIPS // 2, tka=512, name="gate_up_dw", into=g_gate_up,
                           shard0=N_CHIPS // 2)
        exchange.grads(l, "ffn", dict(w_down=g_down.reshape(N_CHIPS, FFN // N_CHIPS, D_MODEL), w_gate_up=g_gate_up))
        d_merged = hosted(("bwd", l, "out_proj_dx"), _mm_nt, dmix, w["w_out"], tm=1024, tko=D_MODEL, name="out_proj_dx")
        g_out = _mm_tn(sv["merged"], dmix, 1, tka=512, name="out_proj_dw")
        d_attn, delta, d_conv_y, g_gain["attn_out_norm"][l], g_gain["conv_out_norm"][l] = hosted(
            ("bwd", l, "merge"), _merge_bwd,
            d_merged, sv["attn"], sv["conv_y"], gains["attn_out_norm"], gains["conv_out_norm"], l, tm=tm)
        d_attn3 = hosted(("bwd", l, "attn"), _attn_bwd, sv["proj"], cos, sin, d_attn, sv["lse"], delta)
        d_conv3, g_conv[l] = _conv_bwd(sv["proj"], w["conv_w"], l, d_conv_y)
        g_in = _in_proj_dw(sv["h1"], d_attn3, d_conv3, N_CHIPS, tka=512)
        exchange.grads(l, "mix", dict(w_out=g_out.reshape(N_CHIPS, D_MODEL // N_CHIPS, D_MODEL), w_in=g_in))
        if l > 0:
            dres, df, g_gain["pre_mix_norm"][l], g_gain["post_ffn_norm"][l - 1] = hosted(
                ("bwd", l, "in_proj_dx"), _in_proj_dx_norms, d_attn3, d_conv3, w["w_in"], dx1,
                (sv["x"], gains["pre_mix_norm"], l), (saved[l - 1]["f"], gains["post_ffn_norm"], l - 1), tm=tm)
        else:
            dh1 = hosted(("bwd", l, "in_proj_dx"), _in_proj_dx, d_attn3, d_conv3, w["w_in"], tm=1024)
            dres, _, g_gain["pre_mix_norm"][l], _ = hosted(
                ("bwd", l, "norm_low"), _norm_bwd, dx1, (dh1, sv["x"], gains["pre_mix_norm"], l), None, tm=tm)

    g_gain = {k: jnp.concatenate(v, axis=0) for k, v in g_gain.items()}
    return loss_lanes, dres, g_gain, jnp.stack(g_conv, axis=0)


class _Exchange:
    GATHER_HOSTS = {"in_proj": (("w_out", 0), ("w_down", 0)), "attn": (("w_gate_up", 0),), "gate_up": (("w_in", 1),)}

    @staticmethod
    def _reduce_hosts(group, l):
        if group == "ffn":
            return "merge", "attn", l
        if l > 0:
            return "in_proj_dx", "gate_up_dx", l - 1
        return "in_proj_dx", "norm_low", l

    def __init__(self, params, place):
        self.place = place
        self.slabs = {k: [_own_shard_slab(params[k], l, place, BF16) for l in range(DEPTH)] for k in MATRIX_NAMES}
        self.gathered = {k: [None] * DEPTH for k in MATRIX_NAMES}
        self.gathered["w_in"][0], self.conv_w = _run_comm(
            _gather_comm([self.slabs["w_in"][0]], _own_conv_slab(params["conv_w"], place)), "gather_first")
        self.full = {k: lax.empty(params[k].shape, F32) for k in MATRIX_NAMES}
        self.pending = {}
        self.raw = {}

    def weight(self, name, l):
        if name == "conv_w":
            return self.conv_w
        g = self.gathered[name][l]
        return g.reshape(1, g.shape[0] * g.shape[1], g.shape[2]) if name in ("w_out", "w_down") else g

    def host(self, tag):
        phase, l, kernel = tag
        if phase == "fwd":
            carried = [(name, l + ahead) for name, ahead in self.GATHER_HOSTS.get(kernel, ()) if l + ahead < DEPTH]
            return _gather_comm([self.slabs[name][layer] for name, layer in carried]) if carried else None
        if tag in self.pending:
            stage, _, _, arrays = self.pending[tag]
            return _halves_comm(arrays) if stage == "halves" else _partials_comm(arrays)
        return None

    def hosted(self, tag, results):
        phase, l, kernel = tag
        if phase == "fwd":
            carried = [(name, l + ahead) for name, ahead in self.GATHER_HOSTS[kernel] if l + ahead < DEPTH]
            for (name, layer), slab in zip(carried, results):
                self.gathered[name][layer] = slab
            return
        stage, gl, group, arrays = self.pending.pop(tag)
        names = list(self.raw[(gl, group)])
        if stage == "partials":
            self._finish_reduction(gl, names, arrays, results)
            return
        partials = [_add_halves(self.raw[(gl, group)][k], r, self.place) for k, r in zip(names, results)]
        _, ici_kernel, ici_layer = self._reduce_hosts(group, gl)
        self.pending[("bwd", ici_layer, ici_kernel)] = ("partials", gl, group, partials)

    def grads(self, l, group, grads):
        self.raw[(l, group)] = grads
        self.pending[("bwd", l, self._reduce_hosts(group, l)[0])] = ("halves", l, group, [grads[k] for k in grads])

    def _finish_reduction(self, l, names, partials, others):
        for k, p, q in zip(names, partials, others):
            self.full[k] = _sum_partials(p, q, self.place, self.full[k], l)
        shared = _run_comm(_share_comm([self.full[k] for k in names], l), "share_halves")
        for k, g in zip(names, shared):
            self.full[k] = g


def kernel(x, positions, pre_mix_norm, w_in, conv_w, attn_out_norm, conv_out_norm, w_out, post_mix_norm, pre_ffn_norm, w_gate_up, w_down, post_ffn_norm, loss_target, m_pre_mix_norm, m_w_in, m_conv_w, m_attn_out_norm, m_conv_out_norm, m_w_out, m_post_mix_norm, m_pre_ffn_norm, m_w_gate_up, m_w_down, m_post_ffn_norm, v_pre_mix_norm, v_w_in, v_conv_w, v_attn_out_norm, v_conv_out_norm, v_w_out, v_post_mix_norm, v_pre_ffn_norm, v_w_gate_up, v_w_down, v_post_ffn_norm):
    params = dict(pre_mix_norm=pre_mix_norm, w_in=w_in, conv_w=conv_w, attn_out_norm=attn_out_norm,
                  conv_out_norm=conv_out_norm, w_out=w_out, post_mix_norm=post_mix_norm, pre_ffn_norm=pre_ffn_norm,
                  w_gate_up=w_gate_up, w_down=w_down, post_ffn_norm=post_ffn_norm)
    mom1 = dict(pre_mix_norm=m_pre_mix_norm, w_in=m_w_in, conv_w=m_conv_w, attn_out_norm=m_attn_out_norm,
                conv_out_norm=m_conv_out_norm, w_out=m_w_out, post_mix_norm=m_post_mix_norm,
                pre_ffn_norm=m_pre_ffn_norm, w_gate_up=m_w_gate_up, w_down=m_w_down, post_ffn_norm=m_post_ffn_norm)
    mom2 = dict(pre_mix_norm=v_pre_mix_norm, w_in=v_w_in, conv_w=v_conv_w, attn_out_norm=v_attn_out_norm,
                conv_out_norm=v_conv_out_norm, w_out=v_w_out, post_mix_norm=v_post_mix_norm,
                pre_ffn_norm=v_pre_ffn_norm, w_gate_up=v_w_gate_up, w_down=v_w_down, post_ffn_norm=v_post_ffn_norm)
    xi, yi, ci = lax.axis_index("x"), lax.axis_index("y"), lax.axis_index("c")
    place = jnp.stack([2 * xi + yi, ci]).astype(jnp.int32)

    exchange = _Exchange(params, place)
    gains = {k: params[k][:, None, :] for k in GAIN_NAMES}
    loss_lanes, grad_x, g_gain, g_conv = _local_step(x[0], positions[0], loss_target[0], gains, exchange)
    grad = dict(exchange.full)

    small = [g_gain[k].reshape(-1) for k in GAIN_NAMES] + [g_conv.reshape(-1), loss_lanes.reshape(-1)]
    sizes = [int(s.shape[0]) for s in small]
    flat = jnp.concatenate(small)
    loss_row = (sum(sizes) - LANES) // LANES
    rows = -(-flat.shape[0] // (8 * LANES)) * 8
    flat = jnp.pad(flat, (0, rows * LANES - flat.shape[0])).reshape(rows, LANES)
    total = _allreduce_small(flat, loss_row).reshape(-1)
    offsets = np.cumsum([0] + sizes)
    for i, k in enumerate(GAIN_NAMES):
        grad[k] = total[offsets[i]:offsets[i + 1]].reshape(params[k].shape)
    conv_all = total[offsets[6]:offsets[7]].reshape(DEPTH, N_CHIPS, 3, LANES)
    grad["conv_w"] = lax.dynamic_index_in_dim(conv_all, 2 * xi + yi, axis=1, keepdims=False)
    loss = total[offsets[7]]

    delta, new_m, new_v = {}, {}, {}
    for k in WEIGHT_ORDER:
        shape = params[k].shape
        if k in MATRIX_NAMES:
            tr = {1024: 512, 704: 352, 256: 256}[shape[1]]
            delta[k], new_m[k], new_v[k], grad[k] = _adamw(params[k], grad[k], mom1[k], mom2[k], tr=tr, emit_grad=True)
        else:
            as3 = (lambda a: a) if len(shape) == 3 else (lambda a: a[None])
            d, m, v = _adamw(as3(params[k]), as3(grad[k]), as3(mom1[k]), as3(mom2[k]), tr=as3(params[k]).shape[1])
            delta[k], new_m[k], new_v[k] = d.reshape(shape), m.reshape(shape), v.reshape(shape)

    return (loss, grad_x[None], *[grad[k] for k in WEIGHT_ORDER], *[delta[k] for k in WEIGHT_ORDER],
            *[new_m[k] for k in WEIGHT_ORDER], *[new_v[k] for k in WEIGHT_ORDER])
```

```python
import functools
from typing import Callable, NamedTuple

import numpy as np
import jax
import jax.numpy as jnp
from jax import lax
from jax.experimental import pallas as pl
from jax.experimental.pallas import tpu as pltpu

F32 = jnp.float32
BF16 = jnp.bfloat16
MESH = pl.DeviceIdType.MESH

D_MODEL = 1024
ATTN_W = 512
CONV_W = 512
HEAD_DIM = 64
ROPE_DIM = 16
ROPE_THETA = 500000.0
FFN = 2816
DEPTH = 4
RMS_EPS = 1e-6
NEG_INF = -1e30
N_CHIPS = 4
N_DEV = 8
LANES = 128
BF16_ROWS = 16
DILATIONS = (1, 4, 16)
BAND = 64
TQ = 128
WIN = TQ + 2 * BAND
SCALE = HEAD_DIM ** -0.5

ADAM_LR = 0.001
ADAM_B1 = 0.9
ADAM_B2 = 0.999
ADAM_EPS = 1e-08
ADAM_WD = 0.01
ADAM_STEP = 10

GAIN_NAMES = ("pre_mix_norm", "attn_out_norm", "conv_out_norm", "post_mix_norm", "pre_ffn_norm", "post_ffn_norm")
MATRIX_NAMES = ("w_in", "w_out", "w_gate_up", "w_down")
WEIGHT_ORDER = ("pre_mix_norm", "w_in", "conv_w", "attn_out_norm", "conv_out_norm", "w_out", "post_mix_norm",
                "pre_ffn_norm", "w_gate_up", "w_down", "post_ffn_norm")

ANY = pl.BlockSpec(memory_space=pl.ANY)
WHOLE_VMEM = pl.BlockSpec(memory_space=pltpu.VMEM)
LANE_CONTRACT = (((1,), (1,)), ((), ()))
ROW_CONTRACT = (((0,), (0,)), ((), ()))
CHUNK = 256


def _const_spec(block, index):
    return pl.BlockSpec(block, lambda *_: index)


def _gain_spec(g3, l):
    return _const_spec((None, 1, g3.shape[-1]), (l, 0, 0))


class _Comm(NamedTuple):
    ins: tuple
    inouts: tuple
    out_shapes: tuple
    n_sems: int
    start: Callable
    finish: Callable


def _place():
    x, y, c = lax.axis_index("x"), lax.axis_index("y"), lax.axis_index("c")
    other_chips = [(1 - x, y), (x, 1 - y), (1 - x, 1 - y)]
    return x, y, c, other_chips


def _remote(src, dst, send_sem, recv_sem, to):
    return pltpu.make_async_remote_copy(src_ref=src, dst_ref=dst, send_sem=send_sem, recv_sem=recv_sem,
                                        device_id=to, device_id_type=MESH)


def _call(body, operands, *, name, grid, in_specs, out_specs, out_shape, scratch_shapes=(), comm=None):
    in_specs, out_specs, out_shape = list(in_specs), list(out_specs), list(out_shape)
    scratch_shapes = list(scratch_shapes)
    if comm is None:
        out = pl.pallas_call(body, grid=grid, in_specs=in_specs, out_specs=out_specs, out_shape=out_shape,
                             scratch_shapes=scratch_shapes, name=name)(*operands)
        return list(out), None
    n_in, n_out, n_scr = len(in_specs), len(out_shape), len(scratch_shapes)
    n_ci, n_cio, n_co = len(comm.ins), len(comm.inouts), len(comm.out_shapes)

    def hosted(*refs):
        refs = list(refs)
        ins, c_ins = refs[:n_in], refs[n_in:n_in + n_ci]
        base = n_in + n_ci + n_cio
        outs = refs[base:base + n_out]
        c_io = refs[base + n_out:base + n_out + n_cio]
        c_out = refs[base + n_out + n_cio:base + n_out + n_cio + n_co]
        scr = refs[base + n_out + n_cio + n_co:]
        send_sems, recv_sems = scr[n_scr], scr[n_scr + 1]
        if grid:
            first = functools.reduce(jnp.logical_and, [pl.program_id(a) == 0 for a in range(len(grid))])
            last = functools.reduce(jnp.logical_and, [pl.program_id(a) == grid[a] - 1 for a in range(len(grid))])
            pl.when(first)(lambda: comm.start(c_ins, c_io, c_out, send_sems, recv_sems))
            body(*ins, *outs, *scr[:n_scr])
            pl.when(last)(lambda: comm.finish(c_ins, c_io, c_out, send_sems, recv_sems))
        else:
            comm.start(c_ins, c_io, c_out, send_sems, recv_sems)
            body(*ins, *outs, *scr[:n_scr])
            comm.finish(c_ins, c_io, c_out, send_sems, recv_sems)

    res = pl.pallas_call(
        hosted, grid=grid, in_specs=in_specs + [ANY] * (n_ci + n_cio), out_specs=out_specs + [ANY] * (n_cio + n_co),
        out_shape=out_shape + [jax.ShapeDtypeStruct(a.shape, a.dtype) for a in comm.inouts] + list(comm.out_shapes),
        input_output_aliases={n_in + n_ci + i: n_out + i for i in range(n_cio)},
        scratch_shapes=scratch_shapes + [pltpu.SemaphoreType.DMA((comm.n_sems,))] * 2,
        name=name)(*operands, *comm.ins, *comm.inouts)
    return list(res[:n_out]), list(res[n_out:])


def _run_comm(comm, name):
    return _call(lambda: None, [], name=name, grid=(), in_specs=[], out_specs=[], out_shape=[], comm=comm)[1]


def _row_half(ref, lead, core, rows, align):
    hr = rows // 2
    return ref.at[(*lead, pl.ds(pl.multiple_of(core * hr, align), hr), slice(None))]


def _gather_comm(slabs, conv_slab=None):
    n = len(slabs)
    n_conv = 0 if conv_slab is None else 3

    def direct(ios, send, recv):
        x, y, c, chips = _place()
        copies = []
        for a in range(n):
            own = _row_half(ios[a], (2 * x + y,), c, slabs[a].shape[1], BF16_ROWS)
            copies += [_remote(own, own, send.at[a * 3 + j], recv.at[a * 3 + j], (*chip, c))
                       for j, chip in enumerate(chips)]
        if conv_slab is not None:
            own = ios[n].at[:, 2 * x + y]
            copies += [_remote(own, own, send.at[6 * n + j], recv.at[6 * n + j], (*chip, c))
                       for j, chip in enumerate(chips)]
        return copies

    def start(ins, ios, outs, send, recv):
        for cp in direct(ios, send, recv):
            cp.start()

    def finish(ins, ios, outs, send, recv):
        x, y, c, chips = _place()
        sibling = (x, y, 1 - c)
        passed = []
        for a in range(n):
            for j, chip in enumerate(chips):
                landed = _row_half(ios[a], (2 * chip[0] + chip[1],), c, slabs[a].shape[1], BF16_ROWS)
                _remote(landed, landed, send.at[a * 3 + j], recv.at[a * 3 + j], (*chip, c)).wait_recv()
                fwd = _remote(landed, landed, send.at[3 * n + a * 3 + j], recv.at[3 * n + a * 3 + j], sibling)
                fwd.start()
                passed.append(fwd)
        if conv_slab is not None:
            for j, chip in enumerate(chips):
                landed = ios[n].at[:, 2 * chip[0] + chip[1]]
                _remote(landed, landed, send.at[6 * n + j], recv.at[6 * n + j], (*chip, c)).wait_recv()
        for a in range(n):
            for j, chip in enumerate(chips):
                landed = _row_half(ios[a], (2 * chip[0] + chip[1],), 1 - c, slabs[a].shape[1], BF16_ROWS)
                _remote(landed, landed, send.at[3 * n + a * 3 + j], recv.at[3 * n + a * 3 + j], sibling).wait_recv()
        for cp in direct(ios, send, recv) + passed:
            cp.wait_send()

    inouts = tuple(slabs) + (() if conv_slab is None else (conv_slab,))
    return _Comm((), inouts, (), 6 * n + n_conv, start, finish)


def _halves_comm(grads):
    n = len(grads)

    def copies(ins, outs, send, recv):
        x, y, c, _ = _place()
        return [_remote(_row_half(ins[a], (slice(None),), 1 - c, grads[a].shape[1], 8), outs[a],
                        send.at[a], recv.at[a], (x, y, 1 - c)) for a in range(n)]

    def start(ins, ios, outs, send, recv):
        for cp in copies(ins, outs, send, recv):
            cp.start()

    def finish(ins, ios, outs, send, recv):
        for cp in copies(ins, outs, send, recv):
            cp.wait()

    out_shapes = tuple(jax.ShapeDtypeStruct((g.shape[0], g.shape[1] // 2, g.shape[2]), F32) for g in grads)
    return _Comm(tuple(grads), (), out_shapes, n, start, finish)


def _partials_comm(partials):
    n = len(partials)

    def copies(ins, outs, send, recv):
        x, y, c, chips = _place()
        return [_remote(ins[a].at[2 * chip[0] + chip[1]], outs[a].at[k], send.at[a * 3 + k], recv.at[a * 3 + k],
                        (*chip, c)) for a in range(n) for k, chip in enumerate(chips)]

    def start(ins, ios, outs, send, recv):
        for cp in copies(ins, outs, send, recv):
            cp.start()

    def finish(ins, ios, outs, send, recv):
        for cp in copies(ins, outs, send, recv):
            cp.wait()

    out_shapes = tuple(jax.ShapeDtypeStruct((3,) + p.shape[1:], BF16) for p in partials)
    return _Comm(tuple(partials), (), out_shapes, 3 * n, start, finish)


def _share_comm(grads):
    n = len(grads)

    def start(ins, ios, outs, send, recv):
        x, y, c, _ = _place()
        for a in range(n):
            mine = _row_half(ios[a], (slice(None),), c, grads[a].shape[1], 8)
            _remote(mine, mine, send.at[a], recv.at[a], (x, y, 1 - c)).start()

    def finish(ins, ios, outs, send, recv):
        x, y, c, _ = _place()
        for a in range(n):
            theirs = _row_half(ios[a], (slice(None),), 1 - c, grads[a].shape[1], 8)
            _remote(theirs, theirs, send.at[a], recv.at[a], (x, y, 1 - c)).wait()

    return _Comm((), tuple(grads), (), n, start, finish)


def _rms_fwd(x, g):
    r = lax.rsqrt(jnp.mean(x * x, axis=-1, keepdims=True) + RMS_EPS)
    return (x * r) * g


def _rms_bwd(x, g, dy):
    r = lax.rsqrt(jnp.mean(x * x, axis=-1, keepdims=True) + RMS_EPS)
    xh = x * r
    u = dy * g
    dx = r * (u - xh * jnp.mean(xh * u, axis=-1, keepdims=True))
    return dx, jnp.sum(dy * xh, axis=0, keepdims=True)


def _accumulate(ref, value, first):
    @pl.when(first)
    def _():
        ref[...] = value

    @pl.when(jnp.logical_not(first))
    def _():
        ref[...] += value


def _rope_coeffs(cos, sin):
    m = lax.broadcasted_iota(jnp.int32, cos.shape, 1) % HEAD_DIM
    a = jnp.where(m < ROPE_DIM, cos, 1.0)
    b = jnp.where(m < ROPE_DIM // 2, -sin, 0.0)
    c = jnp.where((m >= ROPE_DIM // 2) & (m < ROPE_DIM), sin, 0.0)
    return a, b, c


def _rope_apply(t, cos, sin):
    a, b, c = _rope_coeffs(cos, sin)
    n = t.shape[1]
    return a * t + b * pltpu.roll(t, n - ROPE_DIM // 2, 1) + c * pltpu.roll(t, ROPE_DIM // 2, 1)


def _rope_transpose(dt, cos, sin):
    a, b, c = _rope_coeffs(cos, sin)
    n = dt.shape[1]
    return a * dt + pltpu.roll(b * dt, ROPE_DIM // 2, 1) + pltpu.roll(c * dt, n - ROPE_DIM // 2, 1)


def _in_proj(h, w, cos, sin, *, tm, comm=None):
    t, k = h.shape
    s_n, _, n = w.shape
    assert t % tm == 0 and n % LANES == 0

    rotary_shards = -(-2 * ATTN_W // n)
    rotary = list(range(0, 2 * ATTN_W, LANES))
    per_shard = -(-len(rotary) // max(s_n - rotary_shards, 1))

    def body(h_ref, w_ref, cos_ref, sin_ref, o_ref):
        def shard(s):
            o_ref[:, s * n:(s + 1) * n] = jnp.dot(h_ref[...], w_ref[s], preferred_element_type=F32)

        def rope(c0):
            cols = slice(c0, c0 + LANES)
            o_ref[:, cols] = _rope_apply(o_ref[:, cols], cos_ref[...], sin_ref[...])

        for s in range(rotary_shards):
            shard(s)
        pending = list(rotary)
        for s in range(rotary_shards, s_n):
            for c0 in pending[:per_shard]:
                rope(c0)
            pending = pending[per_shard:]
            shard(s)
        for c0 in pending:
            rope(c0)

    lane_tile = pl.BlockSpec((tm, LANES), lambda i: (i, 0))
    out, got = _call(
        body, [h, w, cos, sin], name="in_proj", grid=(t // tm,),
        in_specs=[pl.BlockSpec((tm, k), lambda i: (i, 0)), _const_spec(w.shape, (0, 0, 0)), lane_tile, lane_tile],
        out_specs=[pl.BlockSpec((tm, s_n * n), lambda i: (i, 0))],
        out_shape=[jax.ShapeDtypeStruct((t, s_n * n), F32)], comm=comm)
    return out[0], got


def _mm_nt(a, w, *, tm, tko, name, comm=None):
    t, sn = a.shape
    s_n, ko, n = w.shape
    assert sn == s_n * n and t % tm == 0 and ko % tko == 0

    def body(a_ref, w_ref, o_ref):
        acc = lax.dot_general(a_ref[...], w_ref[...], (((1,), (1,)), ((), ())), preferred_element_type=F32)
        if s_n == 1:
            o_ref[...] = acc
        else:
            _accumulate(o_ref, acc, pl.program_id(2) == 0)

    out, got = _call(
        body, [a, w], name=name, grid=(t // tm, ko // tko, s_n),
        in_specs=[pl.BlockSpec((tm, n), lambda i, j, s: (i, s)),
                  pl.BlockSpec((None, tko, n), lambda i, j, s: (s, j, 0))],
        out_specs=[pl.BlockSpec((tm, tko), lambda i, j, s: (i, j))],
        out_shape=[jax.ShapeDtypeStruct((t, ko), F32)], comm=comm)
    return out[0], got


def _dx_through_norms(operands, in_specs, dx_rows, dres, pre, post, *, tm, name, comm=None):
    t, d_model = dres.shape
    (x, gx3, lx), (y, gy3, ly) = pre, post
    n_op = len(operands)
    row = pl.BlockSpec((tm, d_model), lambda i: (i, 0))
    gsum = _const_spec((1, d_model), (0, 0))

    def body(*refs):
        d_ref, x_ref, gx_ref, y_ref, gy_ref, dn_ref, dgx_ref, dy_ref, dgy_ref = refs[n_op:]
        first = pl.program_id(0) == 0
        dx, dgx = _rms_bwd(x_ref[...], gx_ref[...], dx_rows(refs[:n_op]))
        d_new = d_ref[...] + dx
        dn_ref[...] = d_new
        _accumulate(dgx_ref, dgx, first)
        dy, dgy = _rms_bwd(y_ref[...], gy_ref[...], d_new)
        dy_ref[...] = dy.astype(BF16)
        _accumulate(dgy_ref, dgy, first)

    out, got = _call(
        body, list(operands) + [dres, x, gx3, y, gy3], name=name, grid=(t // tm,),
        in_specs=list(in_specs) + [row, row, _gain_spec(gx3, lx), row, _gain_spec(gy3, ly)],
        out_specs=[row, gsum, row, gsum],
        out_shape=[jax.ShapeDtypeStruct((t, d_model), F32), jax.ShapeDtypeStruct((1, d_model), F32),
                   jax.ShapeDtypeStruct((t, d_model), BF16), jax.ShapeDtypeStruct((1, d_model), F32)],
        comm=comm)
    return out[0], out[2], out[1], out[3], got


def _gate_up_dx_norms(dg, du, w, dres, pre, post, *, tm, comm=None):
    s_n, ko, n = w.shape
    half = s_n // 2

    def dx_rows(refs):
        dg_ref, du_ref, w_ref = refs
        acc = jnp.zeros((tm, ko), F32)
        for s in range(half):
            cols = slice(s * n, (s + 1) * n)
            acc = acc + lax.dot_general(dg_ref[:, cols], w_ref[s], LANE_CONTRACT, preferred_element_type=F32)
            acc = acc + lax.dot_general(du_ref[:, cols], w_ref[half + s], LANE_CONTRACT, preferred_element_type=F32)
        return acc

    a_spec = pl.BlockSpec((tm, half * n), lambda i: (i, 0))
    return _dx_through_norms([dg, du, w], [a_spec, a_spec, _const_spec(w.shape, (0, 0, 0))], dx_rows, dres, pre, post,
                             tm=tm, name="gate_up_dx", comm=comm)


def _in_proj_dx_norms(d_attn3, d_conv3, w, dres, pre, post, *, tm, comm=None):
    s_n, ko, n = w.shape
    per = n // CHUNK
    pieces = d_attn3.shape[0]
    width = d_attn3.shape[2]

    def dx_rows(refs):
        a_ref, b_ref, w_ref = refs
        acc = jnp.zeros((tm, ko), F32)
        for c in range(s_n * per):
            src = a_ref if c // 2 < pieces else b_ref
            piece, c0 = (c // 2) % pieces, (c % 2) * CHUNK
            acc = acc + lax.dot_general(src[piece, :, c0:c0 + CHUNK], w_ref[c // per, :, (c % per) * CHUNK:(c % per + 1) * CHUNK],
                                        LANE_CONTRACT, preferred_element_type=F32)
        return acc

    stack = pl.BlockSpec((pieces, tm, width), lambda i: (0, i, 0))
    return _dx_through_norms([d_attn3, d_conv3, w], [stack, stack, _const_spec(w.shape, (0, 0, 0))], dx_rows, dres, pre,
                             post, tm=tm, name="in_proj_dx", comm=comm)


def _mm_tn(a, b, s_n, *, tka, name, into=None, shard0=0):
    t, ka = a.shape
    n = b.shape[1] // s_n
    assert b.shape[0] == t and ka % tka == 0

    def body(a_ref, b_ref, *rest):
        rest[-1][...] = lax.dot_general(a_ref[...], b_ref[...], ROW_CONTRACT, preferred_element_type=F32)

    operands, in_specs, aliases = [a, b], [pl.BlockSpec((t, tka), lambda i, s: (0, i)),
                                           pl.BlockSpec((t, n), lambda i, s: (0, s))], {}
    out_shape = jax.ShapeDtypeStruct((s_n, ka, n), F32)
    if into is not None:
        operands, in_specs, aliases = operands + [into], in_specs + [ANY], {2: 0}
        out_shape = jax.ShapeDtypeStruct(into.shape, F32)
    return pl.pallas_call(
        body, grid=(ka // tka, s_n), in_specs=in_specs,
        out_specs=pl.BlockSpec((None, tka, n), lambda i, s: (shard0 + s, i, 0)),
        out_shape=out_shape, input_output_aliases=aliases, name=name)(*operands)


def _gate_up_swiglu(h, w, *, tm, comm=None):
    t, k = h.shape
    s_n, _, n = w.shape
    half = s_n // 2

    def body(h_ref, wg_ref, wu_ref, g_ref, u_ref, a_ref):
        g = jnp.dot(h_ref[...], wg_ref[...], preferred_element_type=F32)
        u = jnp.dot(h_ref[...], wu_ref[...], preferred_element_type=F32)
        g_ref[...] = g.astype(BF16)
        u_ref[...] = u.astype(BF16)
        a_ref[...] = (g * jax.nn.sigmoid(g) * u).astype(BF16)

    col = pl.BlockSpec((tm, n), lambda i, j: (i, j))
    out, got = _call(
        body, [h, w, w], name="gate_up", grid=(t // tm, half),
        in_specs=[pl.BlockSpec((tm, k), lambda i, j: (i, 0)), pl.BlockSpec((None, k, n), lambda i, j: (j, 0, 0)),
                  pl.BlockSpec((None, k, n), lambda i, j: (half + j, 0, 0))],
        out_specs=[col, col, col],
        out_shape=[jax.ShapeDtypeStruct((t, half * n), BF16)] * 3, comm=comm)
    return out[0], out[1], out[2], got


def _down_dx_swiglu_bwd(df, w, g, u, *, tm, tko):
    t, k = df.shape
    _, ko, _ = w.shape
    assert t % tm == 0 and ko % tko == 0

    def body(df_ref, w_ref, g_ref, u_ref, dg_ref, du_ref):
        d = lax.dot_general(df_ref[...], w_ref[...], LANE_CONTRACT, preferred_element_type=F32)
        gg = g_ref[...].astype(F32)
        sig = jax.nn.sigmoid(gg)
        dg_ref[...] = (d * u_ref[...].astype(F32) * (sig * (1.0 + gg * (1.0 - sig)))).astype(BF16)
        du_ref[...] = (d * (gg * sig)).astype(BF16)

    col = pl.BlockSpec((tm, tko), lambda i, j: (i, j))
    return pl.pallas_call(
        body, grid=(t // tm, ko // tko),
        in_specs=[pl.BlockSpec((tm, k), lambda i, j: (i, 0)), pl.BlockSpec((None, tko, k), lambda i, j: (0, j, 0)),
                  col, col],
        out_specs=[col, col], out_shape=[jax.ShapeDtypeStruct((t, ko), BF16)] * 2, name="down_dx")(df, w, g, u)


def _in_proj_dx(d_attn3, d_conv3, w, *, tm, comm=None):
    _, t, _ = d_attn3.shape
    s_n, ko, n = w.shape
    half, per = s_n // 2, n // CHUNK
    assert t % tm == 0

    def body(*refs):
        a_refs, b_refs, wa_ref, wb_ref, o_ref = refs[:per], refs[per:2 * per], refs[2 * per], refs[2 * per + 1], refs[-1]
        acc = jnp.zeros(o_ref.shape, F32)
        for r in range(per):
            cols = slice(r * CHUNK, (r + 1) * CHUNK)
            acc = acc + lax.dot_general(a_refs[r][...], wa_ref[:, cols], LANE_CONTRACT, preferred_element_type=F32)
            acc = acc + lax.dot_general(b_refs[r][...], wb_ref[:, cols], LANE_CONTRACT, preferred_element_type=F32)
        _accumulate(o_ref, acc, pl.program_id(1) == 0)

    piece = lambda r: pl.BlockSpec((None, tm, CHUNK), lambda i, s: ((per * s + r) // 2, i, (per * s + r) % 2))
    out, got = _call(
        body, [d_attn3] * per + [d_conv3] * per + [w, w], name="in_proj_dx", grid=(t // tm, half),
        in_specs=[piece(r) for r in range(per)] * 2
        + [pl.BlockSpec((None, ko, n), lambda i, s: (s, 0, 0)), pl.BlockSpec((None, ko, n), lambda i, s: (half + s, 0, 0))],
        out_specs=[pl.BlockSpec((tm, ko), lambda i, s: (i, 0))],
        out_shape=[jax.ShapeDtypeStruct((t, ko), F32)], comm=comm)
    return out[0], got


def _in_proj_dw(h, d_attn3, d_conv3, s_n, *, tka):
    t, ka = h.shape
    half = s_n // 2
    n = 3 * d_attn3.shape[2] // half
    per = n // CHUNK
    assert ka % tka == 0

    def body(*refs):
        h_ref, o_ref = refs[0], refs[-1]
        for side in range(2):
            for r in range(per):
                o_ref[side, :, r * CHUNK:(r + 1) * CHUNK] = lax.dot_general(
                    h_ref[...], refs[1 + side * per + r][...], ROW_CONTRACT, preferred_element_type=F32)

    piece = lambda r: pl.BlockSpec((None, t, CHUNK), lambda i, s: ((per * s + r) // 2, 0, (per * s + r) % 2))
    out = pl.pallas_call(
        body, grid=(ka // tka, half),
        in_specs=[pl.BlockSpec((t, tka), lambda i, s: (0, i))] + [piece(r) for r in range(per)] * 2,
        out_specs=pl.BlockSpec((2, None, tka, n), lambda i, s: (0, s, i, 0)),
        out_shape=jax.ShapeDtypeStruct((2, half, ka, n), F32), name="in_proj_dw")(h, *[d_attn3] * per, *[d_conv3] * per)
    return out.reshape(s_n, ka, n)


def _rope_tables(positions_col, inv_freq_row):
    t = positions_col.shape[0]

    def body(pos_ref, f_ref, cos_ref, sin_ref):
        ang = pos_ref[...].astype(F32) * f_ref[...]
        cos_ref[...] = jnp.cos(ang)
        sin_ref[...] = jnp.sin(ang)

    return pl.pallas_call(
        body, out_shape=[jax.ShapeDtypeStruct((t, LANES), F32)] * 2, name="rope_tables")(positions_col, inv_freq_row)


def _norm_fwd(x, g3, l, *, tm):
    t, w = x.shape

    def body(x_ref, g_ref, h_ref):
        h_ref[...] = _rms_fwd(x_ref[...], g_ref[...]).astype(BF16)

    return pl.pallas_call(
        body, grid=(t // tm,),
        in_specs=[pl.BlockSpec((tm, w), lambda i: (i, 0)), _gain_spec(g3, l)],
        out_specs=pl.BlockSpec((tm, w), lambda i: (i, 0)),
        out_shape=jax.ShapeDtypeStruct((t, w), BF16), name="norm_fwd")(x, g3)


def _mm_resnorm(a, w, x, g_post3, l_post, g_next3, l_next, *, tm, name, comm=None):
    t, k = a.shape
    _, _, n = w.shape
    with_next = g_next3 is not None
    row = pl.BlockSpec((tm, n), lambda i: (i, 0))

    def body(a_ref, w_ref, x_ref, gp_ref, *rest):
        y = jnp.dot(a_ref[...], w_ref[...], preferred_element_type=F32)
        x_new = x_ref[...] + _rms_fwd(y, gp_ref[...])
        if with_next:
            gn_ref, y_ref, xo_ref, h_ref = rest
            h_ref[...] = _rms_fwd(x_new, gn_ref[...]).astype(BF16)
        else:
            y_ref, xo_ref = rest
        y_ref[...] = y
        xo_ref[...] = x_new

    ins = [a, w, x, g_post3] + ([g_next3] if with_next else [])
    in_specs = ([pl.BlockSpec((tm, k), lambda i: (i, 0)), _const_spec((None, k, n), (0, 0, 0)), row,
                 _gain_spec(g_post3, l_post)] + ([_gain_spec(g_next3, l_next)] if with_next else []))
    out_shape = [jax.ShapeDtypeStruct((t, n), F32)] * 2 + ([jax.ShapeDtypeStruct((t, n), BF16)] if with_next else [])
    out, got = _call(body, ins, name=name, grid=(t // tm,), in_specs=in_specs, out_specs=[row] * len(out_shape),
                     out_shape=out_shape, comm=comm)
    return out[0], out[1], (out[2] if with_next else None), got


def _conv_fwd(proj, conv_w, l):
    t = proj.shape[0]
    col0 = 3 * ATTN_W // LANES

    def body(u_ref, gb_ref, gc_ref, w_ref, y_ref):
        c = gc_ref[...] * u_ref[...]
        row = lax.broadcasted_iota(jnp.int32, c.shape, 0)
        c_prev = jnp.where(row == 0, 0.0, pltpu.roll(c, 1, 0))
        c_next = jnp.where(row == t - 1, 0.0, pltpu.roll(c, t - 1, 0))
        w = w_ref[...]
        y_ref[...] = gb_ref[...] * (w[0:1] * c_prev + w[1:2] * c + w[2:3] * c_next)

    nj = CONV_W // LANES
    cols = lambda base: pl.BlockSpec((t, LANES), lambda j: (0, base + j))
    return pl.pallas_call(
        body, grid=(nj,),
        in_specs=[cols(col0), cols(col0 + nj), cols(col0 + 2 * nj),
                  pl.BlockSpec((None, None, 3, LANES), lambda j: (l, j, 0, 0))],
        out_specs=pl.BlockSpec((t, LANES), lambda j: (0, j)),
        out_shape=jax.ShapeDtypeStruct((t, CONV_W), F32), name="conv_fwd")(proj, proj, proj, conv_w)


def _merge_fwd(attn, conv_y, ga3, gc3, l, *, tm):
    t = attn.shape[0]
    row = pl.BlockSpec((tm, ATTN_W), lambda i: (i, 0))

    def body(a_ref, c_ref, ga_ref, gc_ref, m_ref):
        m_ref[:, :ATTN_W] = _rms_fwd(a_ref[...], ga_ref[...]).astype(BF16)
        m_ref[:, ATTN_W:] = _rms_fwd(c_ref[...], gc_ref[...]).astype(BF16)

    return pl.pallas_call(
        body, grid=(t // tm,),
        in_specs=[row, row, _gain_spec(ga3, l), _gain_spec(gc3, l)],
        out_specs=pl.BlockSpec((tm, D_MODEL), lambda i: (i, 0)),
        out_shape=jax.ShapeDtypeStruct((t, D_MODEL), BF16), name="merge_fwd")(attn, conv_y, ga3, gc3)


def _loss_fwd_bwd(y, target, *, tm):
    t, w = y.shape
    row = pl.BlockSpec((tm, w), lambda i: (i, 0))

    def body(y_ref, t_ref, dy_ref, loss_ref):
        e = y_ref[...] - t_ref[...]
        dy_ref[...] = e * (1.0 / w)
        sq = jnp.sum(e * e, axis=0, keepdims=True) * (0.5 / w)
        part = sq[:, :LANES]
        for j in range(1, w // LANES):
            part = part + sq[:, j * LANES:(j + 1) * LANES]
        _accumulate(loss_ref, part, pl.program_id(0) == 0)

    return pl.pallas_call(
        body, grid=(t // tm,), in_specs=[row, row],
        out_specs=[row, _const_spec((1, LANES), (0, 0))],
        out_shape=[jax.ShapeDtypeStruct((t, w), F32), jax.ShapeDtypeStruct((1, LANES), F32)], name="loss")(y, target)


def _tile_rows(t, nt, lb, d):
    r = t // nt
    q0 = (t % nt) * TQ
    m0 = jnp.clip(q0 - BAND, 0, lb - WIN)
    if d == 1:
        return pl.ds(pl.multiple_of(q0, TQ), TQ), pl.ds(pl.multiple_of(m0, BAND), WIN), m0 - q0
    return pl.ds(r + d * q0, TQ, stride=d), pl.ds(r + d * m0, WIN, stride=d), m0 - q0


def _for_row_chunks(t, fn, chunk=512):
    def step(i, carry):
        fn(pl.ds(pl.multiple_of(i * chunk, chunk), chunk))
        return carry

    lax.fori_loop(0, t // chunk, step, 0)


WINDOW_OFFSETS = (-BAND, 0, -2 * BAND)


def _fill_band_bias(bias_ref):
    rel0 = (lax.broadcasted_iota(jnp.int32, (2 * TQ, WIN), 1)
            - lax.broadcasted_iota(jnp.int32, (2 * TQ, WIN), 0) % TQ)
    for j, off in enumerate(WINDOW_OFFSETS):
        rel = rel0 + off
        bias_ref[j] = jnp.where((rel >= -BAND) & (rel <= BAND), 0.0, NEG_INF)


def _fill_sequence_bias(bias_ref):
    rel = (lax.broadcasted_iota(jnp.int32, (2 * WIN, WIN), 1) - lax.broadcasted_iota(jnp.int32, (2 * WIN, WIN), 0) % WIN)
    bias_ref[...] = jnp.where((rel >= -BAND) & (rel <= BAND), 0.0, NEG_INF)


def _band_bias(bias_ref, off):
    return bias_ref[jnp.where(off == WINDOW_OFFSETS[0], 0, jnp.where(off == WINDOW_OFFSETS[1], 1, 2))]


def _stack_heads(a, first_head):
    return jnp.concatenate([jnp.where(first_head, a, 0.0), jnp.where(first_head, 0.0, a)], axis=0)


def _unstack_heads(a2, first_head):
    n = a2.shape[0] // 2
    return jnp.where(first_head, a2[:n], a2[n:])


def _attn_fwd(proj, comm=None):
    t = proj.shape[0]
    npair = ATTN_W // LANES

    def body(q_ref, k_ref, v_ref, o_ref, lse_ref, o1, o2, l0, l1, l2, m1, m2, bias, bias_seq):
        _fill_band_bias(bias)
        _fill_sequence_bias(bias_seq)
        outs, dens, maxs = (o_ref, o1, o2), (l0, l1, l2), (lse_ref, m1, m2)

        def softmax_tile(b, qrows, krows, n_q, band_bias):
            first_head = lax.broadcasted_iota(jnp.int32, (n_q, LANES), 1) < HEAD_DIM
            q2 = _stack_heads(q_ref[qrows, :] * SCALE, first_head).astype(BF16)
            kw = k_ref[krows, :].astype(BF16)
            vw = jnp.concatenate([v_ref[krows, :].astype(BF16), jnp.ones((WIN, LANES), BF16)], axis=1)
            s = lax.dot_general(q2, kw, LANE_CONTRACT, preferred_element_type=F32) + band_bias
            m = jnp.max(s, axis=-1, keepdims=True)
            pv = jnp.dot(jnp.exp(s - m).astype(BF16), vw, preferred_element_type=F32)
            outs[b][qrows, :] = _unstack_heads(pv[:, :LANES], first_head)
            dens[b][qrows, :] = _unstack_heads(pv[:, LANES:], first_head)
            maxs[b][qrows, :] = _unstack_heads(jnp.broadcast_to(m, (2 * n_q, LANES)), first_head)

        for b, d in enumerate(DILATIONS):
            lb = t // d
            if lb == WIN:
                def sequence(r, carry, b=b, d=d):
                    rows = pl.ds(r, WIN, stride=d)
                    softmax_tile(b, rows, rows, WIN, bias_seq[...])
                    return carry

                lax.fori_loop(0, d, sequence, 0, unroll=4)
                continue
            nt = lb // TQ

            def tile(ti, carry, b=b, d=d, lb=lb, nt=nt):
                qrows, krows, off = _tile_rows(ti, nt, lb, d)
                softmax_tile(b, qrows, krows, TQ, _band_bias(bias, off))
                return carry

            lax.fori_loop(0, d * nt, tile, 0, unroll=8)

        def finish(rows):
            ms = [m_b[rows, :] for m_b in maxs]
            m_all = jnp.maximum(jnp.maximum(ms[0], ms[1]), ms[2])
            ws = [jnp.exp(m_b - m_all) for m_b in ms]
            den = ws[0] * dens[0][rows, :] + ws[1] * dens[1][rows, :] + ws[2] * dens[2][rows, :]
            num = ws[0] * outs[0][rows, :] + ws[1] * outs[1][rows, :] + ws[2] * outs[2][rows, :]
            o_ref[rows, :] = num / den
            lse_ref[rows, :] = m_all + jnp.log(den)

        _for_row_chunks(t, finish, 256)

    cols = lambda base: pl.BlockSpec((t, LANES), lambda g: (0, base + g))
    out, got = _call(
        body, [proj, proj, proj], name="attn_fwd", grid=(npair,),
        in_specs=[cols(0), cols(npair), cols(2 * npair)],
        out_specs=[cols(0), cols(0)],
        out_shape=[jax.ShapeDtypeStruct((t, ATTN_W), F32)] * 2,
        scratch_shapes=[pltpu.VMEM((t, LANES), F32)] * 7 + [pltpu.VMEM((len(WINDOW_OFFSETS), 2 * TQ, WIN), F32),
                                                            pltpu.VMEM((2 * WIN, WIN), F32)],
        comm=comm)
    return out[0], out[1], got


def _attn_bwd(proj, cos, sin, d_attn, lse, delta, comm=None):
    t = proj.shape[0]
    npair = ATTN_W // LANES

    def body(q_ref, k_ref, v_ref, cos_ref, sin_ref, do_ref, l_ref, dl_ref, dqkv_ref,
             dq_acc, dk_acc, dv_acc, bias, bias_seq):
        _fill_band_bias(bias)
        _fill_sequence_bias(bias_seq)
        dq_acc[...] = jnp.zeros(dq_acc.shape, F32)
        dk_acc[...] = jnp.zeros(dk_acc.shape, F32)
        dv_acc[...] = jnp.zeros(dv_acc.shape, F32)
        def stack_column(a):
            return jnp.concatenate([a[:, 0:1], a[:, HEAD_DIM:HEAD_DIM + 1]], axis=0)

        def grad_tile(qrows, krows, n_q, band_bias):
            first_head = lax.broadcasted_iota(jnp.int32, (n_q, LANES), 1) < HEAD_DIM
            q2 = _stack_heads(q_ref[qrows, :] * SCALE, first_head).astype(BF16)
            do2 = _stack_heads(do_ref[qrows, :], first_head).astype(BF16)
            kw = k_ref[krows, :].astype(BF16)
            vw = v_ref[krows, :].astype(BF16)
            s = lax.dot_general(q2, kw, LANE_CONTRACT, preferred_element_type=F32) + band_bias
            p = jnp.exp(s - stack_column(l_ref[qrows, :]))
            dp = lax.dot_general(do2, vw, LANE_CONTRACT, preferred_element_type=F32)
            ds = (p * (dp - stack_column(dl_ref[qrows, :]))).astype(BF16)
            dq2 = jnp.dot(ds, kw, preferred_element_type=F32)
            dq_acc[qrows, :] += _unstack_heads(dq2, first_head) * SCALE
            dk_acc[krows, :] += lax.dot_general(ds, q2, ROW_CONTRACT, preferred_element_type=F32)
            dv_acc[krows, :] += lax.dot_general(p.astype(BF16), do2, ROW_CONTRACT, preferred_element_type=F32)

        for d in DILATIONS:
            lb = t // d
            if lb == WIN:
                def sequence(r, carry, d=d):
                    rows = pl.ds(r, WIN, stride=d)
                    grad_tile(rows, rows, WIN, bias_seq[...])
                    return carry

                lax.fori_loop(0, d, sequence, 0, unroll=2)
                continue
            nt = lb // TQ

            def tile(ti, carry, d=d, lb=lb, nt=nt):
                qrows, krows, off = _tile_rows(ti, nt, lb, d)
                grad_tile(qrows, krows, TQ, _band_bias(bias, off))
                return carry

            lax.fori_loop(0, d * nt, tile, 0, unroll=4)

        def finish(rows):
            dqkv_ref[0, rows, :] = _rope_transpose(dq_acc[rows, :], cos_ref[rows, :], sin_ref[rows, :]).astype(BF16)
            dqkv_ref[1, rows, :] = _rope_transpose(dk_acc[rows, :], cos_ref[rows, :], sin_ref[rows, :]).astype(BF16)
            dqkv_ref[2, rows, :] = dv_acc[rows, :].astype(BF16)

        _for_row_chunks(t, finish)

    cols = lambda base: pl.BlockSpec((t, LANES), lambda g: (0, base + g))
    out, got = _call(
        body, [proj, proj, proj, cos, sin, d_attn, lse, delta], name="attn_bwd", grid=(npair,),
        in_specs=[cols(0), cols(npair), cols(2 * npair), WHOLE_VMEM, WHOLE_VMEM, cols(0), cols(0), cols(0)],
        out_specs=[pl.BlockSpec((3, t, LANES), lambda g: (0, 0, g))],
        out_shape=[jax.ShapeDtypeStruct((3, t, ATTN_W), BF16)],
        scratch_shapes=[pltpu.VMEM((t, LANES), F32)] * 3 + [pltpu.VMEM((len(WINDOW_OFFSETS), 2 * TQ, WIN), F32),
                                                            pltpu.VMEM((2 * WIN, WIN), F32)],
        comm=comm)
    return out[0], got


def _norm_bwd(dres, pre, post, *, tm, comm=None):
    t, w = dres.shape
    row = pl.BlockSpec((tm, w), lambda i: (i, 0))
    gsum = _const_spec((1, w), (0, 0))
    ins, in_specs, out_shape, out_specs = [dres], [row], [], []
    if pre is not None:
        dh, x, g3, l = pre
        ins += [dh, x, g3]
        in_specs += [row, row, _gain_spec(g3, l)]
        out_shape += [jax.ShapeDtypeStruct((t, w), F32), jax.ShapeDtypeStruct((1, w), F32)]
        out_specs += [row, gsum]
    if post is not None:
        y, g3, l = post
        ins += [y, g3]
        in_specs += [row, _gain_spec(g3, l)]
        out_shape += [jax.ShapeDtypeStruct((t, w), BF16), jax.ShapeDtypeStruct((1, w), F32)]
        out_specs += [row, gsum]
    n_in = len(ins)

    def body(*refs):
        first = pl.program_id(0) == 0
        ins_r, outs_r = list(refs[:n_in]), list(refs[n_in:])
        d = ins_r.pop(0)[...]
        if pre is not None:
            dh_ref, x_ref, g_ref = ins_r[:3]
            ins_r = ins_r[3:]
            dx, dg = _rms_bwd(x_ref[...], g_ref[...], dh_ref[...])
            d = d + dx
            outs_r.pop(0)[...] = d
            _accumulate(outs_r.pop(0), dg, first)
        if post is not None:
            y_ref, g_ref = ins_r
            dy, dg = _rms_bwd(y_ref[...], g_ref[...], d)
            outs_r.pop(0)[...] = dy.astype(BF16)
            _accumulate(outs_r.pop(0), dg, first)

    out, got = _call(body, ins, name="norm_bwd", grid=(t // tm,), in_specs=in_specs, out_specs=out_specs,
                     out_shape=out_shape, comm=comm)
    d_new, dg_pre = (out.pop(0), out.pop(0)) if pre is not None else (None, None)
    dy, dg_post = (out.pop(0), out.pop(0)) if post is not None else (None, None)
    return d_new, dy, dg_pre, dg_post, got


def _merge_bwd(d_merged, attn, conv_y, ga3, gc3, l, *, tm, comm=None):
    t = attn.shape[0]
    row = pl.BlockSpec((tm, ATTN_W), lambda i: (i, 0))
    gsum = _const_spec((1, ATTN_W), (0, 0))

    def body(dma_ref, dmc_ref, a_ref, c_ref, ga_ref, gc_ref, da_ref, dl_ref, dc_ref, dga_ref, dgc_ref):
        first = pl.program_id(0) == 0
        attn_t = a_ref[...]
        da, dga = _rms_bwd(attn_t, ga_ref[...], dma_ref[...])
        dc, dgc = _rms_bwd(c_ref[...], gc_ref[...], dmc_ref[...])
        da_ref[...] = da
        dc_ref[...] = dc
        same_head = (lax.broadcasted_iota(jnp.int32, (ATTN_W, ATTN_W), 0) // HEAD_DIM
                     == lax.broadcasted_iota(jnp.int32, (ATTN_W, ATTN_W), 1) // HEAD_DIM).astype(BF16)
        rest = da * attn_t
        total = jnp.zeros(rest.shape, F32)
        for _ in range(3):
            term = rest.astype(BF16)
            total = total + jnp.dot(term, same_head, preferred_element_type=F32)
            rest = rest - term.astype(F32)
        dl_ref[...] = total
        _accumulate(dga_ref, dga, first)
        _accumulate(dgc_ref, dgc, first)

    out, got = _call(
        body, [d_merged, d_merged, attn, conv_y, ga3, gc3], name="merge_bwd", grid=(t // tm,),
        in_specs=[pl.BlockSpec((tm, ATTN_W), lambda i: (i, 0)), pl.BlockSpec((tm, CONV_W), lambda i: (i, 1)),
                  row, row, _gain_spec(ga3, l), _gain_spec(gc3, l)],
        out_specs=[row, row, row, gsum, gsum],
        out_shape=[jax.ShapeDtypeStruct((t, ATTN_W), F32)] * 3 + [jax.ShapeDtypeStruct((1, ATTN_W), F32)] * 2,
        comm=comm)
    return (*out, got)


def _conv_bwd(proj, conv_w, l, d_conv_y):
    t = proj.shape[0]
    col0 = 3 * ATTN_W // LANES
    nj = CONV_W // LANES

    def body(u_ref, gb_ref, gc_ref, w_ref, dy_ref, d3_ref, dw_ref):
        u, gc, dy = u_ref[...], gc_ref[...], dy_ref[...]
        row = lax.broadcasted_iota(jnp.int32, u.shape, 0)
        down = lambda a: jnp.where(row == 0, 0.0, pltpu.roll(a, 1, 0))
        up = lambda a: jnp.where(row == t - 1, 0.0, pltpu.roll(a, t - 1, 0))
        w = w_ref[...]
        c = gc * u
        c_prev, c_next = down(c), up(c)
        d3_ref[1] = (dy * (w[0:1] * c_prev + w[1:2] * c + w[2:3] * c_next)).astype(BF16)
        dz = dy * gb_ref[...]
        dc = w[0:1] * up(dz) + w[1:2] * dz + w[2:3] * down(dz)
        d3_ref[0] = (dc * gc).astype(BF16)
        d3_ref[2] = (dc * u).astype(BF16)
        dw_ref[0:1, :] = jnp.sum(dz * c_prev, axis=0, keepdims=True)
        dw_ref[1:2, :] = jnp.sum(dz * c, axis=0, keepdims=True)
        dw_ref[2:3, :] = jnp.sum(dz * c_next, axis=0, keepdims=True)

    cols = lambda base: pl.BlockSpec((t, LANES), lambda j: (0, base + j))
    return pl.pallas_call(
        body, grid=(nj,),
        in_specs=[cols(col0), cols(col0 + nj), cols(col0 + 2 * nj),
                  pl.BlockSpec((None, None, 3, LANES), lambda j: (l, j, 0, 0)), cols(0)],
        out_specs=[pl.BlockSpec((3, t, LANES), lambda j: (0, 0, j)), pl.BlockSpec((None, 3, LANES), lambda j: (j, 0, 0))],
        out_shape=[jax.ShapeDtypeStruct((3, t, CONV_W), BF16), jax.ShapeDtypeStruct((nj, 3, LANES), F32)],
        name="conv_bwd")(proj, proj, proj, conv_w, d_conv_y)


def _own_shard_slab(w, l, place, dtype):
    _, rows, cols = w.shape
    tr = rows if rows <= 704 else 512
    assert rows % tr == 0

    def body(p_ref, w_ref, o_ref):
        del p_ref
        o_ref[...] = w_ref[...].astype(dtype)

    grid_spec = pltpu.PrefetchScalarGridSpec(
        num_scalar_prefetch=1, grid=(rows // tr,),
        in_specs=[pl.BlockSpec((None, tr, cols), lambda i, p: (l, i, 0))],
        out_specs=pl.BlockSpec((None, tr, cols), lambda i, p: (p[0], i, 0)))
    return pl.pallas_call(body, grid_spec=grid_spec, name="own_shard_slab",
                          out_shape=jax.ShapeDtypeStruct((N_CHIPS, rows, cols), dtype))(place, w)


def _own_conv_slab(w, place):
    depth = w.shape[0]

    def body(p_ref, w_ref, o_ref):
        del p_ref
        o_ref[...] = w_ref[...]

    grid_spec = pltpu.PrefetchScalarGridSpec(
        num_scalar_prefetch=1, grid=(depth,),
        in_specs=[pl.BlockSpec((None, 3, LANES), lambda l, p: (l, 0, 0))],
        out_specs=pl.BlockSpec((None, None, 3, LANES), lambda l, p: (l, p[0], 0, 0)))
    return pl.pallas_call(body, grid_spec=grid_spec, name="own_conv_slab",
                          out_shape=jax.ShapeDtypeStruct((depth, N_CHIPS, 3, LANES), F32))(place, w)


def _add_halves(grad, got, place):
    s_n, rows, cols = grad.shape
    hr = rows // 2

    def body(p_ref, g_ref, r_ref, o_ref):
        del p_ref
        o_ref[...] = (g_ref[...] + r_ref[...]).astype(BF16)

    grid_spec = pltpu.PrefetchScalarGridSpec(
        num_scalar_prefetch=1, grid=(s_n,),
        in_specs=[pl.BlockSpec((None, hr, cols), lambda s, p: (s, p[1], 0)),
                  pl.BlockSpec((None, hr, cols), lambda s, p: (s, 0, 0))],
        out_specs=pl.BlockSpec((None, hr, cols), lambda s, p: (s, 0, 0)))
    return pl.pallas_call(body, grid_spec=grid_spec, out_shape=jax.ShapeDtypeStruct((s_n, hr, cols), BF16),
                          name="add_halves")(place, grad, got)


def _sum_partials(partial, got, place, acc, l):
    _, hr, cols = partial.shape

    def body(p_ref, mine_ref, got_ref, acc_ref, o_ref):
        del p_ref, acc_ref
        total = mine_ref[...].astype(F32)
        for k in range(3):
            total = total + got_ref[k].astype(F32)
        o_ref[...] = total

    grid_spec = pltpu.PrefetchScalarGridSpec(
        num_scalar_prefetch=1, grid=(1,),
        in_specs=[pl.BlockSpec((None, hr, cols), lambda i, p: (p[0], 0, 0)),
                  pl.BlockSpec((3, hr, cols), lambda i, p: (0, 0, 0)), ANY],
        out_specs=pl.BlockSpec((None, hr, cols), lambda i, p: (l, p[1], 0)))
    return pl.pallas_call(body, grid_spec=grid_spec, out_shape=jax.ShapeDtypeStruct(acc.shape, F32),
                          input_output_aliases={3: 0}, name="sum_partials")(place, partial, got, acc)


def _allreduce_small(vec, loss_row):
    rows = vec.shape[0]

    def body(v_ref, o_ref, slots, send_sems, recv_sems):
        x, y, c, _ = _place()
        me = 4 * x + 2 * y + c
        slots[me] = v_ref[...]
        copies = []
        for k in range(1, N_DEV):
            flip = lambda v, bit: 1 - v if bit else v
            peer = (flip(x, k & 4), flip(y, k & 2), flip(c, k & 1))
            copies.append(_remote(v_ref, slots.at[me], send_sems.at[k - 1], recv_sems.at[k - 1], peer))
        for cp in copies:
            cp.start()
        for k in range(1, N_DEV):
            flip = lambda v, bit: 1 - v if bit else v
            peer_id = 4 * flip(x, k & 4) + 2 * flip(y, k & 2) + flip(c, k & 1)
            _remote(v_ref, slots.at[peer_id], send_sems.at[k - 1], recv_sems.at[k - 1], (x, y, c)).wait_recv()
        for cp in copies:
            cp.wait_send()
        total = slots[0]
        for dev in range(1, N_DEV):
            total = total + slots[dev]
        o_ref[...] = total
        o_ref[loss_row:loss_row + 1, :] = jnp.broadcast_to(
            jnp.sum(total[loss_row:loss_row + 1, :], axis=-1, keepdims=True), (1, LANES))

    return pl.pallas_call(
        body, in_specs=[WHOLE_VMEM], out_specs=WHOLE_VMEM, out_shape=jax.ShapeDtypeStruct((rows, LANES), F32),
        scratch_shapes=[pltpu.VMEM((N_DEV, rows, LANES), F32), pltpu.SemaphoreType.DMA((N_DEV - 1,)),
                        pltpu.SemaphoreType.DMA((N_DEV - 1,))],
        name="allreduce_small")(vec)


def _adamw(w, g, m, v, *, tr, emit_grad=False):
    depth, rows, cols = w.shape
    assert rows % tr == 0
    c1 = float(np.float32(1.0 - ADAM_B1 ** ADAM_STEP))
    c2 = float(np.float32(1.0 - ADAM_B2 ** ADAM_STEP))

    def body(w_ref, g_ref, m_ref, v_ref, d_ref, mo_ref, vo_ref, *go_ref):
        g_t = g_ref[...]
        if emit_grad:
            go_ref[0][...] = g_t
        m_new = ADAM_B1 * m_ref[...] + (1.0 - ADAM_B1) * g_t
        v_new = ADAM_B2 * v_ref[...] + (1.0 - ADAM_B2) * (g_t * g_t)
        mo_ref[...] = m_new
        vo_ref[...] = v_new
        d_ref[...] = -ADAM_LR * ((m_new / c1) / (jnp.sqrt(v_new / c2) + ADAM_EPS) + ADAM_WD * w_ref[...])

    blk = pl.BlockSpec((None, tr, cols), lambda l, i: (l, i, 0))
    return pl.pallas_call(
        body, grid=(depth, rows // tr), in_specs=[blk] * 4, out_specs=[blk] * (4 if emit_grad else 3),
        out_shape=[jax.ShapeDtypeStruct(w.shape, F32)] * (4 if emit_grad else 3), name="adamw")(w, g, m, v)


def _local_step(x, positions, target, gains, exchange):
    t = x.shape[0]
    tm = 512
    inv_freq = ROPE_THETA ** (-jnp.arange(0, ROPE_DIM, 2, dtype=F32) / ROPE_DIM)
    lane = np.arange(LANES) % HEAD_DIM
    freq_row = jnp.where(lane < ROPE_DIM, inv_freq[lane % (ROPE_DIM // 2)], 0.0).astype(F32)[None, :]
    cos, sin = _rope_tables(positions.reshape(t, 1), freq_row)

    def hosted(tag, fn, *args, **kwargs):
        *out, got = fn(*args, comm=exchange.host(tag), **kwargs)
        if got is not None:
            exchange.hosted(tag, got)
        return out[0] if len(out) == 1 else out

    saved = []
    h1 = _norm_fwd(x, gains["pre_mix_norm"], 0, tm=tm)
    for l in range(DEPTH):
        proj = hosted(("fwd", l, "in_proj"), _in_proj, h1, exchange.weight("w_in", l), cos, sin, tm=tm)
        attn, lse = hosted(("fwd", l, "attn"), _attn_fwd, proj)
        conv_y = _conv_fwd(proj, exchange.weight("conv_w", l), l)
        merged = _merge_fwd(attn, conv_y, gains["attn_out_norm"], gains["conv_out_norm"], l, tm=tm)
        mix, x1, h2 = hosted(("fwd", l, "out_proj"), _mm_resnorm, merged, exchange.weight("w_out", l), x,
                             gains["post_mix_norm"], l, gains["pre_ffn_norm"], l, tm=tm, name="out_proj")
        g, u, act = hosted(("fwd", l, "gate_up"), _gate_up_swiglu, h2, exchange.weight("w_gate_up", l), tm=1024)
        nxt = (gains["pre_mix_norm"], l + 1) if l + 1 < DEPTH else (None, None)
        f, x2, h1_next = hosted(("fwd", l, "down"), _mm_resnorm, act, exchange.weight("w_down", l), x1,
                                gains["post_ffn_norm"], l, *nxt, tm=tm, name="down")
        saved.append(dict(x=x, h1=h1, proj=proj, attn=attn, lse=lse, conv_y=conv_y, merged=merged, mix=mix,
                          x1=x1, h2=h2, g=g, u=u, act=act, f=f))
        x, h1 = x2, h1_next

    dres, loss_lanes = _loss_fwd_bwd(x, target, tm=tm)

    g_gain = {k: [None] * DEPTH for k in gains}
    g_conv = [None] * DEPTH
    _, df, _, g_gain["post_ffn_norm"][DEPTH - 1], _ = _norm_bwd(
        dres, None, (saved[-1]["f"], gains["post_ffn_norm"], DEPTH - 1), tm=tm)
    for l in reversed(range(DEPTH)):
        sv = saved[l]
        w = {k: exchange.weight(k, l) for k in MATRIX_NAMES + ("conv_w",)}
        dg, du = _down_dx_swiglu_bwd(df, w["w_down"], sv["g"], sv["u"], tm=1024, tko=FFN // 2)
        g_down = _mm_tn(sv["act"], df, 1, tka=256, name="down_dw")
        dx1, dmix, g_gain["pre_ffn_norm"][l], g_gain["post_mix_norm"][l] = hosted(
            ("bwd", l, "gate_up_dx"), _gate_up_dx_norms, dg, du, w["w_gate_up"], dres,
            (sv["x1"], gains["pre_ffn_norm"], l), (sv["mix"], gains["post_mix_norm"], l), tm=tm)
        g_gate_up = _mm_tn(sv["h2"], dg, N_CHIPS // 2, tka=512, name="gate_up_dw",
                           into=lax.empty(w["w_gate_up"].shape, F32))
        g_gate_up = _mm_tn(sv["h2"], du, N_CHIPS // 2, tka=512, name="gate_up_dw", into=g_gate_up,
                           shard0=N_CHIPS // 2)
        exchange.grads(l, "ffn", dict(w_down=g_down.reshape(N_CHIPS, FFN // N_CHIPS, D_MODEL), w_gate_up=g_gate_up))
        d_merged = hosted(("bwd", l, "out_proj_dx"), _mm_nt, dmix, w["w_out"], tm=1024, tko=D_MODEL, name="out_proj_dx")
        g_out = _mm_tn(sv["merged"], dmix, 1, tka=512, name="out_proj_dw")
        d_attn, delta, d_conv_y, g_gain["attn_out_norm"][l], g_gain["conv_out_norm"][l] = hosted(
            ("bwd", l, "merge"), _merge_bwd,
            d_merged, sv["attn"], sv["conv_y"], gains["attn_out_norm"], gains["conv_out_norm"], l, tm=tm)
        d_attn3 = hosted(("bwd", l, "attn"), _attn_bwd, sv["proj"], cos, sin, d_attn, sv["lse"], delta)
        d_conv3, g_conv[l] = _conv_bwd(sv["proj"], w["conv_w"], l, d_conv_y)
        g_in = _in_proj_dw(sv["h1"], d_attn3, d_conv3, N_CHIPS, tka=512)
        exchange.grads(l, "mix", dict(w_out=g_out.reshape(N_CHIPS, D_MODEL // N_CHIPS, D_MODEL), w_in=g_in))
        if l > 0:
            dres, df, g_gain["pre_mix_norm"][l], g_gain["post_ffn_norm"][l - 1] = hosted(
                ("bwd", l, "in_proj_dx"), _in_proj_dx_norms, d_attn3, d_conv3, w["w_in"], dx1,
                (sv["x"], gains["pre_mix_norm"], l), (saved[l - 1]["f"], gains["post_ffn_norm"], l - 1), tm=tm)
        else:
            dh1 = hosted(("bwd", l, "in_proj_dx"), _in_proj_dx, d_attn3, d_conv3, w["w_in"], tm=1024)
            dres, _, g_gain["pre_mix_norm"][l], _ = hosted(
                ("bwd", l, "norm_low"), _norm_bwd, dx1, (dh1, sv["x"], gains["pre_mix_norm"], l), None, tm=tm)

    g_gain = {k: jnp.concatenate(v, axis=0) for k, v in g_gain.items()}
    return loss_lanes, dres, g_gain, jnp.stack(g_conv, axis=0)


class _Exchange:
    GATHER_HOSTS = {"in_proj": (("w_out", 0),), "attn": (("w_gate_up", 0),), "gate_up": (("w_down", 0), ("w_in", 1))}

    @staticmethod
    def _reduce_hosts(group, l):
        if group == "ffn":
            return "merge", "attn", l
        if l > 0:
            return "in_proj_dx", "gate_up_dx", l - 1
        return "in_proj_dx", "norm_low", l

    def __init__(self, params, place):
        self.place = place
        self.slabs = {k: [_own_shard_slab(params[k], l, place, BF16) for l in range(DEPTH)] for k in MATRIX_NAMES}
        self.gathered = {k: [None] * DEPTH for k in MATRIX_NAMES}
        self.gathered["w_in"][0], self.conv_w = _run_comm(
            _gather_comm([self.slabs["w_in"][0]], _own_conv_slab(params["conv_w"], place)), "gather_first")
        self.full = {k: lax.empty(params[k].shape, F32) for k in MATRIX_NAMES}
        self.pending = {}
        self.raw = {}

    def weight(self, name, l):
        if name == "conv_w":
            return self.conv_w
        g = self.gathered[name][l]
        return g.reshape(1, g.shape[0] * g.shape[1], g.shape[2]) if name in ("w_out", "w_down") else g

    def host(self, tag):
        phase, l, kernel = tag
        if phase == "fwd":
            carried = [(name, l + ahead) for name, ahead in self.GATHER_HOSTS.get(kernel, ()) if l + ahead < DEPTH]
            return _gather_comm([self.slabs[name][layer] for name, layer in carried]) if carried else None
        if tag in self.pending:
            stage, _, _, arrays = self.pending[tag]
            return _halves_comm(arrays) if stage == "halves" else _partials_comm(arrays)
        return None

    def hosted(self, tag, results):
        phase, l, kernel = tag
        if phase == "fwd":
            carried = [(name, l + ahead) for name, ahead in self.GATHER_HOSTS[kernel] if l + ahead < DEPTH]
            for (name, layer), slab in zip(carried, results):
                self.gathered[name][layer] = slab
            return
        stage, gl, group, arrays = self.pending.pop(tag)
        names = list(self.raw[(gl, group)])
        if stage == "partials":
            self._finish_reduction(gl, names, arrays, results)
            return
        partials = [_add_halves(self.raw[(gl, group)][k], r, self.place) for k, r in zip(names, results)]
        _, ici_kernel, ici_layer = self._reduce_hosts(group, gl)
        self.pending[("bwd", ici_layer, ici_kernel)] = ("partials", gl, group, partials)

    def grads(self, l, group, grads):
        self.raw[(l, group)] = grads
        self.pending[("bwd", l, self._reduce_hosts(group, l)[0])] = ("halves", l, group, [grads[k] for k in grads])

    def _finish_reduction(self, l, names, partials, others):
        for k, p, q in zip(names, partials, others):
            self.full[k] = _sum_partials(p, q, self.place, self.full[k], l)

    def reduced(self):
        assert not self.pending
        return dict(zip(MATRIX_NAMES, _run_comm(_share_comm([self.full[k] for k in MATRIX_NAMES]), "share_halves")))


def kernel(x, positions, pre_mix_norm, w_in, conv_w, attn_out_norm, conv_out_norm, w_out, post_mix_norm, pre_ffn_norm, w_gate_up, w_down, post_ffn_norm, loss_target, m_pre_mix_norm, m_w_in, m_conv_w, m_attn_out_norm, m_conv_out_norm, m_w_out, m_post_mix_norm, m_pre_ffn_norm, m_w_gate_up, m_w_down, m_post_ffn_norm, v_pre_mix_norm, v_w_in, v_conv_w, v_attn_out_norm, v_conv_out_norm, v_w_out, v_post_mix_norm, v_pre_ffn_norm, v_w_gate_up, v_w_down, v_post_ffn_norm):
    params = dict(pre_mix_norm=pre_mix_norm, w_in=w_in, conv_w=conv_w, attn_out_norm=attn_out_norm,
                  conv_out_norm=conv_out_norm, w_out=w_out, post_mix_norm=post_mix_norm, pre_ffn_norm=pre_ffn_norm,
                  w_gate_up=w_gate_up, w_down=w_down, post_ffn_norm=post_ffn_norm)
    mom1 = dict(pre_mix_norm=m_pre_mix_norm, w_in=m_w_in, conv_w=m_conv_w, attn_out_norm=m_attn_out_norm,
                conv_out_norm=m_conv_out_norm, w_out=m_w_out, post_mix_norm=m_post_mix_norm,
                pre_ffn_norm=m_pre_ffn_norm, w_gate_up=m_w_gate_up, w_down=m_w_down, post_ffn_norm=m_post_ffn_norm)
    mom2 = dict(pre_mix_norm=v_pre_mix_norm, w_in=v_w_in, conv_w=v_conv_w, attn_out_norm=v_attn_out_norm,
                conv_out_norm=v_conv_out_norm, w_out=v_w_out, post_mix_norm=v_post_mix_norm,
                pre_ffn_norm=v_pre_ffn_norm, w_gate_up=v_w_gate_up, w_down=v_w_down, post_ffn_norm=v_post_ffn_norm)
    xi, yi, ci = lax.axis_index("x"), lax.axis_index("y"), lax.axis_index("c")
    place = jnp.stack([2 * xi + yi, ci]).astype(jnp.int32)

    exchange = _Exchange(params, place)
    gains = {k: params[k][:, None, :] for k in GAIN_NAMES}
    loss_lanes, grad_x, g_gain, g_conv = _local_step(x[0], positions[0], loss_target[0], gains, exchange)
    grad = exchange.reduced()

    small = [g_gain[k].reshape(-1) for k in GAIN_NAMES] + [g_conv.reshape(-1), loss_lanes.reshape(-1)]
    sizes = [int(s.shape[0]) for s in small]
    flat = jnp.concatenate(small)
    loss_row = (sum(sizes) - LANES) // LANES
    rows = -(-flat.shape[0] // (8 * LANES)) * 8
    flat = jnp.pad(flat, (0, rows * LANES - flat.shape[0])).reshape(rows, LANES)
    total = _allreduce_small(flat, loss_row).reshape(-1)
    offsets = np.cumsum([0] + sizes)
    for i, k in enumerate(GAIN_NAMES):
        grad[k] = total[offsets[i]:offsets[i + 1]].reshape(params[k].shape)
    conv_all = total[offsets[6]:offsets[7]].reshape(DEPTH, N_CHIPS, 3, LANES)
    grad["conv_w"] = lax.dynamic_index_in_dim(conv_all, 2 * xi + yi, axis=1, keepdims=False)
    loss = total[offsets[7]]

    delta, new_m, new_v = {}, {}, {}
    for k in WEIGHT_ORDER:
        shape = params[k].shape
        if k in MATRIX_NAMES:
            tr = {1024: 512, 704: 352, 256: 256}[shape[1]]
            delta[k], new_m[k], new_v[k], grad[k] = _adamw(params[k], grad[k], mom1[k], mom2[k], tr=tr, emit_grad=True)
        else:
            as3 = (lambda a: a) if len(shape) == 3 else (lambda a: a[None])
            d, m, v = _adamw(as3(params[k]), as3(grad[k]), as3(mom1[k]), as3(mom2[k]), tr=as3(params[k]).shape[1])
            delta[k], new_m[k], new_v[k] = d.reshape(shape), m.reshape(shape), v.reshape(shape)

    return (loss, grad_x[None], *[grad[k] for k in WEIGHT_ORDER], *[delta[k] for k in WEIGHT_ORDER],
            *[new_m[k] for k in WEIGHT_ORDER], *[new_v[k] for k in WEIGHT_ORDER])
```

```python
import functools
from typing import Callable, NamedTuple

import numpy as np
import jax
import jax.numpy as jnp
from jax import lax
from jax.experimental import pallas as pl
from jax.experimental.pallas import tpu as pltpu

F32 = jnp.float32
BF16 = jnp.bfloat16
MESH = pl.DeviceIdType.MESH

D_MODEL = 1024
ATTN_W = 512
CONV_W = 512
HEAD_DIM = 64
ROPE_DIM = 16
ROPE_THETA = 500000.0
FFN = 2816
DEPTH = 4
RMS_EPS = 1e-6
NEG_INF = -1e30
N_CHIPS = 4
N_DEV = 8
LANES = 128
BF16_ROWS = 16
DILATIONS = (1, 4, 16)
BAND = 64
TQ = 128
WIN = TQ + 2 * BAND
SCALE = HEAD_DIM ** -0.5

ADAM_LR = 0.001
ADAM_B1 = 0.9
ADAM_B2 = 0.999
ADAM_EPS = 1e-08
ADAM_WD = 0.01
ADAM_STEP = 10

GAIN_NAMES = ("pre_mix_norm", "attn_out_norm", "conv_out_norm", "post_mix_norm", "pre_ffn_norm", "post_ffn_norm")
MATRIX_NAMES = ("w_in", "w_out", "w_gate_up", "w_down")
WEIGHT_ORDER = ("pre_mix_norm", "w_in", "conv_w", "attn_out_norm", "conv_out_norm", "w_out", "post_mix_norm",
                "pre_ffn_norm", "w_gate_up", "w_down", "post_ffn_norm")

ANY = pl.BlockSpec(memory_space=pl.ANY)
WHOLE_VMEM = pl.BlockSpec(memory_space=pltpu.VMEM)
LANE_CONTRACT = (((1,), (1,)), ((), ()))
ROW_CONTRACT = (((0,), (0,)), ((), ()))
CHUNK = 256


def _const_spec(block, index):
    return pl.BlockSpec(block, lambda *_: index)


def _gain_spec(g3, l):
    return _const_spec((None, 1, g3.shape[-1]), (l, 0, 0))


class _Comm(NamedTuple):
    ins: tuple
    inouts: tuple
    out_shapes: tuple
    n_sems: int
    start: Callable
    finish: Callable


def _place():
    x, y, c = lax.axis_index("x"), lax.axis_index("y"), lax.axis_index("c")
    other_chips = [(1 - x, y), (x, 1 - y), (1 - x, 1 - y)]
    return x, y, c, other_chips


def _remote(src, dst, send_sem, recv_sem, to):
    return pltpu.make_async_remote_copy(src_ref=src, dst_ref=dst, send_sem=send_sem, recv_sem=recv_sem,
                                        device_id=to, device_id_type=MESH)


def _call(body, operands, *, name, grid, in_specs, out_specs, out_shape, scratch_shapes=(), aliases=None,
          comm=None):
    in_specs, out_specs, out_shape = list(in_specs), list(out_specs), list(out_shape)
    scratch_shapes = list(scratch_shapes)
    aliases = dict(aliases or {})
    if comm is None:
        out = pl.pallas_call(body, grid=grid, in_specs=in_specs, out_specs=out_specs, out_shape=out_shape,
                             scratch_shapes=scratch_shapes, input_output_aliases=aliases, name=name)(*operands)
        return list(out), None
    n_in, n_out, n_scr = len(in_specs), len(out_shape), len(scratch_shapes)
    n_ci, n_cio, n_co = len(comm.ins), len(comm.inouts), len(comm.out_shapes)

    def hosted(*refs):
        refs = list(refs)
        ins, c_ins = refs[:n_in], refs[n_in:n_in + n_ci]
        base = n_in + n_ci + n_cio
        outs = refs[base:base + n_out]
        c_io = refs[base + n_out:base + n_out + n_cio]
        c_out = refs[base + n_out + n_cio:base + n_out + n_cio + n_co]
        scr = refs[base + n_out + n_cio + n_co:]
        send_sems, recv_sems = scr[n_scr], scr[n_scr + 1]
        if grid:
            first = functools.reduce(jnp.logical_and, [pl.program_id(a) == 0 for a in range(len(grid))])
            last = functools.reduce(jnp.logical_and, [pl.program_id(a) == grid[a] - 1 for a in range(len(grid))])
            pl.when(first)(lambda: comm.start(c_ins, c_io, c_out, send_sems, recv_sems))
            body(*ins, *outs, *scr[:n_scr])
            pl.when(last)(lambda: comm.finish(c_ins, c_io, c_out, send_sems, recv_sems))
        else:
            comm.start(c_ins, c_io, c_out, send_sems, recv_sems)
            body(*ins, *outs, *scr[:n_scr])
            comm.finish(c_ins, c_io, c_out, send_sems, recv_sems)

    res = pl.pallas_call(
        hosted, grid=grid, in_specs=in_specs + [ANY] * (n_ci + n_cio), out_specs=out_specs + [ANY] * (n_cio + n_co),
        out_shape=out_shape + [jax.ShapeDtypeStruct(a.shape, a.dtype) for a in comm.inouts] + list(comm.out_shapes),
        input_output_aliases={**aliases, **{n_in + n_ci + i: n_out + i for i in range(n_cio)}},
        scratch_shapes=scratch_shapes + [pltpu.SemaphoreType.DMA((comm.n_sems,))] * 2,
        name=name)(*operands, *comm.ins, *comm.inouts)
    return list(res[:n_out]), list(res[n_out:])


def _run_comm(comm, name):
    return _call(lambda: None, [], name=name, grid=(), in_specs=[], out_specs=[], out_shape=[], comm=comm)[1]


def _row_half(ref, lead, core, rows, align):
    hr = rows // 2
    return ref.at[(*lead, pl.ds(pl.multiple_of(core * hr, align), hr), slice(None))]


def _gather_comm(slabs, conv_slab=None):
    n = len(slabs)
    n_conv = 0 if conv_slab is None else 3

    def direct(ios, send, recv):
        x, y, c, chips = _place()
        copies = []
        for a in range(n):
            own = _row_half(ios[a], (2 * x + y,), c, slabs[a].shape[1], BF16_ROWS)
            copies += [_remote(own, own, send.at[a * 3 + j], recv.at[a * 3 + j], (*chip, c))
                       for j, chip in enumerate(chips)]
        if conv_slab is not None:
            own = ios[n].at[:, 2 * x + y]
            copies += [_remote(own, own, send.at[6 * n + j], recv.at[6 * n + j], (*chip, c))
                       for j, chip in enumerate(chips)]
        return copies

    def start(ins, ios, outs, send, recv):
        for cp in direct(ios, send, recv):
            cp.start()

    def finish(ins, ios, outs, send, recv):
        x, y, c, chips = _place()
        sibling = (x, y, 1 - c)
        passed = []
        for a in range(n):
            for j, chip in enumerate(chips):
                landed = _row_half(ios[a], (2 * chip[0] + chip[1],), c, slabs[a].shape[1], BF16_ROWS)
                _remote(landed, landed, send.at[a * 3 + j], recv.at[a * 3 + j], (*chip, c)).wait_recv()
                fwd = _remote(landed, landed, send.at[3 * n + a * 3 + j], recv.at[3 * n + a * 3 + j], sibling)
                fwd.start()
                passed.append(fwd)
        if conv_slab is not None:
            for j, chip in enumerate(chips):
                landed = ios[n].at[:, 2 * chip[0] + chip[1]]
                _remote(landed, landed, send.at[6 * n + j], recv.at[6 * n + j], (*chip, c)).wait_recv()
        for a in range(n):
            for j, chip in enumerate(chips):
                landed = _row_half(ios[a], (2 * chip[0] + chip[1],), 1 - c, slabs[a].shape[1], BF16_ROWS)
                _remote(landed, landed, send.at[3 * n + a * 3 + j], recv.at[3 * n + a * 3 + j], sibling).wait_recv()
        for cp in direct(ios, send, recv) + passed:
            cp.wait_send()

    inouts = tuple(slabs) + (() if conv_slab is None else (conv_slab,))
    return _Comm((), inouts, (), 6 * n + n_conv, start, finish)


def _halves_comm(grads):
    n = len(grads)

    def copies(ins, outs, send, recv):
        x, y, c, _ = _place()
        return [_remote(_row_half(ins[a], (slice(None),), 1 - c, grads[a].shape[1], 8), outs[a],
                        send.at[a], recv.at[a], (x, y, 1 - c)) for a in range(n)]

    def start(ins, ios, outs, send, recv):
        for cp in copies(ins, outs, send, recv):
            cp.start()

    def finish(ins, ios, outs, send, recv):
        for cp in copies(ins, outs, send, recv):
            cp.wait()

    out_shapes = tuple(jax.ShapeDtypeStruct((g.shape[0], g.shape[1] // 2, g.shape[2]), F32) for g in grads)
    return _Comm(tuple(grads), (), out_shapes, n, start, finish)


def _partials_comm(partials):
    n = len(partials)

    def copies(ins, outs, send, recv):
        x, y, c, chips = _place()
        return [_remote(ins[a].at[2 * chip[0] + chip[1]], outs[a].at[k], send.at[a * 3 + k], recv.at[a * 3 + k],
                        (*chip, c)) for a in range(n) for k, chip in enumerate(chips)]

    def start(ins, ios, outs, send, recv):
        for cp in copies(ins, outs, send, recv):
            cp.start()

    def finish(ins, ios, outs, send, recv):
        for cp in copies(ins, outs, send, recv):
            cp.wait()

    out_shapes = tuple(jax.ShapeDtypeStruct((3,) + p.shape[1:], BF16) for p in partials)
    return _Comm(tuple(partials), (), out_shapes, 3 * n, start, finish)


def _share_comm(grads):
    n = len(grads)

    def start(ins, ios, outs, send, recv):
        x, y, c, _ = _place()
        for a in range(n):
            mine = _row_half(ios[a], (slice(None),), c, grads[a].shape[1], 8)
            _remote(mine, mine, send.at[a], recv.at[a], (x, y, 1 - c)).start()

    def finish(ins, ios, outs, send, recv):
        x, y, c, _ = _place()
        for a in range(n):
            theirs = _row_half(ios[a], (slice(None),), 1 - c, grads[a].shape[1], 8)
            _remote(theirs, theirs, send.at[a], recv.at[a], (x, y, 1 - c)).wait()

    return _Comm((), tuple(grads), (), n, start, finish)


def _rms_fwd(x, g):
    r = lax.rsqrt(jnp.mean(x * x, axis=-1, keepdims=True) + RMS_EPS)
    return (x * r) * g


def _rms_bwd(x, g, dy):
    r = lax.rsqrt(jnp.mean(x * x, axis=-1, keepdims=True) + RMS_EPS)
    xh = x * r
    u = dy * g
    dx = r * (u - xh * jnp.mean(xh * u, axis=-1, keepdims=True))
    return dx, jnp.sum(dy * xh, axis=0, keepdims=True)


def _accumulate(ref, value, first):
    @pl.when(first)
    def _():
        ref[...] = value

    @pl.when(jnp.logical_not(first))
    def _():
        ref[...] += value


def _rope_coeffs(cos, sin):
    m = lax.broadcasted_iota(jnp.int32, cos.shape, 1) % HEAD_DIM
    a = jnp.where(m < ROPE_DIM, cos, 1.0)
    b = jnp.where(m < ROPE_DIM // 2, -sin, 0.0)
    c = jnp.where((m >= ROPE_DIM // 2) & (m < ROPE_DIM), sin, 0.0)
    return a, b, c


def _rope_apply(t, cos, sin):
    a, b, c = _rope_coeffs(cos, sin)
    n = t.shape[1]
    return a * t + b * pltpu.roll(t, n - ROPE_DIM // 2, 1) + c * pltpu.roll(t, ROPE_DIM // 2, 1)


def _rope_transpose(dt, cos, sin):
    a, b, c = _rope_coeffs(cos, sin)
    n = dt.shape[1]
    return a * dt + pltpu.roll(b * dt, ROPE_DIM // 2, 1) + pltpu.roll(c * dt, n - ROPE_DIM // 2, 1)


def _in_proj(h, w, cos, sin, *, tm, comm=None):
    t, k = h.shape
    s_n, _, n = w.shape
    assert t % tm == 0 and n % LANES == 0

    rotary_shards = -(-2 * ATTN_W // n)
    rotary = list(range(0, 2 * ATTN_W, LANES))
    per_shard = -(-len(rotary) // max(s_n - rotary_shards, 1))

    def body(h_ref, w_ref, cos_ref, sin_ref, o_ref):
        def shard(s):
            o_ref[:, s * n:(s + 1) * n] = jnp.dot(h_ref[...], w_ref[s], preferred_element_type=F32)

        def rope(c0):
            cols = slice(c0, c0 + LANES)
            o_ref[:, cols] = _rope_apply(o_ref[:, cols], cos_ref[...], sin_ref[...])

        for s in range(rotary_shards):
            shard(s)
        pending = list(rotary)
        for s in range(rotary_shards, s_n):
            for c0 in pending[:per_shard]:
                rope(c0)
            pending = pending[per_shard:]
            shard(s)
        for c0 in pending:
            rope(c0)

    lane_tile = pl.BlockSpec((tm, LANES), lambda i: (i, 0))
    out, got = _call(
        body, [h, w, cos, sin], name="in_proj", grid=(t // tm,),
        in_specs=[pl.BlockSpec((tm, k), lambda i: (i, 0)), _const_spec(w.shape, (0, 0, 0)), lane_tile, lane_tile],
        out_specs=[pl.BlockSpec((tm, s_n * n), lambda i: (i, 0))],
        out_shape=[jax.ShapeDtypeStruct((t, s_n * n), F32)], comm=comm)
    return out[0], got


def _mm_nt(a, w, *, tm, tko, name, comm=None):
    t, sn = a.shape
    s_n, ko, n = w.shape
    assert sn == s_n * n and t % tm == 0 and ko % tko == 0

    def body(a_ref, w_ref, o_ref):
        acc = lax.dot_general(a_ref[...], w_ref[...], (((1,), (1,)), ((), ())), preferred_element_type=F32)
        if s_n == 1:
            o_ref[...] = acc
        else:
            _accumulate(o_ref, acc, pl.program_id(2) == 0)

    out, got = _call(
        body, [a, w], name=name, grid=(t // tm, ko // tko, s_n),
        in_specs=[pl.BlockSpec((tm, n), lambda i, j, s: (i, s)),
                  pl.BlockSpec((None, tko, n), lambda i, j, s: (s, j, 0))],
        out_specs=[pl.BlockSpec((tm, tko), lambda i, j, s: (i, j))],
        out_shape=[jax.ShapeDtypeStruct((t, ko), F32)], comm=comm)
    return out[0], got


def _dx_through_norms(operands, in_specs, dx_rows, dres, pre, post, *, tm, name, comm=None):
    t, d_model = dres.shape
    (x, gx3, lx), (y, gy3, ly) = pre, post
    n_op = len(operands)
    row = pl.BlockSpec((tm, d_model), lambda i: (i, 0))
    gsum = _const_spec((1, d_model), (0, 0))

    def body(*refs):
        d_ref, x_ref, gx_ref, y_ref, gy_ref, dn_ref, dgx_ref, dy_ref, dgy_ref = refs[n_op:]
        first = pl.program_id(0) == 0
        dx, dgx = _rms_bwd(x_ref[...], gx_ref[...], dx_rows(refs[:n_op]))
        d_new = d_ref[...] + dx
        dn_ref[...] = d_new
        _accumulate(dgx_ref, dgx, first)
        dy, dgy = _rms_bwd(y_ref[...], gy_ref[...], d_new)
        dy_ref[...] = dy.astype(BF16)
        _accumulate(dgy_ref, dgy, first)

    out, got = _call(
        body, list(operands) + [dres, x, gx3, y, gy3], name=name, grid=(t // tm,),
        in_specs=list(in_specs) + [row, row, _gain_spec(gx3, lx), row, _gain_spec(gy3, ly)],
        out_specs=[row, gsum, row, gsum],
        out_shape=[jax.ShapeDtypeStruct((t, d_model), F32), jax.ShapeDtypeStruct((1, d_model), F32),
                   jax.ShapeDtypeStruct((t, d_model), BF16), jax.ShapeDtypeStruct((1, d_model), F32)],
        comm=comm)
    return out[0], out[2], out[1], out[3], got


def _gate_up_dx_norms(dg, du, w, dres, pre, post, *, tm, comm=None):
    s_n, ko, n = w.shape
    half = s_n // 2

    def dx_rows(refs):
        dg_ref, du_ref, w_ref = refs
        acc = jnp.zeros((tm, ko), F32)
        for s in range(half):
            cols = slice(s * n, (s + 1) * n)
            acc = acc + lax.dot_general(dg_ref[:, cols], w_ref[s], LANE_CONTRACT, preferred_element_type=F32)
            acc = acc + lax.dot_general(du_ref[:, cols], w_ref[half + s], LANE_CONTRACT, preferred_element_type=F32)
        return acc

    a_spec = pl.BlockSpec((tm, half * n), lambda i: (i, 0))
    return _dx_through_norms([dg, du, w], [a_spec, a_spec, _const_spec(w.shape, (0, 0, 0))], dx_rows, dres, pre, post,
                             tm=tm, name="gate_up_dx", comm=comm)


def _in_proj_dx_norms(d_attn3, d_conv3, w, dres, pre, post, *, tm, comm=None):
    s_n, ko, n = w.shape
    per = n // CHUNK
    pieces = d_attn3.shape[0]
    width = d_attn3.shape[2]

    def dx_rows(refs):
        a_ref, b_ref, w_ref = refs
        acc = jnp.zeros((tm, ko), F32)
        for c in range(s_n * per):
            src = a_ref if c // 2 < pieces else b_ref
            piece, c0 = (c // 2) % pieces, (c % 2) * CHUNK
            acc = acc + lax.dot_general(src[piece, :, c0:c0 + CHUNK], w_ref[c // per, :, (c % per) * CHUNK:(c % per + 1) * CHUNK],
                                        LANE_CONTRACT, preferred_element_type=F32)
        return acc

    stack = pl.BlockSpec((pieces, tm, width), lambda i: (0, i, 0))
    return _dx_through_norms([d_attn3, d_conv3, w], [stack, stack, _const_spec(w.shape, (0, 0, 0))], dx_rows, dres, pre,
                             post, tm=tm, name="in_proj_dx", comm=comm)


def _mm_tn(a, b, s_n, *, tka, name, into=None, shard0=0, comm=None):
    t, ka = a.shape
    n = b.shape[1] // s_n
    assert b.shape[0] == t and ka % tka == 0

    def body(a_ref, b_ref, *rest):
        rest[-1][...] = lax.dot_general(a_ref[...], b_ref[...], ROW_CONTRACT, preferred_element_type=F32)

    operands, in_specs, aliases = [a, b], [pl.BlockSpec((t, tka), lambda i, s: (0, i)),
                                           pl.BlockSpec((t, n), lambda i, s: (0, s))], {}
    out_shape = jax.ShapeDtypeStruct((s_n, ka, n), F32)
    if into is not None:
        operands, in_specs, aliases = operands + [into], in_specs + [ANY], {2: 0}
        out_shape = jax.ShapeDtypeStruct(into.shape, F32)
    out, got = _call(body, operands, name=name, grid=(ka // tka, s_n), in_specs=in_specs,
                     out_specs=[pl.BlockSpec((None, tka, n), lambda i, s: (shard0 + s, i, 0))],
                     out_shape=[out_shape], aliases=aliases, comm=comm)
    return out[0], got


def _gate_up_swiglu(h, w, *, tm, comm=None):
    t, k = h.shape
    s_n, _, n = w.shape
    half = s_n // 2

    def body(h_ref, wg_ref, wu_ref, g_ref, u_ref, a_ref):
        g = jnp.dot(h_ref[...], wg_ref[...], preferred_element_type=F32)
        u = jnp.dot(h_ref[...], wu_ref[...], preferred_element_type=F32)
        g_ref[...] = g.astype(BF16)
        u_ref[...] = u.astype(BF16)
        a_ref[...] = (g * jax.nn.sigmoid(g) * u).astype(BF16)

    col = pl.BlockSpec((tm, n), lambda i, j: (i, j))
    out, got = _call(
        body, [h, w, w], name="gate_up", grid=(t // tm, half),
        in_specs=[pl.BlockSpec((tm, k), lambda i, j: (i, 0)), pl.BlockSpec((None, k, n), lambda i, j: (j, 0, 0)),
                  pl.BlockSpec((None, k, n), lambda i, j: (half + j, 0, 0))],
        out_specs=[col, col, col],
        out_shape=[jax.ShapeDtypeStruct((t, half * n), BF16)] * 3, comm=comm)
    return out[0], out[1], out[2], got


def _down_dx_swiglu_bwd(df, w, g, u, *, tm, tko):
    t, k = df.shape
    _, ko, _ = w.shape
    assert t % tm == 0 and ko % tko == 0

    def body(df_ref, w_ref, g_ref, u_ref, dg_ref, du_ref):
        d = lax.dot_general(df_ref[...], w_ref[...], LANE_CONTRACT, preferred_element_type=F32)
        gg = g_ref[...].astype(F32)
        sig = jax.nn.sigmoid(gg)
        dg_ref[...] = (d * u_ref[...].astype(F32) * (sig * (1.0 + gg * (1.0 - sig)))).astype(BF16)
        du_ref[...] = (d * (gg * sig)).astype(BF16)

    col = pl.BlockSpec((tm, tko), lambda i, j: (i, j))
    return pl.pallas_call(
        body, grid=(t // tm, ko // tko),
        in_specs=[pl.BlockSpec((tm, k), lambda i, j: (i, 0)), pl.BlockSpec((None, tko, k), lambda i, j: (0, j, 0)),
                  col, col],
        out_specs=[col, col], out_shape=[jax.ShapeDtypeStruct((t, ko), BF16)] * 2, name="down_dx")(df, w, g, u)


def _in_proj_dx(d_attn3, d_conv3, w, *, tm, comm=None):
    _, t, _ = d_attn3.shape
    s_n, ko, n = w.shape
    half, per = s_n // 2, n // CHUNK
    assert t % tm == 0

    def body(*refs):
        a_refs, b_refs, wa_ref, wb_ref, o_ref = refs[:per], refs[per:2 * per], refs[2 * per], refs[2 * per + 1], refs[-1]
        acc = jnp.zeros(o_ref.shape, F32)
        for r in range(per):
            cols = slice(r * CHUNK, (r + 1) * CHUNK)
            acc = acc + lax.dot_general(a_refs[r][...], wa_ref[:, cols], LANE_CONTRACT, preferred_element_type=F32)
            acc = acc + lax.dot_general(b_refs[r][...], wb_ref[:, cols], LANE_CONTRACT, preferred_element_type=F32)
        _accumulate(o_ref, acc, pl.program_id(1) == 0)

    piece = lambda r: pl.BlockSpec((None, tm, CHUNK), lambda i, s: ((per * s + r) // 2, i, (per * s + r) % 2))
    out, got = _call(
        body, [d_attn3] * per + [d_conv3] * per + [w, w], name="in_proj_dx", grid=(t // tm, half),
        in_specs=[piece(r) for r in range(per)] * 2
        + [pl.BlockSpec((None, ko, n), lambda i, s: (s, 0, 0)), pl.BlockSpec((None, ko, n), lambda i, s: (half + s, 0, 0))],
        out_specs=[pl.BlockSpec((tm, ko), lambda i, s: (i, 0))],
        out_shape=[jax.ShapeDtypeStruct((t, ko), F32)], comm=comm)
    return out[0], got


def _in_proj_dw(h, d_attn3, d_conv3, s_n, *, tka):
    t, ka = h.shape
    half = s_n // 2
    n = 3 * d_attn3.shape[2] // half
    per = n // CHUNK
    assert ka % tka == 0

    def body(*refs):
        h_ref, o_ref = refs[0], refs[-1]
        for side in range(2):
            for r in range(per):
                o_ref[side, :, r * CHUNK:(r + 1) * CHUNK] = lax.dot_general(
                    h_ref[...], refs[1 + side * per + r][...], ROW_CONTRACT, preferred_element_type=F32)

    piece = lambda r: pl.BlockSpec((None, t, CHUNK), lambda i, s: ((per * s + r) // 2, 0, (per * s + r) % 2))
    out = pl.pallas_call(
        body, grid=(ka // tka, half),
        in_specs=[pl.BlockSpec((t, tka), lambda i, s: (0, i))] + [piece(r) for r in range(per)] * 2,
        out_specs=pl.BlockSpec((2, None, tka, n), lambda i, s: (0, s, i, 0)),
        out_shape=jax.ShapeDtypeStruct((2, half, ka, n), F32), name="in_proj_dw")(h, *[d_attn3] * per, *[d_conv3] * per)
    return out.reshape(s_n, ka, n)


def _rope_tables(positions_col, inv_freq_row):
    t = positions_col.shape[0]

    def body(pos_ref, f_ref, cos_ref, sin_ref):
        ang = pos_ref[...].astype(F32) * f_ref[...]
        cos_ref[...] = jnp.cos(ang)
        sin_ref[...] = jnp.sin(ang)

    return pl.pallas_call(
        body, out_shape=[jax.ShapeDtypeStruct((t, LANES), F32)] * 2, name="rope_tables")(positions_col, inv_freq_row)


def _norm_fwd(x, g3, l, *, tm):
    t, w = x.shape

    def body(x_ref, g_ref, h_ref):
        h_ref[...] = _rms_fwd(x_ref[...], g_ref[...]).astype(BF16)

    return pl.pallas_call(
        body, grid=(t // tm,),
        in_specs=[pl.BlockSpec((tm, w), lambda i: (i, 0)), _gain_spec(g3, l)],
        out_specs=pl.BlockSpec((tm, w), lambda i: (i, 0)),
        out_shape=jax.ShapeDtypeStruct((t, w), BF16), name="norm_fwd")(x, g3)


def _mm_resnorm(a, w, x, g_post3, l_post, g_next3, l_next, *, tm, name, comm=None):
    t, k = a.shape
    _, _, n = w.shape
    with_next = g_next3 is not None
    row = pl.BlockSpec((tm, n), lambda i: (i, 0))

    def body(a_ref, w_ref, x_ref, gp_ref, *rest):
        y = jnp.dot(a_ref[...], w_ref[...], preferred_element_type=F32)
        x_new = x_ref[...] + _rms_fwd(y, gp_ref[...])
        if with_next:
            gn_ref, y_ref, xo_ref, h_ref = rest
            h_ref[...] = _rms_fwd(x_new, gn_ref[...]).astype(BF16)
        else:
            y_ref, xo_ref = rest
        y_ref[...] = y
        xo_ref[...] = x_new

    ins = [a, w, x, g_post3] + ([g_next3] if with_next else [])
    in_specs = ([pl.BlockSpec((tm, k), lambda i: (i, 0)), _const_spec((None, k, n), (0, 0, 0)), row,
                 _gain_spec(g_post3, l_post)] + ([_gain_spec(g_next3, l_next)] if with_next else []))
    out_shape = [jax.ShapeDtypeStruct((t, n), F32)] * 2 + ([jax.ShapeDtypeStruct((t, n), BF16)] if with_next else [])
    out, got = _call(body, ins, name=name, grid=(t // tm,), in_specs=in_specs, out_specs=[row] * len(out_shape),
                     out_shape=out_shape, comm=comm)
    return out[0], out[1], (out[2] if with_next else None), got


def _conv_fwd(proj, conv_w, l):
    t = proj.shape[0]
    col0 = 3 * ATTN_W // LANES

    def body(u_ref, gb_ref, gc_ref, w_ref, y_ref):
        c = gc_ref[...] * u_ref[...]
        row = lax.broadcasted_iota(jnp.int32, c.shape, 0)
        c_prev = jnp.where(row == 0, 0.0, pltpu.roll(c, 1, 0))
        c_next = jnp.where(row == t - 1, 0.0, pltpu.roll(c, t - 1, 0))
        w = w_ref[...]
        y_ref[...] = gb_ref[...] * (w[0:1] * c_prev + w[1:2] * c + w[2:3] * c_next)

    nj = CONV_W // LANES
    cols = lambda base: pl.BlockSpec((t, LANES), lambda j: (0, base + j))
    return pl.pallas_call(
        body, grid=(nj,),
        in_specs=[cols(col0), cols(col0 + nj), cols(col0 + 2 * nj),
                  pl.BlockSpec((None, None, 3, LANES), lambda j: (l, j, 0, 0))],
        out_specs=pl.BlockSpec((t, LANES), lambda j: (0, j)),
        out_shape=jax.ShapeDtypeStruct((t, CONV_W), F32), name="conv_fwd")(proj, proj, proj, conv_w)


def _merge_fwd(attn, conv_y, ga3, gc3, l, *, tm):
    t = attn.shape[0]
    row = pl.BlockSpec((tm, ATTN_W), lambda i: (i, 0))

    def body(a_ref, c_ref, ga_ref, gc_ref, m_ref):
        m_ref[:, :ATTN_W] = _rms_fwd(a_ref[...], ga_ref[...]).astype(BF16)
        m_ref[:, ATTN_W:] = _rms_fwd(c_ref[...], gc_ref[...]).astype(BF16)

    return pl.pallas_call(
        body, grid=(t // tm,),
        in_specs=[row, row, _gain_spec(ga3, l), _gain_spec(gc3, l)],
        out_specs=pl.BlockSpec((tm, D_MODEL), lambda i: (i, 0)),
        out_shape=jax.ShapeDtypeStruct((t, D_MODEL), BF16), name="merge_fwd")(attn, conv_y, ga3, gc3)


def _loss_fwd_bwd(y, target, *, tm):
    t, w = y.shape
    row = pl.BlockSpec((tm, w), lambda i: (i, 0))

    def body(y_ref, t_ref, dy_ref, loss_ref):
        e = y_ref[...] - t_ref[...]
        dy_ref[...] = e * (1.0 / w)
        sq = jnp.sum(e * e, axis=0, keepdims=True) * (0.5 / w)
        part = sq[:, :LANES]
        for j in range(1, w // LANES):
            part = part + sq[:, j * LANES:(j + 1) * LANES]
        _accumulate(loss_ref, part, pl.program_id(0) == 0)

    return pl.pallas_call(
        body, grid=(t // tm,), in_specs=[row, row],
        out_specs=[row, _const_spec((1, LANES), (0, 0))],
        out_shape=[jax.ShapeDtypeStruct((t, w), F32), jax.ShapeDtypeStruct((1, LANES), F32)], name="loss")(y, target)


def _tile_rows(t, nt, lb, d):
    r = t // nt
    q0 = (t % nt) * TQ
    m0 = jnp.clip(q0 - BAND, 0, lb - WIN)
    if d == 1:
        return pl.ds(pl.multiple_of(q0, TQ), TQ), pl.ds(pl.multiple_of(m0, BAND), WIN), m0 - q0
    return pl.ds(r + d * q0, TQ, stride=d), pl.ds(r + d * m0, WIN, stride=d), m0 - q0


def _for_row_chunks(t, fn, chunk=512):
    def step(i, carry):
        fn(pl.ds(pl.multiple_of(i * chunk, chunk), chunk))
        return carry

    lax.fori_loop(0, t // chunk, step, 0)


WINDOW_OFFSETS = (-BAND, 0, -2 * BAND)


def _fill_band_bias(bias_ref):
    rel0 = (lax.broadcasted_iota(jnp.int32, (2 * TQ, WIN), 1)
            - lax.broadcasted_iota(jnp.int32, (2 * TQ, WIN), 0) % TQ)
    for j, off in enumerate(WINDOW_OFFSETS):
        rel = rel0 + off
        bias_ref[j] = jnp.where((rel >= -BAND) & (rel <= BAND), 0.0, NEG_INF)


def _fill_sequence_bias(bias_ref):
    rel = (lax.broadcasted_iota(jnp.int32, (2 * WIN, WIN), 1) - lax.broadcasted_iota(jnp.int32, (2 * WIN, WIN), 0) % WIN)
    bias_ref[...] = jnp.where((rel >= -BAND) & (rel <= BAND), 0.0, NEG_INF)


def _band_bias(bias_ref, off):
    return bias_ref[jnp.where(off == WINDOW_OFFSETS[0], 0, jnp.where(off == WINDOW_OFFSETS[1], 1, 2))]


def _stack_heads(a, first_head):
    return jnp.concatenate([jnp.where(first_head, a, 0.0), jnp.where(first_head, 0.0, a)], axis=0)


def _unstack_heads(a2, first_head):
    n = a2.shape[0] // 2
    return jnp.where(first_head, a2[:n], a2[n:])


def _attn_fwd(proj, comm=None):
    t = proj.shape[0]
    npair = ATTN_W // LANES

    def body(q_ref, k_ref, v_ref, o_ref, lse_ref, o1, o2, l0, l1, l2, m1, m2, bias, bias_seq):
        _fill_band_bias(bias)
        _fill_sequence_bias(bias_seq)
        outs, dens, maxs = (o_ref, o1, o2), (l0, l1, l2), (lse_ref, m1, m2)

        def softmax_tile(b, qrows, krows, n_q, band_bias):
            first_head = lax.broadcasted_iota(jnp.int32, (n_q, LANES), 1) < HEAD_DIM
            q2 = _stack_heads(q_ref[qrows, :] * SCALE, first_head).astype(BF16)
            kw = k_ref[krows, :].astype(BF16)
            vw = jnp.concatenate([v_ref[krows, :].astype(BF16), jnp.ones((WIN, LANES), BF16)], axis=1)
            s = lax.dot_general(q2, kw, LANE_CONTRACT, preferred_element_type=F32) + band_bias
            m = jnp.max(s, axis=-1, keepdims=True)
            pv = jnp.dot(jnp.exp(s - m).astype(BF16), vw, preferred_element_type=F32)
            outs[b][qrows, :] = _unstack_heads(pv[:, :LANES], first_head)
            dens[b][qrows, :] = _unstack_heads(pv[:, LANES:], first_head)
            maxs[b][qrows, :] = _unstack_heads(jnp.broadcast_to(m, (2 * n_q, LANES)), first_head)

        for b, d in enumerate(DILATIONS):
            lb = t // d
            if lb == WIN:
                def sequence(r, carry, b=b, d=d):
                    rows = pl.ds(r, WIN, stride=d)
                    softmax_tile(b, rows, rows, WIN, bias_seq[...])
                    return carry

                lax.fori_loop(0, d, sequence, 0, unroll=4)
                continue
            nt = lb // TQ

            def tile(ti, carry, b=b, d=d, lb=lb, nt=nt):
                qrows, krows, off = _tile_rows(ti, nt, lb, d)
                softmax_tile(b, qrows, krows, TQ, _band_bias(bias, off))
                return carry

            lax.fori_loop(0, d * nt, tile, 0, unroll=8)

        def finish(rows):
            ms = [m_b[rows, :] for m_b in maxs]
            m_all = jnp.maximum(jnp.maximum(ms[0], ms[1]), ms[2])
            ws = [jnp.exp(m_b - m_all) for m_b in ms]
            den = ws[0] * dens[0][rows, :] + ws[1] * dens[1][rows, :] + ws[2] * dens[2][rows, :]
            num = ws[0] * outs[0][rows, :] + ws[1] * outs[1][rows, :] + ws[2] * outs[2][rows, :]
            o_ref[rows, :] = num / den
            lse_ref[rows, :] = m_all + jnp.log(den)

        _for_row_chunks(t, finish, 256)

    cols = lambda base: pl.BlockSpec((t, LANES), lambda g: (0, base + g))
    out, got = _call(
        body, [proj, proj, proj], name="attn_fwd", grid=(npair,),
        in_specs=[cols(0), cols(npair), cols(2 * npair)],
        out_specs=[cols(0), cols(0)],
        out_shape=[jax.ShapeDtypeStruct((t, ATTN_W), F32)] * 2,
        scratch_shapes=[pltpu.VMEM((t, LANES), F32)] * 7 + [pltpu.VMEM((len(WINDOW_OFFSETS), 2 * TQ, WIN), F32),
                                                            pltpu.VMEM((2 * WIN, WIN), F32)],
        comm=comm)
    return out[0], out[1], got


def _attn_bwd(proj, cos, sin, d_attn, lse, delta, comm=None):
    t = proj.shape[0]
    npair = ATTN_W // LANES

    def body(q_ref, k_ref, v_ref, cos_ref, sin_ref, do_ref, l_ref, dl_ref, dqkv_ref,
             dq_acc, dk_acc, dv_acc, bias, bias_seq):
        _fill_band_bias(bias)
        _fill_sequence_bias(bias_seq)
        dq_acc[...] = jnp.zeros(dq_acc.shape, F32)
        dk_acc[...] = jnp.zeros(dk_acc.shape, F32)
        dv_acc[...] = jnp.zeros(dv_acc.shape, F32)
        def stack_column(a):
            return jnp.concatenate([a[:, 0:1], a[:, HEAD_DIM:HEAD_DIM + 1]], axis=0)

        def grad_tile(qrows, krows, n_q, band_bias):
            first_head = lax.broadcasted_iota(jnp.int32, (n_q, LANES), 1) < HEAD_DIM
            q2 = _stack_heads(q_ref[qrows, :] * SCALE, first_head).astype(BF16)
            do2 = _stack_heads(do_ref[qrows, :], first_head).astype(BF16)
            kw = k_ref[krows, :].astype(BF16)
            vw = v_ref[krows, :].astype(BF16)
            s = lax.dot_general(q2, kw, LANE_CONTRACT, preferred_element_type=F32) + band_bias
            p = jnp.exp(s - stack_column(l_ref[qrows, :]))
            dp = lax.dot_general(do2, vw, LANE_CONTRACT, preferred_element_type=F32)
            ds = (p * (dp - stack_column(dl_ref[qrows, :]))).astype(BF16)
            dq2 = jnp.dot(ds, kw, preferred_element_type=F32)
            dq_acc[qrows, :] += _unstack_heads(dq2, first_head) * SCALE
            dk_acc[krows, :] += lax.dot_general(ds, q2, ROW_CONTRACT, preferred_element_type=F32)
            dv_acc[krows, :] += lax.dot_general(p.astype(BF16), do2, ROW_CONTRACT, preferred_element_type=F32)

        for d in DILATIONS:
            lb = t // d
            if lb == WIN:
                def sequence(r, carry, d=d):
                    rows = pl.ds(r, WIN, stride=d)
                    grad_tile(rows, rows, WIN, bias_seq[...])
                    return carry

                lax.fori_loop(0, d, sequence, 0, unroll=4)
                continue
            nt = lb // TQ

            def tile(ti, carry, d=d, lb=lb, nt=nt):
                qrows, krows, off = _tile_rows(ti, nt, lb, d)
                grad_tile(qrows, krows, TQ, _band_bias(bias, off))
                return carry

            lax.fori_loop(0, d * nt, tile, 0, unroll=8)

        def finish(rows):
            dqkv_ref[0, rows, :] = _rope_transpose(dq_acc[rows, :], cos_ref[rows, :], sin_ref[rows, :]).astype(BF16)
            dqkv_ref[1, rows, :] = _rope_transpose(dk_acc[rows, :], cos_ref[rows, :], sin_ref[rows, :]).astype(BF16)
            dqkv_ref[2, rows, :] = dv_acc[rows, :].astype(BF16)

        _for_row_chunks(t, finish)

    cols = lambda base: pl.BlockSpec((t, LANES), lambda g: (0, base + g))
    out, got = _call(
        body, [proj, proj, proj, cos, sin, d_attn, lse, delta], name="attn_bwd", grid=(npair,),
        in_specs=[cols(0), cols(npair), cols(2 * npair), WHOLE_VMEM, WHOLE_VMEM, cols(0), cols(0), cols(0)],
        out_specs=[pl.BlockSpec((3, t, LANES), lambda g: (0, 0, g))],
        out_shape=[jax.ShapeDtypeStruct((3, t, ATTN_W), BF16)],
        scratch_shapes=[pltpu.VMEM((t, LANES), F32)] * 3 + [pltpu.VMEM((len(WINDOW_OFFSETS), 2 * TQ, WIN), F32),
                                                            pltpu.VMEM((2 * WIN, WIN), F32)],
        comm=comm)
    return out[0], got


def _norm_bwd(dres, pre, post, *, tm, comm=None):
    t, w = dres.shape
    row = pl.BlockSpec((tm, w), lambda i: (i, 0))
    gsum = _const_spec((1, w), (0, 0))
    ins, in_specs, out_shape, out_specs = [dres], [row], [], []
    if pre is not None:
        dh, x, g3, l = pre
        ins += [dh, x, g3]
        in_specs += [row, row, _gain_spec(g3, l)]
        out_shape += [jax.ShapeDtypeStruct((t, w), F32), jax.ShapeDtypeStruct((1, w), F32)]
        out_specs += [row, gsum]
    if post is not None:
        y, g3, l = post
        ins += [y, g3]
        in_specs += [row, _gain_spec(g3, l)]
        out_shape += [jax.ShapeDtypeStruct((t, w), BF16), jax.ShapeDtypeStruct((1, w), F32)]
        out_specs += [row, gsum]
    n_in = len(ins)

    def body(*refs):
        first = pl.program_id(0) == 0
        ins_r, outs_r = list(refs[:n_in]), list(refs[n_in:])
        d = ins_r.pop(0)[...]
        if pre is not None:
            dh_ref, x_ref, g_ref = ins_r[:3]
            ins_r = ins_r[3:]
            dx, dg = _rms_bwd(x_ref[...], g_ref[...], dh_ref[...])
            d = d + dx
            outs_r.pop(0)[...] = d
            _accumulate(outs_r.pop(0), dg, first)
        if post is not None:
            y_ref, g_ref = ins_r
            dy, dg = _rms_bwd(y_ref[...], g_ref[...], d)
            outs_r.pop(0)[...] = dy.astype(BF16)
            _accumulate(outs_r.pop(0), dg, first)

    out, got = _call(body, ins, name="norm_bwd", grid=(t // tm,), in_specs=in_specs, out_specs=out_specs,
                     out_shape=out_shape, comm=comm)
    d_new, dg_pre = (out.pop(0), out.pop(0)) if pre is not None else (None, None)
    dy, dg_post = (out.pop(0), out.pop(0)) if post is not None else (None, None)
    return d_new, dy, dg_pre, dg_post, got


def _merge_bwd(d_merged, attn, conv_y, ga3, gc3, l, *, tm, comm=None):
    t = attn.shape[0]
    row = pl.BlockSpec((tm, ATTN_W), lambda i: (i, 0))
    gsum = _const_spec((1, ATTN_W), (0, 0))

    def body(dma_ref, dmc_ref, a_ref, c_ref, ga_ref, gc_ref, da_ref, dl_ref, dc_ref, dga_ref, dgc_ref):
        first = pl.program_id(0) == 0
        attn_t = a_ref[...]
        da, dga = _rms_bwd(attn_t, ga_ref[...], dma_ref[...])
        dc, dgc = _rms_bwd(c_ref[...], gc_ref[...], dmc_ref[...])
        da_ref[...] = da
        dc_ref[...] = dc
        same_head = (lax.broadcasted_iota(jnp.int32, (ATTN_W, ATTN_W), 0) // HEAD_DIM
                     == lax.broadcasted_iota(jnp.int32, (ATTN_W, ATTN_W), 1) // HEAD_DIM).astype(BF16)
        rest = da * attn_t
        total = jnp.zeros(rest.shape, F32)
        for _ in range(3):
            term = rest.astype(BF16)
            total = total + jnp.dot(term, same_head, preferred_element_type=F32)
            rest = rest - term.astype(F32)
        dl_ref[...] = total
        _accumulate(dga_ref, dga, first)
        _accumulate(dgc_ref, dgc, first)

    out, got = _call(
        body, [d_merged, d_merged, attn, conv_y, ga3, gc3], name="merge_bwd", grid=(t // tm,),
        in_specs=[pl.BlockSpec((tm, ATTN_W), lambda i: (i, 0)), pl.BlockSpec((tm, CONV_W), lambda i: (i, 1)),
                  row, row, _gain_spec(ga3, l), _gain_spec(gc3, l)],
        out_specs=[row, row, row, gsum, gsum],
        out_shape=[jax.ShapeDtypeStruct((t, ATTN_W), F32)] * 3 + [jax.ShapeDtypeStruct((1, ATTN_W), F32)] * 2,
        comm=comm)
    return (*out, got)


def _conv_bwd(proj, conv_w, l, d_conv_y):
    t = proj.shape[0]
    col0 = 3 * ATTN_W // LANES
    nj = CONV_W // LANES

    def body(u_ref, gb_ref, gc_ref, w_ref, dy_ref, d3_ref, dw_ref):
        u, gc, dy = u_ref[...], gc_ref[...], dy_ref[...]
        row = lax.broadcasted_iota(jnp.int32, u.shape, 0)
        down = lambda a: jnp.where(row == 0, 0.0, pltpu.roll(a, 1, 0))
        up = lambda a: jnp.where(row == t - 1, 0.0, pltpu.roll(a, t - 1, 0))
        w = w_ref[...]
        c = gc * u
        c_prev, c_next = down(c), up(c)
        d3_ref[1] = (dy * (w[0:1] * c_prev + w[1:2] * c + w[2:3] * c_next)).astype(BF16)
        dz = dy * gb_ref[...]
        dc = w[0:1] * up(dz) + w[1:2] * dz + w[2:3] * down(dz)
        d3_ref[0] = (dc * gc).astype(BF16)
        d3_ref[2] = (dc * u).astype(BF16)
        dw_ref[0:1, :] = jnp.sum(dz * c_prev, axis=0, keepdims=True)
        dw_ref[1:2, :] = jnp.sum(dz * c, axis=0, keepdims=True)
        dw_ref[2:3, :] = jnp.sum(dz * c_next, axis=0, keepdims=True)

    cols = lambda base: pl.BlockSpec((t, LANES), lambda j: (0, base + j))
    return pl.pallas_call(
        body, grid=(nj,),
        in_specs=[cols(col0), cols(col0 + nj), cols(col0 + 2 * nj),
                  pl.BlockSpec((None, None, 3, LANES), lambda j: (l, j, 0, 0)), cols(0)],
        out_specs=[pl.BlockSpec((3, t, LANES), lambda j: (0, 0, j)), pl.BlockSpec((None, 3, LANES), lambda j: (j, 0, 0))],
        out_shape=[jax.ShapeDtypeStruct((3, t, CONV_W), BF16), jax.ShapeDtypeStruct((nj, 3, LANES), F32)],
        name="conv_bwd")(proj, proj, proj, conv_w, d_conv_y)


def _own_shard_slab(w, l, place, dtype):
    _, rows, cols = w.shape
    tr = rows if rows <= 704 else 512
    assert rows % tr == 0

    def body(p_ref, w_ref, o_ref):
        del p_ref
        o_ref[...] = w_ref[...].astype(dtype)

    grid_spec = pltpu.PrefetchScalarGridSpec(
        num_scalar_prefetch=1, grid=(rows // tr,),
        in_specs=[pl.BlockSpec((None, tr, cols), lambda i, p: (l, i, 0))],
        out_specs=pl.BlockSpec((None, tr, cols), lambda i, p: (p[0], i, 0)))
    return pl.pallas_call(body, grid_spec=grid_spec, name="own_shard_slab",
                          out_shape=jax.ShapeDtypeStruct((N_CHIPS, rows, cols), dtype))(place, w)


def _own_conv_slab(w, place):
    depth = w.shape[0]

    def body(p_ref, w_ref, o_ref):
        del p_ref
        o_ref[...] = w_ref[...]

    grid_spec = pltpu.PrefetchScalarGridSpec(
        num_scalar_prefetch=1, grid=(depth,),
        in_specs=[pl.BlockSpec((None, 3, LANES), lambda l, p: (l, 0, 0))],
        out_specs=pl.BlockSpec((None, None, 3, LANES), lambda l, p: (l, p[0], 0, 0)))
    return pl.pallas_call(body, grid_spec=grid_spec, name="own_conv_slab",
                          out_shape=jax.ShapeDtypeStruct((depth, N_CHIPS, 3, LANES), F32))(place, w)


def _add_halves(grad, got, place):
    s_n, rows, cols = grad.shape
    hr = rows // 2

    def body(p_ref, g_ref, r_ref, o_ref):
        del p_ref
        o_ref[...] = (g_ref[...] + r_ref[...]).astype(BF16)

    grid_spec = pltpu.PrefetchScalarGridSpec(
        num_scalar_prefetch=1, grid=(s_n,),
        in_specs=[pl.BlockSpec((None, hr, cols), lambda s, p: (s, p[1], 0)),
                  pl.BlockSpec((None, hr, cols), lambda s, p: (s, 0, 0))],
        out_specs=pl.BlockSpec((None, hr, cols), lambda s, p: (s, 0, 0)))
    return pl.pallas_call(body, grid_spec=grid_spec, out_shape=jax.ShapeDtypeStruct((s_n, hr, cols), BF16),
                          name="add_halves")(place, grad, got)


def _sum_partials(partial, got, place, acc, l):
    _, hr, cols = partial.shape

    def body(p_ref, mine_ref, got_ref, acc_ref, o_ref):
        del p_ref, acc_ref
        total = mine_ref[...].astype(F32)
        for k in range(3):
            total = total + got_ref[k].astype(F32)
        o_ref[...] = total

    grid_spec = pltpu.PrefetchScalarGridSpec(
        num_scalar_prefetch=1, grid=(1,),
        in_specs=[pl.BlockSpec((None, hr, cols), lambda i, p: (p[0], 0, 0)),
                  pl.BlockSpec((3, hr, cols), lambda i, p: (0, 0, 0)), ANY],
        out_specs=pl.BlockSpec((None, hr, cols), lambda i, p: (l, p[1], 0)))
    return pl.pallas_call(body, grid_spec=grid_spec, out_shape=jax.ShapeDtypeStruct(acc.shape, F32),
                          input_output_aliases={3: 0}, name="sum_partials")(place, partial, got, acc)


def _allreduce_small(vec, loss_row):
    rows = vec.shape[0]

    def body(v_ref, o_ref, slots, send_sems, recv_sems):
        x, y, c, _ = _place()
        me = 4 * x + 2 * y + c
        slots[me] = v_ref[...]
        copies = []
        for k in range(1, N_DEV):
            flip = lambda v, bit: 1 - v if bit else v
            peer = (flip(x, k & 4), flip(y, k & 2), flip(c, k & 1))
            copies.append(_remote(v_ref, slots.at[me], send_sems.at[k - 1], recv_sems.at[k - 1], peer))
        for cp in copies:
            cp.start()
        for k in range(1, N_DEV):
            flip = lambda v, bit: 1 - v if bit else v
            peer_id = 4 * flip(x, k & 4) + 2 * flip(y, k & 2) + flip(c, k & 1)
            _remote(v_ref, slots.at[peer_id], send_sems.at[k - 1], recv_sems.at[k - 1], (x, y, c)).wait_recv()
        for cp in copies:
            cp.wait_send()
        total = slots[0]
        for dev in range(1, N_DEV):
            total = total + slots[dev]
        o_ref[...] = total
        o_ref[loss_row:loss_row + 1, :] = jnp.broadcast_to(
            jnp.sum(total[loss_row:loss_row + 1, :], axis=-1, keepdims=True), (1, LANES))

    return pl.pallas_call(
        body, in_specs=[WHOLE_VMEM], out_specs=WHOLE_VMEM, out_shape=jax.ShapeDtypeStruct((rows, LANES), F32),
        scratch_shapes=[pltpu.VMEM((N_DEV, rows, LANES), F32), pltpu.SemaphoreType.DMA((N_DEV - 1,)),
                        pltpu.SemaphoreType.DMA((N_DEV - 1,))],
        name="allreduce_small")(vec)


def _adamw(w, g, m, v, *, tr, emit_grad=False):
    depth, rows, cols = w.shape
    assert rows % tr == 0
    c1 = float(np.float32(1.0 - ADAM_B1 ** ADAM_STEP))
    c2 = float(np.float32(1.0 - ADAM_B2 ** ADAM_STEP))

    def body(w_ref, g_ref, m_ref, v_ref, d_ref, mo_ref, vo_ref, *go_ref):
        g_t = g_ref[...]
        if emit_grad:
            go_ref[0][...] = g_t
        m_new = ADAM_B1 * m_ref[...] + (1.0 - ADAM_B1) * g_t
        v_new = ADAM_B2 * v_ref[...] + (1.0 - ADAM_B2) * (g_t * g_t)
        mo_ref[...] = m_new
        vo_ref[...] = v_new
        d_ref[...] = -ADAM_LR * ((m_new / c1) / (jnp.sqrt(v_new / c2) + ADAM_EPS) + ADAM_WD * w_ref[...])

    blk = pl.BlockSpec((None, tr, cols), lambda l, i: (l, i, 0))
    return pl.pallas_call(
        body, grid=(depth, rows // tr), in_specs=[blk] * 4, out_specs=[blk] * (4 if emit_grad else 3),
        out_shape=[jax.ShapeDtypeStruct(w.shape, F32)] * (4 if emit_grad else 3), name="adamw")(w, g, m, v)


def _local_step(x, positions, target, gains, exchange):
    t = x.shape[0]
    tm = 512
    inv_freq = ROPE_THETA ** (-jnp.arange(0, ROPE_DIM, 2, dtype=F32) / ROPE_DIM)
    lane = np.arange(LANES) % HEAD_DIM
    freq_row = jnp.where(lane < ROPE_DIM, inv_freq[lane % (ROPE_DIM // 2)], 0.0).astype(F32)[None, :]
    cos, sin = _rope_tables(positions.reshape(t, 1), freq_row)

    def hosted(tag, fn, *args, **kwargs):
        *out, got = fn(*args, comm=exchange.host(tag), **kwargs)
        if got is not None:
            exchange.hosted(tag, got)
        return out[0] if len(out) == 1 else out

    saved = []
    h1 = _norm_fwd(x, gains["pre_mix_norm"], 0, tm=tm)
    for l in range(DEPTH):
        proj = hosted(("fwd", l, "in_proj"), _in_proj, h1, exchange.weight("w_in", l), cos, sin, tm=tm)
        attn, lse = hosted(("fwd", l, "attn"), _attn_fwd, proj)
        conv_y = _conv_fwd(proj, exchange.weight("conv_w", l), l)
        merged = _merge_fwd(attn, conv_y, gains["attn_out_norm"], gains["conv_out_norm"], l, tm=tm)
        mix, x1, h2 = hosted(("fwd", l, "out_proj"), _mm_resnorm, merged, exchange.weight("w_out", l), x,
                             gains["post_mix_norm"], l, gains["pre_ffn_norm"], l, tm=tm, name="out_proj")
        g, u, act = hosted(("fwd", l, "gate_up"), _gate_up_swiglu, h2, exchange.weight("w_gate_up", l), tm=1024)
        nxt = (gains["pre_mix_norm"], l + 1) if l + 1 < DEPTH else (None, None)
        f, x2, h1_next = hosted(("fwd", l, "down"), _mm_resnorm, act, exchange.weight("w_down", l), x1,
                                gains["post_ffn_norm"], l, *nxt, tm=tm, name="down")
        saved.append(dict(x=x, h1=h1, proj=proj, attn=attn, lse=lse, conv_y=conv_y, merged=merged, mix=mix,
                          x1=x1, h2=h2, g=g, u=u, act=act, f=f))
        x, h1 = x2, h1_next

    dres, loss_lanes = _loss_fwd_bwd(x, target, tm=tm)

    g_gain = {k: [None] * DEPTH for k in gains}
    g_conv = [None] * DEPTH
    _, df, _, g_gain["post_ffn_norm"][DEPTH - 1], _ = _norm_bwd(
        dres, None, (saved[-1]["f"], gains["post_ffn_norm"], DEPTH - 1), tm=tm)
    for l in reversed(range(DEPTH)):
        sv = saved[l]
        w = {k: exchange.weight(k, l) for k in MATRIX_NAMES + ("conv_w",)}
        dg, du = _down_dx_swiglu_bwd(df, w["w_down"], sv["g"], sv["u"], tm=1024, tko=FFN // 2)
        g_down, _ = _mm_tn(sv["act"], df, 1, tka=256, name="down_dw")
        exchange.grads(l, "down", dict(w_down=g_down.reshape(N_CHIPS, FFN // N_CHIPS, D_MODEL)))
        dx1, dmix, g_gain["pre_ffn_norm"][l], g_gain["post_mix_norm"][l] = hosted(
            ("bwd", l, "gate_up_dx"), _gate_up_dx_norms, dg, du, w["w_gate_up"], dres,
            (sv["x1"], gains["pre_ffn_norm"], l), (sv["mix"], gains["post_mix_norm"], l), tm=tm)
        g_gate_up = hosted(("bwd", l, "gate_up_dw"), _mm_tn, sv["h2"], dg, N_CHIPS // 2, tka=512, name="gate_up_dw",
                           into=lax.empty(w["w_gate_up"].shape, F32))
        g_gate_up, _ = _mm_tn(sv["h2"], du, N_CHIPS // 2, tka=512, name="gate_up_dw", into=g_gate_up,
                              shard0=N_CHIPS // 2)
        exchange.grads(l, "gate_up", dict(w_gate_up=g_gate_up))
        d_merged = hosted(("bwd", l, "out_proj_dx"), _mm_nt, dmix, w["w_out"], tm=1024, tko=D_MODEL, name="out_proj_dx")
        g_out, _ = _mm_tn(sv["merged"], dmix, 1, tka=512, name="out_proj_dw")
        d_attn, delta, d_conv_y, g_gain["attn_out_norm"][l], g_gain["conv_out_norm"][l] = hosted(
            ("bwd", l, "merge"), _merge_bwd,
            d_merged, sv["attn"], sv["conv_y"], gains["attn_out_norm"], gains["conv_out_norm"], l, tm=tm)
        d_attn3 = hosted(("bwd", l, "attn"), _attn_bwd, sv["proj"], cos, sin, d_attn, sv["lse"], delta)
        d_conv3, g_conv[l] = _conv_bwd(sv["proj"], w["conv_w"], l, d_conv_y)
        g_in = _in_proj_dw(sv["h1"], d_attn3, d_conv3, N_CHIPS, tka=512)
        exchange.grads(l, "mix", dict(w_out=g_out.reshape(N_CHIPS, D_MODEL // N_CHIPS, D_MODEL), w_in=g_in))
        if l > 0:
            dres, df, g_gain["pre_mix_norm"][l], g_gain["post_ffn_norm"][l - 1] = hosted(
                ("bwd", l, "in_proj_dx"), _in_proj_dx_norms, d_attn3, d_conv3, w["w_in"], dx1,
                (sv["x"], gains["pre_mix_norm"], l), (saved[l - 1]["f"], gains["post_ffn_norm"], l - 1), tm=tm)
        else:
            dh1 = hosted(("bwd", l, "in_proj_dx"), _in_proj_dx, d_attn3, d_conv3, w["w_in"], tm=1024)
            dres, _, g_gain["pre_mix_norm"][l], _ = hosted(
                ("bwd", l, "norm_low"), _norm_bwd, dx1, (dh1, sv["x"], gains["pre_mix_norm"], l), None, tm=tm)

    g_gain = {k: jnp.concatenate(v, axis=0) for k, v in g_gain.items()}
    return loss_lanes, dres, g_gain, jnp.stack(g_conv, axis=0)


class _Exchange:
    GATHER_HOSTS = {"in_proj": (("w_out", 0),), "attn": (("w_gate_up", 0),), "gate_up": (("w_down", 0), ("w_in", 1))}

    @staticmethod
    def _reduce_hosts(group, l):
        if group == "down":
            return "gate_up_dw", "attn", l
        if group == "gate_up":
            return "merge", "attn", l
        if l > 0:
            return "in_proj_dx", "gate_up_dx", l - 1
        return "in_proj_dx", "norm_low", l

    def __init__(self, params, place):
        self.place = place
        self.slabs = {k: [_own_shard_slab(params[k], l, place, BF16) for l in range(DEPTH)] for k in MATRIX_NAMES}
        self.gathered = {k: [None] * DEPTH for k in MATRIX_NAMES}
        self.gathered["w_in"][0], self.conv_w = _run_comm(
            _gather_comm([self.slabs["w_in"][0]], _own_conv_slab(params["conv_w"], place)), "gather_first")
        self.full = {k: lax.empty(params[k].shape, F32) for k in MATRIX_NAMES}
        self.pending = {}
        self.raw = {}

    def weight(self, name, l):
        if name == "conv_w":
            return self.conv_w
        g = self.gathered[name][l]
        return g.reshape(1, g.shape[0] * g.shape[1], g.shape[2]) if name in ("w_out", "w_down") else g

    def host(self, tag):
        phase, l, kernel = tag
        if phase == "fwd":
            carried = [(name, l + ahead) for name, ahead in self.GATHER_HOSTS.get(kernel, ()) if l + ahead < DEPTH]
            return _gather_comm([self.slabs[name][layer] for name, layer in carried]) if carried else None
        if tag in self.pending:
            entries = self.pending[tag]
            arrays = [a for entry in entries for a in entry[3]]
            stages = {entry[0] for entry in entries}
            assert len(stages) == 1
            return _halves_comm(arrays) if stages == {"halves"} else _partials_comm(arrays)
        return None

    def hosted(self, tag, results):
        phase, l, kernel = tag
        if phase == "fwd":
            carried = [(name, l + ahead) for name, ahead in self.GATHER_HOSTS[kernel] if l + ahead < DEPTH]
            for (name, layer), slab in zip(carried, results):
                self.gathered[name][layer] = slab
            return
        results = list(results)
        for stage, gl, group, arrays in self.pending.pop(tag):
            mine, results = results[:len(arrays)], results[len(arrays):]
            names = list(self.raw[(gl, group)])
            if stage == "partials":
                self._finish_reduction(gl, names, arrays, mine)
                continue
            partials = [_add_halves(self.raw[(gl, group)][k], r, self.place) for k, r in zip(names, mine)]
            _, ici_kernel, ici_layer = self._reduce_hosts(group, gl)
            self.pending.setdefault(("bwd", ici_layer, ici_kernel), []).append(("partials", gl, group, partials))

    def grads(self, l, group, grads):
        self.raw[(l, group)] = grads
        self.pending.setdefault(("bwd", l, self._reduce_hosts(group, l)[0]), []).append(
            ("halves", l, group, [grads[k] for k in grads]))

    def _finish_reduction(self, l, names, partials, others):
        for k, p, q in zip(names, partials, others):
            self.full[k] = _sum_partials(p, q, self.place, self.full[k], l)

    def reduced(self):
        assert not self.pending
        return dict(zip(MATRIX_NAMES, _run_comm(_share_comm([self.full[k] for k in MATRIX_NAMES]), "share_halves")))


def kernel(x, positions, pre_mix_norm, w_in, conv_w, attn_out_norm, conv_out_norm, w_out, post_mix_norm, pre_ffn_norm, w_gate_up, w_down, post_ffn_norm, loss_target, m_pre_mix_norm, m_w_in, m_conv_w, m_attn_out_norm, m_conv_out_norm, m_w_out, m_post_mix_norm, m_pre_ffn_norm, m_w_gate_up, m_w_down, m_post_ffn_norm, v_pre_mix_norm, v_w_in, v_conv_w, v_attn_out_norm, v_conv_out_norm, v_w_out, v_post_mix_norm, v_pre_ffn_norm, v_w_gate_up, v_w_down, v_post_ffn_norm):
    params = dict(pre_mix_norm=pre_mix_norm, w_in=w_in, conv_w=conv_w, attn_out_norm=attn_out_norm,
                  conv_out_norm=conv_out_norm, w_out=w_out, post_mix_norm=post_mix_norm, pre_ffn_norm=pre_ffn_norm,
                  w_gate_up=w_gate_up, w_down=w_down, post_ffn_norm=post_ffn_norm)
    mom1 = dict(pre_mix_norm=m_pre_mix_norm, w_in=m_w_in, conv_w=m_conv_w, attn_out_norm=m_attn_out_norm,
                conv_out_norm=m_conv_out_norm, w_out=m_w_out, post_mix_norm=m_post_mix_norm,
                pre_ffn_norm=m_pre_ffn_norm, w_gate_up=m_w_gate_up, w_down=m_w_down, post_ffn_norm=m_post_ffn_norm)
    mom2 = dict(pre_mix_norm=v_pre_mix_norm, w_in=v_w_in, conv_w=v_conv_w, attn_out_norm=v_attn_out_norm,
                conv_out_norm=v_conv_out_norm, w_out=v_w_out, post_mix_norm=v_post_mix_norm,
                pre_ffn_norm=v_pre_ffn_norm, w_gate_up=v_w_gate_up, w_down=v_w_down, post_ffn_norm=v_post_ffn_norm)
    xi, yi, ci = lax.axis_index("x"), lax.axis_index("y"), lax.axis_index("c")
    place = jnp.stack([2 * xi + yi, ci]).astype(jnp.int32)

    exchange = _Exchange(params, place)
    gains = {k: params[k][:, None, :] for k in GAIN_NAMES}
    loss_lanes, grad_x, g_gain, g_conv = _local_step(x[0], positions[0], loss_target[0], gains, exchange)
    grad = exchange.reduced()

    small = [g_gain[k].reshape(-1) for k in GAIN_NAMES] + [g_conv.reshape(-1), loss_lanes.reshape(-1)]
    sizes = [int(s.shape[0]) for s in small]
    flat = jnp.concatenate(small)
    loss_row = (sum(sizes) - LANES) // LANES
    rows = -(-flat.shape[0] // (8 * LANES)) * 8
    flat = jnp.pad(flat, (0, rows * LANES - flat.shape[0])).reshape(rows, LANES)
    total = _allreduce_small(flat, loss_row).reshape(-1)
    offsets = np.cumsum([0] + sizes)
    for i, k in enumerate(GAIN_NAMES):
        grad[k] = total[offsets[i]:offsets[i + 1]].reshape(params[k].shape)
    conv_all = total[offsets[6]:offsets[7]].reshape(DEPTH, N_CHIPS, 3, LANES)
    grad["conv_w"] = lax.dynamic_index_in_dim(conv_all, 2 * xi + yi, axis=1, keepdims=False)
    loss = total[offsets[7]]

    delta, new_m, new_v = {}, {}, {}
    for k in WEIGHT_ORDER:
        shape = params[k].shape
        if k in MATRIX_NAMES:
            tr = {1024: 512, 704: 352, 256: 256}[shape[1]]
            delta[k], new_m[k], new_v[k], grad[k] = _adamw(params[k], grad[k], mom1[k], mom2[k], tr=tr, emit_grad=True)
        else:
            as3 = (lambda a: a) if len(shape) == 3 else (lambda a: a[None])
            d, m, v = _adamw(as3(params[k]), as3(grad[k]), as3(mom1[k]), as3(mom2[k]), tr=as3(params[k]).shape[1])
            delta[k], new_m[k], new_v[k] = d.reshape(shape), m.reshape(shape), v.reshape(shape)

    return (loss, grad_x[None], *[grad[k] for k in WEIGHT_ORDER], *[delta[k] for k in WEIGHT_ORDER],
            *[new_m[k] for k in WEIGHT_ORDER], *[new_v[k] for k in WEIGHT_ORDER])
```

```python
import functools
from typing import Callable, NamedTuple

import numpy as np
import jax
import jax.numpy as jnp
from jax import lax
from jax.experimental import pallas as pl
from jax.experimental.pallas import tpu as pltpu

F32 = jnp.float32
BF16 = jnp.bfloat16
MESH = pl.DeviceIdType.MESH

D_MODEL = 1024
ATTN_W = 512
CONV_W = 512
HEAD_DIM = 64
ROPE_DIM = 16
ROPE_THETA = 500000.0
FFN = 2816
DEPTH = 4
RMS_EPS = 1e-6
NEG_INF = -1e30
N_CHIPS = 4
N_DEV = 8
LANES = 128
BF16_ROWS = 16
DILATIONS = (1, 4, 16)
BAND = 64
TQ = 128
WIN = TQ + 2 * BAND
SCALE = HEAD_DIM ** -0.5

ADAM_LR = 0.001
ADAM_B1 = 0.9
ADAM_B2 = 0.999
ADAM_EPS = 1e-08
ADAM_WD = 0.01
ADAM_STEP = 10

GAIN_NAMES = ("pre_mix_norm", "attn_out_norm", "conv_out_norm", "post_mix_norm", "pre_ffn_norm", "post_ffn_norm")
MATRIX_NAMES = ("w_in", "w_out", "w_gate_up", "w_down")
WEIGHT_ORDER = ("pre_mix_norm", "w_in", "conv_w", "attn_out_norm", "conv_out_norm", "w_out", "post_mix_norm",
                "pre_ffn_norm", "w_gate_up", "w_down", "post_ffn_norm")

ANY = pl.BlockSpec(memory_space=pl.ANY)
WHOLE_VMEM = pl.BlockSpec(memory_space=pltpu.VMEM)
LANE_CONTRACT = (((1,), (1,)), ((), ()))
ROW_CONTRACT = (((0,), (0,)), ((), ()))
CHUNK = 256


def _const_spec(block, index):
    return pl.BlockSpec(block, lambda *_: index)


def _gain_spec(g3, l):
    return _const_spec((None, 1, g3.shape[-1]), (l, 0, 0))


class _Comm(NamedTuple):
    ins: tuple
    inouts: tuple
    out_shapes: tuple
    n_sems: int
    start: Callable
    finish: Callable


def _place():
    x, y, c = lax.axis_index("x"), lax.axis_index("y"), lax.axis_index("c")
    other_chips = [(1 - x, y), (x, 1 - y), (1 - x, 1 - y)]
    return x, y, c, other_chips


def _remote(src, dst, send_sem, recv_sem, to):
    return pltpu.make_async_remote_copy(src_ref=src, dst_ref=dst, send_sem=send_sem, recv_sem=recv_sem,
                                        device_id=to, device_id_type=MESH)


def _call(body, operands, *, name, grid, in_specs, out_specs, out_shape, scratch_shapes=(), aliases=None,
          comm=None):
    in_specs, out_specs, out_shape = list(in_specs), list(out_specs), list(out_shape)
    scratch_shapes = list(scratch_shapes)
    aliases = dict(aliases or {})
    if comm is None:
        out = pl.pallas_call(body, grid=grid, in_specs=in_specs, out_specs=out_specs, out_shape=out_shape,
                             scratch_shapes=scratch_shapes, input_output_aliases=aliases, name=name)(*operands)
        return list(out), None
    n_in, n_out, n_scr = len(in_specs), len(out_shape), len(scratch_shapes)
    n_ci, n_cio, n_co = len(comm.ins), len(comm.inouts), len(comm.out_shapes)

    def hosted(*refs):
        refs = list(refs)
        ins, c_ins = refs[:n_in], refs[n_in:n_in + n_ci]
        base = n_in + n_ci + n_cio
        outs = refs[base:base + n_out]
        c_io = refs[base + n_out:base + n_out + n_cio]
        c_out = refs[base + n_out + n_cio:base + n_out + n_cio + n_co]
        scr = refs[base + n_out + n_cio + n_co:]
        send_sems, recv_sems = scr[n_scr], scr[n_scr + 1]
        if grid:
            first = functools.reduce(jnp.logical_and, [pl.program_id(a) == 0 for a in range(len(grid))])
            last = functools.reduce(jnp.logical_and, [pl.program_id(a) == grid[a] - 1 for a in range(len(grid))])
            pl.when(first)(lambda: comm.start(c_ins, c_io, c_out, send_sems, recv_sems))
            body(*ins, *outs, *scr[:n_scr])
            pl.when(last)(lambda: comm.finish(c_ins, c_io, c_out, send_sems, recv_sems))
        else:
            comm.start(c_ins, c_io, c_out, send_sems, recv_sems)
            body(*ins, *outs, *scr[:n_scr])
            comm.finish(c_ins, c_io, c_out, send_sems, recv_sems)

    res = pl.pallas_call(
        hosted, grid=grid, in_specs=in_specs + [ANY] * (n_ci + n_cio), out_specs=out_specs + [ANY] * (n_cio + n_co),
        out_shape=out_shape + [jax.ShapeDtypeStruct(a.shape, a.dtype) for a in comm.inouts] + list(comm.out_shapes),
        input_output_aliases={**aliases, **{n_in + n_ci + i: n_out + i for i in range(n_cio)}},
        scratch_shapes=scratch_shapes + [pltpu.SemaphoreType.DMA((comm.n_sems,))] * 2,
        name=name)(*operands, *comm.ins, *comm.inouts)
    return list(res[:n_out]), list(res[n_out:])


def _run_comm(comm, name):
    return _call(lambda: None, [], name=name, grid=(), in_specs=[], out_specs=[], out_shape=[], comm=comm)[1]


def _row_half(ref, lead, core, rows, align):
    hr = rows // 2
    return ref.at[(*lead, pl.ds(pl.multiple_of(core * hr, align), hr), slice(None))]


def _gather_comm(slabs, conv_slab=None):
    n = len(slabs)
    n_conv = 0 if conv_slab is None else 3

    def direct(ios, send, recv):
        x, y, c, chips = _place()
        copies = []
        for a in range(n):
            own = _row_half(ios[a], (2 * x + y,), c, slabs[a].shape[1], BF16_ROWS)
            copies += [_remote(own, own, send.at[a * 3 + j], recv.at[a * 3 + j], (*chip, c))
                       for j, chip in enumerate(chips)]
        if conv_slab is not None:
            own = ios[n].at[:, 2 * x + y]
            copies += [_remote(own, own, send.at[6 * n + j], recv.at[6 * n + j], (*chip, c))
                       for j, chip in enumerate(chips)]
        return copies

    def start(ins, ios, outs, send, recv):
        for cp in direct(ios, send, recv):
            cp.start()

    def finish(ins, ios, outs, send, recv):
        x, y, c, chips = _place()
        sibling = (x, y, 1 - c)
        passed = []
        for a in range(n):
            for j, chip in enumerate(chips):
                landed = _row_half(ios[a], (2 * chip[0] + chip[1],), c, slabs[a].shape[1], BF16_ROWS)
                _remote(landed, landed, send.at[a * 3 + j], recv.at[a * 3 + j], (*chip, c)).wait_recv()
                fwd = _remote(landed, landed, send.at[3 * n + a * 3 + j], recv.at[3 * n + a * 3 + j], sibling)
                fwd.start()
                passed.append(fwd)
        if conv_slab is not None:
            for j, chip in enumerate(chips):
                landed = ios[n].at[:, 2 * chip[0] + chip[1]]
                _remote(landed, landed, send.at[6 * n + j], recv.at[6 * n + j], (*chip, c)).wait_recv()
        for a in range(n):
            for j, chip in enumerate(chips):
                landed = _row_half(ios[a], (2 * chip[0] + chip[1],), 1 - c, slabs[a].shape[1], BF16_ROWS)
                _remote(landed, landed, send.at[3 * n + a * 3 + j], recv.at[3 * n + a * 3 + j], sibling).wait_recv()
        for cp in direct(ios, send, recv) + passed:
            cp.wait_send()

    inouts = tuple(slabs) + (() if conv_slab is None else (conv_slab,))
    return _Comm((), inouts, (), 6 * n + n_conv, start, finish)


def _halves_comm(grads):
    n = len(grads)

    def copies(ins, outs, send, recv):
        x, y, c, _ = _place()
        return [_remote(_row_half(ins[a], (slice(None),), 1 - c, grads[a].shape[1], 8), outs[a],
                        send.at[a], recv.at[a], (x, y, 1 - c)) for a in range(n)]

    def start(ins, ios, outs, send, recv):
        for cp in copies(ins, outs, send, recv):
            cp.start()

    def finish(ins, ios, outs, send, recv):
        for cp in copies(ins, outs, send, recv):
            cp.wait()

    out_shapes = tuple(jax.ShapeDtypeStruct((g.shape[0], g.shape[1] // 2, g.shape[2]), F32) for g in grads)
    return _Comm(tuple(grads), (), out_shapes, n, start, finish)


def _partials_comm(partials):
    n = len(partials)

    def copies(ins, outs, send, recv):
        x, y, c, chips = _place()
        return [_remote(ins[a].at[2 * chip[0] + chip[1]], outs[a].at[k], send.at[a * 3 + k], recv.at[a * 3 + k],
                        (*chip, c)) for a in range(n) for k, chip in enumerate(chips)]

    def start(ins, ios, outs, send, recv):
        for cp in copies(ins, outs, send, recv):
            cp.start()

    def finish(ins, ios, outs, send, recv):
        for cp in copies(ins, outs, send, recv):
            cp.wait()

    out_shapes = tuple(jax.ShapeDtypeStruct((3,) + p.shape[1:], BF16) for p in partials)
    return _Comm(tuple(partials), (), out_shapes, 3 * n, start, finish)


def _share_comm(grads):
    n = len(grads)

    def start(ins, ios, outs, send, recv):
        x, y, c, _ = _place()
        for a in range(n):
            mine = _row_half(ios[a], (slice(None),), c, grads[a].shape[1], 8)
            _remote(mine, mine, send.at[a], recv.at[a], (x, y, 1 - c)).start()

    def finish(ins, ios, outs, send, recv):
        x, y, c, _ = _place()
        for a in range(n):
            theirs = _row_half(ios[a], (slice(None),), 1 - c, grads[a].shape[1], 8)
            _remote(theirs, theirs, send.at[a], recv.at[a], (x, y, 1 - c)).wait()

    return _Comm((), tuple(grads), (), n, start, finish)


def _rms_fwd(x, g):
    r = lax.rsqrt(jnp.mean(x * x, axis=-1, keepdims=True) + RMS_EPS)
    return (x * r) * g


def _rms_bwd(x, g, dy):
    r = lax.rsqrt(jnp.mean(x * x, axis=-1, keepdims=True) + RMS_EPS)
    xh = x * r
    u = dy * g
    dx = r * (u - xh * jnp.mean(xh * u, axis=-1, keepdims=True))
    return dx, jnp.sum(dy * xh, axis=0, keepdims=True)


def _accumulate(ref, value, first):
    @pl.when(first)
    def _():
        ref[...] = value

    @pl.when(jnp.logical_not(first))
    def _():
        ref[...] += value


def _rope_coeffs(cos, sin):
    m = lax.broadcasted_iota(jnp.int32, cos.shape, 1) % HEAD_DIM
    a = jnp.where(m < ROPE_DIM, cos, 1.0)
    b = jnp.where(m < ROPE_DIM // 2, -sin, 0.0)
    c = jnp.where((m >= ROPE_DIM // 2) & (m < ROPE_DIM), sin, 0.0)
    return a, b, c


def _rope_apply(t, cos, sin):
    a, b, c = _rope_coeffs(cos, sin)
    n = t.shape[1]
    return a * t + b * pltpu.roll(t, n - ROPE_DIM // 2, 1) + c * pltpu.roll(t, ROPE_DIM // 2, 1)


def _rope_transpose(dt, cos, sin):
    a, b, c = _rope_coeffs(cos, sin)
    n = dt.shape[1]
    return a * dt + pltpu.roll(b * dt, ROPE_DIM // 2, 1) + pltpu.roll(c * dt, n - ROPE_DIM // 2, 1)


def _in_proj(h, w, cos, sin, *, tm, comm=None):
    t, k = h.shape
    s_n, _, n = w.shape
    assert t % tm == 0 and n % LANES == 0

    rotary_shards = -(-2 * ATTN_W // n)
    rotary = list(range(0, 2 * ATTN_W, LANES))
    per_shard = -(-len(rotary) // max(s_n - rotary_shards, 1))

    def body(h_ref, w_ref, cos_ref, sin_ref, o_ref):
        def shard(s):
            o_ref[:, s * n:(s + 1) * n] = jnp.dot(h_ref[...], w_ref[s], preferred_element_type=F32)

        def rope(c0):
            cols = slice(c0, c0 + LANES)
            o_ref[:, cols] = _rope_apply(o_ref[:, cols], cos_ref[...], sin_ref[...])

        for s in range(rotary_shards):
            shard(s)
        pending = list(rotary)
        for s in range(rotary_shards, s_n):
            for c0 in pending[:per_shard]:
                rope(c0)
            pending = pending[per_shard:]
            shard(s)
        for c0 in pending:
            rope(c0)

    lane_tile = pl.BlockSpec((tm, LANES), lambda i: (i, 0))
    out, got = _call(
        body, [h, w, cos, sin], name="in_proj", grid=(t // tm,),
        in_specs=[pl.BlockSpec((tm, k), lambda i: (i, 0)), _const_spec(w.shape, (0, 0, 0)), lane_tile, lane_tile],
        out_specs=[pl.BlockSpec((tm, s_n * n), lambda i: (i, 0))],
        out_shape=[jax.ShapeDtypeStruct((t, s_n * n), F32)], comm=comm)
    return out[0], got


def _mm_nt(a, w, *, tm, tko, name, comm=None):
    t, sn = a.shape
    s_n, ko, n = w.shape
    assert sn == s_n * n and t % tm == 0 and ko % tko == 0

    def body(a_ref, w_ref, o_ref):
        acc = lax.dot_general(a_ref[...], w_ref[...], (((1,), (1,)), ((), ())), preferred_element_type=F32)
        if s_n == 1:
            o_ref[...] = acc
        else:
            _accumulate(o_ref, acc, pl.program_id(2) == 0)

    out, got = _call(
        body, [a, w], name=name, grid=(t // tm, ko // tko, s_n),
        in_specs=[pl.BlockSpec((tm, n), lambda i, j, s: (i, s)),
                  pl.BlockSpec((None, tko, n), lambda i, j, s: (s, j, 0))],
        out_specs=[pl.BlockSpec((tm, tko), lambda i, j, s: (i, j))],
        out_shape=[jax.ShapeDtypeStruct((t, ko), F32)], comm=comm)
    return out[0], got


def _dx_through_norms(operands, in_specs, dx_rows, dres, pre, post, *, tm, name, comm=None):
    t, d_model = dres.shape
    (x, gx3, lx), (y, gy3, ly) = pre, post
    n_op = len(operands)
    row = pl.BlockSpec((tm, d_model), lambda i: (i, 0))
    gsum = _const_spec((1, d_model), (0, 0))

    def body(*refs):
        d_ref, x_ref, gx_ref, y_ref, gy_ref, dn_ref, dgx_ref, dy_ref, dgy_ref = refs[n_op:]
        first = pl.program_id(0) == 0
        dx, dgx = _rms_bwd(x_ref[...], gx_ref[...], dx_rows(refs[:n_op]))
        d_new = d_ref[...] + dx
        dn_ref[...] = d_new
        _accumulate(dgx_ref, dgx, first)
        dy, dgy = _rms_bwd(y_ref[...], gy_ref[...], d_new)
        dy_ref[...] = dy.astype(BF16)
        _accumulate(dgy_ref, dgy, first)

    out, got = _call(
        body, list(operands) + [dres, x, gx3, y, gy3], name=name, grid=(t // tm,),
        in_specs=list(in_specs) + [row, row, _gain_spec(gx3, lx), row, _gain_spec(gy3, ly)],
        out_specs=[row, gsum, row, gsum],
        out_shape=[jax.ShapeDtypeStruct((t, d_model), F32), jax.ShapeDtypeStruct((1, d_model), F32),
                   jax.ShapeDtypeStruct((t, d_model), BF16), jax.ShapeDtypeStruct((1, d_model), F32)],
        comm=comm)
    return out[0], out[2], out[1], out[3], got


def _gate_up_dx_norms(dg, du, w, dres, pre, post, *, tm, comm=None):
    s_n, ko, n = w.shape
    half = s_n // 2

    def dx_rows(refs):
        dg_ref, du_ref, w_ref = refs
        acc = jnp.zeros((tm, ko), F32)
        for s in range(half):
            cols = slice(s * n, (s + 1) * n)
            acc = acc + lax.dot_general(dg_ref[:, cols], w_ref[s], LANE_CONTRACT, preferred_element_type=F32)
            acc = acc + lax.dot_general(du_ref[:, cols], w_ref[half + s], LANE_CONTRACT, preferred_element_type=F32)
        return acc

    a_spec = pl.BlockSpec((tm, half * n), lambda i: (i, 0))
    return _dx_through_norms([dg, du, w], [a_spec, a_spec, _const_spec(w.shape, (0, 0, 0))], dx_rows, dres, pre, post,
                             tm=tm, name="gate_up_dx", comm=comm)


def _in_proj_dx_norms(d_attn3, d_conv3, w, dres, pre, post, *, tm, comm=None):
    s_n, ko, n = w.shape
    per = n // CHUNK
    pieces = d_attn3.shape[0]
    width = d_attn3.shape[2]

    def dx_rows(refs):
        a_ref, b_ref, w_ref = refs
        acc = jnp.zeros((tm, ko), F32)
        for c in range(s_n * per):
            src = a_ref if c // 2 < pieces else b_ref
            piece, c0 = (c // 2) % pieces, (c % 2) * CHUNK
            acc = acc + lax.dot_general(src[piece, :, c0:c0 + CHUNK], w_ref[c // per, :, (c % per) * CHUNK:(c % per + 1) * CHUNK],
                                        LANE_CONTRACT, preferred_element_type=F32)
        return acc

    stack = pl.BlockSpec((pieces, tm, width), lambda i: (0, i, 0))
    return _dx_through_norms([d_attn3, d_conv3, w], [stack, stack, _const_spec(w.shape, (0, 0, 0))], dx_rows, dres, pre,
                             post, tm=tm, name="in_proj_dx", comm=comm)


def _mm_tn(a, b, s_n, *, tka, name, into=None, shard0=0, comm=None):
    t, ka = a.shape
    n = b.shape[1] // s_n
    assert b.shape[0] == t and ka % tka == 0

    def body(a_ref, b_ref, *rest):
        rest[-1][...] = lax.dot_general(a_ref[...], b_ref[...], ROW_CONTRACT, preferred_element_type=F32)

    operands, in_specs, aliases = [a, b], [pl.BlockSpec((t, tka), lambda i, s: (0, i)),
                                           pl.BlockSpec((t, n), lambda i, s: (0, s))], {}
    out_shape = jax.ShapeDtypeStruct((s_n, ka, n), F32)
    if into is not None:
        operands, in_specs, aliases = operands + [into], in_specs + [ANY], {2: 0}
        out_shape = jax.ShapeDtypeStruct(into.shape, F32)
    out, got = _call(body, operands, name=name, grid=(ka // tka, s_n), in_specs=in_specs,
                     out_specs=[pl.BlockSpec((None, tka, n), lambda i, s: (shard0 + s, i, 0))],
                     out_shape=[out_shape], aliases=aliases, comm=comm)
    return out[0], got


def _gate_up_swiglu(h, w, *, tm, comm=None):
    t, k = h.shape
    s_n, _, n = w.shape
    half = s_n // 2

    def body(h_ref, wg_ref, wu_ref, g_ref, u_ref, a_ref):
        g = jnp.dot(h_ref[...], wg_ref[...], preferred_element_type=F32)
        u = jnp.dot(h_ref[...], wu_ref[...], preferred_element_type=F32)
        g_ref[...] = g.astype(BF16)
        u_ref[...] = u.astype(BF16)
        a_ref[...] = (g * jax.nn.sigmoid(g) * u).astype(BF16)

    col = pl.BlockSpec((tm, n), lambda i, j: (i, j))
    out, got = _call(
        body, [h, w, w], name="gate_up", grid=(t // tm, half),
        in_specs=[pl.BlockSpec((tm, k), lambda i, j: (i, 0)), pl.BlockSpec((None, k, n), lambda i, j: (j, 0, 0)),
                  pl.BlockSpec((None, k, n), lambda i, j: (half + j, 0, 0))],
        out_specs=[col, col, col],
        out_shape=[jax.ShapeDtypeStruct((t, half * n), BF16)] * 3, comm=comm)
    return out[0], out[1], out[2], got


def _down_dx_swiglu_bwd(df, w, g, u, *, tm, tko):
    t, k = df.shape
    _, ko, _ = w.shape
    assert t % tm == 0 and ko % tko == 0

    def body(df_ref, w_ref, g_ref, u_ref, dg_ref, du_ref):
        d = lax.dot_general(df_ref[...], w_ref[...], LANE_CONTRACT, preferred_element_type=F32)
        gg = g_ref[...].astype(F32)
        sig = jax.nn.sigmoid(gg)
        dg_ref[...] = (d * u_ref[...].astype(F32) * (sig * (1.0 + gg * (1.0 - sig)))).astype(BF16)
        du_ref[...] = (d * (gg * sig)).astype(BF16)

    col = pl.BlockSpec((tm, tko), lambda i, j: (i, j))
    return pl.pallas_call(
        body, grid=(t // tm, ko // tko),
        in_specs=[pl.BlockSpec((tm, k), lambda i, j: (i, 0)), pl.BlockSpec((None, tko, k), lambda i, j: (0, j, 0)),
                  col, col],
        out_specs=[col, col], out_shape=[jax.ShapeDtypeStruct((t, ko), BF16)] * 2, name="down_dx")(df, w, g, u)


def _in_proj_dx(d_attn3, d_conv3, w, *, tm, comm=None):
    _, t, _ = d_attn3.shape
    s_n, ko, n = w.shape
    half, per = s_n // 2, n // CHUNK
    assert t % tm == 0

    def body(*refs):
        a_refs, b_refs, wa_ref, wb_ref, o_ref = refs[:per], refs[per:2 * per], refs[2 * per], refs[2 * per + 1], refs[-1]
        acc = jnp.zeros(o_ref.shape, F32)
        for r in range(per):
            cols = slice(r * CHUNK, (r + 1) * CHUNK)
            acc = acc + lax.dot_general(a_refs[r][...], wa_ref[:, cols], LANE_CONTRACT, preferred_element_type=F32)
            acc = acc + lax.dot_general(b_refs[r][...], wb_ref[:, cols], LANE_CONTRACT, preferred_element_type=F32)
        _accumulate(o_ref, acc, pl.program_id(1) == 0)

    piece = lambda r: pl.BlockSpec((None, tm, CHUNK), lambda i, s: ((per * s + r) // 2, i, (per * s + r) % 2))
    out, got = _call(
        body, [d_attn3] * per + [d_conv3] * per + [w, w], name="in_proj_dx", grid=(t // tm, half),
        in_specs=[piece(r) for r in range(per)] * 2
        + [pl.BlockSpec((None, ko, n), lambda i, s: (s, 0, 0)), pl.BlockSpec((None, ko, n), lambda i, s: (half + s, 0, 0))],
        out_specs=[pl.BlockSpec((tm, ko), lambda i, s: (i, 0))],
        out_shape=[jax.ShapeDtypeStruct((t, ko), F32)], comm=comm)
    return out[0], got


def _in_proj_dw(h, d_attn3, d_conv3, s_n, *, tka):
    t, ka = h.shape
    half = s_n // 2
    n = 3 * d_attn3.shape[2] // half
    per = n // CHUNK
    assert ka % tka == 0

    def body(*refs):
        h_ref, o_ref = refs[0], refs[-1]
        for side in range(2):
            for r in range(per):
                o_ref[side, :, r * CHUNK:(r + 1) * CHUNK] = lax.dot_general(
                    h_ref[...], refs[1 + side * per + r][...], ROW_CONTRACT, preferred_element_type=F32)

    piece = lambda r: pl.BlockSpec((None, t, CHUNK), lambda i, s: ((per * s + r) // 2, 0, (per * s + r) % 2))
    out = pl.pallas_call(
        body, grid=(ka // tka, half),
        in_specs=[pl.BlockSpec((t, tka), lambda i, s: (0, i))] + [piece(r) for r in range(per)] * 2,
        out_specs=pl.BlockSpec((2, None, tka, n), lambda i, s: (0, s, i, 0)),
        out_shape=jax.ShapeDtypeStruct((2, half, ka, n), F32), name="in_proj_dw")(h, *[d_attn3] * per, *[d_conv3] * per)
    return out.reshape(s_n, ka, n)


def _rope_tables(positions_col, inv_freq_row):
    t = positions_col.shape[0]

    def body(pos_ref, f_ref, cos_ref, sin_ref):
        ang = pos_ref[...].astype(F32) * f_ref[...]
        cos_ref[...] = jnp.cos(ang)
        sin_ref[...] = jnp.sin(ang)

    return pl.pallas_call(
        body, out_shape=[jax.ShapeDtypeStruct((t, LANES), F32)] * 2, name="rope_tables")(positions_col, inv_freq_row)


def _norm_fwd(x, g3, l, *, tm):
    t, w = x.shape

    def body(x_ref, g_ref, h_ref):
        h_ref[...] = _rms_fwd(x_ref[...], g_ref[...]).astype(BF16)

    return pl.pallas_call(
        body, grid=(t // tm,),
        in_specs=[pl.BlockSpec((tm, w), lambda i: (i, 0)), _gain_spec(g3, l)],
        out_specs=pl.BlockSpec((tm, w), lambda i: (i, 0)),
        out_shape=jax.ShapeDtypeStruct((t, w), BF16), name="norm_fwd")(x, g3)


def _mm_resnorm(a, w, x, g_post3, l_post, g_next3, l_next, *, tm, name, comm=None):
    t, k = a.shape
    _, _, n = w.shape
    with_next = g_next3 is not None
    row = pl.BlockSpec((tm, n), lambda i: (i, 0))

    def body(a_ref, w_ref, x_ref, gp_ref, *rest):
        y = jnp.dot(a_ref[...], w_ref[...], preferred_element_type=F32)
        x_new = x_ref[...] + _rms_fwd(y, gp_ref[...])
        if with_next:
            gn_ref, y_ref, xo_ref, h_ref = rest
            h_ref[...] = _rms_fwd(x_new, gn_ref[...]).astype(BF16)
        else:
            y_ref, xo_ref = rest
        y_ref[...] = y
        xo_ref[...] = x_new

    ins = [a, w, x, g_post3] + ([g_next3] if with_next else [])
    in_specs = ([pl.BlockSpec((tm, k), lambda i: (i, 0)), _const_spec((None, k, n), (0, 0, 0)), row,
                 _gain_spec(g_post3, l_post)] + ([_gain_spec(g_next3, l_next)] if with_next else []))
    out_shape = [jax.ShapeDtypeStruct((t, n), F32)] * 2 + ([jax.ShapeDtypeStruct((t, n), BF16)] if with_next else [])
    out, got = _call(body, ins, name=name, grid=(t // tm,), in_specs=in_specs, out_specs=[row] * len(out_shape),
                     out_shape=out_shape, comm=comm)
    return out[0], out[1], (out[2] if with_next else None), got


def _conv_fwd(proj, conv_w, l):
    t = proj.shape[0]
    col0 = 3 * ATTN_W // LANES

    def body(u_ref, gb_ref, gc_ref, w_ref, y_ref):
        c = gc_ref[...] * u_ref[...]
        row = lax.broadcasted_iota(jnp.int32, c.shape, 0)
        c_prev = jnp.where(row == 0, 0.0, pltpu.roll(c, 1, 0))
        c_next = jnp.where(row == t - 1, 0.0, pltpu.roll(c, t - 1, 0))
        w = w_ref[...]
        y_ref[...] = gb_ref[...] * (w[0:1] * c_prev + w[1:2] * c + w[2:3] * c_next)

    nj = CONV_W // LANES
    cols = lambda base: pl.BlockSpec((t, LANES), lambda j: (0, base + j))
    return pl.pallas_call(
        body, grid=(nj,),
        in_specs=[cols(col0), cols(col0 + nj), cols(col0 + 2 * nj),
                  pl.BlockSpec((None, None, 3, LANES), lambda j: (l, j, 0, 0))],
        out_specs=pl.BlockSpec((t, LANES), lambda j: (0, j)),
        out_shape=jax.ShapeDtypeStruct((t, CONV_W), F32), name="conv_fwd")(proj, proj, proj, conv_w)


def _merge_fwd(attn, conv_y, ga3, gc3, l, *, tm):
    t = attn.shape[0]
    row = pl.BlockSpec((tm, ATTN_W), lambda i: (i, 0))

    def body(a_ref, c_ref, ga_ref, gc_ref, m_ref):
        m_ref[:, :ATTN_W] = _rms_fwd(a_ref[...], ga_ref[...]).astype(BF16)
        m_ref[:, ATTN_W:] = _rms_fwd(c_ref[...], gc_ref[...]).astype(BF16)

    return pl.pallas_call(
        body, grid=(t // tm,),
        in_specs=[row, row, _gain_spec(ga3, l), _gain_spec(gc3, l)],
        out_specs=pl.BlockSpec((tm, D_MODEL), lambda i: (i, 0)),
        out_shape=jax.ShapeDtypeStruct((t, D_MODEL), BF16), name="merge_fwd")(attn, conv_y, ga3, gc3)


def _loss_fwd_bwd(y, target, *, tm):
    t, w = y.shape
    row = pl.BlockSpec((tm, w), lambda i: (i, 0))

    def body(y_ref, t_ref, dy_ref, loss_ref):
        e = y_ref[...] - t_ref[...]
        dy_ref[...] = e * (1.0 / w)
        sq = jnp.sum(e * e, axis=0, keepdims=True) * (0.5 / w)
        part = sq[:, :LANES]
        for j in range(1, w // LANES):
            part = part + sq[:, j * LANES:(j + 1) * LANES]
        _accumulate(loss_ref, part, pl.program_id(0) == 0)

    return pl.pallas_call(
        body, grid=(t // tm,), in_specs=[row, row],
        out_specs=[row, _const_spec((1, LANES), (0, 0))],
        out_shape=[jax.ShapeDtypeStruct((t, w), F32), jax.ShapeDtypeStruct((1, LANES), F32)], name="loss")(y, target)


def _tile_rows(t, nt, lb, d):
    r = t // nt
    q0 = (t % nt) * TQ
    m0 = jnp.clip(q0 - BAND, 0, lb - WIN)
    if d == 1:
        return pl.ds(pl.multiple_of(q0, TQ), TQ), pl.ds(pl.multiple_of(m0, BAND), WIN), m0 - q0
    return pl.ds(r + d * q0, TQ, stride=d), pl.ds(r + d * m0, WIN, stride=d), m0 - q0


def _for_row_chunks(t, fn, chunk=512):
    def step(i, carry):
        fn(pl.ds(pl.multiple_of(i * chunk, chunk), chunk))
        return carry

    lax.fori_loop(0, t // chunk, step, 0)


WINDOW_OFFSETS = (-BAND, 0, -2 * BAND)


def _fill_band_bias(bias_ref):
    rel0 = (lax.broadcasted_iota(jnp.int32, (2 * TQ, WIN), 1)
            - lax.broadcasted_iota(jnp.int32, (2 * TQ, WIN), 0) % TQ)
    for j, off in enumerate(WINDOW_OFFSETS):
        rel = rel0 + off
        bias_ref[j] = jnp.where((rel >= -BAND) & (rel <= BAND), 0.0, NEG_INF)


def _fill_sequence_bias(bias_ref):
    rel = (lax.broadcasted_iota(jnp.int32, (2 * WIN, WIN), 1) - lax.broadcasted_iota(jnp.int32, (2 * WIN, WIN), 0) % WIN)
    bias_ref[...] = jnp.where((rel >= -BAND) & (rel <= BAND), 0.0, NEG_INF)


def _band_bias(bias_ref, off):
    return bias_ref[jnp.where(off == WINDOW_OFFSETS[0], 0, jnp.where(off == WINDOW_OFFSETS[1], 1, 2))]


def _stack_heads(a, first_head):
    return jnp.concatenate([jnp.where(first_head, a, 0.0), jnp.where(first_head, 0.0, a)], axis=0)


def _unstack_heads(a2, first_head):
    n = a2.shape[0] // 2
    return jnp.where(first_head, a2[:n], a2[n:])


def _attn_fwd(proj, comm=None):
    t = proj.shape[0]
    npair = ATTN_W // LANES

    def body(q_ref, k_ref, v_ref, o_ref, lse_ref, o1, o2, l0, l1, l2, m1, m2, bias, bias_seq):
        _fill_band_bias(bias)
        _fill_sequence_bias(bias_seq)
        outs, dens, maxs = (o_ref, o1, o2), (l0, l1, l2), (lse_ref, m1, m2)

        def softmax_tile(b, qrows, krows, n_q, band_bias):
            first_head = lax.broadcasted_iota(jnp.int32, (n_q, LANES), 1) < HEAD_DIM
            q2 = _stack_heads(q_ref[qrows, :] * SCALE, first_head).astype(BF16)
            kw = k_ref[krows, :].astype(BF16)
            vw = jnp.concatenate([v_ref[krows, :].astype(BF16), jnp.ones((WIN, LANES), BF16)], axis=1)
            s = lax.dot_general(q2, kw, LANE_CONTRACT, preferred_element_type=F32) + band_bias
            m = jnp.max(s, axis=-1, keepdims=True)
            pv = jnp.dot(jnp.exp(s - m).astype(BF16), vw, preferred_element_type=F32)
            outs[b][qrows, :] = _unstack_heads(pv[:, :LANES], first_head)
            dens[b][qrows, :] = _unstack_heads(pv[:, LANES:], first_head)
            maxs[b][qrows, :] = _unstack_heads(jnp.broadcast_to(m, (2 * n_q, LANES)), first_head)

        for b, d in enumerate(DILATIONS):
            lb = t // d
            if lb == WIN:
                def sequence(r, carry, b=b, d=d):
                    rows = pl.ds(r, WIN, stride=d)
                    softmax_tile(b, rows, rows, WIN, bias_seq[...])
                    return carry

                lax.fori_loop(0, d, sequence, 0, unroll=8)
                continue
            nt = lb // TQ

            def tile(ti, carry, b=b, d=d, lb=lb, nt=nt):
                qrows, krows, off = _tile_rows(ti, nt, lb, d)
                softmax_tile(b, qrows, krows, TQ, _band_bias(bias, off))
                return carry

            lax.fori_loop(0, d * nt, tile, 0, unroll=16)

        def finish(rows):
            ms = [m_b[rows, :] for m_b in maxs]
            m_all = jnp.maximum(jnp.maximum(ms[0], ms[1]), ms[2])
            ws = [jnp.exp(m_b - m_all) for m_b in ms]
            den = ws[0] * dens[0][rows, :] + ws[1] * dens[1][rows, :] + ws[2] * dens[2][rows, :]
            num = ws[0] * outs[0][rows, :] + ws[1] * outs[1][rows, :] + ws[2] * outs[2][rows, :]
            o_ref[rows, :] = num / den
            lse_ref[rows, :] = m_all + jnp.log(den)

        _for_row_chunks(t, finish, 256)

    cols = lambda base: pl.BlockSpec((t, LANES), lambda g: (0, base + g))
    out, got = _call(
        body, [proj, proj, proj], name="attn_fwd", grid=(npair,),
        in_specs=[cols(0), cols(npair), cols(2 * npair)],
        out_specs=[cols(0), cols(0)],
        out_shape=[jax.ShapeDtypeStruct((t, ATTN_W), F32)] * 2,
        scratch_shapes=[pltpu.VMEM((t, LANES), F32)] * 7 + [pltpu.VMEM((len(WINDOW_OFFSETS), 2 * TQ, WIN), F32),
                                                            pltpu.VMEM((2 * WIN, WIN), F32)],
        comm=comm)
    return out[0], out[1], got


def _attn_bwd(proj, cos, sin, d_attn, lse, delta, comm=None):
    t = proj.shape[0]
    npair = ATTN_W // LANES

    def body(q_ref, k_ref, v_ref, cos_ref, sin_ref, do_ref, l_ref, dl_ref, dqkv_ref,
             dq_acc, dk_acc, dv_acc, bias, bias_seq):
        _fill_band_bias(bias)
        _fill_sequence_bias(bias_seq)
        dq_acc[...] = jnp.zeros(dq_acc.shape, F32)
        dk_acc[...] = jnp.zeros(dk_acc.shape, F32)
        dv_acc[...] = jnp.zeros(dv_acc.shape, F32)
        def stack_column(a):
            return jnp.concatenate([a[:, 0:1], a[:, HEAD_DIM:HEAD_DIM + 1]], axis=0)

        def grad_tile(qrows, krows, n_q, band_bias):
            first_head = lax.broadcasted_iota(jnp.int32, (n_q, LANES), 1) < HEAD_DIM
            q2 = _stack_heads(q_ref[qrows, :] * SCALE, first_head).astype(BF16)
            do2 = _stack_heads(do_ref[qrows, :], first_head).astype(BF16)
            kw = k_ref[krows, :].astype(BF16)
            vw = v_ref[krows, :].astype(BF16)
            s = lax.dot_general(q2, kw, LANE_CONTRACT, preferred_element_type=F32) + band_bias
            p = jnp.exp(s - stack_column(l_ref[qrows, :]))
            dp = lax.dot_general(do2, vw, LANE_CONTRACT, preferred_element_type=F32)
            ds = (p * (dp - stack_column(dl_ref[qrows, :]))).astype(BF16)
            dq2 = jnp.dot(ds, kw, preferred_element_type=F32)
            dq_acc[qrows, :] += _unstack_heads(dq2, first_head) * SCALE
            dk_acc[krows, :] += lax.dot_general(ds, q2, ROW_CONTRACT, preferred_element_type=F32)
            dv_acc[krows, :] += lax.dot_general(p.astype(BF16), do2, ROW_CONTRACT, preferred_element_type=F32)

        for d in DILATIONS:
            lb = t // d
            if lb == WIN:
                def sequence(r, carry, d=d):
                    rows = pl.ds(r, WIN, stride=d)
                    grad_tile(rows, rows, WIN, bias_seq[...])
                    return carry

                lax.fori_loop(0, d, sequence, 0, unroll=4)
                continue
            nt = lb // TQ

            def tile(ti, carry, d=d, lb=lb, nt=nt):
                qrows, krows, off = _tile_rows(ti, nt, lb, d)
                grad_tile(qrows, krows, TQ, _band_bias(bias, off))
                return carry

            lax.fori_loop(0, d * nt, tile, 0, unroll=8)

        def finish(rows):
            dqkv_ref[0, rows, :] = _rope_transpose(dq_acc[rows, :], cos_ref[rows, :], sin_ref[rows, :]).astype(BF16)
            dqkv_ref[1, rows, :] = _rope_transpose(dk_acc[rows, :], cos_ref[rows, :], sin_ref[rows, :]).astype(BF16)
            dqkv_ref[2, rows, :] = dv_acc[rows, :].astype(BF16)

        _for_row_chunks(t, finish)

    cols = lambda base: pl.BlockSpec((t, LANES), lambda g: (0, base + g))
    out, got = _call(
        body, [proj, proj, proj, cos, sin, d_attn, lse, delta], name="attn_bwd", grid=(npair,),
        in_specs=[cols(0), cols(npair), cols(2 * npair), WHOLE_VMEM, WHOLE_VMEM, cols(0), cols(0), cols(0)],
        out_specs=[pl.BlockSpec((3, t, LANES), lambda g: (0, 0, g))],
        out_shape=[jax.ShapeDtypeStruct((3, t, ATTN_W), BF16)],
        scratch_shapes=[pltpu.VMEM((t, LANES), F32)] * 3 + [pltpu.VMEM((len(WINDOW_OFFSETS), 2 * TQ, WIN), F32),
                                                            pltpu.VMEM((2 * WIN, WIN), F32)],
        comm=comm)
    return out[0], got


def _norm_bwd(dres, pre, post, *, tm, comm=None):
    t, w = dres.shape
    row = pl.BlockSpec((tm, w), lambda i: (i, 0))
    gsum = _const_spec((1, w), (0, 0))
    ins, in_specs, out_shape, out_specs = [dres], [row], [], []
    if pre is not None:
        dh, x, g3, l = pre
        ins += [dh, x, g3]
        in_specs += [row, row, _gain_spec(g3, l)]
        out_shape += [jax.ShapeDtypeStruct((t, w), F32), jax.ShapeDtypeStruct((1, w), F32)]
        out_specs += [row, gsum]
    if post is not None:
        y, g3, l = post
        ins += [y, g3]
        in_specs += [row, _gain_spec(g3, l)]
        out_shape += [jax.ShapeDtypeStruct((t, w), BF16), jax.ShapeDtypeStruct((1, w), F32)]
        out_specs += [row, gsum]
    n_in = len(ins)

    def body(*refs):
        first = pl.program_id(0) == 0
        ins_r, outs_r = list(refs[:n_in]), list(refs[n_in:])
        d = ins_r.pop(0)[...]
        if pre is not None:
            dh_ref, x_ref, g_ref = ins_r[:3]
            ins_r = ins_r[3:]
            dx, dg = _rms_bwd(x_ref[...], g_ref[...], dh_ref[...])
            d = d + dx
            outs_r.pop(0)[...] = d
            _accumulate(outs_r.pop(0), dg, first)
        if post is not None:
            y_ref, g_ref = ins_r
            dy, dg = _rms_bwd(y_ref[...], g_ref[...], d)
            outs_r.pop(0)[...] = dy.astype(BF16)
            _accumulate(outs_r.pop(0), dg, first)

    out, got = _call(body, ins, name="norm_bwd", grid=(t // tm,), in_specs=in_specs, out_specs=out_specs,
                     out_shape=out_shape, comm=comm)
    d_new, dg_pre = (out.pop(0), out.pop(0)) if pre is not None else (None, None)
    dy, dg_post = (out.pop(0), out.pop(0)) if post is not None else (None, None)
    return d_new, dy, dg_pre, dg_post, got


def _merge_bwd(d_merged, attn, conv_y, ga3, gc3, l, *, tm, comm=None):
    t = attn.shape[0]
    row = pl.BlockSpec((tm, ATTN_W), lambda i: (i, 0))
    gsum = _const_spec((1, ATTN_W), (0, 0))

    def body(dma_ref, dmc_ref, a_ref, c_ref, ga_ref, gc_ref, da_ref, dl_ref, dc_ref, dga_ref, dgc_ref):
        first = pl.program_id(0) == 0
        attn_t = a_ref[...]
        da, dga = _rms_bwd(attn_t, ga_ref[...], dma_ref[...])
        dc, dgc = _rms_bwd(c_ref[...], gc_ref[...], dmc_ref[...])
        da_ref[...] = da
        dc_ref[...] = dc
        same_head = (lax.broadcasted_iota(jnp.int32, (ATTN_W, ATTN_W), 0) // HEAD_DIM
                     == lax.broadcasted_iota(jnp.int32, (ATTN_W, ATTN_W), 1) // HEAD_DIM).astype(BF16)
        rest = da * attn_t
        total = jnp.zeros(rest.shape, F32)
        for _ in range(3):
            term = rest.astype(BF16)
            total = total + jnp.dot(term, same_head, preferred_element_type=F32)
            rest = rest - term.astype(F32)
        dl_ref[...] = total
        _accumulate(dga_ref, dga, first)
        _accumulate(dgc_ref, dgc, first)

    out, got = _call(
        body, [d_merged, d_merged, attn, conv_y, ga3, gc3], name="merge_bwd", grid=(t // tm,),
        in_specs=[pl.BlockSpec((tm, ATTN_W), lambda i: (i, 0)), pl.BlockSpec((tm, CONV_W), lambda i: (i, 1)),
                  row, row, _gain_spec(ga3, l), _gain_spec(gc3, l)],
        out_specs=[row, row, row, gsum, gsum],
        out_shape=[jax.ShapeDtypeStruct((t, ATTN_W), F32)] * 3 + [jax.ShapeDtypeStruct((1, ATTN_W), F32)] * 2,
        comm=comm)
    return (*out, got)


def _conv_bwd(proj, conv_w, l, d_conv_y):
    t = proj.shape[0]
    col0 = 3 * ATTN_W // LANES
    nj = CONV_W // LANES

    def body(u_ref, gb_ref, gc_ref, w_ref, dy_ref, d3_ref, dw_ref):
        u, gc, dy = u_ref[...], gc_ref[...], dy_ref[...]
        row = lax.broadcasted_iota(jnp.int32, u.shape, 0)
        down = lambda a: jnp.where(row == 0, 0.0, pltpu.roll(a, 1, 0))
        up = lambda a: jnp.where(row == t - 1, 0.0, pltpu.roll(a, t - 1, 0))
        w = w_ref[...]
        c = gc * u
        c_prev, c_next = down(c), up(c)
        d3_ref[1] = (dy * (w[0:1] * c_prev + w[1:2] * c + w[2:3] * c_next)).astype(BF16)
        dz = dy * gb_ref[...]
        dc = w[0:1] * up(dz) + w[1:2] * dz + w[2:3] * down(dz)
        d3_ref[0] = (dc * gc).astype(BF16)
        d3_ref[2] = (dc * u).astype(BF16)
        dw_ref[0:1, :] = jnp.sum(dz * c_prev, axis=0, keepdims=True)
        dw_ref[1:2, :] = jnp.sum(dz * c, axis=0, keepdims=True)
        dw_ref[2:3, :] = jnp.sum(dz * c_next, axis=0, keepdims=True)

    cols = lambda base: pl.BlockSpec((t, LANES), lambda j: (0, base + j))
    return pl.pallas_call(
        body, grid=(nj,),
        in_specs=[cols(col0), cols(col0 + nj), cols(col0 + 2 * nj),
                  pl.BlockSpec((None, None, 3, LANES), lambda j: (l, j, 0, 0)), cols(0)],
        out_specs=[pl.BlockSpec((3, t, LANES), lambda j: (0, 0, j)), pl.BlockSpec((None, 3, LANES), lambda j: (j, 0, 0))],
        out_shape=[jax.ShapeDtypeStruct((3, t, CONV_W), BF16), jax.ShapeDtypeStruct((nj, 3, LANES), F32)],
        name="conv_bwd")(proj, proj, proj, conv_w, d_conv_y)


def _own_shard_slab(w, l, place, dtype):
    _, rows, cols = w.shape
    tr = rows if rows <= 704 else 512
    assert rows % tr == 0

    def body(p_ref, w_ref, o_ref):
        del p_ref
        o_ref[...] = w_ref[...].astype(dtype)

    grid_spec = pltpu.PrefetchScalarGridSpec(
        num_scalar_prefetch=1, grid=(rows // tr,),
        in_specs=[pl.BlockSpec((None, tr, cols), lambda i, p: (l, i, 0))],
        out_specs=pl.BlockSpec((None, tr, cols), lambda i, p: (p[0], i, 0)))
    return pl.pallas_call(body, grid_spec=grid_spec, name="own_shard_slab",
                          out_shape=jax.ShapeDtypeStruct((N_CHIPS, rows, cols), dtype))(place, w)


def _own_conv_slab(w, place):
    depth = w.shape[0]

    def body(p_ref, w_ref, o_ref):
        del p_ref
        o_ref[...] = w_ref[...]

    grid_spec = pltpu.PrefetchScalarGridSpec(
        num_scalar_prefetch=1, grid=(depth,),
        in_specs=[pl.BlockSpec((None, 3, LANES), lambda l, p: (l, 0, 0))],
        out_specs=pl.BlockSpec((None, None, 3, LANES), lambda l, p: (l, p[0], 0, 0)))
    return pl.pallas_call(body, grid_spec=grid_spec, name="own_conv_slab",
                          out_shape=jax.ShapeDtypeStruct((depth, N_CHIPS, 3, LANES), F32))(place, w)


def _add_halves(grad, got, place):
    s_n, rows, cols = grad.shape
    hr = rows // 2

    def body(p_ref, g_ref, r_ref, o_ref):
        del p_ref
        o_ref[...] = (g_ref[...] + r_ref[...]).astype(BF16)

    grid_spec = pltpu.PrefetchScalarGridSpec(
        num_scalar_prefetch=1, grid=(s_n,),
        in_specs=[pl.BlockSpec((None, hr, cols), lambda s, p: (s, p[1], 0)),
                  pl.BlockSpec((None, hr, cols), lambda s, p: (s, 0, 0))],
        out_specs=pl.BlockSpec((None, hr, cols), lambda s, p: (s, 0, 0)))
    return pl.pallas_call(body, grid_spec=grid_spec, out_shape=jax.ShapeDtypeStruct((s_n, hr, cols), BF16),
                          name="add_halves")(place, grad, got)


def _sum_partials(partial, got, place, acc, l):
    _, hr, cols = partial.shape

    def body(p_ref, mine_ref, got_ref, acc_ref, o_ref):
        del p_ref, acc_ref
        total = mine_ref[...].astype(F32)
        for k in range(3):
            total = total + got_ref[k].astype(F32)
        o_ref[...] = total

    grid_spec = pltpu.PrefetchScalarGridSpec(
        num_scalar_prefetch=1, grid=(1,),
        in_specs=[pl.BlockSpec((None, hr, cols), lambda i, p: (p[0], 0, 0)),
                  pl.BlockSpec((3, hr, cols), lambda i, p: (0, 0, 0)), ANY],
        out_specs=pl.BlockSpec((None, hr, cols), lambda i, p: (l, p[1], 0)))
    return pl.pallas_call(body, grid_spec=grid_spec, out_shape=jax.ShapeDtypeStruct(acc.shape, F32),
                          input_output_aliases={3: 0}, name="sum_partials")(place, partial, got, acc)


def _allreduce_small(vec, loss_row):
    rows = vec.shape[0]

    def body(v_ref, o_ref, slots, send_sems, recv_sems):
        x, y, c, _ = _place()
        me = 4 * x + 2 * y + c
        slots[me] = v_ref[...]
        copies = []
        for k in range(1, N_DEV):
            flip = lambda v, bit: 1 - v if bit else v
            peer = (flip(x, k & 4), flip(y, k & 2), flip(c, k & 1))
            copies.append(_remote(v_ref, slots.at[me], send_sems.at[k - 1], recv_sems.at[k - 1], peer))
        for cp in copies:
            cp.start()
        for k in range(1, N_DEV):
            flip = lambda v, bit: 1 - v if bit else v
            peer_id = 4 * flip(x, k & 4) + 2 * flip(y, k & 2) + flip(c, k & 1)
            _remote(v_ref, slots.at[peer_id], send_sems.at[k - 1], recv_sems.at[k - 1], (x, y, c)).wait_recv()
        for cp in copies:
            cp.wait_send()
        total = slots[0]
        for dev in range(1, N_DEV):
            total = total + slots[dev]
        o_ref[...] = total
        o_ref[loss_row:loss_row + 1, :] = jnp.broadcast_to(
            jnp.sum(total[loss_row:loss_row + 1, :], axis=-1, keepdims=True), (1, LANES))

    return pl.pallas_call(
        body, in_specs=[WHOLE_VMEM], out_specs=WHOLE_VMEM, out_shape=jax.ShapeDtypeStruct((rows, LANES), F32),
        scratch_shapes=[pltpu.VMEM((N_DEV, rows, LANES), F32), pltpu.SemaphoreType.DMA((N_DEV - 1,)),
                        pltpu.SemaphoreType.DMA((N_DEV - 1,))],
        name="allreduce_small")(vec)


def _adamw(w, g, m, v, *, tr, emit_grad=False):
    depth, rows, cols = w.shape
    assert rows % tr == 0
    c1 = float(np.float32(1.0 - ADAM_B1 ** ADAM_STEP))
    c2 = float(np.float32(1.0 - ADAM_B2 ** ADAM_STEP))

    def body(w_ref, g_ref, m_ref, v_ref, d_ref, mo_ref, vo_ref, *go_ref):
        g_t = g_ref[...]
        if emit_grad:
            go_ref[0][...] = g_t
        m_new = ADAM_B1 * m_ref[...] + (1.0 - ADAM_B1) * g_t
        v_new = ADAM_B2 * v_ref[...] + (1.0 - ADAM_B2) * (g_t * g_t)
        mo_ref[...] = m_new
        vo_ref[...] = v_new
        d_ref[...] = -ADAM_LR * ((m_new / c1) / (jnp.sqrt(v_new / c2) + ADAM_EPS) + ADAM_WD * w_ref[...])

    blk = pl.BlockSpec((None, tr, cols), lambda l, i: (l, i, 0))
    return pl.pallas_call(
        body, grid=(depth, rows // tr), in_specs=[blk] * 4, out_specs=[blk] * (4 if emit_grad else 3),
        out_shape=[jax.ShapeDtypeStruct(w.shape, F32)] * (4 if emit_grad else 3), name="adamw")(w, g, m, v)


def _local_step(x, positions, target, gains, exchange):
    t = x.shape[0]
    tm = 512
    inv_freq = ROPE_THETA ** (-jnp.arange(0, ROPE_DIM, 2, dtype=F32) / ROPE_DIM)
    lane = np.arange(LANES) % HEAD_DIM
    freq_row = jnp.where(lane < ROPE_DIM, inv_freq[lane % (ROPE_DIM // 2)], 0.0).astype(F32)[None, :]
    cos, sin = _rope_tables(positions.reshape(t, 1), freq_row)

    def hosted(tag, fn, *args, **kwargs):
        *out, got = fn(*args, comm=exchange.host(tag), **kwargs)
        if got is not None:
            exchange.hosted(tag, got)
        return out[0] if len(out) == 1 else out

    saved = []
    h1 = _norm_fwd(x, gains["pre_mix_norm"], 0, tm=tm)
    for l in range(DEPTH):
        proj = hosted(("fwd", l, "in_proj"), _in_proj, h1, exchange.weight("w_in", l), cos, sin, tm=tm)
        attn, lse = hosted(("fwd", l, "attn"), _attn_fwd, proj)
        conv_y = _conv_fwd(proj, exchange.weight("conv_w", l), l)
        merged = _merge_fwd(attn, conv_y, gains["attn_out_norm"], gains["conv_out_norm"], l, tm=tm)
        mix, x1, h2 = hosted(("fwd", l, "out_proj"), _mm_resnorm, merged, exchange.weight("w_out", l), x,
                             gains["post_mix_norm"], l, gains["pre_ffn_norm"], l, tm=tm, name="out_proj")
        g, u, act = hosted(("fwd", l, "gate_up"), _gate_up_swiglu, h2, exchange.weight("w_gate_up", l), tm=1024)
        nxt = (gains["pre_mix_norm"], l + 1) if l + 1 < DEPTH else (None, None)
        f, x2, h1_next = hosted(("fwd", l, "down"), _mm_resnorm, act, exchange.weight("w_down", l), x1,
                                gains["post_ffn_norm"], l, *nxt, tm=tm, name="down")
        saved.append(dict(x=x, h1=h1, proj=proj, attn=attn, lse=lse, conv_y=conv_y, merged=merged, mix=mix,
                          x1=x1, h2=h2, g=g, u=u, act=act, f=f))
        x, h1 = x2, h1_next

    dres, loss_lanes = _loss_fwd_bwd(x, target, tm=tm)

    g_gain = {k: [None] * DEPTH for k in gains}
    g_conv = [None] * DEPTH
    _, df, _, g_gain["post_ffn_norm"][DEPTH - 1], _ = _norm_bwd(
        dres, None, (saved[-1]["f"], gains["post_ffn_norm"], DEPTH - 1), tm=tm)
    for l in reversed(range(DEPTH)):
        sv = saved[l]
        w = {k: exchange.weight(k, l) for k in MATRIX_NAMES + ("conv_w",)}
        dg, du = _down_dx_swiglu_bwd(df, w["w_down"], sv["g"], sv["u"], tm=1024, tko=FFN // 2)
        g_down, _ = _mm_tn(sv["act"], df, 1, tka=256, name="down_dw")
        exchange.grads(l, "down", dict(w_down=g_down.reshape(N_CHIPS, FFN // N_CHIPS, D_MODEL)))
        dx1, dmix, g_gain["pre_ffn_norm"][l], g_gain["post_mix_norm"][l] = hosted(
            ("bwd", l, "gate_up_dx"), _gate_up_dx_norms, dg, du, w["w_gate_up"], dres,
            (sv["x1"], gains["pre_ffn_norm"], l), (sv["mix"], gains["post_mix_norm"], l), tm=tm)
        g_gate_up = hosted(("bwd", l, "gate_up_dw"), _mm_tn, sv["h2"], dg, N_CHIPS // 2, tka=512, name="gate_up_dw",
                           into=lax.empty(w["w_gate_up"].shape, F32))
        g_gate_up, _ = _mm_tn(sv["h2"], du, N_CHIPS // 2, tka=512, name="gate_up_dw", into=g_gate_up,
                              shard0=N_CHIPS // 2)
        exchange.grads(l, "gate_up", dict(w_gate_up=g_gate_up))
        d_merged = hosted(("bwd", l, "out_proj_dx"), _mm_nt, dmix, w["w_out"], tm=1024, tko=D_MODEL, name="out_proj_dx")
        g_out, _ = _mm_tn(sv["merged"], dmix, 1, tka=512, name="out_proj_dw")
        d_attn, delta, d_conv_y, g_gain["attn_out_norm"][l], g_gain["conv_out_norm"][l] = hosted(
            ("bwd", l, "merge"), _merge_bwd,
            d_merged, sv["attn"], sv["conv_y"], gains["attn_out_norm"], gains["conv_out_norm"], l, tm=tm)
        d_attn3 = hosted(("bwd", l, "attn"), _attn_bwd, sv["proj"], cos, sin, d_attn, sv["lse"], delta)
        d_conv3, g_conv[l] = _conv_bwd(sv["proj"], w["conv_w"], l, d_conv_y)
        g_in = _in_proj_dw(sv["h1"], d_attn3, d_conv3, N_CHIPS, tka=512)
        exchange.grads(l, "mix", dict(w_out=g_out.reshape(N_CHIPS, D_MODEL // N_CHIPS, D_MODEL), w_in=g_in))
        if l > 0:
            dres, df, g_gain["pre_mix_norm"][l], g_gain["post_ffn_norm"][l - 1] = hosted(
                ("bwd", l, "in_proj_dx"), _in_proj_dx_norms, d_attn3, d_conv3, w["w_in"], dx1,
                (sv["x"], gains["pre_mix_norm"], l), (saved[l - 1]["f"], gains["post_ffn_norm"], l - 1), tm=tm)
        else:
            dh1 = hosted(("bwd", l, "in_proj_dx"), _in_proj_dx, d_attn3, d_conv3, w["w_in"], tm=1024)
            dres, _, g_gain["pre_mix_norm"][l], _ = hosted(
                ("bwd", l, "norm_low"), _norm_bwd, dx1, (dh1, sv["x"], gains["pre_mix_norm"], l), None, tm=tm)

    g_gain = {k: jnp.concatenate(v, axis=0) for k, v in g_gain.items()}
    return loss_lanes, dres, g_gain, jnp.stack(g_conv, axis=0)


class _Exchange:
    GATHER_HOSTS = {"in_proj": (("w_out", 0),), "attn": (("w_gate_up", 0),), "gate_up": (("w_down", 0), ("w_in", 1))}

    @staticmethod
    def _reduce_hosts(group, l):
        if group == "down":
            return "gate_up_dw", "attn", l
        if group == "gate_up":
            return "merge", "attn", l
        if l > 0:
            return "in_proj_dx", "gate_up_dx", l - 1
        return "in_proj_dx", "norm_low", l

    def __init__(self, params, place):
        self.place = place
        self.slabs = {k: [_own_shard_slab(params[k], l, place, BF16) for l in range(DEPTH)] for k in MATRIX_NAMES}
        self.gathered = {k: [None] * DEPTH for k in MATRIX_NAMES}
        self.gathered["w_in"][0], self.conv_w = _run_comm(
            _gather_comm([self.slabs["w_in"][0]], _own_conv_slab(params["conv_w"], place)), "gather_first")
        self.full = {k: lax.empty(params[k].shape, F32) for k in MATRIX_NAMES}
        self.pending = {}
        self.raw = {}

    def weight(self, name, l):
        if name == "conv_w":
            return self.conv_w
        g = self.gathered[name][l]
        return g.reshape(1, g.shape[0] * g.shape[1], g.shape[2]) if name in ("w_out", "w_down") else g

    def host(self, tag):
        phase, l, kernel = tag
        if phase == "fwd":
            carried = [(name, l + ahead) for name, ahead in self.GATHER_HOSTS.get(kernel, ()) if l + ahead < DEPTH]
            return _gather_comm([self.slabs[name][layer] for name, layer in carried]) if carried else None
        if tag in self.pending:
            entries = self.pending[tag]
            arrays = [a for entry in entries for a in entry[3]]
            stages = {entry[0] for entry in entries}
            assert len(stages) == 1
            return _halves_comm(arrays) if stages == {"halves"} else _partials_comm(arrays)
        return None

    def hosted(self, tag, results):
        phase, l, kernel = tag
        if phase == "fwd":
            carried = [(name, l + ahead) for name, ahead in self.GATHER_HOSTS[kernel] if l + ahead < DEPTH]
            for (name, layer), slab in zip(carried, results):
                self.gathered[name][layer] = slab
            return
        results = list(results)
        for stage, gl, group, arrays in self.pending.pop(tag):
            mine, results = results[:len(arrays)], results[len(arrays):]
            names = list(self.raw[(gl, group)])
            if stage == "partials":
                self._finish_reduction(gl, names, arrays, mine)
                continue
            partials = [_add_halves(self.raw[(gl, group)][k], r, self.place) for k, r in zip(names, mine)]
            _, ici_kernel, ici_layer = self._reduce_hosts(group, gl)
            self.pending.setdefault(("bwd", ici_layer, ici_kernel), []).append(("partials", gl, group, partials))

    def grads(self, l, group, grads):
        self.raw[(l, group)] = grads
        self.pending.setdefault(("bwd", l, self._reduce_hosts(group, l)[0]), []).append(
            ("halves", l, group, [grads[k] for k in grads]))

    def _finish_reduction(self, l, names, partials, others):
        for k, p, q in zip(names, partials, others):
            self.full[k] = _sum_partials(p, q, self.place, self.full[k], l)

    def reduced(self):
        assert not self.pending
        return dict(zip(MATRIX_NAMES, _run_comm(_share_comm([self.full[k] for k in MATRIX_NAMES]), "share_halves")))


def kernel(x, positions, pre_mix_norm, w_in, conv_w, attn_out_norm, conv_out_norm, w_out, post_mix_norm, pre_ffn_norm, w_gate_up, w_down, post_ffn_norm, loss_target, m_pre_mix_norm, m_w_in, m_conv_w, m_attn_out_norm, m_conv_out_norm, m_w_out, m_post_mix_norm, m_pre_ffn_norm, m_w_gate_up, m_w_down, m_post_ffn_norm, v_pre_mix_norm, v_w_in, v_conv_w, v_attn_out_norm, v_conv_out_norm, v_w_out, v_post_mix_norm, v_pre_ffn_norm, v_w_gate_up, v_w_down, v_post_ffn_norm):
    params = dict(pre_mix_norm=pre_mix_norm, w_in=w_in, conv_w=conv_w, attn_out_norm=attn_out_norm,
                  conv_out_norm=conv_out_norm, w_out=w_out, post_mix_norm=post_mix_norm, pre_ffn_norm=pre_ffn_norm,
                  w_gate_up=w_gate_up, w_down=w_down, post_ffn_norm=post_ffn_norm)
    mom1 = dict(pre_mix_norm=m_pre_mix_norm, w_in=m_w_in, conv_w=m_conv_w, attn_out_norm=m_attn_out_norm,
                conv_out_norm=m_conv_out_norm, w_out=m_w_out, post_mix_norm=m_post_mix_norm,
                pre_ffn_norm=m_pre_ffn_norm, w_gate_up=m_w_gate_up, w_down=m_w_down, post_ffn_norm=m_post_ffn_norm)
    mom2 = dict(pre_mix_norm=v_pre_mix_norm, w_in=v_w_in, conv_w=v_conv_w, attn_out_norm=v_attn_out_norm,
                conv_out_norm=v_conv_out_norm, w_out=v_w_out, post_mix_norm=v_post_mix_norm,
                pre_ffn_norm=v_pre_ffn_norm, w_gate_up=v_w_gate_up, w_down=v_w_down, post_ffn_norm=v_post_ffn_norm)
    xi, yi, ci = lax.axis_index("x"), lax.axis_index("y"), lax.axis_index("c")
    place = jnp.stack([2 * xi + yi, ci]).astype(jnp.int32)

    exchange = _Exchange(params, place)
    gains = {k: params[k][:, None, :] for k in GAIN_NAMES}
    loss_lanes, grad_x, g_gain, g_conv = _local_step(x[0], positions[0], loss_target[0], gains, exchange)
    grad = exchange.reduced()

    small = [g_gain[k].reshape(-1) for k in GAIN_NAMES] + [g_conv.reshape(-1), loss_lanes.reshape(-1)]
    sizes = [int(s.shape[0]) for s in small]
    flat = jnp.concatenate(small)
    loss_row = (sum(sizes) - LANES) // LANES
    rows = -(-flat.shape[0] // (8 * LANES)) * 8
    flat = jnp.pad(flat, (0, rows * LANES - flat.shape[0])).reshape(rows, LANES)
    total = _allreduce_small(flat, loss_row).reshape(-1)
    offsets = np.cumsum([0] + sizes)
    for i, k in enumerate(GAIN_NAMES):
        grad[k] = total[offsets[i]:offsets[i + 1]].reshape(params[k].shape)
    conv_all = total[offsets[6]:offsets[7]].reshape(DEPTH, N_CHIPS, 3, LANES)
    grad["conv_w"] = lax.dynamic_index_in_dim(conv_all, 2 * xi + yi, axis=1, keepdims=False)
    loss = total[offsets[7]]

    delta, new_m, new_v = {}, {}, {}
    for k in WEIGHT_ORDER:
        shape = params[k].shape
        if k in MATRIX_NAMES:
            tr = {1024: 512, 704: 352, 256: 256}[shape[1]]
            delta[k], new_m[k], new_v[k], grad[k] = _adamw(params[k], grad[k], mom1[k], mom2[k], tr=tr, emit_grad=True)
        else:
            as3 = (lambda a: a) if len(shape) == 3 else (lambda a: a[None])
            d, m, v = _adamw(as3(params[k]), as3(grad[k]), as3(mom1[k]), as3(mom2[k]), tr=as3(params[k]).shape[1])
            delta[k], new_m[k], new_v[k] = d.reshape(shape), m.reshape(shape), v.reshape(shape)

    return (loss, grad_x[None], *[grad[k] for k in WEIGHT_ORDER], *[delta[k] for k in WEIGHT_ORDER],
            *[new_m[k] for k in WEIGHT_ORDER], *[new_v[k] for k in WEIGHT_ORDER])
```

```python
import functools
from typing import Callable, NamedTuple

import numpy as np
import jax
import jax.numpy as jnp
from jax import lax
from jax.experimental import pallas as pl
from jax.experimental.pallas import tpu as pltpu

F32 = jnp.float32
BF16 = jnp.bfloat16
MESH = pl.DeviceIdType.MESH

D_MODEL = 1024
ATTN_W = 512
CONV_W = 512
HEAD_DIM = 64
ROPE_DIM = 16
ROPE_THETA = 500000.0
FFN = 2816
DEPTH = 4
RMS_EPS = 1e-6
NEG_INF = -1e30
N_CHIPS = 4
N_DEV = 8
LANES = 128
BF16_ROWS = 16
DILATIONS = (1, 4, 16)
BAND = 64
TQ = 128
WIN = TQ + 2 * BAND
SCALE = HEAD_DIM ** -0.5

ADAM_LR = 0.001
ADAM_B1 = 0.9
ADAM_B2 = 0.999
ADAM_EPS = 1e-08
ADAM_WD = 0.01
ADAM_STEP = 10

GAIN_NAMES = ("pre_mix_norm", "attn_out_norm", "conv_out_norm", "post_mix_norm", "pre_ffn_norm", "post_ffn_norm")
MATRIX_NAMES = ("w_in", "w_out", "w_gate_up", "w_down")
WEIGHT_ORDER = ("pre_mix_norm", "w_in", "conv_w", "attn_out_norm", "conv_out_norm", "w_out", "post_mix_norm",
                "pre_ffn_norm", "w_gate_up", "w_down", "post_ffn_norm")

ANY = pl.BlockSpec(memory_space=pl.ANY)
WHOLE_VMEM = pl.BlockSpec(memory_space=pltpu.VMEM)
LANE_CONTRACT = (((1,), (1,)), ((), ()))
ROW_CONTRACT = (((0,), (0,)), ((), ()))
CHUNK = 256


def _const_spec(block, index):
    return pl.BlockSpec(block, lambda *_: index)


def _gain_spec(g3, l):
    return _const_spec((None, 1, g3.shape[-1]), (l, 0, 0))


class _Comm(NamedTuple):
    ins: tuple
    inouts: tuple
    out_shapes: tuple
    n_sems: int
    start: Callable
    finish: Callable


def _place():
    x, y, c = lax.axis_index("x"), lax.axis_index("y"), lax.axis_index("c")
    other_chips = [(1 - x, y), (x, 1 - y), (1 - x, 1 - y)]
    return x, y, c, other_chips


def _remote(src, dst, send_sem, recv_sem, to):
    return pltpu.make_async_remote_copy(src_ref=src, dst_ref=dst, send_sem=send_sem, recv_sem=recv_sem,
                                        device_id=to, device_id_type=MESH)


def _call(body, operands, *, name, grid, in_specs, out_specs, out_shape, scratch_shapes=(), aliases=None,
          comm=None):
    in_specs, out_specs, out_shape = list(in_specs), list(out_specs), list(out_shape)
    scratch_shapes = list(scratch_shapes)
    aliases = dict(aliases or {})
    if comm is None:
        out = pl.pallas_call(body, grid=grid, in_specs=in_specs, out_specs=out_specs, out_shape=out_shape,
                             scratch_shapes=scratch_shapes, input_output_aliases=aliases, name=name)(*operands)
        return list(out), None
    n_in, n_out, n_scr = len(in_specs), len(out_shape), len(scratch_shapes)
    n_ci, n_cio, n_co = len(comm.ins), len(comm.inouts), len(comm.out_shapes)

    def hosted(*refs):
        refs = list(refs)
        ins, c_ins = refs[:n_in], refs[n_in:n_in + n_ci]
        base = n_in + n_ci + n_cio
        outs = refs[base:base + n_out]
        c_io = refs[base + n_out:base + n_out + n_cio]
        c_out = refs[base + n_out + n_cio:base + n_out + n_cio + n_co]
        scr = refs[base + n_out + n_cio + n_co:]
        send_sems, recv_sems = scr[n_scr], scr[n_scr + 1]
        if grid:
            first = functools.reduce(jnp.logical_and, [pl.program_id(a) == 0 for a in range(len(grid))])
            last = functools.reduce(jnp.logical_and, [pl.program_id(a) == grid[a] - 1 for a in range(len(grid))])
            pl.when(first)(lambda: comm.start(c_ins, c_io, c_out, send_sems, recv_sems))
            body(*ins, *outs, *scr[:n_scr])
            pl.when(last)(lambda: comm.finish(c_ins, c_io, c_out, send_sems, recv_sems))
        else:
            comm.start(c_ins, c_io, c_out, send_sems, recv_sems)
            body(*ins, *outs, *scr[:n_scr])
            comm.finish(c_ins, c_io, c_out, send_sems, recv_sems)

    res = pl.pallas_call(
        hosted, grid=grid, in_specs=in_specs + [ANY] * (n_ci + n_cio), out_specs=out_specs + [ANY] * (n_cio + n_co),
        out_shape=out_shape + [jax.ShapeDtypeStruct(a.shape, a.dtype) for a in comm.inouts] + list(comm.out_shapes),
        input_output_aliases={**aliases, **{n_in + n_ci + i: n_out + i for i in range(n_cio)}},
        scratch_shapes=scratch_shapes + [pltpu.SemaphoreType.DMA((comm.n_sems,))] * 2,
        name=name)(*operands, *comm.ins, *comm.inouts)
    return list(res[:n_out]), list(res[n_out:])


def _run_comm(comm, name):
    return _call(lambda: None, [], name=name, grid=(), in_specs=[], out_specs=[], out_shape=[], comm=comm)[1]


def _row_half(ref, lead, core, rows, align):
    hr = rows // 2
    return ref.at[(*lead, pl.ds(pl.multiple_of(core * hr, align), hr), slice(None))]


def _gather_comm(slabs, conv_slab=None):
    n = len(slabs)
    n_conv = 0 if conv_slab is None else 3

    def direct(ios, send, recv):
        x, y, c, chips = _place()
        copies = []
        for a in range(n):
            own = _row_half(ios[a], (2 * x + y,), c, slabs[a].shape[1], BF16_ROWS)
            copies += [_remote(own, own, send.at[a * 3 + j], recv.at[a * 3 + j], (*chip, c))
                       for j, chip in enumerate(chips)]
        if conv_slab is not None:
            own = ios[n].at[:, 2 * x + y]
            copies += [_remote(own, own, send.at[6 * n + j], recv.at[6 * n + j], (*chip, c))
                       for j, chip in enumerate(chips)]
        return copies

    def start(ins, ios, outs, send, recv):
        for cp in direct(ios, send, recv):
            cp.start()

    def finish(ins, ios, outs, send, recv):
        x, y, c, chips = _place()
        sibling = (x, y, 1 - c)
        passed = []
        for a in range(n):
            for j, chip in enumerate(chips):
                landed = _row_half(ios[a], (2 * chip[0] + chip[1],), c, slabs[a].shape[1], BF16_ROWS)
                _remote(landed, landed, send.at[a * 3 + j], recv.at[a * 3 + j], (*chip, c)).wait_recv()
                fwd = _remote(landed, landed, send.at[3 * n + a * 3 + j], recv.at[3 * n + a * 3 + j], sibling)
                fwd.start()
                passed.append(fwd)
        if conv_slab is not None:
            for j, chip in enumerate(chips):
                landed = ios[n].at[:, 2 * chip[0] + chip[1]]
                _remote(landed, landed, send.at[6 * n + j], recv.at[6 * n + j], (*chip, c)).wait_recv()
        for a in range(n):
            for j, chip in enumerate(chips):
                landed = _row_half(ios[a], (2 * chip[0] + chip[1],), 1 - c, slabs[a].shape[1], BF16_ROWS)
                _remote(landed, landed, send.at[3 * n + a * 3 + j], recv.at[3 * n + a * 3 + j], sibling).wait_recv()
        for cp in direct(ios, send, recv) + passed:
            cp.wait_send()

    inouts = tuple(slabs) + (() if conv_slab is None else (conv_slab,))
    return _Comm((), inouts, (), 6 * n + n_conv, start, finish)


def _halves_comm(grads):
    n = len(grads)

    def copies(ins, outs, send, recv):
        x, y, c, _ = _place()
        return [_remote(_row_half(ins[a], (slice(None),), 1 - c, grads[a].shape[1], 8), outs[a],
                        send.at[a], recv.at[a], (x, y, 1 - c)) for a in range(n)]

    def start(ins, ios, outs, send, recv):
        for cp in copies(ins, outs, send, recv):
            cp.start()

    def finish(ins, ios, outs, send, recv):
        for cp in copies(ins, outs, send, recv):
            cp.wait()

    out_shapes = tuple(jax.ShapeDtypeStruct((g.shape[0], g.shape[1] // 2, g.shape[2]), F32) for g in grads)
    return _Comm(tuple(grads), (), out_shapes, n, start, finish)


def _partials_comm(partials):
    n = len(partials)

    def copies(ins, outs, send, recv):
        x, y, c, chips = _place()
        return [_remote(ins[a].at[2 * chip[0] + chip[1]], outs[a].at[k], send.at[a * 3 + k], recv.at[a * 3 + k],
                        (*chip, c)) for a in range(n) for k, chip in enumerate(chips)]

    def start(ins, ios, outs, send, recv):
        for cp in copies(ins, outs, send, recv):
            cp.start()

    def finish(ins, ios, outs, send, recv):
        for cp in copies(ins, outs, send, recv):
            cp.wait()

    out_shapes = tuple(jax.ShapeDtypeStruct((3,) + p.shape[1:], BF16) for p in partials)
    return _Comm(tuple(partials), (), out_shapes, 3 * n, start, finish)


def _share_comm(grads):
    n = len(grads)

    def start(ins, ios, outs, send, recv):
        x, y, c, _ = _place()
        for a in range(n):
            mine = _row_half(ios[a], (slice(None),), c, grads[a].shape[1], 8)
            _remote(mine, mine, send.at[a], recv.at[a], (x, y, 1 - c)).start()

    def finish(ins, ios, outs, send, recv):
        x, y, c, _ = _place()
        for a in range(n):
            theirs = _row_half(ios[a], (slice(None),), 1 - c, grads[a].shape[1], 8)
            _remote(theirs, theirs, send.at[a], recv.at[a], (x, y, 1 - c)).wait()

    return _Comm((), tuple(grads), (), n, start, finish)


def _rms_fwd(x, g):
    r = lax.rsqrt(jnp.mean(x * x, axis=-1, keepdims=True) + RMS_EPS)
    return (x * r) * g


def _rms_bwd(x, g, dy):
    r = lax.rsqrt(jnp.mean(x * x, axis=-1, keepdims=True) + RMS_EPS)
    xh = x * r
    u = dy * g
    dx = r * (u - xh * jnp.mean(xh * u, axis=-1, keepdims=True))
    return dx, jnp.sum(dy * xh, axis=0, keepdims=True)


def _accumulate(ref, value, first):
    @pl.when(first)
    def _():
        ref[...] = value

    @pl.when(jnp.logical_not(first))
    def _():
        ref[...] += value


def _rope_coeffs(cos, sin):
    m = lax.broadcasted_iota(jnp.int32, cos.shape, 1) % HEAD_DIM
    a = jnp.where(m < ROPE_DIM, cos, 1.0)
    b = jnp.where(m < ROPE_DIM // 2, -sin, 0.0)
    c = jnp.where((m >= ROPE_DIM // 2) & (m < ROPE_DIM), sin, 0.0)
    return a, b, c


def _rope_apply(t, cos, sin):
    a, b, c = _rope_coeffs(cos, sin)
    n = t.shape[1]
    return a * t + b * pltpu.roll(t, n - ROPE_DIM // 2, 1) + c * pltpu.roll(t, ROPE_DIM // 2, 1)


def _rope_transpose(dt, cos, sin):
    a, b, c = _rope_coeffs(cos, sin)
    n = dt.shape[1]
    return a * dt + pltpu.roll(b * dt, ROPE_DIM // 2, 1) + pltpu.roll(c * dt, n - ROPE_DIM // 2, 1)


def _in_proj(h, w, cos, sin, *, tm, comm=None):
    t, k = h.shape
    s_n, _, n = w.shape
    assert t % tm == 0 and n % LANES == 0

    rotary_shards = -(-2 * ATTN_W // n)
    rotary = list(range(0, 2 * ATTN_W, LANES))
    per_shard = -(-len(rotary) // max(s_n - rotary_shards, 1))

    def body(h_ref, w_ref, cos_ref, sin_ref, o_ref):
        def shard(s):
            o_ref[:, s * n:(s + 1) * n] = jnp.dot(h_ref[...], w_ref[s], preferred_element_type=F32)

        def rope(c0):
            cols = slice(c0, c0 + LANES)
            o_ref[:, cols] = _rope_apply(o_ref[:, cols], cos_ref[...], sin_ref[...])

        for s in range(rotary_shards):
            shard(s)
        pending = list(rotary)
        for s in range(rotary_shards, s_n):
            for c0 in pending[:per_shard]:
                rope(c0)
            pending = pending[per_shard:]
            shard(s)
        for c0 in pending:
            rope(c0)

    lane_tile = pl.BlockSpec((tm, LANES), lambda i: (i, 0))
    out, got = _call(
        body, [h, w, cos, sin], name="in_proj", grid=(t // tm,),
        in_specs=[pl.BlockSpec((tm, k), lambda i: (i, 0)), _const_spec(w.shape, (0, 0, 0)), lane_tile, lane_tile],
        out_specs=[pl.BlockSpec((tm, s_n * n), lambda i: (i, 0))],
        out_shape=[jax.ShapeDtypeStruct((t, s_n * n), F32)], comm=comm)
    return out[0], got


def _mm_nt(a, w, *, tm, tko, name, comm=None):
    t, sn = a.shape
    s_n, ko, n = w.shape
    assert sn == s_n * n and t % tm == 0 and ko % tko == 0

    def body(a_ref, w_ref, o_ref):
        acc = lax.dot_general(a_ref[...], w_ref[...], (((1,), (1,)), ((), ())), preferred_element_type=F32)
        if s_n == 1:
            o_ref[...] = acc
        else:
            _accumulate(o_ref, acc, pl.program_id(2) == 0)

    out, got = _call(
        body, [a, w], name=name, grid=(t // tm, ko // tko, s_n),
        in_specs=[pl.BlockSpec((tm, n), lambda i, j, s: (i, s)),
                  pl.BlockSpec((None, tko, n), lambda i, j, s: (s, j, 0))],
        out_specs=[pl.BlockSpec((tm, tko), lambda i, j, s: (i, j))],
        out_shape=[jax.ShapeDtypeStruct((t, ko), F32)], comm=comm)
    return out[0], got


def _dx_through_norms(operands, in_specs, dx_rows, dres, pre, post, *, tm, name, comm=None):
    t, d_model = dres.shape
    (x, gx3, lx), (y, gy3, ly) = pre, post
    n_op = len(operands)
    row = pl.BlockSpec((tm, d_model), lambda i: (i, 0))
    gsum = _const_spec((1, d_model), (0, 0))

    def body(*refs):
        d_ref, x_ref, gx_ref, y_ref, gy_ref, dn_ref, dgx_ref, dy_ref, dgy_ref = refs[n_op:]
        first = pl.program_id(0) == 0
        dx, dgx = _rms_bwd(x_ref[...], gx_ref[...], dx_rows(refs[:n_op]))
        d_new = d_ref[...] + dx
        dn_ref[...] = d_new
        _accumulate(dgx_ref, dgx, first)
        dy, dgy = _rms_bwd(y_ref[...], gy_ref[...], d_new)
        dy_ref[...] = dy.astype(BF16)
        _accumulate(dgy_ref, dgy, first)

    out, got = _call(
        body, list(operands) + [dres, x, gx3, y, gy3], name=name, grid=(t // tm,),
        in_specs=list(in_specs) + [row, row, _gain_spec(gx3, lx), row, _gain_spec(gy3, ly)],
        out_specs=[row, gsum, row, gsum],
        out_shape=[jax.ShapeDtypeStruct((t, d_model), F32), jax.ShapeDtypeStruct((1, d_model), F32),
                   jax.ShapeDtypeStruct((t, d_model), BF16), jax.ShapeDtypeStruct((1, d_model), F32)],
        comm=comm)
    return out[0], out[2], out[1], out[3], got


def _gate_up_dx_norms(dg, du, w, dres, pre, post, *, tm, comm=None):
    s_n, ko, n = w.shape
    half = s_n // 2

    def dx_rows(refs):
        dg_ref, du_ref, w_ref = refs
        acc = jnp.zeros((tm, ko), F32)
        for s in range(half):
            cols = slice(s * n, (s + 1) * n)
            acc = acc + lax.dot_general(dg_ref[:, cols], w_ref[s], LANE_CONTRACT, preferred_element_type=F32)
            acc = acc + lax.dot_general(du_ref[:, cols], w_ref[half + s], LANE_CONTRACT, preferred_element_type=F32)
        return acc

    a_spec = pl.BlockSpec((tm, half * n), lambda i: (i, 0))
    return _dx_through_norms([dg, du, w], [a_spec, a_spec, _const_spec(w.shape, (0, 0, 0))], dx_rows, dres, pre, post,
                             tm=tm, name="gate_up_dx", comm=comm)


def _in_proj_dx_norms(d_attn3, d_conv3, w, dres, pre, post, *, tm, comm=None):
    s_n, ko, n = w.shape
    per = n // CHUNK
    pieces = d_attn3.shape[0]
    width = d_attn3.shape[2]

    def dx_rows(refs):
        a_ref, b_ref, w_ref = refs
        acc = jnp.zeros((tm, ko), F32)
        for c in range(s_n * per):
            src = a_ref if c // 2 < pieces else b_ref
            piece, c0 = (c // 2) % pieces, (c % 2) * CHUNK
            acc = acc + lax.dot_general(src[piece, :, c0:c0 + CHUNK], w_ref[c // per, :, (c % per) * CHUNK:(c % per + 1) * CHUNK],
                                        LANE_CONTRACT, preferred_element_type=F32)
        return acc

    stack = pl.BlockSpec((pieces, tm, width), lambda i: (0, i, 0))
    return _dx_through_norms([d_attn3, d_conv3, w], [stack, stack, _const_spec(w.shape, (0, 0, 0))], dx_rows, dres, pre,
                             post, tm=tm, name="in_proj_dx", comm=comm)


def _mm_tn(a, b, s_n, *, tka, name, into=None, shard0=0, comm=None):
    t, ka = a.shape
    n = b.shape[1] // s_n
    assert b.shape[0] == t and ka % tka == 0

    def body(a_ref, b_ref, *rest):
        rest[-1][...] = lax.dot_general(a_ref[...], b_ref[...], ROW_CONTRACT, preferred_element_type=F32)

    operands, in_specs, aliases = [a, b], [pl.BlockSpec((t, tka), lambda i, s: (0, i)),
                                           pl.BlockSpec((t, n), lambda i, s: (0, s))], {}
    out_shape = jax.ShapeDtypeStruct((s_n, ka, n), F32)
    if into is not None:
        operands, in_specs, aliases = operands + [into], in_specs + [ANY], {2: 0}
        out_shape = jax.ShapeDtypeStruct(into.shape, F32)
    out, got = _call(body, operands, name=name, grid=(ka // tka, s_n), in_specs=in_specs,
                     out_specs=[pl.BlockSpec((None, tka, n), lambda i, s: (shard0 + s, i, 0))],
                     out_shape=[out_shape], aliases=aliases, comm=comm)
    return out[0], got


def _gate_up_swiglu(h, w, *, tm, comm=None):
    t, k = h.shape
    s_n, _, n = w.shape
    half = s_n // 2

    def body(h_ref, wg_ref, wu_ref, g_ref, u_ref, a_ref):
        g = jnp.dot(h_ref[...], wg_ref[...], preferred_element_type=F32)
        u = jnp.dot(h_ref[...], wu_ref[...], preferred_element_type=F32)
        g_ref[...] = g.astype(BF16)
        u_ref[...] = u.astype(BF16)
        a_ref[...] = (g * jax.nn.sigmoid(g) * u).astype(BF16)

    col = pl.BlockSpec((tm, n), lambda i, j: (i, j))
    out, got = _call(
        body, [h, w, w], name="gate_up", grid=(t // tm, half),
        in_specs=[pl.BlockSpec((tm, k), lambda i, j: (i, 0)), pl.BlockSpec((None, k, n), lambda i, j: (j, 0, 0)),
                  pl.BlockSpec((None, k, n), lambda i, j: (half + j, 0, 0))],
        out_specs=[col, col, col],
        out_shape=[jax.ShapeDtypeStruct((t, half * n), BF16)] * 3, comm=comm)
    return out[0], out[1], out[2], got


def _down_dx_swiglu_bwd(df, w, g, u, *, tm, tko):
    t, k = df.shape
    _, ko, _ = w.shape
    assert t % tm == 0 and ko % tko == 0

    def body(df_ref, w_ref, g_ref, u_ref, dg_ref, du_ref):
        d = lax.dot_general(df_ref[...], w_ref[...], LANE_CONTRACT, preferred_element_type=F32)
        gg = g_ref[...].astype(F32)
        sig = jax.nn.sigmoid(gg)
        dg_ref[...] = (d * u_ref[...].astype(F32) * (sig * (1.0 + gg * (1.0 - sig)))).astype(BF16)
        du_ref[...] = (d * (gg * sig)).astype(BF16)

    col = pl.BlockSpec((tm, tko), lambda i, j: (i, j))
    return pl.pallas_call(
        body, grid=(t // tm, ko // tko),
        in_specs=[pl.BlockSpec((tm, k), lambda i, j: (i, 0)), pl.BlockSpec((None, tko, k), lambda i, j: (0, j, 0)),
                  col, col],
        out_specs=[col, col], out_shape=[jax.ShapeDtypeStruct((t, ko), BF16)] * 2, name="down_dx")(df, w, g, u)


def _in_proj_dx(d_attn3, d_conv3, w, *, tm, comm=None):
    _, t, _ = d_attn3.shape
    s_n, ko, n = w.shape
    half, per = s_n // 2, n // CHUNK
    assert t % tm == 0

    def body(*refs):
        a_refs, b_refs, wa_ref, wb_ref, o_ref = refs[:per], refs[per:2 * per], refs[2 * per], refs[2 * per + 1], refs[-1]
        acc = jnp.zeros(o_ref.shape, F32)
        for r in range(per):
            cols = slice(r * CHUNK, (r + 1) * CHUNK)
            acc = acc + lax.dot_general(a_refs[r][...], wa_ref[:, cols], LANE_CONTRACT, preferred_element_type=F32)
            acc = acc + lax.dot_general(b_refs[r][...], wb_ref[:, cols], LANE_CONTRACT, preferred_element_type=F32)
        _accumulate(o_ref, acc, pl.program_id(1) == 0)

    piece = lambda r: pl.BlockSpec((None, tm, CHUNK), lambda i, s: ((per * s + r) // 2, i, (per * s + r) % 2))
    out, got = _call(
        body, [d_attn3] * per + [d_conv3] * per + [w, w], name="in_proj_dx", grid=(t // tm, half),
        in_specs=[piece(r) for r in range(per)] * 2
        + [pl.BlockSpec((None, ko, n), lambda i, s: (s, 0, 0)), pl.BlockSpec((None, ko, n), lambda i, s: (half + s, 0, 0))],
        out_specs=[pl.BlockSpec((tm, ko), lambda i, s: (i, 0))],
        out_shape=[jax.ShapeDtypeStruct((t, ko), F32)], comm=comm)
    return out[0], got


def _in_proj_dw(h, d_attn3, d_conv3, s_n, *, tka):
    t, ka = h.shape
    half = s_n // 2
    n = 3 * d_attn3.shape[2] // half
    per = n // CHUNK
    assert ka % tka == 0

    def body(*refs):
        h_ref, o_ref = refs[0], refs[-1]
        for side in range(2):
            for r in range(per):
                o_ref[side, :, r * CHUNK:(r + 1) * CHUNK] = lax.dot_general(
                    h_ref[...], refs[1 + side * per + r][...], ROW_CONTRACT, preferred_element_type=F32)

    piece = lambda r: pl.BlockSpec((None, t, CHUNK), lambda i, s: ((per * s + r) // 2, 0, (per * s + r) % 2))
    out = pl.pallas_call(
        body, grid=(ka // tka, half),
        in_specs=[pl.BlockSpec((t, tka), lambda i, s: (0, i))] + [piece(r) for r in range(per)] * 2,
        out_specs=pl.BlockSpec((2, None, tka, n), lambda i, s: (0, s, i, 0)),
        out_shape=jax.ShapeDtypeStruct((2, half, ka, n), F32), name="in_proj_dw")(h, *[d_attn3] * per, *[d_conv3] * per)
    return out.reshape(s_n, ka, n)


def _rope_tables(positions_col, inv_freq_row):
    t = positions_col.shape[0]

    def body(pos_ref, f_ref, cos_ref, sin_ref):
        ang = pos_ref[...].astype(F32) * f_ref[...]
        cos_ref[...] = jnp.cos(ang)
        sin_ref[...] = jnp.sin(ang)

    return pl.pallas_call(
        body, out_shape=[jax.ShapeDtypeStruct((t, LANES), F32)] * 2, name="rope_tables")(positions_col, inv_freq_row)


def _norm_fwd(x, g3, l, *, tm):
    t, w = x.shape

    def body(x_ref, g_ref, h_ref):
        h_ref[...] = _rms_fwd(x_ref[...], g_ref[...]).astype(BF16)

    return pl.pallas_call(
        body, grid=(t // tm,),
        in_specs=[pl.BlockSpec((tm, w), lambda i: (i, 0)), _gain_spec(g3, l)],
        out_specs=pl.BlockSpec((tm, w), lambda i: (i, 0)),
        out_shape=jax.ShapeDtypeStruct((t, w), BF16), name="norm_fwd")(x, g3)


def _mm_resnorm(a, w, x, g_post3, l_post, g_next3, l_next, *, tm, name, comm=None):
    t, k = a.shape
    _, _, n = w.shape
    with_next = g_next3 is not None
    row = pl.BlockSpec((tm, n), lambda i: (i, 0))

    def body(a_ref, w_ref, x_ref, gp_ref, *rest):
        y = jnp.dot(a_ref[...], w_ref[...], preferred_element_type=F32)
        x_new = x_ref[...] + _rms_fwd(y, gp_ref[...])
        if with_next:
            gn_ref, y_ref, xo_ref, h_ref = rest
            h_ref[...] = _rms_fwd(x_new, gn_ref[...]).astype(BF16)
        else:
            y_ref, xo_ref = rest
        y_ref[...] = y
        xo_ref[...] = x_new

    ins = [a, w, x, g_post3] + ([g_next3] if with_next else [])
    in_specs = ([pl.BlockSpec((tm, k), lambda i: (i, 0)), _const_spec((None, k, n), (0, 0, 0)), row,
                 _gain_spec(g_post3, l_post)] + ([_gain_spec(g_next3, l_next)] if with_next else []))
    out_shape = [jax.ShapeDtypeStruct((t, n), F32)] * 2 + ([jax.ShapeDtypeStruct((t, n), BF16)] if with_next else [])
    out, got = _call(body, ins, name=name, grid=(t // tm,), in_specs=in_specs, out_specs=[row] * len(out_shape),
                     out_shape=out_shape, comm=comm)
    return out[0], out[1], (out[2] if with_next else None), got


def _conv_fwd(proj, conv_w, l):
    t = proj.shape[0]
    col0 = 3 * ATTN_W // LANES

    def body(u_ref, gb_ref, gc_ref, w_ref, y_ref):
        c = gc_ref[...] * u_ref[...]
        row = lax.broadcasted_iota(jnp.int32, c.shape, 0)
        c_prev = jnp.where(row == 0, 0.0, pltpu.roll(c, 1, 0))
        c_next = jnp.where(row == t - 1, 0.0, pltpu.roll(c, t - 1, 0))
        w = w_ref[...]
        y_ref[...] = gb_ref[...] * (w[0:1] * c_prev + w[1:2] * c + w[2:3] * c_next)

    nj = CONV_W // LANES
    cols = lambda base: pl.BlockSpec((t, LANES), lambda j: (0, base + j))
    return pl.pallas_call(
        body, grid=(nj,),
        in_specs=[cols(col0), cols(col0 + nj), cols(col0 + 2 * nj),
                  pl.BlockSpec((None, None, 3, LANES), lambda j: (l, j, 0, 0))],
        out_specs=pl.BlockSpec((t, LANES), lambda j: (0, j)),
        out_shape=jax.ShapeDtypeStruct((t, CONV_W), F32), name="conv_fwd")(proj, proj, proj, conv_w)


def _merge_fwd(attn, conv_y, ga3, gc3, l, *, tm):
    t = attn.shape[0]
    row = pl.BlockSpec((tm, ATTN_W), lambda i: (i, 0))

    def body(a_ref, c_ref, ga_ref, gc_ref, m_ref):
        m_ref[:, :ATTN_W] = _rms_fwd(a_ref[...], ga_ref[...]).astype(BF16)
        m_ref[:, ATTN_W:] = _rms_fwd(c_ref[...], gc_ref[...]).astype(BF16)

    return pl.pallas_call(
        body, grid=(t // tm,),
        in_specs=[row, row, _gain_spec(ga3, l), _gain_spec(gc3, l)],
        out_specs=pl.BlockSpec((tm, D_MODEL), lambda i: (i, 0)),
        out_shape=jax.ShapeDtypeStruct((t, D_MODEL), BF16), name="merge_fwd")(attn, conv_y, ga3, gc3)


def _loss_fwd_bwd(y, target, *, tm):
    t, w = y.shape
    row = pl.BlockSpec((tm, w), lambda i: (i, 0))

    def body(y_ref, t_ref, dy_ref, loss_ref):
        e = y_ref[...] - t_ref[...]
        dy_ref[...] = e * (1.0 / w)
        sq = jnp.sum(e * e, axis=0, keepdims=True) * (0.5 / w)
        part = sq[:, :LANES]
        for j in range(1, w // LANES):
            part = part + sq[:, j * LANES:(j + 1) * LANES]
        _accumulate(loss_ref, part, pl.program_id(0) == 0)

    return pl.pallas_call(
        body, grid=(t // tm,), in_specs=[row, row],
        out_specs=[row, _const_spec((1, LANES), (0, 0))],
        out_shape=[jax.ShapeDtypeStruct((t, w), F32), jax.ShapeDtypeStruct((1, LANES), F32)], name="loss")(y, target)


def _tile_rows(t, nt, lb, d):
    r = t // nt
    q0 = (t % nt) * TQ
    m0 = jnp.clip(q0 - BAND, 0, lb - WIN)
    if d == 1:
        return pl.ds(pl.multiple_of(q0, TQ), TQ), pl.ds(pl.multiple_of(m0, BAND), WIN), m0 - q0
    return pl.ds(r + d * q0, TQ, stride=d), pl.ds(r + d * m0, WIN, stride=d), m0 - q0


def _for_row_chunks(t, fn, chunk=512):
    def step(i, carry):
        fn(pl.ds(pl.multiple_of(i * chunk, chunk), chunk))
        return carry

    lax.fori_loop(0, t // chunk, step, 0)


WINDOW_OFFSETS = (-BAND, 0, -2 * BAND)


def _fill_band_bias(bias_ref):
    rel0 = (lax.broadcasted_iota(jnp.int32, (2 * TQ, WIN), 1)
            - lax.broadcasted_iota(jnp.int32, (2 * TQ, WIN), 0) % TQ)
    for j, off in enumerate(WINDOW_OFFSETS):
        rel = rel0 + off
        bias_ref[j] = jnp.where((rel >= -BAND) & (rel <= BAND), 0.0, NEG_INF)


def _fill_sequence_bias(bias_ref):
    rel = (lax.broadcasted_iota(jnp.int32, (2 * WIN, WIN), 1) - lax.broadcasted_iota(jnp.int32, (2 * WIN, WIN), 0) % WIN)
    bias_ref[...] = jnp.where((rel >= -BAND) & (rel <= BAND), 0.0, NEG_INF)


def _band_bias(bias_ref, off):
    return bias_ref[jnp.where(off == WINDOW_OFFSETS[0], 0, jnp.where(off == WINDOW_OFFSETS[1], 1, 2))]


def _stack_heads(a, first_head):
    return jnp.concatenate([jnp.where(first_head, a, 0.0), jnp.where(first_head, 0.0, a)], axis=0)


def _unstack_heads(a2, first_head):
    n = a2.shape[0] // 2
    return jnp.where(first_head, a2[:n], a2[n:])


def _attn_fwd(proj, comm=None):
    t = proj.shape[0]
    npair = ATTN_W // LANES

    def body(q_ref, k_ref, v_ref, o_ref, lse_ref, o1, o2, l0, l1, l2, m1, m2, bias, bias_seq):
        _fill_band_bias(bias)
        _fill_sequence_bias(bias_seq)
        outs, dens, maxs = (o_ref, o1, o2), (l0, l1, l2), (lse_ref, m1, m2)

        def softmax_tile(b, qrows, krows, n_q, band_bias):
            first_head = lax.broadcasted_iota(jnp.int32, (n_q, LANES), 1) < HEAD_DIM
            q2 = _stack_heads(q_ref[qrows, :] * SCALE, first_head).astype(BF16)
            kw = k_ref[krows, :].astype(BF16)
            vw = jnp.concatenate([v_ref[krows, :].astype(BF16), jnp.ones((WIN, LANES), BF16)], axis=1)
            s = lax.dot_general(q2, kw, LANE_CONTRACT, preferred_element_type=F32) + band_bias
            m = jnp.max(s, axis=-1, keepdims=True)
            pv = jnp.dot(jnp.exp(s - m).astype(BF16), vw, preferred_element_type=F32)
            outs[b][qrows, :] = _unstack_heads(pv[:, :LANES], first_head)
            dens[b][qrows, :] = _unstack_heads(pv[:, LANES:], first_head)
            maxs[b][qrows, :] = _unstack_heads(jnp.broadcast_to(m, (2 * n_q, LANES)), first_head)

        for b, d in enumerate(DILATIONS):
            lb = t // d
            if lb == WIN:
                def sequence(r, carry, b=b, d=d):
                    rows = pl.ds(r, WIN, stride=d)
                    softmax_tile(b, rows, rows, WIN, bias_seq[...])
                    return carry

                lax.fori_loop(0, d, sequence, 0, unroll=8)
                continue
            nt = lb // TQ

            def tile(ti, carry, b=b, d=d, lb=lb, nt=nt):
                qrows, krows, off = _tile_rows(ti, nt, lb, d)
                softmax_tile(b, qrows, krows, TQ, _band_bias(bias, off))
                return carry

            lax.fori_loop(0, d * nt, tile, 0, unroll=16)

        def finish(rows):
            ms = [m_b[rows, :] for m_b in maxs]
            m_all = jnp.maximum(jnp.maximum(ms[0], ms[1]), ms[2])
            ws = [jnp.exp(m_b - m_all) for m_b in ms]
            den = ws[0] * dens[0][rows, :] + ws[1] * dens[1][rows, :] + ws[2] * dens[2][rows, :]
            num = ws[0] * outs[0][rows, :] + ws[1] * outs[1][rows, :] + ws[2] * outs[2][rows, :]
            o_ref[rows, :] = num / den
            lse_ref[rows, :] = m_all + jnp.log(den)

        _for_row_chunks(t, finish, 256)

    cols = lambda base: pl.BlockSpec((t, LANES), lambda g: (0, base + g))
    out, got = _call(
        body, [proj, proj, proj], name="attn_fwd", grid=(npair,),
        in_specs=[cols(0), cols(npair), cols(2 * npair)],
        out_specs=[cols(0), cols(0)],
        out_shape=[jax.ShapeDtypeStruct((t, ATTN_W), F32)] * 2,
        scratch_shapes=[pltpu.VMEM((t, LANES), F32)] * 7 + [pltpu.VMEM((len(WINDOW_OFFSETS), 2 * TQ, WIN), F32),
                                                            pltpu.VMEM((2 * WIN, WIN), F32)],
        comm=comm)
    return out[0], out[1], got


def _attn_bwd(proj, cos, sin, d_attn, lse, delta, comm=None):
    t = proj.shape[0]
    npair = ATTN_W // LANES

    def body(q_ref, k_ref, v_ref, cos_ref, sin_ref, do_ref, l_ref, dl_ref, dqkv_ref,
             dq_acc, dk_acc, dv_acc, bias, bias_seq):
        _fill_band_bias(bias)
        _fill_sequence_bias(bias_seq)
        dq_acc[...] = jnp.zeros(dq_acc.shape, F32)
        dk_acc[...] = jnp.zeros(dk_acc.shape, F32)
        dv_acc[...] = jnp.zeros(dv_acc.shape, F32)
        def stack_column(a):
            return jnp.concatenate([a[:, 0:1], a[:, HEAD_DIM:HEAD_DIM + 1]], axis=0)

        def grad_tile(qrows, krows, n_q, band_bias):
            first_head = lax.broadcasted_iota(jnp.int32, (n_q, LANES), 1) < HEAD_DIM
            q2 = _stack_heads(q_ref[qrows, :] * SCALE, first_head).astype(BF16)
            do2 = _stack_heads(do_ref[qrows, :], first_head).astype(BF16)
            kw = k_ref[krows, :].astype(BF16)
            vw = v_ref[krows, :].astype(BF16)
            s = lax.dot_general(q2, kw, LANE_CONTRACT, preferred_element_type=F32) + band_bias
            p = jnp.exp(s - stack_column(l_ref[qrows, :]))
            dp = lax.dot_general(do2, vw, LANE_CONTRACT, preferred_element_type=F32)
            ds = (p * (dp - stack_column(dl_ref[qrows, :]))).astype(BF16)
            dq2 = jnp.dot(ds, kw, preferred_element_type=F32)
            dq_acc[qrows, :] += _unstack_heads(dq2, first_head) * SCALE
            dk_acc[krows, :] += lax.dot_general(ds, q2, ROW_CONTRACT, preferred_element_type=F32)
            dv_acc[krows, :] += lax.dot_general(p.astype(BF16), do2, ROW_CONTRACT, preferred_element_type=F32)

        for d in DILATIONS:
            lb = t // d
            if lb == WIN:
                def sequence(r, carry, d=d):
                    rows = pl.ds(r, WIN, stride=d)
                    grad_tile(rows, rows, WIN, bias_seq[...])
                    return carry

                lax.fori_loop(0, d, sequence, 0, unroll=4)
                continue
            nt = lb // TQ

            def tile(ti, carry, d=d, lb=lb, nt=nt):
                qrows, krows, off = _tile_rows(ti, nt, lb, d)
                grad_tile(qrows, krows, TQ, _band_bias(bias, off))
                return carry

            lax.fori_loop(0, d * nt, tile, 0, unroll=8)

        def finish(rows):
            dqkv_ref[0, rows, :] = _rope_transpose(dq_acc[rows, :], cos_ref[rows, :], sin_ref[rows, :]).astype(BF16)
            dqkv_ref[1, rows, :] = _rope_transpose(dk_acc[rows, :], cos_ref[rows, :], sin_ref[rows, :]).astype(BF16)
            dqkv_ref[2, rows, :] = dv_acc[rows, :].astype(BF16)

        _for_row_chunks(t, finish)

    cols = lambda base: pl.BlockSpec((t, LANES), lambda g: (0, base + g))
    out, got = _call(
        body, [proj, proj, proj, cos, sin, d_attn, lse, delta], name="attn_bwd", grid=(npair,),
        in_specs=[cols(0), cols(npair), cols(2 * npair), WHOLE_VMEM, WHOLE_VMEM, cols(0), cols(0), cols(0)],
        out_specs=[pl.BlockSpec((3, t, LANES), lambda g: (0, 0, g))],
        out_shape=[jax.ShapeDtypeStruct((3, t, ATTN_W), BF16)],
        scratch_shapes=[pltpu.VMEM((t, LANES), F32)] * 3 + [pltpu.VMEM((len(WINDOW_OFFSETS), 2 * TQ, WIN), F32),
                                                            pltpu.VMEM((2 * WIN, WIN), F32)],
        comm=comm)
    return out[0], got


def _norm_bwd(dres, pre, post, *, tm, comm=None):
    t, w = dres.shape
    row = pl.BlockSpec((tm, w), lambda i: (i, 0))
    gsum = _const_spec((1, w), (0, 0))
    ins, in_specs, out_shape, out_specs = [dres], [row], [], []
    if pre is not None:
        dh, x, g3, l = pre
        ins += [dh, x, g3]
        in_specs += [row, row, _gain_spec(g3, l)]
        out_shape += [jax.ShapeDtypeStruct((t, w), F32), jax.ShapeDtypeStruct((1, w), F32)]
        out_specs += [row, gsum]
    if post is not None:
        y, g3, l = post
        ins += [y, g3]
        in_specs += [row, _gain_spec(g3, l)]
        out_shape += [jax.ShapeDtypeStruct((t, w), BF16), jax.ShapeDtypeStruct((1, w), F32)]
        out_specs += [row, gsum]
    n_in = len(ins)

    def body(*refs):
        first = pl.program_id(0) == 0
        ins_r, outs_r = list(refs[:n_in]), list(refs[n_in:])
        d = ins_r.pop(0)[...]
        if pre is not None:
            dh_ref, x_ref, g_ref = ins_r[:3]
            ins_r = ins_r[3:]
            dx, dg = _rms_bwd(x_ref[...], g_ref[...], dh_ref[...])
            d = d + dx
            outs_r.pop(0)[...] = d
            _accumulate(outs_r.pop(0), dg, first)
        if post is not None:
            y_ref, g_ref = ins_r
            dy, dg = _rms_bwd(y_ref[...], g_ref[...], d)
            outs_r.pop(0)[...] = dy.astype(BF16)
            _accumulate(outs_r.pop(0), dg, first)

    out, got = _call(body, ins, name="norm_bwd", grid=(t // tm,), in_specs=in_specs, out_specs=out_specs,
                     out_shape=out_shape, comm=comm)
    d_new, dg_pre = (out.pop(0), out.pop(0)) if pre is not None else (None, None)
    dy, dg_post = (out.pop(0), out.pop(0)) if post is not None else (None, None)
    return d_new, dy, dg_pre, dg_post, got


def _merge_bwd(d_merged, attn, conv_y, ga3, gc3, l, *, tm, comm=None):
    t = attn.shape[0]
    row = pl.BlockSpec((tm, ATTN_W), lambda i: (i, 0))
    gsum = _const_spec((1, ATTN_W), (0, 0))

    def body(dma_ref, dmc_ref, a_ref, c_ref, ga_ref, gc_ref, da_ref, dl_ref, dc_ref, dga_ref, dgc_ref):
        first = pl.program_id(0) == 0
        attn_t = a_ref[...]
        da, dga = _rms_bwd(attn_t, ga_ref[...], dma_ref[...])
        dc, dgc = _rms_bwd(c_ref[...], gc_ref[...], dmc_ref[...])
        da_ref[...] = da
        dc_ref[...] = dc
        same_head = (lax.broadcasted_iota(jnp.int32, (ATTN_W, ATTN_W), 0) // HEAD_DIM
                     == lax.broadcasted_iota(jnp.int32, (ATTN_W, ATTN_W), 1) // HEAD_DIM).astype(BF16)
        rest = da * attn_t
        total = jnp.zeros(rest.shape, F32)
        for _ in range(3):
            term = rest.astype(BF16)
            total = total + jnp.dot(term, same_head, preferred_element_type=F32)
            rest = rest - term.astype(F32)
        dl_ref[...] = total
        _accumulate(dga_ref, dga, first)
        _accumulate(dgc_ref, dgc, first)

    out, got = _call(
        body, [d_merged, d_merged, attn, conv_y, ga3, gc3], name="merge_bwd", grid=(t // tm,),
        in_specs=[pl.BlockSpec((tm, ATTN_W), lambda i: (i, 0)), pl.BlockSpec((tm, CONV_W), lambda i: (i, 1)),
                  row, row, _gain_spec(ga3, l), _gain_spec(gc3, l)],
        out_specs=[row, row, row, gsum, gsum],
        out_shape=[jax.ShapeDtypeStruct((t, ATTN_W), F32)] * 3 + [jax.ShapeDtypeStruct((1, ATTN_W), F32)] * 2,
        comm=comm)
    return (*out, got)


def _conv_bwd(proj, conv_w, l, d_conv_y):
    t = proj.shape[0]
    col0 = 3 * ATTN_W // LANES
    nj = CONV_W // LANES

    def body(u_ref, gb_ref, gc_ref, w_ref, dy_ref, d3_ref, dw_ref):
        u, gc, dy = u_ref[...], gc_ref[...], dy_ref[...]
        row = lax.broadcasted_iota(jnp.int32, u.shape, 0)
        down = lambda a: jnp.where(row == 0, 0.0, pltpu.roll(a, 1, 0))
        up = lambda a: jnp.where(row == t - 1, 0.0, pltpu.roll(a, t - 1, 0))
        w = w_ref[...]
        c = gc * u
        c_prev, c_next = down(c), up(c)
        d3_ref[1] = (dy * (w[0:1] * c_prev + w[1:2] * c + w[2:3] * c_next)).astype(BF16)
        dz = dy * gb_ref[...]
        dc = w[0:1] * up(dz) + w[1:2] * dz + w[2:3] * down(dz)
        d3_ref[0] = (dc * gc).astype(BF16)
        d3_ref[2] = (dc * u).astype(BF16)
        dw_ref[0:1, :] = jnp.sum(dz * c_prev, axis=0, keepdims=True)
        dw_ref[1:2, :] = jnp.sum(dz * c, axis=0, keepdims=True)
        dw_ref[2:3, :] = jnp.sum(dz * c_next, axis=0, keepdims=True)

    cols = lambda base: pl.BlockSpec((t, LANES), lambda j: (0, base + j))
    return pl.pallas_call(
        body, grid=(nj,),
        in_specs=[cols(col0), cols(col0 + nj), cols(col0 + 2 * nj),
                  pl.BlockSpec((None, None, 3, LANES), lambda j: (l, j, 0, 0)), cols(0)],
        out_specs=[pl.BlockSpec((3, t, LANES), lambda j: (0, 0, j)), pl.BlockSpec((None, 3, LANES), lambda j: (j, 0, 0))],
        out_shape=[jax.ShapeDtypeStruct((3, t, CONV_W), BF16), jax.ShapeDtypeStruct((nj, 3, LANES), F32)],
        name="conv_bwd")(proj, proj, proj, conv_w, d_conv_y)


def _own_shard_slab(w, l, place, dtype):
    _, rows, cols = w.shape
    tr = rows if rows <= 704 else 512
    assert rows % tr == 0

    def body(p_ref, w_ref, o_ref):
        del p_ref
        o_ref[...] = w_ref[...].astype(dtype)

    grid_spec = pltpu.PrefetchScalarGridSpec(
        num_scalar_prefetch=1, grid=(rows // tr,),
        in_specs=[pl.BlockSpec((None, tr, cols), lambda i, p: (l, i, 0))],
        out_specs=pl.BlockSpec((None, tr, cols), lambda i, p: (p[0], i, 0)))
    return pl.pallas_call(body, grid_spec=grid_spec, name="own_shard_slab",
                          out_shape=jax.ShapeDtypeStruct((N_CHIPS, rows, cols), dtype))(place, w)


def _own_conv_slab(w, place):
    depth = w.shape[0]

    def body(p_ref, w_ref, o_ref):
        del p_ref
        o_ref[...] = w_ref[...]

    grid_spec = pltpu.PrefetchScalarGridSpec(
        num_scalar_prefetch=1, grid=(depth,),
        in_specs=[pl.BlockSpec((None, 3, LANES), lambda l, p: (l, 0, 0))],
        out_specs=pl.BlockSpec((None, None, 3, LANES), lambda l, p: (l, p[0], 0, 0)))
    return pl.pallas_call(body, grid_spec=grid_spec, name="own_conv_slab",
                          out_shape=jax.ShapeDtypeStruct((depth, N_CHIPS, 3, LANES), F32))(place, w)


def _add_halves(grad, got, place):
    s_n, rows, cols = grad.shape
    hr = rows // 2

    def body(p_ref, g_ref, r_ref, o_ref):
        del p_ref
        o_ref[...] = (g_ref[...] + r_ref[...]).astype(BF16)

    grid_spec = pltpu.PrefetchScalarGridSpec(
        num_scalar_prefetch=1, grid=(s_n,),
        in_specs=[pl.BlockSpec((None, hr, cols), lambda s, p: (s, p[1], 0)),
                  pl.BlockSpec((None, hr, cols), lambda s, p: (s, 0, 0))],
        out_specs=pl.BlockSpec((None, hr, cols), lambda s, p: (s, 0, 0)))
    return pl.pallas_call(body, grid_spec=grid_spec, out_shape=jax.ShapeDtypeStruct((s_n, hr, cols), BF16),
                          name="add_halves")(place, grad, got)


def _sum_partials(partial, got, place, acc, l):
    _, hr, cols = partial.shape

    def body(p_ref, mine_ref, got_ref, acc_ref, o_ref):
        del p_ref, acc_ref
        total = mine_ref[...].astype(F32)
        for k in range(3):
            total = total + got_ref[k].astype(F32)
        o_ref[...] = total

    grid_spec = pltpu.PrefetchScalarGridSpec(
        num_scalar_prefetch=1, grid=(1,),
        in_specs=[pl.BlockSpec((None, hr, cols), lambda i, p: (p[0], 0, 0)),
                  pl.BlockSpec((3, hr, cols), lambda i, p: (0, 0, 0)), ANY],
        out_specs=pl.BlockSpec((None, hr, cols), lambda i, p: (l, p[1], 0)))
    return pl.pallas_call(body, grid_spec=grid_spec, out_shape=jax.ShapeDtypeStruct(acc.shape, F32),
                          input_output_aliases={3: 0}, name="sum_partials")(place, partial, got, acc)


def _allreduce_small(vec, loss_row):
    rows = vec.shape[0]

    def body(v_ref, o_ref, slots, send_sems, recv_sems):
        x, y, c, _ = _place()
        me = 4 * x + 2 * y + c
        slots[me] = v_ref[...]
        copies = []
        for k in range(1, N_DEV):
            flip = lambda v, bit: 1 - v if bit else v
            peer = (flip(x, k & 4), flip(y, k & 2), flip(c, k & 1))
            copies.append(_remote(v_ref, slots.at[me], send_sems.at[k - 1], recv_sems.at[k - 1], peer))
        for cp in copies:
            cp.start()
        for k in range(1, N_DEV):
            flip = lambda v, bit: 1 - v if bit else v
            peer_id = 4 * flip(x, k & 4) + 2 * flip(y, k & 2) + flip(c, k & 1)
            _remote(v_ref, slots.at[peer_id], send_sems.at[k - 1], recv_sems.at[k - 1], (x, y, c)).wait_recv()
        for cp in copies:
            cp.wait_send()
        total = slots[0]
        for dev in range(1, N_DEV):
            total = total + slots[dev]
        o_ref[...] = total
        o_ref[loss_row:loss_row + 1, :] = jnp.broadcast_to(
            jnp.sum(total[loss_row:loss_row + 1, :], axis=-1, keepdims=True), (1, LANES))

    return pl.pallas_call(
        body, in_specs=[WHOLE_VMEM], out_specs=WHOLE_VMEM, out_shape=jax.ShapeDtypeStruct((rows, LANES), F32),
        scratch_shapes=[pltpu.VMEM((N_DEV, rows, LANES), F32), pltpu.SemaphoreType.DMA((N_DEV - 1,)),
                        pltpu.SemaphoreType.DMA((N_DEV - 1,))],
        name="allreduce_small")(vec)


def _adamw(w, g, m, v, *, tr, emit_grad=False):
    depth, rows, cols = w.shape
    assert rows % tr == 0
    c1 = float(np.float32(1.0 - ADAM_B1 ** ADAM_STEP))
    c2 = float(np.float32(1.0 - ADAM_B2 ** ADAM_STEP))

    def body(w_ref, g_ref, m_ref, v_ref, d_ref, mo_ref, vo_ref, *go_ref):
        g_t = g_ref[...]
        if emit_grad:
            go_ref[0][...] = g_t
        m_new = ADAM_B1 * m_ref[...] + (1.0 - ADAM_B1) * g_t
        v_new = ADAM_B2 * v_ref[...] + (1.0 - ADAM_B2) * (g_t * g_t)
        mo_ref[...] = m_new
        vo_ref[...] = v_new
        d_ref[...] = -ADAM_LR * ((m_new / c1) / (jnp.sqrt(v_new / c2) + ADAM_EPS) + ADAM_WD * w_ref[...])

    blk = pl.BlockSpec((None, tr, cols), lambda l, i: (l, i, 0))
    return pl.pallas_call(
        body, grid=(depth, rows // tr), in_specs=[blk] * 4, out_specs=[blk] * (4 if emit_grad else 3),
        out_shape=[jax.ShapeDtypeStruct(w.shape, F32)] * (4 if emit_grad else 3), name="adamw")(w, g, m, v)


def _local_step(x, positions, target, gains, exchange):
    t = x.shape[0]
    tm = 512
    inv_freq = ROPE_THETA ** (-jnp.arange(0, ROPE_DIM, 2, dtype=F32) / ROPE_DIM)
    lane = np.arange(LANES) % HEAD_DIM
    freq_row = jnp.where(lane < ROPE_DIM, inv_freq[lane % (ROPE_DIM // 2)], 0.0).astype(F32)[None, :]
    cos, sin = _rope_tables(positions.reshape(t, 1), freq_row)

    def hosted(tag, fn, *args, **kwargs):
        *out, got = fn(*args, comm=exchange.host(tag), **kwargs)
        if got is not None:
            exchange.hosted(tag, got)
        return out[0] if len(out) == 1 else out

    saved = []
    h1 = _norm_fwd(x, gains["pre_mix_norm"], 0, tm=tm)
    for l in range(DEPTH):
        proj = hosted(("fwd", l, "in_proj"), _in_proj, h1, exchange.weight("w_in", l), cos, sin, tm=tm)
        attn, lse = hosted(("fwd", l, "attn"), _attn_fwd, proj)
        conv_y = _conv_fwd(proj, exchange.weight("conv_w", l), l)
        merged = _merge_fwd(attn, conv_y, gains["attn_out_norm"], gains["conv_out_norm"], l, tm=tm)
        mix, x1, h2 = hosted(("fwd", l, "out_proj"), _mm_resnorm, merged, exchange.weight("w_out", l), x,
                             gains["post_mix_norm"], l, gains["pre_ffn_norm"], l, tm=tm, name="out_proj")
        g, u, act = hosted(("fwd", l, "gate_up"), _gate_up_swiglu, h2, exchange.weight("w_gate_up", l), tm=1024)
        nxt = (gains["pre_mix_norm"], l + 1) if l + 1 < DEPTH else (None, None)
        f, x2, h1_next = hosted(("fwd", l, "down"), _mm_resnorm, act, exchange.weight("w_down", l), x1,
                                gains["post_ffn_norm"], l, *nxt, tm=tm, name="down")
        saved.append(dict(x=x, h1=h1, proj=proj, attn=attn, lse=lse, conv_y=conv_y, merged=merged, mix=mix,
                          x1=x1, h2=h2, g=g, u=u, act=act, f=f))
        x, h1 = x2, h1_next

    dres, loss_lanes = _loss_fwd_bwd(x, target, tm=tm)

    g_gain = {k: [None] * DEPTH for k in gains}
    g_conv = [None] * DEPTH
    _, df, _, g_gain["post_ffn_norm"][DEPTH - 1], _ = _norm_bwd(
        dres, None, (saved[-1]["f"], gains["post_ffn_norm"], DEPTH - 1), tm=tm)
    for l in reversed(range(DEPTH)):
        sv = saved[l]
        w = {k: exchange.weight(k, l) for k in MATRIX_NAMES + ("conv_w",)}
        dg, du = _down_dx_swiglu_bwd(df, w["w_down"], sv["g"], sv["u"], tm=1024, tko=FFN // 2)
        g_down, _ = _mm_tn(sv["act"], df, 1, tka=256, name="down_dw")
        exchange.grads(l, "down", dict(w_down=g_down.reshape(N_CHIPS, FFN // N_CHIPS, D_MODEL)))
        dx1, dmix, g_gain["pre_ffn_norm"][l], g_gain["post_mix_norm"][l] = hosted(
            ("bwd", l, "gate_up_dx"), _gate_up_dx_norms, dg, du, w["w_gate_up"], dres,
            (sv["x1"], gains["pre_ffn_norm"], l), (sv["mix"], gains["post_mix_norm"], l), tm=tm)
        g_gate_up = hosted(("bwd", l, "gate_up_dw"), _mm_tn, sv["h2"], dg, N_CHIPS // 2, tka=512, name="gate_up_dw",
                           into=lax.empty(w["w_gate_up"].shape, F32))
        g_gate_up, _ = _mm_tn(sv["h2"], du, N_CHIPS // 2, tka=512, name="gate_up_dw", into=g_gate_up,
                              shard0=N_CHIPS // 2)
        exchange.grads(l, "gate_up", dict(w_gate_up=g_gate_up))
        d_merged = hosted(("bwd", l, "out_proj_dx"), _mm_nt, dmix, w["w_out"], tm=1024, tko=D_MODEL, name="out_proj_dx")
        g_out, _ = _mm_tn(sv["merged"], dmix, 1, tka=512, name="out_proj_dw")
        d_attn, delta, d_conv_y, g_gain["attn_out_norm"][l], g_gain["conv_out_norm"][l] = hosted(
            ("bwd", l, "merge"), _merge_bwd,
            d_merged, sv["attn"], sv["conv_y"], gains["attn_out_norm"], gains["conv_out_norm"], l, tm=tm)
        d_attn3 = hosted(("bwd", l, "attn"), _attn_bwd, sv["proj"], cos, sin, d_attn, sv["lse"], delta)
        d_conv3, g_conv[l] = _conv_bwd(sv["proj"], w["conv_w"], l, d_conv_y)
        g_in = _in_proj_dw(sv["h1"], d_attn3, d_conv3, N_CHIPS, tka=512)
        exchange.grads(l, "mix", dict(w_out=g_out.reshape(N_CHIPS, D_MODEL // N_CHIPS, D_MODEL), w_in=g_in))
        if l > 0:
            dres, df, g_gain["pre_mix_norm"][l], g_gain["post_ffn_norm"][l - 1] = hosted(
                ("bwd", l, "in_proj_dx"), _in_proj_dx_norms, d_attn3, d_conv3, w["w_in"], dx1,
                (sv["x"], gains["pre_mix_norm"], l), (saved[l - 1]["f"], gains["post_ffn_norm"], l - 1), tm=tm)
        else:
            dh1 = hosted(("bwd", l, "in_proj_dx"), _in_proj_dx, d_attn3, d_conv3, w["w_in"], tm=1024)
            dres, _, g_gain["pre_mix_norm"][l], _ = hosted(
                ("bwd", l, "norm_low"), _norm_bwd, dx1, (dh1, sv["x"], gains["pre_mix_norm"], l), None, tm=tm)

    g_gain = {k: jnp.concatenate(v, axis=0) for k, v in g_gain.items()}
    return loss_lanes, dres, g_gain, jnp.stack(g_conv, axis=0)


class _Exchange:
    GATHER_HOSTS = {"in_proj": (("w_out", 0),), "attn": (("w_gate_up", 0),), "gate_up": (("w_down", 0),),
                    "down": (("w_in", 1),)}

    @staticmethod
    def _reduce_hosts(group, l):
        if group == "down":
            return "gate_up_dw", "attn", l
        if group == "gate_up":
            return "merge", "attn", l
        if l > 0:
            return "in_proj_dx", "gate_up_dx", l - 1
        return "in_proj_dx", "norm_low", l

    def __init__(self, params, place):
        self.place = place
        self.slabs = {k: [_own_shard_slab(params[k], l, place, BF16) for l in range(DEPTH)] for k in MATRIX_NAMES}
        self.gathered = {k: [None] * DEPTH for k in MATRIX_NAMES}
        self.gathered["w_in"][0], self.conv_w = _run_comm(
            _gather_comm([self.slabs["w_in"][0]], _own_conv_slab(params["conv_w"], place)), "gather_first")
        self.full = {k: lax.empty(params[k].shape, F32) for k in MATRIX_NAMES}
        self.pending = {}
        self.raw = {}

    def weight(self, name, l):
        if name == "conv_w":
            return self.conv_w
        g = self.gathered[name][l]
        return g.reshape(1, g.shape[0] * g.shape[1], g.shape[2]) if name in ("w_out", "w_down") else g

    def host(self, tag):
        phase, l, kernel = tag
        if phase == "fwd":
            carried = [(name, l + ahead) for name, ahead in self.GATHER_HOSTS.get(kernel, ()) if l + ahead < DEPTH]
            return _gather_comm([self.slabs[name][layer] for name, layer in carried]) if carried else None
        if tag in self.pending:
            entries = self.pending[tag]
            arrays = [a for entry in entries for a in entry[3]]
            stages = {entry[0] for entry in entries}
            assert len(stages) == 1
            return _halves_comm(arrays) if stages == {"halves"} else _partials_comm(arrays)
        return None

    def hosted(self, tag, results):
        phase, l, kernel = tag
        if phase == "fwd":
            carried = [(name, l + ahead) for name, ahead in self.GATHER_HOSTS[kernel] if l + ahead < DEPTH]
            for (name, layer), slab in zip(carried, results):
                self.gathered[name][layer] = slab
            return
        results = list(results)
        for stage, gl, group, arrays in self.pending.pop(tag):
            mine, results = results[:len(arrays)], results[len(arrays):]
            names = list(self.raw[(gl, group)])
            if stage == "partials":
                self._finish_reduction(gl, names, arrays, mine)
                continue
            partials = [_add_halves(self.raw[(gl, group)][k], r, self.place) for k, r in zip(names, mine)]
            _, ici_kernel, ici_layer = self._reduce_hosts(group, gl)
            self.pending.setdefault(("bwd", ici_layer, ici_kernel), []).append(("partials", gl, group, partials))

    def grads(self, l, group, grads):
        self.raw[(l, group)] = grads
        self.pending.setdefault(("bwd", l, self._reduce_hosts(group, l)[0]), []).append(
            ("halves", l, group, [grads[k] for k in grads]))

    def _finish_reduction(self, l, names, partials, others):
        for k, p, q in zip(names, partials, others):
            self.full[k] = _sum_partials(p, q, self.place, self.full[k], l)

    def reduced(self):
        assert not self.pending
        return dict(zip(MATRIX_NAMES, _run_comm(_share_comm([self.full[k] for k in MATRIX_NAMES]), "share_halves")))


def kernel(x, positions, pre_mix_norm, w_in, conv_w, attn_out_norm, conv_out_norm, w_out, post_mix_norm, pre_ffn_norm, w_gate_up, w_down, post_ffn_norm, loss_target, m_pre_mix_norm, m_w_in, m_conv_w, m_attn_out_norm, m_conv_out_norm, m_w_out, m_post_mix_norm, m_pre_ffn_norm, m_w_gate_up, m_w_down, m_post_ffn_norm, v_pre_mix_norm, v_w_in, v_conv_w, v_attn_out_norm, v_conv_out_norm, v_w_out, v_post_mix_norm, v_pre_ffn_norm, v_w_gate_up, v_w_down, v_post_ffn_norm):
    params = dict(pre_mix_norm=pre_mix_norm, w_in=w_in, conv_w=conv_w, attn_out_norm=attn_out_norm,
                  conv_out_norm=conv_out_norm, w_out=w_out, post_mix_norm=post_mix_norm, pre_ffn_norm=pre_ffn_norm,
                  w_gate_up=w_gate_up, w_down=w_down, post_ffn_norm=post_ffn_norm)
    mom1 = dict(pre_mix_norm=m_pre_mix_norm, w_in=m_w_in, conv_w=m_conv_w, attn_out_norm=m_attn_out_norm,
                conv_out_norm=m_conv_out_norm, w_out=m_w_out, post_mix_norm=m_post_mix_norm,
                pre_ffn_norm=m_pre_ffn_norm, w_gate_up=m_w_gate_up, w_down=m_w_down, post_ffn_norm=m_post_ffn_norm)
    mom2 = dict(pre_mix_norm=v_pre_mix_norm, w_in=v_w_in, conv_w=v_conv_w, attn_out_norm=v_attn_out_norm,
                conv_out_norm=v_conv_out_norm, w_out=v_w_out, post_mix_norm=v_post_mix_norm,
                pre_ffn_norm=v_pre_ffn_norm, w_gate_up=v_w_gate_up, w_down=v_w_down, post_ffn_norm=v_post_ffn_norm)
    xi, yi, ci = lax.axis_index("x"), lax.axis_index("y"), lax.axis_index("c")
    place = jnp.stack([2 * xi + yi, ci]).astype(jnp.int32)

    exchange = _Exchange(params, place)
    gains = {k: params[k][:, None, :] for k in GAIN_NAMES}
    loss_lanes, grad_x, g_gain, g_conv = _local_step(x[0], positions[0], loss_target[0], gains, exchange)
    grad = exchange.reduced()

    small = [g_gain[k].reshape(-1) for k in GAIN_NAMES] + [g_conv.reshape(-1), loss_lanes.reshape(-1)]
    sizes = [int(s.shape[0]) for s in small]
    flat = jnp.concatenate(small)
    loss_row = (sum(sizes) - LANES) // LANES
    rows = -(-flat.shape[0] // (8 * LANES)) * 8
    flat = jnp.pad(flat, (0, rows * LANES - flat.shape[0])).reshape(rows, LANES)
    total = _allreduce_small(flat, loss_row).reshape(-1)
    offsets = np.cumsum([0] + sizes)
    for i, k in enumerate(GAIN_NAMES):
        grad[k] = total[offsets[i]:offsets[i + 1]].reshape(params[k].shape)
    conv_all = total[offsets[6]:offsets[7]].reshape(DEPTH, N_CHIPS, 3, LANES)
    grad["conv_w"] = lax.dynamic_index_in_dim(conv_all, 2 * xi + yi, axis=1, keepdims=False)
    loss = total[offsets[7]]

    delta, new_m, new_v = {}, {}, {}
    for k in WEIGHT_ORDER:
        shape = params[k].shape
        if k in MATRIX_NAMES:
            tr = {1024: 512, 704: 352, 256: 256}[shape[1]]
            delta[k], new_m[k], new_v[k], grad[k] = _adamw(params[k], grad[k], mom1[k], mom2[k], tr=tr, emit_grad=True)
        else:
            as3 = (lambda a: a) if len(shape) == 3 else (lambda a: a[None])
            d, m, v = _adamw(as3(params[k]), as3(grad[k]), as3(mom1[k]), as3(mom2[k]), tr=as3(params[k]).shape[1])
            delta[k], new_m[k], new_v[k] = d.reshape(shape), m.reshape(shape), v.reshape(shape)

    return (loss, grad_x[None], *[grad[k] for k in WEIGHT_ORDER], *[delta[k] for k in WEIGHT_ORDER],
            *[new_m[k] for k in WEIGHT_ORDER], *[new_v[k] for k in WEIGHT_ORDER])
```

```python
import functools
from typing import Callable, NamedTuple

import numpy as np
import jax
import jax.numpy as jnp
from jax import lax
from jax.experimental import pallas as pl
from jax.experimental.pallas import tpu as pltpu

F32 = jnp.float32
BF16 = jnp.bfloat16
MESH = pl.DeviceIdType.MESH

D_MODEL = 1024
ATTN_W = 512
CONV_W = 512
HEAD_DIM = 64
ROPE_DIM = 16
ROPE_THETA = 500000.0
FFN = 2816
DEPTH = 4
RMS_EPS = 1e-6
NEG_INF = -1e30
N_CHIPS = 4
N_DEV = 8
LANES = 128
BF16_ROWS = 16
DILATIONS = (1, 4, 16)
BAND = 64
TQ = 128
WIN = TQ + 2 * BAND
SCALE = HEAD_DIM ** -0.5

ADAM_LR = 0.001
ADAM_B1 = 0.9
ADAM_B2 = 0.999
ADAM_EPS = 1e-08
ADAM_WD = 0.01
ADAM_STEP = 10

GAIN_NAMES = ("pre_mix_norm", "attn_out_norm", "conv_out_norm", "post_mix_norm", "pre_ffn_norm", "post_ffn_norm")
MATRIX_NAMES = ("w_in", "w_out", "w_gate_up", "w_down")
WEIGHT_ORDER = ("pre_mix_norm", "w_in", "conv_w", "attn_out_norm", "conv_out_norm", "w_out", "post_mix_norm",
                "pre_ffn_norm", "w_gate_up", "w_down", "post_ffn_norm")

ANY = pl.BlockSpec(memory_space=pl.ANY)
WHOLE_VMEM = pl.BlockSpec(memory_space=pltpu.VMEM)
LANE_CONTRACT = (((1,), (1,)), ((), ()))
ROW_CONTRACT = (((0,), (0,)), ((), ()))
CHUNK = 256


def _const_spec(block, index):
    return pl.BlockSpec(block, lambda *_: index)


def _gain_spec(g3, l):
    return _const_spec((None, 1, g3.shape[-1]), (l, 0, 0))


class _Comm(NamedTuple):
    ins: tuple
    inouts: tuple
    out_shapes: tuple
    n_sems: int
    start: Callable
    finish: Callable


def _place():
    x, y, c = lax.axis_index("x"), lax.axis_index("y"), lax.axis_index("c")
    other_chips = [(1 - x, y), (x, 1 - y), (1 - x, 1 - y)]
    return x, y, c, other_chips


def _remote(src, dst, send_sem, recv_sem, to):
    return pltpu.make_async_remote_copy(src_ref=src, dst_ref=dst, send_sem=send_sem, recv_sem=recv_sem,
                                        device_id=to, device_id_type=MESH)


def _call(body, operands, *, name, grid, in_specs, out_specs, out_shape, scratch_shapes=(), aliases=None,
          comm=None):
    in_specs, out_specs, out_shape = list(in_specs), list(out_specs), list(out_shape)
    scratch_shapes = list(scratch_shapes)
    aliases = dict(aliases or {})
    if comm is None:
        out = pl.pallas_call(body, grid=grid, in_specs=in_specs, out_specs=out_specs, out_shape=out_shape,
                             scratch_shapes=scratch_shapes, input_output_aliases=aliases, name=name)(*operands)
        return list(out), None
    n_in, n_out, n_scr = len(in_specs), len(out_shape), len(scratch_shapes)
    n_ci, n_cio, n_co = len(comm.ins), len(comm.inouts), len(comm.out_shapes)

    def hosted(*refs):
        refs = list(refs)
        ins, c_ins = refs[:n_in], refs[n_in:n_in + n_ci]
        base = n_in + n_ci + n_cio
        outs = refs[base:base + n_out]
        c_io = refs[base + n_out:base + n_out + n_cio]
        c_out = refs[base + n_out + n_cio:base + n_out + n_cio + n_co]
        scr = refs[base + n_out + n_cio + n_co:]
        send_sems, recv_sems = scr[n_scr], scr[n_scr + 1]
        if grid:
            first = functools.reduce(jnp.logical_and, [pl.program_id(a) == 0 for a in range(len(grid))])
            last = functools.reduce(jnp.logical_and, [pl.program_id(a) == grid[a] - 1 for a in range(len(grid))])
            pl.when(first)(lambda: comm.start(c_ins, c_io, c_out, send_sems, recv_sems))
            body(*ins, *outs, *scr[:n_scr])
            pl.when(last)(lambda: comm.finish(c_ins, c_io, c_out, send_sems, recv_sems))
        else:
            comm.start(c_ins, c_io, c_out, send_sems, recv_sems)
            body(*ins, *outs, *scr[:n_scr])
            comm.finish(c_ins, c_io, c_out, send_sems, recv_sems)

    res = pl.pallas_call(
        hosted, grid=grid, in_specs=in_specs + [ANY] * (n_ci + n_cio), out_specs=out_specs + [ANY] * (n_cio + n_co),
        out_shape=out_shape + [jax.ShapeDtypeStruct(a.shape, a.dtype) for a in comm.inouts] + list(comm.out_shapes),
        input_output_aliases={**aliases, **{n_in + n_ci + i: n_out + i for i in range(n_cio)}},
        scratch_shapes=scratch_shapes + [pltpu.SemaphoreType.DMA((comm.n_sems,))] * 2,
        name=name)(*operands, *comm.ins, *comm.inouts)
    return list(res[:n_out]), list(res[n_out:])


def _run_comm(comm, name):
    return _call(lambda: None, [], name=name, grid=(), in_specs=[], out_specs=[], out_shape=[], comm=comm)[1]


def _row_half(ref, lead, core, rows, align):
    hr = rows // 2
    return ref.at[(*lead, pl.ds(pl.multiple_of(core * hr, align), hr), slice(None))]


def _gather_comm(slabs, conv_slab=None):
    n = len(slabs)
    n_conv = 0 if conv_slab is None else 3

    def direct(ios, send, recv):
        x, y, c, chips = _place()
        copies = []
        for a in range(n):
            own = _row_half(ios[a], (2 * x + y,), c, slabs[a].shape[1], BF16_ROWS)
            copies += [_remote(own, own, send.at[a * 3 + j], recv.at[a * 3 + j], (*chip, c))
                       for j, chip in enumerate(chips)]
        if conv_slab is not None:
            own = ios[n].at[:, 2 * x + y]
            copies += [_remote(own, own, send.at[6 * n + j], recv.at[6 * n + j], (*chip, c))
                       for j, chip in enumerate(chips)]
        return copies

    def start(ins, ios, outs, send, recv):
        for cp in direct(ios, send, recv):
            cp.start()

    def finish(ins, ios, outs, send, recv):
        x, y, c, chips = _place()
        sibling = (x, y, 1 - c)
        passed = []
        for a in range(n):
            for j, chip in enumerate(chips):
                landed = _row_half(ios[a], (2 * chip[0] + chip[1],), c, slabs[a].shape[1], BF16_ROWS)
                _remote(landed, landed, send.at[a * 3 + j], recv.at[a * 3 + j], (*chip, c)).wait_recv()
                fwd = _remote(landed, landed, send.at[3 * n + a * 3 + j], recv.at[3 * n + a * 3 + j], sibling)
                fwd.start()
                passed.append(fwd)
        if conv_slab is not None:
            for j, chip in enumerate(chips):
                landed = ios[n].at[:, 2 * chip[0] + chip[1]]
                _remote(landed, landed, send.at[6 * n + j], recv.at[6 * n + j], (*chip, c)).wait_recv()
        for a in range(n):
            for j, chip in enumerate(chips):
                landed = _row_half(ios[a], (2 * chip[0] + chip[1],), 1 - c, slabs[a].shape[1], BF16_ROWS)
                _remote(landed, landed, send.at[3 * n + a * 3 + j], recv.at[3 * n + a * 3 + j], sibling).wait_recv()
        for cp in direct(ios, send, recv) + passed:
            cp.wait_send()

    inouts = tuple(slabs) + (() if conv_slab is None else (conv_slab,))
    return _Comm((), inouts, (), 6 * n + n_conv, start, finish)


def _halves_comm(grads):
    n = len(grads)

    def copies(ins, outs, send, recv):
        x, y, c, _ = _place()
        return [_remote(_row_half(ins[a], (slice(None),), 1 - c, grads[a].shape[1], 8), outs[a],
                        send.at[a], recv.at[a], (x, y, 1 - c)) for a in range(n)]

    def start(ins, ios, outs, send, recv):
        for cp in copies(ins, outs, send, recv):
            cp.start()

    def finish(ins, ios, outs, send, recv):
        for cp in copies(ins, outs, send, recv):
            cp.wait()

    out_shapes = tuple(jax.ShapeDtypeStruct((g.shape[0], g.shape[1] // 2, g.shape[2]), F32) for g in grads)
    return _Comm(tuple(grads), (), out_shapes, n, start, finish)


def _partials_comm(partials):
    n = len(partials)

    def copies(ins, outs, send, recv):
        x, y, c, chips = _place()
        return [_remote(ins[a].at[2 * chip[0] + chip[1]], outs[a].at[k], send.at[a * 3 + k], recv.at[a * 3 + k],
                        (*chip, c)) for a in range(n) for k, chip in enumerate(chips)]

    def start(ins, ios, outs, send, recv):
        for cp in copies(ins, outs, send, recv):
            cp.start()

    def finish(ins, ios, outs, send, recv):
        for cp in copies(ins, outs, send, recv):
            cp.wait()

    out_shapes = tuple(jax.ShapeDtypeStruct((3,) + p.shape[1:], BF16) for p in partials)
    return _Comm(tuple(partials), (), out_shapes, 3 * n, start, finish)


def _share_comm(grads):
    n = len(grads)

    def start(ins, ios, outs, send, recv):
        x, y, c, _ = _place()
        for a in range(n):
            mine = _row_half(ios[a], (slice(None),), c, grads[a].shape[1], 8)
            _remote(mine, mine, send.at[a], recv.at[a], (x, y, 1 - c)).start()

    def finish(ins, ios, outs, send, recv):
        x, y, c, _ = _place()
        for a in range(n):
            theirs = _row_half(ios[a], (slice(None),), 1 - c, grads[a].shape[1], 8)
            _remote(theirs, theirs, send.at[a], recv.at[a], (x, y, 1 - c)).wait()

    return _Comm((), tuple(grads), (), n, start, finish)


def _rms_fwd(x, g):
    r = lax.rsqrt(jnp.mean(x * x, axis=-1, keepdims=True) + RMS_EPS)
    return (x * r) * g


def _rms_bwd(x, g, dy):
    r = lax.rsqrt(jnp.mean(x * x, axis=-1, keepdims=True) + RMS_EPS)
    xh = x * r
    u = dy * g
    dx = r * (u - xh * jnp.mean(xh * u, axis=-1, keepdims=True))
    return dx, jnp.sum(dy * xh, axis=0, keepdims=True)


def _accumulate(ref, value, first):
    @pl.when(first)
    def _():
        ref[...] = value

    @pl.when(jnp.logical_not(first))
    def _():
        ref[...] += value


def _rope_coeffs(cos, sin):
    m = lax.broadcasted_iota(jnp.int32, cos.shape, 1) % HEAD_DIM
    a = jnp.where(m < ROPE_DIM, cos, 1.0)
    b = jnp.where(m < ROPE_DIM // 2, -sin, 0.0)
    c = jnp.where((m >= ROPE_DIM // 2) & (m < ROPE_DIM), sin, 0.0)
    return a, b, c


def _rope_apply(t, cos, sin):
    a, b, c = _rope_coeffs(cos, sin)
    n = t.shape[1]
    return a * t + b * pltpu.roll(t, n - ROPE_DIM // 2, 1) + c * pltpu.roll(t, ROPE_DIM // 2, 1)


def _rope_transpose(dt, cos, sin):
    a, b, c = _rope_coeffs(cos, sin)
    n = dt.shape[1]
    return a * dt + pltpu.roll(b * dt, ROPE_DIM // 2, 1) + pltpu.roll(c * dt, n - ROPE_DIM // 2, 1)


def _in_proj(h, w, cos, sin, *, tm, comm=None):
    t, k = h.shape
    s_n, _, n = w.shape
    assert t % tm == 0 and n % LANES == 0

    rotary_shards = -(-2 * ATTN_W // n)
    rotary = list(range(0, 2 * ATTN_W, LANES))
    per_shard = -(-len(rotary) // max(s_n - rotary_shards, 1))

    def body(h_ref, w_ref, cos_ref, sin_ref, o_ref):
        def shard(s):
            o_ref[:, s * n:(s + 1) * n] = jnp.dot(h_ref[...], w_ref[s], preferred_element_type=F32)

        def rope(c0):
            cols = slice(c0, c0 + LANES)
            o_ref[:, cols] = _rope_apply(o_ref[:, cols], cos_ref[...], sin_ref[...])

        for s in range(rotary_shards):
            shard(s)
        pending = list(rotary)
        for s in range(rotary_shards, s_n):
            for c0 in pending[:per_shard]:
                rope(c0)
            pending = pending[per_shard:]
            shard(s)
        for c0 in pending:
            rope(c0)

    lane_tile = pl.BlockSpec((tm, LANES), lambda i: (i, 0))
    out, got = _call(
        body, [h, w, cos, sin], name="in_proj", grid=(t // tm,),
        in_specs=[pl.BlockSpec((tm, k), lambda i: (i, 0)), _const_spec(w.shape, (0, 0, 0)), lane_tile, lane_tile],
        out_specs=[pl.BlockSpec((tm, s_n * n), lambda i: (i, 0))],
        out_shape=[jax.ShapeDtypeStruct((t, s_n * n), F32)], comm=comm)
    return out[0], got


def _mm_nt(a, w, *, tm, tko, name, comm=None):
    t, sn = a.shape
    s_n, ko, n = w.shape
    assert sn == s_n * n and t % tm == 0 and ko % tko == 0

    def body(a_ref, w_ref, o_ref):
        acc = lax.dot_general(a_ref[...], w_ref[...], (((1,), (1,)), ((), ())), preferred_element_type=F32)
        if s_n == 1:
            o_ref[...] = acc
        else:
            _accumulate(o_ref, acc, pl.program_id(2) == 0)

    out, got = _call(
        body, [a, w], name=name, grid=(t // tm, ko // tko, s_n),
        in_specs=[pl.BlockSpec((tm, n), lambda i, j, s: (i, s)),
                  pl.BlockSpec((None, tko, n), lambda i, j, s: (s, j, 0))],
        out_specs=[pl.BlockSpec((tm, tko), lambda i, j, s: (i, j))],
        out_shape=[jax.ShapeDtypeStruct((t, ko), F32)], comm=comm)
    return out[0], got


def _dx_through_norms(operands, in_specs, dx_rows, dres, pre, post, *, tm, name, comm=None):
    t, d_model = dres.shape
    (x, gx3, lx), (y, gy3, ly) = pre, post
    n_op = len(operands)
    row = pl.BlockSpec((tm, d_model), lambda i: (i, 0))
    gsum = _const_spec((1, d_model), (0, 0))

    def body(*refs):
        d_ref, x_ref, gx_ref, y_ref, gy_ref, dn_ref, dgx_ref, dy_ref, dgy_ref = refs[n_op:]
        first = pl.program_id(0) == 0
        dx, dgx = _rms_bwd(x_ref[...], gx_ref[...], dx_rows(refs[:n_op]))
        d_new = d_ref[...] + dx
        dn_ref[...] = d_new
        _accumulate(dgx_ref, dgx, first)
        dy, dgy = _rms_bwd(y_ref[...], gy_ref[...], d_new)
        dy_ref[...] = dy.astype(BF16)
        _accumulate(dgy_ref, dgy, first)

    out, got = _call(
        body, list(operands) + [dres, x, gx3, y, gy3], name=name, grid=(t // tm,),
        in_specs=list(in_specs) + [row, row, _gain_spec(gx3, lx), row, _gain_spec(gy3, ly)],
        out_specs=[row, gsum, row, gsum],
        out_shape=[jax.ShapeDtypeStruct((t, d_model), F32), jax.ShapeDtypeStruct((1, d_model), F32),
                   jax.ShapeDtypeStruct((t, d_model), BF16), jax.ShapeDtypeStruct((1, d_model), F32)],
        comm=comm)
    return out[0], out[2], out[1], out[3], got


def _gate_up_dx_norms(dg, du, w, dres, pre, post, *, tm, comm=None):
    s_n, ko, n = w.shape
    half = s_n // 2

    def dx_rows(refs):
        dg_ref, du_ref, w_ref = refs
        acc = jnp.zeros((tm, ko), F32)
        for s in range(half):
            cols = slice(s * n, (s + 1) * n)
            acc = acc + lax.dot_general(dg_ref[:, cols], w_ref[s], LANE_CONTRACT, preferred_element_type=F32)
            acc = acc + lax.dot_general(du_ref[:, cols], w_ref[half + s], LANE_CONTRACT, preferred_element_type=F32)
        return acc

    a_spec = pl.BlockSpec((tm, half * n), lambda i: (i, 0))
    return _dx_through_norms([dg, du, w], [a_spec, a_spec, _const_spec(w.shape, (0, 0, 0))], dx_rows, dres, pre, post,
                             tm=tm, name="gate_up_dx", comm=comm)


def _in_proj_dx_norms(d_attn3, d_conv3, w, dres, pre, post, *, tm, comm=None):
    s_n, ko, n = w.shape
    per = n // CHUNK
    pieces = d_attn3.shape[0]
    width = d_attn3.shape[2]

    def dx_rows(refs):
        a_ref, b_ref, w_ref = refs
        acc = jnp.zeros((tm, ko), F32)
        for c in range(s_n * per):
            src = a_ref if c // 2 < pieces else b_ref
            piece, c0 = (c // 2) % pieces, (c % 2) * CHUNK
            acc = acc + lax.dot_general(src[piece, :, c0:c0 + CHUNK], w_ref[c // per, :, (c % per) * CHUNK:(c % per + 1) * CHUNK],
                                        LANE_CONTRACT, preferred_element_type=F32)
        return acc

    stack = pl.BlockSpec((pieces, tm, width), lambda i: (0, i, 0))
    return _dx_through_norms([d_attn3, d_conv3, w], [stack, stack, _const_spec(w.shape, (0, 0, 0))], dx_rows, dres, pre,
                             post, tm=tm, name="in_proj_dx", comm=comm)


def _mm_tn(a, b, s_n, *, tka, name, into=None, shard0=0, comm=None):
    t, ka = a.shape
    n = b.shape[1] // s_n
    assert b.shape[0] == t and ka % tka == 0

    def body(a_ref, b_ref, *rest):
        rest[-1][...] = lax.dot_general(a_ref[...], b_ref[...], ROW_CONTRACT, preferred_element_type=F32)

    operands, in_specs, aliases = [a, b], [pl.BlockSpec((t, tka), lambda i, s: (0, i)),
                                           pl.BlockSpec((t, n), lambda i, s: (0, s))], {}
    out_shape = jax.ShapeDtypeStruct((s_n, ka, n), F32)
    if into is not None:
        operands, in_specs, aliases = operands + [into], in_specs + [ANY], {2: 0}
        out_shape = jax.ShapeDtypeStruct(into.shape, F32)
    out, got = _call(body, operands, name=name, grid=(ka // tka, s_n), in_specs=in_specs,
                     out_specs=[pl.BlockSpec((None, tka, n), lambda i, s: (shard0 + s, i, 0))],
                     out_shape=[out_shape], aliases=aliases, comm=comm)
    return out[0], got


def _gate_up_swiglu(h, w, *, tm, comm=None):
    t, k = h.shape
    s_n, _, n = w.shape
    half = s_n // 2

    def body(h_ref, wg_ref, wu_ref, g_ref, u_ref, a_ref):
        g = jnp.dot(h_ref[...], wg_ref[...], preferred_element_type=F32)
        u = jnp.dot(h_ref[...], wu_ref[...], preferred_element_type=F32)
        g_ref[...] = g.astype(BF16)
        u_ref[...] = u.astype(BF16)
        a_ref[...] = (g * jax.nn.sigmoid(g) * u).astype(BF16)

    col = pl.BlockSpec((tm, n), lambda i, j: (i, j))
    out, got = _call(
        body, [h, w, w], name="gate_up", grid=(t // tm, half),
        in_specs=[pl.BlockSpec((tm, k), lambda i, j: (i, 0)), pl.BlockSpec((None, k, n), lambda i, j: (j, 0, 0)),
                  pl.BlockSpec((None, k, n), lambda i, j: (half + j, 0, 0))],
        out_specs=[col, col, col],
        out_shape=[jax.ShapeDtypeStruct((t, half * n), BF16)] * 3, comm=comm)
    return out[0], out[1], out[2], got


def _down_dx_swiglu_bwd(df, w, g, u, *, tm, tko):
    t, k = df.shape
    _, ko, _ = w.shape
    assert t % tm == 0 and ko % tko == 0

    def body(df_ref, w_ref, g_ref, u_ref, dg_ref, du_ref):
        d = lax.dot_general(df_ref[...], w_ref[...], LANE_CONTRACT, preferred_element_type=F32)
        gg = g_ref[...].astype(F32)
        sig = jax.nn.sigmoid(gg)
        dg_ref[...] = (d * u_ref[...].astype(F32) * (sig * (1.0 + gg * (1.0 - sig)))).astype(BF16)
        du_ref[...] = (d * (gg * sig)).astype(BF16)

    col = pl.BlockSpec((tm, tko), lambda i, j: (i, j))
    return pl.pallas_call(
        body, grid=(t // tm, ko // tko),
        in_specs=[pl.BlockSpec((tm, k), lambda i, j: (i, 0)), pl.BlockSpec((None, tko, k), lambda i, j: (0, j, 0)),
                  col, col],
        out_specs=[col, col], out_shape=[jax.ShapeDtypeStruct((t, ko), BF16)] * 2, name="down_dx")(df, w, g, u)


def _in_proj_dx(d_attn3, d_conv3, w, *, tm, comm=None):
    _, t, _ = d_attn3.shape
    s_n, ko, n = w.shape
    half, per = s_n // 2, n // CHUNK
    assert t % tm == 0

    def body(*refs):
        a_refs, b_refs, wa_ref, wb_ref, o_ref = refs[:per], refs[per:2 * per], refs[2 * per], refs[2 * per + 1], refs[-1]
        acc = jnp.zeros(o_ref.shape, F32)
        for r in range(per):
            cols = slice(r * CHUNK, (r + 1) * CHUNK)
            acc = acc + lax.dot_general(a_refs[r][...], wa_ref[:, cols], LANE_CONTRACT, preferred_element_type=F32)
            acc = acc + lax.dot_general(b_refs[r][...], wb_ref[:, cols], LANE_CONTRACT, preferred_element_type=F32)
        _accumulate(o_ref, acc, pl.program_id(1) == 0)

    piece = lambda r: pl.BlockSpec((None, tm, CHUNK), lambda i, s: ((per * s + r) // 2, i, (per * s + r) % 2))
    out, got = _call(
        body, [d_attn3] * per + [d_conv3] * per + [w, w], name="in_proj_dx", grid=(t // tm, half),
        in_specs=[piece(r) for r in range(per)] * 2
        + [pl.BlockSpec((None, ko, n), lambda i, s: (s, 0, 0)), pl.BlockSpec((None, ko, n), lambda i, s: (half + s, 0, 0))],
        out_specs=[pl.BlockSpec((tm, ko), lambda i, s: (i, 0))],
        out_shape=[jax.ShapeDtypeStruct((t, ko), F32)], comm=comm)
    return out[0], got


def _in_proj_dw(h, d_attn3, d_conv3, s_n, *, tka):
    t, ka = h.shape
    half = s_n // 2
    n = 3 * d_attn3.shape[2] // half
    per = n // CHUNK
    assert ka % tka == 0

    def body(*refs):
        h_ref, o_ref = refs[0], refs[-1]
        for side in range(2):
            for r in range(per):
                o_ref[side, :, r * CHUNK:(r + 1) * CHUNK] = lax.dot_general(
                    h_ref[...], refs[1 + side * per + r][...], ROW_CONTRACT, preferred_element_type=F32)

    piece = lambda r: pl.BlockSpec((None, t, CHUNK), lambda i, s: ((per * s + r) // 2, 0, (per * s + r) % 2))
    out = pl.pallas_call(
        body, grid=(ka // tka, half),
        in_specs=[pl.BlockSpec((t, tka), lambda i, s: (0, i))] + [piece(r) for r in range(per)] * 2,
        out_specs=pl.BlockSpec((2, None, tka, n), lambda i, s: (0, s, i, 0)),
        out_shape=jax.ShapeDtypeStruct((2, half, ka, n), F32), name="in_proj_dw")(h, *[d_attn3] * per, *[d_conv3] * per)
    return out.reshape(s_n, ka, n)


def _rope_tables(positions_col, inv_freq_row):
    t = positions_col.shape[0]

    def body(pos_ref, f_ref, cos_ref, sin_ref):
        ang = pos_ref[...].astype(F32) * f_ref[...]
        cos_ref[...] = jnp.cos(ang)
        sin_ref[...] = jnp.sin(ang)

    return pl.pallas_call(
        body, out_shape=[jax.ShapeDtypeStruct((t, LANES), F32)] * 2, name="rope_tables")(positions_col, inv_freq_row)


def _norm_fwd(x, g3, l, *, tm):
    t, w = x.shape

    def body(x_ref, g_ref, h_ref):
        h_ref[...] = _rms_fwd(x_ref[...], g_ref[...]).astype(BF16)

    return pl.pallas_call(
        body, grid=(t // tm,),
        in_specs=[pl.BlockSpec((tm, w), lambda i: (i, 0)), _gain_spec(g3, l)],
        out_specs=pl.BlockSpec((tm, w), lambda i: (i, 0)),
        out_shape=jax.ShapeDtypeStruct((t, w), BF16), name="norm_fwd")(x, g3)


def _mm_resnorm(a, w, x, g_post3, l_post, g_next3, l_next, *, tm, name, comm=None):
    t, k = a.shape
    _, _, n = w.shape
    with_next = g_next3 is not None
    row = pl.BlockSpec((tm, n), lambda i: (i, 0))

    def body(a_ref, w_ref, x_ref, gp_ref, *rest):
        y = jnp.dot(a_ref[...], w_ref[...], preferred_element_type=F32)
        x_new = x_ref[...] + _rms_fwd(y, gp_ref[...])
        if with_next:
            gn_ref, y_ref, xo_ref, h_ref = rest
            h_ref[...] = _rms_fwd(x_new, gn_ref[...]).astype(BF16)
        else:
            y_ref, xo_ref = rest
        y_ref[...] = y
        xo_ref[...] = x_new

    ins = [a, w, x, g_post3] + ([g_next3] if with_next else [])
    in_specs = ([pl.BlockSpec((tm, k), lambda i: (i, 0)), _const_spec((None, k, n), (0, 0, 0)), row,
                 _gain_spec(g_post3, l_post)] + ([_gain_spec(g_next3, l_next)] if with_next else []))
    out_shape = [jax.ShapeDtypeStruct((t, n), F32)] * 2 + ([jax.ShapeDtypeStruct((t, n), BF16)] if with_next else [])
    out, got = _call(body, ins, name=name, grid=(t // tm,), in_specs=in_specs, out_specs=[row] * len(out_shape),
                     out_shape=out_shape, comm=comm)
    return out[0], out[1], (out[2] if with_next else None), got


def _conv_fwd(proj, conv_w, l):
    t = proj.shape[0]
    col0 = 3 * ATTN_W // LANES

    def body(u_ref, gb_ref, gc_ref, w_ref, y_ref):
        c = gc_ref[...] * u_ref[...]
        row = lax.broadcasted_iota(jnp.int32, c.shape, 0)
        c_prev = jnp.where(row == 0, 0.0, pltpu.roll(c, 1, 0))
        c_next = jnp.where(row == t - 1, 0.0, pltpu.roll(c, t - 1, 0))
        w = w_ref[...]
        y_ref[...] = gb_ref[...] * (w[0:1] * c_prev + w[1:2] * c + w[2:3] * c_next)

    nj = CONV_W // LANES
    cols = lambda base: pl.BlockSpec((t, LANES), lambda j: (0, base + j))
    return pl.pallas_call(
        body, grid=(nj,),
        in_specs=[cols(col0), cols(col0 + nj), cols(col0 + 2 * nj),
                  pl.BlockSpec((None, None, 3, LANES), lambda j: (l, j, 0, 0))],
        out_specs=pl.BlockSpec((t, LANES), lambda j: (0, j)),
        out_shape=jax.ShapeDtypeStruct((t, CONV_W), F32), name="conv_fwd")(proj, proj, proj, conv_w)


def _merge_out_proj(attn, conv_y, ga3, gc3, w, x, g_post3, g_next3, l, *, tm, comm=None):
    t = attn.shape[0]
    _, k, n = w.shape
    half = pl.BlockSpec((tm, ATTN_W), lambda i: (i, 0))
    row = pl.BlockSpec((tm, n), lambda i: (i, 0))

    def body(a_ref, c_ref, ga_ref, gc_ref, w_ref, x_ref, gp_ref, gn_ref, m_ref, y_ref, xo_ref, h_ref):
        m_ref[:, :ATTN_W] = _rms_fwd(a_ref[...], ga_ref[...]).astype(BF16)
        m_ref[:, ATTN_W:] = _rms_fwd(c_ref[...], gc_ref[...]).astype(BF16)
        y = jnp.dot(m_ref[...], w_ref[...], preferred_element_type=F32)
        x_new = x_ref[...] + _rms_fwd(y, gp_ref[...])
        h_ref[...] = _rms_fwd(x_new, gn_ref[...]).astype(BF16)
        y_ref[...] = y
        xo_ref[...] = x_new

    out, got = _call(
        body, [attn, conv_y, ga3, gc3, w, x, g_post3, g_next3], name="out_proj", grid=(t // tm,),
        in_specs=[half, half, _gain_spec(ga3, l), _gain_spec(gc3, l), _const_spec((None, k, n), (0, 0, 0)), row,
                  _gain_spec(g_post3, l), _gain_spec(g_next3, l)],
        out_specs=[pl.BlockSpec((tm, k), lambda i: (i, 0)), row, row, row],
        out_shape=[jax.ShapeDtypeStruct((t, k), BF16), jax.ShapeDtypeStruct((t, n), F32),
                   jax.ShapeDtypeStruct((t, n), F32), jax.ShapeDtypeStruct((t, n), BF16)], comm=comm)
    return out[0], out[1], out[2], out[3], got


def _loss_fwd_bwd(y, target, *, tm):
    t, w = y.shape
    row = pl.BlockSpec((tm, w), lambda i: (i, 0))

    def body(y_ref, t_ref, dy_ref, loss_ref):
        e = y_ref[...] - t_ref[...]
        dy_ref[...] = e * (1.0 / w)
        sq = jnp.sum(e * e, axis=0, keepdims=True) * (0.5 / w)
        part = sq[:, :LANES]
        for j in range(1, w // LANES):
            part = part + sq[:, j * LANES:(j + 1) * LANES]
        _accumulate(loss_ref, part, pl.program_id(0) == 0)

    return pl.pallas_call(
        body, grid=(t // tm,), in_specs=[row, row],
        out_specs=[row, _const_spec((1, LANES), (0, 0))],
        out_shape=[jax.ShapeDtypeStruct((t, w), F32), jax.ShapeDtypeStruct((1, LANES), F32)], name="loss")(y, target)


def _tile_rows(t, nt, lb, d):
    r = t // nt
    q0 = (t % nt) * TQ
    m0 = jnp.clip(q0 - BAND, 0, lb - WIN)
    if d == 1:
        return pl.ds(pl.multiple_of(q0, TQ), TQ), pl.ds(pl.multiple_of(m0, BAND), WIN), m0 - q0
    return pl.ds(r + d * q0, TQ, stride=d), pl.ds(r + d * m0, WIN, stride=d), m0 - q0


def _for_row_chunks(t, fn, chunk=512):
    def step(i, carry):
        fn(pl.ds(pl.multiple_of(i * chunk, chunk), chunk))
        return carry

    lax.fori_loop(0, t // chunk, step, 0)


WINDOW_OFFSETS = (-BAND, 0, -2 * BAND)


def _fill_band_bias(bias_ref):
    rel0 = (lax.broadcasted_iota(jnp.int32, (2 * TQ, WIN), 1)
            - lax.broadcasted_iota(jnp.int32, (2 * TQ, WIN), 0) % TQ)
    for j, off in enumerate(WINDOW_OFFSETS):
        rel = rel0 + off
        bias_ref[j] = jnp.where((rel >= -BAND) & (rel <= BAND), 0.0, NEG_INF)


def _fill_sequence_bias(bias_ref):
    rel = (lax.broadcasted_iota(jnp.int32, (2 * WIN, WIN), 1) - lax.broadcasted_iota(jnp.int32, (2 * WIN, WIN), 0) % WIN)
    bias_ref[...] = jnp.where((rel >= -BAND) & (rel <= BAND), 0.0, NEG_INF)


def _band_bias(bias_ref, off):
    return bias_ref[jnp.where(off == WINDOW_OFFSETS[0], 0, jnp.where(off == WINDOW_OFFSETS[1], 1, 2))]


def _stack_heads(a, first_head):
    return jnp.concatenate([jnp.where(first_head, a, 0.0), jnp.where(first_head, 0.0, a)], axis=0)


def _unstack_heads(a2, first_head):
    n = a2.shape[0] // 2
    return jnp.where(first_head, a2[:n], a2[n:])


def _attn_fwd(proj, comm=None):
    t = proj.shape[0]
    npair = ATTN_W // LANES

    def body(q_ref, k_ref, v_ref, o_ref, lse_ref, o1, o2, l0, l1, l2, m1, m2, bias, bias_seq):
        _fill_band_bias(bias)
        _fill_sequence_bias(bias_seq)
        outs, dens, maxs = (o_ref, o1, o2), (l0, l1, l2), (lse_ref, m1, m2)

        def softmax_tile(b, qrows, krows, n_q, band_bias):
            first_head = lax.broadcasted_iota(jnp.int32, (n_q, LANES), 1) < HEAD_DIM
            q2 = _stack_heads(q_ref[qrows, :] * SCALE, first_head).astype(BF16)
            kw = k_ref[krows, :].astype(BF16)
            vw = jnp.concatenate([v_ref[krows, :].astype(BF16), jnp.ones((WIN, LANES), BF16)], axis=1)
            s = lax.dot_general(q2, kw, LANE_CONTRACT, preferred_element_type=F32) + band_bias
            m = jnp.max(s, axis=-1, keepdims=True)
            pv = jnp.dot(jnp.exp(s - m).astype(BF16), vw, preferred_element_type=F32)
            outs[b][qrows, :] = _unstack_heads(pv[:, :LANES], first_head)
            dens[b][qrows, :] = _unstack_heads(pv[:, LANES:], first_head)
            maxs[b][qrows, :] = _unstack_heads(jnp.broadcast_to(m, (2 * n_q, LANES)), first_head)

        for b, d in enumerate(DILATIONS):
            lb = t // d
            if lb == WIN:
                def sequence(r, carry, b=b, d=d):
                    rows = pl.ds(r, WIN, stride=d)
                    softmax_tile(b, rows, rows, WIN, bias_seq[...])
                    return carry

                lax.fori_loop(0, d, sequence, 0, unroll=8)
                continue
            nt = lb // TQ

            def tile(ti, carry, b=b, d=d, lb=lb, nt=nt):
                qrows, krows, off = _tile_rows(ti, nt, lb, d)
                softmax_tile(b, qrows, krows, TQ, _band_bias(bias, off))
                return carry

            lax.fori_loop(0, d * nt, tile, 0, unroll=16)

        def finish(rows):
            ms = [m_b[rows, :] for m_b in maxs]
            m_all = jnp.maximum(jnp.maximum(ms[0], ms[1]), ms[2])
            ws = [jnp.exp(m_b - m_all) for m_b in ms]
            den = ws[0] * dens[0][rows, :] + ws[1] * dens[1][rows, :] + ws[2] * dens[2][rows, :]
            num = ws[0] * outs[0][rows, :] + ws[1] * outs[1][rows, :] + ws[2] * outs[2][rows, :]
            o_ref[rows, :] = num / den
            lse_ref[rows, :] = m_all + jnp.log(den)

        _for_row_chunks(t, finish, 256)

    cols = lambda base: pl.BlockSpec((t, LANES), lambda g: (0, base + g))
    out, got = _call(
        body, [proj, proj, proj], name="attn_fwd", grid=(npair,),
        in_specs=[cols(0), cols(npair), cols(2 * npair)],
        out_specs=[cols(0), cols(0)],
        out_shape=[jax.ShapeDtypeStruct((t, ATTN_W), F32)] * 2,
        scratch_shapes=[pltpu.VMEM((t, LANES), F32)] * 7 + [pltpu.VMEM((len(WINDOW_OFFSETS), 2 * TQ, WIN), F32),
                                                            pltpu.VMEM((2 * WIN, WIN), F32)],
        comm=comm)
    return out[0], out[1], got


def _attn_bwd(proj, cos, sin, d_attn, lse, delta, comm=None):
    t = proj.shape[0]
    npair = ATTN_W // LANES

    def body(q_ref, k_ref, v_ref, cos_ref, sin_ref, do_ref, l_ref, dl_ref, dqkv_ref,
             dq_acc, dk_acc, dv_acc, bias, bias_seq):
        _fill_band_bias(bias)
        _fill_sequence_bias(bias_seq)
        dq_acc[...] = jnp.zeros(dq_acc.shape, F32)
        dk_acc[...] = jnp.zeros(dk_acc.shape, F32)
        dv_acc[...] = jnp.zeros(dv_acc.shape, F32)
        def stack_column(a):
            return jnp.concatenate([a[:, 0:1], a[:, HEAD_DIM:HEAD_DIM + 1]], axis=0)

        def grad_tile(qrows, krows, n_q, band_bias):
            first_head = lax.broadcasted_iota(jnp.int32, (n_q, LANES), 1) < HEAD_DIM
            q2 = _stack_heads(q_ref[qrows, :] * SCALE, first_head).astype(BF16)
            do2 = _stack_heads(do_ref[qrows, :], first_head).astype(BF16)
            kw = k_ref[krows, :].astype(BF16)
            vw = v_ref[krows, :].astype(BF16)
            s = lax.dot_general(q2, kw, LANE_CONTRACT, preferred_element_type=F32) + band_bias
            p = jnp.exp(s - stack_column(l_ref[qrows, :]))
            dp = lax.dot_general(do2, vw, LANE_CONTRACT, preferred_element_type=F32)
            ds = (p * (dp - stack_column(dl_ref[qrows, :]))).astype(BF16)
            dq2 = jnp.dot(ds, kw, preferred_element_type=F32)
            dq_acc[qrows, :] += _unstack_heads(dq2, first_head) * SCALE
            dk_acc[krows, :] += lax.dot_general(ds, q2, ROW_CONTRACT, preferred_element_type=F32)
            dv_acc[krows, :] += lax.dot_general(p.astype(BF16), do2, ROW_CONTRACT, preferred_element_type=F32)

        for d in DILATIONS:
            lb = t // d
            if lb == WIN:
                def sequence(r, carry, d=d):
                    rows = pl.ds(r, WIN, stride=d)
                    grad_tile(rows, rows, WIN, bias_seq[...])
                    return carry

                lax.fori_loop(0, d, sequence, 0, unroll=4)
                continue
            nt = lb // TQ

            def tile(ti, carry, d=d, lb=lb, nt=nt):
                qrows, krows, off = _tile_rows(ti, nt, lb, d)
                grad_tile(qrows, krows, TQ, _band_bias(bias, off))
                return carry

            lax.fori_loop(0, d * nt, tile, 0, unroll=8)

        def finish(rows):
            dqkv_ref[0, rows, :] = _rope_transpose(dq_acc[rows, :], cos_ref[rows, :], sin_ref[rows, :]).astype(BF16)
            dqkv_ref[1, rows, :] = _rope_transpose(dk_acc[rows, :], cos_ref[rows, :], sin_ref[rows, :]).astype(BF16)
            dqkv_ref[2, rows, :] = dv_acc[rows, :].astype(BF16)

        _for_row_chunks(t, finish)

    cols = lambda base: pl.BlockSpec((t, LANES), lambda g: (0, base + g))
    out, got = _call(
        body, [proj, proj, proj, cos, sin, d_attn, lse, delta], name="attn_bwd", grid=(npair,),
        in_specs=[cols(0), cols(npair), cols(2 * npair), WHOLE_VMEM, WHOLE_VMEM, cols(0), cols(0), cols(0)],
        out_specs=[pl.BlockSpec((3, t, LANES), lambda g: (0, 0, g))],
        out_shape=[jax.ShapeDtypeStruct((3, t, ATTN_W), BF16)],
        scratch_shapes=[pltpu.VMEM((t, LANES), F32)] * 3 + [pltpu.VMEM((len(WINDOW_OFFSETS), 2 * TQ, WIN), F32),
                                                            pltpu.VMEM((2 * WIN, WIN), F32)],
        comm=comm)
    return out[0], got


def _norm_bwd(dres, pre, post, *, tm, comm=None):
    t, w = dres.shape
    row = pl.BlockSpec((tm, w), lambda i: (i, 0))
    gsum = _const_spec((1, w), (0, 0))
    ins, in_specs, out_shape, out_specs = [dres], [row], [], []
    if pre is not None:
        dh, x, g3, l = pre
        ins += [dh, x, g3]
        in_specs += [row, row, _gain_spec(g3, l)]
        out_shape += [jax.ShapeDtypeStruct((t, w), F32), jax.ShapeDtypeStruct((1, w), F32)]
        out_specs += [row, gsum]
    if post is not None:
        y, g3, l = post
        ins += [y, g3]
        in_specs += [row, _gain_spec(g3, l)]
        out_shape += [jax.ShapeDtypeStruct((t, w), BF16), jax.ShapeDtypeStruct((1, w), F32)]
        out_specs += [row, gsum]
    n_in = len(ins)

    def body(*refs):
        first = pl.program_id(0) == 0
        ins_r, outs_r = list(refs[:n_in]), list(refs[n_in:])
        d = ins_r.pop(0)[...]
        if pre is not None:
            dh_ref, x_ref, g_ref = ins_r[:3]
            ins_r = ins_r[3:]
            dx, dg = _rms_bwd(x_ref[...], g_ref[...], dh_ref[...])
            d = d + dx
            outs_r.pop(0)[...] = d
            _accumulate(outs_r.pop(0), dg, first)
        if post is not None:
            y_ref, g_ref = ins_r
            dy, dg = _rms_bwd(y_ref[...], g_ref[...], d)
            outs_r.pop(0)[...] = dy.astype(BF16)
            _accumulate(outs_r.pop(0), dg, first)

    out, got = _call(body, ins, name="norm_bwd", grid=(t // tm,), in_specs=in_specs, out_specs=out_specs,
                     out_shape=out_shape, comm=comm)
    d_new, dg_pre = (out.pop(0), out.pop(0)) if pre is not None else (None, None)
    dy, dg_post = (out.pop(0), out.pop(0)) if post is not None else (None, None)
    return d_new, dy, dg_pre, dg_post, got


def _merge_bwd(d_merged, attn, conv_y, ga3, gc3, l, *, tm, comm=None):
    t = attn.shape[0]
    row = pl.BlockSpec((tm, ATTN_W), lambda i: (i, 0))
    gsum = _const_spec((1, ATTN_W), (0, 0))

    def body(dma_ref, dmc_ref, a_ref, c_ref, ga_ref, gc_ref, da_ref, dl_ref, dc_ref, dga_ref, dgc_ref):
        first = pl.program_id(0) == 0
        attn_t = a_ref[...]
        da, dga = _rms_bwd(attn_t, ga_ref[...], dma_ref[...])
        dc, dgc = _rms_bwd(c_ref[...], gc_ref[...], dmc_ref[...])
        da_ref[...] = da
        dc_ref[...] = dc
        same_head = (lax.broadcasted_iota(jnp.int32, (ATTN_W, ATTN_W), 0) // HEAD_DIM
                     == lax.broadcasted_iota(jnp.int32, (ATTN_W, ATTN_W), 1) // HEAD_DIM).astype(BF16)
        rest = da * attn_t
        total = jnp.zeros(rest.shape, F32)
        for _ in range(3):
            term = rest.astype(BF16)
            total = total + jnp.dot(term, same_head, preferred_element_type=F32)
            rest = rest - term.astype(F32)
        dl_ref[...] = total
        _accumulate(dga_ref, dga, first)
        _accumulate(dgc_ref, dgc, first)

    out, got = _call(
        body, [d_merged, d_merged, attn, conv_y, ga3, gc3], name="merge_bwd", grid=(t // tm,),
        in_specs=[pl.BlockSpec((tm, ATTN_W), lambda i: (i, 0)), pl.BlockSpec((tm, CONV_W), lambda i: (i, 1)),
                  row, row, _gain_spec(ga3, l), _gain_spec(gc3, l)],
        out_specs=[row, row, row, gsum, gsum],
        out_shape=[jax.ShapeDtypeStruct((t, ATTN_W), F32)] * 3 + [jax.ShapeDtypeStruct((1, ATTN_W), F32)] * 2,
        comm=comm)
    return (*out, got)


def _conv_bwd(proj, conv_w, l, d_conv_y):
    t = proj.shape[0]
    col0 = 3 * ATTN_W // LANES
    nj = CONV_W // LANES

    def body(u_ref, gb_ref, gc_ref, w_ref, dy_ref, d3_ref, dw_ref):
        u, gc, dy = u_ref[...], gc_ref[...], dy_ref[...]
        row = lax.broadcasted_iota(jnp.int32, u.shape, 0)
        down = lambda a: jnp.where(row == 0, 0.0, pltpu.roll(a, 1, 0))
        up = lambda a: jnp.where(row == t - 1, 0.0, pltpu.roll(a, t - 1, 0))
        w = w_ref[...]
        c = gc * u
        c_prev, c_next = down(c), up(c)
        d3_ref[1] = (dy * (w[0:1] * c_prev + w[1:2] * c + w[2:3] * c_next)).astype(BF16)
        dz = dy * gb_ref[...]
        dc = w[0:1] * up(dz) + w[1:2] * dz + w[2:3] * down(dz)
        d3_ref[0] = (dc * gc).astype(BF16)
        d3_ref[2] = (dc * u).astype(BF16)
        dw_ref[0:1, :] = jnp.sum(dz * c_prev, axis=0, keepdims=True)
        dw_ref[1:2, :] = jnp.sum(dz * c, axis=0, keepdims=True)
        dw_ref[2:3, :] = jnp.sum(dz * c_next, axis=0, keepdims=True)

    cols = lambda base: pl.BlockSpec((t, LANES), lambda j: (0, base + j))
    return pl.pallas_call(
        body, grid=(nj,),
        in_specs=[cols(col0), cols(col0 + nj), cols(col0 + 2 * nj),
                  pl.BlockSpec((None, None, 3, LANES), lambda j: (l, j, 0, 0)), cols(0)],
        out_specs=[pl.BlockSpec((3, t, LANES), lambda j: (0, 0, j)), pl.BlockSpec((None, 3, LANES), lambda j: (j, 0, 0))],
        out_shape=[jax.ShapeDtypeStruct((3, t, CONV_W), BF16), jax.ShapeDtypeStruct((nj, 3, LANES), F32)],
        name="conv_bwd")(proj, proj, proj, conv_w, d_conv_y)


def _own_shard_slab(w, l, place, dtype):
    _, rows, cols = w.shape
    tr = rows if rows <= 704 else 512
    assert rows % tr == 0

    def body(p_ref, w_ref, o_ref):
        del p_ref
        o_ref[...] = w_ref[...].astype(dtype)

    grid_spec = pltpu.PrefetchScalarGridSpec(
        num_scalar_prefetch=1, grid=(rows // tr,),
        in_specs=[pl.BlockSpec((None, tr, cols), lambda i, p: (l, i, 0))],
        out_specs=pl.BlockSpec((None, tr, cols), lambda i, p: (p[0], i, 0)))
    return pl.pallas_call(body, grid_spec=grid_spec, name="own_shard_slab",
                          out_shape=jax.ShapeDtypeStruct((N_CHIPS, rows, cols), dtype))(place, w)


def _own_conv_slab(w, place):
    depth = w.shape[0]

    def body(p_ref, w_ref, o_ref):
        del p_ref
        o_ref[...] = w_ref[...]

    grid_spec = pltpu.PrefetchScalarGridSpec(
        num_scalar_prefetch=1, grid=(depth,),
        in_specs=[pl.BlockSpec((None, 3, LANES), lambda l, p: (l, 0, 0))],
        out_specs=pl.BlockSpec((None, None, 3, LANES), lambda l, p: (l, p[0], 0, 0)))
    return pl.pallas_call(body, grid_spec=grid_spec, name="own_conv_slab",
                          out_shape=jax.ShapeDtypeStruct((depth, N_CHIPS, 3, LANES), F32))(place, w)


def _add_halves(grad, got, place):
    s_n, rows, cols = grad.shape
    hr = rows // 2

    def body(p_ref, g_ref, r_ref, o_ref):
        del p_ref
        o_ref[...] = (g_ref[...] + r_ref[...]).astype(BF16)

    grid_spec = pltpu.PrefetchScalarGridSpec(
        num_scalar_prefetch=1, grid=(s_n,),
        in_specs=[pl.BlockSpec((None, hr, cols), lambda s, p: (s, p[1], 0)),
                  pl.BlockSpec((None, hr, cols), lambda s, p: (s, 0, 0))],
        out_specs=pl.BlockSpec((None, hr, cols), lambda s, p: (s, 0, 0)))
    return pl.pallas_call(body, grid_spec=grid_spec, out_shape=jax.ShapeDtypeStruct((s_n, hr, cols), BF16),
                          name="add_halves")(place, grad, got)


def _sum_partials(partial, got, place, acc, l):
    _, hr, cols = partial.shape

    def body(p_ref, mine_ref, got_ref, acc_ref, o_ref):
        del p_ref, acc_ref
        total = mine_ref[...].astype(F32)
        for k in range(3):
            total = total + got_ref[k].astype(F32)
        o_ref[...] = total

    grid_spec = pltpu.PrefetchScalarGridSpec(
        num_scalar_prefetch=1, grid=(1,),
        in_specs=[pl.BlockSpec((None, hr, cols), lambda i, p: (p[0], 0, 0)),
                  pl.BlockSpec((3, hr, cols), lambda i, p: (0, 0, 0)), ANY],
        out_specs=pl.BlockSpec((None, hr, cols), lambda i, p: (l, p[1], 0)))
    return pl.pallas_call(body, grid_spec=grid_spec, out_shape=jax.ShapeDtypeStruct(acc.shape, F32),
                          input_output_aliases={3: 0}, name="sum_partials")(place, partial, got, acc)


def _allreduce_small(vec, loss_row):
    rows = vec.shape[0]

    def body(v_ref, o_ref, slots, send_sems, recv_sems):
        x, y, c, _ = _place()
        me = 4 * x + 2 * y + c
        slots[me] = v_ref[...]
        copies = []
        for k in range(1, N_DEV):
            flip = lambda v, bit: 1 - v if bit else v
            peer = (flip(x, k & 4), flip(y, k & 2), flip(c, k & 1))
            copies.append(_remote(v_ref, slots.at[me], send_sems.at[k - 1], recv_sems.at[k - 1], peer))
        for cp in copies:
            cp.start()
        for k in range(1, N_DEV):
            flip = lambda v, bit: 1 - v if bit else v
            peer_id = 4 * flip(x, k & 4) + 2 * flip(y, k & 2) + flip(c, k & 1)
            _remote(v_ref, slots.at[peer_id], send_sems.at[k - 1], recv_sems.at[k - 1], (x, y, c)).wait_recv()
        for cp in copies:
            cp.wait_send()
        total = slots[0]
        for dev in range(1, N_DEV):
            total = total + slots[dev]
        o_ref[...] = total
        o_ref[loss_row:loss_row + 1, :] = jnp.broadcast_to(
            jnp.sum(total[loss_row:loss_row + 1, :], axis=-1, keepdims=True), (1, LANES))

    return pl.pallas_call(
        body, in_specs=[WHOLE_VMEM], out_specs=WHOLE_VMEM, out_shape=jax.ShapeDtypeStruct((rows, LANES), F32),
        scratch_shapes=[pltpu.VMEM((N_DEV, rows, LANES), F32), pltpu.SemaphoreType.DMA((N_DEV - 1,)),
                        pltpu.SemaphoreType.DMA((N_DEV - 1,))],
        name="allreduce_small")(vec)


def _adamw(w, g, m, v, *, tr, emit_grad=False):
    depth, rows, cols = w.shape
    assert rows % tr == 0
    c1 = float(np.float32(1.0 - ADAM_B1 ** ADAM_STEP))
    c2 = float(np.float32(1.0 - ADAM_B2 ** ADAM_STEP))

    def body(w_ref, g_ref, m_ref, v_ref, d_ref, mo_ref, vo_ref, *go_ref):
        g_t = g_ref[...]
        if emit_grad:
            go_ref[0][...] = g_t
        m_new = ADAM_B1 * m_ref[...] + (1.0 - ADAM_B1) * g_t
        v_new = ADAM_B2 * v_ref[...] + (1.0 - ADAM_B2) * (g_t * g_t)
        mo_ref[...] = m_new
        vo_ref[...] = v_new
        d_ref[...] = -ADAM_LR * ((m_new / c1) / (jnp.sqrt(v_new / c2) + ADAM_EPS) + ADAM_WD * w_ref[...])

    blk = pl.BlockSpec((None, tr, cols), lambda l, i: (l, i, 0))
    return pl.pallas_call(
        body, grid=(depth, rows // tr), in_specs=[blk] * 4, out_specs=[blk] * (4 if emit_grad else 3),
        out_shape=[jax.ShapeDtypeStruct(w.shape, F32)] * (4 if emit_grad else 3), name="adamw")(w, g, m, v)


def _local_step(x, positions, target, gains, exchange):
    t = x.shape[0]
    tm = 512
    inv_freq = ROPE_THETA ** (-jnp.arange(0, ROPE_DIM, 2, dtype=F32) / ROPE_DIM)
    lane = np.arange(LANES) % HEAD_DIM
    freq_row = jnp.where(lane < ROPE_DIM, inv_freq[lane % (ROPE_DIM // 2)], 0.0).astype(F32)[None, :]
    cos, sin = _rope_tables(positions.reshape(t, 1), freq_row)

    def hosted(tag, fn, *args, **kwargs):
        *out, got = fn(*args, comm=exchange.host(tag), **kwargs)
        if got is not None:
            exchange.hosted(tag, got)
        return out[0] if len(out) == 1 else out

    saved = []
    h1 = _norm_fwd(x, gains["pre_mix_norm"], 0, tm=tm)
    for l in range(DEPTH):
        proj = hosted(("fwd", l, "in_proj"), _in_proj, h1, exchange.weight("w_in", l), cos, sin, tm=tm)
        attn, lse = hosted(("fwd", l, "attn"), _attn_fwd, proj)
        conv_y = _conv_fwd(proj, exchange.weight("conv_w", l), l)
        merged, mix, x1, h2 = hosted(
            ("fwd", l, "out_proj"), _merge_out_proj, attn, conv_y, gains["attn_out_norm"], gains["conv_out_norm"],
            exchange.weight("w_out", l), x, gains["post_mix_norm"], gains["pre_ffn_norm"], l, tm=tm)
        g, u, act = hosted(("fwd", l, "gate_up"), _gate_up_swiglu, h2, exchange.weight("w_gate_up", l), tm=1024)
        nxt = (gains["pre_mix_norm"], l + 1) if l + 1 < DEPTH else (None, None)
        f, x2, h1_next = hosted(("fwd", l, "down"), _mm_resnorm, act, exchange.weight("w_down", l), x1,
                                gains["post_ffn_norm"], l, *nxt, tm=tm, name="down")
        saved.append(dict(x=x, h1=h1, proj=proj, attn=attn, lse=lse, conv_y=conv_y, merged=merged, mix=mix,
                          x1=x1, h2=h2, g=g, u=u, act=act, f=f))
        x, h1 = x2, h1_next

    dres, loss_lanes = _loss_fwd_bwd(x, target, tm=tm)

    g_gain = {k: [None] * DEPTH for k in gains}
    g_conv = [None] * DEPTH
    _, df, _, g_gain["post_ffn_norm"][DEPTH - 1], _ = _norm_bwd(
        dres, None, (saved[-1]["f"], gains["post_ffn_norm"], DEPTH - 1), tm=tm)
    for l in reversed(range(DEPTH)):
        sv = saved[l]
        w = {k: exchange.weight(k, l) for k in MATRIX_NAMES + ("conv_w",)}
        dg, du = _down_dx_swiglu_bwd(df, w["w_down"], sv["g"], sv["u"], tm=1024, tko=FFN // 2)
        g_down, _ = _mm_tn(sv["act"], df, 1, tka=256, name="down_dw")
        exchange.grads(l, "down", dict(w_down=g_down.reshape(N_CHIPS, FFN // N_CHIPS, D_MODEL)))
        dx1, dmix, g_gain["pre_ffn_norm"][l], g_gain["post_mix_norm"][l] = hosted(
            ("bwd", l, "gate_up_dx"), _gate_up_dx_norms, dg, du, w["w_gate_up"], dres,
            (sv["x1"], gains["pre_ffn_norm"], l), (sv["mix"], gains["post_mix_norm"], l), tm=tm)
        g_gate_up = hosted(("bwd", l, "gate_up_dw"), _mm_tn, sv["h2"], dg, N_CHIPS // 2, tka=512, name="gate_up_dw",
                           into=lax.empty(w["w_gate_up"].shape, F32))
        g_gate_up, _ = _mm_tn(sv["h2"], du, N_CHIPS // 2, tka=512, name="gate_up_dw", into=g_gate_up,
                              shard0=N_CHIPS // 2)
        exchange.grads(l, "gate_up", dict(w_gate_up=g_gate_up))
        d_merged = hosted(("bwd", l, "out_proj_dx"), _mm_nt, dmix, w["w_out"], tm=1024, tko=D_MODEL, name="out_proj_dx")
        g_out, _ = _mm_tn(sv["merged"], dmix, 1, tka=512, name="out_proj_dw")
        d_attn, delta, d_conv_y, g_gain["attn_out_norm"][l], g_gain["conv_out_norm"][l] = hosted(
            ("bwd", l, "merge"), _merge_bwd,
            d_merged, sv["attn"], sv["conv_y"], gains["attn_out_norm"], gains["conv_out_norm"], l, tm=tm)
        d_attn3 = hosted(("bwd", l, "attn"), _attn_bwd, sv["proj"], cos, sin, d_attn, sv["lse"], delta)
        d_conv3, g_conv[l] = _conv_bwd(sv["proj"], w["conv_w"], l, d_conv_y)
        g_in = _in_proj_dw(sv["h1"], d_attn3, d_conv3, N_CHIPS, tka=512)
        exchange.grads(l, "mix", dict(w_out=g_out.reshape(N_CHIPS, D_MODEL // N_CHIPS, D_MODEL), w_in=g_in))
        if l > 0:
            dres, df, g_gain["pre_mix_norm"][l], g_gain["post_ffn_norm"][l - 1] = hosted(
                ("bwd", l, "in_proj_dx"), _in_proj_dx_norms, d_attn3, d_conv3, w["w_in"], dx1,
                (sv["x"], gains["pre_mix_norm"], l), (saved[l - 1]["f"], gains["post_ffn_norm"], l - 1), tm=tm)
        else:
            dh1 = hosted(("bwd", l, "in_proj_dx"), _in_proj_dx, d_attn3, d_conv3, w["w_in"], tm=1024)
            dres, _, g_gain["pre_mix_norm"][l], _ = hosted(
                ("bwd", l, "norm_low"), _norm_bwd, dx1, (dh1, sv["x"], gains["pre_mix_norm"], l), None, tm=tm)

    g_gain = {k: jnp.concatenate(v, axis=0) for k, v in g_gain.items()}
    return loss_lanes, dres, g_gain, jnp.stack(g_conv, axis=0)


class _Exchange:
    GATHER_HOSTS = {"in_proj": (("w_out", 0),), "attn": (("w_gate_up", 0),), "gate_up": (("w_down", 0),),
                    "down": (("w_in", 1),)}

    @staticmethod
    def _reduce_hosts(group, l):
        if group == "down":
            return "gate_up_dw", "attn", l
        if group == "gate_up":
            return "merge", "attn", l
        if l > 0:
            return "in_proj_dx", "gate_up_dx", l - 1
        return "in_proj_dx", "norm_low", l

    def __init__(self, params, place):
        self.place = place
        self.slabs = {k: [_own_shard_slab(params[k], l, place, BF16) for l in range(DEPTH)] for k in MATRIX_NAMES}
        self.gathered = {k: [None] * DEPTH for k in MATRIX_NAMES}
        self.gathered["w_in"][0], self.conv_w = _run_comm(
            _gather_comm([self.slabs["w_in"][0]], _own_conv_slab(params["conv_w"], place)), "gather_first")
        self.full = {k: lax.empty(params[k].shape, F32) for k in MATRIX_NAMES}
        self.pending = {}
        self.raw = {}

    def weight(self, name, l):
        if name == "conv_w":
            return self.conv_w
        g = self.gathered[name][l]
        return g.reshape(1, g.shape[0] * g.shape[1], g.shape[2]) if name in ("w_out", "w_down") else g

    def host(self, tag):
        phase, l, kernel = tag
        if phase == "fwd":
            carried = [(name, l + ahead) for name, ahead in self.GATHER_HOSTS.get(kernel, ()) if l + ahead < DEPTH]
            return _gather_comm([self.slabs[name][layer] for name, layer in carried]) if carried else None
        if tag in self.pending:
            entries = self.pending[tag]
            arrays = [a for entry in entries for a in entry[3]]
            stages = {entry[0] for entry in entries}
            assert len(stages) == 1
            return _halves_comm(arrays) if stages == {"halves"} else _partials_comm(arrays)
        return None

    def hosted(self, tag, results):
        phase, l, kernel = tag
        if phase == "fwd":
            carried = [(name, l + ahead) for name, ahead in self.GATHER_HOSTS[kernel] if l + ahead < DEPTH]
            for (name, layer), slab in zip(carried, results):
                self.gathered[name][layer] = slab
            return
        results = list(results)
        for stage, gl, group, arrays in self.pending.pop(tag):
            mine, results = results[:len(arrays)], results[len(arrays):]
            names = list(self.raw[(gl, group)])
            if stage == "partials":
                self._finish_reduction(gl, names, arrays, mine)
                continue
            partials = [_add_halves(self.raw[(gl, group)][k], r, self.place) for k, r in zip(names, mine)]
            _, ici_kernel, ici_layer = self._reduce_hosts(group, gl)
            self.pending.setdefault(("bwd", ici_layer, ici_kernel), []).append(("partials", gl, group, partials))

    def grads(self, l, group, grads):
        self.raw[(l, group)] = grads
        self.pending.setdefault(("bwd", l, self._reduce_hosts(group, l)[0]), []).append(
            ("halves", l, group, [grads[k] for k in grads]))

    def _finish_reduction(self, l, names, partials, others):
        for k, p, q in zip(names, partials, others):
            self.full[k] = _sum_partials(p, q, self.place, self.full[k], l)

    def reduced(self):
        assert not self.pending
        return dict(zip(MATRIX_NAMES, _run_comm(_share_comm([self.full[k] for k in MATRIX_NAMES]), "share_halves")))


def kernel(x, positions, pre_mix_norm, w_in, conv_w, attn_out_norm, conv_out_norm, w_out, post_mix_norm, pre_ffn_norm, w_gate_up, w_down, post_ffn_norm, loss_target, m_pre_mix_norm, m_w_in, m_conv_w, m_attn_out_norm, m_conv_out_norm, m_w_out, m_post_mix_norm, m_pre_ffn_norm, m_w_gate_up, m_w_down, m_post_ffn_norm, v_pre_mix_norm, v_w_in, v_conv_w, v_attn_out_norm, v_conv_out_norm, v_w_out, v_post_mix_norm, v_pre_ffn_norm, v_w_gate_up, v_w_down, v_post_ffn_norm):
    params = dict(pre_mix_norm=pre_mix_norm, w_in=w_in, conv_w=conv_w, attn_out_norm=attn_out_norm,
                  conv_out_norm=conv_out_norm, w_out=w_out, post_mix_norm=post_mix_norm, pre_ffn_norm=pre_ffn_norm,
                  w_gate_up=w_gate_up, w_down=w_down, post_ffn_norm=post_ffn_norm)
    mom1 = dict(pre_mix_norm=m_pre_mix_norm, w_in=m_w_in, conv_w=m_conv_w, attn_out_norm=m_attn_out_norm,
                conv_out_norm=m_conv_out_norm, w_out=m_w_out, post_mix_norm=m_post_mix_norm,
                pre_ffn_norm=m_pre_ffn_norm, w_gate_up=m_w_gate_up, w_down=m_w_down, post_ffn_norm=m_post_ffn_norm)
    mom2 = dict(pre_mix_norm=v_pre_mix_norm, w_in=v_w_in, conv_w=v_conv_w, attn_out_norm=v_attn_out_norm,
                conv_out_norm=v_conv_out_norm, w_out=v_w_out, post_mix_norm=v_post_mix_norm,
                pre_ffn_norm=v_pre_ffn_norm, w_gate_up=v_w_gate_up, w_down=v_w_down, post_ffn_norm=v_post_ffn_norm)
    xi, yi, ci = lax.axis_index("x"), lax.axis_index("y"), lax.axis_index("c")
    place = jnp.stack([2 * xi + yi, ci]).astype(jnp.int32)

    exchange = _Exchange(params, place)
    gains = {k: params[k][:, None, :] for k in GAIN_NAMES}
    loss_lanes, grad_x, g_gain, g_conv = _local_step(x[0], positions[0], loss_target[0], gains, exchange)
    grad = exchange.reduced()

    small = [g_gain[k].reshape(-1) for k in GAIN_NAMES] + [g_conv.reshape(-1), loss_lanes.reshape(-1)]
    sizes = [int(s.shape[0]) for s in small]
    flat = jnp.concatenate(small)
    loss_row = (sum(sizes) - LANES) // LANES
    rows = -(-flat.shape[0] // (8 * LANES)) * 8
    flat = jnp.pad(flat, (0, rows * LANES - flat.shape[0])).reshape(rows, LANES)
    total = _allreduce_small(flat, loss_row).reshape(-1)
    offsets = np.cumsum([0] + sizes)
    for i, k in enumerate(GAIN_NAMES):
        grad[k] = total[offsets[i]:offsets[i + 1]].reshape(params[k].shape)
    conv_all = total[offsets[6]:offsets[7]].reshape(DEPTH, N_CHIPS, 3, LANES)
    grad["conv_w"] = lax.dynamic_index_in_dim(conv_all, 2 * xi + yi, axis=1, keepdims=False)
    loss = total[offsets[7]]

    delta, new_m, new_v = {}, {}, {}
    for k in WEIGHT_ORDER:
        shape = params[k].shape
        if k in MATRIX_NAMES:
            tr = {1024: 512, 704: 352, 256: 256}[shape[1]]
            delta[k], new_m[k], new_v[k], grad[k] = _adamw(params[k], grad[k], mom1[k], mom2[k], tr=tr, emit_grad=True)
        else:
            as3 = (lambda a: a) if len(shape) == 3 else (lambda a: a[None])
            d, m, v = _adamw(as3(params[k]), as3(grad[k]), as3(mom1[k]), as3(mom2[k]), tr=as3(params[k]).shape[1])
            delta[k], new_m[k], new_v[k] = d.reshape(shape), m.reshape(shape), v.reshape(shape)

    return (loss, grad_x[None], *[grad[k] for k in WEIGHT_ORDER], *[delta[k] for k in WEIGHT_ORDER],
            *[new_m[k] for k in WEIGHT_ORDER], *[new_v[k] for k in WEIGHT_ORDER])
```

```python
import functools
from typing import Callable, NamedTuple

import numpy as np
import jax
import jax.numpy as jnp
from jax import lax
from jax.experimental import pallas as pl
from jax.experimental.pallas import tpu as pltpu

F32 = jnp.float32
BF16 = jnp.bfloat16
MESH = pl.DeviceIdType.MESH

D_MODEL = 1024
ATTN_W = 512
CONV_W = 512
HEAD_DIM = 64
ROPE_DIM = 16
ROPE_THETA = 500000.0
FFN = 2816
DEPTH = 4
RMS_EPS = 1e-6
NEG_INF = -1e30
N_CHIPS = 4
N_DEV = 8
LANES = 128
BF16_ROWS = 16
DILATIONS = (1, 4, 16)
BAND = 64
TQ = 128
WIN = TQ + 2 * BAND
SCALE = HEAD_DIM ** -0.5

ADAM_LR = 0.001
ADAM_B1 = 0.9
ADAM_B2 = 0.999
ADAM_EPS = 1e-08
ADAM_WD = 0.01
ADAM_STEP = 10

GAIN_NAMES = ("pre_mix_norm", "attn_out_norm", "conv_out_norm", "post_mix_norm", "pre_ffn_norm", "post_ffn_norm")
MATRIX_NAMES = ("w_in", "w_out", "w_gate_up", "w_down")
WEIGHT_ORDER = ("pre_mix_norm", "w_in", "conv_w", "attn_out_norm", "conv_out_norm", "w_out", "post_mix_norm",
                "pre_ffn_norm", "w_gate_up", "w_down", "post_ffn_norm")

ANY = pl.BlockSpec(memory_space=pl.ANY)
WHOLE_VMEM = pl.BlockSpec(memory_space=pltpu.VMEM)
LANE_CONTRACT = (((1,), (1,)), ((), ()))
ROW_CONTRACT = (((0,), (0,)), ((), ()))
CHUNK = 256


def _const_spec(block, index):
    return pl.BlockSpec(block, lambda *_: index)


def _gain_spec(g3, l):
    return _const_spec((None, 1, g3.shape[-1]), (l, 0, 0))


class _Comm(NamedTuple):
    ins: tuple
    inouts: tuple
    out_shapes: tuple
    n_sems: int
    start: Callable
    finish: Callable


def _place():
    x, y, c = lax.axis_index("x"), lax.axis_index("y"), lax.axis_index("c")
    other_chips = [(1 - x, y), (x, 1 - y), (1 - x, 1 - y)]
    return x, y, c, other_chips


def _remote(src, dst, send_sem, recv_sem, to):
    return pltpu.make_async_remote_copy(src_ref=src, dst_ref=dst, send_sem=send_sem, recv_sem=recv_sem,
                                        device_id=to, device_id_type=MESH)


def _call(body, operands, *, name, grid, in_specs, out_specs, out_shape, scratch_shapes=(), aliases=None,
          comm=None):
    in_specs, out_specs, out_shape = list(in_specs), list(out_specs), list(out_shape)
    scratch_shapes = list(scratch_shapes)
    aliases = dict(aliases or {})
    if comm is None:
        out = pl.pallas_call(body, grid=grid, in_specs=in_specs, out_specs=out_specs, out_shape=out_shape,
                             scratch_shapes=scratch_shapes, input_output_aliases=aliases, name=name)(*operands)
        return list(out), None
    n_in, n_out, n_scr = len(in_specs), len(out_shape), len(scratch_shapes)
    n_ci, n_cio, n_co = len(comm.ins), len(comm.inouts), len(comm.out_shapes)

    def hosted(*refs):
        refs = list(refs)
        ins, c_ins = refs[:n_in], refs[n_in:n_in + n_ci]
        base = n_in + n_ci + n_cio
        outs = refs[base:base + n_out]
        c_io = refs[base + n_out:base + n_out + n_cio]
        c_out = refs[base + n_out + n_cio:base + n_out + n_cio + n_co]
        scr = refs[base + n_out + n_cio + n_co:]
        send_sems, recv_sems = scr[n_scr], scr[n_scr + 1]
        if grid:
            first = functools.reduce(jnp.logical_and, [pl.program_id(a) == 0 for a in range(len(grid))])
            last = functools.reduce(jnp.logical_and, [pl.program_id(a) == grid[a] - 1 for a in range(len(grid))])
            pl.when(first)(lambda: comm.start(c_ins, c_io, c_out, send_sems, recv_sems))
            body(*ins, *outs, *scr[:n_scr])
            pl.when(last)(lambda: comm.finish(c_ins, c_io, c_out, send_sems, recv_sems))
        else:
            comm.start(c_ins, c_io, c_out, send_sems, recv_sems)
            body(*ins, *outs, *scr[:n_scr])
            comm.finish(c_ins, c_io, c_out, send_sems, recv_sems)

    res = pl.pallas_call(
        hosted, grid=grid, in_specs=in_specs + [ANY] * (n_ci + n_cio), out_specs=out_specs + [ANY] * (n_cio + n_co),
        out_shape=out_shape + [jax.ShapeDtypeStruct(a.shape, a.dtype) for a in comm.inouts] + list(comm.out_shapes),
        input_output_aliases={**aliases, **{n_in + n_ci + i: n_out + i for i in range(n_cio)}},
        scratch_shapes=scratch_shapes + [pltpu.SemaphoreType.DMA((comm.n_sems,))] * 2,
        name=name)(*operands, *comm.ins, *comm.inouts)
    return list(res[:n_out]), list(res[n_out:])


def _run_comm(comm, name):
    return _call(lambda: None, [], name=name, grid=(), in_specs=[], out_specs=[], out_shape=[], comm=comm)[1]


def _row_half(ref, lead, core, rows, align):
    hr = rows // 2
    return ref.at[(*lead, pl.ds(pl.multiple_of(core * hr, align), hr), slice(None))]


def _gather_comm(slabs, conv_slab=None):
    n = len(slabs)
    n_conv = 0 if conv_slab is None else 3

    def direct(ios, send, recv):
        x, y, c, chips = _place()
        copies = []
        for a in range(n):
            own = _row_half(ios[a], (2 * x + y,), c, slabs[a].shape[1], BF16_ROWS)
            copies += [_remote(own, own, send.at[a * 3 + j], recv.at[a * 3 + j], (*chip, c))
                       for j, chip in enumerate(chips)]
        if conv_slab is not None:
            own = ios[n].at[:, 2 * x + y]
            copies += [_remote(own, own, send.at[6 * n + j], recv.at[6 * n + j], (*chip, c))
                       for j, chip in enumerate(chips)]
        return copies

    def start(ins, ios, outs, send, recv):
        for cp in direct(ios, send, recv):
            cp.start()

    def finish(ins, ios, outs, send, recv):
        x, y, c, chips = _place()
        sibling = (x, y, 1 - c)
        passed = []
        for a in range(n):
            for j, chip in enumerate(chips):
                landed = _row_half(ios[a], (2 * chip[0] + chip[1],), c, slabs[a].shape[1], BF16_ROWS)
                _remote(landed, landed, send.at[a * 3 + j], recv.at[a * 3 + j], (*chip, c)).wait_recv()
                fwd = _remote(landed, landed, send.at[3 * n + a * 3 + j], recv.at[3 * n + a * 3 + j], sibling)
                fwd.start()
                passed.append(fwd)
        if conv_slab is not None:
            for j, chip in enumerate(chips):
                landed = ios[n].at[:, 2 * chip[0] + chip[1]]
                _remote(landed, landed, send.at[6 * n + j], recv.at[6 * n + j], (*chip, c)).wait_recv()
        for a in range(n):
            for j, chip in enumerate(chips):
                landed = _row_half(ios[a], (2 * chip[0] + chip[1],), 1 - c, slabs[a].shape[1], BF16_ROWS)
                _remote(landed, landed, send.at[3 * n + a * 3 + j], recv.at[3 * n + a * 3 + j], sibling).wait_recv()
        for cp in direct(ios, send, recv) + passed:
            cp.wait_send()

    inouts = tuple(slabs) + (() if conv_slab is None else (conv_slab,))
    return _Comm((), inouts, (), 6 * n + n_conv, start, finish)


def _halves_comm(grads):
    n = len(grads)

    def copies(ins, outs, send, recv):
        x, y, c, _ = _place()
        return [_remote(_row_half(ins[a], (slice(None),), 1 - c, grads[a].shape[1], 8), outs[a],
                        send.at[a], recv.at[a], (x, y, 1 - c)) for a in range(n)]

    def start(ins, ios, outs, send, recv):
        for cp in copies(ins, outs, send, recv):
            cp.start()

    def finish(ins, ios, outs, send, recv):
        for cp in copies(ins, outs, send, recv):
            cp.wait()

    out_shapes = tuple(jax.ShapeDtypeStruct((g.shape[0], g.shape[1] // 2, g.shape[2]), F32) for g in grads)
    return _Comm(tuple(grads), (), out_shapes, n, start, finish)


def _partials_comm(partials):
    n = len(partials)

    def copies(ins, outs, send, recv):
        x, y, c, chips = _place()
        return [_remote(ins[a].at[2 * chip[0] + chip[1]], outs[a].at[k], send.at[a * 3 + k], recv.at[a * 3 + k],
                        (*chip, c)) for a in range(n) for k, chip in enumerate(chips)]

    def start(ins, ios, outs, send, recv):
        for cp in copies(ins, outs, send, recv):
            cp.start()

    def finish(ins, ios, outs, send, recv):
        for cp in copies(ins, outs, send, recv):
            cp.wait()

    out_shapes = tuple(jax.ShapeDtypeStruct((3,) + p.shape[1:], BF16) for p in partials)
    return _Comm(tuple(partials), (), out_shapes, 3 * n, start, finish)


def _share_comm(grads):
    n = len(grads)

    def start(ins, ios, outs, send, recv):
        x, y, c, _ = _place()
        for a in range(n):
            mine = _row_half(ios[a], (slice(None),), c, grads[a].shape[1], 8)
            _remote(mine, mine, send.at[a], recv.at[a], (x, y, 1 - c)).start()

    def finish(ins, ios, outs, send, recv):
        x, y, c, _ = _place()
        for a in range(n):
            theirs = _row_half(ios[a], (slice(None),), 1 - c, grads[a].shape[1], 8)
            _remote(theirs, theirs, send.at[a], recv.at[a], (x, y, 1 - c)).wait()

    return _Comm((), tuple(grads), (), n, start, finish)


def _rms_fwd(x, g):
    r = lax.rsqrt(jnp.mean(x * x, axis=-1, keepdims=True) + RMS_EPS)
    return (x * r) * g


def _rms_bwd(x, g, dy):
    r = lax.rsqrt(jnp.mean(x * x, axis=-1, keepdims=True) + RMS_EPS)
    xh = x * r
    u = dy * g
    dx = r * (u - xh * jnp.mean(xh * u, axis=-1, keepdims=True))
    return dx, jnp.sum(dy * xh, axis=0, keepdims=True)


def _accumulate(ref, value, first):
    @pl.when(first)
    def _():
        ref[...] = value

    @pl.when(jnp.logical_not(first))
    def _():
        ref[...] += value


def _rope_coeffs(cos, sin):
    m = lax.broadcasted_iota(jnp.int32, cos.shape, 1) % HEAD_DIM
    a = jnp.where(m < ROPE_DIM, cos, 1.0)
    b = jnp.where(m < ROPE_DIM // 2, -sin, 0.0)
    c = jnp.where((m >= ROPE_DIM // 2) & (m < ROPE_DIM), sin, 0.0)
    return a, b, c


def _rope_apply(t, cos, sin):
    a, b, c = _rope_coeffs(cos, sin)
    n = t.shape[1]
    return a * t + b * pltpu.roll(t, n - ROPE_DIM // 2, 1) + c * pltpu.roll(t, ROPE_DIM // 2, 1)


def _rope_transpose(dt, cos, sin):
    a, b, c = _rope_coeffs(cos, sin)
    n = dt.shape[1]
    return a * dt + pltpu.roll(b * dt, ROPE_DIM // 2, 1) + pltpu.roll(c * dt, n - ROPE_DIM // 2, 1)


def _in_proj(h, w, cos, sin, *, tm, comm=None):
    t, k = h.shape
    s_n, _, n = w.shape
    assert t % tm == 0 and n % LANES == 0

    rotary_shards = -(-2 * ATTN_W // n)
    rotary = list(range(0, 2 * ATTN_W, LANES))
    per_shard = -(-len(rotary) // max(s_n - rotary_shards, 1))

    def body(h_ref, w_ref, cos_ref, sin_ref, o_ref):
        def shard(s):
            o_ref[:, s * n:(s + 1) * n] = jnp.dot(h_ref[...], w_ref[s], preferred_element_type=F32)

        def rope(c0):
            cols = slice(c0, c0 + LANES)
            o_ref[:, cols] = _rope_apply(o_ref[:, cols], cos_ref[...], sin_ref[...])

        for s in range(rotary_shards):
            shard(s)
        pending = list(rotary)
        for s in range(rotary_shards, s_n):
            for c0 in pending[:per_shard]:
                rope(c0)
            pending = pending[per_shard:]
            shard(s)
        for c0 in pending:
            rope(c0)

    lane_tile = pl.BlockSpec((tm, LANES), lambda i: (i, 0))
    out, got = _call(
        body, [h, w, cos, sin], name="in_proj", grid=(t // tm,),
        in_specs=[pl.BlockSpec((tm, k), lambda i: (i, 0)), _const_spec(w.shape, (0, 0, 0)), lane_tile, lane_tile],
        out_specs=[pl.BlockSpec((tm, s_n * n), lambda i: (i, 0))],
        out_shape=[jax.ShapeDtypeStruct((t, s_n * n), F32)], comm=comm)
    return out[0], got


def _dx_through_norms(operands, in_specs, dx_rows, dres, pre, post, *, tm, name, comm=None):
    t, d_model = dres.shape
    (x, gx3, lx), (y, gy3, ly) = pre, post
    n_op = len(operands)
    row = pl.BlockSpec((tm, d_model), lambda i: (i, 0))
    gsum = _const_spec((1, d_model), (0, 0))

    def body(*refs):
        d_ref, x_ref, gx_ref, y_ref, gy_ref, dn_ref, dgx_ref, dy_ref, dgy_ref = refs[n_op:]
        first = pl.program_id(0) == 0
        dx, dgx = _rms_bwd(x_ref[...], gx_ref[...], dx_rows(refs[:n_op]))
        d_new = d_ref[...] + dx
        dn_ref[...] = d_new
        _accumulate(dgx_ref, dgx, first)
        dy, dgy = _rms_bwd(y_ref[...], gy_ref[...], d_new)
        dy_ref[...] = dy.astype(BF16)
        _accumulate(dgy_ref, dgy, first)

    out, got = _call(
        body, list(operands) + [dres, x, gx3, y, gy3], name=name, grid=(t // tm,),
        in_specs=list(in_specs) + [row, row, _gain_spec(gx3, lx), row, _gain_spec(gy3, ly)],
        out_specs=[row, gsum, row, gsum],
        out_shape=[jax.ShapeDtypeStruct((t, d_model), F32), jax.ShapeDtypeStruct((1, d_model), F32),
                   jax.ShapeDtypeStruct((t, d_model), BF16), jax.ShapeDtypeStruct((1, d_model), F32)],
        comm=comm)
    return out[0], out[2], out[1], out[3], got


def _gate_up_dx_norms(dg, du, w, dres, pre, post, *, tm, comm=None):
    s_n, ko, n = w.shape
    half = s_n // 2

    def dx_rows(refs):
        dg_ref, du_ref, w_ref = refs
        acc = jnp.zeros((tm, ko), F32)
        for s in range(half):
            cols = slice(s * n, (s + 1) * n)
            acc = acc + lax.dot_general(dg_ref[:, cols], w_ref[s], LANE_CONTRACT, preferred_element_type=F32)
            acc = acc + lax.dot_general(du_ref[:, cols], w_ref[half + s], LANE_CONTRACT, preferred_element_type=F32)
        return acc

    a_spec = pl.BlockSpec((tm, half * n), lambda i: (i, 0))
    return _dx_through_norms([dg, du, w], [a_spec, a_spec, _const_spec(w.shape, (0, 0, 0))], dx_rows, dres, pre, post,
                             tm=tm, name="gate_up_dx", comm=comm)


def _in_proj_dx_norms(d_attn3, d_conv3, w, dres, pre, post, *, tm, comm=None):
    s_n, ko, n = w.shape
    per = n // CHUNK
    pieces = d_attn3.shape[0]
    width = d_attn3.shape[2]

    def dx_rows(refs):
        a_ref, b_ref, w_ref = refs
        acc = jnp.zeros((tm, ko), F32)
        for c in range(s_n * per):
            src = a_ref if c // 2 < pieces else b_ref
            piece, c0 = (c // 2) % pieces, (c % 2) * CHUNK
            acc = acc + lax.dot_general(src[piece, :, c0:c0 + CHUNK], w_ref[c // per, :, (c % per) * CHUNK:(c % per + 1) * CHUNK],
                                        LANE_CONTRACT, preferred_element_type=F32)
        return acc

    stack = pl.BlockSpec((pieces, tm, width), lambda i: (0, i, 0))
    return _dx_through_norms([d_attn3, d_conv3, w], [stack, stack, _const_spec(w.shape, (0, 0, 0))], dx_rows, dres, pre,
                             post, tm=tm, name="in_proj_dx", comm=comm)


def _mm_tn(a, b, s_n, *, tka, name, into=None, shard0=0, comm=None):
    t, ka = a.shape
    n = b.shape[1] // s_n
    assert b.shape[0] == t and ka % tka == 0

    def body(a_ref, b_ref, *rest):
        rest[-1][...] = lax.dot_general(a_ref[...], b_ref[...], ROW_CONTRACT, preferred_element_type=F32)

    operands, in_specs, aliases = [a, b], [pl.BlockSpec((t, tka), lambda i, s: (0, i)),
                                           pl.BlockSpec((t, n), lambda i, s: (0, s))], {}
    out_shape = jax.ShapeDtypeStruct((s_n, ka, n), F32)
    if into is not None:
        operands, in_specs, aliases = operands + [into], in_specs + [ANY], {2: 0}
        out_shape = jax.ShapeDtypeStruct(into.shape, F32)
    out, got = _call(body, operands, name=name, grid=(ka // tka, s_n), in_specs=in_specs,
                     out_specs=[pl.BlockSpec((None, tka, n), lambda i, s: (shard0 + s, i, 0))],
                     out_shape=[out_shape], aliases=aliases, comm=comm)
    return out[0], got


def _gate_up_swiglu(h, w, *, tm, comm=None):
    t, k = h.shape
    s_n, _, n = w.shape
    half = s_n // 2

    def body(h_ref, wg_ref, wu_ref, g_ref, u_ref, a_ref):
        g = jnp.dot(h_ref[...], wg_ref[...], preferred_element_type=F32)
        u = jnp.dot(h_ref[...], wu_ref[...], preferred_element_type=F32)
        g_ref[...] = g.astype(BF16)
        u_ref[...] = u.astype(BF16)
        a_ref[...] = (g * jax.nn.sigmoid(g) * u).astype(BF16)

    col = pl.BlockSpec((tm, n), lambda i, j: (i, j))
    out, got = _call(
        body, [h, w, w], name="gate_up", grid=(t // tm, half),
        in_specs=[pl.BlockSpec((tm, k), lambda i, j: (i, 0)), pl.BlockSpec((None, k, n), lambda i, j: (j, 0, 0)),
                  pl.BlockSpec((None, k, n), lambda i, j: (half + j, 0, 0))],
        out_specs=[col, col, col],
        out_shape=[jax.ShapeDtypeStruct((t, half * n), BF16)] * 3, comm=comm)
    return out[0], out[1], out[2], got


def _down_dx_swiglu_bwd(df, w, g, u, *, tm, tko):
    t, k = df.shape
    _, ko, _ = w.shape
    assert t % tm == 0 and ko % tko == 0

    def body(df_ref, w_ref, g_ref, u_ref, dg_ref, du_ref):
        d = lax.dot_general(df_ref[...], w_ref[...], LANE_CONTRACT, preferred_element_type=F32)
        gg = g_ref[...].astype(F32)
        sig = jax.nn.sigmoid(gg)
        dg_ref[...] = (d * u_ref[...].astype(F32) * (sig * (1.0 + gg * (1.0 - sig)))).astype(BF16)
        du_ref[...] = (d * (gg * sig)).astype(BF16)

    col = pl.BlockSpec((tm, tko), lambda i, j: (i, j))
    return pl.pallas_call(
        body, grid=(t // tm, ko // tko),
        in_specs=[pl.BlockSpec((tm, k), lambda i, j: (i, 0)), pl.BlockSpec((None, tko, k), lambda i, j: (0, j, 0)),
                  col, col],
        out_specs=[col, col], out_shape=[jax.ShapeDtypeStruct((t, ko), BF16)] * 2, name="down_dx")(df, w, g, u)


def _in_proj_dx(d_attn3, d_conv3, w, *, tm, comm=None):
    _, t, _ = d_attn3.shape
    s_n, ko, n = w.shape
    half, per = s_n // 2, n // CHUNK
    assert t % tm == 0

    def body(*refs):
        a_refs, b_refs, wa_ref, wb_ref, o_ref = refs[:per], refs[per:2 * per], refs[2 * per], refs[2 * per + 1], refs[-1]
        acc = jnp.zeros(o_ref.shape, F32)
        for r in range(per):
            cols = slice(r * CHUNK, (r + 1) * CHUNK)
            acc = acc + lax.dot_general(a_refs[r][...], wa_ref[:, cols], LANE_CONTRACT, preferred_element_type=F32)
            acc = acc + lax.dot_general(b_refs[r][...], wb_ref[:, cols], LANE_CONTRACT, preferred_element_type=F32)
        _accumulate(o_ref, acc, pl.program_id(1) == 0)

    piece = lambda r: pl.BlockSpec((None, tm, CHUNK), lambda i, s: ((per * s + r) // 2, i, (per * s + r) % 2))
    out, got = _call(
        body, [d_attn3] * per + [d_conv3] * per + [w, w], name="in_proj_dx", grid=(t // tm, half),
        in_specs=[piece(r) for r in range(per)] * 2
        + [pl.BlockSpec((None, ko, n), lambda i, s: (s, 0, 0)), pl.BlockSpec((None, ko, n), lambda i, s: (half + s, 0, 0))],
        out_specs=[pl.BlockSpec((tm, ko), lambda i, s: (i, 0))],
        out_shape=[jax.ShapeDtypeStruct((t, ko), F32)], comm=comm)
    return out[0], got


def _in_proj_dw(h, d_attn3, d_conv3, s_n, *, tka):
    t, ka = h.shape
    half = s_n // 2
    n = 3 * d_attn3.shape[2] // half
    per = n // CHUNK
    assert ka % tka == 0

    def body(*refs):
        h_ref, o_ref = refs[0], refs[-1]
        for side in range(2):
            for r in range(per):
                o_ref[side, :, r * CHUNK:(r + 1) * CHUNK] = lax.dot_general(
                    h_ref[...], refs[1 + side * per + r][...], ROW_CONTRACT, preferred_element_type=F32)

    piece = lambda r: pl.BlockSpec((None, t, CHUNK), lambda i, s: ((per * s + r) // 2, 0, (per * s + r) % 2))
    out = pl.pallas_call(
        body, grid=(ka // tka, half),
        in_specs=[pl.BlockSpec((t, tka), lambda i, s: (0, i))] + [piece(r) for r in range(per)] * 2,
        out_specs=pl.BlockSpec((2, None, tka, n), lambda i, s: (0, s, i, 0)),
        out_shape=jax.ShapeDtypeStruct((2, half, ka, n), F32), name="in_proj_dw")(h, *[d_attn3] * per, *[d_conv3] * per)
    return out.reshape(s_n, ka, n)


def _rope_tables(positions_col, inv_freq_row):
    t = positions_col.shape[0]

    def body(pos_ref, f_ref, cos_ref, sin_ref):
        ang = pos_ref[...].astype(F32) * f_ref[...]
        cos_ref[...] = jnp.cos(ang)
        sin_ref[...] = jnp.sin(ang)

    return pl.pallas_call(
        body, out_shape=[jax.ShapeDtypeStruct((t, LANES), F32)] * 2, name="rope_tables")(positions_col, inv_freq_row)


def _norm_fwd(x, g3, l, *, tm):
    t, w = x.shape

    def body(x_ref, g_ref, h_ref):
        h_ref[...] = _rms_fwd(x_ref[...], g_ref[...]).astype(BF16)

    return pl.pallas_call(
        body, grid=(t // tm,),
        in_specs=[pl.BlockSpec((tm, w), lambda i: (i, 0)), _gain_spec(g3, l)],
        out_specs=pl.BlockSpec((tm, w), lambda i: (i, 0)),
        out_shape=jax.ShapeDtypeStruct((t, w), BF16), name="norm_fwd")(x, g3)


def _mm_resnorm(a, w, x, g_post3, l_post, g_next3, l_next, *, tm, name, comm=None):
    t, k = a.shape
    _, _, n = w.shape
    with_next = g_next3 is not None
    row = pl.BlockSpec((tm, n), lambda i: (i, 0))

    def body(a_ref, w_ref, x_ref, gp_ref, *rest):
        y = jnp.dot(a_ref[...], w_ref[...], preferred_element_type=F32)
        x_new = x_ref[...] + _rms_fwd(y, gp_ref[...])
        if with_next:
            gn_ref, y_ref, xo_ref, h_ref = rest
            h_ref[...] = _rms_fwd(x_new, gn_ref[...]).astype(BF16)
        else:
            y_ref, xo_ref = rest
        y_ref[...] = y
        xo_ref[...] = x_new

    ins = [a, w, x, g_post3] + ([g_next3] if with_next else [])
    in_specs = ([pl.BlockSpec((tm, k), lambda i: (i, 0)), _const_spec((None, k, n), (0, 0, 0)), row,
                 _gain_spec(g_post3, l_post)] + ([_gain_spec(g_next3, l_next)] if with_next else []))
    out_shape = [jax.ShapeDtypeStruct((t, n), F32)] * 2 + ([jax.ShapeDtypeStruct((t, n), BF16)] if with_next else [])
    out, got = _call(body, ins, name=name, grid=(t // tm,), in_specs=in_specs, out_specs=[row] * len(out_shape),
                     out_shape=out_shape, comm=comm)
    return out[0], out[1], (out[2] if with_next else None), got


def _conv_fwd(proj, conv_w, l):
    t = proj.shape[0]
    col0 = 3 * ATTN_W // LANES

    def body(u_ref, gb_ref, gc_ref, w_ref, y_ref):
        c = gc_ref[...] * u_ref[...]
        row = lax.broadcasted_iota(jnp.int32, c.shape, 0)
        c_prev = jnp.where(row == 0, 0.0, pltpu.roll(c, 1, 0))
        c_next = jnp.where(row == t - 1, 0.0, pltpu.roll(c, t - 1, 0))
        w = w_ref[...]
        y_ref[...] = gb_ref[...] * (w[0:1] * c_prev + w[1:2] * c + w[2:3] * c_next)

    nj = CONV_W // LANES
    cols = lambda base: pl.BlockSpec((t, LANES), lambda j: (0, base + j))
    return pl.pallas_call(
        body, grid=(nj,),
        in_specs=[cols(col0), cols(col0 + nj), cols(col0 + 2 * nj),
                  pl.BlockSpec((None, None, 3, LANES), lambda j: (l, j, 0, 0))],
        out_specs=pl.BlockSpec((t, LANES), lambda j: (0, j)),
        out_shape=jax.ShapeDtypeStruct((t, CONV_W), F32), name="conv_fwd")(proj, proj, proj, conv_w)


def _merge_out_proj(attn, conv_y, ga3, gc3, w, x, g_post3, g_next3, l, *, tm, comm=None):
    t = attn.shape[0]
    _, k, n = w.shape
    half = pl.BlockSpec((tm, ATTN_W), lambda i: (i, 0))
    row = pl.BlockSpec((tm, n), lambda i: (i, 0))

    def body(a_ref, c_ref, ga_ref, gc_ref, w_ref, x_ref, gp_ref, gn_ref, m_ref, y_ref, xo_ref, h_ref):
        m_ref[:, :ATTN_W] = _rms_fwd(a_ref[...], ga_ref[...]).astype(BF16)
        m_ref[:, ATTN_W:] = _rms_fwd(c_ref[...], gc_ref[...]).astype(BF16)
        y = jnp.dot(m_ref[...], w_ref[...], preferred_element_type=F32)
        x_new = x_ref[...] + _rms_fwd(y, gp_ref[...])
        h_ref[...] = _rms_fwd(x_new, gn_ref[...]).astype(BF16)
        y_ref[...] = y
        xo_ref[...] = x_new

    out, got = _call(
        body, [attn, conv_y, ga3, gc3, w, x, g_post3, g_next3], name="out_proj", grid=(t // tm,),
        in_specs=[half, half, _gain_spec(ga3, l), _gain_spec(gc3, l), _const_spec((None, k, n), (0, 0, 0)), row,
                  _gain_spec(g_post3, l), _gain_spec(g_next3, l)],
        out_specs=[pl.BlockSpec((tm, k), lambda i: (i, 0)), row, row, row],
        out_shape=[jax.ShapeDtypeStruct((t, k), BF16), jax.ShapeDtypeStruct((t, n), F32),
                   jax.ShapeDtypeStruct((t, n), F32), jax.ShapeDtypeStruct((t, n), BF16)], comm=comm)
    return out[0], out[1], out[2], out[3], got


def _loss_fwd_bwd(y, target, *, tm):
    t, w = y.shape
    row = pl.BlockSpec((tm, w), lambda i: (i, 0))

    def body(y_ref, t_ref, dy_ref, loss_ref):
        e = y_ref[...] - t_ref[...]
        dy_ref[...] = e * (1.0 / w)
        sq = jnp.sum(e * e, axis=0, keepdims=True) * (0.5 / w)
        part = sq[:, :LANES]
        for j in range(1, w // LANES):
            part = part + sq[:, j * LANES:(j + 1) * LANES]
        _accumulate(loss_ref, part, pl.program_id(0) == 0)

    return pl.pallas_call(
        body, grid=(t // tm,), in_specs=[row, row],
        out_specs=[row, _const_spec((1, LANES), (0, 0))],
        out_shape=[jax.ShapeDtypeStruct((t, w), F32), jax.ShapeDtypeStruct((1, LANES), F32)], name="loss")(y, target)


def _tile_rows(t, nt, lb, d):
    r = t // nt
    q0 = (t % nt) * TQ
    m0 = jnp.clip(q0 - BAND, 0, lb - WIN)
    if d == 1:
        return pl.ds(pl.multiple_of(q0, TQ), TQ), pl.ds(pl.multiple_of(m0, BAND), WIN), m0 - q0
    return pl.ds(r + d * q0, TQ, stride=d), pl.ds(r + d * m0, WIN, stride=d), m0 - q0


def _for_row_chunks(t, fn, chunk=512):
    def step(i, carry):
        fn(pl.ds(pl.multiple_of(i * chunk, chunk), chunk))
        return carry

    lax.fori_loop(0, t // chunk, step, 0)


WINDOW_OFFSETS = (-BAND, 0, -2 * BAND)


def _fill_band_bias(bias_ref):
    rel0 = (lax.broadcasted_iota(jnp.int32, (2 * TQ, WIN), 1)
            - lax.broadcasted_iota(jnp.int32, (2 * TQ, WIN), 0) % TQ)
    for j, off in enumerate(WINDOW_OFFSETS):
        rel = rel0 + off
        bias_ref[j] = jnp.where((rel >= -BAND) & (rel <= BAND), 0.0, NEG_INF)


def _fill_sequence_bias(bias_ref):
    rel = (lax.broadcasted_iota(jnp.int32, (2 * WIN, WIN), 1) - lax.broadcasted_iota(jnp.int32, (2 * WIN, WIN), 0) % WIN)
    bias_ref[...] = jnp.where((rel >= -BAND) & (rel <= BAND), 0.0, NEG_INF)


def _band_bias(bias_ref, off):
    return bias_ref[jnp.where(off == WINDOW_OFFSETS[0], 0, jnp.where(off == WINDOW_OFFSETS[1], 1, 2))]


def _stack_heads(a, first_head):
    return jnp.concatenate([jnp.where(first_head, a, 0.0), jnp.where(first_head, 0.0, a)], axis=0)


def _unstack_heads(a2, first_head):
    n = a2.shape[0] // 2
    return jnp.where(first_head, a2[:n], a2[n:])


def _attn_fwd(proj, comm=None):
    t = proj.shape[0]
    npair = ATTN_W // LANES

    def body(q_ref, k_ref, v_ref, o_ref, lse_ref, o1, o2, l0, l1, l2, m1, m2, bias, bias_seq):
        _fill_band_bias(bias)
        _fill_sequence_bias(bias_seq)
        outs, dens, maxs = (o_ref, o1, o2), (l0, l1, l2), (lse_ref, m1, m2)

        def softmax_tile(b, qrows, krows, n_q, band_bias):
            first_head = lax.broadcasted_iota(jnp.int32, (n_q, LANES), 1) < HEAD_DIM
            q2 = _stack_heads(q_ref[qrows, :] * SCALE, first_head).astype(BF16)
            kw = k_ref[krows, :].astype(BF16)
            vw = jnp.concatenate([v_ref[krows, :].astype(BF16), jnp.ones((WIN, LANES), BF16)], axis=1)
            s = lax.dot_general(q2, kw, LANE_CONTRACT, preferred_element_type=F32) + band_bias
            m = jnp.max(s, axis=-1, keepdims=True)
            pv = jnp.dot(jnp.exp(s - m).astype(BF16), vw, preferred_element_type=F32)
            outs[b][qrows, :] = _unstack_heads(pv[:, :LANES], first_head)
            dens[b][qrows, :] = _unstack_heads(pv[:, LANES:], first_head)
            maxs[b][qrows, :] = _unstack_heads(jnp.broadcast_to(m, (2 * n_q, LANES)), first_head)

        for b, d in enumerate(DILATIONS):
            lb = t // d
            if lb == WIN:
                def sequence(r, carry, b=b, d=d):
                    rows = pl.ds(r, WIN, stride=d)
                    softmax_tile(b, rows, rows, WIN, bias_seq[...])
                    return carry

                lax.fori_loop(0, d, sequence, 0, unroll=8)
                continue
            nt = lb // TQ

            def tile(ti, carry, b=b, d=d, lb=lb, nt=nt):
                qrows, krows, off = _tile_rows(ti, nt, lb, d)
                softmax_tile(b, qrows, krows, TQ, _band_bias(bias, off))
                return carry

            lax.fori_loop(0, d * nt, tile, 0, unroll=16)

        def finish(rows):
            ms = [m_b[rows, :] for m_b in maxs]
            m_all = jnp.maximum(jnp.maximum(ms[0], ms[1]), ms[2])
            ws = [jnp.exp(m_b - m_all) for m_b in ms]
            den = ws[0] * dens[0][rows, :] + ws[1] * dens[1][rows, :] + ws[2] * dens[2][rows, :]
            num = ws[0] * outs[0][rows, :] + ws[1] * outs[1][rows, :] + ws[2] * outs[2][rows, :]
            o_ref[rows, :] = num / den
            lse_ref[rows, :] = m_all + jnp.log(den)

        _for_row_chunks(t, finish, 256)

    cols = lambda base: pl.BlockSpec((t, LANES), lambda g: (0, base + g))
    out, got = _call(
        body, [proj, proj, proj], name="attn_fwd", grid=(npair,),
        in_specs=[cols(0), cols(npair), cols(2 * npair)],
        out_specs=[cols(0), cols(0)],
        out_shape=[jax.ShapeDtypeStruct((t, ATTN_W), F32)] * 2,
        scratch_shapes=[pltpu.VMEM((t, LANES), F32)] * 7 + [pltpu.VMEM((len(WINDOW_OFFSETS), 2 * TQ, WIN), F32),
                                                            pltpu.VMEM((2 * WIN, WIN), F32)],
        comm=comm)
    return out[0], out[1], got


def _attn_bwd(proj, cos, sin, d_attn, lse, delta, comm=None):
    t = proj.shape[0]
    npair = ATTN_W // LANES

    def body(q_ref, k_ref, v_ref, cos_ref, sin_ref, do_ref, l_ref, dl_ref, dqkv_ref,
             dq_acc, dk_acc, dv_acc, bias, bias_seq):
        _fill_band_bias(bias)
        _fill_sequence_bias(bias_seq)
        dq_acc[...] = jnp.zeros(dq_acc.shape, F32)
        dk_acc[...] = jnp.zeros(dk_acc.shape, F32)
        dv_acc[...] = jnp.zeros(dv_acc.shape, F32)
        def stack_column(a):
            return jnp.concatenate([a[:, 0:1], a[:, HEAD_DIM:HEAD_DIM + 1]], axis=0)

        def grad_tile(qrows, krows, n_q, band_bias):
            first_head = lax.broadcasted_iota(jnp.int32, (n_q, LANES), 1) < HEAD_DIM
            q2 = _stack_heads(q_ref[qrows, :] * SCALE, first_head).astype(BF16)
            do2 = _stack_heads(do_ref[qrows, :], first_head).astype(BF16)
            kw = k_ref[krows, :].astype(BF16)
            vw = v_ref[krows, :].astype(BF16)
            s = lax.dot_general(q2, kw, LANE_CONTRACT, preferred_element_type=F32) + band_bias
            p = jnp.exp(s - stack_column(l_ref[qrows, :]))
            dp = lax.dot_general(do2, vw, LANE_CONTRACT, preferred_element_type=F32)
            ds = (p * (dp - stack_column(dl_ref[qrows, :]))).astype(BF16)
            dq2 = jnp.dot(ds, kw, preferred_element_type=F32)
            dq_acc[qrows, :] += _unstack_heads(dq2, first_head) * SCALE
            dk_acc[krows, :] += lax.dot_general(ds, q2, ROW_CONTRACT, preferred_element_type=F32)
            dv_acc[krows, :] += lax.dot_general(p.astype(BF16), do2, ROW_CONTRACT, preferred_element_type=F32)

        for d in DILATIONS:
            lb = t // d
            if lb == WIN:
                def sequence(r, carry, d=d):
                    rows = pl.ds(r, WIN, stride=d)
                    grad_tile(rows, rows, WIN, bias_seq[...])
                    return carry

                lax.fori_loop(0, d, sequence, 0, unroll=4)
                continue
            nt = lb // TQ

            def tile(ti, carry, d=d, lb=lb, nt=nt):
                qrows, krows, off = _tile_rows(ti, nt, lb, d)
                grad_tile(qrows, krows, TQ, _band_bias(bias, off))
                return carry

            lax.fori_loop(0, d * nt, tile, 0, unroll=8)

        def finish(rows):
            dqkv_ref[0, rows, :] = _rope_transpose(dq_acc[rows, :], cos_ref[rows, :], sin_ref[rows, :]).astype(BF16)
            dqkv_ref[1, rows, :] = _rope_transpose(dk_acc[rows, :], cos_ref[rows, :], sin_ref[rows, :]).astype(BF16)
            dqkv_ref[2, rows, :] = dv_acc[rows, :].astype(BF16)

        _for_row_chunks(t, finish)

    cols = lambda base: pl.BlockSpec((t, LANES), lambda g: (0, base + g))
    out, got = _call(
        body, [proj, proj, proj, cos, sin, d_attn, lse, delta], name="attn_bwd", grid=(npair,),
        in_specs=[cols(0), cols(npair), cols(2 * npair), WHOLE_VMEM, WHOLE_VMEM, cols(0), cols(0), cols(0)],
        out_specs=[pl.BlockSpec((3, t, LANES), lambda g: (0, 0, g))],
        out_shape=[jax.ShapeDtypeStruct((3, t, ATTN_W), BF16)],
        scratch_shapes=[pltpu.VMEM((t, LANES), F32)] * 3 + [pltpu.VMEM((len(WINDOW_OFFSETS), 2 * TQ, WIN), F32),
                                                            pltpu.VMEM((2 * WIN, WIN), F32)],
        comm=comm)
    return out[0], got


def _norm_bwd(dres, pre, post, *, tm, comm=None):
    t, w = dres.shape
    row = pl.BlockSpec((tm, w), lambda i: (i, 0))
    gsum = _const_spec((1, w), (0, 0))
    ins, in_specs, out_shape, out_specs = [dres], [row], [], []
    if pre is not None:
        dh, x, g3, l = pre
        ins += [dh, x, g3]
        in_specs += [row, row, _gain_spec(g3, l)]
        out_shape += [jax.ShapeDtypeStruct((t, w), F32), jax.ShapeDtypeStruct((1, w), F32)]
        out_specs += [row, gsum]
    if post is not None:
        y, g3, l = post
        ins += [y, g3]
        in_specs += [row, _gain_spec(g3, l)]
        out_shape += [jax.ShapeDtypeStruct((t, w), BF16), jax.ShapeDtypeStruct((1, w), F32)]
        out_specs += [row, gsum]
    n_in = len(ins)

    def body(*refs):
        first = pl.program_id(0) == 0
        ins_r, outs_r = list(refs[:n_in]), list(refs[n_in:])
        d = ins_r.pop(0)[...]
        if pre is not None:
            dh_ref, x_ref, g_ref = ins_r[:3]
            ins_r = ins_r[3:]
            dx, dg = _rms_bwd(x_ref[...], g_ref[...], dh_ref[...])
            d = d + dx
            outs_r.pop(0)[...] = d
            _accumulate(outs_r.pop(0), dg, first)
        if post is not None:
            y_ref, g_ref = ins_r
            dy, dg = _rms_bwd(y_ref[...], g_ref[...], d)
            outs_r.pop(0)[...] = dy.astype(BF16)
            _accumulate(outs_r.pop(0), dg, first)

    out, got = _call(body, ins, name="norm_bwd", grid=(t // tm,), in_specs=in_specs, out_specs=out_specs,
                     out_shape=out_shape, comm=comm)
    d_new, dg_pre = (out.pop(0), out.pop(0)) if pre is not None else (None, None)
    dy, dg_post = (out.pop(0), out.pop(0)) if post is not None else (None, None)
    return d_new, dy, dg_pre, dg_post, got


def _merge_bwd(dmix, w, attn, conv_y, ga3, gc3, l, *, tm, comm=None):
    t = attn.shape[0]
    _, ko, k = w.shape
    row = pl.BlockSpec((tm, ATTN_W), lambda i: (i, 0))
    gsum = _const_spec((1, ATTN_W), (0, 0))

    def body(dmix_ref, w_ref, a_ref, c_ref, ga_ref, gc_ref, da_ref, dl_ref, dc_ref, dga_ref, dgc_ref):
        first = pl.program_id(0) == 0
        d_merged = lax.dot_general(dmix_ref[...], w_ref[...], LANE_CONTRACT, preferred_element_type=F32)
        attn_t = a_ref[...]
        da, dga = _rms_bwd(attn_t, ga_ref[...], d_merged[:, :ATTN_W])
        dc, dgc = _rms_bwd(c_ref[...], gc_ref[...], d_merged[:, ATTN_W:])
        da_ref[...] = da
        dc_ref[...] = dc
        same_head = (lax.broadcasted_iota(jnp.int32, (ATTN_W, ATTN_W), 0) // HEAD_DIM
                     == lax.broadcasted_iota(jnp.int32, (ATTN_W, ATTN_W), 1) // HEAD_DIM).astype(BF16)
        rest = da * attn_t
        total = jnp.zeros(rest.shape, F32)
        for _ in range(3):
            term = rest.astype(BF16)
            total = total + jnp.dot(term, same_head, preferred_element_type=F32)
            rest = rest - term.astype(F32)
        dl_ref[...] = total
        _accumulate(dga_ref, dga, first)
        _accumulate(dgc_ref, dgc, first)

    out, got = _call(
        body, [dmix, w, attn, conv_y, ga3, gc3], name="merge_bwd", grid=(t // tm,),
        in_specs=[pl.BlockSpec((tm, k), lambda i: (i, 0)), _const_spec((None, ko, k), (0, 0, 0)),
                  row, row, _gain_spec(ga3, l), _gain_spec(gc3, l)],
        out_specs=[row, row, row, gsum, gsum],
        out_shape=[jax.ShapeDtypeStruct((t, ATTN_W), F32)] * 3 + [jax.ShapeDtypeStruct((1, ATTN_W), F32)] * 2,
        comm=comm)
    return (*out, got)


def _conv_bwd(proj, conv_w, l, d_conv_y):
    t = proj.shape[0]
    col0 = 3 * ATTN_W // LANES
    nj = CONV_W // LANES

    def body(u_ref, gb_ref, gc_ref, w_ref, dy_ref, d3_ref, dw_ref):
        u, gc, dy = u_ref[...], gc_ref[...], dy_ref[...]
        row = lax.broadcasted_iota(jnp.int32, u.shape, 0)
        down = lambda a: jnp.where(row == 0, 0.0, pltpu.roll(a, 1, 0))
        up = lambda a: jnp.where(row == t - 1, 0.0, pltpu.roll(a, t - 1, 0))
        w = w_ref[...]
        c = gc * u
        c_prev, c_next = down(c), up(c)
        d3_ref[1] = (dy * (w[0:1] * c_prev + w[1:2] * c + w[2:3] * c_next)).astype(BF16)
        dz = dy * gb_ref[...]
        dc = w[0:1] * up(dz) + w[1:2] * dz + w[2:3] * down(dz)
        d3_ref[0] = (dc * gc).astype(BF16)
        d3_ref[2] = (dc * u).astype(BF16)
        dw_ref[0:1, :] = jnp.sum(dz * c_prev, axis=0, keepdims=True)
        dw_ref[1:2, :] = jnp.sum(dz * c, axis=0, keepdims=True)
        dw_ref[2:3, :] = jnp.sum(dz * c_next, axis=0, keepdims=True)

    cols = lambda base: pl.BlockSpec((t, LANES), lambda j: (0, base + j))
    return pl.pallas_call(
        body, grid=(nj,),
        in_specs=[cols(col0), cols(col0 + nj), cols(col0 + 2 * nj),
                  pl.BlockSpec((None, None, 3, LANES), lambda j: (l, j, 0, 0)), cols(0)],
        out_specs=[pl.BlockSpec((3, t, LANES), lambda j: (0, 0, j)), pl.BlockSpec((None, 3, LANES), lambda j: (j, 0, 0))],
        out_shape=[jax.ShapeDtypeStruct((3, t, CONV_W), BF16), jax.ShapeDtypeStruct((nj, 3, LANES), F32)],
        name="conv_bwd")(proj, proj, proj, conv_w, d_conv_y)


def _own_shard_slab(w, l, place, dtype):
    _, rows, cols = w.shape
    tr = rows if rows <= 704 else 512
    assert rows % tr == 0

    def body(p_ref, w_ref, o_ref):
        del p_ref
        o_ref[...] = w_ref[...].astype(dtype)

    grid_spec = pltpu.PrefetchScalarGridSpec(
        num_scalar_prefetch=1, grid=(rows // tr,),
        in_specs=[pl.BlockSpec((None, tr, cols), lambda i, p: (l, i, 0))],
        out_specs=pl.BlockSpec((None, tr, cols), lambda i, p: (p[0], i, 0)))
    return pl.pallas_call(body, grid_spec=grid_spec, name="own_shard_slab",
                          out_shape=jax.ShapeDtypeStruct((N_CHIPS, rows, cols), dtype))(place, w)


def _own_conv_slab(w, place):
    depth = w.shape[0]

    def body(p_ref, w_ref, o_ref):
        del p_ref
        o_ref[...] = w_ref[...]

    grid_spec = pltpu.PrefetchScalarGridSpec(
        num_scalar_prefetch=1, grid=(depth,),
        in_specs=[pl.BlockSpec((None, 3, LANES), lambda l, p: (l, 0, 0))],
        out_specs=pl.BlockSpec((None, None, 3, LANES), lambda l, p: (l, p[0], 0, 0)))
    return pl.pallas_call(body, grid_spec=grid_spec, name="own_conv_slab",
                          out_shape=jax.ShapeDtypeStruct((depth, N_CHIPS, 3, LANES), F32))(place, w)


def _add_halves(grad, got, place):
    s_n, rows, cols = grad.shape
    hr = rows // 2

    def body(p_ref, g_ref, r_ref, o_ref):
        del p_ref
        o_ref[...] = (g_ref[...] + r_ref[...]).astype(BF16)

    grid_spec = pltpu.PrefetchScalarGridSpec(
        num_scalar_prefetch=1, grid=(s_n,),
        in_specs=[pl.BlockSpec((None, hr, cols), lambda s, p: (s, p[1], 0)),
                  pl.BlockSpec((None, hr, cols), lambda s, p: (s, 0, 0))],
        out_specs=pl.BlockSpec((None, hr, cols), lambda s, p: (s, 0, 0)))
    return pl.pallas_call(body, grid_spec=grid_spec, out_shape=jax.ShapeDtypeStruct((s_n, hr, cols), BF16),
                          name="add_halves")(place, grad, got)


def _sum_partials(partial, got, place, acc, l):
    _, hr, cols = partial.shape

    def body(p_ref, mine_ref, got_ref, acc_ref, o_ref):
        del p_ref, acc_ref
        total = mine_ref[...].astype(F32)
        for k in range(3):
            total = total + got_ref[k].astype(F32)
        o_ref[...] = total

    grid_spec = pltpu.PrefetchScalarGridSpec(
        num_scalar_prefetch=1, grid=(1,),
        in_specs=[pl.BlockSpec((None, hr, cols), lambda i, p: (p[0], 0, 0)),
                  pl.BlockSpec((3, hr, cols), lambda i, p: (0, 0, 0)), ANY],
        out_specs=pl.BlockSpec((None, hr, cols), lambda i, p: (l, p[1], 0)))
    return pl.pallas_call(body, grid_spec=grid_spec, out_shape=jax.ShapeDtypeStruct(acc.shape, F32),
                          input_output_aliases={3: 0}, name="sum_partials")(place, partial, got, acc)


def _allreduce_small(vec, loss_row):
    rows = vec.shape[0]

    def body(v_ref, o_ref, slots, send_sems, recv_sems):
        x, y, c, _ = _place()
        me = 4 * x + 2 * y + c
        slots[me] = v_ref[...]
        copies = []
        for k in range(1, N_DEV):
            flip = lambda v, bit: 1 - v if bit else v
            peer = (flip(x, k & 4), flip(y, k & 2), flip(c, k & 1))
            copies.append(_remote(v_ref, slots.at[me], send_sems.at[k - 1], recv_sems.at[k - 1], peer))
        for cp in copies:
            cp.start()
        for k in range(1, N_DEV):
            flip = lambda v, bit: 1 - v if bit else v
            peer_id = 4 * flip(x, k & 4) + 2 * flip(y, k & 2) + flip(c, k & 1)
            _remote(v_ref, slots.at[peer_id], send_sems.at[k - 1], recv_sems.at[k - 1], (x, y, c)).wait_recv()
        for cp in copies:
            cp.wait_send()
        total = slots[0]
        for dev in range(1, N_DEV):
            total = total + slots[dev]
        o_ref[...] = total
        o_ref[loss_row:loss_row + 1, :] = jnp.broadcast_to(
            jnp.sum(total[loss_row:loss_row + 1, :], axis=-1, keepdims=True), (1, LANES))

    return pl.pallas_call(
        body, in_specs=[WHOLE_VMEM], out_specs=WHOLE_VMEM, out_shape=jax.ShapeDtypeStruct((rows, LANES), F32),
        scratch_shapes=[pltpu.VMEM((N_DEV, rows, LANES), F32), pltpu.SemaphoreType.DMA((N_DEV - 1,)),
                        pltpu.SemaphoreType.DMA((N_DEV - 1,))],
        name="allreduce_small")(vec)


def _adamw(w, g, m, v, *, tr, emit_grad=False):
    depth, rows, cols = w.shape
    assert rows % tr == 0
    c1 = float(np.float32(1.0 - ADAM_B1 ** ADAM_STEP))
    c2 = float(np.float32(1.0 - ADAM_B2 ** ADAM_STEP))

    def body(w_ref, g_ref, m_ref, v_ref, d_ref, mo_ref, vo_ref, *go_ref):
        g_t = g_ref[...]
        if emit_grad:
            go_ref[0][...] = g_t
        m_new = ADAM_B1 * m_ref[...] + (1.0 - ADAM_B1) * g_t
        v_new = ADAM_B2 * v_ref[...] + (1.0 - ADAM_B2) * (g_t * g_t)
        mo_ref[...] = m_new
        vo_ref[...] = v_new
        d_ref[...] = -ADAM_LR * ((m_new / c1) / (jnp.sqrt(v_new / c2) + ADAM_EPS) + ADAM_WD * w_ref[...])

    blk = pl.BlockSpec((None, tr, cols), lambda l, i: (l, i, 0))
    return pl.pallas_call(
        body, grid=(depth, rows // tr), in_specs=[blk] * 4, out_specs=[blk] * (4 if emit_grad else 3),
        out_shape=[jax.ShapeDtypeStruct(w.shape, F32)] * (4 if emit_grad else 3), name="adamw")(w, g, m, v)


def _local_step(x, positions, target, gains, exchange):
    t = x.shape[0]
    tm = 512
    inv_freq = ROPE_THETA ** (-jnp.arange(0, ROPE_DIM, 2, dtype=F32) / ROPE_DIM)
    lane = np.arange(LANES) % HEAD_DIM
    freq_row = jnp.where(lane < ROPE_DIM, inv_freq[lane % (ROPE_DIM // 2)], 0.0).astype(F32)[None, :]
    cos, sin = _rope_tables(positions.reshape(t, 1), freq_row)

    def hosted(tag, fn, *args, **kwargs):
        *out, got = fn(*args, comm=exchange.host(tag), **kwargs)
        if got is not None:
            exchange.hosted(tag, got)
        return out[0] if len(out) == 1 else out

    saved = []
    h1 = _norm_fwd(x, gains["pre_mix_norm"], 0, tm=tm)
    for l in range(DEPTH):
        proj = hosted(("fwd", l, "in_proj"), _in_proj, h1, exchange.weight("w_in", l), cos, sin, tm=tm)
        attn, lse = hosted(("fwd", l, "attn"), _attn_fwd, proj)
        conv_y = _conv_fwd(proj, exchange.weight("conv_w", l), l)
        merged, mix, x1, h2 = hosted(
            ("fwd", l, "out_proj"), _merge_out_proj, attn, conv_y, gains["attn_out_norm"], gains["conv_out_norm"],
            exchange.weight("w_out", l), x, gains["post_mix_norm"], gains["pre_ffn_norm"], l, tm=tm)
        g, u, act = hosted(("fwd", l, "gate_up"), _gate_up_swiglu, h2, exchange.weight("w_gate_up", l), tm=1024)
        nxt = (gains["pre_mix_norm"], l + 1) if l + 1 < DEPTH else (None, None)
        f, x2, h1_next = hosted(("fwd", l, "down"), _mm_resnorm, act, exchange.weight("w_down", l), x1,
                                gains["post_ffn_norm"], l, *nxt, tm=tm, name="down")
        saved.append(dict(x=x, h1=h1, proj=proj, attn=attn, lse=lse, conv_y=conv_y, merged=merged, mix=mix,
                          x1=x1, h2=h2, g=g, u=u, act=act, f=f))
        x, h1 = x2, h1_next

    dres, loss_lanes = _loss_fwd_bwd(x, target, tm=tm)

    g_gain = {k: [None] * DEPTH for k in gains}
    g_conv = [None] * DEPTH
    _, df, _, g_gain["post_ffn_norm"][DEPTH - 1], _ = _norm_bwd(
        dres, None, (saved[-1]["f"], gains["post_ffn_norm"], DEPTH - 1), tm=tm)
    for l in reversed(range(DEPTH)):
        sv = saved[l]
        w = {k: exchange.weight(k, l) for k in MATRIX_NAMES + ("conv_w",)}
        dg, du = _down_dx_swiglu_bwd(df, w["w_down"], sv["g"], sv["u"], tm=1024, tko=FFN // 2)
        g_down, _ = _mm_tn(sv["act"], df, 1, tka=256, name="down_dw")
        exchange.grads(l, "down", dict(w_down=g_down.reshape(N_CHIPS, FFN // N_CHIPS, D_MODEL)))
        dx1, dmix, g_gain["pre_ffn_norm"][l], g_gain["post_mix_norm"][l] = hosted(
            ("bwd", l, "gate_up_dx"), _gate_up_dx_norms, dg, du, w["w_gate_up"], dres,
            (sv["x1"], gains["pre_ffn_norm"], l), (sv["mix"], gains["post_mix_norm"], l), tm=tm)
        g_gate_up = hosted(("bwd", l, "gate_up_dw"), _mm_tn, sv["h2"], dg, N_CHIPS // 2, tka=512, name="gate_up_dw",
                           into=lax.empty(w["w_gate_up"].shape, F32))
        g_gate_up, _ = _mm_tn(sv["h2"], du, N_CHIPS // 2, tka=512, name="gate_up_dw", into=g_gate_up,
                              shard0=N_CHIPS // 2)
        exchange.grads(l, "gate_up", dict(w_gate_up=g_gate_up))
        g_out, _ = _mm_tn(sv["merged"], dmix, 1, tka=512, name="out_proj_dw")
        d_attn, delta, d_conv_y, g_gain["attn_out_norm"][l], g_gain["conv_out_norm"][l] = hosted(
            ("bwd", l, "merge"), _merge_bwd, dmix, w["w_out"],
            sv["attn"], sv["conv_y"], gains["attn_out_norm"], gains["conv_out_norm"], l, tm=tm)
        d_attn3 = hosted(("bwd", l, "attn"), _attn_bwd, sv["proj"], cos, sin, d_attn, sv["lse"], delta)
        d_conv3, g_conv[l] = _conv_bwd(sv["proj"], w["conv_w"], l, d_conv_y)
        g_in = _in_proj_dw(sv["h1"], d_attn3, d_conv3, N_CHIPS, tka=512)
        exchange.grads(l, "mix", dict(w_out=g_out.reshape(N_CHIPS, D_MODEL // N_CHIPS, D_MODEL), w_in=g_in))
        if l > 0:
            dres, df, g_gain["pre_mix_norm"][l], g_gain["post_ffn_norm"][l - 1] = hosted(
                ("bwd", l, "in_proj_dx"), _in_proj_dx_norms, d_attn3, d_conv3, w["w_in"], dx1,
                (sv["x"], gains["pre_mix_norm"], l), (saved[l - 1]["f"], gains["post_ffn_norm"], l - 1), tm=tm)
        else:
            dh1 = hosted(("bwd", l, "in_proj_dx"), _in_proj_dx, d_attn3, d_conv3, w["w_in"], tm=1024)
            dres, _, g_gain["pre_mix_norm"][l], _ = hosted(
                ("bwd", l, "norm_low"), _norm_bwd, dx1, (dh1, sv["x"], gains["pre_mix_norm"], l), None, tm=tm)

    g_gain = {k: jnp.concatenate(v, axis=0) for k, v in g_gain.items()}
    return loss_lanes, dres, g_gain, jnp.stack(g_conv, axis=0)


class _Exchange:
    GATHER_HOSTS = {"in_proj": (("w_out", 0),), "attn": (("w_gate_up", 0),), "gate_up": (("w_down", 0),),
                    "down": (("w_in", 1),)}

    @staticmethod
    def _reduce_hosts(group, l):
        if group == "down":
            return "gate_up_dw", "attn", l
        if group == "gate_up":
            return "merge", "attn", l
        if l > 0:
            return "in_proj_dx", "gate_up_dx", l - 1
        return "in_proj_dx", "norm_low", l

    def __init__(self, params, place):
        self.place = place
        self.slabs = {k: [_own_shard_slab(params[k], l, place, BF16) for l in range(DEPTH)] for k in MATRIX_NAMES}
        self.gathered = {k: [None] * DEPTH for k in MATRIX_NAMES}
        self.gathered["w_in"][0], self.conv_w = _run_comm(
            _gather_comm([self.slabs["w_in"][0]], _own_conv_slab(params["conv_w"], place)), "gather_first")
        self.full = {k: lax.empty(params[k].shape, F32) for k in MATRIX_NAMES}
        self.pending = {}
        self.raw = {}

    def weight(self, name, l):
        if name == "conv_w":
            return self.conv_w
        g = self.gathered[name][l]
        return g.reshape(1, g.shape[0] * g.shape[1], g.shape[2]) if name in ("w_out", "w_down") else g

    def host(self, tag):
        phase, l, kernel = tag
        if phase == "fwd":
            carried = [(name, l + ahead) for name, ahead in self.GATHER_HOSTS.get(kernel, ()) if l + ahead < DEPTH]
            return _gather_comm([self.slabs[name][layer] for name, layer in carried]) if carried else None
        if tag in self.pending:
            entries = self.pending[tag]
            arrays = [a for entry in entries for a in entry[3]]
            stages = {entry[0] for entry in entries}
            assert len(stages) == 1
            return _halves_comm(arrays) if stages == {"halves"} else _partials_comm(arrays)
        return None

    def hosted(self, tag, results):
        phase, l, kernel = tag
        if phase == "fwd":
            carried = [(name, l + ahead) for name, ahead in self.GATHER_HOSTS[kernel] if l + ahead < DEPTH]
            for (name, layer), slab in zip(carried, results):
                self.gathered[name][layer] = slab
            return
        results = list(results)
        for stage, gl, group, arrays in self.pending.pop(tag):
            mine, results = results[:len(arrays)], results[len(arrays):]
            names = list(self.raw[(gl, group)])
            if stage == "partials":
                self._finish_reduction(gl, names, arrays, mine)
                continue
            partials = [_add_halves(self.raw[(gl, group)][k], r, self.place) for k, r in zip(names, mine)]
            _, ici_kernel, ici_layer = self._reduce_hosts(group, gl)
            self.pending.setdefault(("bwd", ici_layer, ici_kernel), []).append(("partials", gl, group, partials))

    def grads(self, l, group, grads):
        self.raw[(l, group)] = grads
        self.pending.setdefault(("bwd", l, self._reduce_hosts(group, l)[0]), []).append(
            ("halves", l, group, [grads[k] for k in grads]))

    def _finish_reduction(self, l, names, partials, others):
        for k, p, q in zip(names, partials, others):
            self.full[k] = _sum_partials(p, q, self.place, self.full[k], l)

    def reduced(self):
        assert not self.pending
        return dict(zip(MATRIX_NAMES, _run_comm(_share_comm([self.full[k] for k in MATRIX_NAMES]), "share_halves")))


def kernel(x, positions, pre_mix_norm, w_in, conv_w, attn_out_norm, conv_out_norm, w_out, post_mix_norm, pre_ffn_norm, w_gate_up, w_down, post_ffn_norm, loss_target, m_pre_mix_norm, m_w_in, m_conv_w, m_attn_out_norm, m_conv_out_norm, m_w_out, m_post_mix_norm, m_pre_ffn_norm, m_w_gate_up, m_w_down, m_post_ffn_norm, v_pre_mix_norm, v_w_in, v_conv_w, v_attn_out_norm, v_conv_out_norm, v_w_out, v_post_mix_norm, v_pre_ffn_norm, v_w_gate_up, v_w_down, v_post_ffn_norm):
    params = dict(pre_mix_norm=pre_mix_norm, w_in=w_in, conv_w=conv_w, attn_out_norm=attn_out_norm,
                  conv_out_norm=conv_out_norm, w_out=w_out, post_mix_norm=post_mix_norm, pre_ffn_norm=pre_ffn_norm,
                  w_gate_up=w_gate_up, w_down=w_down, post_ffn_norm=post_ffn_norm)
    mom1 = dict(pre_mix_norm=m_pre_mix_norm, w_in=m_w_in, conv_w=m_conv_w, attn_out_norm=m_attn_out_norm,
                conv_out_norm=m_conv_out_norm, w_out=m_w_out, post_mix_norm=m_post_mix_norm,
                pre_ffn_norm=m_pre_ffn_norm, w_gate_up=m_w_gate_up, w_down=m_w_down, post_ffn_norm=m_post_ffn_norm)
    mom2 = dict(pre_mix_norm=v_pre_mix_norm, w_in=v_w_in, conv_w=v_conv_w, attn_out_norm=v_attn_out_norm,
                conv_out_norm=v_conv_out_norm, w_out=v_w_out, post_mix_norm=v_post_mix_norm,
                pre_ffn_norm=v_pre_ffn_norm, w_gate_up=v_w_gate_up, w_down=v_w_down, post_ffn_norm=v_post_ffn_norm)
    xi, yi, ci = lax.axis_index("x"), lax.axis_index("y"), lax.axis_index("c")
    place = jnp.stack([2 * xi + yi, ci]).astype(jnp.int32)

    exchange = _Exchange(params, place)
    gains = {k: params[k][:, None, :] for k in GAIN_NAMES}
    loss_lanes, grad_x, g_gain, g_conv = _local_step(x[0], positions[0], loss_target[0], gains, exchange)
    grad = exchange.reduced()

    small = [g_gain[k].reshape(-1) for k in GAIN_NAMES] + [g_conv.reshape(-1), loss_lanes.reshape(-1)]
    sizes = [int(s.shape[0]) for s in small]
    flat = jnp.concatenate(small)
    loss_row = (sum(sizes) - LANES) // LANES
    rows = -(-flat.shape[0] // (8 * LANES)) * 8
    flat = jnp.pad(flat, (0, rows * LANES - flat.shape[0])).reshape(rows, LANES)
    total = _allreduce_small(flat, loss_row).reshape(-1)
    offsets = np.cumsum([0] + sizes)
    for i, k in enumerate(GAIN_NAMES):
        grad[k] = total[offsets[i]:offsets[i + 1]].reshape(params[k].shape)
    conv_all = total[offsets[6]:offsets[7]].reshape(DEPTH, N_CHIPS, 3, LANES)
    grad["conv_w"] = lax.dynamic_index_in_dim(conv_all, 2 * xi + yi, axis=1, keepdims=False)
    loss = total[offsets[7]]

    delta, new_m, new_v = {}, {}, {}
    for k in WEIGHT_ORDER:
        shape = params[k].shape
        if k in MATRIX_NAMES:
            tr = {1024: 512, 704: 352, 256: 256}[shape[1]]
            delta[k], new_m[k], new_v[k], grad[k] = _adamw(params[k], grad[k], mom1[k], mom2[k], tr=tr, emit_grad=True)
        else:
            as3 = (lambda a: a) if len(shape) == 3 else (lambda a: a[None])
            d, m, v = _adamw(as3(params[k]), as3(grad[k]), as3(mom1[k]), as3(mom2[k]), tr=as3(params[k]).shape[1])
            delta[k], new_m[k], new_v[k] = d.reshape(shape), m.reshape(shape), v.reshape(shape)

    return (loss, grad_x[None], *[grad[k] for k in WEIGHT_ORDER], *[delta[k] for k in WEIGHT_ORDER],
            *[new_m[k] for k in WEIGHT_ORDER], *[new_v[k] for k in WEIGHT_ORDER])
```

```python
import functools
from typing import Callable, NamedTuple

import numpy as np
import jax
import jax.numpy as jnp
from jax import lax
from jax.experimental import pallas as pl
from jax.experimental.pallas import tpu as pltpu

F32 = jnp.float32
BF16 = jnp.bfloat16
MESH = pl.DeviceIdType.MESH

D_MODEL = 1024
ATTN_W = 512
CONV_W = 512
HEAD_DIM = 64
ROPE_DIM = 16
ROPE_THETA = 500000.0
FFN = 2816
DEPTH = 4
RMS_EPS = 1e-6
NEG_INF = -1e30
N_CHIPS = 4
N_DEV = 8
LANES = 128
BF16_ROWS = 16
DILATIONS = (1, 4, 16)
BAND = 64
TQ = 128
WIN = TQ + 2 * BAND
SCALE = HEAD_DIM ** -0.5

ADAM_LR = 0.001
ADAM_B1 = 0.9
ADAM_B2 = 0.999
ADAM_EPS = 1e-08
ADAM_WD = 0.01
ADAM_STEP = 10

GAIN_NAMES = ("pre_mix_norm", "attn_out_norm", "conv_out_norm", "post_mix_norm", "pre_ffn_norm", "post_ffn_norm")
MATRIX_NAMES = ("w_in", "w_out", "w_gate_up", "w_down")
WEIGHT_ORDER = ("pre_mix_norm", "w_in", "conv_w", "attn_out_norm", "conv_out_norm", "w_out", "post_mix_norm",
                "pre_ffn_norm", "w_gate_up", "w_down", "post_ffn_norm")

ANY = pl.BlockSpec(memory_space=pl.ANY)
WHOLE_VMEM = pl.BlockSpec(memory_space=pltpu.VMEM)
LANE_CONTRACT = (((1,), (1,)), ((), ()))
ROW_CONTRACT = (((0,), (0,)), ((), ()))
CHUNK = 256


def _const_spec(block, index):
    return pl.BlockSpec(block, lambda *_: index)


def _gain_spec(g3, l):
    return _const_spec((None, 1, g3.shape[-1]), (l, 0, 0))


class _Comm(NamedTuple):
    ins: tuple
    inouts: tuple
    out_shapes: tuple
    n_sems: int
    start: Callable
    finish: Callable


def _place():
    x, y, c = lax.axis_index("x"), lax.axis_index("y"), lax.axis_index("c")
    other_chips = [(1 - x, y), (x, 1 - y), (1 - x, 1 - y)]
    return x, y, c, other_chips


def _remote(src, dst, send_sem, recv_sem, to):
    return pltpu.make_async_remote_copy(src_ref=src, dst_ref=dst, send_sem=send_sem, recv_sem=recv_sem,
                                        device_id=to, device_id_type=MESH)


def _call(body, operands, *, name, grid, in_specs, out_specs, out_shape, scratch_shapes=(), aliases=None,
          comm=None):
    in_specs, out_specs, out_shape = list(in_specs), list(out_specs), list(out_shape)
    scratch_shapes = list(scratch_shapes)
    aliases = dict(aliases or {})
    if comm is None:
        out = pl.pallas_call(body, grid=grid, in_specs=in_specs, out_specs=out_specs, out_shape=out_shape,
                             scratch_shapes=scratch_shapes, input_output_aliases=aliases, name=name)(*operands)
        return list(out), None
    n_in, n_out, n_scr = len(in_specs), len(out_shape), len(scratch_shapes)
    n_ci, n_cio, n_co = len(comm.ins), len(comm.inouts), len(comm.out_shapes)

    def hosted(*refs):
        refs = list(refs)
        ins, c_ins = refs[:n_in], refs[n_in:n_in + n_ci]
        base = n_in + n_ci + n_cio
        outs = refs[base:base + n_out]
        c_io = refs[base + n_out:base + n_out + n_cio]
        c_out = refs[base + n_out + n_cio:base + n_out + n_cio + n_co]
        scr = refs[base + n_out + n_cio + n_co:]
        send_sems, recv_sems = scr[n_scr], scr[n_scr + 1]
        if grid:
            first = functools.reduce(jnp.logical_and, [pl.program_id(a) == 0 for a in range(len(grid))])
            last = functools.reduce(jnp.logical_and, [pl.program_id(a) == grid[a] - 1 for a in range(len(grid))])
            pl.when(first)(lambda: comm.start(c_ins, c_io, c_out, send_sems, recv_sems))
            body(*ins, *outs, *scr[:n_scr])
            pl.when(last)(lambda: comm.finish(c_ins, c_io, c_out, send_sems, recv_sems))
        else:
            comm.start(c_ins, c_io, c_out, send_sems, recv_sems)
            body(*ins, *outs, *scr[:n_scr])
            comm.finish(c_ins, c_io, c_out, send_sems, recv_sems)

    res = pl.pallas_call(
        hosted, grid=grid, in_specs=in_specs + [ANY] * (n_ci + n_cio), out_specs=out_specs + [ANY] * (n_cio + n_co),
        out_shape=out_shape + [jax.ShapeDtypeStruct(a.shape, a.dtype) for a in comm.inouts] + list(comm.out_shapes),
        input_output_aliases={**aliases, **{n_in + n_ci + i: n_out + i for i in range(n_cio)}},
        scratch_shapes=scratch_shapes + [pltpu.SemaphoreType.DMA((comm.n_sems,))] * 2,
        name=name)(*operands, *comm.ins, *comm.inouts)
    return list(res[:n_out]), list(res[n_out:])


def _run_comm(comm, name):
    return _call(lambda: None, [], name=name, grid=(), in_specs=[], out_specs=[], out_shape=[], comm=comm)[1]


def _row_half(ref, lead, core, rows, align):
    hr = rows // 2
    return ref.at[(*lead, pl.ds(pl.multiple_of(core * hr, align), hr), slice(None))]


def _gather_comm(slabs, conv_slab=None):
    n = len(slabs)
    n_conv = 0 if conv_slab is None else 3

    def direct(ios, send, recv):
        x, y, c, chips = _place()
        copies = []
        for a in range(n):
            own = _row_half(ios[a], (2 * x + y,), c, slabs[a].shape[1], BF16_ROWS)
            copies += [_remote(own, own, send.at[a * 3 + j], recv.at[a * 3 + j], (*chip, c))
                       for j, chip in enumerate(chips)]
        if conv_slab is not None:
            own = ios[n].at[:, 2 * x + y]
            copies += [_remote(own, own, send.at[6 * n + j], recv.at[6 * n + j], (*chip, c))
                       for j, chip in enumerate(chips)]
        return copies

    def start(ins, ios, outs, send, recv):
        for cp in direct(ios, send, recv):
            cp.start()

    def finish(ins, ios, outs, send, recv):
        x, y, c, chips = _place()
        sibling = (x, y, 1 - c)
        passed = []
        for a in range(n):
            for j, chip in enumerate(chips):
                landed = _row_half(ios[a], (2 * chip[0] + chip[1],), c, slabs[a].shape[1], BF16_ROWS)
                _remote(landed, landed, send.at[a * 3 + j], recv.at[a * 3 + j], (*chip, c)).wait_recv()
                fwd = _remote(landed, landed, send.at[3 * n + a * 3 + j], recv.at[3 * n + a * 3 + j], sibling)
                fwd.start()
                passed.append(fwd)
        if conv_slab is not None:
            for j, chip in enumerate(chips):
                landed = ios[n].at[:, 2 * chip[0] + chip[1]]
                _remote(landed, landed, send.at[6 * n + j], recv.at[6 * n + j], (*chip, c)).wait_recv()
        for a in range(n):
            for j, chip in enumerate(chips):
                landed = _row_half(ios[a], (2 * chip[0] + chip[1],), 1 - c, slabs[a].shape[1], BF16_ROWS)
                _remote(landed, landed, send.at[3 * n + a * 3 + j], recv.at[3 * n + a * 3 + j], sibling).wait_recv()
        for cp in direct(ios, send, recv) + passed:
            cp.wait_send()

    inouts = tuple(slabs) + (() if conv_slab is None else (conv_slab,))
    return _Comm((), inouts, (), 6 * n + n_conv, start, finish)


def _halves_comm(grads):
    n = len(grads)

    def copies(ins, outs, send, recv):
        x, y, c, _ = _place()
        return [_remote(_row_half(ins[a], (slice(None),), 1 - c, grads[a].shape[1], 8), outs[a],
                        send.at[a], recv.at[a], (x, y, 1 - c)) for a in range(n)]

    def start(ins, ios, outs, send, recv):
        for cp in copies(ins, outs, send, recv):
            cp.start()

    def finish(ins, ios, outs, send, recv):
        for cp in copies(ins, outs, send, recv):
            cp.wait()

    out_shapes = tuple(jax.ShapeDtypeStruct((g.shape[0], g.shape[1] // 2, g.shape[2]), F32) for g in grads)
    return _Comm(tuple(grads), (), out_shapes, n, start, finish)


def _partials_comm(partials):
    n = len(partials)

    def copies(ins, outs, send, recv):
        x, y, c, chips = _place()
        return [_remote(ins[a].at[2 * chip[0] + chip[1]], outs[a].at[k], send.at[a * 3 + k], recv.at[a * 3 + k],
                        (*chip, c)) for a in range(n) for k, chip in enumerate(chips)]

    def start(ins, ios, outs, send, recv):
        for cp in copies(ins, outs, send, recv):
            cp.start()

    def finish(ins, ios, outs, send, recv):
        for cp in copies(ins, outs, send, recv):
            cp.wait()

    out_shapes = tuple(jax.ShapeDtypeStruct((3,) + p.shape[1:], BF16) for p in partials)
    return _Comm(tuple(partials), (), out_shapes, 3 * n, start, finish)


def _share_comm(grads):
    n = len(grads)

    def start(ins, ios, outs, send, recv):
        x, y, c, _ = _place()
        for a in range(n):
            mine = _row_half(ios[a], (slice(None),), c, grads[a].shape[1], 8)
            _remote(mine, mine, send.at[a], recv.at[a], (x, y, 1 - c)).start()

    def finish(ins, ios, outs, send, recv):
        x, y, c, _ = _place()
        for a in range(n):
            theirs = _row_half(ios[a], (slice(None),), 1 - c, grads[a].shape[1], 8)
            _remote(theirs, theirs, send.at[a], recv.at[a], (x, y, 1 - c)).wait()

    return _Comm((), tuple(grads), (), n, start, finish)


def _rms_fwd(x, g):
    r = lax.rsqrt(jnp.mean(x * x, axis=-1, keepdims=True) + RMS_EPS)
    return (x * r) * g


def _rms_bwd(x, g, dy):
    r = lax.rsqrt(jnp.mean(x * x, axis=-1, keepdims=True) + RMS_EPS)
    xh = x * r
    u = dy * g
    dx = r * (u - xh * jnp.mean(xh * u, axis=-1, keepdims=True))
    return dx, jnp.sum(dy * xh, axis=0, keepdims=True)


def _accumulate(ref, value, first):
    @pl.when(first)
    def _():
        ref[...] = value

    @pl.when(jnp.logical_not(first))
    def _():
        ref[...] += value


def _rope_coeffs(cos, sin):
    m = lax.broadcasted_iota(jnp.int32, cos.shape, 1) % HEAD_DIM
    a = jnp.where(m < ROPE_DIM, cos, 1.0)
    b = jnp.where(m < ROPE_DIM // 2, -sin, 0.0)
    c = jnp.where((m >= ROPE_DIM // 2) & (m < ROPE_DIM), sin, 0.0)
    return a, b, c


def _rope_apply(t, cos, sin):
    a, b, c = _rope_coeffs(cos, sin)
    n = t.shape[1]
    return a * t + b * pltpu.roll(t, n - ROPE_DIM // 2, 1) + c * pltpu.roll(t, ROPE_DIM // 2, 1)


def _rope_transpose(dt, cos, sin):
    a, b, c = _rope_coeffs(cos, sin)
    n = dt.shape[1]
    return a * dt + pltpu.roll(b * dt, ROPE_DIM // 2, 1) + pltpu.roll(c * dt, n - ROPE_DIM // 2, 1)


def _in_proj(h, w, cos, sin, *, tm, comm=None):
    t, k = h.shape
    s_n, _, n = w.shape
    assert t % tm == 0 and n % LANES == 0

    rotary_shards = -(-2 * ATTN_W // n)
    rotary = list(range(0, 2 * ATTN_W, LANES))
    per_shard = -(-len(rotary) // max(s_n - rotary_shards, 1))

    def body(h_ref, w_ref, cos_ref, sin_ref, o_ref):
        def shard(s):
            o_ref[:, s * n:(s + 1) * n] = jnp.dot(h_ref[...], w_ref[s], preferred_element_type=F32)

        def rope(c0):
            cols = slice(c0, c0 + LANES)
            o_ref[:, cols] = _rope_apply(o_ref[:, cols], cos_ref[...], sin_ref[...])

        for s in range(rotary_shards):
            shard(s)
        pending = list(rotary)
        for s in range(rotary_shards, s_n):
            for c0 in pending[:per_shard]:
                rope(c0)
            pending = pending[per_shard:]
            shard(s)
        for c0 in pending:
            rope(c0)

    lane_tile = pl.BlockSpec((tm, LANES), lambda i: (i, 0))
    out, got = _call(
        body, [h, w, cos, sin], name="in_proj", grid=(t // tm,),
        in_specs=[pl.BlockSpec((tm, k), lambda i: (i, 0)), _const_spec(w.shape, (0, 0, 0)), lane_tile, lane_tile],
        out_specs=[pl.BlockSpec((tm, s_n * n), lambda i: (i, 0))],
        out_shape=[jax.ShapeDtypeStruct((t, s_n * n), F32)], comm=comm)
    return out[0], got


def _dx_through_norms(operands, in_specs, dx_rows, dres, pre, post, *, tm, name, comm=None):
    t, d_model = dres.shape
    (x, gx3, lx), (y, gy3, ly) = pre, post
    n_op = len(operands)
    row = pl.BlockSpec((tm, d_model), lambda i: (i, 0))
    gsum = _const_spec((1, d_model), (0, 0))

    def body(*refs):
        d_ref, x_ref, gx_ref, y_ref, gy_ref, dn_ref, dgx_ref, dy_ref, dgy_ref = refs[n_op:]
        first = pl.program_id(0) == 0
        dx, dgx = _rms_bwd(x_ref[...], gx_ref[...], dx_rows(refs[:n_op]))
        d_new = d_ref[...] + dx
        dn_ref[...] = d_new
        _accumulate(dgx_ref, dgx, first)
        dy, dgy = _rms_bwd(y_ref[...], gy_ref[...], d_new)
        dy_ref[...] = dy.astype(BF16)
        _accumulate(dgy_ref, dgy, first)

    out, got = _call(
        body, list(operands) + [dres, x, gx3, y, gy3], name=name, grid=(t // tm,),
        in_specs=list(in_specs) + [row, row, _gain_spec(gx3, lx), row, _gain_spec(gy3, ly)],
        out_specs=[row, gsum, row, gsum],
        out_shape=[jax.ShapeDtypeStruct((t, d_model), F32), jax.ShapeDtypeStruct((1, d_model), F32),
                   jax.ShapeDtypeStruct((t, d_model), BF16), jax.ShapeDtypeStruct((1, d_model), F32)],
        comm=comm)
    return out[0], out[2], out[1], out[3], got


def _gate_up_dx_norms(dg, du, w, dres, pre, post, *, tm, comm=None):
    s_n, ko, n = w.shape
    half = s_n // 2

    def dx_rows(refs):
        dg_ref, du_ref, w_ref = refs
        acc = jnp.zeros((tm, ko), F32)
        for s in range(half):
            cols = slice(s * n, (s + 1) * n)
            acc = acc + lax.dot_general(dg_ref[:, cols], w_ref[s], LANE_CONTRACT, preferred_element_type=F32)
            acc = acc + lax.dot_general(du_ref[:, cols], w_ref[half + s], LANE_CONTRACT, preferred_element_type=F32)
        return acc

    a_spec = pl.BlockSpec((tm, half * n), lambda i: (i, 0))
    return _dx_through_norms([dg, du, w], [a_spec, a_spec, _const_spec(w.shape, (0, 0, 0))], dx_rows, dres, pre, post,
                             tm=tm, name="gate_up_dx", comm=comm)


def _in_proj_dx_norms(d_attn3, d_conv3, w, dres, pre, post, *, tm, comm=None):
    s_n, ko, n = w.shape
    per = n // CHUNK
    pieces = d_attn3.shape[0]
    width = d_attn3.shape[2]

    def dx_rows(refs):
        a_ref, b_ref, w_ref = refs
        acc = jnp.zeros((tm, ko), F32)
        for c in range(s_n * per):
            src = a_ref if c // 2 < pieces else b_ref
            piece, c0 = (c // 2) % pieces, (c % 2) * CHUNK
            acc = acc + lax.dot_general(src[piece, :, c0:c0 + CHUNK], w_ref[c // per, :, (c % per) * CHUNK:(c % per + 1) * CHUNK],
                                        LANE_CONTRACT, preferred_element_type=F32)
        return acc

    stack = pl.BlockSpec((pieces, tm, width), lambda i: (0, i, 0))
    return _dx_through_norms([d_attn3, d_conv3, w], [stack, stack, _const_spec(w.shape, (0, 0, 0))], dx_rows, dres, pre,
                             post, tm=tm, name="in_proj_dx", comm=comm)


def _mm_tn(a, b, s_n, *, tka, name, into=None, shard0=0, comm=None):
    t, ka = a.shape
    n = b.shape[1] // s_n
    assert b.shape[0] == t and ka % tka == 0

    def body(a_ref, b_ref, *rest):
        rest[-1][...] = lax.dot_general(a_ref[...], b_ref[...], ROW_CONTRACT, preferred_element_type=F32)

    operands, in_specs, aliases = [a, b], [pl.BlockSpec((t, tka), lambda i, s: (0, i)),
                                           pl.BlockSpec((t, n), lambda i, s: (0, s))], {}
    out_shape = jax.ShapeDtypeStruct((s_n, ka, n), F32)
    if into is not None:
        operands, in_specs, aliases = operands + [into], in_specs + [ANY], {2: 0}
        out_shape = jax.ShapeDtypeStruct(into.shape, F32)
    out, got = _call(body, operands, name=name, grid=(ka // tka, s_n), in_specs=in_specs,
                     out_specs=[pl.BlockSpec((None, tka, n), lambda i, s: (shard0 + s, i, 0))],
                     out_shape=[out_shape], aliases=aliases, comm=comm)
    return out[0], got


def _gate_up_swiglu(h, w, *, tm, comm=None):
    t, k = h.shape
    s_n, _, n = w.shape
    half = s_n // 2

    def body(h_ref, wg_ref, wu_ref, g_ref, u_ref, a_ref):
        g = jnp.dot(h_ref[...], wg_ref[...], preferred_element_type=F32)
        u = jnp.dot(h_ref[...], wu_ref[...], preferred_element_type=F32)
        g_ref[...] = g.astype(BF16)
        u_ref[...] = u.astype(BF16)
        a_ref[...] = (g * jax.nn.sigmoid(g) * u).astype(BF16)

    col = pl.BlockSpec((tm, n), lambda i, j: (i, j))
    out, got = _call(
        body, [h, w, w], name="gate_up", grid=(t // tm, half),
        in_specs=[pl.BlockSpec((tm, k), lambda i, j: (i, 0)), pl.BlockSpec((None, k, n), lambda i, j: (j, 0, 0)),
                  pl.BlockSpec((None, k, n), lambda i, j: (half + j, 0, 0))],
        out_specs=[col, col, col],
        out_shape=[jax.ShapeDtypeStruct((t, half * n), BF16)] * 3, comm=comm)
    return out[0], out[1], out[2], got


def _down_dx_swiglu_bwd(df, w, g, u, *, tm, tko):
    t, k = df.shape
    _, ko, _ = w.shape
    assert t % tm == 0 and ko % tko == 0

    row_parts = 4

    def body(df_ref, w_ref, g_ref, u_ref, dg_ref, du_ref):
        for part in range(row_parts):
            rows = slice(part * tm // row_parts, (part + 1) * tm // row_parts)
            d = lax.dot_general(df_ref[rows, :], w_ref[...], LANE_CONTRACT, preferred_element_type=F32)
            gg = g_ref[rows, :].astype(F32)
            sig = jax.nn.sigmoid(gg)
            dg_ref[rows, :] = (d * u_ref[rows, :].astype(F32) * (sig * (1.0 + gg * (1.0 - sig)))).astype(BF16)
            du_ref[rows, :] = (d * (gg * sig)).astype(BF16)

    col = pl.BlockSpec((tm, tko), lambda i, j: (i, j))
    return pl.pallas_call(
        body, grid=(t // tm, ko // tko),
        in_specs=[pl.BlockSpec((tm, k), lambda i, j: (i, 0)), pl.BlockSpec((None, tko, k), lambda i, j: (0, j, 0)),
                  col, col],
        out_specs=[col, col], out_shape=[jax.ShapeDtypeStruct((t, ko), BF16)] * 2, name="down_dx")(df, w, g, u)


def _in_proj_dx(d_attn3, d_conv3, w, *, tm, comm=None):
    _, t, _ = d_attn3.shape
    s_n, ko, n = w.shape
    half, per = s_n // 2, n // CHUNK
    assert t % tm == 0

    def body(*refs):
        a_refs, b_refs, wa_ref, wb_ref, o_ref = refs[:per], refs[per:2 * per], refs[2 * per], refs[2 * per + 1], refs[-1]
        acc = jnp.zeros(o_ref.shape, F32)
        for r in range(per):
            cols = slice(r * CHUNK, (r + 1) * CHUNK)
            acc = acc + lax.dot_general(a_refs[r][...], wa_ref[:, cols], LANE_CONTRACT, preferred_element_type=F32)
            acc = acc + lax.dot_general(b_refs[r][...], wb_ref[:, cols], LANE_CONTRACT, preferred_element_type=F32)
        _accumulate(o_ref, acc, pl.program_id(1) == 0)

    piece = lambda r: pl.BlockSpec((None, tm, CHUNK), lambda i, s: ((per * s + r) // 2, i, (per * s + r) % 2))
    out, got = _call(
        body, [d_attn3] * per + [d_conv3] * per + [w, w], name="in_proj_dx", grid=(t // tm, half),
        in_specs=[piece(r) for r in range(per)] * 2
        + [pl.BlockSpec((None, ko, n), lambda i, s: (s, 0, 0)), pl.BlockSpec((None, ko, n), lambda i, s: (half + s, 0, 0))],
        out_specs=[pl.BlockSpec((tm, ko), lambda i, s: (i, 0))],
        out_shape=[jax.ShapeDtypeStruct((t, ko), F32)], comm=comm)
    return out[0], got


def _in_proj_dw(h, d_attn3, d_conv3, s_n, *, tka):
    t, ka = h.shape
    half = s_n // 2
    n = 3 * d_attn3.shape[2] // half
    per = n // CHUNK
    assert ka % tka == 0

    def body(*refs):
        h_ref, o_ref = refs[0], refs[-1]
        for side in range(2):
            for r in range(per):
                o_ref[side, :, r * CHUNK:(r + 1) * CHUNK] = lax.dot_general(
                    h_ref[...], refs[1 + side * per + r][...], ROW_CONTRACT, preferred_element_type=F32)

    piece = lambda r: pl.BlockSpec((None, t, CHUNK), lambda i, s: ((per * s + r) // 2, 0, (per * s + r) % 2))
    out = pl.pallas_call(
        body, grid=(ka // tka, half),
        in_specs=[pl.BlockSpec((t, tka), lambda i, s: (0, i))] + [piece(r) for r in range(per)] * 2,
        out_specs=pl.BlockSpec((2, None, tka, n), lambda i, s: (0, s, i, 0)),
        out_shape=jax.ShapeDtypeStruct((2, half, ka, n), F32), name="in_proj_dw")(h, *[d_attn3] * per, *[d_conv3] * per)
    return out.reshape(s_n, ka, n)


def _rope_tables(positions_col, inv_freq_row):
    t = positions_col.shape[0]

    def body(pos_ref, f_ref, cos_ref, sin_ref):
        ang = pos_ref[...].astype(F32) * f_ref[...]
        cos_ref[...] = jnp.cos(ang)
        sin_ref[...] = jnp.sin(ang)

    return pl.pallas_call(
        body, out_shape=[jax.ShapeDtypeStruct((t, LANES), F32)] * 2, name="rope_tables")(positions_col, inv_freq_row)


def _norm_fwd(x, g3, l, *, tm):
    t, w = x.shape

    def body(x_ref, g_ref, h_ref):
        h_ref[...] = _rms_fwd(x_ref[...], g_ref[...]).astype(BF16)

    return pl.pallas_call(
        body, grid=(t // tm,),
        in_specs=[pl.BlockSpec((tm, w), lambda i: (i, 0)), _gain_spec(g3, l)],
        out_specs=pl.BlockSpec((tm, w), lambda i: (i, 0)),
        out_shape=jax.ShapeDtypeStruct((t, w), BF16), name="norm_fwd")(x, g3)


def _mm_resnorm(a, w, x, g_post3, l_post, g_next3, l_next, *, tm, name, comm=None):
    t, k = a.shape
    _, _, n = w.shape
    with_next = g_next3 is not None
    row = pl.BlockSpec((tm, n), lambda i: (i, 0))

    def body(a_ref, w_ref, x_ref, gp_ref, *rest):
        y = jnp.dot(a_ref[...], w_ref[...], preferred_element_type=F32)
        x_new = x_ref[...] + _rms_fwd(y, gp_ref[...])
        if with_next:
            gn_ref, y_ref, xo_ref, h_ref = rest
            h_ref[...] = _rms_fwd(x_new, gn_ref[...]).astype(BF16)
        else:
            y_ref, xo_ref = rest
        y_ref[...] = y
        xo_ref[...] = x_new

    ins = [a, w, x, g_post3] + ([g_next3] if with_next else [])
    in_specs = ([pl.BlockSpec((tm, k), lambda i: (i, 0)), _const_spec((None, k, n), (0, 0, 0)), row,
                 _gain_spec(g_post3, l_post)] + ([_gain_spec(g_next3, l_next)] if with_next else []))
    out_shape = [jax.ShapeDtypeStruct((t, n), F32)] * 2 + ([jax.ShapeDtypeStruct((t, n), BF16)] if with_next else [])
    out, got = _call(body, ins, name=name, grid=(t // tm,), in_specs=in_specs, out_specs=[row] * len(out_shape),
                     out_shape=out_shape, comm=comm)
    return out[0], out[1], (out[2] if with_next else None), got


def _conv_fwd(proj, conv_w, l):
    t = proj.shape[0]
    col0 = 3 * ATTN_W // LANES

    def body(u_ref, gb_ref, gc_ref, w_ref, y_ref):
        c = gc_ref[...] * u_ref[...]
        row = lax.broadcasted_iota(jnp.int32, c.shape, 0)
        c_prev = jnp.where(row == 0, 0.0, pltpu.roll(c, 1, 0))
        c_next = jnp.where(row == t - 1, 0.0, pltpu.roll(c, t - 1, 0))
        w = w_ref[...]
        y_ref[...] = gb_ref[...] * (w[0:1] * c_prev + w[1:2] * c + w[2:3] * c_next)

    nj = CONV_W // LANES
    cols = lambda base: pl.BlockSpec((t, LANES), lambda j: (0, base + j))
    return pl.pallas_call(
        body, grid=(nj,),
        in_specs=[cols(col0), cols(col0 + nj), cols(col0 + 2 * nj),
                  pl.BlockSpec((None, None, 3, LANES), lambda j: (l, j, 0, 0))],
        out_specs=pl.BlockSpec((t, LANES), lambda j: (0, j)),
        out_shape=jax.ShapeDtypeStruct((t, CONV_W), F32), name="conv_fwd")(proj, proj, proj, conv_w)


def _merge_out_proj(attn, conv_y, ga3, gc3, w, x, g_post3, g_next3, l, *, tm, comm=None):
    t = attn.shape[0]
    _, k, n = w.shape
    half = pl.BlockSpec((tm, ATTN_W), lambda i: (i, 0))
    row = pl.BlockSpec((tm, n), lambda i: (i, 0))

    def body(a_ref, c_ref, ga_ref, gc_ref, w_ref, x_ref, gp_ref, gn_ref, m_ref, y_ref, xo_ref, h_ref):
        m_ref[:, :ATTN_W] = _rms_fwd(a_ref[...], ga_ref[...]).astype(BF16)
        m_ref[:, ATTN_W:] = _rms_fwd(c_ref[...], gc_ref[...]).astype(BF16)
        y = jnp.dot(m_ref[...], w_ref[...], preferred_element_type=F32)
        x_new = x_ref[...] + _rms_fwd(y, gp_ref[...])
        h_ref[...] = _rms_fwd(x_new, gn_ref[...]).astype(BF16)
        y_ref[...] = y
        xo_ref[...] = x_new

    out, got = _call(
        body, [attn, conv_y, ga3, gc3, w, x, g_post3, g_next3], name="out_proj", grid=(t // tm,),
        in_specs=[half, half, _gain_spec(ga3, l), _gain_spec(gc3, l), _const_spec((None, k, n), (0, 0, 0)), row,
                  _gain_spec(g_post3, l), _gain_spec(g_next3, l)],
        out_specs=[pl.BlockSpec((tm, k), lambda i: (i, 0)), row, row, row],
        out_shape=[jax.ShapeDtypeStruct((t, k), BF16), jax.ShapeDtypeStruct((t, n), F32),
                   jax.ShapeDtypeStruct((t, n), F32), jax.ShapeDtypeStruct((t, n), BF16)], comm=comm)
    return out[0], out[1], out[2], out[3], got


def _loss_fwd_bwd(y, target, *, tm):
    t, w = y.shape
    row = pl.BlockSpec((tm, w), lambda i: (i, 0))

    def body(y_ref, t_ref, dy_ref, loss_ref):
        e = y_ref[...] - t_ref[...]
        dy_ref[...] = e * (1.0 / w)
        sq = jnp.sum(e * e, axis=0, keepdims=True) * (0.5 / w)
        part = sq[:, :LANES]
        for j in range(1, w // LANES):
            part = part + sq[:, j * LANES:(j + 1) * LANES]
        _accumulate(loss_ref, part, pl.program_id(0) == 0)

    return pl.pallas_call(
        body, grid=(t // tm,), in_specs=[row, row],
        out_specs=[row, _const_spec((1, LANES), (0, 0))],
        out_shape=[jax.ShapeDtypeStruct((t, w), F32), jax.ShapeDtypeStruct((1, LANES), F32)], name="loss")(y, target)


def _tile_rows(t, nt, lb, d):
    r = t // nt
    q0 = (t % nt) * TQ
    m0 = jnp.clip(q0 - BAND, 0, lb - WIN)
    if d == 1:
        return pl.ds(pl.multiple_of(q0, TQ), TQ), pl.ds(pl.multiple_of(m0, BAND), WIN), m0 - q0
    return pl.ds(r + d * q0, TQ, stride=d), pl.ds(r + d * m0, WIN, stride=d), m0 - q0


def _for_row_chunks(t, fn, chunk=512):
    def step(i, carry):
        fn(pl.ds(pl.multiple_of(i * chunk, chunk), chunk))
        return carry

    lax.fori_loop(0, t // chunk, step, 0)


WINDOW_OFFSETS = (-BAND, 0, -2 * BAND)


def _fill_band_bias(bias_ref):
    rel0 = (lax.broadcasted_iota(jnp.int32, (2 * TQ, WIN), 1)
            - lax.broadcasted_iota(jnp.int32, (2 * TQ, WIN), 0) % TQ)
    for j, off in enumerate(WINDOW_OFFSETS):
        rel = rel0 + off
        bias_ref[j] = jnp.where((rel >= -BAND) & (rel <= BAND), 0.0, NEG_INF)


def _fill_sequence_bias(bias_ref):
    rel = (lax.broadcasted_iota(jnp.int32, (2 * WIN, WIN), 1) - lax.broadcasted_iota(jnp.int32, (2 * WIN, WIN), 0) % WIN)
    bias_ref[...] = jnp.where((rel >= -BAND) & (rel <= BAND), 0.0, NEG_INF)


def _band_bias(bias_ref, off):
    return bias_ref[jnp.where(off == WINDOW_OFFSETS[0], 0, jnp.where(off == WINDOW_OFFSETS[1], 1, 2))]


def _stack_heads(a, first_head):
    return jnp.concatenate([jnp.where(first_head, a, 0.0), jnp.where(first_head, 0.0, a)], axis=0)


def _unstack_heads(a2, first_head):
    n = a2.shape[0] // 2
    return jnp.where(first_head, a2[:n], a2[n:])


def _attn_fwd(proj, comm=None):
    t = proj.shape[0]
    npair = ATTN_W // LANES

    def body(q_ref, k_ref, v_ref, o_ref, lse_ref, o1, o2, l0, l1, l2, m1, m2, bias, bias_seq):
        _fill_band_bias(bias)
        _fill_sequence_bias(bias_seq)
        outs, dens, maxs = (o_ref, o1, o2), (l0, l1, l2), (lse_ref, m1, m2)

        def softmax_tile(b, qrows, krows, n_q, band_bias):
            first_head = lax.broadcasted_iota(jnp.int32, (n_q, LANES), 1) < HEAD_DIM
            q2 = _stack_heads(q_ref[qrows, :] * SCALE, first_head).astype(BF16)
            kw = k_ref[krows, :].astype(BF16)
            vw = jnp.concatenate([v_ref[krows, :].astype(BF16), jnp.ones((WIN, LANES), BF16)], axis=1)
            s = lax.dot_general(q2, kw, LANE_CONTRACT, preferred_element_type=F32) + band_bias
            m = jnp.max(s, axis=-1, keepdims=True)
            pv = jnp.dot(jnp.exp(s - m).astype(BF16), vw, preferred_element_type=F32)
            outs[b][qrows, :] = _unstack_heads(pv[:, :LANES], first_head)
            dens[b][qrows, :] = _unstack_heads(pv[:, LANES:], first_head)
            maxs[b][qrows, :] = _unstack_heads(jnp.broadcast_to(m, (2 * n_q, LANES)), first_head)

        for b, d in enumerate(DILATIONS):
            lb = t // d
            if lb == WIN:
                def sequence(r, carry, b=b, d=d):
                    rows = pl.ds(r, WIN, stride=d)
                    softmax_tile(b, rows, rows, WIN, bias_seq[...])
                    return carry

                lax.fori_loop(0, d, sequence, 0, unroll=8)
                continue
            nt = lb // TQ

            def tile(ti, carry, b=b, d=d, lb=lb, nt=nt):
                qrows, krows, off = _tile_rows(ti, nt, lb, d)
                softmax_tile(b, qrows, krows, TQ, _band_bias(bias, off))
                return carry

            lax.fori_loop(0, d * nt, tile, 0, unroll=16)

        def finish(rows):
            ms = [m_b[rows, :] for m_b in maxs]
            m_all = jnp.maximum(jnp.maximum(ms[0], ms[1]), ms[2])
            ws = [jnp.exp(m_b - m_all) for m_b in ms]
            den = ws[0] * dens[0][rows, :] + ws[1] * dens[1][rows, :] + ws[2] * dens[2][rows, :]
            num = ws[0] * outs[0][rows, :] + ws[1] * outs[1][rows, :] + ws[2] * outs[2][rows, :]
            o_ref[rows, :] = num / den
            lse_ref[rows, :] = m_all + jnp.log(den)

        _for_row_chunks(t, finish, 256)

    cols = lambda base: pl.BlockSpec((t, LANES), lambda g: (0, base + g))
    out, got = _call(
        body, [proj, proj, proj], name="attn_fwd", grid=(npair,),
        in_specs=[cols(0), cols(npair), cols(2 * npair)],
        out_specs=[cols(0), cols(0)],
        out_shape=[jax.ShapeDtypeStruct((t, ATTN_W), F32)] * 2,
        scratch_shapes=[pltpu.VMEM((t, LANES), F32)] * 7 + [pltpu.VMEM((len(WINDOW_OFFSETS), 2 * TQ, WIN), F32),
                                                            pltpu.VMEM((2 * WIN, WIN), F32)],
        comm=comm)
    return out[0], out[1], got


def _attn_bwd(proj, cos, sin, d_attn, lse, delta, comm=None):
    t = proj.shape[0]
    npair = ATTN_W // LANES

    def body(q_ref, k_ref, v_ref, cos_ref, sin_ref, do_ref, l_ref, dl_ref, dqkv_ref,
             dq_acc, dk_acc, dv_acc, bias, bias_seq):
        _fill_band_bias(bias)
        _fill_sequence_bias(bias_seq)
        dq_acc[...] = jnp.zeros(dq_acc.shape, F32)
        dk_acc[...] = jnp.zeros(dk_acc.shape, F32)
        dv_acc[...] = jnp.zeros(dv_acc.shape, F32)
        def stack_column(a):
            return jnp.concatenate([a[:, 0:1], a[:, HEAD_DIM:HEAD_DIM + 1]], axis=0)

        def grad_tile(qrows, krows, n_q, band_bias):
            first_head = lax.broadcasted_iota(jnp.int32, (n_q, LANES), 1) < HEAD_DIM
            q2 = _stack_heads(q_ref[qrows, :] * SCALE, first_head).astype(BF16)
            do2 = _stack_heads(do_ref[qrows, :], first_head).astype(BF16)
            kw = k_ref[krows, :].astype(BF16)
            vw = v_ref[krows, :].astype(BF16)
            s = lax.dot_general(q2, kw, LANE_CONTRACT, preferred_element_type=F32) + band_bias
            p = jnp.exp(s - stack_column(l_ref[qrows, :]))
            dp = lax.dot_general(do2, vw, LANE_CONTRACT, preferred_element_type=F32)
            ds = (p * (dp - stack_column(dl_ref[qrows, :]))).astype(BF16)
            dq2 = jnp.dot(ds, kw, preferred_element_type=F32)
            dq_acc[qrows, :] += _unstack_heads(dq2, first_head) * SCALE
            dk_acc[krows, :] += lax.dot_general(ds, q2, ROW_CONTRACT, preferred_element_type=F32)
            dv_acc[krows, :] += lax.dot_general(p.astype(BF16), do2, ROW_CONTRACT, preferred_element_type=F32)

        for d in DILATIONS:
            lb = t // d
            if lb == WIN:
                def sequence(r, carry, d=d):
                    rows = pl.ds(r, WIN, stride=d)
                    grad_tile(rows, rows, WIN, bias_seq[...])
                    return carry

                lax.fori_loop(0, d, sequence, 0, unroll=4)
                continue
            nt = lb // TQ

            def tile(ti, carry, d=d, lb=lb, nt=nt):
                qrows, krows, off = _tile_rows(ti, nt, lb, d)
                grad_tile(qrows, krows, TQ, _band_bias(bias, off))
                return carry

            lax.fori_loop(0, d * nt, tile, 0, unroll=8)

        def finish(rows):
            dqkv_ref[0, rows, :] = _rope_transpose(dq_acc[rows, :], cos_ref[rows, :], sin_ref[rows, :]).astype(BF16)
            dqkv_ref[1, rows, :] = _rope_transpose(dk_acc[rows, :], cos_ref[rows, :], sin_ref[rows, :]).astype(BF16)
            dqkv_ref[2, rows, :] = dv_acc[rows, :].astype(BF16)

        _for_row_chunks(t, finish)

    cols = lambda base: pl.BlockSpec((t, LANES), lambda g: (0, base + g))
    out, got = _call(
        body, [proj, proj, proj, cos, sin, d_attn, lse, delta], name="attn_bwd", grid=(npair,),
        in_specs=[cols(0), cols(npair), cols(2 * npair), WHOLE_VMEM, WHOLE_VMEM, cols(0), cols(0), cols(0)],
        out_specs=[pl.BlockSpec((3, t, LANES), lambda g: (0, 0, g))],
        out_shape=[jax.ShapeDtypeStruct((3, t, ATTN_W), BF16)],
        scratch_shapes=[pltpu.VMEM((t, LANES), F32)] * 3 + [pltpu.VMEM((len(WINDOW_OFFSETS), 2 * TQ, WIN), F32),
                                                            pltpu.VMEM((2 * WIN, WIN), F32)],
        comm=comm)
    return out[0], got


def _norm_bwd(dres, pre, post, *, tm, comm=None):
    t, w = dres.shape
    row = pl.BlockSpec((tm, w), lambda i: (i, 0))
    gsum = _const_spec((1, w), (0, 0))
    ins, in_specs, out_shape, out_specs = [dres], [row], [], []
    if pre is not None:
        dh, x, g3, l = pre
        ins += [dh, x, g3]
        in_specs += [row, row, _gain_spec(g3, l)]
        out_shape += [jax.ShapeDtypeStruct((t, w), F32), jax.ShapeDtypeStruct((1, w), F32)]
        out_specs += [row, gsum]
    if post is not None:
        y, g3, l = post
        ins += [y, g3]
        in_specs += [row, _gain_spec(g3, l)]
        out_shape += [jax.ShapeDtypeStruct((t, w), BF16), jax.ShapeDtypeStruct((1, w), F32)]
        out_specs += [row, gsum]
    n_in = len(ins)

    def body(*refs):
        first = pl.program_id(0) == 0
        ins_r, outs_r = list(refs[:n_in]), list(refs[n_in:])
        d = ins_r.pop(0)[...]
        if pre is not None:
            dh_ref, x_ref, g_ref = ins_r[:3]
            ins_r = ins_r[3:]
            dx, dg = _rms_bwd(x_ref[...], g_ref[...], dh_ref[...])
            d = d + dx
            outs_r.pop(0)[...] = d
            _accumulate(outs_r.pop(0), dg, first)
        if post is not None:
            y_ref, g_ref = ins_r
            dy, dg = _rms_bwd(y_ref[...], g_ref[...], d)
            outs_r.pop(0)[...] = dy.astype(BF16)
            _accumulate(outs_r.pop(0), dg, first)

    out, got = _call(body, ins, name="norm_bwd", grid=(t // tm,), in_specs=in_specs, out_specs=out_specs,
                     out_shape=out_shape, comm=comm)
    d_new, dg_pre = (out.pop(0), out.pop(0)) if pre is not None else (None, None)
    dy, dg_post = (out.pop(0), out.pop(0)) if post is not None else (None, None)
    return d_new, dy, dg_pre, dg_post, got


def _merge_bwd(dmix, w, attn, conv_y, ga3, gc3, l, *, tm, comm=None):
    t = attn.shape[0]
    _, ko, k = w.shape
    row = pl.BlockSpec((tm, ATTN_W), lambda i: (i, 0))
    gsum = _const_spec((1, ATTN_W), (0, 0))

    def body(dmix_ref, w_ref, a_ref, c_ref, ga_ref, gc_ref, da_ref, dl_ref, dc_ref, dga_ref, dgc_ref):
        first = pl.program_id(0) == 0
        d_merged = lax.dot_general(dmix_ref[...], w_ref[...], LANE_CONTRACT, preferred_element_type=F32)
        attn_t = a_ref[...]
        da, dga = _rms_bwd(attn_t, ga_ref[...], d_merged[:, :ATTN_W])
        dc, dgc = _rms_bwd(c_ref[...], gc_ref[...], d_merged[:, ATTN_W:])
        da_ref[...] = da
        dc_ref[...] = dc
        same_head = (lax.broadcasted_iota(jnp.int32, (ATTN_W, ATTN_W), 0) // HEAD_DIM
                     == lax.broadcasted_iota(jnp.int32, (ATTN_W, ATTN_W), 1) // HEAD_DIM).astype(BF16)
        rest = da * attn_t
        total = jnp.zeros(rest.shape, F32)
        for _ in range(3):
            term = rest.astype(BF16)
            total = total + jnp.dot(term, same_head, preferred_element_type=F32)
            rest = rest - term.astype(F32)
        dl_ref[...] = total
        _accumulate(dga_ref, dga, first)
        _accumulate(dgc_ref, dgc, first)

    out, got = _call(
        body, [dmix, w, attn, conv_y, ga3, gc3], name="merge_bwd", grid=(t // tm,),
        in_specs=[pl.BlockSpec((tm, k), lambda i: (i, 0)), _const_spec((None, ko, k), (0, 0, 0)),
                  row, row, _gain_spec(ga3, l), _gain_spec(gc3, l)],
        out_specs=[row, row, row, gsum, gsum],
        out_shape=[jax.ShapeDtypeStruct((t, ATTN_W), F32)] * 3 + [jax.ShapeDtypeStruct((1, ATTN_W), F32)] * 2,
        comm=comm)
    return (*out, got)


def _conv_bwd(proj, conv_w, l, d_conv_y):
    t = proj.shape[0]
    col0 = 3 * ATTN_W // LANES
    nj = CONV_W // LANES

    def body(u_ref, gb_ref, gc_ref, w_ref, dy_ref, d3_ref, dw_ref):
        u, gc, dy = u_ref[...], gc_ref[...], dy_ref[...]
        row = lax.broadcasted_iota(jnp.int32, u.shape, 0)
        down = lambda a: jnp.where(row == 0, 0.0, pltpu.roll(a, 1, 0))
        up = lambda a: jnp.where(row == t - 1, 0.0, pltpu.roll(a, t - 1, 0))
        w = w_ref[...]
        c = gc * u
        c_prev, c_next = down(c), up(c)
        d3_ref[1] = (dy * (w[0:1] * c_prev + w[1:2] * c + w[2:3] * c_next)).astype(BF16)
        dz = dy * gb_ref[...]
        dc = w[0:1] * up(dz) + w[1:2] * dz + w[2:3] * down(dz)
        d3_ref[0] = (dc * gc).astype(BF16)
        d3_ref[2] = (dc * u).astype(BF16)
        dw_ref[0:1, :] = jnp.sum(dz * c_prev, axis=0, keepdims=True)
        dw_ref[1:2, :] = jnp.sum(dz * c, axis=0, keepdims=True)
        dw_ref[2:3, :] = jnp.sum(dz * c_next, axis=0, keepdims=True)

    cols = lambda base: pl.BlockSpec((t, LANES), lambda j: (0, base + j))
    return pl.pallas_call(
        body, grid=(nj,),
        in_specs=[cols(col0), cols(col0 + nj), cols(col0 + 2 * nj),
                  pl.BlockSpec((None, None, 3, LANES), lambda j: (l, j, 0, 0)), cols(0)],
        out_specs=[pl.BlockSpec((3, t, LANES), lambda j: (0, 0, j)), pl.BlockSpec((None, 3, LANES), lambda j: (j, 0, 0))],
        out_shape=[jax.ShapeDtypeStruct((3, t, CONV_W), BF16), jax.ShapeDtypeStruct((nj, 3, LANES), F32)],
        name="conv_bwd")(proj, proj, proj, conv_w, d_conv_y)


def _own_shard_slab(w, l, place, dtype):
    _, rows, cols = w.shape
    tr = rows if rows <= 704 else 512
    assert rows % tr == 0

    def body(p_ref, w_ref, o_ref):
        del p_ref
        o_ref[...] = w_ref[...].astype(dtype)

    grid_spec = pltpu.PrefetchScalarGridSpec(
        num_scalar_prefetch=1, grid=(rows // tr,),
        in_specs=[pl.BlockSpec((None, tr, cols), lambda i, p: (l, i, 0))],
        out_specs=pl.BlockSpec((None, tr, cols), lambda i, p: (p[0], i, 0)))
    return pl.pallas_call(body, grid_spec=grid_spec, name="own_shard_slab",
                          out_shape=jax.ShapeDtypeStruct((N_CHIPS, rows, cols), dtype))(place, w)


def _own_conv_slab(w, place):
    depth = w.shape[0]

    def body(p_ref, w_ref, o_ref):
        del p_ref
        o_ref[...] = w_ref[...]

    grid_spec = pltpu.PrefetchScalarGridSpec(
        num_scalar_prefetch=1, grid=(depth,),
        in_specs=[pl.BlockSpec((None, 3, LANES), lambda l, p: (l, 0, 0))],
        out_specs=pl.BlockSpec((None, None, 3, LANES), lambda l, p: (l, p[0], 0, 0)))
    return pl.pallas_call(body, grid_spec=grid_spec, name="own_conv_slab",
                          out_shape=jax.ShapeDtypeStruct((depth, N_CHIPS, 3, LANES), F32))(place, w)


def _add_halves(grad, got, place):
    s_n, rows, cols = grad.shape
    hr = rows // 2

    def body(p_ref, g_ref, r_ref, o_ref):
        del p_ref
        o_ref[...] = (g_ref[...] + r_ref[...]).astype(BF16)

    grid_spec = pltpu.PrefetchScalarGridSpec(
        num_scalar_prefetch=1, grid=(s_n,),
        in_specs=[pl.BlockSpec((None, hr, cols), lambda s, p: (s, p[1], 0)),
                  pl.BlockSpec((None, hr, cols), lambda s, p: (s, 0, 0))],
        out_specs=pl.BlockSpec((None, hr, cols), lambda s, p: (s, 0, 0)))
    return pl.pallas_call(body, grid_spec=grid_spec, out_shape=jax.ShapeDtypeStruct((s_n, hr, cols), BF16),
                          name="add_halves")(place, grad, got)


def _sum_partials(partial, got, place, acc, l):
    _, hr, cols = partial.shape

    def body(p_ref, mine_ref, got_ref, acc_ref, o_ref):
        del p_ref, acc_ref
        total = mine_ref[...].astype(F32)
        for k in range(3):
            total = total + got_ref[k].astype(F32)
        o_ref[...] = total

    grid_spec = pltpu.PrefetchScalarGridSpec(
        num_scalar_prefetch=1, grid=(1,),
        in_specs=[pl.BlockSpec((None, hr, cols), lambda i, p: (p[0], 0, 0)),
                  pl.BlockSpec((3, hr, cols), lambda i, p: (0, 0, 0)), ANY],
        out_specs=pl.BlockSpec((None, hr, cols), lambda i, p: (l, p[1], 0)))
    return pl.pallas_call(body, grid_spec=grid_spec, out_shape=jax.ShapeDtypeStruct(acc.shape, F32),
                          input_output_aliases={3: 0}, name="sum_partials")(place, partial, got, acc)


def _allreduce_small(vec, loss_row):
    rows = vec.shape[0]

    def body(v_ref, o_ref, slots, send_sems, recv_sems):
        x, y, c, _ = _place()
        me = 4 * x + 2 * y + c
        slots[me] = v_ref[...]
        copies = []
        for k in range(1, N_DEV):
            flip = lambda v, bit: 1 - v if bit else v
            peer = (flip(x, k & 4), flip(y, k & 2), flip(c, k & 1))
            copies.append(_remote(v_ref, slots.at[me], send_sems.at[k - 1], recv_sems.at[k - 1], peer))
        for cp in copies:
            cp.start()
        for k in range(1, N_DEV):
            flip = lambda v, bit: 1 - v if bit else v
            peer_id = 4 * flip(x, k & 4) + 2 * flip(y, k & 2) + flip(c, k & 1)
            _remote(v_ref, slots.at[peer_id], send_sems.at[k - 1], recv_sems.at[k - 1], (x, y, c)).wait_recv()
        for cp in copies:
            cp.wait_send()
        total = slots[0]
        for dev in range(1, N_DEV):
            total = total + slots[dev]
        o_ref[...] = total
        o_ref[loss_row:loss_row + 1, :] = jnp.broadcast_to(
            jnp.sum(total[loss_row:loss_row + 1, :], axis=-1, keepdims=True), (1, LANES))

    return pl.pallas_call(
        body, in_specs=[WHOLE_VMEM], out_specs=WHOLE_VMEM, out_shape=jax.ShapeDtypeStruct((rows, LANES), F32),
        scratch_shapes=[pltpu.VMEM((N_DEV, rows, LANES), F32), pltpu.SemaphoreType.DMA((N_DEV - 1,)),
                        pltpu.SemaphoreType.DMA((N_DEV - 1,))],
        name="allreduce_small")(vec)


def _adamw(w, g, m, v, *, tr, emit_grad=False):
    depth, rows, cols = w.shape
    assert rows % tr == 0
    c1 = float(np.float32(1.0 - ADAM_B1 ** ADAM_STEP))
    c2 = float(np.float32(1.0 - ADAM_B2 ** ADAM_STEP))

    def body(w_ref, g_ref, m_ref, v_ref, d_ref, mo_ref, vo_ref, *go_ref):
        g_t = g_ref[...]
        if emit_grad:
            go_ref[0][...] = g_t
        m_new = ADAM_B1 * m_ref[...] + (1.0 - ADAM_B1) * g_t
        v_new = ADAM_B2 * v_ref[...] + (1.0 - ADAM_B2) * (g_t * g_t)
        mo_ref[...] = m_new
        vo_ref[...] = v_new
        d_ref[...] = -ADAM_LR * ((m_new / c1) / (jnp.sqrt(v_new / c2) + ADAM_EPS) + ADAM_WD * w_ref[...])

    blk = pl.BlockSpec((None, tr, cols), lambda l, i: (l, i, 0))
    return pl.pallas_call(
        body, grid=(depth, rows // tr), in_specs=[blk] * 4, out_specs=[blk] * (4 if emit_grad else 3),
        out_shape=[jax.ShapeDtypeStruct(w.shape, F32)] * (4 if emit_grad else 3), name="adamw")(w, g, m, v)


def _local_step(x, positions, target, gains, exchange):
    t = x.shape[0]
    tm = 512
    inv_freq = ROPE_THETA ** (-jnp.arange(0, ROPE_DIM, 2, dtype=F32) / ROPE_DIM)
    lane = np.arange(LANES) % HEAD_DIM
    freq_row = jnp.where(lane < ROPE_DIM, inv_freq[lane % (ROPE_DIM // 2)], 0.0).astype(F32)[None, :]
    cos, sin = _rope_tables(positions.reshape(t, 1), freq_row)

    def hosted(tag, fn, *args, **kwargs):
        *out, got = fn(*args, comm=exchange.host(tag), **kwargs)
        if got is not None:
            exchange.hosted(tag, got)
        return out[0] if len(out) == 1 else out

    saved = []
    h1 = _norm_fwd(x, gains["pre_mix_norm"], 0, tm=tm)
    for l in range(DEPTH):
        proj = hosted(("fwd", l, "in_proj"), _in_proj, h1, exchange.weight("w_in", l), cos, sin, tm=tm)
        attn, lse = hosted(("fwd", l, "attn"), _attn_fwd, proj)
        conv_y = _conv_fwd(proj, exchange.weight("conv_w", l), l)
        merged, mix, x1, h2 = hosted(
            ("fwd", l, "out_proj"), _merge_out_proj, attn, conv_y, gains["attn_out_norm"], gains["conv_out_norm"],
            exchange.weight("w_out", l), x, gains["post_mix_norm"], gains["pre_ffn_norm"], l, tm=tm)
        g, u, act = hosted(("fwd", l, "gate_up"), _gate_up_swiglu, h2, exchange.weight("w_gate_up", l), tm=1024)
        nxt = (gains["pre_mix_norm"], l + 1) if l + 1 < DEPTH else (None, None)
        f, x2, h1_next = hosted(("fwd", l, "down"), _mm_resnorm, act, exchange.weight("w_down", l), x1,
                                gains["post_ffn_norm"], l, *nxt, tm=tm, name="down")
        saved.append(dict(x=x, h1=h1, proj=proj, attn=attn, lse=lse, conv_y=conv_y, merged=merged, mix=mix,
                          x1=x1, h2=h2, g=g, u=u, act=act, f=f))
        x, h1 = x2, h1_next

    dres, loss_lanes = _loss_fwd_bwd(x, target, tm=tm)

    g_gain = {k: [None] * DEPTH for k in gains}
    g_conv = [None] * DEPTH
    _, df, _, g_gain["post_ffn_norm"][DEPTH - 1], _ = _norm_bwd(
        dres, None, (saved[-1]["f"], gains["post_ffn_norm"], DEPTH - 1), tm=tm)
    for l in reversed(range(DEPTH)):
        sv = saved[l]
        w = {k: exchange.weight(k, l) for k in MATRIX_NAMES + ("conv_w",)}
        dg, du = _down_dx_swiglu_bwd(df, w["w_down"], sv["g"], sv["u"], tm=1024, tko=FFN // 2)
        g_down, _ = _mm_tn(sv["act"], df, 1, tka=256, name="down_dw")
        exchange.grads(l, "down", dict(w_down=g_down.reshape(N_CHIPS, FFN // N_CHIPS, D_MODEL)))
        dx1, dmix, g_gain["pre_ffn_norm"][l], g_gain["post_mix_norm"][l] = hosted(
            ("bwd", l, "gate_up_dx"), _gate_up_dx_norms, dg, du, w["w_gate_up"], dres,
            (sv["x1"], gains["pre_ffn_norm"], l), (sv["mix"], gains["post_mix_norm"], l), tm=tm)
        g_gate_up = hosted(("bwd", l, "gate_up_dw"), _mm_tn, sv["h2"], dg, N_CHIPS // 2, tka=512, name="gate_up_dw",
                           into=lax.empty(w["w_gate_up"].shape, F32))
        g_gate_up, _ = _mm_tn(sv["h2"], du, N_CHIPS // 2, tka=512, name="gate_up_dw", into=g_gate_up,
                              shard0=N_CHIPS // 2)
        exchange.grads(l, "gate_up", dict(w_gate_up=g_gate_up))
        g_out, _ = _mm_tn(sv["merged"], dmix, 1, tka=512, name="out_proj_dw")
        d_attn, delta, d_conv_y, g_gain["attn_out_norm"][l], g_gain["conv_out_norm"][l] = hosted(
            ("bwd", l, "merge"), _merge_bwd, dmix, w["w_out"],
            sv["attn"], sv["conv_y"], gains["attn_out_norm"], gains["conv_out_norm"], l, tm=tm)
        d_attn3 = hosted(("bwd", l, "attn"), _attn_bwd, sv["proj"], cos, sin, d_attn, sv["lse"], delta)
        d_conv3, g_conv[l] = _conv_bwd(sv["proj"], w["conv_w"], l, d_conv_y)
        g_in = _in_proj_dw(sv["h1"], d_attn3, d_conv3, N_CHIPS, tka=512)
        exchange.grads(l, "mix", dict(w_out=g_out.reshape(N_CHIPS, D_MODEL // N_CHIPS, D_MODEL), w_in=g_in))
        if l > 0:
            dres, df, g_gain["pre_mix_norm"][l], g_gain["post_ffn_norm"][l - 1] = hosted(
                ("bwd", l, "in_proj_dx"), _in_proj_dx_norms, d_attn3, d_conv3, w["w_in"], dx1,
                (sv["x"], gains["pre_mix_norm"], l), (saved[l - 1]["f"], gains["post_ffn_norm"], l - 1), tm=tm)
        else:
            dh1 = hosted(("bwd", l, "in_proj_dx"), _in_proj_dx, d_attn3, d_conv3, w["w_in"], tm=1024)
            dres, _, g_gain["pre_mix_norm"][l], _ = hosted(
                ("bwd", l, "norm_low"), _norm_bwd, dx1, (dh1, sv["x"], gains["pre_mix_norm"], l), None, tm=tm)

    g_gain = {k: jnp.concatenate(v, axis=0) for k, v in g_gain.items()}
    return loss_lanes, dres, g_gain, jnp.stack(g_conv, axis=0)


class _Exchange:
    GATHER_HOSTS = {"in_proj": (("w_out", 0),), "attn": (("w_gate_up", 0),), "gate_up": (("w_down", 0),),
                    "down": (("w_in", 1),)}

    @staticmethod
    def _reduce_hosts(group, l):
        if group == "down":
            return "gate_up_dw", "attn", l
        if group == "gate_up":
            return "merge", "attn", l
        if l > 0:
            return "in_proj_dx", "gate_up_dx", l - 1
        return "in_proj_dx", "norm_low", l

    def __init__(self, params, place):
        self.place = place
        self.slabs = {k: [_own_shard_slab(params[k], l, place, BF16) for l in range(DEPTH)] for k in MATRIX_NAMES}
        self.gathered = {k: [None] * DEPTH for k in MATRIX_NAMES}
        self.gathered["w_in"][0], self.conv_w = _run_comm(
            _gather_comm([self.slabs["w_in"][0]], _own_conv_slab(params["conv_w"], place)), "gather_first")
        self.full = {k: lax.empty(params[k].shape, F32) for k in MATRIX_NAMES}
        self.pending = {}
        self.raw = {}

    def weight(self, name, l):
        if name == "conv_w":
            return self.conv_w
        g = self.gathered[name][l]
        return g.reshape(1, g.shape[0] * g.shape[1], g.shape[2]) if name in ("w_out", "w_down") else g

    def host(self, tag):
        phase, l, kernel = tag
        if phase == "fwd":
            carried = [(name, l + ahead) for name, ahead in self.GATHER_HOSTS.get(kernel, ()) if l + ahead < DEPTH]
            return _gather_comm([self.slabs[name][layer] for name, layer in carried]) if carried else None
        if tag in self.pending:
            entries = self.pending[tag]
            arrays = [a for entry in entries for a in entry[3]]
            stages = {entry[0] for entry in entries}
            assert len(stages) == 1
            return _halves_comm(arrays) if stages == {"halves"} else _partials_comm(arrays)
        return None

    def hosted(self, tag, results):
        phase, l, kernel = tag
        if phase == "fwd":
            carried = [(name, l + ahead) for name, ahead in self.GATHER_HOSTS[kernel] if l + ahead < DEPTH]
            for (name, layer), slab in zip(carried, results):
                self.gathered[name][layer] = slab
            return
        results = list(results)
        for stage, gl, group, arrays in self.pending.pop(tag):
            mine, results = results[:len(arrays)], results[len(arrays):]
            names = list(self.raw[(gl, group)])
            if stage == "partials":
                self._finish_reduction(gl, names, arrays, mine)
                continue
            partials = [_add_halves(self.raw[(gl, group)][k], r, self.place) for k, r in zip(names, mine)]
            _, ici_kernel, ici_layer = self._reduce_hosts(group, gl)
            self.pending.setdefault(("bwd", ici_layer, ici_kernel), []).append(("partials", gl, group, partials))

    def grads(self, l, group, grads):
        self.raw[(l, group)] = grads
        self.pending.setdefault(("bwd", l, self._reduce_hosts(group, l)[0]), []).append(
            ("halves", l, group, [grads[k] for k in grads]))

    def _finish_reduction(self, l, names, partials, others):
        for k, p, q in zip(names, partials, others):
            self.full[k] = _sum_partials(p, q, self.place, self.full[k], l)

    def reduced(self):
        assert not self.pending
        return dict(zip(MATRIX_NAMES, _run_comm(_share_comm([self.full[k] for k in MATRIX_NAMES]), "share_halves")))


def kernel(x, positions, pre_mix_norm, w_in, conv_w, attn_out_norm, conv_out_norm, w_out, post_mix_norm, pre_ffn_norm, w_gate_up, w_down, post_ffn_norm, loss_target, m_pre_mix_norm, m_w_in, m_conv_w, m_attn_out_norm, m_conv_out_norm, m_w_out, m_post_mix_norm, m_pre_ffn_norm, m_w_gate_up, m_w_down, m_post_ffn_norm, v_pre_mix_norm, v_w_in, v_conv_w, v_attn_out_norm, v_conv_out_norm, v_w_out, v_post_mix_norm, v_pre_ffn_norm, v_w_gate_up, v_w_down, v_post_ffn_norm):
    params = dict(pre_mix_norm=pre_mix_norm, w_in=w_in, conv_w=conv_w, attn_out_norm=attn_out_norm,
                  conv_out_norm=conv_out_norm, w_out=w_out, post_mix_norm=post_mix_norm, pre_ffn_norm=pre_ffn_norm,
                  w_gate_up=w_gate_up, w_down=w_down, post_ffn_norm=post_ffn_norm)
    mom1 = dict(pre_mix_norm=m_pre_mix_norm, w_in=m_w_in, conv_w=m_conv_w, attn_out_norm=m_attn_out_norm,
                conv_out_norm=m_conv_out_norm, w_out=m_w_out, post_mix_norm=m_post_mix_norm,
                pre_ffn_norm=m_pre_ffn_norm, w_gate_up=m_w_gate_up, w_down=m_w_down, post_ffn_norm=m_post_ffn_norm)
    mom2 = dict(pre_mix_norm=v_pre_mix_norm, w_in=v_w_in, conv_w=v_conv_w, attn_out_norm=v_attn_out_norm,
                conv_out_norm=v_conv_out_norm, w_out=v_w_out, post_mix_norm=v_post_mix_norm,
                pre_ffn_norm=v_pre_ffn_norm, w_gate_up=v_w_gate_up, w_down=v_w_down, post_ffn_norm=v_post_ffn_norm)
    xi, yi, ci = lax.axis_index("x"), lax.axis_index("y"), lax.axis_index("c")
    place = jnp.stack([2 * xi + yi, ci]).astype(jnp.int32)

    exchange = _Exchange(params, place)
    gains = {k: params[k][:, None, :] for k in GAIN_NAMES}
    loss_lanes, grad_x, g_gain, g_conv = _local_step(x[0], positions[0], loss_target[0], gains, exchange)
    grad = exchange.reduced()

    small = [g_gain[k].reshape(-1) for k in GAIN_NAMES] + [g_conv.reshape(-1), loss_lanes.reshape(-1)]
    sizes = [int(s.shape[0]) for s in small]
    flat = jnp.concatenate(small)
    loss_row = (sum(sizes) - LANES) // LANES
    rows = -(-flat.shape[0] // (8 * LANES)) * 8
    flat = jnp.pad(flat, (0, rows * LANES - flat.shape[0])).reshape(rows, LANES)
    total = _allreduce_small(flat, loss_row).reshape(-1)
    offsets = np.cumsum([0] + sizes)
    for i, k in enumerate(GAIN_NAMES):
        grad[k] = total[offsets[i]:offsets[i + 1]].reshape(params[k].shape)
    conv_all = total[offsets[6]:offsets[7]].reshape(DEPTH, N_CHIPS, 3, LANES)
    grad["conv_w"] = lax.dynamic_index_in_dim(conv_all, 2 * xi + yi, axis=1, keepdims=False)
    loss = total[offsets[7]]

    delta, new_m, new_v = {}, {}, {}
    for k in WEIGHT_ORDER:
        shape = params[k].shape
        if k in MATRIX_NAMES:
            tr = {1024: 512, 704: 352, 256: 256}[shape[1]]
            delta[k], new_m[k], new_v[k], grad[k] = _adamw(params[k], grad[k], mom1[k], mom2[k], tr=tr, emit_grad=True)
        else:
            as3 = (lambda a: a) if len(shape) == 3 else (lambda a: a[None])
            d, m, v = _adamw(as3(params[k]), as3(grad[k]), as3(mom1[k]), as3(mom2[k]), tr=as3(params[k]).shape[1])
            delta[k], new_m[k], new_v[k] = d.reshape(shape), m.reshape(shape), v.reshape(shape)

    return (loss, grad_x[None], *[grad[k] for k in WEIGHT_ORDER], *[delta[k] for k in WEIGHT_ORDER],
            *[new_m[k] for k in WEIGHT_ORDER], *[new_v[k] for k in WEIGHT_ORDER])
```
